```python
import math
import jax, jax.numpy as jnp
from jax import lax
import numpy as np

D_MODEL = 1024
BATCH = 32
SEQ = 2048
DEPTH = 1

MIX_WIDTH = D_MODEL
POOL_WIDTH = MIX_WIDTH // 2
POOL_WINDOWS = (2, 4, 8, 16)
POOL_GROUPS = len(POOL_WINDOWS)
POOL_CH = POOL_WIDTH // POOL_GROUPS
SGU_WIDTH = MIX_WIDTH - POOL_WIDTH
SGU_HEADS = 4
SGU_HD = SGU_WIDTH // SGU_HEADS
CHUNK = 128
IN_COLS = POOL_WIDTH + 2 * SGU_WIDTH
D_FF = int(math.ceil((8 * D_MODEL / 3) / 256) * 256)
LN_EPS = 1e-5
DEEPNORM_ALPHA = float((2.0 * DEPTH) ** 0.25)
DEEPNORM_BETA = float((8.0 * DEPTH) ** -0.25)

kernel_name = "hybrid_pool_sgu_deepnorm_layer"


def layer_norm(x, g, b):
    xf = x.astype(jnp.float32)
    mu = jnp.mean(xf, axis=-1, keepdims=True)
    var = jnp.mean(jnp.square(xf - mu), axis=-1, keepdims=True)
    out = (xf - mu) * lax.rsqrt(var + LN_EPS)
    return (out * g.astype(jnp.float32) + b.astype(jnp.float32)).astype(x.dtype)


def causal_multiscale_pool(xp):
    S = xp.shape[1]
    xf = xp.astype(jnp.float32)
    cs0 = jnp.pad(jnp.cumsum(xf, axis=1), ((0, 0), (1, 0), (0, 0)))
    pos = jnp.arange(1, S + 1, dtype=jnp.int32)
    outs = []
    for g, w in enumerate(POOL_WINDOWS):
        sl = slice(g * POOL_CH, (g + 1) * POOL_CH)
        c = cs0[..., sl]
        lower = jnp.pad(c, ((0, 0), (w - 1, 0), (0, 0)))[:, :S]
        cnt = jnp.minimum(pos, w).astype(jnp.float32)[None, :, None]
        outs.append((c[:, 1:] - lower) / cnt - xf[..., sl])
    return jnp.stack(outs, axis=2)


def spatial_gating(z, ln_g, ln_b, w_s, b_s):
    B, S, _ = z.shape
    u, v = z[..., :SGU_WIDTH], z[..., SGU_WIDTH:]
    v = layer_norm(v, ln_g, ln_b)
    v = v.reshape(B, S // CHUNK, CHUNK, SGU_HEADS, SGU_HD)
    mask = jnp.tril(jnp.ones((CHUNK, CHUNK), dtype=w_s.dtype))
    ws = w_s * mask[None]
    mixed = jnp.einsum('hts,bnshd->bnthd', ws, v)
    mixed = mixed + jnp.transpose(b_s)[None, None, :, :, None]
    return u * mixed.reshape(B, S, SGU_WIDTH)


def swiglu_ffn(h, w_gate_up, w_down):
    gu = jnp.einsum('bsd,df->bsf', h, w_gate_up)
    gate, up = gu[..., :D_FF], gu[..., D_FF:]
    return jnp.einsum('bsf,fd->bsd', jax.nn.silu(gate) * up, w_down)


def _fwd_setup_inputs(seed: int = 0) -> dict:
    key = jax.random.key(seed)
    ks = jax.random.split(key, 16)
    f32 = jnp.float32
    def nrm(k, shape, scale):
        return jax.random.normal(k, shape, f32) * scale
    return {
        "x": jax.random.normal(ks[0], (BATCH, SEQ, D_MODEL), f32),
        "w_in": nrm(ks[1], (DEPTH, D_MODEL, IN_COLS), D_MODEL ** -0.5),
        "pool_w": nrm(ks[2], (DEPTH, POOL_GROUPS, POOL_CH, POOL_CH), POOL_CH ** -0.5),
        "pool_scale": 1.0 + nrm(ks[3], (DEPTH, POOL_WIDTH), 0.1),
        "sgu_ln_g": 1.0 + nrm(ks[4], (DEPTH, SGU_WIDTH), 0.01),
        "sgu_ln_b": nrm(ks[5], (DEPTH, SGU_WIDTH), 0.01),
        "sgu_w": nrm(ks[6], (DEPTH, SGU_HEADS, CHUNK, CHUNK), CHUNK ** -0.5),
        "sgu_b": 1.0 + nrm(ks[7], (DEPTH, SGU_HEADS, CHUNK), 0.01),
        "w_out": nrm(ks[8], (DEPTH, MIX_WIDTH, D_MODEL), MIX_WIDTH ** -0.5 * DEEPNORM_BETA),
        "ln1_g": 1.0 + nrm(ks[9], (DEPTH, D_MODEL), 0.01),
        "ln1_b": nrm(ks[10], (DEPTH, D_MODEL), 0.01),
        "w_gate_up": nrm(ks[11], (DEPTH, D_MODEL, 2 * D_FF), D_MODEL ** -0.5),
        "w_down": nrm(ks[12], (DEPTH, D_FF, D_MODEL), D_FF ** -0.5 * DEEPNORM_BETA),
        "ln2_g": 1.0 + nrm(ks[13], (DEPTH, D_MODEL), 0.01),
        "ln2_b": nrm(ks[14], (DEPTH, D_MODEL), 0.01),
    }


def _fwd_reference(x, w_in, pool_w, pool_scale, sgu_ln_g, sgu_ln_b, sgu_w, sgu_b,
              w_out, ln1_g, ln1_b, w_gate_up, w_down, ln2_g, ln2_b):
    B, S, _ = x.shape
    alpha = jnp.asarray(DEEPNORM_ALPHA, dtype=x.dtype)
    for l in range(DEPTH):
        proj = jnp.einsum('bsd,dc->bsc', x, w_in[l])
        xp = proj[..., :POOL_WIDTH]
        zg = jax.nn.gelu(proj[..., POOL_WIDTH:], approximate=False)
        pooled = causal_multiscale_pool(xp)
        pool_out = jnp.einsum('bsgc,gcd->bsgd', pooled, pool_w[l].astype(jnp.float32))
        pool_out = (pool_out.reshape(B, S, POOL_WIDTH) * pool_scale[l]).astype(x.dtype)
        sgu_out = spatial_gating(zg, sgu_ln_g[l], sgu_ln_b[l], sgu_w[l], sgu_b[l])
        mix = jnp.concatenate([pool_out, sgu_out], axis=-1)
        mix = jnp.einsum('bsc,cd->bsd', mix, w_out[l])
        h = layer_norm(alpha * x + mix, ln1_g[l], ln1_b[l])
        x = layer_norm(alpha * h + swiglu_ffn(h, w_gate_up[l], w_down[l]), ln2_g[l], ln2_b[l])
    return x


import jax as _jax
import jax.numpy as _jnp

TWIN_FORMAT = 'train_step'
FWD_PARAMS = ['x', 'w_in', 'pool_w', 'pool_scale', 'sgu_ln_g', 'sgu_ln_b', 'sgu_w', 'sgu_b', 'w_out', 'ln1_g', 'ln1_b', 'w_gate_up', 'w_down', 'ln2_g', 'ln2_b']
TWIN_WEIGHTS = ['w_in', 'pool_w', 'pool_scale', 'sgu_ln_g', 'sgu_ln_b', 'sgu_w', 'sgu_b', 'w_out', 'ln1_g', 'ln1_b', 'w_gate_up', 'w_down', 'ln2_g', 'ln2_b']
TWIN_DIFF_INPUT = 'x'
TWIN_INPUTS = ['x', 'w_in', 'pool_w', 'pool_scale', 'sgu_ln_g', 'sgu_ln_b', 'sgu_w', 'sgu_b', 'w_out', 'ln1_g', 'ln1_b', 'w_gate_up', 'w_down', 'ln2_g', 'ln2_b', 'loss_target', 'm_w_in', 'm_pool_w', 'm_pool_scale', 'm_sgu_ln_g', 'm_sgu_ln_b', 'm_sgu_w', 'm_sgu_b', 'm_w_out', 'm_ln1_g', 'm_ln1_b', 'm_w_gate_up', 'm_w_down', 'm_ln2_g', 'm_ln2_b', 'v_w_in', 'v_pool_w', 'v_pool_scale', 'v_sgu_ln_g', 'v_sgu_ln_b', 'v_sgu_w', 'v_sgu_b', 'v_w_out', 'v_ln1_g', 'v_ln1_b', 'v_w_gate_up', 'v_w_down', 'v_ln2_g', 'v_ln2_b']
TWIN_OUTPUTS = ['loss', 'grad_x', 'grad_w_in', 'grad_pool_w', 'grad_pool_scale', 'grad_sgu_ln_g', 'grad_sgu_ln_b', 'grad_sgu_w', 'grad_sgu_b', 'grad_w_out', 'grad_ln1_g', 'grad_ln1_b', 'grad_w_gate_up', 'grad_w_down', 'grad_ln2_g', 'grad_ln2_b', 'delta_w_in', 'delta_pool_w', 'delta_pool_scale', 'delta_sgu_ln_g', 'delta_sgu_ln_b', 'delta_sgu_w', 'delta_sgu_b', 'delta_w_out', 'delta_ln1_g', 'delta_ln1_b', 'delta_w_gate_up', 'delta_w_down', 'delta_ln2_g', 'delta_ln2_b', 'new_m_w_in', 'new_m_pool_w', 'new_m_pool_scale', 'new_m_sgu_ln_g', 'new_m_sgu_ln_b', 'new_m_sgu_w', 'new_m_sgu_b', 'new_m_w_out', 'new_m_ln1_g', 'new_m_ln1_b', 'new_m_w_gate_up', 'new_m_w_down', 'new_m_ln2_g', 'new_m_ln2_b', 'new_v_w_in', 'new_v_pool_w', 'new_v_pool_scale', 'new_v_sgu_ln_g', 'new_v_sgu_ln_b', 'new_v_sgu_w', 'new_v_sgu_b', 'new_v_w_out', 'new_v_ln1_g', 'new_v_ln1_b', 'new_v_w_gate_up', 'new_v_w_down', 'new_v_ln2_g', 'new_v_ln2_b']
TWIN_LEAF_KINDS = {'loss': 'loss', 'grad_x': 'grad_x', 'grad_w_in': 'grad_w', 'grad_pool_w': 'grad_w', 'grad_pool_scale': 'grad_w', 'grad_sgu_ln_g': 'grad_w', 'grad_sgu_ln_b': 'grad_w', 'grad_sgu_w': 'grad_w', 'grad_sgu_b': 'grad_w', 'grad_w_out': 'grad_w', 'grad_ln1_g': 'grad_w', 'grad_ln1_b': 'grad_w', 'grad_w_gate_up': 'grad_w', 'grad_w_down': 'grad_w', 'grad_ln2_g': 'grad_w', 'grad_ln2_b': 'grad_w', 'delta_w_in': 'delta_w', 'delta_pool_w': 'delta_w', 'delta_pool_scale': 'delta_w', 'delta_sgu_ln_g': 'delta_w', 'delta_sgu_ln_b': 'delta_w', 'delta_sgu_w': 'delta_w', 'delta_sgu_b': 'delta_w', 'delta_w_out': 'delta_w', 'delta_ln1_g': 'delta_w', 'delta_ln1_b': 'delta_w', 'delta_w_gate_up': 'delta_w', 'delta_w_down': 'delta_w', 'delta_ln2_g': 'delta_w', 'delta_ln2_b': 'delta_w', 'new_m_w_in': 'new_m', 'new_m_pool_w': 'new_m', 'new_m_pool_scale': 'new_m', 'new_m_sgu_ln_g': 'new_m', 'new_m_sgu_ln_b': 'new_m', 'new_m_sgu_w': 'new_m', 'new_m_sgu_b': 'new_m', 'new_m_w_out': 'new_m', 'new_m_ln1_g': 'new_m', 'new_m_ln1_b': 'new_m', 'new_m_w_gate_up': 'new_m', 'new_m_w_down': 'new_m', 'new_m_ln2_g': 'new_m', 'new_m_ln2_b': 'new_m', 'new_v_w_in': 'new_v', 'new_v_pool_w': 'new_v', 'new_v_pool_scale': 'new_v', 'new_v_sgu_ln_g': 'new_v', 'new_v_sgu_ln_b': 'new_v', 'new_v_sgu_w': 'new_v', 'new_v_sgu_b': 'new_v', 'new_v_w_out': 'new_v', 'new_v_ln1_g': 'new_v', 'new_v_ln1_b': 'new_v', 'new_v_w_gate_up': 'new_v', 'new_v_w_down': 'new_v', 'new_v_ln2_g': 'new_v', 'new_v_ln2_b': 'new_v'}


def _forward(args):
    return _fwd_reference(*[args[k] for k in FWD_PARAMS])


def _output_shape():
    out = _jax.eval_shape(lambda: _forward(_fwd_setup_inputs(0)))
    return out.shape, out.dtype

N_MICROBATCH = 1
ADAM_LR = 0.001
ADAM_B1 = 0.9
ADAM_B2 = 0.999
ADAM_EPS = 1e-08
ADAM_WD = 0.01
ADAM_STEP = 10
PER_EXAMPLE_BATCH_AXIS = {'x': 0, 'loss_target': 0}
SHARED_INPUTS = []
_WEIGHT_DTYPES = {'w_in': _jnp.float32, 'pool_w': _jnp.float32, 'pool_scale': _jnp.float32, 'sgu_ln_g': _jnp.float32, 'sgu_ln_b': _jnp.float32, 'sgu_w': _jnp.float32, 'sgu_b': _jnp.float32, 'w_out': _jnp.float32, 'ln1_g': _jnp.float32, 'ln1_b': _jnp.float32, 'w_gate_up': _jnp.float32, 'w_down': _jnp.float32, 'ln2_g': _jnp.float32, 'ln2_b': _jnp.float32}
MOMENT_SCALE = {'w_in': 9.152378e-02, 'pool_w': 1.060949e-01, 'pool_scale': 1.080154e-01, 'sgu_ln_g': 5.675021e-02, 'sgu_ln_b': 6.138779e-02, 'sgu_w': 5.645672e-02, 'sgu_b': 7.900780e-02, 'w_out': 1.735853e-01, 'ln1_g': 7.008495e-01, 'ln1_b': 3.596710e-01, 'w_gate_up': 4.377638e-02, 'w_down': 1.205132e-01, 'ln2_g': 6.380114e+01, 'ln2_b': 5.011419e+00}


def _to_microbatches(a, axis):
    t = _jnp.moveaxis(a, axis, 0)
    t = t.reshape((N_MICROBATCH, t.shape[0] // N_MICROBATCH) + t.shape[1:])
    return _jnp.moveaxis(t, 1, axis + 1)


def setup_inputs(seed: int = 0) -> dict:
    inp = _fwd_setup_inputs(seed)
    key = _jax.random.fold_in(_jax.random.key(seed), 7919)
    shape, _ = _output_shape()
    out = dict(inp)
    out["loss_target"] = _jax.random.normal(_jax.random.fold_in(key, 0), shape, _jnp.float32)
    for i, name in enumerate(TWIN_WEIGHTS):
        w = inp[name].astype(_jnp.float32)
        if MOMENT_SCALE is None:
            s = _jnp.sqrt(_jnp.mean(_jnp.square(w)) + 1e-30)
        else:
            s = MOMENT_SCALE[name]
        km, kv = _jax.random.split(_jax.random.fold_in(key, i + 1))
        out[name] = w
        out["m_" + name] = s * _jax.random.normal(km, w.shape, _jnp.float32)
        out["v_" + name] = (s * s) * _jax.random.uniform(kv, w.shape, _jnp.float32, 0.5, 1.5)
    if N_MICROBATCH > 1:
        for name, axis in PER_EXAMPLE_BATCH_AXIS.items():
            out[name] = _to_microbatches(out[name], axis)
    return {'x': out['x'], 'w_in': out['w_in'], 'pool_w': out['pool_w'], 'pool_scale': out['pool_scale'], 'sgu_ln_g': out['sgu_ln_g'], 'sgu_ln_b': out['sgu_ln_b'], 'sgu_w': out['sgu_w'], 'sgu_b': out['sgu_b'], 'w_out': out['w_out'], 'ln1_g': out['ln1_g'], 'ln1_b': out['ln1_b'], 'w_gate_up': out['w_gate_up'], 'w_down': out['w_down'], 'ln2_g': out['ln2_g'], 'ln2_b': out['ln2_b'], 'loss_target': out['loss_target'], 'm_w_in': out['m_w_in'], 'm_pool_w': out['m_pool_w'], 'm_pool_scale': out['m_pool_scale'], 'm_sgu_ln_g': out['m_sgu_ln_g'], 'm_sgu_ln_b': out['m_sgu_ln_b'], 'm_sgu_w': out['m_sgu_w'], 'm_sgu_b': out['m_sgu_b'], 'm_w_out': out['m_w_out'], 'm_ln1_g': out['m_ln1_g'], 'm_ln1_b': out['m_ln1_b'], 'm_w_gate_up': out['m_w_gate_up'], 'm_w_down': out['m_w_down'], 'm_ln2_g': out['m_ln2_g'], 'm_ln2_b': out['m_ln2_b'], 'v_w_in': out['v_w_in'], 'v_pool_w': out['v_pool_w'], 'v_pool_scale': out['v_pool_scale'], 'v_sgu_ln_g': out['v_sgu_ln_g'], 'v_sgu_ln_b': out['v_sgu_ln_b'], 'v_sgu_w': out['v_sgu_w'], 'v_sgu_b': out['v_sgu_b'], 'v_w_out': out['v_w_out'], 'v_ln1_g': out['v_ln1_g'], 'v_ln1_b': out['v_ln1_b'], 'v_w_gate_up': out['v_w_gate_up'], 'v_w_down': out['v_w_down'], 'v_ln2_g': out['v_ln2_g'], 'v_ln2_b': out['v_ln2_b']}


def _loss(weights, diff, rest, loss_target):
    with _jax.named_scope("forward"):
        args = {**rest, TWIN_DIFF_INPUT: diff, **{k: w.astype(_WEIGHT_DTYPES[k]) for k, w in weights.items()}}
        y = _forward(args)
    with _jax.named_scope("loss_head"):
        err = _jnp.square(y.astype(_jnp.float32) - loss_target)
        return 0.5 * _jnp.sum(_jnp.mean(err, axis=-1)) if err.ndim else 0.5 * err


def _adamw(w, g, m, v):
    m = ADAM_B1 * m + (1.0 - ADAM_B1) * g
    v = ADAM_B2 * v + (1.0 - ADAM_B2) * _jnp.square(g)
    m_hat = m / (1.0 - ADAM_B1 ** ADAM_STEP)
    v_hat = v / (1.0 - ADAM_B2 ** ADAM_STEP)
    delta = -ADAM_LR * (m_hat / (_jnp.sqrt(v_hat) + ADAM_EPS) + ADAM_WD * w)
    return delta, m, v


def reference(x, w_in, pool_w, pool_scale, sgu_ln_g, sgu_ln_b, sgu_w, sgu_b, w_out, ln1_g, ln1_b, w_gate_up, w_down, ln2_g, ln2_b, loss_target, m_w_in, m_pool_w, m_pool_scale, m_sgu_ln_g, m_sgu_ln_b, m_sgu_w, m_sgu_b, m_w_out, m_ln1_g, m_ln1_b, m_w_gate_up, m_w_down, m_ln2_g, m_ln2_b, v_w_in, v_pool_w, v_pool_scale, v_sgu_ln_g, v_sgu_ln_b, v_sgu_w, v_sgu_b, v_w_out, v_ln1_g, v_ln1_b, v_w_gate_up, v_w_down, v_ln2_g, v_ln2_b):
    given = dict(x=x, w_in=w_in, pool_w=pool_w, pool_scale=pool_scale, sgu_ln_g=sgu_ln_g, sgu_ln_b=sgu_ln_b, sgu_w=sgu_w, sgu_b=sgu_b, w_out=w_out, ln1_g=ln1_g, ln1_b=ln1_b, w_gate_up=w_gate_up, w_down=w_down, ln2_g=ln2_g, ln2_b=ln2_b, loss_target=loss_target, m_w_in=m_w_in, m_pool_w=m_pool_w, m_pool_scale=m_pool_scale, m_sgu_ln_g=m_sgu_ln_g, m_sgu_ln_b=m_sgu_ln_b, m_sgu_w=m_sgu_w, m_sgu_b=m_sgu_b, m_w_out=m_w_out, m_ln1_g=m_ln1_g, m_ln1_b=m_ln1_b, m_w_gate_up=m_w_gate_up, m_w_down=m_w_down, m_ln2_g=m_ln2_g, m_ln2_b=m_ln2_b, v_w_in=v_w_in, v_pool_w=v_pool_w, v_pool_scale=v_pool_scale, v_sgu_ln_g=v_sgu_ln_g, v_sgu_ln_b=v_sgu_ln_b, v_sgu_w=v_sgu_w, v_sgu_b=v_sgu_b, v_w_out=v_w_out, v_ln1_g=v_ln1_g, v_ln1_b=v_ln1_b, v_w_gate_up=v_w_gate_up, v_w_down=v_w_down, v_ln2_g=v_ln2_g, v_ln2_b=v_ln2_b)
    weights = {n: given[n] for n in TWIN_WEIGHTS}
    shared = {n: given[n] for n in SHARED_INPUTS}
    per_example = {n: given[n] for n in ['x']}
    grad_fn = _jax.value_and_grad(_loss, argnums=(0, 1))

    def one_microbatch(ex, loss_target):
        ex = dict(ex)
        diff = ex.pop(TWIN_DIFF_INPUT)
        return grad_fn(weights, diff, {**shared, **ex}, loss_target)

    if N_MICROBATCH == 1:
        loss, (grad_w, grad_x) = one_microbatch(per_example, given["loss_target"])
    else:
        def body(carry, xs):
            loss_sum, grad_sum = carry
            l_k, (gw_k, gx_k) = one_microbatch(xs[0], xs[1])
            with _jax.named_scope("update"):
                return (loss_sum + l_k, _jax.tree.map(_jnp.add, grad_sum, gw_k)), gx_k

        init = (_jnp.zeros((), _jnp.float32), _jax.tree.map(_jnp.zeros_like, weights))
        (loss, grad_w), grad_x = _jax.lax.scan(body, init, (per_example, given["loss_target"]))
    with _jax.named_scope("update"):
        delta_w, new_m, new_v = {}, {}, {}
        for n in TWIN_WEIGHTS:
            delta_w[n], new_m[n], new_v[n] = _adamw(weights[n], grad_w[n], given["m_" + n], given["v_" + n])
    return (loss, grad_x, *[grad_w[n] for n in TWIN_WEIGHTS], *[delta_w[n] for n in TWIN_WEIGHTS],
            *[new_m[n] for n in TWIN_WEIGHTS], *[new_v[n] for n in TWIN_WEIGHTS])
```

```python
import functools

import jax
import jax.numpy as jnp
from jax import lax
from jax.experimental import pallas as pl
from jax.experimental.pallas import tpu as pltpu

F32 = jnp.float32
BF16 = jnp.bfloat16
MESH = pl.DeviceIdType.MESH

D_MODEL = 1024
POOL_WIDTH = 512
SGU_WIDTH = 512
POOL_WINDOWS = (2, 4, 8, 16)
GROUP = 128
N_HEADS = 4
IN_COLS = POOL_WIDTH + 2 * SGU_WIDTH
D_FF = 2816
FF_CHUNK = 1408
LN_EPS = 1e-5
ALPHA = float(2.0 ** 0.25)
HALO = 16
N_DEV = 8
N_CHIPS = 4

ADAM_LR = 0.001
ADAM_B1 = 0.9
ADAM_B2 = 0.999
ADAM_EPS = 1e-08
ADAM_WD = 0.01
ADAM_STEP = 10

VMEM_LIMIT_BYTES = 56 * 1024 * 1024
SMALL_ROWS = 1072
PACK_ROWS = SMALL_ROWS + 8

_SQRT_HALF = 0.7071067811865476
_INV_SQRT_2PI = 0.3989422804014327


def _dot_nn(a, b):
    return lax.dot_general(a, b, (((1,), (0,)), ((), ())), preferred_element_type=F32)


def _dot_nt(a, b):
    return lax.dot_general(a, b, (((1,), (1,)), ((), ())), preferred_element_type=F32)


def _dot_tn(a, b):
    return lax.dot_general(a, b, (((0,), (0,)), ((), ())), preferred_element_type=F32)


def _gelu(x):
    return 0.5 * x * (1.0 + lax.erf(x * _SQRT_HALF))


def _gelu_grad(x):
    return 0.5 * (1.0 + lax.erf(x * _SQRT_HALF)) + x * jnp.exp(-0.5 * x * x) * _INV_SQRT_2PI


def _ln_stats(r):
    mu = jnp.mean(r, axis=-1, keepdims=True)
    d = r - mu
    var = jnp.mean(d * d, axis=-1, keepdims=True)
    rstd = lax.rsqrt(var + LN_EPS)
    return d * rstd, rstd


def _ln_bwd(dout, xhat, rstd, g):
    dxh = dout * g
    m1 = jnp.mean(dxh, axis=-1, keepdims=True)
    m2 = jnp.mean(dxh * xhat, axis=-1, keepdims=True)
    return rstd * (dxh - m1 - xhat * m2)


def _rowsum(a):
    return jnp.sum(a, axis=0, keepdims=True)


def _pool_fwd(xp, xp_prev, inv_cnt, w):
    s = jnp.concatenate([xp_prev, xp], axis=0)
    k = 1
    while k < w:
        s = s + pltpu.roll(s, k, 0)
        k *= 2
    return s[HALO:, :] * inv_cnt - xp


def _pool_bwd(dpooled, dpooled_next, inv_cnt, inv_cnt_next, w):
    n = dpooled.shape[0] + HALO
    s = jnp.concatenate([dpooled * inv_cnt, dpooled_next * inv_cnt_next], axis=0)
    k = 1
    while k < w:
        s = s + pltpu.roll(s, n - k, 0)
        k *= 2
    return s[: dpooled.shape[0], :] - dpooled


def _inv_count(pos, w):
    return 1.0 / jnp.minimum(pos + 1, w).astype(F32)


def _to_head_major(a, h, nc):
    return jnp.concatenate(
        [a[c * GROUP:(c + 1) * GROUP, h * GROUP:(h + 1) * GROUP] for c in range(nc)], axis=1)


def _masked_sgu_w(sw_ref, h):
    row = lax.broadcasted_iota(jnp.int32, (GROUP, GROUP), 0)
    col = lax.broadcasted_iota(jnp.int32, (GROUP, GROUP), 1)
    return jnp.where(row >= col, sw_ref[h], 0.0)


def _row_block(rows, limit):
    return max(b for b in range(16, min(rows, limit) + 1, 16) if rows % b == 0)


def _mesh_position():
    return lax.axis_index("x"), lax.axis_index("y"), lax.axis_index("c")


def _other_chips(x, y):
    return [(1 - x, y), (x, 1 - y), (1 - x, 1 - y)]


def _all_gather_rows(shards, name):
    na = len(shards)

    def body(*refs):
        ins, outs = refs[:na], refs[na:2 * na]
        send_sems, recv_sems, local_sems = refs[2 * na:]
        x, y, c = _mesh_position()
        me, sibling = (x, y, c), (x, y, 1 - c)
        chips = _other_chips(x, y)

        def rows(a, px, py, pc):
            n = shards[a].shape[0]
            return outs[a].at[pl.ds((4 * px + 2 * py + pc) * n, n), :]

        def copy(a, k, block, to, src=None):
            return pltpu.make_async_remote_copy(
                src_ref=rows(a, *block) if src is None else src, dst_ref=rows(a, *block),
                send_sem=send_sems.at[a * 7 + k], recv_sem=recv_sems.at[a * 7 + k],
                device_id=to, device_id_type=MESH)

        mine = [pltpu.make_async_copy(ins[a], rows(a, *me), local_sems.at[a]) for a in range(na)]
        for cp in mine:
            cp.start()
        first = []
        for a in range(na):
            first.append(copy(a, 0, me, sibling, src=ins[a]))
            first += [copy(a, 1 + j, me, (*chip, c), src=ins[a]) for j, chip in enumerate(chips)]
        for cp in first:
            cp.start()
        passed = []
        for j, chip in enumerate(chips):
            for a in range(na):
                copy(a, 1 + j, (*chip, c), me).wait_recv()
                fwd = copy(a, 4 + j, (*chip, c), sibling)
                fwd.start()
                passed.append(fwd)
        for a in range(na):
            copy(a, 0, sibling, me).wait_recv()
            for j, chip in enumerate(chips):
                copy(a, 4 + j, (*chip, 1 - c), me).wait_recv()
        for cp in first + passed:
            cp.wait_send()
        for cp in mine:
            cp.wait()

    any_spec = pl.BlockSpec(memory_space=pl.ANY)
    return pl.pallas_call(
        body, name=name,
        out_shape=[jax.ShapeDtypeStruct((N_DEV * s.shape[0], s.shape[1]), s.dtype) for s in shards],
        in_specs=[any_spec] * na, out_specs=[any_spec] * na,
        scratch_shapes=[pltpu.SemaphoreType.DMA((7 * na,)), pltpu.SemaphoreType.DMA((7 * na,)),
                        pltpu.SemaphoreType.DMA((na,))],
    )(*shards)


def _mixer_fwd(x2d, win_t, wout, pool_w, pool_scale, sln_g, sln_b, sgu_w, sgu_b_t, ln1_g, ln1_b, seq):
    tokens = x2d.shape[0]
    tt = min(512, seq)
    tiles_per_seq = seq // tt
    nc = tt // GROUP

    def body(x_ref, xh_ref, win_ref, wout_ref, pw_ref, ps_ref, lg_ref, lb_ref, sw_ref, sb_ref, g1_ref, b1_ref,
             proj_ref, xhat_ref, rstd_ref, mix_ref, xbf_ref, mix_scr):
        i = pl.program_id(0)
        tile_in_seq = i % tiles_per_seq
        x = x_ref[...]
        xb = x.astype(BF16)
        xbf_ref[...] = xb
        proj = _dot_nt(xb, win_ref[...])
        proj_ref[...] = proj
        xp_prev = _dot_nt(xh_ref[...].astype(BF16), win_ref[0:POOL_WIDTH, :])
        xp_prev = jnp.where(tile_in_seq == 0, 0.0, xp_prev)
        pos = tile_in_seq * tt + lax.broadcasted_iota(jnp.int32, (tt, 1), 0)
        for g, w in enumerate(POOL_WINDOWS):
            sl = slice(g * GROUP, (g + 1) * GROUP)
            pooled = _pool_fwd(proj[:, sl], xp_prev[:, sl], _inv_count(pos, w), w)
            pre = _dot_nn(pooled.astype(BF16), pw_ref[g].astype(BF16))
            mix_scr[:, sl] = pre * ps_ref[:, sl]
        u = _gelu(proj[:, POOL_WIDTH:POOL_WIDTH + SGU_WIDTH])
        v = _gelu(proj[:, POOL_WIDTH + SGU_WIDTH:])
        vhat, _ = _ln_stats(v)
        v_ln = vhat * lg_ref[...] + lb_ref[...]
        for h in range(N_HEADS):
            ws = _masked_sgu_w(sw_ref, h).astype(BF16)
            mixed = _dot_nn(ws, _to_head_major(v_ln, h, nc).astype(BF16)) + sb_ref[:, h:h + 1]
            for c in range(nc):
                rs = slice(c * GROUP, (c + 1) * GROUP)
                mix_scr[rs, POOL_WIDTH + h * GROUP:POOL_WIDTH + (h + 1) * GROUP] = (
                    u[rs, h * GROUP:(h + 1) * GROUP] * mixed[:, c * GROUP:(c + 1) * GROUP])
        mixb = mix_scr[...].astype(BF16)
        mix_ref[...] = mixb
        r1 = ALPHA * x + _dot_nn(mixb, wout_ref[...])
        xhat, rstd = _ln_stats(r1)
        xhat_ref[...] = xhat
        rstd_ref[...] = rstd

    def tile(cols):
        return pl.BlockSpec((tt, cols), lambda i: (i, 0))

    def whole(a):
        nd = a.ndim
        return pl.BlockSpec(a.shape, lambda i: (0,) * nd)

    halo = pl.BlockSpec((HALO, D_MODEL), lambda i: (jnp.maximum(i * (tt // HALO) - 1, 0), 0))
    consts = [win_t, wout, pool_w, pool_scale, sln_g, sln_b, sgu_w, sgu_b_t, ln1_g, ln1_b]
    return pl.pallas_call(
        body, name="mixer_fwd", grid=(tokens // tt,),
        in_specs=[tile(D_MODEL), halo] + [whole(a) for a in consts],
        out_specs=[tile(IN_COLS), tile(D_MODEL), tile(1), tile(D_MODEL), tile(D_MODEL)],
        out_shape=[jax.ShapeDtypeStruct((tokens, IN_COLS), F32), jax.ShapeDtypeStruct((tokens, D_MODEL), F32),
                   jax.ShapeDtypeStruct((tokens, 1), F32), jax.ShapeDtypeStruct((tokens, D_MODEL), BF16),
                   jax.ShapeDtypeStruct((tokens, D_MODEL), BF16)],
        scratch_shapes=[pltpu.VMEM((tt, D_MODEL), F32)],
        compiler_params=pltpu.CompilerParams(dimension_semantics=("arbitrary",), vmem_limit_bytes=VMEM_LIMIT_BYTES),
    )(x2d, x2d, *consts)


def _ffn_fwd_bwd(xhat1, rstd1, target, wgu_t, wdown, ln1_g, ln1_b, ln2_g, ln2_b):
    tokens = xhat1.shape[0]
    tt = min(256, tokens)
    n_chunks = D_FF // FF_CHUNK

    def body(xhat_ref, rstd_ref, tgt_ref, wgu_hbm, wd_hbm, g1_ref, b1_ref, g2_ref, b2_ref,
             dr1_ref, hbf_ref, dr2bf_ref, a_ref, dgu_ref, stats_ref, wgu_ref, wd_ref, gu_scr, sems):
        i = pl.program_id(0)

        @pl.when(i == 0)
        def _():
            loads = [pltpu.make_async_copy(wgu_hbm, wgu_ref, sems.at[0]),
                     pltpu.make_async_copy(wd_hbm, wd_ref, sems.at[1])]
            for cp in loads:
                cp.start()
            stats_ref[...] = jnp.zeros_like(stats_ref)
            for cp in loads:
                cp.wait()

        xhat1_t = xhat_ref[...]
        h = xhat1_t * g1_ref[...] + b1_ref[...]
        hb = h.astype(BF16)
        hbf_ref[...] = hb
        f = jnp.zeros((tt, D_MODEL), F32)
        for k in range(n_chunks):
            rg = slice(k * FF_CHUNK, (k + 1) * FF_CHUNK)
            ru = slice(D_FF + k * FF_CHUNK, D_FF + (k + 1) * FF_CHUNK)
            gate = _dot_nt(hb, wgu_ref[rg, :])
            up = _dot_nt(hb, wgu_ref[ru, :])
            gu_scr[:, rg] = gate
            gu_scr[:, ru] = up
            ab = (gate * jax.nn.sigmoid(gate) * up).astype(BF16)
            a_ref[:, rg] = ab
            f = f + _dot_nn(ab, wd_ref[rg, :])
        xhat2, rstd2 = _ln_stats(ALPHA * h + f)
        err = xhat2 * g2_ref[...] + b2_ref[...] - tgt_ref[...]
        dy = err * (1.0 / D_MODEL)
        stats_ref[0:1, :] += _rowsum(dy * xhat2)
        stats_ref[1:2, :] += _rowsum(dy)
        stats_ref[4:5, :] += _rowsum(err * err)
        dr2 = _ln_bwd(dy, xhat2, rstd2, g2_ref[...])
        dr2b = dr2.astype(BF16)
        dr2bf_ref[...] = dr2b
        dh = ALPHA * dr2
        for k in range(n_chunks):
            rg = slice(k * FF_CHUNK, (k + 1) * FF_CHUNK)
            ru = slice(D_FF + k * FF_CHUNK, D_FF + (k + 1) * FF_CHUNK)
            da = _dot_nt(dr2b, wd_ref[rg, :])
            gate = gu_scr[:, rg]
            up = gu_scr[:, ru]
            sg = jax.nn.sigmoid(gate)
            dgate = (da * up * (sg * (1.0 + gate * (1.0 - sg)))).astype(BF16)
            dup = (da * (gate * sg)).astype(BF16)
            dgu_ref[:, rg] = dgate
            dgu_ref[:, ru] = dup
            dh = dh + _dot_nn(dgate, wgu_ref[rg, :]) + _dot_nn(dup, wgu_ref[ru, :])
        stats_ref[2:3, :] += _rowsum(dh * xhat1_t)
        stats_ref[3:4, :] += _rowsum(dh)
        dr1_ref[...] = _ln_bwd(dh, xhat1_t, rstd_ref[...], g1_ref[...])

    def tile(cols):
        return pl.BlockSpec((tt, cols), lambda i: (i, 0))

    def whole(a):
        nd = a.ndim
        return pl.BlockSpec(a.shape, lambda i: (0,) * nd)

    any_spec = pl.BlockSpec(memory_space=pl.ANY)
    vecs = [ln1_g, ln1_b, ln2_g, ln2_b]
    return pl.pallas_call(
        body, name="ffn_fwd_bwd", grid=(tokens // tt,),
        in_specs=[tile(D_MODEL), tile(1), tile(D_MODEL), any_spec, any_spec] + [whole(a) for a in vecs],
        out_specs=[tile(D_MODEL), tile(D_MODEL), tile(D_MODEL), tile(D_FF), tile(2 * D_FF),
                   pl.BlockSpec((8, D_MODEL), lambda i: (0, 0))],
        out_shape=[jax.ShapeDtypeStruct((tokens, D_MODEL), F32), jax.ShapeDtypeStruct((tokens, D_MODEL), BF16),
                   jax.ShapeDtypeStruct((tokens, D_MODEL), BF16), jax.ShapeDtypeStruct((tokens, D_FF), BF16),
                   jax.ShapeDtypeStruct((tokens, 2 * D_FF), BF16), jax.ShapeDtypeStruct((8, D_MODEL), F32)],
        scratch_shapes=[pltpu.VMEM(wgu_t.shape, BF16), pltpu.VMEM(wdown.shape, BF16),
                        pltpu.VMEM((tt, 2 * D_FF), F32), pltpu.SemaphoreType.DMA((2,))],
        compiler_params=pltpu.CompilerParams(dimension_semantics=("arbitrary",), vmem_limit_bytes=VMEM_LIMIT_BYTES),
    )(xhat1, rstd1, target, wgu_t, wdown, *vecs)


def _wgrad(lhs, rhs, block_rows, name):
    tokens, n = lhs.shape
    m = rhs.shape[1]
    tw = min(1024, tokens)

    def body(l_ref, r_ref, o_ref):
        @pl.when(pl.program_id(1) == 0)
        def _():
            o_ref[...] = jnp.zeros_like(o_ref)

        o_ref[...] += _dot_tn(l_ref[...], r_ref[...])

    return pl.pallas_call(
        body, name=name, grid=(n // block_rows, tokens // tw),
        in_specs=[pl.BlockSpec((tw, block_rows), lambda j, t: (t, j)), pl.BlockSpec((tw, m), lambda j, t: (t, 0))],
        out_specs=pl.BlockSpec((block_rows, m), lambda j, t: (j, 0)),
        out_shape=jax.ShapeDtypeStruct((n, m), F32),
        compiler_params=pltpu.CompilerParams(dimension_semantics=("arbitrary", "arbitrary"),
                                             vmem_limit_bytes=VMEM_LIMIT_BYTES),
    )(lhs, rhs)


def _mixer_bwd(dr1, proj, win_t, wout, pool_w, pool_scale, sln_g, sln_b, sgu_w, sgu_b_t, seq):
    tokens = dr1.shape[0]
    tt = min(512, seq)
    tiles_per_seq = seq // tt
    nc = tt // GROUP
    n_halo_blocks = tokens // HALO

    def body(dr1_ref, dr1n_ref, proj_ref, projh_ref, win_ref, wout_ref, pw_ref, ps_ref, lg_ref, lb_ref, sw_ref, sb_ref,
             gx_ref, dproj_ref, dr1bf_ref, dpw_ref, dps_ref, dlg_ref, dlb_ref, dsw_ref, dsb_ref, du_scr, dv_scr):
        i = pl.program_id(0)
        tile_in_seq = i % tiles_per_seq

        @pl.when(i == 0)
        def _():
            for r in (dpw_ref, dps_ref, dlg_ref, dlb_ref, dsw_ref, dsb_ref):
                r[...] = jnp.zeros_like(r)

        dr1_t = dr1_ref[...]
        dr1b = dr1_t.astype(BF16)
        dr1bf_ref[...] = dr1b
        dmix = _dot_nt(dr1b, wout_ref[...])
        dpo_next = _dot_nt(dr1n_ref[...].astype(BF16), wout_ref[0:POOL_WIDTH, :])
        dpo_next = jnp.where(tile_in_seq == tiles_per_seq - 1, 0.0, dpo_next)
        proj = proj_ref[...]
        xp_prev = jnp.where(tile_in_seq == 0, 0.0, projh_ref[...])
        pos = tile_in_seq * tt + lax.broadcasted_iota(jnp.int32, (tt, 1), 0)
        pos_next = (tile_in_seq + 1) * tt + lax.broadcasted_iota(jnp.int32, (HALO, 1), 0)

        for g, w in enumerate(POOL_WINDOWS):
            sl = slice(g * GROUP, (g + 1) * GROUP)
            inv_cnt = _inv_count(pos, w)
            pwb = pw_ref[g].astype(BF16)
            pooledb = _pool_fwd(proj[:, sl], xp_prev[:, sl], inv_cnt, w).astype(BF16)
            pre = _dot_nn(pooledb, pwb)
            dpo = dmix[:, sl]
            dps_ref[:, sl] += _rowsum(dpo * pre)
            dsb = (dpo * ps_ref[:, sl]).astype(BF16)
            dpw_ref[g] += _dot_tn(pooledb, dsb)
            dpooled = _dot_nt(dsb, pwb)
            dpooled_next = _dot_nt((dpo_next[:, sl] * ps_ref[:, sl]).astype(BF16), pwb)
            dxp = _pool_bwd(dpooled, dpooled_next, inv_cnt, _inv_count(pos_next, w), w)
            dproj_ref[:, sl] = dxp.astype(BF16)

        zu = proj[:, POOL_WIDTH:POOL_WIDTH + SGU_WIDTH]
        zv = proj[:, POOL_WIDTH + SGU_WIDTH:]
        u = _gelu(zu)
        vhat, rstd_v = _ln_stats(_gelu(zv))
        v_ln = vhat * lg_ref[...] + lb_ref[...]
        dsg = dmix[:, POOL_WIDTH:]
        row = lax.broadcasted_iota(jnp.int32, (GROUP, GROUP), 0)
        col = lax.broadcasted_iota(jnp.int32, (GROUP, GROUP), 1)
        for h in range(N_HEADS):
            ws = _masked_sgu_w(sw_ref, h).astype(BF16)
            vh = _to_head_major(v_ln, h, nc).astype(BF16)
            mixed = _dot_nn(ws, vh) + sb_ref[:, h:h + 1]
            dsg_h = _to_head_major(dsg, h, nc)
            du_h = dsg_h * mixed
            dm_h = dsg_h * _to_head_major(u, h, nc)
            dsb_ref[:, h:h + 1] += jnp.sum(dm_h, axis=1, keepdims=True)
            dmb = dm_h.astype(BF16)
            dsw_ref[h] += jnp.where(row >= col, _dot_nt(dmb, vh), 0.0)
            dv_h = _dot_tn(ws, dmb)
            for c in range(nc):
                rs = slice(c * GROUP, (c + 1) * GROUP)
                cs = slice(h * GROUP, (h + 1) * GROUP)
                du_scr[rs, cs] = du_h[:, c * GROUP:(c + 1) * GROUP]
                dv_scr[rs, cs] = dv_h[:, c * GROUP:(c + 1) * GROUP]
        dv_ln = dv_scr[...]
        dlb_ref[...] += _rowsum(dv_ln)
        dlg_ref[...] += _rowsum(dv_ln * vhat)
        dv = _ln_bwd(dv_ln, vhat, rstd_v, lg_ref[...])
        dproj_ref[:, POOL_WIDTH:POOL_WIDTH + SGU_WIDTH] = (du_scr[...] * _gelu_grad(zu)).astype(BF16)
        dproj_ref[:, POOL_WIDTH + SGU_WIDTH:] = (dv * _gelu_grad(zv)).astype(BF16)
        gx_ref[...] = ALPHA * dr1_t + _dot_nn(dproj_ref[...], win_ref[...])

    def tile(cols):
        return pl.BlockSpec((tt, cols), lambda i: (i, 0))

    def whole(a):
        nd = a.ndim
        return pl.BlockSpec(a.shape, lambda i: (0,) * nd)

    def resident(shape):
        nd = len(shape)
        return pl.BlockSpec(shape, lambda i: (0,) * nd)

    next_halo = pl.BlockSpec((HALO, D_MODEL), lambda i: (jnp.minimum((i + 1) * (tt // HALO), n_halo_blocks - 1), 0))
    prev_halo = pl.BlockSpec((HALO, POOL_WIDTH), lambda i: (jnp.maximum(i * (tt // HALO) - 1, 0), 0))
    consts = [win_t, wout, pool_w, pool_scale, sln_g, sln_b, sgu_w, sgu_b_t]
    small_shapes = [(len(POOL_WINDOWS), GROUP, GROUP), (1, POOL_WIDTH), (1, SGU_WIDTH), (1, SGU_WIDTH),
                    (N_HEADS, GROUP, GROUP), (GROUP, N_HEADS)]
    return pl.pallas_call(
        body, name="mixer_bwd", grid=(tokens // tt,),
        in_specs=[tile(D_MODEL), next_halo, tile(IN_COLS), prev_halo] + [whole(a) for a in consts],
        out_specs=[tile(D_MODEL), tile(IN_COLS), tile(D_MODEL)] + [resident(s) for s in small_shapes],
        out_shape=[jax.ShapeDtypeStruct((tokens, D_MODEL), F32), jax.ShapeDtypeStruct((tokens, IN_COLS), BF16),
                   jax.ShapeDtypeStruct((tokens, D_MODEL), BF16)]
                  + [jax.ShapeDtypeStruct(s, F32) for s in small_shapes],
        scratch_shapes=[pltpu.VMEM((tt, SGU_WIDTH), F32), pltpu.VMEM((tt, SGU_WIDTH), F32)],
        compiler_params=pltpu.CompilerParams(dimension_semantics=("arbitrary",), vmem_limit_bytes=VMEM_LIMIT_BYTES),
    )(dr1, dr1, proj, proj, *consts)


def _exchange_with_sibling(partials):
    na = len(partials)

    def body(*refs):
        ins, outs = refs[:na], refs[na:2 * na]
        send_sems, recv_sems = refs[2 * na:]
        x, y, c = _mesh_position()
        copies = []
        for a in range(na):
            n = partials[a].shape[0] // N_DEV
            for q in range(N_CHIPS):
                copies.append(pltpu.make_async_remote_copy(
                    src_ref=ins[a].at[pl.ds((2 * q + 1 - c) * n, n), :], dst_ref=outs[a].at[q],
                    send_sem=send_sems.at[a * N_CHIPS + q], recv_sem=recv_sems.at[a * N_CHIPS + q],
                    device_id=(x, y, 1 - c), device_id_type=MESH))
        for cp in copies:
            cp.start()
        for cp in copies:
            cp.wait_send()
            cp.wait_recv()

    any_spec = pl.BlockSpec(memory_space=pl.ANY)
    return pl.pallas_call(
        body, name="grad_exchange_d2d",
        out_shape=[jax.ShapeDtypeStruct((N_CHIPS, p.shape[0] // N_DEV, p.shape[1]), F32) for p in partials],
        in_specs=[any_spec] * na, out_specs=[any_spec] * na,
        scratch_shapes=[pltpu.SemaphoreType.DMA((N_CHIPS * na,)), pltpu.SemaphoreType.DMA((N_CHIPS * na,))],
    )(*partials)


def _add_sibling(partial, from_sibling, place, name):
    n = partial.shape[0] // N_DEV
    cols = partial.shape[1]
    rb = _row_block(n, 512)

    def body(place_ref, mine_ref, sib_ref, bf_ref, own_ref):
        q = pl.program_id(1)
        s = mine_ref[...] + sib_ref[...]
        bf_ref[...] = s.astype(BF16)

        @pl.when(q == place_ref[1])
        def _():
            own_ref[...] = s

    grid_spec = pltpu.PrefetchScalarGridSpec(
        num_scalar_prefetch=1, grid=(n // rb, N_CHIPS),
        in_specs=[pl.BlockSpec((None, None, rb, cols), lambda r, q, pr: (q, pr[0], r, 0)),
                  pl.BlockSpec((None, rb, cols), lambda r, q, pr: (q, r, 0))],
        out_specs=[pl.BlockSpec((None, rb, cols), lambda r, q, pr: (q, r, 0)),
                   pl.BlockSpec((rb, cols), lambda r, q, pr: (r, 0))],
    )
    return pl.pallas_call(
        body, name=name, grid_spec=grid_spec,
        out_shape=[jax.ShapeDtypeStruct((N_CHIPS, n, cols), BF16), jax.ShapeDtypeStruct((n, cols), F32)],
        compiler_params=pltpu.CompilerParams(dimension_semantics=("arbitrary", "arbitrary")),
    )(place, partial.reshape(N_CHIPS, 2, n, cols), from_sibling)


def _send_to_owners(chip_partials):
    na = len(chip_partials)

    def body(*refs):
        ins, outs = refs[:na], refs[na:2 * na]
        send_sems, recv_sems = refs[2 * na:]
        x, y, c = _mesh_position()
        copies = []
        for a in range(na):
            for j, (cx, cy) in enumerate(_other_chips(x, y)):
                copies.append(pltpu.make_async_remote_copy(
                    src_ref=ins[a].at[2 * cx + cy], dst_ref=outs[a].at[j],
                    send_sem=send_sems.at[a * 3 + j], recv_sem=recv_sems.at[a * 3 + j],
                    device_id=(cx, cy, c), device_id_type=MESH))
        for cp in copies:
            cp.start()
        for cp in copies:
            cp.wait_send()
            cp.wait_recv()

    any_spec = pl.BlockSpec(memory_space=pl.ANY)
    return pl.pallas_call(
        body, name="grad_scatter_ici",
        out_shape=[jax.ShapeDtypeStruct((3,) + p.shape[1:], BF16) for p in chip_partials],
        in_specs=[any_spec] * na, out_specs=[any_spec] * na,
        scratch_shapes=[pltpu.SemaphoreType.DMA((3 * na,)), pltpu.SemaphoreType.DMA((3 * na,))],
    )(*chip_partials)


def _sum_partials(own, received, name):
    n, cols = own.shape
    rb = _row_block(n, 512)

    def body(own_ref, rec_ref, g_ref):
        g = own_ref[...]
        for j in range(3):
            g = g + rec_ref[j].astype(F32)
        g_ref[...] = g

    return pl.pallas_call(
        body, name=name, grid=(n // rb,),
        in_specs=[pl.BlockSpec((rb, cols), lambda r: (r, 0)), pl.BlockSpec((3, rb, cols), lambda r: (0, r, 0))],
        out_specs=pl.BlockSpec((rb, cols), lambda r: (r, 0)),
        out_shape=jax.ShapeDtypeStruct((n, cols), F32),
        compiler_params=pltpu.CompilerParams(dimension_semantics=("arbitrary",)),
    )(own, received)


def _adamw_math(w, g, m, v):
    m = ADAM_B1 * m + (1.0 - ADAM_B1) * g
    v = ADAM_B2 * v + (1.0 - ADAM_B2) * (g * g)
    m_hat = m / (1.0 - ADAM_B1 ** ADAM_STEP)
    v_hat = v / (1.0 - ADAM_B2 ** ADAM_STEP)
    delta = -ADAM_LR * (m_hat / (jnp.sqrt(v_hat) + ADAM_EPS) + ADAM_WD * w)
    return delta, m, v


def _adamw(w, g, m, v, name):
    rows, cols = w.shape
    rb = _row_block(rows, 256)

    def body(w_ref, g_ref, m_ref, v_ref, d_ref, mo_ref, vo_ref):
        d_ref[...], mo_ref[...], vo_ref[...] = _adamw_math(w_ref[...], g_ref[...], m_ref[...], v_ref[...])

    spec = pl.BlockSpec((rb, cols), lambda r: (r, 0))
    return pl.pallas_call(
        body, name=name, grid=(rows // rb,),
        in_specs=[spec] * 4, out_specs=[spec] * 3,
        out_shape=[jax.ShapeDtypeStruct((rows, cols), F32)] * 3,
        compiler_params=pltpu.CompilerParams(dimension_semantics=("arbitrary",)),
    )(w, g, m, v)


def _small_allreduce_adamw(part, w, m, v):
    def body(part_ref, w_ref, m_ref, v_ref, g_ref, loss_ref, d_ref, mo_ref, vo_ref, gath, send_sems, recv_sems):
        x, y, c = _mesh_position()
        me, sibling = (x, y, c), (x, y, 1 - c)
        chips = _other_chips(x, y)

        def slot(px, py, pc):
            return gath.at[4 * px + 2 * py + pc]

        def copy(k, block, to, src=None):
            return pltpu.make_async_remote_copy(
                src_ref=slot(*block) if src is None else src, dst_ref=slot(*block),
                send_sem=send_sems.at[k], recv_sem=recv_sems.at[k], device_id=to, device_id_type=MESH)

        first = [copy(0, me, sibling, src=part_ref)]
        first += [copy(1 + j, me, (*chip, c), src=part_ref) for j, chip in enumerate(chips)]
        for cp in first:
            cp.start()
        slot(*me)[...] = part_ref[...]
        passed = [copy(4 + j, (*chip, c), sibling) for j, chip in enumerate(chips)]
        for j, chip in enumerate(chips):
            copy(1 + j, (*chip, c), me).wait_recv()
            passed[j].start()
        copy(0, sibling, me).wait_recv()
        for j, chip in enumerate(chips):
            copy(4 + j, (*chip, 1 - c), me).wait_recv()
        for cp in first + passed:
            cp.wait_send()
        total = gath[0]
        for d in range(1, N_DEV):
            total = total + gath[d]
        g = total[:SMALL_ROWS, :]
        g_ref[...] = g
        loss = jnp.sum(jnp.sum(total[SMALL_ROWS:, :], axis=1, keepdims=True), axis=0, keepdims=True)
        loss_ref[...] = jnp.broadcast_to(loss * (0.5 / D_MODEL), loss_ref.shape)
        d_ref[...], mo_ref[...], vo_ref[...] = _adamw_math(w_ref[...], g, m_ref[...], v_ref[...])

    vmem = pl.BlockSpec(memory_space=pltpu.VMEM)
    small = jax.ShapeDtypeStruct((SMALL_ROWS, GROUP), F32)
    return pl.pallas_call(
        body, name="small_allreduce_adamw",
        out_shape=[small, jax.ShapeDtypeStruct((8, GROUP), F32), small, small, small],
        in_specs=[vmem] * 4, out_specs=[vmem] * 5,
        scratch_shapes=[pltpu.VMEM((N_DEV, PACK_ROWS, GROUP), F32),
                        pltpu.SemaphoreType.DMA((7,)), pltpu.SemaphoreType.DMA((7,))],
    )(part, w, m, v)


_SMALL_LAYOUT = (("pool_w", 512), ("sgu_w", 512), ("pool_scale", 4), ("sgu_ln_g", 4), ("sgu_ln_b", 4), ("sgu_b", 4),
                 ("ln1_g", 8), ("ln1_b", 8), ("ln2_g", 8), ("ln2_b", 8))


def _pack_small(d):
    return jnp.concatenate([d[k].reshape(r, GROUP) for k, r in _SMALL_LAYOUT], axis=0)


def _unpack_small(p, shapes):
    out, at = {}, 0
    for k, r in _SMALL_LAYOUT:
        out[k] = p[at:at + r].reshape(shapes[k])
        at += r
    return out


def kernel(x, w_in, pool_w, pool_scale, sgu_ln_g, sgu_ln_b, sgu_w, sgu_b, w_out, ln1_g, ln1_b, w_gate_up, w_down, ln2_g, ln2_b, loss_target, m_w_in, m_pool_w, m_pool_scale, m_sgu_ln_g, m_sgu_ln_b, m_sgu_w, m_sgu_b, m_w_out, m_ln1_g, m_ln1_b, m_w_gate_up, m_w_down, m_ln2_g, m_ln2_b, v_w_in, v_pool_w, v_pool_scale, v_sgu_ln_g, v_sgu_ln_b, v_sgu_w, v_sgu_b, v_w_out, v_ln1_g, v_ln1_b, v_w_gate_up, v_w_down, v_ln2_g, v_ln2_b):
    bl, seq, _ = x.shape
    tokens = bl * seq
    x2d = x.reshape(tokens, D_MODEL)
    tgt2d = loss_target.reshape(tokens, D_MODEL)
    my_c = lax.axis_index("c")
    place = jnp.stack([my_c, 2 * lax.axis_index("x") + lax.axis_index("y")]).astype(jnp.int32)

    shards = [w_in[0].T.astype(BF16), w_out[0].astype(BF16), w_gate_up[0].T.astype(BF16), w_down[0].astype(BF16)]
    win_t, wout, wgu_t, wdown = _all_gather_rows(shards, "weight_all_gather")

    pool_w3, sgu_w3 = pool_w[0], sgu_w[0]
    sgu_b_t = sgu_b[0].T
    proj, xhat1, rstd1, mix_bf, x_bf = _mixer_fwd(
        x2d, win_t, wout, pool_w3, pool_scale, sgu_ln_g, sgu_ln_b, sgu_w3, sgu_b_t, ln1_g, ln1_b, seq)
    dr1, h_bf, dr2_bf, a_bf, dgu_bf, stats = _ffn_fwd_bwd(
        xhat1, rstd1, tgt2d, wgu_t, wdown, ln1_g, ln1_b, ln2_g, ln2_b)
    d_wgu_t = _wgrad(dgu_bf, h_bf, FF_CHUNK, "wgrad_gate_up")
    d_wdown = _wgrad(a_bf, dr2_bf, FF_CHUNK, "wgrad_down")
    gx, dproj_bf, dr1_bf, d_pw, d_ps, d_lg, d_lb, d_sw, d_sb_t = _mixer_bwd(
        dr1, proj, win_t, wout, pool_w3, pool_scale, sgu_ln_g, sgu_ln_b, sgu_w3, sgu_b_t, seq)
    d_win_t = _wgrad(dproj_bf, x_bf, IN_COLS // 2, "wgrad_in")
    d_wout = _wgrad(mix_bf, dr1_bf, D_MODEL, "wgrad_out")

    partials = [d_win_t, d_wout, d_wgu_t, d_wdown]
    names = ["w_in", "w_out", "w_gate_up", "w_down"]
    from_sibling = _exchange_with_sibling(partials)
    chip_bf, own = [], []
    for p, s, nm in zip(partials, from_sibling, names):
        b, o = _add_sibling(p, s, place, "chip_sum_" + nm)
        chip_bf.append(b)
        own.append(o)
    received = _send_to_owners(chip_bf)
    g_rows = [_sum_partials(o, r, "grad_sum_" + nm) for o, r, nm in zip(own, received, names)]
    g_big = {"w_in": g_rows[0].T, "w_out": g_rows[1], "w_gate_up": g_rows[2].T, "w_down": g_rows[3]}
    big_w = {"w_in": w_in, "w_out": w_out, "w_gate_up": w_gate_up, "w_down": w_down}
    big_m = {"w_in": m_w_in, "w_out": m_w_out, "w_gate_up": m_w_gate_up, "w_down": m_w_down}
    big_v = {"w_in": v_w_in, "w_out": v_w_out, "w_gate_up": v_w_gate_up, "w_down": v_w_down}
    grads, deltas, new_m, new_v = {}, {}, {}, {}
    for nm in names:
        d, mo, vo = _adamw(big_w[nm][0], g_big[nm], big_m[nm][0], big_v[nm][0], "adamw_" + nm)
        grads[nm], deltas[nm], new_m[nm], new_v[nm] = g_big[nm][None], d[None], mo[None], vo[None]

    small_w = {"pool_w": pool_w, "pool_scale": pool_scale, "sgu_ln_g": sgu_ln_g, "sgu_ln_b": sgu_ln_b, "sgu_w": sgu_w,
               "sgu_b": sgu_b, "ln1_g": ln1_g, "ln1_b": ln1_b, "ln2_g": ln2_g, "ln2_b": ln2_b}
    small_m = {"pool_w": m_pool_w, "pool_scale": m_pool_scale, "sgu_ln_g": m_sgu_ln_g, "sgu_ln_b": m_sgu_ln_b,
               "sgu_w": m_sgu_w, "sgu_b": m_sgu_b, "ln1_g": m_ln1_g, "ln1_b": m_ln1_b, "ln2_g": m_ln2_g, "ln2_b": m_ln2_b}
    small_v = {"pool_w": v_pool_w, "pool_scale": v_pool_scale, "sgu_ln_g": v_sgu_ln_g, "sgu_ln_b": v_sgu_ln_b,
               "sgu_w": v_sgu_w, "sgu_b": v_sgu_b, "ln1_g": v_ln1_g, "ln1_b": v_ln1_b, "ln2_g": v_ln2_g, "ln2_b": v_ln2_b}
    small_g = {"pool_w": d_pw, "sgu_w": d_sw, "pool_scale": d_ps, "sgu_ln_g": d_lg, "sgu_ln_b": d_lb, "sgu_b": d_sb_t.T,
               "ln1_g": stats[2], "ln1_b": stats[3], "ln2_g": stats[0], "ln2_b": stats[1]}
    part = jnp.concatenate([_pack_small(small_g), stats[4].reshape(8, GROUP)], axis=0)
    g_pk, loss_blk, d_pk, m_pk, v_pk = _small_allreduce_adamw(
        part, _pack_small(small_w), _pack_small(small_m), _pack_small(small_v))
    shapes = {k: a.shape for k, a in small_w.items()}
    for pk, dst in ((g_pk, grads), (d_pk, deltas), (m_pk, new_m), (v_pk, new_v)):
        dst.update(_unpack_small(pk, shapes))

    order = ["w_in", "pool_w", "pool_scale", "sgu_ln_g", "sgu_ln_b", "sgu_w", "sgu_b", "w_out", "ln1_g", "ln1_b",
             "w_gate_up", "w_down", "ln2_g", "ln2_b"]
    return (loss_blk[0, 0], gx.reshape(bl, seq, D_MODEL), *[grads[k] for k in order], *[deltas[k] for k in order],
            *[new_m[k] for k in order], *[new_v[k] for k in order])
```

```python
import functools

import jax
import jax.numpy as jnp
from jax import lax
from jax.experimental import pallas as pl
from jax.experimental.pallas import tpu as pltpu

F32 = jnp.float32
BF16 = jnp.bfloat16
MESH = pl.DeviceIdType.MESH

D_MODEL = 1024
POOL_WIDTH = 512
SGU_WIDTH = 512
POOL_WINDOWS = (2, 4, 8, 16)
GROUP = 128
N_HEADS = 4
IN_COLS = POOL_WIDTH + 2 * SGU_WIDTH
D_FF = 2816
FF_CHUNK = 1408
LN_EPS = 1e-5
ALPHA = float(2.0 ** 0.25)
HALO = 16
N_DEV = 8
N_CHIPS = 4

ADAM_LR = 0.001
ADAM_B1 = 0.9
ADAM_B2 = 0.999
ADAM_EPS = 1e-08
ADAM_WD = 0.01
ADAM_STEP = 10

VMEM_LIMIT_BYTES = 56 * 1024 * 1024
SMALL_ROWS = 1072
PACK_ROWS = SMALL_ROWS + 8

_SQRT_HALF = 0.7071067811865476
_INV_SQRT_2PI = 0.3989422804014327


def _dot_nn(a, b):
    return lax.dot_general(a, b, (((1,), (0,)), ((), ())), preferred_element_type=F32)


def _dot_nt(a, b):
    return lax.dot_general(a, b, (((1,), (1,)), ((), ())), preferred_element_type=F32)


def _dot_tn(a, b):
    return lax.dot_general(a, b, (((0,), (0,)), ((), ())), preferred_element_type=F32)


def _gelu(x):
    return 0.5 * x * (1.0 + lax.erf(x * _SQRT_HALF))


def _gelu_grad(x):
    return 0.5 * (1.0 + lax.erf(x * _SQRT_HALF)) + x * jnp.exp(-0.5 * x * x) * _INV_SQRT_2PI


def _ln_stats(r):
    mu = jnp.mean(r, axis=-1, keepdims=True)
    d = r - mu
    var = jnp.mean(d * d, axis=-1, keepdims=True)
    rstd = lax.rsqrt(var + LN_EPS)
    return d * rstd, rstd


def _ln_bwd(dout, xhat, rstd, g):
    dxh = dout * g
    m1 = jnp.mean(dxh, axis=-1, keepdims=True)
    m2 = jnp.mean(dxh * xhat, axis=-1, keepdims=True)
    return rstd * (dxh - m1 - xhat * m2)


def _rowsum(a):
    return jnp.sum(a, axis=0, keepdims=True)


def _pool_fwd(xp, xp_prev, inv_cnt, w):
    s = jnp.concatenate([xp_prev, xp], axis=0)
    k = 1
    while k < w:
        s = s + pltpu.roll(s, k, 0)
        k *= 2
    return s[HALO:, :] * inv_cnt - xp


def _pool_bwd(dpooled, dpooled_next, inv_cnt, inv_cnt_next, w):
    n = dpooled.shape[0] + HALO
    s = jnp.concatenate([dpooled * inv_cnt, dpooled_next * inv_cnt_next], axis=0)
    k = 1
    while k < w:
        s = s + pltpu.roll(s, n - k, 0)
        k *= 2
    return s[: dpooled.shape[0], :] - dpooled


def _inv_count(pos, w):
    return 1.0 / jnp.minimum(pos + 1, w).astype(F32)


def _to_head_major(a, h, nc):
    return jnp.concatenate(
        [a[c * GROUP:(c + 1) * GROUP, h * GROUP:(h + 1) * GROUP] for c in range(nc)], axis=1)


def _masked_sgu_w(sw_ref, h):
    row = lax.broadcasted_iota(jnp.int32, (GROUP, GROUP), 0)
    col = lax.broadcasted_iota(jnp.int32, (GROUP, GROUP), 1)
    return jnp.where(row >= col, sw_ref[h], 0.0)


def _row_block(rows, limit):
    return max(b for b in range(16, min(rows, limit) + 1, 16) if rows % b == 0)


def _mesh_position():
    return lax.axis_index("x"), lax.axis_index("y"), lax.axis_index("c")


def _other_chips(x, y):
    return [(1 - x, y), (x, 1 - y), (1 - x, 1 - y)]


class _TwoLevelGather:
    def __init__(self, ins, outs, send_sems, recv_sems, local_sems):
        self.ins, self.outs = ins, outs
        self.send_sems, self.recv_sems, self.local_sems = send_sems, recv_sems, local_sems
        self.na = len(ins)
        x, y, c = _mesh_position()
        self.c = c
        self.me, self.sibling = (x, y, c), (x, y, 1 - c)
        self.chips = _other_chips(x, y)

    def _rows(self, a, px, py, pc):
        n = self.ins[a].shape[0]
        return self.outs[a].at[pl.ds((4 * px + 2 * py + pc) * n, n), :]

    def _copy(self, a, k, block, to, src=None):
        return pltpu.make_async_remote_copy(
            src_ref=self._rows(a, *block) if src is None else src, dst_ref=self._rows(a, *block),
            send_sem=self.send_sems.at[a * 7 + k], recv_sem=self.recv_sems.at[a * 7 + k],
            device_id=to, device_id_type=MESH)

    def _mine(self, a):
        return pltpu.make_async_copy(self.ins[a], self._rows(a, *self.me), self.local_sems.at[a])

    def start(self):
        for a in range(self.na):
            self._mine(a).start()
        for a in range(self.na):
            self._copy(a, 0, self.me, self.sibling, src=self.ins[a]).start()
            for j, chip in enumerate(self.chips):
                self._copy(a, 1 + j, self.me, (*chip, self.c), src=self.ins[a]).start()

    def pass_on(self, a):
        for j, chip in enumerate(self.chips):
            self._copy(a, 1 + j, (*chip, self.c), self.me).wait_recv()
            self._copy(a, 4 + j, (*chip, self.c), self.sibling).start()

    def finish(self):
        for a in range(self.na):
            self._copy(a, 0, self.sibling, self.me).wait_recv()
            for j, chip in enumerate(self.chips):
                self._copy(a, 4 + j, (*chip, 1 - self.c), self.me).wait_recv()
        for a in range(self.na):
            for k in range(7):
                self._copy(a, k, self.me, self.sibling, src=self.ins[a]).wait_send()
            self._mine(a).wait()

    @staticmethod
    def scratch(na):
        return [pltpu.SemaphoreType.DMA((7 * na,)), pltpu.SemaphoreType.DMA((7 * na,)), pltpu.SemaphoreType.DMA((na,))]


def _gathered_shape(s):
    return jax.ShapeDtypeStruct((N_DEV * s.shape[0], s.shape[1]), s.dtype)


def _all_gather_rows(shards, name):
    na = len(shards)

    def body(*refs):
        gather = _TwoLevelGather(refs[:na], refs[na:2 * na], *refs[2 * na:])
        gather.start()
        for a in range(na):
            gather.pass_on(a)
        gather.finish()

    any_spec = pl.BlockSpec(memory_space=pl.ANY)
    return pl.pallas_call(
        body, name=name, out_shape=[_gathered_shape(s) for s in shards],
        in_specs=[any_spec] * na, out_specs=[any_spec] * na, scratch_shapes=_TwoLevelGather.scratch(na),
    )(*shards)


def _mixer_fwd(x2d, win_t, wout, pool_w, pool_scale, sln_g, sln_b, sgu_w, sgu_b_t, ln1_g, ln1_b, later_shards, seq):
    tokens = x2d.shape[0]
    tt = min(512, seq)
    tiles_per_seq = seq // tt
    nc = tt // GROUP
    n_tiles = tokens // tt
    n_later = len(later_shards)

    def body(x_ref, xh_ref, win_ref, wout_ref, pw_ref, ps_ref, lg_ref, lb_ref, sw_ref, sb_ref, g1_ref, b1_ref, *rest):
        shard_refs, rest = rest[:n_later], rest[n_later:]
        proj_ref, xhat_ref, rstd_ref, mix_ref, xbf_ref = rest[:5]
        gathered_refs, rest = rest[5:5 + n_later], rest[5 + n_later:]
        mix_scr, send_sems, recv_sems, local_sems = rest
        i = pl.program_id(0)
        gather = _TwoLevelGather(shard_refs, gathered_refs, send_sems, recv_sems, local_sems)

        @pl.when(i == 0)
        def _():
            gather.start()

        tile_in_seq = i % tiles_per_seq
        x = x_ref[...]
        xb = x.astype(BF16)
        xbf_ref[...] = xb
        proj = _dot_nt(xb, win_ref[...])
        proj_ref[...] = proj
        xp_prev = _dot_nt(xh_ref[...].astype(BF16), win_ref[0:POOL_WIDTH, :])
        xp_prev = jnp.where(tile_in_seq == 0, 0.0, xp_prev)
        pos = tile_in_seq * tt + lax.broadcasted_iota(jnp.int32, (tt, 1), 0)
        for g, w in enumerate(POOL_WINDOWS):
            sl = slice(g * GROUP, (g + 1) * GROUP)
            pooled = _pool_fwd(proj[:, sl], xp_prev[:, sl], _inv_count(pos, w), w)
            pre = _dot_nn(pooled.astype(BF16), pw_ref[g].astype(BF16))
            mix_scr[:, sl] = pre * ps_ref[:, sl]
        u = _gelu(proj[:, POOL_WIDTH:POOL_WIDTH + SGU_WIDTH])
        v = _gelu(proj[:, POOL_WIDTH + SGU_WIDTH:])
        vhat, _ = _ln_stats(v)
        v_ln = vhat * lg_ref[...] + lb_ref[...]
        for h in range(N_HEADS):
            ws = _masked_sgu_w(sw_ref, h).astype(BF16)
            mixed = _dot_nn(ws, _to_head_major(v_ln, h, nc).astype(BF16)) + sb_ref[:, h:h + 1]
            for c in range(nc):
                rs = slice(c * GROUP, (c + 1) * GROUP)
                mix_scr[rs, POOL_WIDTH + h * GROUP:POOL_WIDTH + (h + 1) * GROUP] = (
                    u[rs, h * GROUP:(h + 1) * GROUP] * mixed[:, c * GROUP:(c + 1) * GROUP])
        mixb = mix_scr[...].astype(BF16)
        mix_ref[...] = mixb
        r1 = ALPHA * x + _dot_nn(mixb, wout_ref[...])
        xhat, rstd = _ln_stats(r1)
        xhat_ref[...] = xhat
        rstd_ref[...] = rstd

        for a in range(n_later):
            @pl.when(i == max(n_tiles - n_later + a, 0))
            def _(a=a):
                gather.pass_on(a)

        @pl.when(i == n_tiles - 1)
        def _():
            gather.finish()

    def tile(cols):
        return pl.BlockSpec((tt, cols), lambda i: (i, 0))

    def whole(a):
        nd = a.ndim
        return pl.BlockSpec(a.shape, lambda i: (0,) * nd)

    any_spec = pl.BlockSpec(memory_space=pl.ANY)
    halo = pl.BlockSpec((HALO, D_MODEL), lambda i: (jnp.maximum(i * (tt // HALO) - 1, 0), 0))
    consts = [win_t, wout, pool_w, pool_scale, sln_g, sln_b, sgu_w, sgu_b_t, ln1_g, ln1_b]
    return pl.pallas_call(
        body, name="mixer_fwd", grid=(n_tiles,),
        in_specs=[tile(D_MODEL), halo] + [whole(a) for a in consts] + [any_spec] * n_later,
        out_specs=[tile(IN_COLS), tile(D_MODEL), tile(1), tile(D_MODEL), tile(D_MODEL)] + [any_spec] * n_later,
        out_shape=[jax.ShapeDtypeStruct((tokens, IN_COLS), F32), jax.ShapeDtypeStruct((tokens, D_MODEL), F32),
                   jax.ShapeDtypeStruct((tokens, 1), F32), jax.ShapeDtypeStruct((tokens, D_MODEL), BF16),
                   jax.ShapeDtypeStruct((tokens, D_MODEL), BF16)] + [_gathered_shape(s) for s in later_shards],
        scratch_shapes=[pltpu.VMEM((tt, D_MODEL), F32)] + _TwoLevelGather.scratch(n_later),
        compiler_params=pltpu.CompilerParams(dimension_semantics=("arbitrary",), vmem_limit_bytes=VMEM_LIMIT_BYTES),
    )(x2d, x2d, *consts, *later_shards)


def _ffn_fwd_bwd(xhat1, rstd1, target, wgu_t, wdown, ln1_g, ln1_b, ln2_g, ln2_b):
    tokens = xhat1.shape[0]
    tt = min(256, tokens)
    n_chunks = D_FF // FF_CHUNK

    def body(xhat_ref, rstd_ref, tgt_ref, wgu_hbm, wd_hbm, g1_ref, b1_ref, g2_ref, b2_ref,
             dr1_ref, dr1bf_ref, hbf_ref, dr2bf_ref, a_ref, dgu_ref, stats_ref, wgu_ref, wd_ref, gu_scr, sems):
        i = pl.program_id(0)

        @pl.when(i == 0)
        def _():
            loads = [pltpu.make_async_copy(wgu_hbm, wgu_ref, sems.at[0]),
                     pltpu.make_async_copy(wd_hbm, wd_ref, sems.at[1])]
            for cp in loads:
                cp.start()
            stats_ref[...] = jnp.zeros_like(stats_ref)
            for cp in loads:
                cp.wait()

        xhat1_t = xhat_ref[...]
        h = xhat1_t * g1_ref[...] + b1_ref[...]
        hb = h.astype(BF16)
        hbf_ref[...] = hb
        f = jnp.zeros((tt, D_MODEL), F32)
        for k in range(n_chunks):
            rg = slice(k * FF_CHUNK, (k + 1) * FF_CHUNK)
            ru = slice(D_FF + k * FF_CHUNK, D_FF + (k + 1) * FF_CHUNK)
            gate = _dot_nt(hb, wgu_ref[rg, :])
            up = _dot_nt(hb, wgu_ref[ru, :])
            gu_scr[:, rg] = gate
            gu_scr[:, ru] = up
            ab = (gate * jax.nn.sigmoid(gate) * up).astype(BF16)
            a_ref[:, rg] = ab
            f = f + _dot_nn(ab, wd_ref[rg, :])
        xhat2, rstd2 = _ln_stats(ALPHA * h + f)
        err = xhat2 * g2_ref[...] + b2_ref[...] - tgt_ref[...]
        dy = err * (1.0 / D_MODEL)
        stats_ref[0:1, :] += _rowsum(dy * xhat2)
        stats_ref[1:2, :] += _rowsum(dy)
        stats_ref[4:5, :] += _rowsum(err * err)
        dr2 = _ln_bwd(dy, xhat2, rstd2, g2_ref[...])
        dr2b = dr2.astype(BF16)
        dr2bf_ref[...] = dr2b
        dh = ALPHA * dr2
        for k in range(n_chunks):
            rg = slice(k * FF_CHUNK, (k + 1) * FF_CHUNK)
            ru = slice(D_FF + k * FF_CHUNK, D_FF + (k + 1) * FF_CHUNK)
            da = _dot_nt(dr2b, wd_ref[rg, :])
            gate = gu_scr[:, rg]
            up = gu_scr[:, ru]
            sg = jax.nn.sigmoid(gate)
            dgate = (da * up * (sg * (1.0 + gate * (1.0 - sg)))).astype(BF16)
            dup = (da * (gate * sg)).astype(BF16)
            dgu_ref[:, rg] = dgate
            dgu_ref[:, ru] = dup
            dh = dh + _dot_nn(dgate, wgu_ref[rg, :]) + _dot_nn(dup, wgu_ref[ru, :])
        stats_ref[2:3, :] += _rowsum(dh * xhat1_t)
        stats_ref[3:4, :] += _rowsum(dh)
        dr1 = _ln_bwd(dh, xhat1_t, rstd_ref[...], g1_ref[...])
        dr1_ref[...] = dr1
        dr1bf_ref[...] = dr1.astype(BF16)

    def tile(cols):
        return pl.BlockSpec((tt, cols), lambda i: (i, 0))

    def whole(a):
        nd = a.ndim
        return pl.BlockSpec(a.shape, lambda i: (0,) * nd)

    any_spec = pl.BlockSpec(memory_space=pl.ANY)
    vecs = [ln1_g, ln1_b, ln2_g, ln2_b]
    return pl.pallas_call(
        body, name="ffn_fwd_bwd", grid=(tokens // tt,),
        in_specs=[tile(D_MODEL), tile(1), tile(D_MODEL), any_spec, any_spec] + [whole(a) for a in vecs],
        out_specs=[tile(D_MODEL), tile(D_MODEL), tile(D_MODEL), tile(D_MODEL), tile(D_FF), tile(2 * D_FF),
                   pl.BlockSpec((8, D_MODEL), lambda i: (0, 0))],
        out_shape=[jax.ShapeDtypeStruct((tokens, D_MODEL), F32), jax.ShapeDtypeStruct((tokens, D_MODEL), BF16),
                   jax.ShapeDtypeStruct((tokens, D_MODEL), BF16),
                   jax.ShapeDtypeStruct((tokens, D_MODEL), BF16), jax.ShapeDtypeStruct((tokens, D_FF), BF16),
                   jax.ShapeDtypeStruct((tokens, 2 * D_FF), BF16), jax.ShapeDtypeStruct((8, D_MODEL), F32)],
        scratch_shapes=[pltpu.VMEM(wgu_t.shape, BF16), pltpu.VMEM(wdown.shape, BF16),
                        pltpu.VMEM((tt, 2 * D_FF), F32), pltpu.SemaphoreType.DMA((2,))],
        compiler_params=pltpu.CompilerParams(dimension_semantics=("arbitrary",), vmem_limit_bytes=VMEM_LIMIT_BYTES),
    )(xhat1, rstd1, target, wgu_t, wdown, *vecs)


def _wgrad_exchange(lhs, rhs, chips_per_block, name):
    tokens, n_all = lhs.shape
    m = rhs.shape[1]
    n = n_all // N_DEV
    tw = min(1024, tokens)
    nt = tokens // tw
    cpb = chips_per_block
    nj = N_CHIPS // cpb

    def body(l_ref, r_ref, kept_ref, sib_ref, acc, sendbuf, send_sems, recv_sems):
        j, t = pl.program_id(0), pl.program_id(1)
        x, y, c = _mesh_position()

        def copy(q):
            return pltpu.make_async_remote_copy(
                src_ref=sendbuf.at[q], dst_ref=sib_ref.at[q], send_sem=send_sems.at[q], recv_sem=recv_sems.at[q],
                device_id=(x, y, 1 - c), device_id_type=MESH)

        prod = _dot_tn(l_ref[...], r_ref[...])

        @pl.when(t == 0)
        def _():
            acc[...] = prod

        @pl.when(t > 0)
        def _():
            acc[...] += prod

        @pl.when(t == nt - 1)
        def _():
            for qq in range(cpb):
                q = j * cpb + qq
                kept_ref[qq] = acc[pl.ds(pl.multiple_of(qq * 2 * n + c * n, 8), n), :]
                sendbuf[q] = acc[pl.ds(pl.multiple_of(qq * 2 * n + (1 - c) * n, 8), n), :]
                copy(q).start()

        @pl.when((j == nj - 1) & (t == nt - 1))
        def _():
            for q in range(N_CHIPS):
                copy(q).wait_send()
                copy(q).wait_recv()

    shard4 = jax.ShapeDtypeStruct((N_CHIPS, n, m), F32)
    return pl.pallas_call(
        body, name=name, grid=(nj, nt),
        in_specs=[pl.BlockSpec((tw, 2 * n * cpb), lambda j, t: (t, j)), pl.BlockSpec((tw, m), lambda j, t: (t, 0))],
        out_specs=[pl.BlockSpec((cpb, n, m), lambda j, t: (j, 0, 0)), pl.BlockSpec(memory_space=pl.ANY)],
        out_shape=[shard4, shard4],
        scratch_shapes=[pltpu.VMEM((2 * n * cpb, m), F32), pltpu.VMEM((N_CHIPS, n, m), F32),
                        pltpu.SemaphoreType.DMA((N_CHIPS,)), pltpu.SemaphoreType.DMA((N_CHIPS,))],
        compiler_params=pltpu.CompilerParams(dimension_semantics=("arbitrary", "arbitrary"),
                                             vmem_limit_bytes=VMEM_LIMIT_BYTES),
    )(lhs, rhs)


class _ScatterToOwners:
    def __init__(self, ins, outs, send_sems, recv_sems):
        self.ins, self.outs, self.send_sems, self.recv_sems = ins, outs, send_sems, recv_sems
        x, y, c = _mesh_position()
        self.c, self.chips = c, _other_chips(x, y)

    def _copies(self):
        return [pltpu.make_async_remote_copy(
            src_ref=self.ins[a].at[2 * cx + cy], dst_ref=self.outs[a].at[j],
            send_sem=self.send_sems.at[a * 3 + j], recv_sem=self.recv_sems.at[a * 3 + j],
            device_id=(cx, cy, self.c), device_id_type=MESH)
            for a in range(len(self.ins)) for j, (cx, cy) in enumerate(self.chips)]

    def start(self):
        for cp in self._copies():
            cp.start()

    def finish(self):
        for cp in self._copies():
            cp.wait_send()
            cp.wait_recv()

    @staticmethod
    def scratch(na):
        return [pltpu.SemaphoreType.DMA((3 * na,)), pltpu.SemaphoreType.DMA((3 * na,))]


def _received_shape(p):
    return jax.ShapeDtypeStruct((3,) + p.shape[1:], p.dtype)


def _mixer_bwd(dr1, proj, win_t, wout, pool_w, pool_scale, sln_g, sln_b, sgu_w, sgu_b_t, chip_partials, seq):
    tokens = dr1.shape[0]
    tt = min(512, seq)
    tiles_per_seq = seq // tt
    nc = tt // GROUP
    n_halo_blocks = tokens // HALO
    n_tiles = tokens // tt
    n_part = len(chip_partials)

    def body(dr1_ref, dr1n_ref, proj_ref, projh_ref, win_ref, wout_ref, pw_ref, ps_ref, lg_ref, lb_ref, sw_ref, sb_ref,
             *rest):
        part_refs, rest = rest[:n_part], rest[n_part:]
        gx_ref, dproj_ref, dpw_ref, dps_ref, dlg_ref, dlb_ref, dsw_ref, dsb_ref = rest[:8]
        recv_refs, rest = rest[8:8 + n_part], rest[8 + n_part:]
        du_scr, dv_scr, send_sems, recv_sems = rest
        i = pl.program_id(0)
        tile_in_seq = i % tiles_per_seq
        scatter = _ScatterToOwners(part_refs, recv_refs, send_sems, recv_sems)

        @pl.when(i == 0)
        def _():
            scatter.start()
            for r in (dpw_ref, dps_ref, dlg_ref, dlb_ref, dsw_ref, dsb_ref):
                r[...] = jnp.zeros_like(r)

        dr1_t = dr1_ref[...]
        dr1b = dr1_t.astype(BF16)
        dmix = _dot_nt(dr1b, wout_ref[...])
        dpo_next = _dot_nt(dr1n_ref[...].astype(BF16), wout_ref[0:POOL_WIDTH, :])
        dpo_next = jnp.where(tile_in_seq == tiles_per_seq - 1, 0.0, dpo_next)
        proj = proj_ref[...]
        xp_prev = jnp.where(tile_in_seq == 0, 0.0, projh_ref[...])
        pos = tile_in_seq * tt + lax.broadcasted_iota(jnp.int32, (tt, 1), 0)
        pos_next = (tile_in_seq + 1) * tt + lax.broadcasted_iota(jnp.int32, (HALO, 1), 0)

        for g, w in enumerate(POOL_WINDOWS):
            sl = slice(g * GROUP, (g + 1) * GROUP)
            inv_cnt = _inv_count(pos, w)
            pwb = pw_ref[g].astype(BF16)
            pooledb = _pool_fwd(proj[:, sl], xp_prev[:, sl], inv_cnt, w).astype(BF16)
            pre = _dot_nn(pooledb, pwb)
            dpo = dmix[:, sl]
            dps_ref[:, sl] += _rowsum(dpo * pre)
            dsb = (dpo * ps_ref[:, sl]).astype(BF16)
            dpw_ref[g] += _dot_tn(pooledb, dsb)
            dpooled = _dot_nt(dsb, pwb)
            dpooled_next = _dot_nt((dpo_next[:, sl] * ps_ref[:, sl]).astype(BF16), pwb)
            dxp = _pool_bwd(dpooled, dpooled_next, inv_cnt, _inv_count(pos_next, w), w)
            dproj_ref[:, sl] = dxp.astype(BF16)

        zu = proj[:, POOL_WIDTH:POOL_WIDTH + SGU_WIDTH]
        zv = proj[:, POOL_WIDTH + SGU_WIDTH:]
        u = _gelu(zu)
        vhat, rstd_v = _ln_stats(_gelu(zv))
        v_ln = vhat * lg_ref[...] + lb_ref[...]
        dsg = dmix[:, POOL_WIDTH:]
        row = lax.broadcasted_iota(jnp.int32, (GROUP, GROUP), 0)
        col = lax.broadcasted_iota(jnp.int32, (GROUP, GROUP), 1)
        for h in range(N_HEADS):
            ws = _masked_sgu_w(sw_ref, h).astype(BF16)
            vh = _to_head_major(v_ln, h, nc).astype(BF16)
            mixed = _dot_nn(ws, vh) + sb_ref[:, h:h + 1]
            dsg_h = _to_head_major(dsg, h, nc)
            du_h = dsg_h * mixed
            dm_h = dsg_h * _to_head_major(u, h, nc)
            dsb_ref[:, h:h + 1] += jnp.sum(dm_h, axis=1, keepdims=True)
            dmb = dm_h.astype(BF16)
            dsw_ref[h] += jnp.where(row >= col, _dot_nt(dmb, vh), 0.0)
            dv_h = _dot_tn(ws, dmb)
            for c in range(nc):
                rs = slice(c * GROUP, (c + 1) * GROUP)
                cs = slice(h * GROUP, (h + 1) * GROUP)
                du_scr[rs, cs] = du_h[:, c * GROUP:(c + 1) * GROUP]
                dv_scr[rs, cs] = dv_h[:, c * GROUP:(c + 1) * GROUP]
        dv_ln = dv_scr[...]
        dlb_ref[...] += _rowsum(dv_ln)
        dlg_ref[...] += _rowsum(dv_ln * vhat)
        dv = _ln_bwd(dv_ln, vhat, rstd_v, lg_ref[...])
        dproj_ref[:, POOL_WIDTH:POOL_WIDTH + SGU_WIDTH] = (du_scr[...] * _gelu_grad(zu)).astype(BF16)
        dproj_ref[:, POOL_WIDTH + SGU_WIDTH:] = (dv * _gelu_grad(zv)).astype(BF16)
        gx_ref[...] = ALPHA * dr1_t + _dot_nn(dproj_ref[...], win_ref[...])

        @pl.when(i == n_tiles - 1)
        def _():
            scatter.finish()

    def tile(cols):
        return pl.BlockSpec((tt, cols), lambda i: (i, 0))

    def whole(a):
        nd = a.ndim
        return pl.BlockSpec(a.shape, lambda i: (0,) * nd)

    def resident(shape):
        nd = len(shape)
        return pl.BlockSpec(shape, lambda i: (0,) * nd)

    any_spec = pl.BlockSpec(memory_space=pl.ANY)
    next_halo = pl.BlockSpec((HALO, D_MODEL), lambda i: (jnp.minimum((i + 1) * (tt // HALO), n_halo_blocks - 1), 0))
    prev_halo = pl.BlockSpec((HALO, POOL_WIDTH), lambda i: (jnp.maximum(i * (tt // HALO) - 1, 0), 0))
    consts = [win_t, wout, pool_w, pool_scale, sln_g, sln_b, sgu_w, sgu_b_t]
    small_shapes = [(len(POOL_WINDOWS), GROUP, GROUP), (1, POOL_WIDTH), (1, SGU_WIDTH), (1, SGU_WIDTH),
                    (N_HEADS, GROUP, GROUP), (GROUP, N_HEADS)]
    return pl.pallas_call(
        body, name="mixer_bwd", grid=(n_tiles,),
        in_specs=[tile(D_MODEL), next_halo, tile(IN_COLS), prev_halo] + [whole(a) for a in consts]
                 + [any_spec] * n_part,
        out_specs=[tile(D_MODEL), tile(IN_COLS)] + [resident(s) for s in small_shapes] + [any_spec] * n_part,
        out_shape=[jax.ShapeDtypeStruct((tokens, D_MODEL), F32), jax.ShapeDtypeStruct((tokens, IN_COLS), BF16)]
                  + [jax.ShapeDtypeStruct(s, F32) for s in small_shapes] + [_received_shape(p) for p in chip_partials],
        scratch_shapes=[pltpu.VMEM((tt, SGU_WIDTH), F32), pltpu.VMEM((tt, SGU_WIDTH), F32)]
                       + _ScatterToOwners.scratch(n_part),
        compiler_params=pltpu.CompilerParams(dimension_semantics=("arbitrary",), vmem_limit_bytes=VMEM_LIMIT_BYTES),
    )(dr1, dr1, proj, proj, *consts, *chip_partials)


def _chip_sums(kept, from_sibling, place, name):
    na = len(kept)

    def body(place_ref, *refs):
        kept_refs, sib_refs = refs[:na], refs[na:2 * na]
        bf_refs, own_refs = refs[2 * na:3 * na], refs[3 * na:]
        q = pl.program_id(0)
        for a in range(na):
            s = kept_refs[a][...] + sib_refs[a][...]
            bf_refs[a][...] = s.astype(BF16)

            @pl.when(q == place_ref[1])
            def _(a=a, s=s):
                own_refs[a][...] = s

    by_chip = [pl.BlockSpec((None,) + k.shape[1:], lambda q, pr: (q, 0, 0)) for k in kept]
    grid_spec = pltpu.PrefetchScalarGridSpec(
        num_scalar_prefetch=1, grid=(N_CHIPS,), in_specs=by_chip + by_chip,
        out_specs=by_chip + [pl.BlockSpec(k.shape[1:], lambda q, pr: (0, 0)) for k in kept])
    return pl.pallas_call(
        body, name=name, grid_spec=grid_spec,
        out_shape=[jax.ShapeDtypeStruct(k.shape, BF16) for k in kept]
                  + [jax.ShapeDtypeStruct(k.shape[1:], F32) for k in kept],
        compiler_params=pltpu.CompilerParams(dimension_semantics=("arbitrary",), vmem_limit_bytes=VMEM_LIMIT_BYTES),
    )(place, *kept, *from_sibling)


def _send_to_owners(chip_partials, name):
    na = len(chip_partials)

    def body(*refs):
        scatter = _ScatterToOwners(refs[:na], refs[na:2 * na], *refs[2 * na:])
        scatter.start()
        scatter.finish()

    any_spec = pl.BlockSpec(memory_space=pl.ANY)
    return pl.pallas_call(
        body, name=name, out_shape=[_received_shape(p) for p in chip_partials],
        in_specs=[any_spec] * na, out_specs=[any_spec] * na, scratch_shapes=_ScatterToOwners.scratch(na),
    )(*chip_partials)


def _sum_partials(own, received, name):
    n, cols = own.shape
    rb = _row_block(n, 512)

    def body(own_ref, rec_ref, g_ref):
        g = own_ref[...]
        for j in range(3):
            g = g + rec_ref[j].astype(F32)
        g_ref[...] = g

    return pl.pallas_call(
        body, name=name, grid=(n // rb,),
        in_specs=[pl.BlockSpec((rb, cols), lambda r: (r, 0)), pl.BlockSpec((3, rb, cols), lambda r: (0, r, 0))],
        out_specs=pl.BlockSpec((rb, cols), lambda r: (r, 0)),
        out_shape=jax.ShapeDtypeStruct((n, cols), F32),
        compiler_params=pltpu.CompilerParams(dimension_semantics=("arbitrary",)),
    )(own, received)


def _adamw_math(w, g, m, v):
    m = ADAM_B1 * m + (1.0 - ADAM_B1) * g
    v = ADAM_B2 * v + (1.0 - ADAM_B2) * (g * g)
    m_hat = m / (1.0 - ADAM_B1 ** ADAM_STEP)
    v_hat = v / (1.0 - ADAM_B2 ** ADAM_STEP)
    delta = -ADAM_LR * (m_hat / (jnp.sqrt(v_hat) + ADAM_EPS) + ADAM_WD * w)
    return delta, m, v


def _adamw(w, g, m, v, name):
    rows, cols = w.shape
    rb = _row_block(rows, 256)

    def body(w_ref, g_ref, m_ref, v_ref, d_ref, mo_ref, vo_ref):
        d_ref[...], mo_ref[...], vo_ref[...] = _adamw_math(w_ref[...], g_ref[...], m_ref[...], v_ref[...])

    spec = pl.BlockSpec((rb, cols), lambda r: (r, 0))
    return pl.pallas_call(
        body, name=name, grid=(rows // rb,),
        in_specs=[spec] * 4, out_specs=[spec] * 3,
        out_shape=[jax.ShapeDtypeStruct((rows, cols), F32)] * 3,
        compiler_params=pltpu.CompilerParams(dimension_semantics=("arbitrary",)),
    )(w, g, m, v)


def _small_allreduce_adamw(part, w, m, v):
    def body(part_ref, w_ref, m_ref, v_ref, g_ref, loss_ref, d_ref, mo_ref, vo_ref, gath, send_sems, recv_sems):
        x, y, c = _mesh_position()
        me, sibling = (x, y, c), (x, y, 1 - c)
        chips = _other_chips(x, y)

        def slot(px, py, pc):
            return gath.at[4 * px + 2 * py + pc]

        def copy(k, block, to, src=None):
            return pltpu.make_async_remote_copy(
                src_ref=slot(*block) if src is None else src, dst_ref=slot(*block),
                send_sem=send_sems.at[k], recv_sem=recv_sems.at[k], device_id=to, device_id_type=MESH)

        first = [copy(0, me, sibling, src=part_ref)]
        first += [copy(1 + j, me, (*chip, c), src=part_ref) for j, chip in enumerate(chips)]
        for cp in first:
            cp.start()
        slot(*me)[...] = part_ref[...]
        passed = [copy(4 + j, (*chip, c), sibling) for j, chip in enumerate(chips)]
        for j, chip in enumerate(chips):
            copy(1 + j, (*chip, c), me).wait_recv()
            passed[j].start()
        copy(0, sibling, me).wait_recv()
        for j, chip in enumerate(chips):
            copy(4 + j, (*chip, 1 - c), me).wait_recv()
        for cp in first + passed:
            cp.wait_send()
        total = gath[0]
        for d in range(1, N_DEV):
            total = total + gath[d]
        g = total[:SMALL_ROWS, :]
        g_ref[...] = g
        loss = jnp.sum(jnp.sum(total[SMALL_ROWS:, :], axis=1, keepdims=True), axis=0, keepdims=True)
        loss_ref[...] = jnp.broadcast_to(loss * (0.5 / D_MODEL), loss_ref.shape)
        d_ref[...], mo_ref[...], vo_ref[...] = _adamw_math(w_ref[...], g, m_ref[...], v_ref[...])

    vmem = pl.BlockSpec(memory_space=pltpu.VMEM)
    small = jax.ShapeDtypeStruct((SMALL_ROWS, GROUP), F32)
    return pl.pallas_call(
        body, name="small_allreduce_adamw",
        out_shape=[small, jax.ShapeDtypeStruct((8, GROUP), F32), small, small, small],
        in_specs=[vmem] * 4, out_specs=[vmem] * 5,
        scratch_shapes=[pltpu.VMEM((N_DEV, PACK_ROWS, GROUP), F32),
                        pltpu.SemaphoreType.DMA((7,)), pltpu.SemaphoreType.DMA((7,))],
    )(part, w, m, v)


_SMALL_LAYOUT = (("pool_w", 512), ("sgu_w", 512), ("pool_scale", 4), ("sgu_ln_g", 4), ("sgu_ln_b", 4), ("sgu_b", 4),
                 ("ln1_g", 8), ("ln1_b", 8), ("ln2_g", 8), ("ln2_b", 8))


def _pack_small(d):
    return jnp.concatenate([d[k].reshape(r, GROUP) for k, r in _SMALL_LAYOUT], axis=0)


def _unpack_small(p, shapes):
    out, at = {}, 0
    for k, r in _SMALL_LAYOUT:
        out[k] = p[at:at + r].reshape(shapes[k])
        at += r
    return out


def kernel(x, w_in, pool_w, pool_scale, sgu_ln_g, sgu_ln_b, sgu_w, sgu_b, w_out, ln1_g, ln1_b, w_gate_up, w_down, ln2_g, ln2_b, loss_target, m_w_in, m_pool_w, m_pool_scale, m_sgu_ln_g, m_sgu_ln_b, m_sgu_w, m_sgu_b, m_w_out, m_ln1_g, m_ln1_b, m_w_gate_up, m_w_down, m_ln2_g, m_ln2_b, v_w_in, v_pool_w, v_pool_scale, v_sgu_ln_g, v_sgu_ln_b, v_sgu_w, v_sgu_b, v_w_out, v_ln1_g, v_ln1_b, v_w_gate_up, v_w_down, v_ln2_g, v_ln2_b):
    bl, seq, _ = x.shape
    tokens = bl * seq
    x2d = x.reshape(tokens, D_MODEL)
    tgt2d = loss_target.reshape(tokens, D_MODEL)
    my_c = lax.axis_index("c")
    place = jnp.stack([my_c, 2 * lax.axis_index("x") + lax.axis_index("y")]).astype(jnp.int32)

    win_t, wout = _all_gather_rows([w_in[0].T.astype(BF16), w_out[0].astype(BF16)], "weight_all_gather")

    pool_w3, sgu_w3 = pool_w[0], sgu_w[0]
    sgu_b_t = sgu_b[0].T
    proj, xhat1, rstd1, mix_bf, x_bf, wgu_t, wdown = _mixer_fwd(
        x2d, win_t, wout, pool_w3, pool_scale, sgu_ln_g, sgu_ln_b, sgu_w3, sgu_b_t, ln1_g, ln1_b,
        [w_gate_up[0].T.astype(BF16), w_down[0].astype(BF16)], seq)
    dr1, dr1_bf, h_bf, dr2_bf, a_bf, dgu_bf, stats = _ffn_fwd_bwd(
        xhat1, rstd1, tgt2d, wgu_t, wdown, ln1_g, ln1_b, ln2_g, ln2_b)

    kept_gu, sib_gu = _wgrad_exchange(dgu_bf, h_bf, 1, "wgrad_gate_up")
    kept_dn, sib_dn = _wgrad_exchange(a_bf, dr2_bf, 2, "wgrad_down")
    kept_out, sib_out = _wgrad_exchange(mix_bf, dr1_bf, N_CHIPS, "wgrad_out")
    bf_gu, bf_dn, bf_out, own_gu, own_dn, own_out = _chip_sums(
        [kept_gu, kept_dn, kept_out], [sib_gu, sib_dn, sib_out], place, "chip_sums")
    gx, dproj_bf, d_pw, d_ps, d_lg, d_lb, d_sw, d_sb_t, rec_gu, rec_dn, rec_out = _mixer_bwd(
        dr1, proj, win_t, wout, pool_w3, pool_scale, sgu_ln_g, sgu_ln_b, sgu_w3, sgu_b_t, [bf_gu, bf_dn, bf_out], seq)
    kept_in, sib_in = _wgrad_exchange(dproj_bf, x_bf, N_CHIPS, "wgrad_in")
    bf_in, own_in = _chip_sums([kept_in], [sib_in], place, "chip_sum_w_in")
    rec_in, = _send_to_owners([bf_in], "grad_scatter_w_in")

    names = ["w_in", "w_out", "w_gate_up", "w_down"]
    g_rows = [_sum_partials(o, r, "grad_sum_" + nm)
              for o, r, nm in zip([own_in, own_out, own_gu, own_dn], [rec_in, rec_out, rec_gu, rec_dn], names)]
    g_big = {"w_in": g_rows[0].T, "w_out": g_rows[1], "w_gate_up": g_rows[2].T, "w_down": g_rows[3]}
    big_w = {"w_in": w_in, "w_out": w_out, "w_gate_up": w_gate_up, "w_down": w_down}
    big_m = {"w_in": m_w_in, "w_out": m_w_out, "w_gate_up": m_w_gate_up, "w_down": m_w_down}
    big_v = {"w_in": v_w_in, "w_out": v_w_out, "w_gate_up": v_w_gate_up, "w_down": v_w_down}
    grads, deltas, new_m, new_v = {}, {}, {}, {}
    for nm in names:
        d, mo, vo = _adamw(big_w[nm][0], g_big[nm], big_m[nm][0], big_v[nm][0], "adamw_" + nm)
        grads[nm], deltas[nm], new_m[nm], new_v[nm] = g_big[nm][None], d[None], mo[None], vo[None]

    small_w = {"pool_w": pool_w, "pool_scale": pool_scale, "sgu_ln_g": sgu_ln_g, "sgu_ln_b": sgu_ln_b, "sgu_w": sgu_w,
               "sgu_b": sgu_b, "ln1_g": ln1_g, "ln1_b": ln1_b, "ln2_g": ln2_g, "ln2_b": ln2_b}
    small_m = {"pool_w": m_pool_w, "pool_scale": m_pool_scale, "sgu_ln_g": m_sgu_ln_g, "sgu_ln_b": m_sgu_ln_b,
               "sgu_w": m_sgu_w, "sgu_b": m_sgu_b, "ln1_g": m_ln1_g, "ln1_b": m_ln1_b, "ln2_g": m_ln2_g, "ln2_b": m_ln2_b}
    small_v = {"pool_w": v_pool_w, "pool_scale": v_pool_scale, "sgu_ln_g": v_sgu_ln_g, "sgu_ln_b": v_sgu_ln_b,
               "sgu_w": v_sgu_w, "sgu_b": v_sgu_b, "ln1_g": v_ln1_g, "ln1_b": v_ln1_b, "ln2_g": v_ln2_g, "ln2_b": v_ln2_b}
    small_g = {"pool_w": d_pw, "sgu_w": d_sw, "pool_scale": d_ps, "sgu_ln_g": d_lg, "sgu_ln_b": d_lb, "sgu_b": d_sb_t.T,
               "ln1_g": stats[2], "ln1_b": stats[3], "ln2_g": stats[0], "ln2_b": stats[1]}
    part = jnp.concatenate([_pack_small(small_g), stats[4].reshape(8, GROUP)], axis=0)
    g_pk, loss_blk, d_pk, m_pk, v_pk = _small_allreduce_adamw(
        part, _pack_small(small_w), _pack_small(small_m), _pack_small(small_v))
    shapes = {k: a.shape for k, a in small_w.items()}
    for pk, dst in ((g_pk, grads), (d_pk, deltas), (m_pk, new_m), (v_pk, new_v)):
        dst.update(_unpack_small(pk, shapes))

    order = ["w_in", "pool_w", "pool_scale", "sgu_ln_g", "sgu_ln_b", "sgu_w", "sgu_b", "w_out", "ln1_g", "ln1_b",
             "w_gate_up", "w_down", "ln2_g", "ln2_b"]
    return (loss_blk[0, 0], gx.reshape(bl, seq, D_MODEL), *[grads[k] for k in order], *[deltas[k] for k in order],
            *[new_m[k] for k in order], *[new_v[k] for k in order])
```

```python
import functools

import jax
import jax.numpy as jnp
from jax import lax
from jax.experimental import pallas as pl
from jax.experimental.pallas import tpu as pltpu

F32 = jnp.float32
BF16 = jnp.bfloat16
MESH = pl.DeviceIdType.MESH

D_MODEL = 1024
POOL_WIDTH = 512
SGU_WIDTH = 512
POOL_WINDOWS = (2, 4, 8, 16)
GROUP = 128
N_HEADS = 4
IN_COLS = POOL_WIDTH + 2 * SGU_WIDTH
D_FF = 2816
FF_CHUNK = 1408
LN_EPS = 1e-5
ALPHA = float(2.0 ** 0.25)
HALO = 16
N_DEV = 8
N_CHIPS = 4

ADAM_LR = 0.001
ADAM_B1 = 0.9
ADAM_B2 = 0.999
ADAM_EPS = 1e-08
ADAM_WD = 0.01
ADAM_STEP = 10

VMEM_LIMIT_BYTES = 56 * 1024 * 1024

_SQRT_HALF = 0.7071067811865476
_INV_SQRT_2PI = 0.3989422804014327


def _dot_nn(a, b):
    return lax.dot_general(a, b, (((1,), (0,)), ((), ())), preferred_element_type=F32)


def _dot_nt(a, b):
    return lax.dot_general(a, b, (((1,), (1,)), ((), ())), preferred_element_type=F32)


def _dot_tn(a, b):
    return lax.dot_general(a, b, (((0,), (0,)), ((), ())), preferred_element_type=F32)


def _gelu(x):
    return 0.5 * x * (1.0 + lax.erf(x * _SQRT_HALF))


def _gelu_grad(x):
    return 0.5 * (1.0 + lax.erf(x * _SQRT_HALF)) + x * jnp.exp(-0.5 * x * x) * _INV_SQRT_2PI


def _ln_stats(r):
    mu = jnp.mean(r, axis=-1, keepdims=True)
    d = r - mu
    var = jnp.mean(d * d, axis=-1, keepdims=True)
    rstd = lax.rsqrt(var + LN_EPS)
    return d * rstd, rstd


def _ln_bwd(dout, xhat, rstd, g):
    dxh = dout * g
    m1 = jnp.mean(dxh, axis=-1, keepdims=True)
    m2 = jnp.mean(dxh * xhat, axis=-1, keepdims=True)
    return rstd * (dxh - m1 - xhat * m2)


def _rowsum(a):
    return jnp.sum(a, axis=0, keepdims=True)


def _pool_fwd(xp, xp_prev, inv_cnt, w):
    s = jnp.concatenate([xp_prev, xp], axis=0)
    k = 1
    while k < w:
        s = s + pltpu.roll(s, k, 0)
        k *= 2
    return s[HALO:, :] * inv_cnt - xp


def _pool_bwd(dpooled, dpooled_next, inv_cnt, inv_cnt_next, w):
    n = dpooled.shape[0] + HALO
    s = jnp.concatenate([dpooled * inv_cnt, dpooled_next * inv_cnt_next], axis=0)
    k = 1
    while k < w:
        s = s + pltpu.roll(s, n - k, 0)
        k *= 2
    return s[: dpooled.shape[0], :] - dpooled


def _inv_count(pos, w):
    return 1.0 / jnp.minimum(pos + 1, w).astype(F32)


def _to_head_major(a, h, nc):
    return jnp.concatenate(
        [a[c * GROUP:(c + 1) * GROUP, h * GROUP:(h + 1) * GROUP] for c in range(nc)], axis=1)


def _masked_sgu_w(sw_ref, h):
    row = lax.broadcasted_iota(jnp.int32, (GROUP, GROUP), 0)
    col = lax.broadcasted_iota(jnp.int32, (GROUP, GROUP), 1)
    return jnp.where(row >= col, sw_ref[h], 0.0)


def _row_block(rows, limit):
    return max(b for b in range(16, min(rows, limit) + 1, 16) if rows % b == 0)


def _mesh_position():
    return lax.axis_index("x"), lax.axis_index("y"), lax.axis_index("c")


def _other_chips(x, y):
    return [(1 - x, y), (x, 1 - y), (1 - x, 1 - y)]


class _TwoLevelGather:
    def __init__(self, ins, outs, send_sems, recv_sems, local_sems):
        self.ins, self.outs = ins, outs
        self.send_sems, self.recv_sems, self.local_sems = send_sems, recv_sems, local_sems
        self.na = len(ins)
        x, y, c = _mesh_position()
        self.c = c
        self.me, self.sibling = (x, y, c), (x, y, 1 - c)
        self.chips = _other_chips(x, y)

    def _rows(self, a, px, py, pc):
        n = self.ins[a].shape[0]
        return self.outs[a].at[pl.ds((4 * px + 2 * py + pc) * n, n), :]

    def _copy(self, a, k, block, to, src=None):
        return pltpu.make_async_remote_copy(
            src_ref=self._rows(a, *block) if src is None else src, dst_ref=self._rows(a, *block),
            send_sem=self.send_sems.at[a * 7 + k], recv_sem=self.recv_sems.at[a * 7 + k],
            device_id=to, device_id_type=MESH)

    def _mine(self, a):
        return pltpu.make_async_copy(self.ins[a], self._rows(a, *self.me), self.local_sems.at[a])

    def start(self):
        for a in range(self.na):
            self._mine(a).start()
        for a in range(self.na):
            self._copy(a, 0, self.me, self.sibling, src=self.ins[a]).start()
            for j, chip in enumerate(self.chips):
                self._copy(a, 1 + j, self.me, (*chip, self.c), src=self.ins[a]).start()

    def pass_on(self, a):
        for j, chip in enumerate(self.chips):
            self._copy(a, 1 + j, (*chip, self.c), self.me).wait_recv()
            self._copy(a, 4 + j, (*chip, self.c), self.sibling).start()

    def finish(self):
        for a in range(self.na):
            self._copy(a, 0, self.sibling, self.me).wait_recv()
            for j, chip in enumerate(self.chips):
                self._copy(a, 4 + j, (*chip, 1 - self.c), self.me).wait_recv()
        for a in range(self.na):
            for k in range(7):
                self._copy(a, k, self.me, self.sibling, src=self.ins[a]).wait_send()
            self._mine(a).wait()

    @staticmethod
    def scratch(na):
        return [pltpu.SemaphoreType.DMA((7 * na,)), pltpu.SemaphoreType.DMA((7 * na,)), pltpu.SemaphoreType.DMA((na,))]


def _gathered_shape(s):
    return jax.ShapeDtypeStruct((N_DEV * s.shape[0], s.shape[1]), s.dtype)


def _all_gather_rows(shards, name):
    na = len(shards)

    def body(*refs):
        gather = _TwoLevelGather(refs[:na], refs[na:2 * na], *refs[2 * na:])
        gather.start()
        for a in range(na):
            gather.pass_on(a)
        gather.finish()

    any_spec = pl.BlockSpec(memory_space=pl.ANY)
    return pl.pallas_call(
        body, name=name, out_shape=[_gathered_shape(s) for s in shards],
        in_specs=[any_spec] * na, out_specs=[any_spec] * na, scratch_shapes=_TwoLevelGather.scratch(na),
    )(*shards)


def _mixer_fwd(x2d, win_t, wout, pool_w, pool_scale, sln_g, sln_b, sgu_w, sgu_b_t, ln1_g, ln1_b, later_shards, seq):
    tokens = x2d.shape[0]
    tt = min(512, seq)
    tiles_per_seq = seq // tt
    nc = tt // GROUP
    n_tiles = tokens // tt
    n_later = len(later_shards)

    def body(x_ref, xh_ref, win_ref, wout_ref, pw_ref, ps_ref, lg_ref, lb_ref, sw_ref, sb_ref, g1_ref, b1_ref, *rest):
        shard_refs, rest = rest[:n_later], rest[n_later:]
        proj_ref, xhat_ref, rstd_ref, mix_ref, xbf_ref = rest[:5]
        gathered_refs, rest = rest[5:5 + n_later], rest[5 + n_later:]
        mix_scr, send_sems, recv_sems, local_sems = rest
        i = pl.program_id(0)
        gather = _TwoLevelGather(shard_refs, gathered_refs, send_sems, recv_sems, local_sems)

        @pl.when(i == 0)
        def _():
            gather.start()

        tile_in_seq = i % tiles_per_seq
        x = x_ref[...]
        xb = x.astype(BF16)
        xbf_ref[...] = xb
        proj = _dot_nt(xb, win_ref[...])
        proj_ref[...] = proj
        xp_prev = _dot_nt(xh_ref[...].astype(BF16), win_ref[0:POOL_WIDTH, :])
        xp_prev = jnp.where(tile_in_seq == 0, 0.0, xp_prev)
        pos = tile_in_seq * tt + lax.broadcasted_iota(jnp.int32, (tt, 1), 0)
        for g, w in enumerate(POOL_WINDOWS):
            sl = slice(g * GROUP, (g + 1) * GROUP)
            pooled = _pool_fwd(proj[:, sl], xp_prev[:, sl], _inv_count(pos, w), w)
            pre = _dot_nn(pooled.astype(BF16), pw_ref[g].astype(BF16))
            mix_scr[:, sl] = pre * ps_ref[:, sl]
        u = _gelu(proj[:, POOL_WIDTH:POOL_WIDTH + SGU_WIDTH])
        v = _gelu(proj[:, POOL_WIDTH + SGU_WIDTH:])
        vhat, _ = _ln_stats(v)
        v_ln = vhat * lg_ref[...] + lb_ref[...]
        for h in range(N_HEADS):
            ws = _masked_sgu_w(sw_ref, h).astype(BF16)
            mixed = _dot_nn(ws, _to_head_major(v_ln, h, nc).astype(BF16)) + sb_ref[:, h:h + 1]
            for c in range(nc):
                rs = slice(c * GROUP, (c + 1) * GROUP)
                mix_scr[rs, POOL_WIDTH + h * GROUP:POOL_WIDTH + (h + 1) * GROUP] = (
                    u[rs, h * GROUP:(h + 1) * GROUP] * mixed[:, c * GROUP:(c + 1) * GROUP])
        mixb = mix_scr[...].astype(BF16)
        mix_ref[...] = mixb
        r1 = ALPHA * x + _dot_nn(mixb, wout_ref[...])
        xhat, rstd = _ln_stats(r1)
        xhat_ref[...] = xhat
        rstd_ref[...] = rstd

        for a in range(n_later):
            @pl.when(i == max(n_tiles - n_later + a, 0))
            def _(a=a):
                gather.pass_on(a)

        @pl.when(i == n_tiles - 1)
        def _():
            gather.finish()

    def tile(cols):
        return pl.BlockSpec((tt, cols), lambda i: (i, 0))

    def whole(a):
        nd = a.ndim
        return pl.BlockSpec(a.shape, lambda i: (0,) * nd)

    any_spec = pl.BlockSpec(memory_space=pl.ANY)
    halo = pl.BlockSpec((HALO, D_MODEL), lambda i: (jnp.maximum(i * (tt // HALO) - 1, 0), 0))
    consts = [win_t, wout, pool_w, pool_scale, sln_g, sln_b, sgu_w, sgu_b_t, ln1_g, ln1_b]
    return pl.pallas_call(
        body, name="mixer_fwd", grid=(n_tiles,),
        in_specs=[tile(D_MODEL), halo] + [whole(a) for a in consts] + [any_spec] * n_later,
        out_specs=[tile(IN_COLS), tile(D_MODEL), tile(1), tile(D_MODEL), tile(D_MODEL)] + [any_spec] * n_later,
        out_shape=[jax.ShapeDtypeStruct((tokens, IN_COLS), F32), jax.ShapeDtypeStruct((tokens, D_MODEL), F32),
                   jax.ShapeDtypeStruct((tokens, 1), F32), jax.ShapeDtypeStruct((tokens, D_MODEL), BF16),
                   jax.ShapeDtypeStruct((tokens, D_MODEL), BF16)] + [_gathered_shape(s) for s in later_shards],
        scratch_shapes=[pltpu.VMEM((tt, D_MODEL), F32)] + _TwoLevelGather.scratch(n_later),
        compiler_params=pltpu.CompilerParams(dimension_semantics=("arbitrary",), vmem_limit_bytes=VMEM_LIMIT_BYTES),
    )(x2d, x2d, *consts, *later_shards)


def _ffn_fwd_bwd(xhat1, rstd1, target, wgu_t, wdown, ln1_g, ln1_b, ln2_g, ln2_b):
    tokens = xhat1.shape[0]
    tt = min(256, tokens)
    n_chunks = D_FF // FF_CHUNK

    def body(xhat_ref, rstd_ref, tgt_ref, wgu_hbm, wd_hbm, g1_ref, b1_ref, g2_ref, b2_ref,
             dr1_ref, dr1bf_ref, hbf_ref, dr2bf_ref, a_ref, dgu_ref, stats_ref, wgu_ref, wd_ref, gu_scr, sems):
        i = pl.program_id(0)

        @pl.when(i == 0)
        def _():
            loads = [pltpu.make_async_copy(wgu_hbm, wgu_ref, sems.at[0]),
                     pltpu.make_async_copy(wd_hbm, wd_ref, sems.at[1])]
            for cp in loads:
                cp.start()
            stats_ref[...] = jnp.zeros_like(stats_ref)
            for cp in loads:
                cp.wait()

        xhat1_t = xhat_ref[...]
        h = xhat1_t * g1_ref[...] + b1_ref[...]
        hb = h.astype(BF16)
        hbf_ref[...] = hb
        f = jnp.zeros((tt, D_MODEL), F32)
        for k in range(n_chunks):
            rg = slice(k * FF_CHUNK, (k + 1) * FF_CHUNK)
            ru = slice(D_FF + k * FF_CHUNK, D_FF + (k + 1) * FF_CHUNK)
            gate = _dot_nt(hb, wgu_ref[rg, :])
            up = _dot_nt(hb, wgu_ref[ru, :])
            gu_scr[:, rg] = gate
            gu_scr[:, ru] = up
            ab = (gate * jax.nn.sigmoid(gate) * up).astype(BF16)
            a_ref[:, rg] = ab
            f = f + _dot_nn(ab, wd_ref[rg, :])
        xhat2, rstd2 = _ln_stats(ALPHA * h + f)
        err = xhat2 * g2_ref[...] + b2_ref[...] - tgt_ref[...]
        dy = err * (1.0 / D_MODEL)
        stats_ref[0:1, :] += _rowsum(dy * xhat2)
        stats_ref[1:2, :] += _rowsum(dy)
        stats_ref[4:5, :] += _rowsum(err * err)
        dr2 = _ln_bwd(dy, xhat2, rstd2, g2_ref[...])
        dr2b = dr2.astype(BF16)
        dr2bf_ref[...] = dr2b
        dh = ALPHA * dr2
        for k in range(n_chunks):
            rg = slice(k * FF_CHUNK, (k + 1) * FF_CHUNK)
            ru = slice(D_FF + k * FF_CHUNK, D_FF + (k + 1) * FF_CHUNK)
            da = _dot_nt(dr2b, wd_ref[rg, :])
            gate = gu_scr[:, rg]
            up = gu_scr[:, ru]
            sg = jax.nn.sigmoid(gate)
            dgate = (da * up * (sg * (1.0 + gate * (1.0 - sg)))).astype(BF16)
            dup = (da * (gate * sg)).astype(BF16)
            dgu_ref[:, rg] = dgate
            dgu_ref[:, ru] = dup
            dh = dh + _dot_nn(dgate, wgu_ref[rg, :]) + _dot_nn(dup, wgu_ref[ru, :])
        stats_ref[2:3, :] += _rowsum(dh * xhat1_t)
        stats_ref[3:4, :] += _rowsum(dh)
        dr1 = _ln_bwd(dh, xhat1_t, rstd_ref[...], g1_ref[...])
        dr1_ref[...] = dr1
        dr1bf_ref[...] = dr1.astype(BF16)

    def tile(cols):
        return pl.BlockSpec((tt, cols), lambda i: (i, 0))

    def whole(a):
        nd = a.ndim
        return pl.BlockSpec(a.shape, lambda i: (0,) * nd)

    any_spec = pl.BlockSpec(memory_space=pl.ANY)
    vecs = [ln1_g, ln1_b, ln2_g, ln2_b]
    return pl.pallas_call(
        body, name="ffn_fwd_bwd", grid=(tokens // tt,),
        in_specs=[tile(D_MODEL), tile(1), tile(D_MODEL), any_spec, any_spec] + [whole(a) for a in vecs],
        out_specs=[tile(D_MODEL), tile(D_MODEL), tile(D_MODEL), tile(D_MODEL), tile(D_FF), tile(2 * D_FF),
                   pl.BlockSpec((8, D_MODEL), lambda i: (0, 0))],
        out_shape=[jax.ShapeDtypeStruct((tokens, D_MODEL), F32), jax.ShapeDtypeStruct((tokens, D_MODEL), BF16),
                   jax.ShapeDtypeStruct((tokens, D_MODEL), BF16),
                   jax.ShapeDtypeStruct((tokens, D_MODEL), BF16), jax.ShapeDtypeStruct((tokens, D_FF), BF16),
                   jax.ShapeDtypeStruct((tokens, 2 * D_FF), BF16), jax.ShapeDtypeStruct((8, D_MODEL), F32)],
        scratch_shapes=[pltpu.VMEM(wgu_t.shape, BF16), pltpu.VMEM(wdown.shape, BF16),
                        pltpu.VMEM((tt, 2 * D_FF), F32), pltpu.SemaphoreType.DMA((2,))],
        compiler_params=pltpu.CompilerParams(dimension_semantics=("arbitrary",), vmem_limit_bytes=VMEM_LIMIT_BYTES),
    )(xhat1, rstd1, target, wgu_t, wdown, *vecs)


def _wgrad_exchange(lhs, rhs, chips_per_block, name, gather_rows=()):
    tokens, n_all = lhs.shape
    m = rhs.shape[1]
    n = n_all // N_DEV
    tw = min(1024, tokens)
    nt = tokens // tw
    cpb = chips_per_block
    nj = N_CHIPS // cpb
    ng = len(gather_rows)

    def body(l_ref, r_ref, *rest):
        small_refs, rest = rest[:ng], rest[ng:]
        kept_ref, sib_ref = rest[:2]
        gathered_refs, rest = rest[2:2 + ng], rest[2 + ng:]
        acc, sendbuf, send_sems, recv_sems = rest[:4]
        j, t = pl.program_id(0), pl.program_id(1)
        first, last = (j == 0) & (t == 0), (j == nj - 1) & (t == nt - 1)
        x, y, c = _mesh_position()
        if ng:
            gather = _TwoLevelGather(small_refs, gathered_refs, *rest[4:])

            @pl.when(first)
            def _():
                gather.start()

        def copy(q):
            return pltpu.make_async_remote_copy(
                src_ref=sendbuf.at[q], dst_ref=sib_ref.at[q], send_sem=send_sems.at[q], recv_sem=recv_sems.at[q],
                device_id=(x, y, 1 - c), device_id_type=MESH)

        @pl.when(t == 0)
        def _():
            acc[...] = jnp.zeros_like(acc)

        acc[...] += _dot_tn(l_ref[...], r_ref[...])

        @pl.when(t == nt - 1)
        def _():
            for qq in range(cpb):
                q = j * cpb + qq
                kept_ref[qq] = acc[pl.ds(pl.multiple_of(qq * 2 * n + c * n, 8), n), :]
                sendbuf[q] = acc[pl.ds(pl.multiple_of(qq * 2 * n + (1 - c) * n, 8), n), :]
                copy(q).start()

        if ng:
            @pl.when((j == nj - 1) & (t == max(nt - 2, 0)))
            def _():
                for a in range(ng):
                    gather.pass_on(a)

        @pl.when(last)
        def _():
            if ng:
                gather.finish()
            for q in range(N_CHIPS):
                copy(q).wait_send()
                copy(q).wait_recv()

    shard4 = jax.ShapeDtypeStruct((N_CHIPS, n, m), F32)
    any_spec = pl.BlockSpec(memory_space=pl.ANY)
    return pl.pallas_call(
        body, name=name, grid=(nj, nt),
        in_specs=[pl.BlockSpec((tw, 2 * n * cpb), lambda j, t: (t, j)), pl.BlockSpec((tw, m), lambda j, t: (t, 0))]
                 + [any_spec] * ng,
        out_specs=[pl.BlockSpec((cpb, n, m), lambda j, t: (j, 0, 0)), any_spec] + [any_spec] * ng,
        out_shape=[shard4, shard4] + [_gathered_shape(s) for s in gather_rows],
        scratch_shapes=[pltpu.VMEM((2 * n * cpb, m), F32), pltpu.VMEM((N_CHIPS, n, m), F32),
                        pltpu.SemaphoreType.DMA((N_CHIPS,)), pltpu.SemaphoreType.DMA((N_CHIPS,))]
                       + (_TwoLevelGather.scratch(ng) if ng else []),
        compiler_params=pltpu.CompilerParams(dimension_semantics=("arbitrary", "arbitrary"),
                                             vmem_limit_bytes=VMEM_LIMIT_BYTES),
    )(lhs, rhs, *gather_rows)


class _ScatterToOwners:
    def __init__(self, ins, outs, send_sems, recv_sems):
        self.ins, self.outs, self.send_sems, self.recv_sems = ins, outs, send_sems, recv_sems
        x, y, c = _mesh_position()
        self.c, self.chips = c, _other_chips(x, y)

    def _copies(self):
        return [pltpu.make_async_remote_copy(
            src_ref=self.ins[a].at[2 * cx + cy], dst_ref=self.outs[a].at[j],
            send_sem=self.send_sems.at[a * 3 + j], recv_sem=self.recv_sems.at[a * 3 + j],
            device_id=(cx, cy, self.c), device_id_type=MESH)
            for a in range(len(self.ins)) for j, (cx, cy) in enumerate(self.chips)]

    def start(self):
        for cp in self._copies():
            cp.start()

    def finish(self):
        for cp in self._copies():
            cp.wait_send()
            cp.wait_recv()

    @staticmethod
    def scratch(na):
        return [pltpu.SemaphoreType.DMA((3 * na,)), pltpu.SemaphoreType.DMA((3 * na,))]


def _received_shape(p):
    return jax.ShapeDtypeStruct((3,) + p.shape[1:], p.dtype)


ROW_POOL_SCALE, ROW_SLN_G, ROW_SLN_B, ROW_SGU_B = 0, 1, 2, 3
ROW_LN2_G, ROW_LN2_B, ROW_LN1_G, ROW_LN1_B, ROW_LOSS = 8, 9, 10, 11, 12
VEC_ROWS = 16


def _mixer_bwd(dr1_bf, proj, wout, pool_w, pool_scale, sln_g, sln_b, sgu_w, sgu_b_t, stats, chip_partials, seq):
    tokens = dr1_bf.shape[0]
    tt = min(512, seq)
    tiles_per_seq = seq // tt
    nc = tt // GROUP
    n_halo_blocks = tokens // HALO
    n_tiles = tokens // tt
    n_part = len(chip_partials)

    def body(dr1_ref, dr1n_ref, proj_ref, projh_ref, wout_ref, pw_ref, ps_ref, lg_ref, lb_ref, sw_ref, sb_ref, stats_ref,
             *rest):
        part_refs, rest = rest[:n_part], rest[n_part:]
        dproj_ref, dmat_ref, dvec_ref = rest[:3]
        recv_refs, rest = rest[3:3 + n_part], rest[3 + n_part:]
        du_scr, dv_scr, send_sems, recv_sems = rest
        i = pl.program_id(0)
        tile_in_seq = i % tiles_per_seq
        scatter = _ScatterToOwners(part_refs, recv_refs, send_sems, recv_sems)

        @pl.when(i == 0)
        def _():
            scatter.start()
            dmat_ref[...] = jnp.zeros_like(dmat_ref)
            dvec_ref[0:8, :] = jnp.zeros((8, D_MODEL), F32)
            dvec_ref[8:16, :] = stats_ref[...]

        dr1b = dr1_ref[...]
        dmix = _dot_nt(dr1b, wout_ref[...])
        dpo_next = _dot_nt(dr1n_ref[...], wout_ref[0:POOL_WIDTH, :])
        dpo_next = jnp.where(tile_in_seq == tiles_per_seq - 1, 0.0, dpo_next)
        proj = proj_ref[...]
        xp_prev = jnp.where(tile_in_seq == 0, 0.0, projh_ref[...])
        pos = tile_in_seq * tt + lax.broadcasted_iota(jnp.int32, (tt, 1), 0)
        pos_next = (tile_in_seq + 1) * tt + lax.broadcasted_iota(jnp.int32, (HALO, 1), 0)

        for g, w in enumerate(POOL_WINDOWS):
            sl = slice(g * GROUP, (g + 1) * GROUP)
            inv_cnt = _inv_count(pos, w)
            pwb = pw_ref[g].astype(BF16)
            pooledb = _pool_fwd(proj[:, sl], xp_prev[:, sl], inv_cnt, w).astype(BF16)
            pre = _dot_nn(pooledb, pwb)
            dpo = dmix[:, sl]
            dvec_ref[ROW_POOL_SCALE:ROW_POOL_SCALE + 1, sl] += _rowsum(dpo * pre)
            dsb = (dpo * ps_ref[:, sl]).astype(BF16)
            dmat_ref[g] += _dot_tn(pooledb, dsb)
            dpooled = _dot_nt(dsb, pwb)
            dpooled_next = _dot_nt((dpo_next[:, sl] * ps_ref[:, sl]).astype(BF16), pwb)
            dxp = _pool_bwd(dpooled, dpooled_next, inv_cnt, _inv_count(pos_next, w), w)
            dproj_ref[:, sl] = dxp.astype(BF16)

        zu = proj[:, POOL_WIDTH:POOL_WIDTH + SGU_WIDTH]
        zv = proj[:, POOL_WIDTH + SGU_WIDTH:]
        u = _gelu(zu)
        vhat, rstd_v = _ln_stats(_gelu(zv))
        v_ln = vhat * lg_ref[...] + lb_ref[...]
        dsg = dmix[:, POOL_WIDTH:]
        row = lax.broadcasted_iota(jnp.int32, (GROUP, GROUP), 0)
        col = lax.broadcasted_iota(jnp.int32, (GROUP, GROUP), 1)
        for h in range(N_HEADS):
            ws = _masked_sgu_w(sw_ref, h).astype(BF16)
            vh = _to_head_major(v_ln, h, nc).astype(BF16)
            mixed = _dot_nn(ws, vh) + sb_ref[:, h:h + 1]
            dsg_h = _to_head_major(dsg, h, nc)
            du_h = dsg_h * mixed
            dm_h = dsg_h * _to_head_major(u, h, nc)
            pos_sums = lax.dot_general(jnp.ones((8, nc * GROUP), F32), dm_h, (((1,), (1,)), ((), ())),
                                       precision=lax.Precision.HIGHEST, preferred_element_type=F32)
            dvec_ref[ROW_SGU_B + h:ROW_SGU_B + h + 1, 0:GROUP] += pos_sums[0:1, :]
            dmb = dm_h.astype(BF16)
            dmat_ref[len(POOL_WINDOWS) + h] += jnp.where(row >= col, _dot_nt(dmb, vh), 0.0)
            dv_h = _dot_tn(ws, dmb)
            for c in range(nc):
                rs = slice(c * GROUP, (c + 1) * GROUP)
                cs = slice(h * GROUP, (h + 1) * GROUP)
                du_scr[rs, cs] = du_h[:, c * GROUP:(c + 1) * GROUP]
                dv_scr[rs, cs] = dv_h[:, c * GROUP:(c + 1) * GROUP]
        dv_ln = dv_scr[...]
        dvec_ref[ROW_SLN_B:ROW_SLN_B + 1, 0:SGU_WIDTH] += _rowsum(dv_ln)
        dvec_ref[ROW_SLN_G:ROW_SLN_G + 1, 0:SGU_WIDTH] += _rowsum(dv_ln * vhat)
        dv = _ln_bwd(dv_ln, vhat, rstd_v, lg_ref[...])
        dproj_ref[:, POOL_WIDTH:POOL_WIDTH + SGU_WIDTH] = (du_scr[...] * _gelu_grad(zu)).astype(BF16)
        dproj_ref[:, POOL_WIDTH + SGU_WIDTH:] = (dv * _gelu_grad(zv)).astype(BF16)

        @pl.when(i == n_tiles - 1)
        def _():
            scatter.finish()

    def tile(cols):
        return pl.BlockSpec((tt, cols), lambda i: (i, 0))

    def whole(a):
        nd = a.ndim
        return pl.BlockSpec(a.shape, lambda i: (0,) * nd)

    def resident(shape):
        nd = len(shape)
        return pl.BlockSpec(shape, lambda i: (0,) * nd)

    any_spec = pl.BlockSpec(memory_space=pl.ANY)
    next_halo = pl.BlockSpec((HALO, D_MODEL), lambda i: (jnp.minimum((i + 1) * (tt // HALO), n_halo_blocks - 1), 0))
    prev_halo = pl.BlockSpec((HALO, POOL_WIDTH), lambda i: (jnp.maximum(i * (tt // HALO) - 1, 0), 0))
    consts = [wout, pool_w, pool_scale, sln_g, sln_b, sgu_w, sgu_b_t, stats]
    small_shapes = [(len(POOL_WINDOWS) + N_HEADS, GROUP, GROUP), (VEC_ROWS, D_MODEL)]
    return pl.pallas_call(
        body, name="mixer_bwd", grid=(n_tiles,),
        in_specs=[tile(D_MODEL), next_halo, tile(IN_COLS), prev_halo] + [whole(a) for a in consts]
                 + [any_spec] * n_part,
        out_specs=[tile(IN_COLS)] + [resident(s) for s in small_shapes] + [any_spec] * n_part,
        out_shape=[jax.ShapeDtypeStruct((tokens, IN_COLS), BF16)]
                  + [jax.ShapeDtypeStruct(s, F32) for s in small_shapes] + [_received_shape(p) for p in chip_partials],
        scratch_shapes=[pltpu.VMEM((tt, SGU_WIDTH), F32), pltpu.VMEM((tt, SGU_WIDTH), F32)]
                       + _ScatterToOwners.scratch(n_part),
        compiler_params=pltpu.CompilerParams(dimension_semantics=("arbitrary",), vmem_limit_bytes=VMEM_LIMIT_BYTES),
    )(dr1_bf, dr1_bf, proj, proj, *consts, *chip_partials)


def _in_proj_bwd_x(dr1, dproj_bf, win_t):
    tokens = dr1.shape[0]
    tt = min(512, tokens)

    def body(dr1_ref, dproj_ref, win_ref, gx_ref):
        gx_ref[...] = ALPHA * dr1_ref[...] + _dot_nn(dproj_ref[...], win_ref[...])

    return pl.pallas_call(
        body, name="in_proj_bwd_x", grid=(tokens // tt,),
        in_specs=[pl.BlockSpec((tt, D_MODEL), lambda i: (i, 0)), pl.BlockSpec((tt, IN_COLS), lambda i: (i, 0)),
                  pl.BlockSpec(win_t.shape, lambda i: (0, 0))],
        out_specs=pl.BlockSpec((tt, D_MODEL), lambda i: (i, 0)),
        out_shape=jax.ShapeDtypeStruct((tokens, D_MODEL), F32),
        compiler_params=pltpu.CompilerParams(dimension_semantics=("arbitrary",), vmem_limit_bytes=VMEM_LIMIT_BYTES),
    )(dr1, dproj_bf, win_t)


def _chip_sums(kept, from_sibling, place, name):
    na = len(kept)

    def body(place_ref, *refs):
        kept_refs, sib_refs = refs[:na], refs[na:2 * na]
        bf_refs, own_refs = refs[2 * na:3 * na], refs[3 * na:]
        q = pl.program_id(0)
        for a in range(na):
            s = kept_refs[a][...] + sib_refs[a][...]
            bf_refs[a][...] = s.astype(BF16)

            @pl.when(q == place_ref[1])
            def _(a=a, s=s):
                own_refs[a][...] = s

    by_chip = [pl.BlockSpec((None,) + k.shape[1:], lambda q, pr: (q, 0, 0)) for k in kept]
    grid_spec = pltpu.PrefetchScalarGridSpec(
        num_scalar_prefetch=1, grid=(N_CHIPS,), in_specs=by_chip + by_chip,
        out_specs=by_chip + [pl.BlockSpec(k.shape[1:], lambda q, pr: (0, 0)) for k in kept])
    return pl.pallas_call(
        body, name=name, grid_spec=grid_spec,
        out_shape=[jax.ShapeDtypeStruct(k.shape, BF16) for k in kept]
                  + [jax.ShapeDtypeStruct(k.shape[1:], F32) for k in kept],
        compiler_params=pltpu.CompilerParams(dimension_semantics=("arbitrary",), vmem_limit_bytes=VMEM_LIMIT_BYTES),
    )(place, *kept, *from_sibling)


def _send_to_owners(chip_partials, name):
    na = len(chip_partials)

    def body(*refs):
        scatter = _ScatterToOwners(refs[:na], refs[na:2 * na], *refs[2 * na:])
        scatter.start()
        scatter.finish()

    any_spec = pl.BlockSpec(memory_space=pl.ANY)
    return pl.pallas_call(
        body, name=name, out_shape=[_received_shape(p) for p in chip_partials],
        in_specs=[any_spec] * na, out_specs=[any_spec] * na, scratch_shapes=_ScatterToOwners.scratch(na),
    )(*chip_partials)


def _sum_partials(own, received, name):
    n, cols = own.shape
    rb = _row_block(n, 512)

    def body(own_ref, rec_ref, g_ref):
        g = own_ref[...]
        for j in range(3):
            g = g + rec_ref[j].astype(F32)
        g_ref[...] = g

    return pl.pallas_call(
        body, name=name, grid=(n // rb,),
        in_specs=[pl.BlockSpec((rb, cols), lambda r: (r, 0)), pl.BlockSpec((3, rb, cols), lambda r: (0, r, 0))],
        out_specs=pl.BlockSpec((rb, cols), lambda r: (r, 0)),
        out_shape=jax.ShapeDtypeStruct((n, cols), F32),
        compiler_params=pltpu.CompilerParams(dimension_semantics=("arbitrary",)),
    )(own, received)


def _adamw_math(w, g, m, v):
    m = ADAM_B1 * m + (1.0 - ADAM_B1) * g
    v = ADAM_B2 * v + (1.0 - ADAM_B2) * (g * g)
    m_hat = m / (1.0 - ADAM_B1 ** ADAM_STEP)
    v_hat = v / (1.0 - ADAM_B2 ** ADAM_STEP)
    delta = -ADAM_LR * (m_hat / (jnp.sqrt(v_hat) + ADAM_EPS) + ADAM_WD * w)
    return delta, m, v


def _adamw(w, g, m, v, name):
    rows, cols = w.shape
    rb = _row_block(rows, 256)

    def body(w_ref, g_ref, m_ref, v_ref, d_ref, mo_ref, vo_ref):
        d_ref[...], mo_ref[...], vo_ref[...] = _adamw_math(w_ref[...], g_ref[...], m_ref[...], v_ref[...])

    spec = pl.BlockSpec((rb, cols), lambda r: (r, 0))
    return pl.pallas_call(
        body, name=name, grid=(rows // rb,),
        in_specs=[spec] * 4, out_specs=[spec] * 3,
        out_shape=[jax.ShapeDtypeStruct((rows, cols), F32)] * 3,
        compiler_params=pltpu.CompilerParams(dimension_semantics=("arbitrary",)),
    )(w, g, m, v)


def _sum_adamw(w, m, v, own, received, name):
    rows, cols = w.shape
    rb = _row_block(rows, 256)

    def body(w_ref, m_ref, v_ref, own_ref, rec_ref, g_ref, d_ref, mo_ref, vo_ref):
        g = own_ref[...]
        for j in range(3):
            g = g + rec_ref[j].astype(F32)
        g_ref[...] = g
        d_ref[...], mo_ref[...], vo_ref[...] = _adamw_math(w_ref[...], g, m_ref[...], v_ref[...])

    spec = pl.BlockSpec((rb, cols), lambda r: (r, 0))
    return pl.pallas_call(
        body, name=name, grid=(rows // rb,),
        in_specs=[spec] * 4 + [pl.BlockSpec((3, rb, cols), lambda r: (0, r, 0))], out_specs=[spec] * 4,
        out_shape=[jax.ShapeDtypeStruct((rows, cols), F32)] * 4,
        compiler_params=pltpu.CompilerParams(dimension_semantics=("arbitrary",)),
    )(w, m, v, own, received)


SMALL_NAMES = ("pool_w", "sgu_w", "pool_scale", "sgu_ln_g", "sgu_ln_b", "sgu_b", "ln1_g", "ln1_b", "ln2_g", "ln2_b")
_SMALL_VEC_ROWS = {"pool_scale": (ROW_POOL_SCALE, POOL_WIDTH), "sgu_ln_g": (ROW_SLN_G, SGU_WIDTH),
                   "sgu_ln_b": (ROW_SLN_B, SGU_WIDTH), "ln1_g": (ROW_LN1_G, D_MODEL), "ln1_b": (ROW_LN1_B, D_MODEL),
                   "ln2_g": (ROW_LN2_G, D_MODEL), "ln2_b": (ROW_LN2_B, D_MODEL)}
_SMALL_MAT_FIRST = {"pool_w": 0, "sgu_w": len(POOL_WINDOWS)}


def _small_sum_adamw(mats_all, vecs_all, w, m, v):
    n = len(SMALL_NAMES)

    def body(mats_ref, vecs_ref, *refs):
        w_refs, m_refs, v_refs = refs[:n], refs[n:2 * n], refs[2 * n:3 * n]
        loss_ref = refs[3 * n]
        g_refs, d_refs, mo_refs, vo_refs = (refs[3 * n + 1 + k * n:3 * n + 1 + (k + 1) * n] for k in range(4))
        vec_scr = refs[7 * n + 1]

        def update(k, idx, g):
            d, mo, vo = _adamw_math(w_refs[k][idx], g, m_refs[k][idx], v_refs[k][idx])
            g_refs[k][idx], d_refs[k][idx], mo_refs[k][idx], vo_refs[k][idx] = g, d, mo, vo

        total = vecs_ref[0]
        for dev in range(1, N_DEV):
            total = total + vecs_ref[dev]
        vec_scr[...] = total
        for k, name in enumerate(SMALL_NAMES):
            if name in _SMALL_MAT_FIRST:
                for b in range(4):
                    g = mats_ref[0, _SMALL_MAT_FIRST[name] + b]
                    for dev in range(1, N_DEV):
                        g = g + mats_ref[dev, _SMALL_MAT_FIRST[name] + b]
                    update(k, (0, b), g)
            elif name == "sgu_b":
                update(k, (0,), vec_scr[ROW_SGU_B:ROW_SGU_B + N_HEADS, 0:GROUP])
            else:
                row, width = _SMALL_VEC_ROWS[name]
                update(k, (slice(None), slice(None)), vec_scr[row:row + 1, 0:width])
        loss = jnp.sum(vec_scr[ROW_LOSS:ROW_LOSS + 1, :], axis=1, keepdims=True) * (0.5 / D_MODEL)
        loss_ref[...] = jnp.broadcast_to(loss, loss_ref.shape)

    vmem = pl.BlockSpec(memory_space=pltpu.VMEM)
    shapes = [jax.ShapeDtypeStruct(w[k].shape, F32) for k in SMALL_NAMES]
    outs = pl.pallas_call(
        body, name="small_sum_adamw",
        out_shape=[jax.ShapeDtypeStruct((8, GROUP), F32)] + shapes * 4,
        in_specs=[vmem] * (2 + 3 * n), out_specs=[vmem] * (1 + 4 * n),
        scratch_shapes=[pltpu.VMEM((VEC_ROWS, D_MODEL), F32)],
    )(mats_all, vecs_all, *[w[k] for k in SMALL_NAMES], *[m[k] for k in SMALL_NAMES], *[v[k] for k in SMALL_NAMES])
    return outs[0], outs[1:1 + n], outs[1 + n:1 + 2 * n], outs[1 + 2 * n:1 + 3 * n], outs[1 + 3 * n:]


def kernel(x, w_in, pool_w, pool_scale, sgu_ln_g, sgu_ln_b, sgu_w, sgu_b, w_out, ln1_g, ln1_b, w_gate_up, w_down, ln2_g, ln2_b, loss_target, m_w_in, m_pool_w, m_pool_scale, m_sgu_ln_g, m_sgu_ln_b, m_sgu_w, m_sgu_b, m_w_out, m_ln1_g, m_ln1_b, m_w_gate_up, m_w_down, m_ln2_g, m_ln2_b, v_w_in, v_pool_w, v_pool_scale, v_sgu_ln_g, v_sgu_ln_b, v_sgu_w, v_sgu_b, v_w_out, v_ln1_g, v_ln1_b, v_w_gate_up, v_w_down, v_ln2_g, v_ln2_b):
    bl, seq, _ = x.shape
    tokens = bl * seq
    x2d = x.reshape(tokens, D_MODEL)
    tgt2d = loss_target.reshape(tokens, D_MODEL)
    my_c = lax.axis_index("c")
    place = jnp.stack([my_c, 2 * lax.axis_index("x") + lax.axis_index("y")]).astype(jnp.int32)

    win_t, wout = _all_gather_rows([w_in[0].T.astype(BF16), w_out[0].astype(BF16)], "weight_all_gather")

    pool_w3, sgu_w3 = pool_w[0], sgu_w[0]
    sgu_b_t = sgu_b[0].T
    proj, xhat1, rstd1, mix_bf, x_bf, wgu_t, wdown = _mixer_fwd(
        x2d, win_t, wout, pool_w3, pool_scale, sgu_ln_g, sgu_ln_b, sgu_w3, sgu_b_t, ln1_g, ln1_b,
        [w_gate_up[0].T.astype(BF16), w_down[0].astype(BF16)], seq)
    dr1, dr1_bf, h_bf, dr2_bf, a_bf, dgu_bf, stats = _ffn_fwd_bwd(
        xhat1, rstd1, tgt2d, wgu_t, wdown, ln1_g, ln1_b, ln2_g, ln2_b)

    kept_gu, sib_gu = _wgrad_exchange(dgu_bf, h_bf, 1, "wgrad_gate_up")
    kept_dn, sib_dn = _wgrad_exchange(a_bf, dr2_bf, 2, "wgrad_down")
    kept_out, sib_out = _wgrad_exchange(mix_bf, dr1_bf, N_CHIPS, "wgrad_out")
    bf_gu, bf_dn, bf_out, own_gu, own_dn, own_out = _chip_sums(
        [kept_gu, kept_dn, kept_out], [sib_gu, sib_dn, sib_out], place, "chip_sums")
    dproj_bf, d_mats, d_vecs, rec_gu, rec_dn, rec_out = _mixer_bwd(
        dr1_bf, proj, wout, pool_w3, pool_scale, sgu_ln_g, sgu_ln_b, sgu_w3, sgu_b_t, stats,
        [bf_gu, bf_dn, bf_out], seq)
    gx = _in_proj_bwd_x(dr1, dproj_bf, win_t)
    kept_in, sib_in, mats_all, vecs_all = _wgrad_exchange(
        dproj_bf, x_bf, N_CHIPS, "wgrad_in", gather_rows=[d_mats.reshape(-1, GROUP), d_vecs])
    bf_in, own_in = _chip_sums([kept_in], [sib_in], place, "chip_sum_w_in")
    rec_in, = _send_to_owners([bf_in], "grad_scatter_w_in")

    grads, deltas, new_m, new_v = {}, {}, {}, {}
    for nm, w, m, v, own, rec in (("w_out", w_out, m_w_out, v_w_out, own_out, rec_out),
                                  ("w_down", w_down, m_w_down, v_w_down, own_dn, rec_dn)):
        g, d, mo, vo = _sum_adamw(w[0], m[0], v[0], own, rec, "adamw_" + nm)
        grads[nm], deltas[nm], new_m[nm], new_v[nm] = g[None], d[None], mo[None], vo[None]
    for nm, w, m, v, own, rec in (("w_in", w_in, m_w_in, v_w_in, own_in, rec_in),
                                  ("w_gate_up", w_gate_up, m_w_gate_up, v_w_gate_up, own_gu, rec_gu)):
        g = _sum_partials(own, rec, "grad_sum_" + nm).T
        d, mo, vo = _adamw(w[0], g, m[0], v[0], "adamw_" + nm)
        grads[nm], deltas[nm], new_m[nm], new_v[nm] = g[None], d[None], mo[None], vo[None]

    small_w = {"pool_w": pool_w, "pool_scale": pool_scale, "sgu_ln_g": sgu_ln_g, "sgu_ln_b": sgu_ln_b, "sgu_w": sgu_w,
               "sgu_b": sgu_b, "ln1_g": ln1_g, "ln1_b": ln1_b, "ln2_g": ln2_g, "ln2_b": ln2_b}
    small_m = {"pool_w": m_pool_w, "pool_scale": m_pool_scale, "sgu_ln_g": m_sgu_ln_g, "sgu_ln_b": m_sgu_ln_b,
               "sgu_w": m_sgu_w, "sgu_b": m_sgu_b, "ln1_g": m_ln1_g, "ln1_b": m_ln1_b, "ln2_g": m_ln2_g, "ln2_b": m_ln2_b}
    small_v = {"pool_w": v_pool_w, "pool_scale": v_pool_scale, "sgu_ln_g": v_sgu_ln_g, "sgu_ln_b": v_sgu_ln_b,
               "sgu_w": v_sgu_w, "sgu_b": v_sgu_b, "ln1_g": v_ln1_g, "ln1_b": v_ln1_b, "ln2_g": v_ln2_g, "ln2_b": v_ln2_b}
    loss_blk, g_small, d_small, m_small, v_small = _small_sum_adamw(
        mats_all.reshape(N_DEV, 2 * N_HEADS, GROUP, GROUP), vecs_all.reshape(N_DEV, VEC_ROWS, D_MODEL),
        small_w, small_m, small_v)
    for vals, dst in ((g_small, grads), (d_small, deltas), (m_small, new_m), (v_small, new_v)):
        dst.update(zip(SMALL_NAMES, vals))

    order = ["w_in", "pool_w", "pool_scale", "sgu_ln_g", "sgu_ln_b", "sgu_w", "sgu_b", "w_out", "ln1_g", "ln1_b",
             "w_gate_up", "w_down", "ln2_g", "ln2_b"]
    return (loss_blk[0, 0], gx.reshape(bl, seq, D_MODEL), *[grads[k] for k in order], *[deltas[k] for k in order],
            *[new_m[k] for k in order], *[new_v[k] for k in order])
```

```python
import functools

import jax
import jax.numpy as jnp
from jax import lax
from jax.experimental import pallas as pl
from jax.experimental.pallas import tpu as pltpu

F32 = jnp.float32
BF16 = jnp.bfloat16
MESH = pl.DeviceIdType.MESH

D_MODEL = 1024
POOL_WIDTH = 512
SGU_WIDTH = 512
POOL_WINDOWS = (2, 4, 8, 16)
GROUP = 128
N_HEADS = 4
IN_COLS = POOL_WIDTH + 2 * SGU_WIDTH
D_FF = 2816
FF_CHUNK = 1408
LN_EPS = 1e-5
ALPHA = float(2.0 ** 0.25)
HALO = 16
N_DEV = 8
N_CHIPS = 4

ADAM_LR = 0.001
ADAM_B1 = 0.9
ADAM_B2 = 0.999
ADAM_EPS = 1e-08
ADAM_WD = 0.01
ADAM_STEP = 10

VMEM_LIMIT_BYTES = 56 * 1024 * 1024

_SQRT_HALF = 0.7071067811865476
_INV_SQRT_2PI = 0.3989422804014327


def _dot_nn(a, b):
    return lax.dot_general(a, b, (((1,), (0,)), ((), ())), preferred_element_type=F32)


def _dot_nt(a, b):
    return lax.dot_general(a, b, (((1,), (1,)), ((), ())), preferred_element_type=F32)


def _dot_tn(a, b):
    return lax.dot_general(a, b, (((0,), (0,)), ((), ())), preferred_element_type=F32)


def _gelu(x):
    return 0.5 * x * (1.0 + lax.erf(x * _SQRT_HALF))


def _gelu_grad(x):
    return 0.5 * (1.0 + lax.erf(x * _SQRT_HALF)) + x * jnp.exp(-0.5 * x * x) * _INV_SQRT_2PI


def _ln_stats(r):
    mu = jnp.mean(r, axis=-1, keepdims=True)
    d = r - mu
    var = jnp.mean(d * d, axis=-1, keepdims=True)
    rstd = lax.rsqrt(var + LN_EPS)
    return d * rstd, rstd


def _ln_bwd(dout, xhat, rstd, g):
    dxh = dout * g
    m1 = jnp.mean(dxh, axis=-1, keepdims=True)
    m2 = jnp.mean(dxh * xhat, axis=-1, keepdims=True)
    return rstd * (dxh - m1 - xhat * m2)


def _rowsum(a):
    return jnp.sum(a, axis=0, keepdims=True)


def _pool_fwd(xp, xp_prev, inv_cnt, w):
    s = jnp.concatenate([xp_prev, xp], axis=0)
    k = 1
    while k < w:
        s = s + pltpu.roll(s, k, 0)
        k *= 2
    return s[HALO:, :] * inv_cnt - xp


def _pool_bwd(dpooled, dpooled_next, inv_cnt, inv_cnt_next, w):
    n = dpooled.shape[0] + HALO
    s = jnp.concatenate([dpooled * inv_cnt, dpooled_next * inv_cnt_next], axis=0)
    k = 1
    while k < w:
        s = s + pltpu.roll(s, n - k, 0)
        k *= 2
    return s[: dpooled.shape[0], :] - dpooled


def _inv_count(pos, w):
    return 1.0 / jnp.minimum(pos + 1, w).astype(F32)


def _to_head_major(a, h, nc):
    return jnp.concatenate(
        [a[c * GROUP:(c + 1) * GROUP, h * GROUP:(h + 1) * GROUP] for c in range(nc)], axis=1)


def _masked_sgu_w(sw_ref, h):
    row = lax.broadcasted_iota(jnp.int32, (GROUP, GROUP), 0)
    col = lax.broadcasted_iota(jnp.int32, (GROUP, GROUP), 1)
    return jnp.where(row >= col, sw_ref[h], 0.0)


def _row_block(rows, limit):
    return max(b for b in range(16, min(rows, limit) + 1, 16) if rows % b == 0)


def _mesh_position():
    return lax.axis_index("x"), lax.axis_index("y"), lax.axis_index("c")


def _other_chips(x, y):
    return [(1 - x, y), (x, 1 - y), (1 - x, 1 - y)]


class _TwoLevelGather:
    def __init__(self, ins, outs, send_sems, recv_sems, local_sems):
        self.ins, self.outs = ins, outs
        self.send_sems, self.recv_sems, self.local_sems = send_sems, recv_sems, local_sems
        self.na = len(ins)
        x, y, c = _mesh_position()
        self.c = c
        self.me, self.sibling = (x, y, c), (x, y, 1 - c)
        self.chips = _other_chips(x, y)

    def _rows(self, a, px, py, pc):
        n = self.ins[a].shape[0]
        return self.outs[a].at[pl.ds((4 * px + 2 * py + pc) * n, n), :]

    def _copy(self, a, k, block, to, src=None):
        return pltpu.make_async_remote_copy(
            src_ref=self._rows(a, *block) if src is None else src, dst_ref=self._rows(a, *block),
            send_sem=self.send_sems.at[a * 7 + k], recv_sem=self.recv_sems.at[a * 7 + k],
            device_id=to, device_id_type=MESH)

    def _mine(self, a):
        return pltpu.make_async_copy(self.ins[a], self._rows(a, *self.me), self.local_sems.at[a])

    def start(self):
        for a in range(self.na):
            self._mine(a).start()
        for a in range(self.na):
            self._copy(a, 0, self.me, self.sibling, src=self.ins[a]).start()
            for j, chip in enumerate(self.chips):
                self._copy(a, 1 + j, self.me, (*chip, self.c), src=self.ins[a]).start()

    def pass_on(self, a):
        for j, chip in enumerate(self.chips):
            self._copy(a, 1 + j, (*chip, self.c), self.me).wait_recv()
            self._copy(a, 4 + j, (*chip, self.c), self.sibling).start()

    def finish(self):
        for a in range(self.na):
            self._copy(a, 0, self.sibling, self.me).wait_recv()
            for j, chip in enumerate(self.chips):
                self._copy(a, 4 + j, (*chip, 1 - self.c), self.me).wait_recv()
        for a in range(self.na):
            for k in range(7):
                self._copy(a, k, self.me, self.sibling, src=self.ins[a]).wait_send()
            self._mine(a).wait()

    @staticmethod
    def scratch(na):
        return [pltpu.SemaphoreType.DMA((7 * na,)), pltpu.SemaphoreType.DMA((7 * na,)), pltpu.SemaphoreType.DMA((na,))]


def _gathered_shape(s):
    return jax.ShapeDtypeStruct((N_DEV * s.shape[0], s.shape[1]), s.dtype)


def _all_gather_rows(shards, name):
    na = len(shards)

    def body(*refs):
        gather = _TwoLevelGather(refs[:na], refs[na:2 * na], *refs[2 * na:])
        gather.start()
        for a in range(na):
            gather.pass_on(a)
        gather.finish()

    any_spec = pl.BlockSpec(memory_space=pl.ANY)
    return pl.pallas_call(
        body, name=name, out_shape=[_gathered_shape(s) for s in shards],
        in_specs=[any_spec] * na, out_specs=[any_spec] * na, scratch_shapes=_TwoLevelGather.scratch(na),
    )(*shards)


def _mixer_fwd(x2d, win_t, wout, pool_w, pool_scale, sln_g, sln_b, sgu_w, sgu_b_t, ln1_g, ln1_b, later_shards, seq):
    tokens = x2d.shape[0]
    tt = min(512, seq)
    tiles_per_seq = seq // tt
    nc = tt // GROUP
    n_tiles = tokens // tt
    n_later = len(later_shards)

    def body(x_ref, xh_ref, win_ref, wout_ref, pw_ref, ps_ref, lg_ref, lb_ref, sw_ref, sb_ref, g1_ref, b1_ref, *rest):
        shard_refs, rest = rest[:n_later], rest[n_later:]
        proj_ref, xhat_ref, rstd_ref, mix_ref, xbf_ref = rest[:5]
        gathered_refs, rest = rest[5:5 + n_later], rest[5 + n_later:]
        mix_scr, send_sems, recv_sems, local_sems = rest
        i = pl.program_id(0)
        gather = _TwoLevelGather(shard_refs, gathered_refs, send_sems, recv_sems, local_sems)

        @pl.when(i == 0)
        def _():
            gather.start()

        tile_in_seq = i % tiles_per_seq
        x = x_ref[...]
        xb = x.astype(BF16)
        xbf_ref[...] = xb
        proj = _dot_nt(xb, win_ref[...])
        proj_ref[...] = proj
        xp_prev = _dot_nt(xh_ref[...].astype(BF16), win_ref[0:POOL_WIDTH, :])
        xp_prev = jnp.where(tile_in_seq == 0, 0.0, xp_prev)
        pos = tile_in_seq * tt + lax.broadcasted_iota(jnp.int32, (tt, 1), 0)
        for g, w in enumerate(POOL_WINDOWS):
            sl = slice(g * GROUP, (g + 1) * GROUP)
            pooled = _pool_fwd(proj[:, sl], xp_prev[:, sl], _inv_count(pos, w), w)
            pre = _dot_nn(pooled.astype(BF16), pw_ref[g].astype(BF16))
            mix_scr[:, sl] = pre * ps_ref[:, sl]
        u = _gelu(proj[:, POOL_WIDTH:POOL_WIDTH + SGU_WIDTH])
        v = _gelu(proj[:, POOL_WIDTH + SGU_WIDTH:])
        vhat, _ = _ln_stats(v)
        v_ln = vhat * lg_ref[...] + lb_ref[...]
        for h in range(N_HEADS):
            ws = _masked_sgu_w(sw_ref, h).astype(BF16)
            mixed = _dot_nn(ws, _to_head_major(v_ln, h, nc).astype(BF16)) + sb_ref[:, h:h + 1]
            for c in range(nc):
                rs = slice(c * GROUP, (c + 1) * GROUP)
                mix_scr[rs, POOL_WIDTH + h * GROUP:POOL_WIDTH + (h + 1) * GROUP] = (
                    u[rs, h * GROUP:(h + 1) * GROUP] * mixed[:, c * GROUP:(c + 1) * GROUP])
        mixb = mix_scr[...].astype(BF16)
        mix_ref[...] = mixb
        r1 = ALPHA * x + _dot_nn(mixb, wout_ref[...])
        xhat, rstd = _ln_stats(r1)
        xhat_ref[...] = xhat
        rstd_ref[...] = rstd

        for a in range(n_later):
            @pl.when(i == max(n_tiles - n_later + a, 0))
            def _(a=a):
                gather.pass_on(a)

        @pl.when(i == n_tiles - 1)
        def _():
            gather.finish()

    def tile(cols):
        return pl.BlockSpec((tt, cols), lambda i: (i, 0))

    def whole(a):
        nd = a.ndim
        return pl.BlockSpec(a.shape, lambda i: (0,) * nd)

    any_spec = pl.BlockSpec(memory_space=pl.ANY)
    halo = pl.BlockSpec((HALO, D_MODEL), lambda i: (jnp.maximum(i * (tt // HALO) - 1, 0), 0))
    consts = [win_t, wout, pool_w, pool_scale, sln_g, sln_b, sgu_w, sgu_b_t, ln1_g, ln1_b]
    return pl.pallas_call(
        body, name="mixer_fwd", grid=(n_tiles,),
        in_specs=[tile(D_MODEL), halo] + [whole(a) for a in consts] + [any_spec] * n_later,
        out_specs=[tile(IN_COLS), tile(D_MODEL), tile(1), tile(D_MODEL), tile(D_MODEL)] + [any_spec] * n_later,
        out_shape=[jax.ShapeDtypeStruct((tokens, IN_COLS), F32), jax.ShapeDtypeStruct((tokens, D_MODEL), F32),
                   jax.ShapeDtypeStruct((tokens, 1), F32), jax.ShapeDtypeStruct((tokens, D_MODEL), BF16),
                   jax.ShapeDtypeStruct((tokens, D_MODEL), BF16)] + [_gathered_shape(s) for s in later_shards],
        scratch_shapes=[pltpu.VMEM((tt, D_MODEL), F32)] + _TwoLevelGather.scratch(n_later),
        compiler_params=pltpu.CompilerParams(dimension_semantics=("arbitrary",), vmem_limit_bytes=VMEM_LIMIT_BYTES),
    )(x2d, x2d, *consts, *later_shards)


def _ffn_fwd_bwd(xhat1, rstd1, target, wgu_t, wdown, ln1_g, ln1_b, ln2_g, ln2_b):
    tokens = xhat1.shape[0]
    tt = min(256, tokens)
    n_chunks = D_FF // FF_CHUNK

    def body(xhat_ref, rstd_ref, tgt_ref, wgu_hbm, wd_hbm, g1_ref, b1_ref, g2_ref, b2_ref,
             dr1_ref, dr1bf_ref, hbf_ref, dr2bf_ref, a_ref, dgu_ref, stats_ref, wgu_ref, wd_ref, gu_scr, sems):
        i = pl.program_id(0)

        @pl.when(i == 0)
        def _():
            loads = [pltpu.make_async_copy(wgu_hbm, wgu_ref, sems.at[0]),
                     pltpu.make_async_copy(wd_hbm, wd_ref, sems.at[1])]
            for cp in loads:
                cp.start()
            stats_ref[...] = jnp.zeros_like(stats_ref)
            for cp in loads:
                cp.wait()

        xhat1_t = xhat_ref[...]
        h = xhat1_t * g1_ref[...] + b1_ref[...]
        hb = h.astype(BF16)
        hbf_ref[...] = hb
        f = jnp.zeros((tt, D_MODEL), F32)
        for k in range(n_chunks):
            rg = slice(k * FF_CHUNK, (k + 1) * FF_CHUNK)
            ru = slice(D_FF + k * FF_CHUNK, D_FF + (k + 1) * FF_CHUNK)
            gate = _dot_nt(hb, wgu_ref[rg, :])
            up = _dot_nt(hb, wgu_ref[ru, :])
            gu_scr[:, rg] = gate
            gu_scr[:, ru] = up
            ab = (gate * jax.nn.sigmoid(gate) * up).astype(BF16)
            a_ref[:, rg] = ab
            f = f + _dot_nn(ab, wd_ref[rg, :])
        xhat2, rstd2 = _ln_stats(ALPHA * h + f)
        err = xhat2 * g2_ref[...] + b2_ref[...] - tgt_ref[...]
        dy = err * (1.0 / D_MODEL)
        stats_ref[0:1, :] += _rowsum(dy * xhat2)
        stats_ref[1:2, :] += _rowsum(dy)
        stats_ref[4:5, :] += _rowsum(err * err)
        dr2 = _ln_bwd(dy, xhat2, rstd2, g2_ref[...])
        dr2b = dr2.astype(BF16)
        dr2bf_ref[...] = dr2b
        dh = ALPHA * dr2
        for k in range(n_chunks):
            rg = slice(k * FF_CHUNK, (k + 1) * FF_CHUNK)
            ru = slice(D_FF + k * FF_CHUNK, D_FF + (k + 1) * FF_CHUNK)
            da = _dot_nt(dr2b, wd_ref[rg, :])
            gate = gu_scr[:, rg]
            up = gu_scr[:, ru]
            sg = jax.nn.sigmoid(gate)
            dgate = (da * up * (sg * (1.0 + gate * (1.0 - sg)))).astype(BF16)
            dup = (da * (gate * sg)).astype(BF16)
            dgu_ref[:, rg] = dgate
            dgu_ref[:, ru] = dup
            dh = dh + _dot_nn(dgate, wgu_ref[rg, :]) + _dot_nn(dup, wgu_ref[ru, :])
        stats_ref[2:3, :] += _rowsum(dh * xhat1_t)
        stats_ref[3:4, :] += _rowsum(dh)
        dr1 = _ln_bwd(dh, xhat1_t, rstd_ref[...], g1_ref[...])
        dr1_ref[...] = dr1
        dr1bf_ref[...] = dr1.astype(BF16)

    def tile(cols):
        return pl.BlockSpec((tt, cols), lambda i: (i, 0))

    def whole(a):
        nd = a.ndim
        return pl.BlockSpec(a.shape, lambda i: (0,) * nd)

    any_spec = pl.BlockSpec(memory_space=pl.ANY)
    vecs = [ln1_g, ln1_b, ln2_g, ln2_b]
    return pl.pallas_call(
        body, name="ffn_fwd_bwd", grid=(tokens // tt,),
        in_specs=[tile(D_MODEL), tile(1), tile(D_MODEL), any_spec, any_spec] + [whole(a) for a in vecs],
        out_specs=[tile(D_MODEL), tile(D_MODEL), tile(D_MODEL), tile(D_MODEL), tile(D_FF), tile(2 * D_FF),
                   pl.BlockSpec((8, D_MODEL), lambda i: (0, 0))],
        out_shape=[jax.ShapeDtypeStruct((tokens, D_MODEL), F32), jax.ShapeDtypeStruct((tokens, D_MODEL), BF16),
                   jax.ShapeDtypeStruct((tokens, D_MODEL), BF16),
                   jax.ShapeDtypeStruct((tokens, D_MODEL), BF16), jax.ShapeDtypeStruct((tokens, D_FF), BF16),
                   jax.ShapeDtypeStruct((tokens, 2 * D_FF), BF16), jax.ShapeDtypeStruct((8, D_MODEL), F32)],
        scratch_shapes=[pltpu.VMEM(wgu_t.shape, BF16), pltpu.VMEM(wdown.shape, BF16),
                        pltpu.VMEM((tt, 2 * D_FF), F32), pltpu.SemaphoreType.DMA((2,))],
        compiler_params=pltpu.CompilerParams(dimension_semantics=("arbitrary",), vmem_limit_bytes=VMEM_LIMIT_BYTES),
    )(xhat1, rstd1, target, wgu_t, wdown, *vecs)


def _wgrad_exchange(lhs, rhs, chips_per_block, name, gather_rows=()):
    tokens, n_all = lhs.shape
    m = rhs.shape[1]
    n = n_all // N_DEV
    tw = min(1024, tokens)
    nt = tokens // tw
    cpb = chips_per_block
    nj = N_CHIPS // cpb
    ng = len(gather_rows)

    def body(l_ref, r_ref, *rest):
        small_refs, rest = rest[:ng], rest[ng:]
        kept_ref, sib_ref = rest[:2]
        gathered_refs, rest = rest[2:2 + ng], rest[2 + ng:]
        acc, sendbuf, send_sems, recv_sems = rest[:4]
        j, t = pl.program_id(0), pl.program_id(1)
        first, last = (j == 0) & (t == 0), (j == nj - 1) & (t == nt - 1)
        x, y, c = _mesh_position()
        if ng:
            gather = _TwoLevelGather(small_refs, gathered_refs, *rest[4:])

            @pl.when(first)
            def _():
                gather.start()

        def copy(q):
            return pltpu.make_async_remote_copy(
                src_ref=sendbuf.at[q], dst_ref=sib_ref.at[q], send_sem=send_sems.at[q], recv_sem=recv_sems.at[q],
                device_id=(x, y, 1 - c), device_id_type=MESH)

        @pl.when(t == 0)
        def _():
            acc[...] = jnp.zeros_like(acc)

        acc[...] += _dot_tn(l_ref[...], r_ref[...])

        @pl.when(t == nt - 1)
        def _():
            for qq in range(cpb):
                q = j * cpb + qq
                kept_ref[qq] = acc[pl.ds(pl.multiple_of(qq * 2 * n + c * n, 8), n), :]
                sendbuf[q] = acc[pl.ds(pl.multiple_of(qq * 2 * n + (1 - c) * n, 8), n), :]
                copy(q).start()

        if ng:
            @pl.when((j == nj - 1) & (t == max(nt - 2, 0)))
            def _():
                for a in range(ng):
                    gather.pass_on(a)

        @pl.when(last)
        def _():
            if ng:
                gather.finish()
            for q in range(N_CHIPS):
                copy(q).wait_send()
                copy(q).wait_recv()

    shard4 = jax.ShapeDtypeStruct((N_CHIPS, n, m), F32)
    any_spec = pl.BlockSpec(memory_space=pl.ANY)
    return pl.pallas_call(
        body, name=name, grid=(nj, nt),
        in_specs=[pl.BlockSpec((tw, 2 * n * cpb), lambda j, t: (t, j)), pl.BlockSpec((tw, m), lambda j, t: (t, 0))]
                 + [any_spec] * ng,
        out_specs=[pl.BlockSpec((cpb, n, m), lambda j, t: (j, 0, 0)), any_spec] + [any_spec] * ng,
        out_shape=[shard4, shard4] + [_gathered_shape(s) for s in gather_rows],
        scratch_shapes=[pltpu.VMEM((2 * n * cpb, m), F32), pltpu.VMEM((N_CHIPS, n, m), F32),
                        pltpu.SemaphoreType.DMA((N_CHIPS,)), pltpu.SemaphoreType.DMA((N_CHIPS,))]
                       + (_TwoLevelGather.scratch(ng) if ng else []),
        compiler_params=pltpu.CompilerParams(dimension_semantics=("arbitrary", "arbitrary"),
                                             vmem_limit_bytes=VMEM_LIMIT_BYTES),
    )(lhs, rhs, *gather_rows)


class _ScatterToOwners:
    def __init__(self, ins, outs, send_sems, recv_sems):
        self.ins, self.outs, self.send_sems, self.recv_sems = ins, outs, send_sems, recv_sems
        x, y, c = _mesh_position()
        self.c, self.chips = c, _other_chips(x, y)

    def _copies(self):
        return [pltpu.make_async_remote_copy(
            src_ref=self.ins[a].at[2 * cx + cy], dst_ref=self.outs[a].at[j],
            send_sem=self.send_sems.at[a * 3 + j], recv_sem=self.recv_sems.at[a * 3 + j],
            device_id=(cx, cy, self.c), device_id_type=MESH)
            for a in range(len(self.ins)) for j, (cx, cy) in enumerate(self.chips)]

    def start(self):
        for cp in self._copies():
            cp.start()

    def finish(self):
        for cp in self._copies():
            cp.wait_send()
            cp.wait_recv()

    @staticmethod
    def scratch(na):
        return [pltpu.SemaphoreType.DMA((3 * na,)), pltpu.SemaphoreType.DMA((3 * na,))]


def _received_shape(p):
    return jax.ShapeDtypeStruct((3,) + p.shape[1:], p.dtype)


ROW_POOL_SCALE, ROW_SLN_G, ROW_SLN_B, ROW_SGU_B = 0, 1, 2, 3
ROW_LN2_G, ROW_LN2_B, ROW_LN1_G, ROW_LN1_B, ROW_LOSS = 8, 9, 10, 11, 12
VEC_ROWS = 16


def _mixer_bwd(dr1_bf, proj, wout, pool_w, pool_scale, sln_g, sln_b, sgu_w, sgu_b_t, stats, chip_partials, seq):
    tokens = dr1_bf.shape[0]
    tt = min(512, seq)
    tiles_per_seq = seq // tt
    nc = tt // GROUP
    n_halo_blocks = tokens // HALO
    n_tiles = tokens // tt
    n_part = len(chip_partials)

    def body(dr1_ref, dr1n_ref, proj_ref, projh_ref, wout_ref, pw_ref, ps_ref, lg_ref, lb_ref, sw_ref, sb_ref, stats_ref,
             *rest):
        part_refs, rest = rest[:n_part], rest[n_part:]
        dproj_ref, dmat_ref, dvec_ref = rest[:3]
        recv_refs, rest = rest[3:3 + n_part], rest[3 + n_part:]
        du_scr, dv_scr, send_sems, recv_sems = rest
        i = pl.program_id(0)
        tile_in_seq = i % tiles_per_seq
        scatter = _ScatterToOwners(part_refs, recv_refs, send_sems, recv_sems)

        @pl.when(i == 0)
        def _():
            scatter.start()
            dmat_ref[...] = jnp.zeros_like(dmat_ref)
            dvec_ref[0:8, :] = jnp.zeros((8, D_MODEL), F32)
            dvec_ref[8:16, :] = stats_ref[...]

        dr1b = dr1_ref[...]
        dmix = _dot_nt(dr1b, wout_ref[...])
        dpo_next = _dot_nt(dr1n_ref[...], wout_ref[0:POOL_WIDTH, :])
        dpo_next = jnp.where(tile_in_seq == tiles_per_seq - 1, 0.0, dpo_next)
        proj = proj_ref[...]
        xp_prev = jnp.where(tile_in_seq == 0, 0.0, projh_ref[...])
        pos = tile_in_seq * tt + lax.broadcasted_iota(jnp.int32, (tt, 1), 0)
        pos_next = (tile_in_seq + 1) * tt + lax.broadcasted_iota(jnp.int32, (HALO, 1), 0)

        for g, w in enumerate(POOL_WINDOWS):
            sl = slice(g * GROUP, (g + 1) * GROUP)
            inv_cnt = _inv_count(pos, w)
            pwb = pw_ref[g].astype(BF16)
            pooledb = _pool_fwd(proj[:, sl], xp_prev[:, sl], inv_cnt, w).astype(BF16)
            pre = _dot_nn(pooledb, pwb)
            dpo = dmix[:, sl]
            dvec_ref[ROW_POOL_SCALE:ROW_POOL_SCALE + 1, sl] += _rowsum(dpo * pre)
            dsb = (dpo * ps_ref[:, sl]).astype(BF16)
            dmat_ref[g] += _dot_tn(pooledb, dsb)
            dpooled = _dot_nt(dsb, pwb)
            dpooled_next = _dot_nt((dpo_next[:, sl] * ps_ref[:, sl]).astype(BF16), pwb)
            dxp = _pool_bwd(dpooled, dpooled_next, inv_cnt, _inv_count(pos_next, w), w)
            dproj_ref[:, sl] = dxp.astype(BF16)

        zu = proj[:, POOL_WIDTH:POOL_WIDTH + SGU_WIDTH]
        zv = proj[:, POOL_WIDTH + SGU_WIDTH:]
        u = _gelu(zu)
        vhat, rstd_v = _ln_stats(_gelu(zv))
        v_ln = vhat * lg_ref[...] + lb_ref[...]
        dsg = dmix[:, POOL_WIDTH:]
        row = lax.broadcasted_iota(jnp.int32, (GROUP, GROUP), 0)
        col = lax.broadcasted_iota(jnp.int32, (GROUP, GROUP), 1)
        for h in range(N_HEADS):
            ws = _masked_sgu_w(sw_ref, h).astype(BF16)
            vh = _to_head_major(v_ln, h, nc).astype(BF16)
            mixed = _dot_nn(ws, vh) + sb_ref[:, h:h + 1]
            dsg_h = _to_head_major(dsg, h, nc)
            du_h = dsg_h * mixed
            dm_h = dsg_h * _to_head_major(u, h, nc)
            pos_sums = lax.dot_general(jnp.ones((8, nc * GROUP), F32), dm_h, (((1,), (1,)), ((), ())),
                                       precision=lax.Precision.HIGHEST, preferred_element_type=F32)
            dvec_ref[ROW_SGU_B + h:ROW_SGU_B + h + 1, 0:GROUP] += pos_sums[0:1, :]
            dmb = dm_h.astype(BF16)
            dmat_ref[len(POOL_WINDOWS) + h] += jnp.where(row >= col, _dot_nt(dmb, vh), 0.0)
            dv_h = _dot_tn(ws, dmb)
            for c in range(nc):
                rs = slice(c * GROUP, (c + 1) * GROUP)
                cs = slice(h * GROUP, (h + 1) * GROUP)
                du_scr[rs, cs] = du_h[:, c * GROUP:(c + 1) * GROUP]
                dv_scr[rs, cs] = dv_h[:, c * GROUP:(c + 1) * GROUP]
        dv_ln = dv_scr[...]
        dvec_ref[ROW_SLN_B:ROW_SLN_B + 1, 0:SGU_WIDTH] += _rowsum(dv_ln)
        dvec_ref[ROW_SLN_G:ROW_SLN_G + 1, 0:SGU_WIDTH] += _rowsum(dv_ln * vhat)
        dv = _ln_bwd(dv_ln, vhat, rstd_v, lg_ref[...])
        dproj_ref[:, POOL_WIDTH:POOL_WIDTH + SGU_WIDTH] = (du_scr[...] * _gelu_grad(zu)).astype(BF16)
        dproj_ref[:, POOL_WIDTH + SGU_WIDTH:] = (dv * _gelu_grad(zv)).astype(BF16)

        @pl.when(i == n_tiles - 1)
        def _():
            scatter.finish()

    def tile(cols):
        return pl.BlockSpec((tt, cols), lambda i: (i, 0))

    def whole(a):
        nd = a.ndim
        return pl.BlockSpec(a.shape, lambda i: (0,) * nd)

    def resident(shape):
        nd = len(shape)
        return pl.BlockSpec(shape, lambda i: (0,) * nd)

    any_spec = pl.BlockSpec(memory_space=pl.ANY)
    next_halo = pl.BlockSpec((HALO, D_MODEL), lambda i: (jnp.minimum((i + 1) * (tt // HALO), n_halo_blocks - 1), 0))
    prev_halo = pl.BlockSpec((HALO, POOL_WIDTH), lambda i: (jnp.maximum(i * (tt // HALO) - 1, 0), 0))
    consts = [wout, pool_w, pool_scale, sln_g, sln_b, sgu_w, sgu_b_t, stats]
    small_shapes = [(len(POOL_WINDOWS) + N_HEADS, GROUP, GROUP), (VEC_ROWS, D_MODEL)]
    return pl.pallas_call(
        body, name="mixer_bwd", grid=(n_tiles,),
        in_specs=[tile(D_MODEL), next_halo, tile(IN_COLS), prev_halo] + [whole(a) for a in consts]
                 + [any_spec] * n_part,
        out_specs=[tile(IN_COLS)] + [resident(s) for s in small_shapes] + [any_spec] * n_part,
        out_shape=[jax.ShapeDtypeStruct((tokens, IN_COLS), BF16)]
                  + [jax.ShapeDtypeStruct(s, F32) for s in small_shapes] + [_received_shape(p) for p in chip_partials],
        scratch_shapes=[pltpu.VMEM((tt, SGU_WIDTH), F32), pltpu.VMEM((tt, SGU_WIDTH), F32)]
                       + _ScatterToOwners.scratch(n_part),
        compiler_params=pltpu.CompilerParams(dimension_semantics=("arbitrary",), vmem_limit_bytes=VMEM_LIMIT_BYTES),
    )(dr1_bf, dr1_bf, proj, proj, *consts, *chip_partials)


def _in_proj_bwd_x(dr1, dproj_bf, win_t):
    tokens = dr1.shape[0]
    tt = min(512, tokens)

    def body(dr1_ref, dproj_ref, win_ref, gx_ref):
        gx_ref[...] = ALPHA * dr1_ref[...] + _dot_nn(dproj_ref[...], win_ref[...])

    return pl.pallas_call(
        body, name="in_proj_bwd_x", grid=(tokens // tt,),
        in_specs=[pl.BlockSpec((tt, D_MODEL), lambda i: (i, 0)), pl.BlockSpec((tt, IN_COLS), lambda i: (i, 0)),
                  pl.BlockSpec(win_t.shape, lambda i: (0, 0))],
        out_specs=pl.BlockSpec((tt, D_MODEL), lambda i: (i, 0)),
        out_shape=jax.ShapeDtypeStruct((tokens, D_MODEL), F32),
        compiler_params=pltpu.CompilerParams(dimension_semantics=("arbitrary",), vmem_limit_bytes=VMEM_LIMIT_BYTES),
    )(dr1, dproj_bf, win_t)


def _chip_sums(kept, from_sibling, place, name):
    na = len(kept)

    def body(place_ref, *refs):
        kept_refs, sib_refs = refs[:na], refs[na:2 * na]
        bf_refs, own_refs = refs[2 * na:3 * na], refs[3 * na:]
        q = pl.program_id(0)
        for a in range(na):
            s = kept_refs[a][...] + sib_refs[a][...]
            bf_refs[a][...] = s.astype(BF16)

            @pl.when(q == place_ref[1])
            def _(a=a, s=s):
                own_refs[a][...] = s

    by_chip = [pl.BlockSpec((None,) + k.shape[1:], lambda q, pr: (q, 0, 0)) for k in kept]
    grid_spec = pltpu.PrefetchScalarGridSpec(
        num_scalar_prefetch=1, grid=(N_CHIPS,), in_specs=by_chip + by_chip,
        out_specs=by_chip + [pl.BlockSpec(k.shape[1:], lambda q, pr: (0, 0)) for k in kept])
    return pl.pallas_call(
        body, name=name, grid_spec=grid_spec,
        out_shape=[jax.ShapeDtypeStruct(k.shape, BF16) for k in kept]
                  + [jax.ShapeDtypeStruct(k.shape[1:], F32) for k in kept],
        compiler_params=pltpu.CompilerParams(dimension_semantics=("arbitrary",), vmem_limit_bytes=VMEM_LIMIT_BYTES),
    )(place, *kept, *from_sibling)


def _send_to_owners(chip_partials, name):
    na = len(chip_partials)

    def body(*refs):
        scatter = _ScatterToOwners(refs[:na], refs[na:2 * na], *refs[2 * na:])
        scatter.start()
        scatter.finish()

    any_spec = pl.BlockSpec(memory_space=pl.ANY)
    return pl.pallas_call(
        body, name=name, out_shape=[_received_shape(p) for p in chip_partials],
        in_specs=[any_spec] * na, out_specs=[any_spec] * na, scratch_shapes=_ScatterToOwners.scratch(na),
    )(*chip_partials)


def _adamw_math(w, g, m, v):
    m = ADAM_B1 * m + (1.0 - ADAM_B1) * g
    v = ADAM_B2 * v + (1.0 - ADAM_B2) * (g * g)
    m_hat = m / (1.0 - ADAM_B1 ** ADAM_STEP)
    v_hat = v / (1.0 - ADAM_B2 ** ADAM_STEP)
    delta = -ADAM_LR * (m_hat / (jnp.sqrt(v_hat) + ADAM_EPS) + ADAM_WD * w)
    return delta, m, v


def _sum_adamw(w, m, v, own, received, name):
    rows, cols = w.shape
    rb = _row_block(rows, 256)

    def body(w_ref, m_ref, v_ref, own_ref, rec_ref, g_ref, d_ref, mo_ref, vo_ref):
        g = own_ref[...]
        for j in range(3):
            g = g + rec_ref[j].astype(F32)
        g_ref[...] = g
        d_ref[...], mo_ref[...], vo_ref[...] = _adamw_math(w_ref[...], g, m_ref[...], v_ref[...])

    spec = pl.BlockSpec((rb, cols), lambda r: (r, 0))
    return pl.pallas_call(
        body, name=name, grid=(rows // rb,),
        in_specs=[spec] * 4 + [pl.BlockSpec((3, rb, cols), lambda r: (0, r, 0))], out_specs=[spec] * 4,
        out_shape=[jax.ShapeDtypeStruct((rows, cols), F32)] * 4,
        compiler_params=pltpu.CompilerParams(dimension_semantics=("arbitrary",)),
    )(w, m, v, own, received)


SMALL_NAMES = ("pool_w", "sgu_w", "pool_scale", "sgu_ln_g", "sgu_ln_b", "sgu_b", "ln1_g", "ln1_b", "ln2_g", "ln2_b")
_SMALL_VEC_ROWS = {"pool_scale": (ROW_POOL_SCALE, POOL_WIDTH), "sgu_ln_g": (ROW_SLN_G, SGU_WIDTH),
                   "sgu_ln_b": (ROW_SLN_B, SGU_WIDTH), "ln1_g": (ROW_LN1_G, D_MODEL), "ln1_b": (ROW_LN1_B, D_MODEL),
                   "ln2_g": (ROW_LN2_G, D_MODEL), "ln2_b": (ROW_LN2_B, D_MODEL)}
_SMALL_MAT_FIRST = {"pool_w": 0, "sgu_w": len(POOL_WINDOWS)}


def _small_sum_adamw(mats_all, vecs_all, w, m, v):
    n = len(SMALL_NAMES)

    def body(mats_ref, vecs_ref, *refs):
        w_refs, m_refs, v_refs = refs[:n], refs[n:2 * n], refs[2 * n:3 * n]
        loss_ref = refs[3 * n]
        g_refs, d_refs, mo_refs, vo_refs = (refs[3 * n + 1 + k * n:3 * n + 1 + (k + 1) * n] for k in range(4))
        vec_scr = refs[7 * n + 1]

        def update(k, idx, g):
            d, mo, vo = _adamw_math(w_refs[k][idx], g, m_refs[k][idx], v_refs[k][idx])
            g_refs[k][idx], d_refs[k][idx], mo_refs[k][idx], vo_refs[k][idx] = g, d, mo, vo

        total = vecs_ref[0]
        for dev in range(1, N_DEV):
            total = total + vecs_ref[dev]
        vec_scr[...] = total
        for k, name in enumerate(SMALL_NAMES):
            if name in _SMALL_MAT_FIRST:
                for b in range(4):
                    g = mats_ref[0, _SMALL_MAT_FIRST[name] + b]
                    for dev in range(1, N_DEV):
                        g = g + mats_ref[dev, _SMALL_MAT_FIRST[name] + b]
                    update(k, (0, b), g)
            elif name == "sgu_b":
                update(k, (0,), vec_scr[ROW_SGU_B:ROW_SGU_B + N_HEADS, 0:GROUP])
            else:
                row, width = _SMALL_VEC_ROWS[name]
                update(k, (slice(None), slice(None)), vec_scr[row:row + 1, 0:width])
        loss = jnp.sum(vec_scr[ROW_LOSS:ROW_LOSS + 1, :], axis=1, keepdims=True) * (0.5 / D_MODEL)
        loss_ref[...] = jnp.broadcast_to(loss, loss_ref.shape)

    vmem = pl.BlockSpec(memory_space=pltpu.VMEM)
    shapes = [jax.ShapeDtypeStruct(w[k].shape, F32) for k in SMALL_NAMES]
    outs = pl.pallas_call(
        body, name="small_sum_adamw",
        out_shape=[jax.ShapeDtypeStruct((8, GROUP), F32)] + shapes * 4,
        in_specs=[vmem] * (2 + 3 * n), out_specs=[vmem] * (1 + 4 * n),
        scratch_shapes=[pltpu.VMEM((VEC_ROWS, D_MODEL), F32)],
    )(mats_all, vecs_all, *[w[k] for k in SMALL_NAMES], *[m[k] for k in SMALL_NAMES], *[v[k] for k in SMALL_NAMES])
    return outs[0], outs[1:1 + n], outs[1 + n:1 + 2 * n], outs[1 + 2 * n:1 + 3 * n], outs[1 + 3 * n:]


def kernel(x, w_in, pool_w, pool_scale, sgu_ln_g, sgu_ln_b, sgu_w, sgu_b, w_out, ln1_g, ln1_b, w_gate_up, w_down, ln2_g, ln2_b, loss_target, m_w_in, m_pool_w, m_pool_scale, m_sgu_ln_g, m_sgu_ln_b, m_sgu_w, m_sgu_b, m_w_out, m_ln1_g, m_ln1_b, m_w_gate_up, m_w_down, m_ln2_g, m_ln2_b, v_w_in, v_pool_w, v_pool_scale, v_sgu_ln_g, v_sgu_ln_b, v_sgu_w, v_sgu_b, v_w_out, v_ln1_g, v_ln1_b, v_w_gate_up, v_w_down, v_ln2_g, v_ln2_b):
    bl, seq, _ = x.shape
    tokens = bl * seq
    x2d = x.reshape(tokens, D_MODEL)
    tgt2d = loss_target.reshape(tokens, D_MODEL)
    my_c = lax.axis_index("c")
    place = jnp.stack([my_c, 2 * lax.axis_index("x") + lax.axis_index("y")]).astype(jnp.int32)

    win_t, wout = _all_gather_rows([w_in[0].T.astype(BF16), w_out[0].astype(BF16)], "weight_all_gather")

    pool_w3, sgu_w3 = pool_w[0], sgu_w[0]
    sgu_b_t = sgu_b[0].T
    proj, xhat1, rstd1, mix_bf, x_bf, wgu_t, wdown = _mixer_fwd(
        x2d, win_t, wout, pool_w3, pool_scale, sgu_ln_g, sgu_ln_b, sgu_w3, sgu_b_t, ln1_g, ln1_b,
        [w_gate_up[0].T.astype(BF16), w_down[0].astype(BF16)], seq)
    dr1, dr1_bf, h_bf, dr2_bf, a_bf, dgu_bf, stats = _ffn_fwd_bwd(
        xhat1, rstd1, tgt2d, wgu_t, wdown, ln1_g, ln1_b, ln2_g, ln2_b)

    kept_gu, sib_gu = _wgrad_exchange(dgu_bf, h_bf, 1, "wgrad_gate_up")
    kept_dn, sib_dn = _wgrad_exchange(a_bf, dr2_bf, 2, "wgrad_down")
    kept_out, sib_out = _wgrad_exchange(mix_bf, dr1_bf, N_CHIPS, "wgrad_out")
    bf_gu, bf_dn, bf_out, own_gu, own_dn, own_out = _chip_sums(
        [kept_gu, kept_dn, kept_out], [sib_gu, sib_dn, sib_out], place, "chip_sums")
    dproj_bf, d_mats, d_vecs, rec_gu, rec_dn, rec_out = _mixer_bwd(
        dr1_bf, proj, wout, pool_w3, pool_scale, sgu_ln_g, sgu_ln_b, sgu_w3, sgu_b_t, stats,
        [bf_gu, bf_dn, bf_out], seq)
    gx = _in_proj_bwd_x(dr1, dproj_bf, win_t)
    kept_in, sib_in, mats_all, vecs_all = _wgrad_exchange(
        dproj_bf, x_bf, N_CHIPS, "wgrad_in", gather_rows=[d_mats.reshape(-1, GROUP), d_vecs])
    bf_in, own_in = _chip_sums([kept_in], [sib_in], place, "chip_sum_w_in")
    rec_in, = _send_to_owners([bf_in], "grad_scatter_w_in")

    grads, deltas, new_m, new_v = {}, {}, {}, {}
    for nm, w, m, v, own, rec, transposed in (("w_out", w_out, m_w_out, v_w_out, own_out, rec_out, False),
                                              ("w_down", w_down, m_w_down, v_w_down, own_dn, rec_dn, False),
                                              ("w_in", w_in, m_w_in, v_w_in, own_in, rec_in, True),
                                              ("w_gate_up", w_gate_up, m_w_gate_up, v_w_gate_up, own_gu, rec_gu, True)):
        rows = (lambda a: a[0].T) if transposed else (lambda a: a[0])
        back = (lambda a: a.T[None]) if transposed else (lambda a: a[None])
        g, d, mo, vo = _sum_adamw(rows(w), rows(m), rows(v), own, rec, "adamw_" + nm)
        grads[nm], deltas[nm], new_m[nm], new_v[nm] = back(g), back(d), back(mo), back(vo)

    small_w = {"pool_w": pool_w, "pool_scale": pool_scale, "sgu_ln_g": sgu_ln_g, "sgu_ln_b": sgu_ln_b, "sgu_w": sgu_w,
               "sgu_b": sgu_b, "ln1_g": ln1_g, "ln1_b": ln1_b, "ln2_g": ln2_g, "ln2_b": ln2_b}
    small_m = {"pool_w": m_pool_w, "pool_scale": m_pool_scale, "sgu_ln_g": m_sgu_ln_g, "sgu_ln_b": m_sgu_ln_b,
               "sgu_w": m_sgu_w, "sgu_b": m_sgu_b, "ln1_g": m_ln1_g, "ln1_b": m_ln1_b, "ln2_g": m_ln2_g, "ln2_b": m_ln2_b}
    small_v = {"pool_w": v_pool_w, "pool_scale": v_pool_scale, "sgu_ln_g": v_sgu_ln_g, "sgu_ln_b": v_sgu_ln_b,
               "sgu_w": v_sgu_w, "sgu_b": v_sgu_b, "ln1_g": v_ln1_g, "ln1_b": v_ln1_b, "ln2_g": v_ln2_g, "ln2_b": v_ln2_b}
    loss_blk, g_small, d_small, m_small, v_small = _small_sum_adamw(
        mats_all.reshape(N_DEV, 2 * N_HEADS, GROUP, GROUP), vecs_all.reshape(N_DEV, VEC_ROWS, D_MODEL),
        small_w, small_m, small_v)
    for vals, dst in ((g_small, grads), (d_small, deltas), (m_small, new_m), (v_small, new_v)):
        dst.update(zip(SMALL_NAMES, vals))

    order = ["w_in", "pool_w", "pool_scale", "sgu_ln_g", "sgu_ln_b", "sgu_w", "sgu_b", "w_out", "ln1_g", "ln1_b",
             "w_gate_up", "w_down", "ln2_g", "ln2_b"]
    return (loss_blk[0, 0], gx.reshape(bl, seq, D_MODEL), *[grads[k] for k in order], *[deltas[k] for k in order],
            *[new_m[k] for k in order], *[new_v[k] for k in order])
```

```python
import functools

import jax
import jax.numpy as jnp
from jax import lax
from jax.experimental import pallas as pl
from jax.experimental.pallas import tpu as pltpu

F32 = jnp.float32
BF16 = jnp.bfloat16
MESH = pl.DeviceIdType.MESH

D_MODEL = 1024
POOL_WIDTH = 512
SGU_WIDTH = 512
POOL_WINDOWS = (2, 4, 8, 16)
GROUP = 128
N_HEADS = 4
IN_COLS = POOL_WIDTH + 2 * SGU_WIDTH
D_FF = 2816
FF_CHUNK = 1408
LN_EPS = 1e-5
ALPHA = float(2.0 ** 0.25)
HALO = 16
N_DEV = 8
N_CHIPS = 4

ADAM_LR = 0.001
ADAM_B1 = 0.9
ADAM_B2 = 0.999
ADAM_EPS = 1e-08
ADAM_WD = 0.01
ADAM_STEP = 10

VMEM_LIMIT_BYTES = 56 * 1024 * 1024

_SQRT_HALF = 0.7071067811865476
_INV_SQRT_2PI = 0.3989422804014327


def _dot_nn(a, b):
    return lax.dot_general(a, b, (((1,), (0,)), ((), ())), preferred_element_type=F32)


def _dot_nt(a, b):
    return lax.dot_general(a, b, (((1,), (1,)), ((), ())), preferred_element_type=F32)


def _dot_tn(a, b):
    return lax.dot_general(a, b, (((0,), (0,)), ((), ())), preferred_element_type=F32)


def _gelu(x):
    return 0.5 * x * (1.0 + lax.erf(x * _SQRT_HALF))


def _gelu_grad(x):
    return 0.5 * (1.0 + lax.erf(x * _SQRT_HALF)) + x * jnp.exp(-0.5 * x * x) * _INV_SQRT_2PI


def _ln_stats(r):
    mu = jnp.mean(r, axis=-1, keepdims=True)
    d = r - mu
    var = jnp.mean(d * d, axis=-1, keepdims=True)
    rstd = lax.rsqrt(var + LN_EPS)
    return d * rstd, rstd


def _ln_bwd(dout, xhat, rstd, g):
    dxh = dout * g
    m1 = jnp.mean(dxh, axis=-1, keepdims=True)
    m2 = jnp.mean(dxh * xhat, axis=-1, keepdims=True)
    return rstd * (dxh - m1 - xhat * m2)


def _rowsum(a):
    return jnp.sum(a, axis=0, keepdims=True)


def _pool_fwd(xp, xp_prev, inv_cnt, w):
    s = jnp.concatenate([xp_prev, xp], axis=0)
    k = 1
    while k < w:
        s = s + pltpu.roll(s, k, 0)
        k *= 2
    return s[HALO:, :] * inv_cnt - xp


def _pool_bwd(dpooled, dpooled_next, inv_cnt, inv_cnt_next, w):
    n = dpooled.shape[0] + HALO
    s = jnp.concatenate([dpooled * inv_cnt, dpooled_next * inv_cnt_next], axis=0)
    k = 1
    while k < w:
        s = s + pltpu.roll(s, n - k, 0)
        k *= 2
    return s[: dpooled.shape[0], :] - dpooled


def _inv_count(pos, w):
    return 1.0 / jnp.minimum(pos + 1, w).astype(F32)


def _to_head_major(a, h, nc):
    return jnp.concatenate(
        [a[c * GROUP:(c + 1) * GROUP, h * GROUP:(h + 1) * GROUP] for c in range(nc)], axis=1)


def _masked_sgu_w(sw_ref, h):
    row = lax.broadcasted_iota(jnp.int32, (GROUP, GROUP), 0)
    col = lax.broadcasted_iota(jnp.int32, (GROUP, GROUP), 1)
    return jnp.where(row >= col, sw_ref[h], 0.0)


def _row_block(rows, limit):
    return max(b for b in range(16, min(rows, limit) + 1, 16) if rows % b == 0)


def _mesh_position():
    return lax.axis_index("x"), lax.axis_index("y"), lax.axis_index("c")


def _other_chips(x, y):
    return [(1 - x, y), (x, 1 - y), (1 - x, 1 - y)]


class _TwoLevelGather:
    def __init__(self, ins, outs, send_sems, recv_sems, local_sems):
        self.ins, self.outs = ins, outs
        self.send_sems, self.recv_sems, self.local_sems = send_sems, recv_sems, local_sems
        self.na = len(ins)
        x, y, c = _mesh_position()
        self.c = c
        self.me, self.sibling = (x, y, c), (x, y, 1 - c)
        self.chips = _other_chips(x, y)
        self.relay_from = (x + (1 - c) * (1 - 2 * x), y + c * (1 - 2 * y))
        self.relay_to = (x + c * (1 - 2 * x), y + (1 - c) * (1 - 2 * y))

    def _rows(self, a, px, py, pc):
        n = self.ins[a].shape[0]
        return self.outs[a].at[pl.ds((4 * px + 2 * py + pc) * n, n), :]

    def _copy(self, a, k, block, to, src=None):
        return pltpu.make_async_remote_copy(
            src_ref=self._rows(a, *block) if src is None else src, dst_ref=self._rows(a, *block),
            send_sem=self.send_sems.at[a * 7 + k], recv_sem=self.recv_sems.at[a * 7 + k],
            device_id=to, device_id_type=MESH)

    def _mine(self, a):
        return pltpu.make_async_copy(self.ins[a], self._rows(a, *self.me), self.local_sems.at[a])

    def start(self):
        for a in range(self.na):
            self._mine(a).start()
        for a in range(self.na):
            self._copy(a, 0, self.me, self.sibling, src=self.ins[a]).start()
            for j, chip in enumerate(self.chips[:2]):
                self._copy(a, 1 + j, self.me, (*chip, self.c), src=self.ins[a]).start()

    def relay(self, a):
        c, block = self.c, (*self.relay_from, self.c)
        self._copy(a, 1 + c, block, self.me).wait_recv()
        self._copy(a, 3, block, (*self.relay_to, c)).start()
        self._copy(a, 4 + c, block, self.sibling).start()

    def pass_on(self, a):
        c = self.c
        self._copy(a, 2 - c, (*self.relay_to, c), self.me).wait_recv()
        self._copy(a, 5 - c, (*self.relay_to, c), self.sibling).start()
        self._copy(a, 3, (*self.chips[2], c), self.me).wait_recv()
        self._copy(a, 6, (*self.chips[2], c), self.sibling).start()

    def finish(self):
        for a in range(self.na):
            self._copy(a, 0, self.sibling, self.me).wait_recv()
            for j, chip in enumerate(self.chips):
                self._copy(a, 4 + j, (*chip, 1 - self.c), self.me).wait_recv()
        for a in range(self.na):
            for k in range(7):
                self._copy(a, k, self.me, self.sibling, src=self.ins[a]).wait_send()
            self._mine(a).wait()

    @staticmethod
    def scratch(na):
        return [pltpu.SemaphoreType.DMA((7 * na,)), pltpu.SemaphoreType.DMA((7 * na,)), pltpu.SemaphoreType.DMA((na,))]


def _gathered_shape(s):
    return jax.ShapeDtypeStruct((N_DEV * s.shape[0], s.shape[1]), s.dtype)


def _all_gather_rows(shards, name):
    na = len(shards)

    def body(*refs):
        gather = _TwoLevelGather(refs[:na], refs[na:2 * na], *refs[2 * na:])
        gather.start()
        for a in range(na):
            gather.relay(a)
        for a in range(na):
            gather.pass_on(a)
        gather.finish()

    any_spec = pl.BlockSpec(memory_space=pl.ANY)
    return pl.pallas_call(
        body, name=name, out_shape=[_gathered_shape(s) for s in shards],
        in_specs=[any_spec] * na, out_specs=[any_spec] * na, scratch_shapes=_TwoLevelGather.scratch(na),
    )(*shards)


def _mixer_fwd(x2d, win_t, wout, pool_w, pool_scale, sln_g, sln_b, sgu_w, sgu_b_t, ln1_g, ln1_b, later_shards, seq):
    tokens = x2d.shape[0]
    tt = min(512, seq)
    tiles_per_seq = seq // tt
    nc = tt // GROUP
    n_tiles = tokens // tt
    n_later = len(later_shards)

    def body(x_ref, xh_ref, win_ref, wout_ref, pw_ref, ps_ref, lg_ref, lb_ref, sw_ref, sb_ref, g1_ref, b1_ref, *rest):
        shard_refs, rest = rest[:n_later], rest[n_later:]
        proj_ref, xhat_ref, rstd_ref, mix_ref, xbf_ref = rest[:5]
        gathered_refs, rest = rest[5:5 + n_later], rest[5 + n_later:]
        mix_scr, send_sems, recv_sems, local_sems = rest
        i = pl.program_id(0)
        gather = _TwoLevelGather(shard_refs, gathered_refs, send_sems, recv_sems, local_sems)

        @pl.when(i == 0)
        def _():
            gather.start()

        tile_in_seq = i % tiles_per_seq
        x = x_ref[...]
        xb = x.astype(BF16)
        xbf_ref[...] = xb
        proj = _dot_nt(xb, win_ref[...])
        proj_ref[...] = proj
        xp_prev = _dot_nt(xh_ref[...].astype(BF16), win_ref[0:POOL_WIDTH, :])
        xp_prev = jnp.where(tile_in_seq == 0, 0.0, xp_prev)
        pos = tile_in_seq * tt + lax.broadcasted_iota(jnp.int32, (tt, 1), 0)
        for g, w in enumerate(POOL_WINDOWS):
            sl = slice(g * GROUP, (g + 1) * GROUP)
            pooled = _pool_fwd(proj[:, sl], xp_prev[:, sl], _inv_count(pos, w), w)
            pre = _dot_nn(pooled.astype(BF16), pw_ref[g].astype(BF16))
            mix_scr[:, sl] = pre * ps_ref[:, sl]
        u = _gelu(proj[:, POOL_WIDTH:POOL_WIDTH + SGU_WIDTH])
        v = _gelu(proj[:, POOL_WIDTH + SGU_WIDTH:])
        vhat, _ = _ln_stats(v)
        v_ln = vhat * lg_ref[...] + lb_ref[...]
        for h in range(N_HEADS):
            ws = _masked_sgu_w(sw_ref, h).astype(BF16)
            mixed = _dot_nn(ws, _to_head_major(v_ln, h, nc).astype(BF16)) + sb_ref[:, h:h + 1]
            for c in range(nc):
                rs = slice(c * GROUP, (c + 1) * GROUP)
                mix_scr[rs, POOL_WIDTH + h * GROUP:POOL_WIDTH + (h + 1) * GROUP] = (
                    u[rs, h * GROUP:(h + 1) * GROUP] * mixed[:, c * GROUP:(c + 1) * GROUP])
        mixb = mix_scr[...].astype(BF16)
        mix_ref[...] = mixb
        r1 = ALPHA * x + _dot_nn(mixb, wout_ref[...])
        xhat, rstd = _ln_stats(r1)
        xhat_ref[...] = xhat
        rstd_ref[...] = rstd

        for a in range(n_later):
            relay_tile = min(n_tiles // 2 + a, n_tiles - 1)

            @pl.when(i == relay_tile)
            def _(a=a):
                gather.relay(a)

            @pl.when(i == max(n_tiles - n_later + a, relay_tile))
            def _(a=a):
                gather.pass_on(a)

        @pl.when(i == n_tiles - 1)
        def _():
            gather.finish()

    def tile(cols):
        return pl.BlockSpec((tt, cols), lambda i: (i, 0))

    def whole(a):
        nd = a.ndim
        return pl.BlockSpec(a.shape, lambda i: (0,) * nd)

    any_spec = pl.BlockSpec(memory_space=pl.ANY)
    halo = pl.BlockSpec((HALO, D_MODEL), lambda i: (jnp.maximum(i * (tt // HALO) - 1, 0), 0))
    consts = [win_t, wout, pool_w, pool_scale, sln_g, sln_b, sgu_w, sgu_b_t, ln1_g, ln1_b]
    return pl.pallas_call(
        body, name="mixer_fwd", grid=(n_tiles,),
        in_specs=[tile(D_MODEL), halo] + [whole(a) for a in consts] + [any_spec] * n_later,
        out_specs=[tile(IN_COLS), tile(D_MODEL), tile(1), tile(D_MODEL), tile(D_MODEL)] + [any_spec] * n_later,
        out_shape=[jax.ShapeDtypeStruct((tokens, IN_COLS), F32), jax.ShapeDtypeStruct((tokens, D_MODEL), F32),
                   jax.ShapeDtypeStruct((tokens, 1), F32), jax.ShapeDtypeStruct((tokens, D_MODEL), BF16),
                   jax.ShapeDtypeStruct((tokens, D_MODEL), BF16)] + [_gathered_shape(s) for s in later_shards],
        scratch_shapes=[pltpu.VMEM((tt, D_MODEL), F32)] + _TwoLevelGather.scratch(n_later),
        compiler_params=pltpu.CompilerParams(dimension_semantics=("arbitrary",), vmem_limit_bytes=VMEM_LIMIT_BYTES),
    )(x2d, x2d, *consts, *later_shards)


def _ffn_fwd_bwd(xhat1, rstd1, target, wgu_t, wdown, ln1_g, ln1_b, ln2_g, ln2_b):
    tokens = xhat1.shape[0]
    tt = min(256, tokens)
    n_chunks = D_FF // FF_CHUNK

    def body(xhat_ref, rstd_ref, tgt_ref, wgu_hbm, wd_hbm, g1_ref, b1_ref, g2_ref, b2_ref,
             dr1_ref, dr1bf_ref, hbf_ref, dr2bf_ref, a_ref, dgu_ref, stats_ref, wgu_ref, wd_ref, gu_scr, sems):
        i = pl.program_id(0)

        @pl.when(i == 0)
        def _():
            loads = [pltpu.make_async_copy(wgu_hbm, wgu_ref, sems.at[0]),
                     pltpu.make_async_copy(wd_hbm, wd_ref, sems.at[1])]
            for cp in loads:
                cp.start()
            stats_ref[...] = jnp.zeros_like(stats_ref)
            for cp in loads:
                cp.wait()

        xhat1_t = xhat_ref[...]
        h = xhat1_t * g1_ref[...] + b1_ref[...]
        hb = h.astype(BF16)
        hbf_ref[...] = hb
        f = jnp.zeros((tt, D_MODEL), F32)
        for k in range(n_chunks):
            rg = slice(k * FF_CHUNK, (k + 1) * FF_CHUNK)
            ru = slice(D_FF + k * FF_CHUNK, D_FF + (k + 1) * FF_CHUNK)
            gate = _dot_nt(hb, wgu_ref[rg, :])
            up = _dot_nt(hb, wgu_ref[ru, :])
            gu_scr[:, rg] = gate
            gu_scr[:, ru] = up
            ab = (gate * jax.nn.sigmoid(gate) * up).astype(BF16)
            a_ref[:, rg] = ab
            f = f + _dot_nn(ab, wd_ref[rg, :])
        xhat2, rstd2 = _ln_stats(ALPHA * h + f)
        err = xhat2 * g2_ref[...] + b2_ref[...] - tgt_ref[...]
        dy = err * (1.0 / D_MODEL)
        stats_ref[0:1, :] += _rowsum(dy * xhat2)
        stats_ref[1:2, :] += _rowsum(dy)
        stats_ref[4:5, :] += _rowsum(err * err)
        dr2 = _ln_bwd(dy, xhat2, rstd2, g2_ref[...])
        dr2b = dr2.astype(BF16)
        dr2bf_ref[...] = dr2b
        dh = ALPHA * dr2
        for k in range(n_chunks):
            rg = slice(k * FF_CHUNK, (k + 1) * FF_CHUNK)
            ru = slice(D_FF + k * FF_CHUNK, D_FF + (k + 1) * FF_CHUNK)
            da = _dot_nt(dr2b, wd_ref[rg, :])
            gate = gu_scr[:, rg]
            up = gu_scr[:, ru]
            sg = jax.nn.sigmoid(gate)
            dgate = (da * up * (sg * (1.0 + gate * (1.0 - sg)))).astype(BF16)
            dup = (da * (gate * sg)).astype(BF16)
            dgu_ref[:, rg] = dgate
            dgu_ref[:, ru] = dup
            dh = dh + _dot_nn(dgate, wgu_ref[rg, :]) + _dot_nn(dup, wgu_ref[ru, :])
        stats_ref[2:3, :] += _rowsum(dh * xhat1_t)
        stats_ref[3:4, :] += _rowsum(dh)
        dr1 = _ln_bwd(dh, xhat1_t, rstd_ref[...], g1_ref[...])
        dr1_ref[...] = dr1
        dr1bf_ref[...] = dr1.astype(BF16)

    def tile(cols):
        return pl.BlockSpec((tt, cols), lambda i: (i, 0))

    def whole(a):
        nd = a.ndim
        return pl.BlockSpec(a.shape, lambda i: (0,) * nd)

    any_spec = pl.BlockSpec(memory_space=pl.ANY)
    vecs = [ln1_g, ln1_b, ln2_g, ln2_b]
    return pl.pallas_call(
        body, name="ffn_fwd_bwd", grid=(tokens // tt,),
        in_specs=[tile(D_MODEL), tile(1), tile(D_MODEL), any_spec, any_spec] + [whole(a) for a in vecs],
        out_specs=[tile(D_MODEL), tile(D_MODEL), tile(D_MODEL), tile(D_MODEL), tile(D_FF), tile(2 * D_FF),
                   pl.BlockSpec((8, D_MODEL), lambda i: (0, 0))],
        out_shape=[jax.ShapeDtypeStruct((tokens, D_MODEL), F32), jax.ShapeDtypeStruct((tokens, D_MODEL), BF16),
                   jax.ShapeDtypeStruct((tokens, D_MODEL), BF16),
                   jax.ShapeDtypeStruct((tokens, D_MODEL), BF16), jax.ShapeDtypeStruct((tokens, D_FF), BF16),
                   jax.ShapeDtypeStruct((tokens, 2 * D_FF), BF16), jax.ShapeDtypeStruct((8, D_MODEL), F32)],
        scratch_shapes=[pltpu.VMEM(wgu_t.shape, BF16), pltpu.VMEM(wdown.shape, BF16),
                        pltpu.VMEM((tt, 2 * D_FF), F32), pltpu.SemaphoreType.DMA((2,))],
        compiler_params=pltpu.CompilerParams(dimension_semantics=("arbitrary",), vmem_limit_bytes=VMEM_LIMIT_BYTES),
    )(xhat1, rstd1, target, wgu_t, wdown, *vecs)


def _wgrad_exchange(lhs, rhs, chips_per_block, name, gather_rows=(), scatter_partials=()):
    tokens, n_all = lhs.shape
    m = rhs.shape[1]
    n = n_all // N_DEV
    tw = min(2048, tokens)
    nt = tokens // tw
    cpb = chips_per_block
    nj = N_CHIPS // cpb
    ng, ns = len(gather_rows), len(scatter_partials)

    def body(l_ref, r_ref, *rest):
        small_refs, part_refs, rest = rest[:ng], rest[ng:ng + ns], rest[ng + ns:]
        kept_ref, sib_ref = rest[:2]
        gathered_refs, recv_refs, rest = rest[2:2 + ng], rest[2 + ng:2 + ng + ns], rest[2 + ng + ns:]
        acc, sendbuf, send_sems, recv_sems = rest[:4]
        j, t = pl.program_id(0), pl.program_id(1)
        first, last = (j == 0) & (t == 0), (j == nj - 1) & (t == nt - 1)
        x, y, c = _mesh_position()
        if ng:
            gather = _TwoLevelGather(small_refs, gathered_refs, *rest[4:7])

            @pl.when(first)
            def _():
                gather.start()
        if ns:
            scatter = _ScatterToOwners(part_refs, recv_refs, *rest[4 + 3 * bool(ng):])

            @pl.when(first)
            def _():
                scatter.start()

        def copy(q):
            return pltpu.make_async_remote_copy(
                src_ref=sendbuf.at[q], dst_ref=sib_ref.at[q], send_sem=send_sems.at[q], recv_sem=recv_sems.at[q],
                device_id=(x, y, 1 - c), device_id_type=MESH)

        @pl.when(t == 0)
        def _():
            acc[...] = jnp.zeros_like(acc)

        acc[...] += _dot_tn(l_ref[...], r_ref[...])

        @pl.when(t == nt - 1)
        def _():
            for qq in range(cpb):
                q = j * cpb + qq
                kept_ref[qq] = acc[pl.ds(pl.multiple_of(qq * 2 * n + c * n, 8), n), :]
                sendbuf[q] = acc[pl.ds(pl.multiple_of(qq * 2 * n + (1 - c) * n, 8), n), :]
                copy(q).start()

        if ng:
            @pl.when((j == nj - 1) & (t == nt // 2))
            def _():
                for a in range(ng):
                    gather.relay(a)

            @pl.when(last)
            def _():
                for a in range(ng):
                    gather.pass_on(a)
                gather.finish()

        @pl.when(last)
        def _():
            if ns:
                scatter.finish()
            for q in range(N_CHIPS):
                copy(q).wait_send()
                copy(q).wait_recv()

    shard4 = jax.ShapeDtypeStruct((N_CHIPS, n, m), F32)
    any_spec = pl.BlockSpec(memory_space=pl.ANY)
    return pl.pallas_call(
        body, name=name, grid=(nj, nt),
        in_specs=[pl.BlockSpec((tw, 2 * n * cpb), lambda j, t: (t, j)), pl.BlockSpec((tw, m), lambda j, t: (t, 0))]
                 + [any_spec] * (ng + ns),
        out_specs=[pl.BlockSpec((cpb, n, m), lambda j, t: (j, 0, 0)), any_spec] + [any_spec] * (ng + ns),
        out_shape=[shard4, shard4] + [_gathered_shape(s) for s in gather_rows]
                  + [_received_shape(p) for p in scatter_partials],
        scratch_shapes=[pltpu.VMEM((2 * n * cpb, m), F32), pltpu.VMEM((N_CHIPS, n, m), F32),
                        pltpu.SemaphoreType.DMA((N_CHIPS,)), pltpu.SemaphoreType.DMA((N_CHIPS,))]
                       + (_TwoLevelGather.scratch(ng) if ng else []) + (_ScatterToOwners.scratch(ns) if ns else []),
        compiler_params=pltpu.CompilerParams(dimension_semantics=("arbitrary", "arbitrary"),
                                             vmem_limit_bytes=VMEM_LIMIT_BYTES),
    )(lhs, rhs, *gather_rows, *scatter_partials)


class _ScatterToOwners:
    def __init__(self, ins, outs, send_sems, recv_sems):
        self.ins, self.outs, self.send_sems, self.recv_sems = ins, outs, send_sems, recv_sems
        x, y, c = _mesh_position()
        self.c, self.chips = c, _other_chips(x, y)

    def _copies(self):
        return [pltpu.make_async_remote_copy(
            src_ref=self.ins[a].at[2 * cx + cy], dst_ref=self.outs[a].at[j],
            send_sem=self.send_sems.at[a * 3 + j], recv_sem=self.recv_sems.at[a * 3 + j],
            device_id=(cx, cy, self.c), device_id_type=MESH)
            for a in range(len(self.ins)) for j, (cx, cy) in enumerate(self.chips)]

    def start(self):
        for cp in self._copies():
            cp.start()

    def finish(self):
        for cp in self._copies():
            cp.wait_send()
            cp.wait_recv()

    @staticmethod
    def scratch(na):
        return [pltpu.SemaphoreType.DMA((3 * na,)), pltpu.SemaphoreType.DMA((3 * na,))]


def _received_shape(p):
    return jax.ShapeDtypeStruct((3,) + p.shape[1:], p.dtype)


ROW_POOL_SCALE, ROW_SLN_G, ROW_SLN_B, ROW_SGU_B = 0, 1, 2, 3
ROW_LN2_G, ROW_LN2_B, ROW_LN1_G, ROW_LN1_B, ROW_LOSS = 8, 9, 10, 11, 12
VEC_ROWS = 16


def _mixer_bwd(dr1_bf, proj, wout, pool_w, pool_scale, sln_g, sln_b, sgu_w, sgu_b_t, stats, chip_partials, seq):
    tokens = dr1_bf.shape[0]
    tt = min(512, seq)
    tiles_per_seq = seq // tt
    nc = tt // GROUP
    n_halo_blocks = tokens // HALO
    n_tiles = tokens // tt
    n_part = len(chip_partials)

    def body(dr1_ref, dr1n_ref, proj_ref, projh_ref, wout_ref, pw_ref, ps_ref, lg_ref, lb_ref, sw_ref, sb_ref, stats_ref,
             *rest):
        part_refs, rest = rest[:n_part], rest[n_part:]
        dproj_ref, dmat_ref, dvec_ref = rest[:3]
        recv_refs, rest = rest[3:3 + n_part], rest[3 + n_part:]
        du_scr, dv_scr, send_sems, recv_sems = rest
        i = pl.program_id(0)
        tile_in_seq = i % tiles_per_seq
        scatter = _ScatterToOwners(part_refs, recv_refs, send_sems, recv_sems)

        @pl.when(i == 0)
        def _():
            scatter.start()
            dmat_ref[...] = jnp.zeros_like(dmat_ref)
            dvec_ref[0:8, :] = jnp.zeros((8, D_MODEL), F32)
            dvec_ref[8:16, :] = stats_ref[...]

        dr1b = dr1_ref[...]
        dmix = _dot_nt(dr1b, wout_ref[...])
        dpo_next = _dot_nt(dr1n_ref[...], wout_ref[0:POOL_WIDTH, :])
        dpo_next = jnp.where(tile_in_seq == tiles_per_seq - 1, 0.0, dpo_next)
        proj = proj_ref[...]
        xp_prev = jnp.where(tile_in_seq == 0, 0.0, projh_ref[...])
        pos = tile_in_seq * tt + lax.broadcasted_iota(jnp.int32, (tt, 1), 0)
        pos_next = (tile_in_seq + 1) * tt + lax.broadcasted_iota(jnp.int32, (HALO, 1), 0)

        for g, w in enumerate(POOL_WINDOWS):
            sl = slice(g * GROUP, (g + 1) * GROUP)
            inv_cnt = _inv_count(pos, w)
            pwb = pw_ref[g].astype(BF16)
            pooledb = _pool_fwd(proj[:, sl], xp_prev[:, sl], inv_cnt, w).astype(BF16)
            pre = _dot_nn(pooledb, pwb)
            dpo = dmix[:, sl]
            dvec_ref[ROW_POOL_SCALE:ROW_POOL_SCALE + 1, sl] += _rowsum(dpo * pre)
            dsb = (dpo * ps_ref[:, sl]).astype(BF16)
            dmat_ref[g] += _dot_tn(pooledb, dsb)
            dpooled = _dot_nt(dsb, pwb)
            dpooled_next = _dot_nt((dpo_next[:, sl] * ps_ref[:, sl]).astype(BF16), pwb)
            dxp = _pool_bwd(dpooled, dpooled_next, inv_cnt, _inv_count(pos_next, w), w)
            dproj_ref[:, sl] = dxp.astype(BF16)

        zu = proj[:, POOL_WIDTH:POOL_WIDTH + SGU_WIDTH]
        zv = proj[:, POOL_WIDTH + SGU_WIDTH:]
        u = _gelu(zu)
        vhat, rstd_v = _ln_stats(_gelu(zv))
        v_ln = vhat * lg_ref[...] + lb_ref[...]
        dsg = dmix[:, POOL_WIDTH:]
        row = lax.broadcasted_iota(jnp.int32, (GROUP, GROUP), 0)
        col = lax.broadcasted_iota(jnp.int32, (GROUP, GROUP), 1)
        for h in range(N_HEADS):
            ws = _masked_sgu_w(sw_ref, h).astype(BF16)
            vh = _to_head_major(v_ln, h, nc).astype(BF16)
            mixed = _dot_nn(ws, vh) + sb_ref[:, h:h + 1]
            dsg_h = _to_head_major(dsg, h, nc)
            du_h = dsg_h * mixed
            dm_h = dsg_h * _to_head_major(u, h, nc)
            pos_sums = lax.dot_general(jnp.ones((8, nc * GROUP), F32), dm_h, (((1,), (1,)), ((), ())),
                                       precision=lax.Precision.HIGHEST, preferred_element_type=F32)
            dvec_ref[ROW_SGU_B + h:ROW_SGU_B + h + 1, 0:GROUP] += pos_sums[0:1, :]
            dmb = dm_h.astype(BF16)
            dmat_ref[len(POOL_WINDOWS) + h] += jnp.where(row >= col, _dot_nt(dmb, vh), 0.0)
            dv_h = _dot_tn(ws, dmb)
            for c in range(nc):
                rs = slice(c * GROUP, (c + 1) * GROUP)
                cs = slice(h * GROUP, (h + 1) * GROUP)
                du_scr[rs, cs] = du_h[:, c * GROUP:(c + 1) * GROUP]
                dv_scr[rs, cs] = dv_h[:, c * GROUP:(c + 1) * GROUP]
        dv_ln = dv_scr[...]
        dvec_ref[ROW_SLN_B:ROW_SLN_B + 1, 0:SGU_WIDTH] += _rowsum(dv_ln)
        dvec_ref[ROW_SLN_G:ROW_SLN_G + 1, 0:SGU_WIDTH] += _rowsum(dv_ln * vhat)
        dv = _ln_bwd(dv_ln, vhat, rstd_v, lg_ref[...])
        dproj_ref[:, POOL_WIDTH:POOL_WIDTH + SGU_WIDTH] = (du_scr[...] * _gelu_grad(zu)).astype(BF16)
        dproj_ref[:, POOL_WIDTH + SGU_WIDTH:] = (dv * _gelu_grad(zv)).astype(BF16)

        @pl.when(i == n_tiles - 1)
        def _():
            scatter.finish()

    def tile(cols):
        return pl.BlockSpec((tt, cols), lambda i: (i, 0))

    def whole(a):
        nd = a.ndim
        return pl.BlockSpec(a.shape, lambda i: (0,) * nd)

    def resident(shape):
        nd = len(shape)
        return pl.BlockSpec(shape, lambda i: (0,) * nd)

    any_spec = pl.BlockSpec(memory_space=pl.ANY)
    next_halo = pl.BlockSpec((HALO, D_MODEL), lambda i: (jnp.minimum((i + 1) * (tt // HALO), n_halo_blocks - 1), 0))
    prev_halo = pl.BlockSpec((HALO, POOL_WIDTH), lambda i: (jnp.maximum(i * (tt // HALO) - 1, 0), 0))
    consts = [wout, pool_w, pool_scale, sln_g, sln_b, sgu_w, sgu_b_t, stats]
    small_shapes = [(len(POOL_WINDOWS) + N_HEADS, GROUP, GROUP), (VEC_ROWS, D_MODEL)]
    return pl.pallas_call(
        body, name="mixer_bwd", grid=(n_tiles,),
        in_specs=[tile(D_MODEL), next_halo, tile(IN_COLS), prev_halo] + [whole(a) for a in consts]
                 + [any_spec] * n_part,
        out_specs=[tile(IN_COLS)] + [resident(s) for s in small_shapes] + [any_spec] * n_part,
        out_shape=[jax.ShapeDtypeStruct((tokens, IN_COLS), BF16)]
                  + [jax.ShapeDtypeStruct(s, F32) for s in small_shapes] + [_received_shape(p) for p in chip_partials],
        scratch_shapes=[pltpu.VMEM((tt, SGU_WIDTH), F32), pltpu.VMEM((tt, SGU_WIDTH), F32)]
                       + _ScatterToOwners.scratch(n_part),
        compiler_params=pltpu.CompilerParams(dimension_semantics=("arbitrary",), vmem_limit_bytes=VMEM_LIMIT_BYTES),
    )(dr1_bf, dr1_bf, proj, proj, *consts, *chip_partials)


def _in_proj_bwd_x(dr1, dproj_bf, win_t):
    tokens = dr1.shape[0]
    tt = min(512, tokens)

    def body(dr1_ref, dproj_ref, win_ref, gx_ref):
        gx_ref[...] = ALPHA * dr1_ref[...] + _dot_nn(dproj_ref[...], win_ref[...])

    return pl.pallas_call(
        body, name="in_proj_bwd_x", grid=(tokens // tt,),
        in_specs=[pl.BlockSpec((tt, D_MODEL), lambda i: (i, 0)), pl.BlockSpec((tt, IN_COLS), lambda i: (i, 0)),
                  pl.BlockSpec(win_t.shape, lambda i: (0, 0))],
        out_specs=pl.BlockSpec((tt, D_MODEL), lambda i: (i, 0)),
        out_shape=jax.ShapeDtypeStruct((tokens, D_MODEL), F32),
        compiler_params=pltpu.CompilerParams(dimension_semantics=("arbitrary",), vmem_limit_bytes=VMEM_LIMIT_BYTES),
    )(dr1, dproj_bf, win_t)


def _chip_sums(kept, from_sibling, place, name):
    na = len(kept)

    def body(place_ref, *refs):
        kept_refs, sib_refs = refs[:na], refs[na:2 * na]
        bf_refs, own_refs = refs[2 * na:3 * na], refs[3 * na:]
        q = pl.program_id(0)
        for a in range(na):
            s = kept_refs[a][...] + sib_refs[a][...]
            bf_refs[a][...] = s.astype(BF16)

            @pl.when(q == place_ref[1])
            def _(a=a, s=s):
                own_refs[a][...] = s

    by_chip = [pl.BlockSpec((None,) + k.shape[1:], lambda q, pr: (q, 0, 0)) for k in kept]
    grid_spec = pltpu.PrefetchScalarGridSpec(
        num_scalar_prefetch=1, grid=(N_CHIPS,), in_specs=by_chip + by_chip,
        out_specs=by_chip + [pl.BlockSpec(k.shape[1:], lambda q, pr: (0, 0)) for k in kept])
    return pl.pallas_call(
        body, name=name, grid_spec=grid_spec,
        out_shape=[jax.ShapeDtypeStruct(k.shape, BF16) for k in kept]
                  + [jax.ShapeDtypeStruct(k.shape[1:], F32) for k in kept],
        compiler_params=pltpu.CompilerParams(dimension_semantics=("arbitrary",), vmem_limit_bytes=VMEM_LIMIT_BYTES),
    )(place, *kept, *from_sibling)


def _send_to_owners(chip_partials, name):
    na = len(chip_partials)

    def body(*refs):
        scatter = _ScatterToOwners(refs[:na], refs[na:2 * na], *refs[2 * na:])
        scatter.start()
        scatter.finish()

    any_spec = pl.BlockSpec(memory_space=pl.ANY)
    return pl.pallas_call(
        body, name=name, out_shape=[_received_shape(p) for p in chip_partials],
        in_specs=[any_spec] * na, out_specs=[any_spec] * na, scratch_shapes=_ScatterToOwners.scratch(na),
    )(*chip_partials)


def _adamw_math(w, g, m, v):
    m = ADAM_B1 * m + (1.0 - ADAM_B1) * g
    v = ADAM_B2 * v + (1.0 - ADAM_B2) * (g * g)
    m_hat = m / (1.0 - ADAM_B1 ** ADAM_STEP)
    v_hat = v / (1.0 - ADAM_B2 ** ADAM_STEP)
    delta = -ADAM_LR * (m_hat / (jnp.sqrt(v_hat) + ADAM_EPS) + ADAM_WD * w)
    return delta, m, v


def _sum_adamw(w, m, v, own, received, name):
    rows, cols = w.shape
    rb = _row_block(rows, 256)

    def body(w_ref, m_ref, v_ref, own_ref, rec_ref, g_ref, d_ref, mo_ref, vo_ref):
        g = own_ref[...]
        for j in range(3):
            g = g + rec_ref[j].astype(F32)
        g_ref[...] = g
        d_ref[...], mo_ref[...], vo_ref[...] = _adamw_math(w_ref[...], g, m_ref[...], v_ref[...])

    spec = pl.BlockSpec((rb, cols), lambda r: (r, 0))
    return pl.pallas_call(
        body, name=name, grid=(rows // rb,),
        in_specs=[spec] * 4 + [pl.BlockSpec((3, rb, cols), lambda r: (0, r, 0))], out_specs=[spec] * 4,
        out_shape=[jax.ShapeDtypeStruct((rows, cols), F32)] * 4,
        compiler_params=pltpu.CompilerParams(dimension_semantics=("arbitrary",)),
    )(w, m, v, own, received)


SMALL_NAMES = ("pool_w", "sgu_w", "pool_scale", "sgu_ln_g", "sgu_ln_b", "sgu_b", "ln1_g", "ln1_b", "ln2_g", "ln2_b")
_SMALL_VEC_ROWS = {"pool_scale": (ROW_POOL_SCALE, POOL_WIDTH), "sgu_ln_g": (ROW_SLN_G, SGU_WIDTH),
                   "sgu_ln_b": (ROW_SLN_B, SGU_WIDTH), "ln1_g": (ROW_LN1_G, D_MODEL), "ln1_b": (ROW_LN1_B, D_MODEL),
                   "ln2_g": (ROW_LN2_G, D_MODEL), "ln2_b": (ROW_LN2_B, D_MODEL)}
_SMALL_MAT_FIRST = {"pool_w": 0, "sgu_w": len(POOL_WINDOWS)}


def _small_sum_adamw(mats_all, vecs_all, w, m, v):
    n = len(SMALL_NAMES)

    def body(mats_ref, vecs_ref, *refs):
        w_refs, m_refs, v_refs = refs[:n], refs[n:2 * n], refs[2 * n:3 * n]
        loss_ref = refs[3 * n]
        g_refs, d_refs, mo_refs, vo_refs = (refs[3 * n + 1 + k * n:3 * n + 1 + (k + 1) * n] for k in range(4))
        vec_scr = refs[7 * n + 1]

        def update(k, idx, g):
            d, mo, vo = _adamw_math(w_refs[k][idx], g, m_refs[k][idx], v_refs[k][idx])
            g_refs[k][idx], d_refs[k][idx], mo_refs[k][idx], vo_refs[k][idx] = g, d, mo, vo

        total = vecs_ref[0]
        for dev in range(1, N_DEV):
            total = total + vecs_ref[dev]
        vec_scr[...] = total
        for k, name in enumerate(SMALL_NAMES):
            if name in _SMALL_MAT_FIRST:
                for b in range(4):
                    g = mats_ref[0, _SMALL_MAT_FIRST[name] + b]
                    for dev in range(1, N_DEV):
                        g = g + mats_ref[dev, _SMALL_MAT_FIRST[name] + b]
                    update(k, (0, b), g)
            elif name == "sgu_b":
                update(k, (0,), vec_scr[ROW_SGU_B:ROW_SGU_B + N_HEADS, 0:GROUP])
            else:
                row, width = _SMALL_VEC_ROWS[name]
                update(k, (slice(None), slice(None)), vec_scr[row:row + 1, 0:width])
        loss = jnp.sum(vec_scr[ROW_LOSS:ROW_LOSS + 1, :], axis=1, keepdims=True) * (0.5 / D_MODEL)
        loss_ref[...] = jnp.broadcast_to(loss, loss_ref.shape)

    vmem = pl.BlockSpec(memory_space=pltpu.VMEM)
    shapes = [jax.ShapeDtypeStruct(w[k].shape, F32) for k in SMALL_NAMES]
    outs = pl.pallas_call(
        body, name="small_sum_adamw",
        out_shape=[jax.ShapeDtypeStruct((8, GROUP), F32)] + shapes * 4,
        in_specs=[vmem] * (2 + 3 * n), out_specs=[vmem] * (1 + 4 * n),
        scratch_shapes=[pltpu.VMEM((VEC_ROWS, D_MODEL), F32)],
    )(mats_all, vecs_all, *[w[k] for k in SMALL_NAMES], *[m[k] for k in SMALL_NAMES], *[v[k] for k in SMALL_NAMES])
    return outs[0], outs[1:1 + n], outs[1 + n:1 + 2 * n], outs[1 + 2 * n:1 + 3 * n], outs[1 + 3 * n:]


def kernel(x, w_in, pool_w, pool_scale, sgu_ln_g, sgu_ln_b, sgu_w, sgu_b, w_out, ln1_g, ln1_b, w_gate_up, w_down, ln2_g, ln2_b, loss_target, m_w_in, m_pool_w, m_pool_scale, m_sgu_ln_g, m_sgu_ln_b, m_sgu_w, m_sgu_b, m_w_out, m_ln1_g, m_ln1_b, m_w_gate_up, m_w_down, m_ln2_g, m_ln2_b, v_w_in, v_pool_w, v_pool_scale, v_sgu_ln_g, v_sgu_ln_b, v_sgu_w, v_sgu_b, v_w_out, v_ln1_g, v_ln1_b, v_w_gate_up, v_w_down, v_ln2_g, v_ln2_b):
    bl, seq, _ = x.shape
    tokens = bl * seq
    x2d = x.reshape(tokens, D_MODEL)
    tgt2d = loss_target.reshape(tokens, D_MODEL)
    my_c = lax.axis_index("c")
    place = jnp.stack([my_c, 2 * lax.axis_index("x") + lax.axis_index("y")]).astype(jnp.int32)

    win_t, wout = _all_gather_rows([w_in[0].T.astype(BF16), w_out[0].astype(BF16)], "weight_all_gather")

    pool_w3, sgu_w3 = pool_w[0], sgu_w[0]
    sgu_b_t = sgu_b[0].T
    proj, xhat1, rstd1, mix_bf, x_bf, wgu_t, wdown = _mixer_fwd(
        x2d, win_t, wout, pool_w3, pool_scale, sgu_ln_g, sgu_ln_b, sgu_w3, sgu_b_t, ln1_g, ln1_b,
        [w_gate_up[0].T.astype(BF16), w_down[0].astype(BF16)], seq)
    dr1, dr1_bf, h_bf, dr2_bf, a_bf, dgu_bf, stats = _ffn_fwd_bwd(
        xhat1, rstd1, tgt2d, wgu_t, wdown, ln1_g, ln1_b, ln2_g, ln2_b)

    kept_gu, sib_gu = _wgrad_exchange(dgu_bf, h_bf, 1, "wgrad_gate_up")
    bf_gu, own_gu = _chip_sums([kept_gu], [sib_gu], place, "chip_sum_w_gate_up")
    kept_dn, sib_dn, rec_gu = _wgrad_exchange(a_bf, dr2_bf, 2, "wgrad_down", scatter_partials=[bf_gu])
    kept_out, sib_out = _wgrad_exchange(mix_bf, dr1_bf, N_CHIPS, "wgrad_out")
    bf_dn, bf_out, own_dn, own_out = _chip_sums([kept_dn, kept_out], [sib_dn, sib_out], place, "chip_sums")
    dproj_bf, d_mats, d_vecs, rec_dn, rec_out = _mixer_bwd(
        dr1_bf, proj, wout, pool_w3, pool_scale, sgu_ln_g, sgu_ln_b, sgu_w3, sgu_b_t, stats, [bf_dn, bf_out], seq)
    gx = _in_proj_bwd_x(dr1, dproj_bf, win_t)
    kept_in, sib_in, mats_all, vecs_all = _wgrad_exchange(
        dproj_bf, x_bf, N_CHIPS, "wgrad_in", gather_rows=[d_mats.reshape(-1, GROUP), d_vecs])
    bf_in, own_in = _chip_sums([kept_in], [sib_in], place, "chip_sum_w_in")
    rec_in, = _send_to_owners([bf_in], "grad_scatter_w_in")

    grads, deltas, new_m, new_v = {}, {}, {}, {}
    for nm, w, m, v, own, rec, transposed in (("w_out", w_out, m_w_out, v_w_out, own_out, rec_out, False),
                                              ("w_down", w_down, m_w_down, v_w_down, own_dn, rec_dn, False),
                                              ("w_in", w_in, m_w_in, v_w_in, own_in, rec_in, True),
                                              ("w_gate_up", w_gate_up, m_w_gate_up, v_w_gate_up, own_gu, rec_gu, True)):
        rows = (lambda a: a[0].T) if transposed else (lambda a: a[0])
        back = (lambda a: a.T[None]) if transposed else (lambda a: a[None])
        g, d, mo, vo = _sum_adamw(rows(w), rows(m), rows(v), own, rec, "adamw_" + nm)
        grads[nm], deltas[nm], new_m[nm], new_v[nm] = back(g), back(d), back(mo), back(vo)

    small_w = {"pool_w": pool_w, "pool_scale": pool_scale, "sgu_ln_g": sgu_ln_g, "sgu_ln_b": sgu_ln_b, "sgu_w": sgu_w,
               "sgu_b": sgu_b, "ln1_g": ln1_g, "ln1_b": ln1_b, "ln2_g": ln2_g, "ln2_b": ln2_b}
    small_m = {"pool_w": m_pool_w, "pool_scale": m_pool_scale, "sgu_ln_g": m_sgu_ln_g, "sgu_ln_b": m_sgu_ln_b,
               "sgu_w": m_sgu_w, "sgu_b": m_sgu_b, "ln1_g": m_ln1_g, "ln1_b": m_ln1_b, "ln2_g": m_ln2_g, "ln2_b": m_ln2_b}
    small_v = {"pool_w": v_pool_w, "pool_scale": v_pool_scale, "sgu_ln_g": v_sgu_ln_g, "sgu_ln_b": v_sgu_ln_b,
               "sgu_w": v_sgu_w, "sgu_b": v_sgu_b, "ln1_g": v_ln1_g, "ln1_b": v_ln1_b, "ln2_g": v_ln2_g, "ln2_b": v_ln2_b}
    loss_blk, g_small, d_small, m_small, v_small = _small_sum_adamw(
        mats_all.reshape(N_DEV, 2 * N_HEADS, GROUP, GROUP), vecs_all.reshape(N_DEV, VEC_ROWS, D_MODEL),
        small_w, small_m, small_v)
    for vals, dst in ((g_small, grads), (d_small, deltas), (m_small, new_m), (v_small, new_v)):
        dst.update(zip(SMALL_NAMES, vals))

    order = ["w_in", "pool_w", "pool_scale", "sgu_ln_g", "sgu_ln_b", "sgu_w", "sgu_b", "w_out", "ln1_g", "ln1_b",
             "w_gate_up", "w_down", "ln2_g", "ln2_b"]
    return (loss_blk[0, 0], gx.reshape(bl, seq, D_MODEL), *[grads[k] for k in order], *[deltas[k] for k in order],
            *[new_m[k] for k in order], *[new_v[k] for k in order])
```

```python
import functools

import jax
import jax.numpy as jnp
from jax import lax
from jax.experimental import pallas as pl
from jax.experimental.pallas import tpu as pltpu

F32 = jnp.float32
BF16 = jnp.bfloat16
MESH = pl.DeviceIdType.MESH

D_MODEL = 1024
POOL_WIDTH = 512
SGU_WIDTH = 512
POOL_WINDOWS = (2, 4, 8, 16)
GROUP = 128
N_HEADS = 4
IN_COLS = POOL_WIDTH + 2 * SGU_WIDTH
D_FF = 2816
FF_CHUNK = 2816
LN_EPS = 1e-5
ALPHA = float(2.0 ** 0.25)
HALO = 16
N_DEV = 8
N_CHIPS = 4

ADAM_LR = 0.001
ADAM_B1 = 0.9
ADAM_B2 = 0.999
ADAM_EPS = 1e-08
ADAM_WD = 0.01
ADAM_STEP = 10

VMEM_LIMIT_BYTES = 56 * 1024 * 1024

_SQRT_HALF = 0.7071067811865476
_INV_SQRT_2PI = 0.3989422804014327


def _dot_nn(a, b):
    return lax.dot_general(a, b, (((1,), (0,)), ((), ())), preferred_element_type=F32)


def _dot_nt(a, b):
    return lax.dot_general(a, b, (((1,), (1,)), ((), ())), preferred_element_type=F32)


def _dot_tn(a, b):
    return lax.dot_general(a, b, (((0,), (0,)), ((), ())), preferred_element_type=F32)


def _gelu(x):
    return 0.5 * x * (1.0 + lax.erf(x * _SQRT_HALF))


def _gelu_grad(x):
    return 0.5 * (1.0 + lax.erf(x * _SQRT_HALF)) + x * jnp.exp(-0.5 * x * x) * _INV_SQRT_2PI


def _ln_stats(r):
    mu = jnp.mean(r, axis=-1, keepdims=True)
    d = r - mu
    var = jnp.mean(d * d, axis=-1, keepdims=True)
    rstd = lax.rsqrt(var + LN_EPS)
    return d * rstd, rstd


def _ln_bwd(dout, xhat, rstd, g):
    dxh = dout * g
    m1 = jnp.mean(dxh, axis=-1, keepdims=True)
    m2 = jnp.mean(dxh * xhat, axis=-1, keepdims=True)
    return rstd * (dxh - m1 - xhat * m2)


def _rowsum(a):
    return jnp.sum(a, axis=0, keepdims=True)


def _pool_fwd(xp, xp_prev, inv_cnt, w):
    s = jnp.concatenate([xp_prev, xp], axis=0)
    k = 1
    while k < w:
        s = s + pltpu.roll(s, k, 0)
        k *= 2
    return s[HALO:, :] * inv_cnt - xp


def _pool_bwd(dpooled, dpooled_next, inv_cnt, inv_cnt_next, w):
    n = dpooled.shape[0] + HALO
    s = jnp.concatenate([dpooled * inv_cnt, dpooled_next * inv_cnt_next], axis=0)
    k = 1
    while k < w:
        s = s + pltpu.roll(s, n - k, 0)
        k *= 2
    return s[: dpooled.shape[0], :] - dpooled


def _inv_count(pos, w):
    return 1.0 / jnp.minimum(pos + 1, w).astype(F32)


def _to_head_major(a, h, nc):
    return jnp.concatenate(
        [a[c * GROUP:(c + 1) * GROUP, h * GROUP:(h + 1) * GROUP] for c in range(nc)], axis=1)


def _masked_sgu_w(sw_ref, h):
    row = lax.broadcasted_iota(jnp.int32, (GROUP, GROUP), 0)
    col = lax.broadcasted_iota(jnp.int32, (GROUP, GROUP), 1)
    return jnp.where(row >= col, sw_ref[h], 0.0)


def _row_block(rows, limit):
    return max(b for b in range(16, min(rows, limit) + 1, 16) if rows % b == 0)


def _mesh_position():
    return lax.axis_index("x"), lax.axis_index("y"), lax.axis_index("c")


def _other_chips(x, y):
    return [(1 - x, y), (x, 1 - y), (1 - x, 1 - y)]


class _TwoLevelGather:
    def __init__(self, ins, outs, send_sems, recv_sems, local_sems):
        self.ins, self.outs = ins, outs
        self.send_sems, self.recv_sems, self.local_sems = send_sems, recv_sems, local_sems
        self.na = len(ins)
        x, y, c = _mesh_position()
        self.c = c
        self.me, self.sibling = (x, y, c), (x, y, 1 - c)
        self.chips = _other_chips(x, y)
        self.relay_from = (x + (1 - c) * (1 - 2 * x), y + c * (1 - 2 * y))
        self.relay_to = (x + c * (1 - 2 * x), y + (1 - c) * (1 - 2 * y))

    def _rows(self, a, px, py, pc):
        n = self.ins[a].shape[0]
        return self.outs[a].at[pl.ds((4 * px + 2 * py + pc) * n, n), :]

    def _copy(self, a, k, block, to, src=None):
        return pltpu.make_async_remote_copy(
            src_ref=self._rows(a, *block) if src is None else src, dst_ref=self._rows(a, *block),
            send_sem=self.send_sems.at[a * 7 + k], recv_sem=self.recv_sems.at[a * 7 + k],
            device_id=to, device_id_type=MESH)

    def _mine(self, a):
        return pltpu.make_async_copy(self.ins[a], self._rows(a, *self.me), self.local_sems.at[a])

    def start(self):
        for a in range(self.na):
            self._mine(a).start()
        for a in range(self.na):
            self._copy(a, 0, self.me, self.sibling, src=self.ins[a]).start()
            for j, chip in enumerate(self.chips[:2]):
                self._copy(a, 1 + j, self.me, (*chip, self.c), src=self.ins[a]).start()

    def relay(self, a):
        c, block = self.c, (*self.relay_from, self.c)
        self._copy(a, 1 + c, block, self.me).wait_recv()
        self._copy(a, 3, block, (*self.relay_to, c)).start()
        self._copy(a, 4 + c, block, self.sibling).start()

    def pass_on(self, a):
        c = self.c
        self._copy(a, 2 - c, (*self.relay_to, c), self.me).wait_recv()
        self._copy(a, 5 - c, (*self.relay_to, c), self.sibling).start()
        self._copy(a, 3, (*self.chips[2], c), self.me).wait_recv()
        self._copy(a, 6, (*self.chips[2], c), self.sibling).start()

    def finish(self):
        for a in range(self.na):
            self._copy(a, 0, self.sibling, self.me).wait_recv()
            for j, chip in enumerate(self.chips):
                self._copy(a, 4 + j, (*chip, 1 - self.c), self.me).wait_recv()
        for a in range(self.na):
            for k in range(7):
                self._copy(a, k, self.me, self.sibling, src=self.ins[a]).wait_send()
            self._mine(a).wait()

    @staticmethod
    def scratch(na):
        return [pltpu.SemaphoreType.DMA((7 * na,)), pltpu.SemaphoreType.DMA((7 * na,)), pltpu.SemaphoreType.DMA((na,))]


def _gathered_shape(s):
    return jax.ShapeDtypeStruct((N_DEV * s.shape[0], s.shape[1]), s.dtype)


def _all_gather_rows(shards, name):
    na = len(shards)

    def body(*refs):
        gather = _TwoLevelGather(refs[:na], refs[na:2 * na], *refs[2 * na:])
        gather.start()
        for a in range(na):
            gather.relay(a)
        for a in range(na):
            gather.pass_on(a)
        gather.finish()

    any_spec = pl.BlockSpec(memory_space=pl.ANY)
    return pl.pallas_call(
        body, name=name, out_shape=[_gathered_shape(s) for s in shards],
        in_specs=[any_spec] * na, out_specs=[any_spec] * na, scratch_shapes=_TwoLevelGather.scratch(na),
    )(*shards)


def _mixer_fwd(x2d, win_t, wout, pool_w, pool_scale, sln_g, sln_b, sgu_w, sgu_b_t, ln1_g, ln1_b, later_shards, seq):
    tokens = x2d.shape[0]
    tt = min(512, seq)
    tiles_per_seq = seq // tt
    nc = tt // GROUP
    n_tiles = tokens // tt
    n_later = len(later_shards)

    def body(x_ref, xh_ref, win_ref, wout_ref, pw_ref, ps_ref, lg_ref, lb_ref, sw_ref, sb_ref, g1_ref, b1_ref, *rest):
        shard_refs, rest = rest[:n_later], rest[n_later:]
        proj_ref, xhat_ref, rstd_ref, mix_ref, xbf_ref = rest[:5]
        gathered_refs, rest = rest[5:5 + n_later], rest[5 + n_later:]
        mix_scr, send_sems, recv_sems, local_sems = rest
        i = pl.program_id(0)
        gather = _TwoLevelGather(shard_refs, gathered_refs, send_sems, recv_sems, local_sems)

        @pl.when(i == 0)
        def _():
            gather.start()

        tile_in_seq = i % tiles_per_seq
        x = x_ref[...]
        xb = x.astype(BF16)
        xbf_ref[...] = xb
        proj = _dot_nt(xb, win_ref[...])
        proj_ref[...] = proj
        xp_prev = _dot_nt(xh_ref[...].astype(BF16), win_ref[0:POOL_WIDTH, :])
        xp_prev = jnp.where(tile_in_seq == 0, 0.0, xp_prev)
        pos = tile_in_seq * tt + lax.broadcasted_iota(jnp.int32, (tt, 1), 0)
        for g, w in enumerate(POOL_WINDOWS):
            sl = slice(g * GROUP, (g + 1) * GROUP)
            pooled = _pool_fwd(proj[:, sl], xp_prev[:, sl], _inv_count(pos, w), w)
            pre = _dot_nn(pooled.astype(BF16), pw_ref[g].astype(BF16))
            mix_scr[:, sl] = pre * ps_ref[:, sl]
        u = _gelu(proj[:, POOL_WIDTH:POOL_WIDTH + SGU_WIDTH])
        v = _gelu(proj[:, POOL_WIDTH + SGU_WIDTH:])
        vhat, _ = _ln_stats(v)
        v_ln = vhat * lg_ref[...] + lb_ref[...]
        for h in range(N_HEADS):
            ws = _masked_sgu_w(sw_ref, h).astype(BF16)
            mixed = _dot_nn(ws, _to_head_major(v_ln, h, nc).astype(BF16)) + sb_ref[:, h:h + 1]
            for c in range(nc):
                rs = slice(c * GROUP, (c + 1) * GROUP)
                mix_scr[rs, POOL_WIDTH + h * GROUP:POOL_WIDTH + (h + 1) * GROUP] = (
                    u[rs, h * GROUP:(h + 1) * GROUP] * mixed[:, c * GROUP:(c + 1) * GROUP])
        mixb = mix_scr[...].astype(BF16)
        mix_ref[...] = mixb
        r1 = ALPHA * x + _dot_nn(mixb, wout_ref[...])
        xhat, rstd = _ln_stats(r1)
        xhat_ref[...] = xhat
        rstd_ref[...] = rstd

        for a in range(n_later):
            relay_tile = min(n_tiles // 2 + a, n_tiles - 1)

            @pl.when(i == relay_tile)
            def _(a=a):
                gather.relay(a)

            @pl.when(i == max(n_tiles - n_later + a, relay_tile))
            def _(a=a):
                gather.pass_on(a)

        @pl.when(i == n_tiles - 1)
        def _():
            gather.finish()

    def tile(cols):
        return pl.BlockSpec((tt, cols), lambda i: (i, 0))

    def whole(a):
        nd = a.ndim
        return pl.BlockSpec(a.shape, lambda i: (0,) * nd)

    any_spec = pl.BlockSpec(memory_space=pl.ANY)
    halo = pl.BlockSpec((HALO, D_MODEL), lambda i: (jnp.maximum(i * (tt // HALO) - 1, 0), 0))
    consts = [win_t, wout, pool_w, pool_scale, sln_g, sln_b, sgu_w, sgu_b_t, ln1_g, ln1_b]
    return pl.pallas_call(
        body, name="mixer_fwd", grid=(n_tiles,),
        in_specs=[tile(D_MODEL), halo] + [whole(a) for a in consts] + [any_spec] * n_later,
        out_specs=[tile(IN_COLS), tile(D_MODEL), tile(1), tile(D_MODEL), tile(D_MODEL)] + [any_spec] * n_later,
        out_shape=[jax.ShapeDtypeStruct((tokens, IN_COLS), F32), jax.ShapeDtypeStruct((tokens, D_MODEL), F32),
                   jax.ShapeDtypeStruct((tokens, 1), F32), jax.ShapeDtypeStruct((tokens, D_MODEL), BF16),
                   jax.ShapeDtypeStruct((tokens, D_MODEL), BF16)] + [_gathered_shape(s) for s in later_shards],
        scratch_shapes=[pltpu.VMEM((tt, D_MODEL), F32)] + _TwoLevelGather.scratch(n_later),
        compiler_params=pltpu.CompilerParams(dimension_semantics=("arbitrary",), vmem_limit_bytes=VMEM_LIMIT_BYTES),
    )(x2d, x2d, *consts, *later_shards)


def _ffn_fwd_bwd(xhat1, rstd1, target, wgu_t, wdown, ln1_g, ln1_b, ln2_g, ln2_b):
    tokens = xhat1.shape[0]
    tt = min(256, tokens)
    n_chunks = D_FF // FF_CHUNK

    def body(xhat_ref, rstd_ref, tgt_ref, wgu_hbm, wd_hbm, g1_ref, b1_ref, g2_ref, b2_ref,
             dr1_ref, dr1bf_ref, hbf_ref, dr2bf_ref, a_ref, dgu_ref, stats_ref, wgu_ref, wd_ref, gu_scr, sems):
        i = pl.program_id(0)

        @pl.when(i == 0)
        def _():
            loads = [pltpu.make_async_copy(wgu_hbm, wgu_ref, sems.at[0]),
                     pltpu.make_async_copy(wd_hbm, wd_ref, sems.at[1])]
            for cp in loads:
                cp.start()
            stats_ref[...] = jnp.zeros_like(stats_ref)
            for cp in loads:
                cp.wait()

        xhat1_t = xhat_ref[...]
        h = xhat1_t * g1_ref[...] + b1_ref[...]
        hb = h.astype(BF16)
        hbf_ref[...] = hb
        f = jnp.zeros((tt, D_MODEL), F32)
        for k in range(n_chunks):
            rg = slice(k * FF_CHUNK, (k + 1) * FF_CHUNK)
            ru = slice(D_FF + k * FF_CHUNK, D_FF + (k + 1) * FF_CHUNK)
            gate = _dot_nt(hb, wgu_ref[rg, :])
            up = _dot_nt(hb, wgu_ref[ru, :])
            gu_scr[:, rg] = gate
            gu_scr[:, ru] = up
            ab = (gate * jax.nn.sigmoid(gate) * up).astype(BF16)
            a_ref[:, rg] = ab
            f = f + _dot_nn(ab, wd_ref[rg, :])
        xhat2, rstd2 = _ln_stats(ALPHA * h + f)
        err = xhat2 * g2_ref[...] + b2_ref[...] - tgt_ref[...]
        dy = err * (1.0 / D_MODEL)
        stats_ref[0:1, :] += _rowsum(dy * xhat2)
        stats_ref[1:2, :] += _rowsum(dy)
        stats_ref[4:5, :] += _rowsum(err * err)
        dr2 = _ln_bwd(dy, xhat2, rstd2, g2_ref[...])
        dr2b = dr2.astype(BF16)
        dr2bf_ref[...] = dr2b
        dh = ALPHA * dr2
        for k in range(n_chunks):
            rg = slice(k * FF_CHUNK, (k + 1) * FF_CHUNK)
            ru = slice(D_FF + k * FF_CHUNK, D_FF + (k + 1) * FF_CHUNK)
            da = _dot_nt(dr2b, wd_ref[rg, :])
            gate = gu_scr[:, rg]
            up = gu_scr[:, ru]
            sg = jax.nn.sigmoid(gate)
            dgate = (da * up * (sg * (1.0 + gate * (1.0 - sg)))).astype(BF16)
            dup = (da * (gate * sg)).astype(BF16)
            dgu_ref[:, rg] = dgate
            dgu_ref[:, ru] = dup
            dh = dh + _dot_nn(dgate, wgu_ref[rg, :]) + _dot_nn(dup, wgu_ref[ru, :])
        stats_ref[2:3, :] += _rowsum(dh * xhat1_t)
        stats_ref[3:4, :] += _rowsum(dh)
        dr1 = _ln_bwd(dh, xhat1_t, rstd_ref[...], g1_ref[...])
        dr1_ref[...] = dr1
        dr1bf_ref[...] = dr1.astype(BF16)

    def tile(cols):
        return pl.BlockSpec((tt, cols), lambda i: (i, 0))

    def whole(a):
        nd = a.ndim
        return pl.BlockSpec(a.shape, lambda i: (0,) * nd)

    any_spec = pl.BlockSpec(memory_space=pl.ANY)
    vecs = [ln1_g, ln1_b, ln2_g, ln2_b]
    return pl.pallas_call(
        body, name="ffn_fwd_bwd", grid=(tokens // tt,),
        in_specs=[tile(D_MODEL), tile(1), tile(D_MODEL), any_spec, any_spec] + [whole(a) for a in vecs],
        out_specs=[tile(D_MODEL), tile(D_MODEL), tile(D_MODEL), tile(D_MODEL), tile(D_FF), tile(2 * D_FF),
                   pl.BlockSpec((8, D_MODEL), lambda i: (0, 0))],
        out_shape=[jax.ShapeDtypeStruct((tokens, D_MODEL), F32), jax.ShapeDtypeStruct((tokens, D_MODEL), BF16),
                   jax.ShapeDtypeStruct((tokens, D_MODEL), BF16),
                   jax.ShapeDtypeStruct((tokens, D_MODEL), BF16), jax.ShapeDtypeStruct((tokens, D_FF), BF16),
                   jax.ShapeDtypeStruct((tokens, 2 * D_FF), BF16), jax.ShapeDtypeStruct((8, D_MODEL), F32)],
        scratch_shapes=[pltpu.VMEM(wgu_t.shape, BF16), pltpu.VMEM(wdown.shape, BF16),
                        pltpu.VMEM((tt, 2 * D_FF), F32), pltpu.SemaphoreType.DMA((2,))],
        compiler_params=pltpu.CompilerParams(dimension_semantics=("arbitrary",), vmem_limit_bytes=VMEM_LIMIT_BYTES),
    )(xhat1, rstd1, target, wgu_t, wdown, *vecs)


def _wgrad_exchange(lhs, rhs, chips_per_block, name, gather_rows=(), scatter_partials=()):
    tokens, n_all = lhs.shape
    m = rhs.shape[1]
    n = n_all // N_DEV
    tw = min(2048, tokens)
    nt = tokens // tw
    cpb = chips_per_block
    nj = N_CHIPS // cpb
    ng, ns = len(gather_rows), len(scatter_partials)

    def body(l_ref, r_ref, *rest):
        small_refs, part_refs, rest = rest[:ng], rest[ng:ng + ns], rest[ng + ns:]
        kept_ref, sib_ref = rest[:2]
        gathered_refs, recv_refs, rest = rest[2:2 + ng], rest[2 + ng:2 + ng + ns], rest[2 + ng + ns:]
        acc, sendbuf, send_sems, recv_sems = rest[:4]
        j, t = pl.program_id(0), pl.program_id(1)
        first, last = (j == 0) & (t == 0), (j == nj - 1) & (t == nt - 1)
        x, y, c = _mesh_position()
        if ng:
            gather = _TwoLevelGather(small_refs, gathered_refs, *rest[4:7])

            @pl.when(first)
            def _():
                gather.start()
        if ns:
            scatter = _ScatterToOwners(part_refs, recv_refs, *rest[4 + 3 * bool(ng):])

            @pl.when(first)
            def _():
                scatter.start()

        def copy(q):
            return pltpu.make_async_remote_copy(
                src_ref=sendbuf.at[q], dst_ref=sib_ref.at[q], send_sem=send_sems.at[q], recv_sem=recv_sems.at[q],
                device_id=(x, y, 1 - c), device_id_type=MESH)

        @pl.when(t == 0)
        def _():
            acc[...] = jnp.zeros_like(acc)

        acc[...] += _dot_tn(l_ref[...], r_ref[...])

        @pl.when(t == nt - 1)
        def _():
            for qq in range(cpb):
                q = j * cpb + qq
                kept_ref[qq] = acc[pl.ds(pl.multiple_of(qq * 2 * n + c * n, 8), n), :]
                sendbuf[q] = acc[pl.ds(pl.multiple_of(qq * 2 * n + (1 - c) * n, 8), n), :]
                copy(q).start()

        if ng:
            @pl.when((j == nj - 1) & (t == nt // 2))
            def _():
                for a in range(ng):
                    gather.relay(a)

            @pl.when(last)
            def _():
                for a in range(ng):
                    gather.pass_on(a)
                gather.finish()

        @pl.when(last)
        def _():
            if ns:
                scatter.finish()
            for q in range(N_CHIPS):
                copy(q).wait_send()
                copy(q).wait_recv()

    shard4 = jax.ShapeDtypeStruct((N_CHIPS, n, m), F32)
    any_spec = pl.BlockSpec(memory_space=pl.ANY)
    return pl.pallas_call(
        body, name=name, grid=(nj, nt),
        in_specs=[pl.BlockSpec((tw, 2 * n * cpb), lambda j, t: (t, j)), pl.BlockSpec((tw, m), lambda j, t: (t, 0))]
                 + [any_spec] * (ng + ns),
        out_specs=[pl.BlockSpec((cpb, n, m), lambda j, t: (j, 0, 0)), any_spec] + [any_spec] * (ng + ns),
        out_shape=[shard4, shard4] + [_gathered_shape(s) for s in gather_rows]
                  + [_received_shape(p) for p in scatter_partials],
        scratch_shapes=[pltpu.VMEM((2 * n * cpb, m), F32), pltpu.VMEM((N_CHIPS, n, m), F32),
                        pltpu.SemaphoreType.DMA((N_CHIPS,)), pltpu.SemaphoreType.DMA((N_CHIPS,))]
                       + (_TwoLevelGather.scratch(ng) if ng else []) + (_ScatterToOwners.scratch(ns) if ns else []),
        compiler_params=pltpu.CompilerParams(dimension_semantics=("arbitrary", "arbitrary"),
                                             vmem_limit_bytes=VMEM_LIMIT_BYTES),
    )(lhs, rhs, *gather_rows, *scatter_partials)


class _ScatterToOwners:
    def __init__(self, ins, outs, send_sems, recv_sems):
        self.ins, self.outs, self.send_sems, self.recv_sems = ins, outs, send_sems, recv_sems
        x, y, c = _mesh_position()
        self.c, self.chips = c, _other_chips(x, y)

    def _copies(self):
        return [pltpu.make_async_remote_copy(
            src_ref=self.ins[a].at[2 * cx + cy], dst_ref=self.outs[a].at[j],
            send_sem=self.send_sems.at[a * 3 + j], recv_sem=self.recv_sems.at[a * 3 + j],
            device_id=(cx, cy, self.c), device_id_type=MESH)
            for a in range(len(self.ins)) for j, (cx, cy) in enumerate(self.chips)]

    def start(self):
        for cp in self._copies():
            cp.start()

    def finish(self):
        for cp in self._copies():
            cp.wait_send()
            cp.wait_recv()

    @staticmethod
    def scratch(na):
        return [pltpu.SemaphoreType.DMA((3 * na,)), pltpu.SemaphoreType.DMA((3 * na,))]


def _received_shape(p):
    return jax.ShapeDtypeStruct((3,) + p.shape[1:], p.dtype)


ROW_POOL_SCALE, ROW_SLN_G, ROW_SLN_B, ROW_SGU_B = 0, 1, 2, 3
ROW_LN2_G, ROW_LN2_B, ROW_LN1_G, ROW_LN1_B, ROW_LOSS = 8, 9, 10, 11, 12
VEC_ROWS = 16


def _mixer_bwd(dr1, proj, win_t, wout, pool_w, pool_scale, sln_g, sln_b, sgu_w, sgu_b_t, stats, seq):
    tokens = dr1.shape[0]
    tt = min(512, seq)
    tiles_per_seq = seq // tt
    nc = tt // GROUP
    n_halo_blocks = tokens // HALO
    n_tiles = tokens // tt

    def body(dr1_ref, dr1n_ref, proj_ref, projh_ref, win_ref, wout_ref, pw_ref, ps_ref, lg_ref, lb_ref, sw_ref, sb_ref,
             stats_ref, gx_ref, dproj_ref, dmat_ref, dvec_ref, du_scr, dv_scr):
        i = pl.program_id(0)
        tile_in_seq = i % tiles_per_seq

        @pl.when(i == 0)
        def _():
            dmat_ref[...] = jnp.zeros_like(dmat_ref)
            dvec_ref[0:8, :] = jnp.zeros((8, D_MODEL), F32)
            dvec_ref[8:16, :] = stats_ref[...]

        dr1_t = dr1_ref[...]
        dr1b = dr1_t.astype(BF16)
        dmix = _dot_nt(dr1b, wout_ref[...])
        dpo_next = _dot_nt(dr1n_ref[...].astype(BF16), wout_ref[0:POOL_WIDTH, :])
        dpo_next = jnp.where(tile_in_seq == tiles_per_seq - 1, 0.0, dpo_next)
        proj = proj_ref[...]
        xp_prev = jnp.where(tile_in_seq == 0, 0.0, projh_ref[...])
        pos = tile_in_seq * tt + lax.broadcasted_iota(jnp.int32, (tt, 1), 0)
        pos_next = (tile_in_seq + 1) * tt + lax.broadcasted_iota(jnp.int32, (HALO, 1), 0)

        for g, w in enumerate(POOL_WINDOWS):
            sl = slice(g * GROUP, (g + 1) * GROUP)
            inv_cnt = _inv_count(pos, w)
            pwb = pw_ref[g].astype(BF16)
            pooledb = _pool_fwd(proj[:, sl], xp_prev[:, sl], inv_cnt, w).astype(BF16)
            pre = _dot_nn(pooledb, pwb)
            dpo = dmix[:, sl]
            dvec_ref[ROW_POOL_SCALE:ROW_POOL_SCALE + 1, sl] += _rowsum(dpo * pre)
            dsb = (dpo * ps_ref[:, sl]).astype(BF16)
            dmat_ref[g] += _dot_tn(pooledb, dsb)
            dpooled = _dot_nt(dsb, pwb)
            dpooled_next = _dot_nt((dpo_next[:, sl] * ps_ref[:, sl]).astype(BF16), pwb)
            dxp = _pool_bwd(dpooled, dpooled_next, inv_cnt, _inv_count(pos_next, w), w)
            dproj_ref[:, sl] = dxp.astype(BF16)

        zu = proj[:, POOL_WIDTH:POOL_WIDTH + SGU_WIDTH]
        zv = proj[:, POOL_WIDTH + SGU_WIDTH:]
        u = _gelu(zu)
        vhat, rstd_v = _ln_stats(_gelu(zv))
        v_ln = vhat * lg_ref[...] + lb_ref[...]
        dsg = dmix[:, POOL_WIDTH:]
        row = lax.broadcasted_iota(jnp.int32, (GROUP, GROUP), 0)
        col = lax.broadcasted_iota(jnp.int32, (GROUP, GROUP), 1)
        for h in range(N_HEADS):
            ws = _masked_sgu_w(sw_ref, h).astype(BF16)
            vh = _to_head_major(v_ln, h, nc).astype(BF16)
            mixed = _dot_nn(ws, vh) + sb_ref[:, h:h + 1]
            dsg_h = _to_head_major(dsg, h, nc)
            du_h = dsg_h * mixed
            dm_h = dsg_h * _to_head_major(u, h, nc)
            pos_sums = lax.dot_general(jnp.ones((8, nc * GROUP), F32), dm_h, (((1,), (1,)), ((), ())),
                                       precision=lax.Precision.HIGHEST, preferred_element_type=F32)
            dvec_ref[ROW_SGU_B + h:ROW_SGU_B + h + 1, 0:GROUP] += pos_sums[0:1, :]
            dmb = dm_h.astype(BF16)
            dmat_ref[len(POOL_WINDOWS) + h] += jnp.where(row >= col, _dot_nt(dmb, vh), 0.0)
            dv_h = _dot_tn(ws, dmb)
            for c in range(nc):
                rs = slice(c * GROUP, (c + 1) * GROUP)
                cs = slice(h * GROUP, (h + 1) * GROUP)
                du_scr[rs, cs] = du_h[:, c * GROUP:(c + 1) * GROUP]
                dv_scr[rs, cs] = dv_h[:, c * GROUP:(c + 1) * GROUP]
        dv_ln = dv_scr[...]
        dvec_ref[ROW_SLN_B:ROW_SLN_B + 1, 0:SGU_WIDTH] += _rowsum(dv_ln)
        dvec_ref[ROW_SLN_G:ROW_SLN_G + 1, 0:SGU_WIDTH] += _rowsum(dv_ln * vhat)
        dv = _ln_bwd(dv_ln, vhat, rstd_v, lg_ref[...])
        dproj_ref[:, POOL_WIDTH:POOL_WIDTH + SGU_WIDTH] = (du_scr[...] * _gelu_grad(zu)).astype(BF16)
        dproj_ref[:, POOL_WIDTH + SGU_WIDTH:] = (dv * _gelu_grad(zv)).astype(BF16)
        gx_ref[...] = ALPHA * dr1_t + _dot_nn(dproj_ref[...], win_ref[...])

    def tile(cols):
        return pl.BlockSpec((tt, cols), lambda i: (i, 0))

    def whole(a):
        nd = a.ndim
        return pl.BlockSpec(a.shape, lambda i: (0,) * nd)

    def resident(shape):
        nd = len(shape)
        return pl.BlockSpec(shape, lambda i: (0,) * nd)

    next_halo = pl.BlockSpec((HALO, D_MODEL), lambda i: (jnp.minimum((i + 1) * (tt // HALO), n_halo_blocks - 1), 0))
    prev_halo = pl.BlockSpec((HALO, POOL_WIDTH), lambda i: (jnp.maximum(i * (tt // HALO) - 1, 0), 0))
    consts = [win_t, wout, pool_w, pool_scale, sln_g, sln_b, sgu_w, sgu_b_t, stats]
    small_shapes = [(len(POOL_WINDOWS) + N_HEADS, GROUP, GROUP), (VEC_ROWS, D_MODEL)]
    return pl.pallas_call(
        body, name="mixer_bwd", grid=(n_tiles,),
        in_specs=[tile(D_MODEL), next_halo, tile(IN_COLS), prev_halo] + [whole(a) for a in consts],
        out_specs=[tile(D_MODEL), tile(IN_COLS)] + [resident(s) for s in small_shapes],
        out_shape=[jax.ShapeDtypeStruct((tokens, D_MODEL), F32), jax.ShapeDtypeStruct((tokens, IN_COLS), BF16)]
                  + [jax.ShapeDtypeStruct(s, F32) for s in small_shapes],
        scratch_shapes=[pltpu.VMEM((tt, SGU_WIDTH), F32), pltpu.VMEM((tt, SGU_WIDTH), F32)],
        compiler_params=pltpu.CompilerParams(dimension_semantics=("arbitrary",), vmem_limit_bytes=VMEM_LIMIT_BYTES),
    )(dr1, dr1, proj, proj, *consts)


def _chip_sums(kept, from_sibling, place, name):
    na = len(kept)

    def body(place_ref, *refs):
        kept_refs, sib_refs = refs[:na], refs[na:2 * na]
        bf_refs, own_refs = refs[2 * na:3 * na], refs[3 * na:]
        q = pl.program_id(0)
        for a in range(na):
            s = kept_refs[a][...] + sib_refs[a][...]
            bf_refs[a][...] = s.astype(BF16)

            @pl.when(q == place_ref[1])
            def _(a=a, s=s):
                own_refs[a][...] = s

    by_chip = [pl.BlockSpec((None,) + k.shape[1:], lambda q, pr: (q, 0, 0)) for k in kept]
    grid_spec = pltpu.PrefetchScalarGridSpec(
        num_scalar_prefetch=1, grid=(N_CHIPS,), in_specs=by_chip + by_chip,
        out_specs=by_chip + [pl.BlockSpec(k.shape[1:], lambda q, pr: (0, 0)) for k in kept])
    return pl.pallas_call(
        body, name=name, grid_spec=grid_spec,
        out_shape=[jax.ShapeDtypeStruct(k.shape, BF16) for k in kept]
                  + [jax.ShapeDtypeStruct(k.shape[1:], F32) for k in kept],
        compiler_params=pltpu.CompilerParams(dimension_semantics=("arbitrary",), vmem_limit_bytes=VMEM_LIMIT_BYTES),
    )(place, *kept, *from_sibling)


def _send_to_owners(chip_partials, name):
    na = len(chip_partials)

    def body(*refs):
        scatter = _ScatterToOwners(refs[:na], refs[na:2 * na], *refs[2 * na:])
        scatter.start()
        scatter.finish()

    any_spec = pl.BlockSpec(memory_space=pl.ANY)
    return pl.pallas_call(
        body, name=name, out_shape=[_received_shape(p) for p in chip_partials],
        in_specs=[any_spec] * na, out_specs=[any_spec] * na, scratch_shapes=_ScatterToOwners.scratch(na),
    )(*chip_partials)


def _adamw_math(w, g, m, v):
    m = ADAM_B1 * m + (1.0 - ADAM_B1) * g
    v = ADAM_B2 * v + (1.0 - ADAM_B2) * (g * g)
    m_hat = m / (1.0 - ADAM_B1 ** ADAM_STEP)
    v_hat = v / (1.0 - ADAM_B2 ** ADAM_STEP)
    delta = -ADAM_LR * (m_hat / (jnp.sqrt(v_hat) + ADAM_EPS) + ADAM_WD * w)
    return delta, m, v


def _sum_adamw(w, m, v, own, received, name):
    rows, cols = w.shape
    rb = _row_block(rows, 256)

    def body(w_ref, m_ref, v_ref, own_ref, rec_ref, g_ref, d_ref, mo_ref, vo_ref):
        g = own_ref[...]
        for j in range(3):
            g = g + rec_ref[j].astype(F32)
        g_ref[...] = g
        d_ref[...], mo_ref[...], vo_ref[...] = _adamw_math(w_ref[...], g, m_ref[...], v_ref[...])

    spec = pl.BlockSpec((rb, cols), lambda r: (r, 0))
    return pl.pallas_call(
        body, name=name, grid=(rows // rb,),
        in_specs=[spec] * 4 + [pl.BlockSpec((3, rb, cols), lambda r: (0, r, 0))], out_specs=[spec] * 4,
        out_shape=[jax.ShapeDtypeStruct((rows, cols), F32)] * 4,
        compiler_params=pltpu.CompilerParams(dimension_semantics=("arbitrary",)),
    )(w, m, v, own, received)


SMALL_NAMES = ("pool_w", "sgu_w", "pool_scale", "sgu_ln_g", "sgu_ln_b", "sgu_b", "ln1_g", "ln1_b", "ln2_g", "ln2_b")
_SMALL_VEC_ROWS = {"pool_scale": (ROW_POOL_SCALE, POOL_WIDTH), "sgu_ln_g": (ROW_SLN_G, SGU_WIDTH),
                   "sgu_ln_b": (ROW_SLN_B, SGU_WIDTH), "ln1_g": (ROW_LN1_G, D_MODEL), "ln1_b": (ROW_LN1_B, D_MODEL),
                   "ln2_g": (ROW_LN2_G, D_MODEL), "ln2_b": (ROW_LN2_B, D_MODEL)}
_SMALL_MAT_FIRST = {"pool_w": 0, "sgu_w": len(POOL_WINDOWS)}


def _small_sum_adamw(mats_all, vecs_all, w, m, v):
    n = len(SMALL_NAMES)

    def body(mats_ref, vecs_ref, *refs):
        w_refs, m_refs, v_refs = refs[:n], refs[n:2 * n], refs[2 * n:3 * n]
        loss_ref = refs[3 * n]
        g_refs, d_refs, mo_refs, vo_refs = (refs[3 * n + 1 + k * n:3 * n + 1 + (k + 1) * n] for k in range(4))
        vec_scr = refs[7 * n + 1]

        def update(k, idx, g):
            d, mo, vo = _adamw_math(w_refs[k][idx], g, m_refs[k][idx], v_refs[k][idx])
            g_refs[k][idx], d_refs[k][idx], mo_refs[k][idx], vo_refs[k][idx] = g, d, mo, vo

        total = vecs_ref[0]
        for dev in range(1, N_DEV):
            total = total + vecs_ref[dev]
        vec_scr[...] = total
        for k, name in enumerate(SMALL_NAMES):
            if name in _SMALL_MAT_FIRST:
                for b in range(4):
                    g = mats_ref[0, _SMALL_MAT_FIRST[name] + b]
                    for dev in range(1, N_DEV):
                        g = g + mats_ref[dev, _SMALL_MAT_FIRST[name] + b]
                    update(k, (0, b), g)
            elif name == "sgu_b":
                update(k, (0,), vec_scr[ROW_SGU_B:ROW_SGU_B + N_HEADS, 0:GROUP])
            else:
                row, width = _SMALL_VEC_ROWS[name]
                update(k, (slice(None), slice(None)), vec_scr[row:row + 1, 0:width])
        loss = jnp.sum(vec_scr[ROW_LOSS:ROW_LOSS + 1, :], axis=1, keepdims=True) * (0.5 / D_MODEL)
        loss_ref[...] = jnp.broadcast_to(loss, loss_ref.shape)

    vmem = pl.BlockSpec(memory_space=pltpu.VMEM)
    shapes = [jax.ShapeDtypeStruct(w[k].shape, F32) for k in SMALL_NAMES]
    outs = pl.pallas_call(
        body, name="small_sum_adamw",
        out_shape=[jax.ShapeDtypeStruct((8, GROUP), F32)] + shapes * 4,
        in_specs=[vmem] * (2 + 3 * n), out_specs=[vmem] * (1 + 4 * n),
        scratch_shapes=[pltpu.VMEM((VEC_ROWS, D_MODEL), F32)],
    )(mats_all, vecs_all, *[w[k] for k in SMALL_NAMES], *[m[k] for k in SMALL_NAMES], *[v[k] for k in SMALL_NAMES])
    return outs[0], outs[1:1 + n], outs[1 + n:1 + 2 * n], outs[1 + 2 * n:1 + 3 * n], outs[1 + 3 * n:]


def kernel(x, w_in, pool_w, pool_scale, sgu_ln_g, sgu_ln_b, sgu_w, sgu_b, w_out, ln1_g, ln1_b, w_gate_up, w_down, ln2_g, ln2_b, loss_target, m_w_in, m_pool_w, m_pool_scale, m_sgu_ln_g, m_sgu_ln_b, m_sgu_w, m_sgu_b, m_w_out, m_ln1_g, m_ln1_b, m_w_gate_up, m_w_down, m_ln2_g, m_ln2_b, v_w_in, v_pool_w, v_pool_scale, v_sgu_ln_g, v_sgu_ln_b, v_sgu_w, v_sgu_b, v_w_out, v_ln1_g, v_ln1_b, v_w_gate_up, v_w_down, v_ln2_g, v_ln2_b):
    bl, seq, _ = x.shape
    tokens = bl * seq
    x2d = x.reshape(tokens, D_MODEL)
    tgt2d = loss_target.reshape(tokens, D_MODEL)
    my_c = lax.axis_index("c")
    place = jnp.stack([my_c, 2 * lax.axis_index("x") + lax.axis_index("y")]).astype(jnp.int32)

    win_t, wout = _all_gather_rows([w_in[0].T.astype(BF16), w_out[0].astype(BF16)], "weight_all_gather")

    pool_w3, sgu_w3 = pool_w[0], sgu_w[0]
    sgu_b_t = sgu_b[0].T
    proj, xhat1, rstd1, mix_bf, x_bf, wgu_t, wdown = _mixer_fwd(
        x2d, win_t, wout, pool_w3, pool_scale, sgu_ln_g, sgu_ln_b, sgu_w3, sgu_b_t, ln1_g, ln1_b,
        [w_gate_up[0].T.astype(BF16), w_down[0].astype(BF16)], seq)
    dr1, dr1_bf, h_bf, dr2_bf, a_bf, dgu_bf, stats = _ffn_fwd_bwd(
        xhat1, rstd1, tgt2d, wgu_t, wdown, ln1_g, ln1_b, ln2_g, ln2_b)

    kept_gu, sib_gu = _wgrad_exchange(dgu_bf, h_bf, 1, "wgrad_gate_up")
    bf_gu, own_gu = _chip_sums([kept_gu], [sib_gu], place, "chip_sum_w_gate_up")
    kept_dn, sib_dn, rec_gu = _wgrad_exchange(a_bf, dr2_bf, 2, "wgrad_down", scatter_partials=[bf_gu])
    bf_dn, own_dn = _chip_sums([kept_dn], [sib_dn], place, "chip_sum_w_down")
    kept_out, sib_out, rec_dn = _wgrad_exchange(mix_bf, dr1_bf, N_CHIPS, "wgrad_out", scatter_partials=[bf_dn])
    bf_out, own_out = _chip_sums([kept_out], [sib_out], place, "chip_sum_w_out")
    gx, dproj_bf, d_mats, d_vecs = _mixer_bwd(
        dr1, proj, win_t, wout, pool_w3, pool_scale, sgu_ln_g, sgu_ln_b, sgu_w3, sgu_b_t, stats, seq)
    kept_in, sib_in, mats_all, vecs_all, rec_out = _wgrad_exchange(
        dproj_bf, x_bf, N_CHIPS, "wgrad_in", gather_rows=[d_mats.reshape(-1, GROUP), d_vecs], scatter_partials=[bf_out])
    bf_in, own_in = _chip_sums([kept_in], [sib_in], place, "chip_sum_w_in")
    rec_in, = _send_to_owners([bf_in], "grad_scatter_w_in")

    grads, deltas, new_m, new_v = {}, {}, {}, {}
    for nm, w, m, v, own, rec, transposed in (("w_out", w_out, m_w_out, v_w_out, own_out, rec_out, False),
                                              ("w_down", w_down, m_w_down, v_w_down, own_dn, rec_dn, False),
                                              ("w_in", w_in, m_w_in, v_w_in, own_in, rec_in, True),
                                              ("w_gate_up", w_gate_up, m_w_gate_up, v_w_gate_up, own_gu, rec_gu, True)):
        rows = (lambda a: a[0].T) if transposed else (lambda a: a[0])
        back = (lambda a: a.T[None]) if transposed else (lambda a: a[None])
        g, d, mo, vo = _sum_adamw(rows(w), rows(m), rows(v), own, rec, "adamw_" + nm)
        grads[nm], deltas[nm], new_m[nm], new_v[nm] = back(g), back(d), back(mo), back(vo)

    small_w = {"pool_w": pool_w, "pool_scale": pool_scale, "sgu_ln_g": sgu_ln_g, "sgu_ln_b": sgu_ln_b, "sgu_w": sgu_w,
               "sgu_b": sgu_b, "ln1_g": ln1_g, "ln1_b": ln1_b, "ln2_g": ln2_g, "ln2_b": ln2_b}
    small_m = {"pool_w": m_pool_w, "pool_scale": m_pool_scale, "sgu_ln_g": m_sgu_ln_g, "sgu_ln_b": m_sgu_ln_b,
               "sgu_w": m_sgu_w, "sgu_b": m_sgu_b, "ln1_g": m_ln1_g, "ln1_b": m_ln1_b, "ln2_g": m_ln2_g, "ln2_b": m_ln2_b}
    small_v = {"pool_w": v_pool_w, "pool_scale": v_pool_scale, "sgu_ln_g": v_sgu_ln_g, "sgu_ln_b": v_sgu_ln_b,
               "sgu_w": v_sgu_w, "sgu_b": v_sgu_b, "ln1_g": v_ln1_g, "ln1_b": v_ln1_b, "ln2_g": v_ln2_g, "ln2_b": v_ln2_b}
    loss_blk, g_small, d_small, m_small, v_small = _small_sum_adamw(
        mats_all.reshape(N_DEV, 2 * N_HEADS, GROUP, GROUP), vecs_all.reshape(N_DEV, VEC_ROWS, D_MODEL),
        small_w, small_m, small_v)
    for vals, dst in ((g_small, grads), (d_small, deltas), (m_small, new_m), (v_small, new_v)):
        dst.update(zip(SMALL_NAMES, vals))

    order = ["w_in", "pool_w", "pool_scale", "sgu_ln_g", "sgu_ln_b", "sgu_w", "sgu_b", "w_out", "ln1_g", "ln1_b",
             "w_gate_up", "w_down", "ln2_g", "ln2_b"]
    return (loss_blk[0, 0], gx.reshape(bl, seq, D_MODEL), *[grads[k] for k in order], *[deltas[k] for k in order],
            *[new_m[k] for k in order], *[new_v[k] for k in order])
```

```python
import functools

import jax
import jax.numpy as jnp
from jax import lax
from jax.experimental import pallas as pl
from jax.experimental.pallas import tpu as pltpu

F32 = jnp.float32
BF16 = jnp.bfloat16
MESH = pl.DeviceIdType.MESH

D_MODEL = 1024
POOL_WIDTH = 512
SGU_WIDTH = 512
POOL_WINDOWS = (2, 4, 8, 16)
GROUP = 128
N_HEADS = 4
IN_COLS = POOL_WIDTH + 2 * SGU_WIDTH
D_FF = 2816
FF_CHUNK = 2816
LN_EPS = 1e-5
ALPHA = float(2.0 ** 0.25)
HALO = 16
N_DEV = 8
N_CHIPS = 4

ADAM_LR = 0.001
ADAM_B1 = 0.9
ADAM_B2 = 0.999
ADAM_EPS = 1e-08
ADAM_WD = 0.01
ADAM_STEP = 10

VMEM_LIMIT_BYTES = 56 * 1024 * 1024

_SQRT_HALF = 0.7071067811865476
_INV_SQRT_2PI = 0.3989422804014327


def _dot_nn(a, b):
    return lax.dot_general(a, b, (((1,), (0,)), ((), ())), preferred_element_type=F32)


def _dot_nt(a, b):
    return lax.dot_general(a, b, (((1,), (1,)), ((), ())), preferred_element_type=F32)


def _dot_tn(a, b):
    return lax.dot_general(a, b, (((0,), (0,)), ((), ())), preferred_element_type=F32)


def _gelu(x):
    return 0.5 * x * (1.0 + lax.erf(x * _SQRT_HALF))


def _gelu_grad(x):
    return 0.5 * (1.0 + lax.erf(x * _SQRT_HALF)) + x * jnp.exp(-0.5 * x * x) * _INV_SQRT_2PI


def _ln_stats(r):
    mu = jnp.mean(r, axis=-1, keepdims=True)
    d = r - mu
    var = jnp.mean(d * d, axis=-1, keepdims=True)
    rstd = lax.rsqrt(var + LN_EPS)
    return d * rstd, rstd


def _ln_bwd(dout, xhat, rstd, g):
    dxh = dout * g
    m1 = jnp.mean(dxh, axis=-1, keepdims=True)
    m2 = jnp.mean(dxh * xhat, axis=-1, keepdims=True)
    return rstd * (dxh - m1 - xhat * m2)


def _rowsum(a):
    return jnp.sum(a, axis=0, keepdims=True)


def _pool_fwd(xp, xp_prev, inv_cnt, w):
    s = jnp.concatenate([xp_prev, xp], axis=0)
    k = 1
    while k < w:
        s = s + pltpu.roll(s, k, 0)
        k *= 2
    return s[HALO:, :] * inv_cnt - xp


def _pool_bwd(dpooled, dpooled_next, inv_cnt, inv_cnt_next, w):
    n = dpooled.shape[0] + HALO
    s = jnp.concatenate([dpooled * inv_cnt, dpooled_next * inv_cnt_next], axis=0)
    k = 1
    while k < w:
        s = s + pltpu.roll(s, n - k, 0)
        k *= 2
    return s[: dpooled.shape[0], :] - dpooled


def _inv_count(pos, w):
    return 1.0 / jnp.minimum(pos + 1, w).astype(F32)


def _to_head_major(a, h, nc):
    return jnp.concatenate(
        [a[c * GROUP:(c + 1) * GROUP, h * GROUP:(h + 1) * GROUP] for c in range(nc)], axis=1)


def _masked_sgu_w(sw_ref, h):
    row = lax.broadcasted_iota(jnp.int32, (GROUP, GROUP), 0)
    col = lax.broadcasted_iota(jnp.int32, (GROUP, GROUP), 1)
    return jnp.where(row >= col, sw_ref[h], 0.0)


def _row_block(rows, limit):
    return max(b for b in range(16, min(rows, limit) + 1, 16) if rows % b == 0)


def _mesh_position():
    return lax.axis_index("x"), lax.axis_index("y"), lax.axis_index("c")


def _other_chips(x, y):
    return [(1 - x, y), (x, 1 - y), (1 - x, 1 - y)]


class _TwoLevelGather:
    def __init__(self, ins, outs, send_sems, recv_sems, local_sems):
        self.ins, self.outs = ins, outs
        self.send_sems, self.recv_sems, self.local_sems = send_sems, recv_sems, local_sems
        self.na = len(ins)
        x, y, c = _mesh_position()
        self.c = c
        self.me, self.sibling = (x, y, c), (x, y, 1 - c)
        self.chips = _other_chips(x, y)
        self.relay_from = (x + (1 - c) * (1 - 2 * x), y + c * (1 - 2 * y))
        self.relay_to = (x + c * (1 - 2 * x), y + (1 - c) * (1 - 2 * y))

    def _rows(self, a, px, py, pc):
        n = self.ins[a].shape[0]
        return self.outs[a].at[pl.ds((4 * px + 2 * py + pc) * n, n), :]

    def _copy(self, a, k, block, to, src=None):
        return pltpu.make_async_remote_copy(
            src_ref=self._rows(a, *block) if src is None else src, dst_ref=self._rows(a, *block),
            send_sem=self.send_sems.at[a * 7 + k], recv_sem=self.recv_sems.at[a * 7 + k],
            device_id=to, device_id_type=MESH)

    def _mine(self, a):
        return pltpu.make_async_copy(self.ins[a], self._rows(a, *self.me), self.local_sems.at[a])

    def start(self):
        for a in range(self.na):
            self._mine(a).start()
        for a in range(self.na):
            self._copy(a, 0, self.me, self.sibling, src=self.ins[a]).start()
            for j, chip in enumerate(self.chips[:2]):
                self._copy(a, 1 + j, self.me, (*chip, self.c), src=self.ins[a]).start()

    def relay(self, a):
        c, block = self.c, (*self.relay_from, self.c)
        self._copy(a, 1 + c, block, self.me).wait_recv()
        self._copy(a, 3, block, (*self.relay_to, c)).start()
        self._copy(a, 4 + c, block, self.sibling).start()

    def pass_on(self, a):
        c = self.c
        self._copy(a, 2 - c, (*self.relay_to, c), self.me).wait_recv()
        self._copy(a, 5 - c, (*self.relay_to, c), self.sibling).start()
        self._copy(a, 3, (*self.chips[2], c), self.me).wait_recv()
        self._copy(a, 6, (*self.chips[2], c), self.sibling).start()

    def finish(self):
        for a in range(self.na):
            self._copy(a, 0, self.sibling, self.me).wait_recv()
            for j, chip in enumerate(self.chips):
                self._copy(a, 4 + j, (*chip, 1 - self.c), self.me).wait_recv()
        for a in range(self.na):
            for k in range(7):
                self._copy(a, k, self.me, self.sibling, src=self.ins[a]).wait_send()
            self._mine(a).wait()

    @staticmethod
    def scratch(na):
        return [pltpu.SemaphoreType.DMA((7 * na,)), pltpu.SemaphoreType.DMA((7 * na,)), pltpu.SemaphoreType.DMA((na,))]


def _gathered_shape(s):
    return jax.ShapeDtypeStruct((N_DEV * s.shape[0], s.shape[1]), s.dtype)


def _all_gather_rows(shards, name):
    na = len(shards)

    def body(*refs):
        gather = _TwoLevelGather(refs[:na], refs[na:2 * na], *refs[2 * na:])
        gather.start()
        for a in range(na):
            gather.relay(a)
        for a in range(na):
            gather.pass_on(a)
        gather.finish()

    any_spec = pl.BlockSpec(memory_space=pl.ANY)
    return pl.pallas_call(
        body, name=name, out_shape=[_gathered_shape(s) for s in shards],
        in_specs=[any_spec] * na, out_specs=[any_spec] * na, scratch_shapes=_TwoLevelGather.scratch(na),
    )(*shards)


def _mixer_fwd(x2d, win_t, wout, pool_w, pool_scale, sln_g, sln_b, sgu_w, sgu_b_t, ln1_g, ln1_b, later_shards, seq):
    tokens = x2d.shape[0]
    tt = min(512, seq)
    tiles_per_seq = seq // tt
    nc = tt // GROUP
    n_tiles = tokens // tt
    n_later = len(later_shards)

    def body(x_ref, xh_ref, win_ref, wout_ref, pw_ref, ps_ref, lg_ref, lb_ref, sw_ref, sb_ref, g1_ref, b1_ref, *rest):
        shard_refs, rest = rest[:n_later], rest[n_later:]
        proj_ref, xhat_ref, rstd_ref, mix_ref, xbf_ref = rest[:5]
        gathered_refs, rest = rest[5:5 + n_later], rest[5 + n_later:]
        mix_scr, send_sems, recv_sems, local_sems = rest
        i = pl.program_id(0)
        gather = _TwoLevelGather(shard_refs, gathered_refs, send_sems, recv_sems, local_sems)

        @pl.when(i == 0)
        def _():
            gather.start()

        tile_in_seq = i % tiles_per_seq
        x = x_ref[...]
        xb = x.astype(BF16)
        xbf_ref[...] = xb
        proj = _dot_nt(xb, win_ref[...])
        proj_ref[...] = proj
        xp_prev = _dot_nt(xh_ref[...].astype(BF16), win_ref[0:POOL_WIDTH, :])
        xp_prev = jnp.where(tile_in_seq == 0, 0.0, xp_prev)
        pos = tile_in_seq * tt + lax.broadcasted_iota(jnp.int32, (tt, 1), 0)
        for g, w in enumerate(POOL_WINDOWS):
            sl = slice(g * GROUP, (g + 1) * GROUP)
            pooled = _pool_fwd(proj[:, sl], xp_prev[:, sl], _inv_count(pos, w), w)
            pre = _dot_nn(pooled.astype(BF16), pw_ref[g].astype(BF16))
            mix_scr[:, sl] = pre * ps_ref[:, sl]
        u = _gelu(proj[:, POOL_WIDTH:POOL_WIDTH + SGU_WIDTH])
        v = _gelu(proj[:, POOL_WIDTH + SGU_WIDTH:])
        vhat, _ = _ln_stats(v)
        v_ln = vhat * lg_ref[...] + lb_ref[...]
        for h in range(N_HEADS):
            ws = _masked_sgu_w(sw_ref, h).astype(BF16)
            mixed = _dot_nn(ws, _to_head_major(v_ln, h, nc).astype(BF16)) + sb_ref[:, h:h + 1]
            for c in range(nc):
                rs = slice(c * GROUP, (c + 1) * GROUP)
                mix_scr[rs, POOL_WIDTH + h * GROUP:POOL_WIDTH + (h + 1) * GROUP] = (
                    u[rs, h * GROUP:(h + 1) * GROUP] * mixed[:, c * GROUP:(c + 1) * GROUP])
        mixb = mix_scr[...].astype(BF16)
        mix_ref[...] = mixb
        r1 = ALPHA * x + _dot_nn(mixb, wout_ref[...])
        xhat, rstd = _ln_stats(r1)
        xhat_ref[...] = xhat
        rstd_ref[...] = rstd

        for a in range(n_later):
            relay_tile = min(n_tiles // 2 + a, n_tiles - 1)

            @pl.when(i == relay_tile)
            def _(a=a):
                gather.relay(a)

            @pl.when(i == max(n_tiles - n_later + a, relay_tile))
            def _(a=a):
                gather.pass_on(a)

        @pl.when(i == n_tiles - 1)
        def _():
            gather.finish()

    def tile(cols):
        return pl.BlockSpec((tt, cols), lambda i: (i, 0))

    def whole(a):
        nd = a.ndim
        return pl.BlockSpec(a.shape, lambda i: (0,) * nd)

    any_spec = pl.BlockSpec(memory_space=pl.ANY)
    halo = pl.BlockSpec((HALO, D_MODEL), lambda i: (jnp.maximum(i * (tt // HALO) - 1, 0), 0))
    consts = [win_t, wout, pool_w, pool_scale, sln_g, sln_b, sgu_w, sgu_b_t, ln1_g, ln1_b]
    return pl.pallas_call(
        body, name="mixer_fwd", grid=(n_tiles,),
        in_specs=[tile(D_MODEL), halo] + [whole(a) for a in consts] + [any_spec] * n_later,
        out_specs=[tile(IN_COLS), tile(D_MODEL), tile(1), tile(D_MODEL), tile(D_MODEL)] + [any_spec] * n_later,
        out_shape=[jax.ShapeDtypeStruct((tokens, IN_COLS), F32), jax.ShapeDtypeStruct((tokens, D_MODEL), F32),
                   jax.ShapeDtypeStruct((tokens, 1), F32), jax.ShapeDtypeStruct((tokens, D_MODEL), BF16),
                   jax.ShapeDtypeStruct((tokens, D_MODEL), BF16)] + [_gathered_shape(s) for s in later_shards],
        scratch_shapes=[pltpu.VMEM((tt, D_MODEL), F32)] + _TwoLevelGather.scratch(n_later),
        compiler_params=pltpu.CompilerParams(dimension_semantics=("arbitrary",), vmem_limit_bytes=VMEM_LIMIT_BYTES),
    )(x2d, x2d, *consts, *later_shards)


def _ffn_fwd_bwd(xhat1, rstd1, target, wgu_t, wdown, ln1_g, ln1_b, ln2_g, ln2_b):
    tokens = xhat1.shape[0]
    tt = min(256, tokens)
    n_chunks = D_FF // FF_CHUNK

    def body(xhat_ref, rstd_ref, tgt_ref, wgu_hbm, wd_hbm, g1_ref, b1_ref, g2_ref, b2_ref,
             dr1_ref, dr1bf_ref, hbf_ref, dr2bf_ref, a_ref, dgu_ref, stats_ref, wgu_ref, wd_ref, gu_scr, sems):
        i = pl.program_id(0)

        @pl.when(i == 0)
        def _():
            loads = [pltpu.make_async_copy(wgu_hbm, wgu_ref, sems.at[0]),
                     pltpu.make_async_copy(wd_hbm, wd_ref, sems.at[1])]
            for cp in loads:
                cp.start()
            stats_ref[...] = jnp.zeros_like(stats_ref)
            for cp in loads:
                cp.wait()

        xhat1_t = xhat_ref[...]
        h = xhat1_t * g1_ref[...] + b1_ref[...]
        hb = h.astype(BF16)
        hbf_ref[...] = hb
        f = jnp.zeros((tt, D_MODEL), F32)
        for k in range(n_chunks):
            rg = slice(k * FF_CHUNK, (k + 1) * FF_CHUNK)
            ru = slice(D_FF + k * FF_CHUNK, D_FF + (k + 1) * FF_CHUNK)
            gate = _dot_nt(hb, wgu_ref[rg, :])
            up = _dot_nt(hb, wgu_ref[ru, :])
            gu_scr[:, rg] = gate
            gu_scr[:, ru] = up
            ab = (gate * jax.nn.sigmoid(gate) * up).astype(BF16)
            a_ref[:, rg] = ab
            f = f + _dot_nn(ab, wd_ref[rg, :])
        xhat2, rstd2 = _ln_stats(ALPHA * h + f)
        err = xhat2 * g2_ref[...] + b2_ref[...] - tgt_ref[...]
        dy = err * (1.0 / D_MODEL)
        stats_ref[0:1, :] += _rowsum(dy * xhat2)
        stats_ref[1:2, :] += _rowsum(dy)
        stats_ref[4:5, :] += _rowsum(err * err)
        dr2 = _ln_bwd(dy, xhat2, rstd2, g2_ref[...])
        dr2b = dr2.astype(BF16)
        dr2bf_ref[...] = dr2b
        dh = ALPHA * dr2
        for k in range(n_chunks):
            rg = slice(k * FF_CHUNK, (k + 1) * FF_CHUNK)
            ru = slice(D_FF + k * FF_CHUNK, D_FF + (k + 1) * FF_CHUNK)
            da = _dot_nt(dr2b, wd_ref[rg, :])
            gate = gu_scr[:, rg]
            up = gu_scr[:, ru]
            sg = jax.nn.sigmoid(gate)
            dgate = (da * up * (sg * (1.0 + gate * (1.0 - sg)))).astype(BF16)
            dup = (da * (gate * sg)).astype(BF16)
            dgu_ref[:, rg] = dgate
            dgu_ref[:, ru] = dup
            dh = dh + _dot_nn(dgate, wgu_ref[rg, :]) + _dot_nn(dup, wgu_ref[ru, :])
        stats_ref[2:3, :] += _rowsum(dh * xhat1_t)
        stats_ref[3:4, :] += _rowsum(dh)
        dr1 = _ln_bwd(dh, xhat1_t, rstd_ref[...], g1_ref[...])
        dr1_ref[...] = dr1
        dr1bf_ref[...] = dr1.astype(BF16)

    def tile(cols):
        return pl.BlockSpec((tt, cols), lambda i: (i, 0))

    def whole(a):
        nd = a.ndim
        return pl.BlockSpec(a.shape, lambda i: (0,) * nd)

    any_spec = pl.BlockSpec(memory_space=pl.ANY)
    vecs = [ln1_g, ln1_b, ln2_g, ln2_b]
    return pl.pallas_call(
        body, name="ffn_fwd_bwd", grid=(tokens // tt,),
        in_specs=[tile(D_MODEL), tile(1), tile(D_MODEL), any_spec, any_spec] + [whole(a) for a in vecs],
        out_specs=[tile(D_MODEL), tile(D_MODEL), tile(D_MODEL), tile(D_MODEL), tile(D_FF), tile(2 * D_FF),
                   pl.BlockSpec((8, D_MODEL), lambda i: (0, 0))],
        out_shape=[jax.ShapeDtypeStruct((tokens, D_MODEL), F32), jax.ShapeDtypeStruct((tokens, D_MODEL), BF16),
                   jax.ShapeDtypeStruct((tokens, D_MODEL), BF16),
                   jax.ShapeDtypeStruct((tokens, D_MODEL), BF16), jax.ShapeDtypeStruct((tokens, D_FF), BF16),
                   jax.ShapeDtypeStruct((tokens, 2 * D_FF), BF16), jax.ShapeDtypeStruct((8, D_MODEL), F32)],
        scratch_shapes=[pltpu.VMEM(wgu_t.shape, BF16), pltpu.VMEM(wdown.shape, BF16),
                        pltpu.VMEM((tt, 2 * D_FF), F32), pltpu.SemaphoreType.DMA((2,))],
        compiler_params=pltpu.CompilerParams(dimension_semantics=("arbitrary",), vmem_limit_bytes=VMEM_LIMIT_BYTES),
    )(xhat1, rstd1, target, wgu_t, wdown, *vecs)


def _wgrad_exchange(lhs, rhs, chips_per_block, name, gather_rows=(), scatter_partials=()):
    tokens, n_all = lhs.shape
    m = rhs.shape[1]
    n = n_all // N_DEV
    tw = min(2048, tokens)
    nt = tokens // tw
    cpb = chips_per_block
    nj = N_CHIPS // cpb
    ng, ns = len(gather_rows), len(scatter_partials)

    def body(l_ref, r_ref, *rest):
        small_refs, part_refs, rest = rest[:ng], rest[ng:ng + ns], rest[ng + ns:]
        kept_ref, sib_ref = rest[:2]
        gathered_refs, recv_refs, rest = rest[2:2 + ng], rest[2 + ng:2 + ng + ns], rest[2 + ng + ns:]
        acc, sendbuf, send_sems, recv_sems = rest[:4]
        j, t = pl.program_id(0), pl.program_id(1)
        first, last = (j == 0) & (t == 0), (j == nj - 1) & (t == nt - 1)
        x, y, c = _mesh_position()
        if ng:
            gather = _TwoLevelGather(small_refs, gathered_refs, *rest[4:7])

            @pl.when(first)
            def _():
                gather.start()
        if ns:
            scatter = _ScatterToOwners(part_refs, recv_refs, *rest[4 + 3 * bool(ng):])

            @pl.when(first)
            def _():
                scatter.start()

        def copy(q):
            return pltpu.make_async_remote_copy(
                src_ref=sendbuf.at[q], dst_ref=sib_ref.at[q], send_sem=send_sems.at[q], recv_sem=recv_sems.at[q],
                device_id=(x, y, 1 - c), device_id_type=MESH)

        @pl.when(t == 0)
        def _():
            acc[...] = jnp.zeros_like(acc)

        acc[...] += _dot_tn(l_ref[...], r_ref[...])

        @pl.when(t == nt - 1)
        def _():
            for qq in range(cpb):
                q = j * cpb + qq
                kept_ref[qq] = acc[pl.ds(pl.multiple_of(qq * 2 * n + c * n, 8), n), :]
                sendbuf[q] = acc[pl.ds(pl.multiple_of(qq * 2 * n + (1 - c) * n, 8), n), :]
                copy(q).start()

        if ng:
            @pl.when((j == nj - 1) & (t == nt // 2))
            def _():
                for a in range(ng):
                    gather.relay(a)

            @pl.when(last)
            def _():
                for a in range(ng):
                    gather.pass_on(a)
                gather.finish()

        @pl.when(last)
        def _():
            if ns:
                scatter.finish()
            for q in range(N_CHIPS):
                copy(q).wait_send()
                copy(q).wait_recv()

    shard4 = jax.ShapeDtypeStruct((N_CHIPS, n, m), F32)
    any_spec = pl.BlockSpec(memory_space=pl.ANY)
    return pl.pallas_call(
        body, name=name, grid=(nj, nt),
        in_specs=[pl.BlockSpec((tw, 2 * n * cpb), lambda j, t: (t, j)), pl.BlockSpec((tw, m), lambda j, t: (t, 0))]
                 + [any_spec] * (ng + ns),
        out_specs=[pl.BlockSpec((cpb, n, m), lambda j, t: (j, 0, 0)), any_spec] + [any_spec] * (ng + ns),
        out_shape=[shard4, shard4] + [_gathered_shape(s) for s in gather_rows]
                  + [_received_shape(p) for p in scatter_partials],
        scratch_shapes=[pltpu.VMEM((2 * n * cpb, m), F32), pltpu.VMEM((N_CHIPS, n, m), F32),
                        pltpu.SemaphoreType.DMA((N_CHIPS,)), pltpu.SemaphoreType.DMA((N_CHIPS,))]
                       + (_TwoLevelGather.scratch(ng) if ng else []) + (_ScatterToOwners.scratch(ns) if ns else []),
        compiler_params=pltpu.CompilerParams(dimension_semantics=("arbitrary", "arbitrary"),
                                             vmem_limit_bytes=VMEM_LIMIT_BYTES),
    )(lhs, rhs, *gather_rows, *scatter_partials)


class _ScatterToOwners:
    def __init__(self, ins, outs, send_sems, recv_sems):
        self.ins, self.outs, self.send_sems, self.recv_sems = ins, outs, send_sems, recv_sems
        x, y, c = _mesh_position()
        self.c, self.chips = c, _other_chips(x, y)

    def _copies(self):
        return [pltpu.make_async_remote_copy(
            src_ref=self.ins[a].at[2 * cx + cy], dst_ref=self.outs[a].at[j],
            send_sem=self.send_sems.at[a * 3 + j], recv_sem=self.recv_sems.at[a * 3 + j],
            device_id=(cx, cy, self.c), device_id_type=MESH)
            for a in range(len(self.ins)) for j, (cx, cy) in enumerate(self.chips)]

    def start(self):
        for cp in self._copies():
            cp.start()

    def finish(self):
        for cp in self._copies():
            cp.wait_send()
            cp.wait_recv()

    @staticmethod
    def scratch(na):
        return [pltpu.SemaphoreType.DMA((3 * na,)), pltpu.SemaphoreType.DMA((3 * na,))]


def _received_shape(p):
    return jax.ShapeDtypeStruct((3,) + p.shape[1:], p.dtype)


ROW_POOL_SCALE, ROW_SLN_G, ROW_SLN_B, ROW_SGU_B = 0, 1, 2, 3
ROW_LN2_G, ROW_LN2_B, ROW_LN1_G, ROW_LN1_B, ROW_LOSS = 8, 9, 10, 11, 12
VEC_ROWS = 16


def _mixer_bwd(dr1, proj, win_t, wout, pool_w, pool_scale, sln_g, sln_b, sgu_w, sgu_b_t, stats, seq):
    tokens = dr1.shape[0]
    tt = min(512, seq)
    tiles_per_seq = seq // tt
    nc = tt // GROUP
    n_halo_blocks = tokens // HALO
    n_tiles = tokens // tt

    def body(dr1_ref, dr1n_ref, proj_ref, projh_ref, win_ref, wout_ref, pw_ref, ps_ref, lg_ref, lb_ref, sw_ref, sb_ref,
             stats_ref, gx_ref, dproj_ref, dmat_ref, dvec_ref, du_scr, dv_scr):
        i = pl.program_id(0)
        tile_in_seq = i % tiles_per_seq

        @pl.when(i == 0)
        def _():
            dmat_ref[...] = jnp.zeros_like(dmat_ref)
            dvec_ref[0:8, :] = jnp.zeros((8, D_MODEL), F32)
            dvec_ref[8:16, :] = stats_ref[...]

        dr1_t = dr1_ref[...]
        dr1b = dr1_t.astype(BF16)
        dmix = _dot_nt(dr1b, wout_ref[...])
        dpo_next = _dot_nt(dr1n_ref[...].astype(BF16), wout_ref[0:POOL_WIDTH, :])
        dpo_next = jnp.where(tile_in_seq == tiles_per_seq - 1, 0.0, dpo_next)
        proj = proj_ref[...]
        xp_prev = jnp.where(tile_in_seq == 0, 0.0, projh_ref[...])
        pos = tile_in_seq * tt + lax.broadcasted_iota(jnp.int32, (tt, 1), 0)
        pos_next = (tile_in_seq + 1) * tt + lax.broadcasted_iota(jnp.int32, (HALO, 1), 0)

        for g, w in enumerate(POOL_WINDOWS):
            sl = slice(g * GROUP, (g + 1) * GROUP)
            inv_cnt = _inv_count(pos, w)
            pwb = pw_ref[g].astype(BF16)
            pooledb = _pool_fwd(proj[:, sl], xp_prev[:, sl], inv_cnt, w).astype(BF16)
            pre = _dot_nn(pooledb, pwb)
            dpo = dmix[:, sl]
            dvec_ref[ROW_POOL_SCALE:ROW_POOL_SCALE + 1, sl] += _rowsum(dpo * pre)
            dsb = (dpo * ps_ref[:, sl]).astype(BF16)
            dmat_ref[g] += _dot_tn(pooledb, dsb)
            dpooled = _dot_nt(dsb, pwb)
            dpooled_next = _dot_nt((dpo_next[:, sl] * ps_ref[:, sl]).astype(BF16), pwb)
            dxp = _pool_bwd(dpooled, dpooled_next, inv_cnt, _inv_count(pos_next, w), w)
            dproj_ref[:, sl] = dxp.astype(BF16)

        zu = proj[:, POOL_WIDTH:POOL_WIDTH + SGU_WIDTH]
        zv = proj[:, POOL_WIDTH + SGU_WIDTH:]
        u = _gelu(zu)
        vhat, rstd_v = _ln_stats(_gelu(zv))
        v_ln = vhat * lg_ref[...] + lb_ref[...]
        dsg = dmix[:, POOL_WIDTH:]
        row = lax.broadcasted_iota(jnp.int32, (GROUP, GROUP), 0)
        col = lax.broadcasted_iota(jnp.int32, (GROUP, GROUP), 1)
        for h in range(N_HEADS):
            ws = _masked_sgu_w(sw_ref, h).astype(BF16)
            vh = _to_head_major(v_ln, h, nc).astype(BF16)
            mixed = _dot_nn(ws, vh) + sb_ref[:, h:h + 1]
            dsg_h = _to_head_major(dsg, h, nc)
            du_h = dsg_h * mixed
            dm_h = dsg_h * _to_head_major(u, h, nc)
            pos_sums = lax.dot_general(jnp.ones((8, nc * GROUP), F32), dm_h, (((1,), (1,)), ((), ())),
                                       precision=lax.Precision.HIGHEST, preferred_element_type=F32)
            dvec_ref[ROW_SGU_B + h:ROW_SGU_B + h + 1, 0:GROUP] += pos_sums[0:1, :]
            dmb = dm_h.astype(BF16)
            dmat_ref[len(POOL_WINDOWS) + h] += jnp.where(row >= col, _dot_nt(dmb, vh), 0.0)
            dv_h = _dot_tn(ws, dmb)
            for c in range(nc):
                rs = slice(c * GROUP, (c + 1) * GROUP)
                cs = slice(h * GROUP, (h + 1) * GROUP)
                du_scr[rs, cs] = du_h[:, c * GROUP:(c + 1) * GROUP]
                dv_scr[rs, cs] = dv_h[:, c * GROUP:(c + 1) * GROUP]
        dv_ln = dv_scr[...]
        dvec_ref[ROW_SLN_B:ROW_SLN_B + 1, 0:SGU_WIDTH] += _rowsum(dv_ln)
        dvec_ref[ROW_SLN_G:ROW_SLN_G + 1, 0:SGU_WIDTH] += _rowsum(dv_ln * vhat)
        dv = _ln_bwd(dv_ln, vhat, rstd_v, lg_ref[...])
        dproj_ref[:, POOL_WIDTH:POOL_WIDTH + SGU_WIDTH] = (du_scr[...] * _gelu_grad(zu)).astype(BF16)
        dproj_ref[:, POOL_WIDTH + SGU_WIDTH:] = (dv * _gelu_grad(zv)).astype(BF16)
        gx_ref[...] = ALPHA * dr1_t + _dot_nn(dproj_ref[...], win_ref[...])

    def tile(cols):
        return pl.BlockSpec((tt, cols), lambda i: (i, 0))

    def whole(a):
        nd = a.ndim
        return pl.BlockSpec(a.shape, lambda i: (0,) * nd)

    def resident(shape):
        nd = len(shape)
        return pl.BlockSpec(shape, lambda i: (0,) * nd)

    next_halo = pl.BlockSpec((HALO, D_MODEL), lambda i: (jnp.minimum((i + 1) * (tt // HALO), n_halo_blocks - 1), 0))
    prev_halo = pl.BlockSpec((HALO, POOL_WIDTH), lambda i: (jnp.maximum(i * (tt // HALO) - 1, 0), 0))
    consts = [win_t, wout, pool_w, pool_scale, sln_g, sln_b, sgu_w, sgu_b_t, stats]
    small_shapes = [(len(POOL_WINDOWS) + N_HEADS, GROUP, GROUP), (VEC_ROWS, D_MODEL)]
    return pl.pallas_call(
        body, name="mixer_bwd", grid=(n_tiles,),
        in_specs=[tile(D_MODEL), next_halo, tile(IN_COLS), prev_halo] + [whole(a) for a in consts],
        out_specs=[tile(D_MODEL), tile(IN_COLS)] + [resident(s) for s in small_shapes],
        out_shape=[jax.ShapeDtypeStruct((tokens, D_MODEL), F32), jax.ShapeDtypeStruct((tokens, IN_COLS), BF16)]
                  + [jax.ShapeDtypeStruct(s, F32) for s in small_shapes],
        scratch_shapes=[pltpu.VMEM((tt, SGU_WIDTH), F32), pltpu.VMEM((tt, SGU_WIDTH), F32)],
        compiler_params=pltpu.CompilerParams(dimension_semantics=("arbitrary",), vmem_limit_bytes=VMEM_LIMIT_BYTES),
    )(dr1, dr1, proj, proj, *consts)


def _chip_sums(kept, from_sibling, place, name):
    na = len(kept)

    def body(place_ref, *refs):
        kept_refs, sib_refs = refs[:na], refs[na:2 * na]
        bf_refs, own_refs = refs[2 * na:3 * na], refs[3 * na:]
        q = pl.program_id(0)
        for a in range(na):
            s = kept_refs[a][...] + sib_refs[a][...]
            bf_refs[a][...] = s.astype(BF16)

            @pl.when(q == place_ref[1])
            def _(a=a, s=s):
                own_refs[a][...] = s

    by_chip = [pl.BlockSpec((None,) + k.shape[1:], lambda q, pr: (q, 0, 0)) for k in kept]
    grid_spec = pltpu.PrefetchScalarGridSpec(
        num_scalar_prefetch=1, grid=(N_CHIPS,), in_specs=by_chip + by_chip,
        out_specs=by_chip + [pl.BlockSpec(k.shape[1:], lambda q, pr: (0, 0)) for k in kept])
    return pl.pallas_call(
        body, name=name, grid_spec=grid_spec,
        out_shape=[jax.ShapeDtypeStruct(k.shape, BF16) for k in kept]
                  + [jax.ShapeDtypeStruct(k.shape[1:], F32) for k in kept],
        compiler_params=pltpu.CompilerParams(dimension_semantics=("arbitrary",), vmem_limit_bytes=VMEM_LIMIT_BYTES),
    )(place, *kept, *from_sibling)


def _owner_copies(src_ref, land_ref, sems):
    x, y, c = _mesh_position()
    return [pltpu.make_async_remote_copy(
        src_ref=src_ref.at[2 * cx + cy], dst_ref=land_ref.at[j], send_sem=sems[j], recv_sem=sems[3 + j],
        device_id=(cx, cy, c), device_id_type=MESH) for j, (cx, cy) in enumerate(_other_chips(x, y))]


def _send_to_owners_start(chip_partial, name):
    land = _received_shape(chip_partial)

    def body(src_ref, land_ref, *rest):
        for cp in _owner_copies(src_ref, land_ref, rest[:6]):
            cp.start()
        rest[8][...] = jnp.zeros_like(rest[8])

    hbm = pl.BlockSpec(memory_space=pltpu.HBM)
    sem = pl.BlockSpec(memory_space=pltpu.SEMAPHORE)
    outs = pl.pallas_call(
        body, name=name,
        out_shape=[pltpu.SemaphoreType.DMA(())] * 6 + [pltpu.HBM(chip_partial.shape, chip_partial.dtype),
                                                       pltpu.HBM(land.shape, land.dtype),
                                                       jax.ShapeDtypeStruct((8, GROUP), F32)],
        in_specs=[hbm, hbm], out_specs=[sem] * 6 + [hbm, hbm, pl.BlockSpec(memory_space=pltpu.VMEM)],
        input_output_aliases={0: 6, 1: 7},
        compiler_params=pltpu.CompilerParams(has_side_effects=pltpu.SideEffectType.DATAFLOW_SIDE_EFFECTING),
    )(pltpu.with_memory_space_constraint(chip_partial, pltpu.HBM),
      pltpu.with_memory_space_constraint(lax.empty(land.shape, land.dtype), pltpu.HBM))
    return outs[:6], outs[6], outs[7], outs[8]


def _send_to_owners_wait(sems, src_thru, land_thru, after, name):
    def body(src_ref, land_ref, *rest):
        for cp in _owner_copies(src_ref, land_ref, rest[:6]):
            cp.wait_send()
            cp.wait_recv()

    hbm = pl.BlockSpec(memory_space=pltpu.HBM)
    sem = pl.BlockSpec(memory_space=pltpu.SEMAPHORE)
    return pl.pallas_call(
        body, name=name,
        out_shape=[pltpu.HBM(src_thru.shape, src_thru.dtype), pltpu.HBM(land_thru.shape, land_thru.dtype)],
        in_specs=[hbm, hbm] + [sem] * 6 + [pl.BlockSpec(memory_space=pl.ANY)], out_specs=[hbm, hbm],
        input_output_aliases={0: 0, 1: 1},
        compiler_params=pltpu.CompilerParams(has_side_effects=pltpu.SideEffectType.DATAFLOW_SIDE_EFFECTING),
    )(src_thru, land_thru, *sems, after)[1]


def _adamw_math(w, g, m, v):
    m = ADAM_B1 * m + (1.0 - ADAM_B1) * g
    v = ADAM_B2 * v + (1.0 - ADAM_B2) * (g * g)
    m_hat = m / (1.0 - ADAM_B1 ** ADAM_STEP)
    v_hat = v / (1.0 - ADAM_B2 ** ADAM_STEP)
    delta = -ADAM_LR * (m_hat / (jnp.sqrt(v_hat) + ADAM_EPS) + ADAM_WD * w)
    return delta, m, v


def _sum_adamw(w, m, v, own, received, name, after=None):
    rows, cols = w.shape
    rb = _row_block(rows, 256)

    def body(w_ref, m_ref, v_ref, own_ref, rec_ref, *rest):
        g_ref, d_ref, mo_ref, vo_ref = rest[-4:]
        g = own_ref[...]
        for j in range(3):
            g = g + rec_ref[j].astype(F32)
        g_ref[...] = g
        d_ref[...], mo_ref[...], vo_ref[...] = _adamw_math(w_ref[...], g, m_ref[...], v_ref[...])

    spec = pl.BlockSpec((rb, cols), lambda r: (r, 0))
    anchors = [] if after is None else [after]
    return pl.pallas_call(
        body, name=name, grid=(rows // rb,),
        in_specs=[spec] * 4 + [pl.BlockSpec((3, rb, cols), lambda r: (0, r, 0))]
                 + [pl.BlockSpec(memory_space=pl.ANY)] * len(anchors),
        out_specs=[spec] * 4, out_shape=[jax.ShapeDtypeStruct((rows, cols), F32)] * 4,
        compiler_params=pltpu.CompilerParams(dimension_semantics=("arbitrary",)),
    )(w, m, v, own, received, *anchors)


SMALL_NAMES = ("pool_w", "sgu_w", "pool_scale", "sgu_ln_g", "sgu_ln_b", "sgu_b", "ln1_g", "ln1_b", "ln2_g", "ln2_b")
_SMALL_VEC_ROWS = {"pool_scale": (ROW_POOL_SCALE, POOL_WIDTH), "sgu_ln_g": (ROW_SLN_G, SGU_WIDTH),
                   "sgu_ln_b": (ROW_SLN_B, SGU_WIDTH), "ln1_g": (ROW_LN1_G, D_MODEL), "ln1_b": (ROW_LN1_B, D_MODEL),
                   "ln2_g": (ROW_LN2_G, D_MODEL), "ln2_b": (ROW_LN2_B, D_MODEL)}
_SMALL_MAT_FIRST = {"pool_w": 0, "sgu_w": len(POOL_WINDOWS)}


def _small_sum_adamw(mats_all, vecs_all, w, m, v):
    n = len(SMALL_NAMES)

    def body(mats_ref, vecs_ref, *refs):
        w_refs, m_refs, v_refs = refs[:n], refs[n:2 * n], refs[2 * n:3 * n]
        loss_ref = refs[3 * n]
        g_refs, d_refs, mo_refs, vo_refs = (refs[3 * n + 1 + k * n:3 * n + 1 + (k + 1) * n] for k in range(4))
        vec_scr = refs[7 * n + 1]

        def update(k, idx, g):
            d, mo, vo = _adamw_math(w_refs[k][idx], g, m_refs[k][idx], v_refs[k][idx])
            g_refs[k][idx], d_refs[k][idx], mo_refs[k][idx], vo_refs[k][idx] = g, d, mo, vo

        total = vecs_ref[0]
        for dev in range(1, N_DEV):
            total = total + vecs_ref[dev]
        vec_scr[...] = total
        for k, name in enumerate(SMALL_NAMES):
            if name in _SMALL_MAT_FIRST:
                for b in range(4):
                    g = mats_ref[0, _SMALL_MAT_FIRST[name] + b]
                    for dev in range(1, N_DEV):
                        g = g + mats_ref[dev, _SMALL_MAT_FIRST[name] + b]
                    update(k, (0, b), g)
            elif name == "sgu_b":
                update(k, (0,), vec_scr[ROW_SGU_B:ROW_SGU_B + N_HEADS, 0:GROUP])
            else:
                row, width = _SMALL_VEC_ROWS[name]
                update(k, (slice(None), slice(None)), vec_scr[row:row + 1, 0:width])
        loss = jnp.sum(vec_scr[ROW_LOSS:ROW_LOSS + 1, :], axis=1, keepdims=True) * (0.5 / D_MODEL)
        loss_ref[...] = jnp.broadcast_to(loss, loss_ref.shape)

    vmem = pl.BlockSpec(memory_space=pltpu.VMEM)
    shapes = [jax.ShapeDtypeStruct(w[k].shape, F32) for k in SMALL_NAMES]
    outs = pl.pallas_call(
        body, name="small_sum_adamw",
        out_shape=[jax.ShapeDtypeStruct((8, GROUP), F32)] + shapes * 4,
        in_specs=[vmem] * (2 + 3 * n), out_specs=[vmem] * (1 + 4 * n),
        scratch_shapes=[pltpu.VMEM((VEC_ROWS, D_MODEL), F32)],
    )(mats_all, vecs_all, *[w[k] for k in SMALL_NAMES], *[m[k] for k in SMALL_NAMES], *[v[k] for k in SMALL_NAMES])
    return outs[0], outs[1:1 + n], outs[1 + n:1 + 2 * n], outs[1 + 2 * n:1 + 3 * n], outs[1 + 3 * n:]


def kernel(x, w_in, pool_w, pool_scale, sgu_ln_g, sgu_ln_b, sgu_w, sgu_b, w_out, ln1_g, ln1_b, w_gate_up, w_down, ln2_g, ln2_b, loss_target, m_w_in, m_pool_w, m_pool_scale, m_sgu_ln_g, m_sgu_ln_b, m_sgu_w, m_sgu_b, m_w_out, m_ln1_g, m_ln1_b, m_w_gate_up, m_w_down, m_ln2_g, m_ln2_b, v_w_in, v_pool_w, v_pool_scale, v_sgu_ln_g, v_sgu_ln_b, v_sgu_w, v_sgu_b, v_w_out, v_ln1_g, v_ln1_b, v_w_gate_up, v_w_down, v_ln2_g, v_ln2_b):
    bl, seq, _ = x.shape
    tokens = bl * seq
    x2d = x.reshape(tokens, D_MODEL)
    tgt2d = loss_target.reshape(tokens, D_MODEL)
    my_c = lax.axis_index("c")
    place = jnp.stack([my_c, 2 * lax.axis_index("x") + lax.axis_index("y")]).astype(jnp.int32)

    win_t, wout = _all_gather_rows([w_in[0].T.astype(BF16), w_out[0].astype(BF16)], "weight_all_gather")

    pool_w3, sgu_w3 = pool_w[0], sgu_w[0]
    sgu_b_t = sgu_b[0].T
    proj, xhat1, rstd1, mix_bf, x_bf, wgu_t, wdown = _mixer_fwd(
        x2d, win_t, wout, pool_w3, pool_scale, sgu_ln_g, sgu_ln_b, sgu_w3, sgu_b_t, ln1_g, ln1_b,
        [w_gate_up[0].T.astype(BF16), w_down[0].astype(BF16)], seq)
    dr1, dr1_bf, h_bf, dr2_bf, a_bf, dgu_bf, stats = _ffn_fwd_bwd(
        xhat1, rstd1, tgt2d, wgu_t, wdown, ln1_g, ln1_b, ln2_g, ln2_b)

    kept_gu, sib_gu = _wgrad_exchange(dgu_bf, h_bf, 1, "wgrad_gate_up")
    bf_gu, own_gu = _chip_sums([kept_gu], [sib_gu], place, "chip_sum_w_gate_up")
    kept_dn, sib_dn, rec_gu = _wgrad_exchange(a_bf, dr2_bf, 2, "wgrad_down", scatter_partials=[bf_gu])
    bf_dn, own_dn = _chip_sums([kept_dn], [sib_dn], place, "chip_sum_w_down")
    kept_out, sib_out, rec_dn = _wgrad_exchange(mix_bf, dr1_bf, N_CHIPS, "wgrad_out", scatter_partials=[bf_dn])
    bf_out, own_out = _chip_sums([kept_out], [sib_out], place, "chip_sum_w_out")
    gx, dproj_bf, d_mats, d_vecs = _mixer_bwd(
        dr1, proj, win_t, wout, pool_w3, pool_scale, sgu_ln_g, sgu_ln_b, sgu_w3, sgu_b_t, stats, seq)
    kept_in, sib_in, mats_all, vecs_all, rec_out = _wgrad_exchange(
        dproj_bf, x_bf, N_CHIPS, "wgrad_in", gather_rows=[d_mats.reshape(-1, GROUP), d_vecs], scatter_partials=[bf_out])
    bf_in, own_in = _chip_sums([kept_in], [sib_in], place, "chip_sum_w_in")
    sems_in, bf_in_thru, land_in, after = _send_to_owners_start(bf_in, "grad_scatter_w_in_start")

    grads, deltas, new_m, new_v = {}, {}, {}, {}
    for nm, w, m, v, own, rec, transposed in (("w_gate_up", w_gate_up, m_w_gate_up, v_w_gate_up, own_gu, rec_gu, True),
                                              ("w_down", w_down, m_w_down, v_w_down, own_dn, rec_dn, False),
                                              ("w_out", w_out, m_w_out, v_w_out, own_out, rec_out, False),
                                              ("w_in", w_in, m_w_in, v_w_in, own_in, None, True)):
        rows = (lambda a: a[0].T) if transposed else (lambda a: a[0])
        back = (lambda a: a.T[None]) if transposed else (lambda a: a[None])
        if rec is None:
            rec = _send_to_owners_wait(sems_in, bf_in_thru, land_in, after, "grad_scatter_w_in_wait")
            after = None
        g, d, mo, vo = _sum_adamw(rows(w), rows(m), rows(v), own, rec, "adamw_" + nm, after=after)
        after = vo
        grads[nm], deltas[nm], new_m[nm], new_v[nm] = back(g), back(d), back(mo), back(vo)

    small_w = {"pool_w": pool_w, "pool_scale": pool_scale, "sgu_ln_g": sgu_ln_g, "sgu_ln_b": sgu_ln_b, "sgu_w": sgu_w,
               "sgu_b": sgu_b, "ln1_g": ln1_g, "ln1_b": ln1_b, "ln2_g": ln2_g, "ln2_b": ln2_b}
    small_m = {"pool_w": m_pool_w, "pool_scale": m_pool_scale, "sgu_ln_g": m_sgu_ln_g, "sgu_ln_b": m_sgu_ln_b,
               "sgu_w": m_sgu_w, "sgu_b": m_sgu_b, "ln1_g": m_ln1_g, "ln1_b": m_ln1_b, "ln2_g": m_ln2_g, "ln2_b": m_ln2_b}
    small_v = {"pool_w": v_pool_w, "pool_scale": v_pool_scale, "sgu_ln_g": v_sgu_ln_g, "sgu_ln_b": v_sgu_ln_b,
               "sgu_w": v_sgu_w, "sgu_b": v_sgu_b, "ln1_g": v_ln1_g, "ln1_b": v_ln1_b, "ln2_g": v_ln2_g, "ln2_b": v_ln2_b}
    loss_blk, g_small, d_small, m_small, v_small = _small_sum_adamw(
        mats_all.reshape(N_DEV, 2 * N_HEADS, GROUP, GROUP), vecs_all.reshape(N_DEV, VEC_ROWS, D_MODEL),
        small_w, small_m, small_v)
    for vals, dst in ((g_small, grads), (d_small, deltas), (m_small, new_m), (v_small, new_v)):
        dst.update(zip(SMALL_NAMES, vals))

    order = ["w_in", "pool_w", "pool_scale", "sgu_ln_g", "sgu_ln_b", "sgu_w", "sgu_b", "w_out", "ln1_g", "ln1_b",
             "w_gate_up", "w_down", "ln2_g", "ln2_b"]
    return (loss_blk[0, 0], gx.reshape(bl, seq, D_MODEL), *[grads[k] for k in order], *[deltas[k] for k in order],
            *[new_m[k] for k in order], *[new_v[k] for k in order])
```

```python
import functools

import jax
import jax.numpy as jnp
from jax import lax
from jax.experimental import pallas as pl
from jax.experimental.pallas import tpu as pltpu

F32 = jnp.float32
BF16 = jnp.bfloat16
MESH = pl.DeviceIdType.MESH

D_MODEL = 1024
POOL_WIDTH = 512
SGU_WIDTH = 512
POOL_WINDOWS = (2, 4, 8, 16)
GROUP = 128
N_HEADS = 4
IN_COLS = POOL_WIDTH + 2 * SGU_WIDTH
D_FF = 2816
LN_EPS = 1e-5
ALPHA = float(2.0 ** 0.25)
HALO = 16
N_DEV = 8
N_CHIPS = 4

ADAM_LR = 0.001
ADAM_B1 = 0.9
ADAM_B2 = 0.999
ADAM_EPS = 1e-08
ADAM_WD = 0.01
ADAM_STEP = 10

VMEM_LIMIT_BYTES = 56 * 1024 * 1024

_SQRT_HALF = 0.7071067811865476
_INV_SQRT_2PI = 0.3989422804014327


def _dot_nn(a, b):
    return lax.dot_general(a, b, (((1,), (0,)), ((), ())), preferred_element_type=F32)


def _dot_nt(a, b):
    return lax.dot_general(a, b, (((1,), (1,)), ((), ())), preferred_element_type=F32)


def _dot_tn(a, b):
    return lax.dot_general(a, b, (((0,), (0,)), ((), ())), preferred_element_type=F32)


def _gelu(x):
    return 0.5 * x * (1.0 + lax.erf(x * _SQRT_HALF))


def _gelu_grad(x):
    return 0.5 * (1.0 + lax.erf(x * _SQRT_HALF)) + x * jnp.exp(-0.5 * x * x) * _INV_SQRT_2PI


def _ln_stats(r):
    mu = jnp.mean(r, axis=-1, keepdims=True)
    d = r - mu
    var = jnp.mean(d * d, axis=-1, keepdims=True)
    rstd = lax.rsqrt(var + LN_EPS)
    return d * rstd, rstd


def _ln_bwd(dout, xhat, rstd, g):
    dxh = dout * g
    m1 = jnp.mean(dxh, axis=-1, keepdims=True)
    m2 = jnp.mean(dxh * xhat, axis=-1, keepdims=True)
    return rstd * (dxh - m1 - xhat * m2)


def _rowsum(a):
    return jnp.sum(a, axis=0, keepdims=True)


def _pool_fwd(xp, xp_prev, inv_cnt, w):
    s = jnp.concatenate([xp_prev, xp], axis=0)
    k = 1
    while k < w:
        s = s + pltpu.roll(s, k, 0)
        k *= 2
    return s[HALO:, :] * inv_cnt - xp


def _pool_bwd(dpooled, dpooled_next, inv_cnt, inv_cnt_next, w):
    n = dpooled.shape[0] + HALO
    s = jnp.concatenate([dpooled * inv_cnt, dpooled_next * inv_cnt_next], axis=0)
    k = 1
    while k < w:
        s = s + pltpu.roll(s, n - k, 0)
        k *= 2
    return s[: dpooled.shape[0], :] - dpooled


def _inv_count(pos, w):
    return 1.0 / jnp.minimum(pos + 1, w).astype(F32)


def _to_head_major(a, h, nc):
    return jnp.concatenate(
        [a[c * GROUP:(c + 1) * GROUP, h * GROUP:(h + 1) * GROUP] for c in range(nc)], axis=1)


def _masked_sgu_w(sw_ref, h):
    row = lax.broadcasted_iota(jnp.int32, (GROUP, GROUP), 0)
    col = lax.broadcasted_iota(jnp.int32, (GROUP, GROUP), 1)
    return jnp.where(row >= col, sw_ref[h], 0.0)


def _row_block(rows, limit):
    return max(b for b in range(16, min(rows, limit) + 1, 16) if rows % b == 0)


def _mesh_position():
    return lax.axis_index("x"), lax.axis_index("y"), lax.axis_index("c")


def _other_chips(x, y):
    return [(1 - x, y), (x, 1 - y), (1 - x, 1 - y)]


class _TwoLevelGather:
    def __init__(self, ins, outs, send_sems, recv_sems, local_sems):
        self.ins, self.outs = ins, outs
        self.send_sems, self.recv_sems, self.local_sems = send_sems, recv_sems, local_sems
        self.na = len(ins)
        x, y, c = _mesh_position()
        self.c = c
        self.me, self.sibling = (x, y, c), (x, y, 1 - c)
        self.chips = _other_chips(x, y)
        self.relay_from = (x + (1 - c) * (1 - 2 * x), y + c * (1 - 2 * y))
        self.relay_to = (x + c * (1 - 2 * x), y + (1 - c) * (1 - 2 * y))

    def _rows(self, a, px, py, pc):
        n = self.ins[a].shape[0]
        return self.outs[a].at[pl.ds((4 * px + 2 * py + pc) * n, n), :]

    def _copy(self, a, k, block, to, src=None):
        return pltpu.make_async_remote_copy(
            src_ref=self._rows(a, *block) if src is None else src, dst_ref=self._rows(a, *block),
            send_sem=self.send_sems.at[a * 7 + k], recv_sem=self.recv_sems.at[a * 7 + k],
            device_id=to, device_id_type=MESH)

    def _mine(self, a):
        return pltpu.make_async_copy(self.ins[a], self._rows(a, *self.me), self.local_sems.at[a])

    def start(self):
        for a in range(self.na):
            self._mine(a).start()
        for a in range(self.na):
            self._copy(a, 0, self.me, self.sibling, src=self.ins[a]).start()
            for j, chip in enumerate(self.chips[:2]):
                self._copy(a, 1 + j, self.me, (*chip, self.c), src=self.ins[a]).start()

    def relay(self, a):
        c, block = self.c, (*self.relay_from, self.c)
        self._copy(a, 1 + c, block, self.me).wait_recv()
        self._copy(a, 3, block, (*self.relay_to, c)).start()
        self._copy(a, 4 + c, block, self.sibling).start()

    def pass_on(self, a):
        c = self.c
        self._copy(a, 2 - c, (*self.relay_to, c), self.me).wait_recv()
        self._copy(a, 5 - c, (*self.relay_to, c), self.sibling).start()
        self._copy(a, 3, (*self.chips[2], c), self.me).wait_recv()
        self._copy(a, 6, (*self.chips[2], c), self.sibling).start()

    def finish(self):
        for a in range(self.na):
            self._copy(a, 0, self.sibling, self.me).wait_recv()
            for j, chip in enumerate(self.chips):
                self._copy(a, 4 + j, (*chip, 1 - self.c), self.me).wait_recv()
        for a in range(self.na):
            for k in range(7):
                self._copy(a, k, self.me, self.sibling, src=self.ins[a]).wait_send()
            self._mine(a).wait()

    @staticmethod
    def scratch(na):
        return [pltpu.SemaphoreType.DMA((7 * na,)), pltpu.SemaphoreType.DMA((7 * na,)), pltpu.SemaphoreType.DMA((na,))]


def _gathered_shape(s):
    return jax.ShapeDtypeStruct((N_DEV * s.shape[0], s.shape[1]), s.dtype)


def _all_gather_rows(shards, name):
    na = len(shards)

    def body(*refs):
        gather = _TwoLevelGather(refs[:na], refs[na:2 * na], *refs[2 * na:])
        gather.start()
        for a in range(na):
            gather.relay(a)
        for a in range(na):
            gather.pass_on(a)
        gather.finish()

    any_spec = pl.BlockSpec(memory_space=pl.ANY)
    return pl.pallas_call(
        body, name=name, out_shape=[_gathered_shape(s) for s in shards],
        in_specs=[any_spec] * na, out_specs=[any_spec] * na, scratch_shapes=_TwoLevelGather.scratch(na),
    )(*shards)


def _mixer_fwd(x2d, win_t, wout, pool_w, pool_scale, sln_g, sln_b, sgu_w, sgu_b_t, ln1_g, ln1_b, later_shards, seq):
    tokens = x2d.shape[0]
    tt = min(512, seq)
    tiles_per_seq = seq // tt
    nc = tt // GROUP
    n_tiles = tokens // tt
    n_later = len(later_shards)

    def body(x_ref, xh_ref, win_ref, wout_ref, pw_ref, ps_ref, lg_ref, lb_ref, sw_ref, sb_ref, g1_ref, b1_ref, *rest):
        shard_refs, rest = rest[:n_later], rest[n_later:]
        proj_ref, xhat_ref, rstd_ref, mix_ref, xbf_ref = rest[:5]
        gathered_refs, rest = rest[5:5 + n_later], rest[5 + n_later:]
        mix_scr, send_sems, recv_sems, local_sems = rest
        i = pl.program_id(0)
        gather = _TwoLevelGather(shard_refs, gathered_refs, send_sems, recv_sems, local_sems)

        @pl.when(i == 0)
        def _():
            gather.start()

        tile_in_seq = i % tiles_per_seq
        x = x_ref[...]
        xb = x.astype(BF16)
        xbf_ref[...] = xb
        proj = _dot_nt(xb, win_ref[...])
        proj_ref[...] = proj
        xp_prev = _dot_nt(xh_ref[...].astype(BF16), win_ref[0:POOL_WIDTH, :])
        xp_prev = jnp.where(tile_in_seq == 0, 0.0, xp_prev)
        pos = tile_in_seq * tt + lax.broadcasted_iota(jnp.int32, (tt, 1), 0)
        for g, w in enumerate(POOL_WINDOWS):
            sl = slice(g * GROUP, (g + 1) * GROUP)
            pooled = _pool_fwd(proj[:, sl], xp_prev[:, sl], _inv_count(pos, w), w)
            pre = _dot_nn(pooled.astype(BF16), pw_ref[g].astype(BF16))
            mix_scr[:, sl] = pre * ps_ref[:, sl]
        u = _gelu(proj[:, POOL_WIDTH:POOL_WIDTH + SGU_WIDTH])
        v = _gelu(proj[:, POOL_WIDTH + SGU_WIDTH:])
        vhat, _ = _ln_stats(v)
        v_ln = vhat * lg_ref[...] + lb_ref[...]
        for h in range(N_HEADS):
            ws = _masked_sgu_w(sw_ref, h).astype(BF16)
            mixed = _dot_nn(ws, _to_head_major(v_ln, h, nc).astype(BF16)) + sb_ref[:, h:h + 1]
            for c in range(nc):
                rs = slice(c * GROUP, (c + 1) * GROUP)
                mix_scr[rs, POOL_WIDTH + h * GROUP:POOL_WIDTH + (h + 1) * GROUP] = (
                    u[rs, h * GROUP:(h + 1) * GROUP] * mixed[:, c * GROUP:(c + 1) * GROUP])
        mixb = mix_scr[...].astype(BF16)
        mix_ref[...] = mixb
        r1 = ALPHA * x + _dot_nn(mixb, wout_ref[...])
        xhat, rstd = _ln_stats(r1)
        xhat_ref[...] = xhat
        rstd_ref[...] = rstd

        for a in range(n_later):
            relay_tile = min(n_tiles // 2 + a, n_tiles - 1)

            @pl.when(i == relay_tile)
            def _(a=a):
                gather.relay(a)

            @pl.when(i == max(n_tiles - n_later + a, relay_tile))
            def _(a=a):
                gather.pass_on(a)

        @pl.when(i == n_tiles - 1)
        def _():
            gather.finish()

    def tile(cols):
        return pl.BlockSpec((tt, cols), lambda i: (i, 0))

    def whole(a):
        nd = a.ndim
        return pl.BlockSpec(a.shape, lambda i: (0,) * nd)

    any_spec = pl.BlockSpec(memory_space=pl.ANY)
    halo = pl.BlockSpec((HALO, D_MODEL), lambda i: (jnp.maximum(i * (tt // HALO) - 1, 0), 0))
    consts = [win_t, wout, pool_w, pool_scale, sln_g, sln_b, sgu_w, sgu_b_t, ln1_g, ln1_b]
    return pl.pallas_call(
        body, name="mixer_fwd", grid=(n_tiles,),
        in_specs=[tile(D_MODEL), halo] + [whole(a) for a in consts] + [any_spec] * n_later,
        out_specs=[tile(IN_COLS), tile(D_MODEL), tile(1), tile(D_MODEL), tile(D_MODEL)] + [any_spec] * n_later,
        out_shape=[jax.ShapeDtypeStruct((tokens, IN_COLS), F32), jax.ShapeDtypeStruct((tokens, D_MODEL), F32),
                   jax.ShapeDtypeStruct((tokens, 1), F32), jax.ShapeDtypeStruct((tokens, D_MODEL), BF16),
                   jax.ShapeDtypeStruct((tokens, D_MODEL), BF16)] + [_gathered_shape(s) for s in later_shards],
        scratch_shapes=[pltpu.VMEM((tt, D_MODEL), F32)] + _TwoLevelGather.scratch(n_later),
        compiler_params=pltpu.CompilerParams(dimension_semantics=("arbitrary",), vmem_limit_bytes=VMEM_LIMIT_BYTES),
    )(x2d, x2d, *consts, *later_shards)


def _ffn_fwd_bwd(xhat1, rstd1, target, wgu_t, wdown, ln1_g, ln1_b, ln2_g, ln2_b):
    tokens = xhat1.shape[0]
    tt = min(256, tokens)

    def body(xhat_ref, rstd_ref, tgt_ref, wgu_hbm, wd_hbm, g1_ref, b1_ref, g2_ref, b2_ref,
             dr1_ref, dr1bf_ref, hbf_ref, dr2bf_ref, a_ref, dgu_ref, stats_ref, wgu_ref, wd_ref, gu_scr, sems):
        i = pl.program_id(0)

        @pl.when(i == 0)
        def _():
            loads = [pltpu.make_async_copy(wgu_hbm, wgu_ref, sems.at[0]),
                     pltpu.make_async_copy(wd_hbm, wd_ref, sems.at[1])]
            for cp in loads:
                cp.start()
            stats_ref[...] = jnp.zeros_like(stats_ref)
            for cp in loads:
                cp.wait()

        xhat1_t = xhat_ref[...]
        h = xhat1_t * g1_ref[...] + b1_ref[...]
        hb = h.astype(BF16)
        hbf_ref[...] = hb
        gate = _dot_nt(hb, wgu_ref[0:D_FF, :])
        up = _dot_nt(hb, wgu_ref[D_FF:, :])
        gu_scr[:, 0:D_FF] = gate
        gu_scr[:, D_FF:] = up
        ab = (gate * jax.nn.sigmoid(gate) * up).astype(BF16)
        a_ref[...] = ab
        xhat2, rstd2 = _ln_stats(ALPHA * h + _dot_nn(ab, wd_ref[...]))
        err = xhat2 * g2_ref[...] + b2_ref[...] - tgt_ref[...]
        dy = err * (1.0 / D_MODEL)
        stats_ref[0:1, :] += _rowsum(dy * xhat2)
        stats_ref[1:2, :] += _rowsum(dy)
        stats_ref[4:5, :] += _rowsum(err * err)
        dr2 = _ln_bwd(dy, xhat2, rstd2, g2_ref[...])
        dr2b = dr2.astype(BF16)
        dr2bf_ref[...] = dr2b
        da = _dot_nt(dr2b, wd_ref[...])
        gate = gu_scr[:, 0:D_FF]
        up = gu_scr[:, D_FF:]
        sg = jax.nn.sigmoid(gate)
        dgate = (da * up * (sg * (1.0 + gate * (1.0 - sg)))).astype(BF16)
        dup = (da * (gate * sg)).astype(BF16)
        dgu_ref[:, 0:D_FF] = dgate
        dgu_ref[:, D_FF:] = dup
        dh = ALPHA * dr2 + _dot_nn(dgate, wgu_ref[0:D_FF, :]) + _dot_nn(dup, wgu_ref[D_FF:, :])
        stats_ref[2:3, :] += _rowsum(dh * xhat1_t)
        stats_ref[3:4, :] += _rowsum(dh)
        dr1 = _ln_bwd(dh, xhat1_t, rstd_ref[...], g1_ref[...])
        dr1_ref[...] = dr1
        dr1bf_ref[...] = dr1.astype(BF16)

    def tile(cols):
        return pl.BlockSpec((tt, cols), lambda i: (i, 0))

    def whole(a):
        nd = a.ndim
        return pl.BlockSpec(a.shape, lambda i: (0,) * nd)

    any_spec = pl.BlockSpec(memory_space=pl.ANY)
    vecs = [ln1_g, ln1_b, ln2_g, ln2_b]
    return pl.pallas_call(
        body, name="ffn_fwd_bwd", grid=(tokens // tt,),
        in_specs=[tile(D_MODEL), tile(1), tile(D_MODEL), any_spec, any_spec] + [whole(a) for a in vecs],
        out_specs=[tile(D_MODEL), tile(D_MODEL), tile(D_MODEL), tile(D_MODEL), tile(D_FF), tile(2 * D_FF),
                   pl.BlockSpec((8, D_MODEL), lambda i: (0, 0))],
        out_shape=[jax.ShapeDtypeStruct((tokens, D_MODEL), F32), jax.ShapeDtypeStruct((tokens, D_MODEL), BF16),
                   jax.ShapeDtypeStruct((tokens, D_MODEL), BF16),
                   jax.ShapeDtypeStruct((tokens, D_MODEL), BF16), jax.ShapeDtypeStruct((tokens, D_FF), BF16),
                   jax.ShapeDtypeStruct((tokens, 2 * D_FF), BF16), jax.ShapeDtypeStruct((8, D_MODEL), F32)],
        scratch_shapes=[pltpu.VMEM(wgu_t.shape, BF16), pltpu.VMEM(wdown.shape, BF16),
                        pltpu.VMEM((tt, 2 * D_FF), F32), pltpu.SemaphoreType.DMA((2,))],
        compiler_params=pltpu.CompilerParams(dimension_semantics=("arbitrary",), vmem_limit_bytes=VMEM_LIMIT_BYTES),
    )(xhat1, rstd1, target, wgu_t, wdown, *vecs)


def _wgrad_exchange(lhs, rhs, chips_per_block, name, gather_rows=(), after=None):
    tokens, n_all = lhs.shape
    m = rhs.shape[1]
    n = n_all // N_DEV
    tw = min(2048, tokens)
    nt = tokens // tw
    cpb = chips_per_block
    nj = N_CHIPS // cpb
    ng = len(gather_rows)
    anchors = [] if after is None else [after]

    def body(l_ref, r_ref, *rest):
        small_refs, rest = rest[:ng], rest[ng + len(anchors):]
        kept_ref, sib_ref = rest[:2]
        gathered_refs, rest = rest[2:2 + ng], rest[2 + ng:]
        acc, sendbuf, send_sems, recv_sems = rest[:4]
        j, t = pl.program_id(0), pl.program_id(1)
        first, last = (j == 0) & (t == 0), (j == nj - 1) & (t == nt - 1)
        x, y, c = _mesh_position()
        if ng:
            gather = _TwoLevelGather(small_refs, gathered_refs, *rest[4:7])

            @pl.when(first)
            def _():
                gather.start()

        def copy(q):
            return pltpu.make_async_remote_copy(
                src_ref=sendbuf.at[q], dst_ref=sib_ref.at[q], send_sem=send_sems.at[q], recv_sem=recv_sems.at[q],
                device_id=(x, y, 1 - c), device_id_type=MESH)

        @pl.when(t == 0)
        def _():
            acc[...] = jnp.zeros_like(acc)

        acc[...] += _dot_tn(l_ref[...], r_ref[...])

        @pl.when(t == nt - 1)
        def _():
            for qq in range(cpb):
                q = j * cpb + qq
                kept_ref[qq] = acc[pl.ds(pl.multiple_of(qq * 2 * n + c * n, 8), n), :]
                sendbuf[q] = acc[pl.ds(pl.multiple_of(qq * 2 * n + (1 - c) * n, 8), n), :]
                copy(q).start()

        if ng:
            @pl.when((j == nj - 1) & (t == nt // 2))
            def _():
                for a in range(ng):
                    gather.relay(a)

            @pl.when(last)
            def _():
                for a in range(ng):
                    gather.pass_on(a)
                gather.finish()

        @pl.when(last)
        def _():
            for q in range(N_CHIPS):
                copy(q).wait_send()
                copy(q).wait_recv()

    shard4 = jax.ShapeDtypeStruct((N_CHIPS, n, m), F32)
    any_spec = pl.BlockSpec(memory_space=pl.ANY)
    return pl.pallas_call(
        body, name=name, grid=(nj, nt),
        in_specs=[pl.BlockSpec((tw, 2 * n * cpb), lambda j, t: (t, j)), pl.BlockSpec((tw, m), lambda j, t: (t, 0))]
                 + [any_spec] * (ng + len(anchors)),
        out_specs=[pl.BlockSpec((cpb, n, m), lambda j, t: (j, 0, 0)), any_spec] + [any_spec] * ng,
        out_shape=[shard4, shard4] + [_gathered_shape(s) for s in gather_rows],
        scratch_shapes=[pltpu.VMEM((2 * n * cpb, m), F32), pltpu.VMEM((N_CHIPS, n, m), F32),
                        pltpu.SemaphoreType.DMA((N_CHIPS,)), pltpu.SemaphoreType.DMA((N_CHIPS,))]
                       + (_TwoLevelGather.scratch(ng) if ng else []),
        compiler_params=pltpu.CompilerParams(dimension_semantics=("arbitrary", "arbitrary"),
                                             vmem_limit_bytes=VMEM_LIMIT_BYTES),
    )(lhs, rhs, *gather_rows, *anchors)


def _received_shape(p):
    return jax.ShapeDtypeStruct((3,) + p.shape[1:], p.dtype)


ROW_POOL_SCALE, ROW_SLN_G, ROW_SLN_B, ROW_SGU_B = 0, 1, 2, 3
ROW_LN2_G, ROW_LN2_B, ROW_LN1_G, ROW_LN1_B, ROW_LOSS = 8, 9, 10, 11, 12
VEC_ROWS = 16


def _mixer_bwd(dr1, proj, win_t, wout, pool_w, pool_scale, sln_g, sln_b, sgu_w, sgu_b_t, stats, seq):
    tokens = dr1.shape[0]
    tt = min(512, seq)
    tiles_per_seq = seq // tt
    nc = tt // GROUP
    n_halo_blocks = tokens // HALO
    n_tiles = tokens // tt

    def body(dr1_ref, dr1n_ref, proj_ref, projh_ref, win_ref, wout_ref, pw_ref, ps_ref, lg_ref, lb_ref, sw_ref, sb_ref,
             stats_ref, gx_ref, dproj_ref, dmat_ref, dvec_ref, du_scr, dv_scr):
        i = pl.program_id(0)
        tile_in_seq = i % tiles_per_seq

        @pl.when(i == 0)
        def _():
            dmat_ref[...] = jnp.zeros_like(dmat_ref)
            dvec_ref[0:8, :] = jnp.zeros((8, D_MODEL), F32)
            dvec_ref[8:16, :] = stats_ref[...]

        dr1_t = dr1_ref[...]
        dr1b = dr1_t.astype(BF16)
        dmix = _dot_nt(dr1b, wout_ref[...])
        dpo_next = _dot_nt(dr1n_ref[...].astype(BF16), wout_ref[0:POOL_WIDTH, :])
        dpo_next = jnp.where(tile_in_seq == tiles_per_seq - 1, 0.0, dpo_next)
        proj = proj_ref[...]
        xp_prev = jnp.where(tile_in_seq == 0, 0.0, projh_ref[...])
        pos = tile_in_seq * tt + lax.broadcasted_iota(jnp.int32, (tt, 1), 0)
        pos_next = (tile_in_seq + 1) * tt + lax.broadcasted_iota(jnp.int32, (HALO, 1), 0)

        for g, w in enumerate(POOL_WINDOWS):
            sl = slice(g * GROUP, (g + 1) * GROUP)
            inv_cnt = _inv_count(pos, w)
            pwb = pw_ref[g].astype(BF16)
            pooledb = _pool_fwd(proj[:, sl], xp_prev[:, sl], inv_cnt, w).astype(BF16)
            pre = _dot_nn(pooledb, pwb)
            dpo = dmix[:, sl]
            dvec_ref[ROW_POOL_SCALE:ROW_POOL_SCALE + 1, sl] += _rowsum(dpo * pre)
            dsb = (dpo * ps_ref[:, sl]).astype(BF16)
            dmat_ref[g] += _dot_tn(pooledb, dsb)
            dpooled = _dot_nt(dsb, pwb)
            dpooled_next = _dot_nt((dpo_next[:, sl] * ps_ref[:, sl]).astype(BF16), pwb)
            dxp = _pool_bwd(dpooled, dpooled_next, inv_cnt, _inv_count(pos_next, w), w)
            dproj_ref[:, sl] = dxp.astype(BF16)

        zu = proj[:, POOL_WIDTH:POOL_WIDTH + SGU_WIDTH]
        zv = proj[:, POOL_WIDTH + SGU_WIDTH:]
        u = _gelu(zu)
        vhat, rstd_v = _ln_stats(_gelu(zv))
        v_ln = vhat * lg_ref[...] + lb_ref[...]
        dsg = dmix[:, POOL_WIDTH:]
        row = lax.broadcasted_iota(jnp.int32, (GROUP, GROUP), 0)
        col = lax.broadcasted_iota(jnp.int32, (GROUP, GROUP), 1)
        for h in range(N_HEADS):
            ws = _masked_sgu_w(sw_ref, h).astype(BF16)
            vh = _to_head_major(v_ln, h, nc).astype(BF16)
            mixed = _dot_nn(ws, vh) + sb_ref[:, h:h + 1]
            dsg_h = _to_head_major(dsg, h, nc)
            du_h = dsg_h * mixed
            dm_h = dsg_h * _to_head_major(u, h, nc)
            pos_sums = lax.dot_general(jnp.ones((8, nc * GROUP), F32), dm_h, (((1,), (1,)), ((), ())),
                                       precision=lax.Precision.HIGHEST, preferred_element_type=F32)
            dvec_ref[ROW_SGU_B + h:ROW_SGU_B + h + 1, 0:GROUP] += pos_sums[0:1, :]
            dmb = dm_h.astype(BF16)
            dmat_ref[len(POOL_WINDOWS) + h] += jnp.where(row >= col, _dot_nt(dmb, vh), 0.0)
            dv_h = _dot_tn(ws, dmb)
            for c in range(nc):
                rs = slice(c * GROUP, (c + 1) * GROUP)
                cs = slice(h * GROUP, (h + 1) * GROUP)
                du_scr[rs, cs] = du_h[:, c * GROUP:(c + 1) * GROUP]
                dv_scr[rs, cs] = dv_h[:, c * GROUP:(c + 1) * GROUP]
        dv_ln = dv_scr[...]
        dvec_ref[ROW_SLN_B:ROW_SLN_B + 1, 0:SGU_WIDTH] += _rowsum(dv_ln)
        dvec_ref[ROW_SLN_G:ROW_SLN_G + 1, 0:SGU_WIDTH] += _rowsum(dv_ln * vhat)
        dv = _ln_bwd(dv_ln, vhat, rstd_v, lg_ref[...])
        dproj_ref[:, POOL_WIDTH:POOL_WIDTH + SGU_WIDTH] = (du_scr[...] * _gelu_grad(zu)).astype(BF16)
        dproj_ref[:, POOL_WIDTH + SGU_WIDTH:] = (dv * _gelu_grad(zv)).astype(BF16)
        gx_ref[...] = ALPHA * dr1_t + _dot_nn(dproj_ref[...], win_ref[...])

    def tile(cols):
        return pl.BlockSpec((tt, cols), lambda i: (i, 0))

    def whole(a):
        nd = a.ndim
        return pl.BlockSpec(a.shape, lambda i: (0,) * nd)

    def resident(shape):
        nd = len(shape)
        return pl.BlockSpec(shape, lambda i: (0,) * nd)

    next_halo = pl.BlockSpec((HALO, D_MODEL), lambda i: (jnp.minimum((i + 1) * (tt // HALO), n_halo_blocks - 1), 0))
    prev_halo = pl.BlockSpec((HALO, POOL_WIDTH), lambda i: (jnp.maximum(i * (tt // HALO) - 1, 0), 0))
    consts = [win_t, wout, pool_w, pool_scale, sln_g, sln_b, sgu_w, sgu_b_t, stats]
    small_shapes = [(len(POOL_WINDOWS) + N_HEADS, GROUP, GROUP), (VEC_ROWS, D_MODEL)]
    return pl.pallas_call(
        body, name="mixer_bwd", grid=(n_tiles,),
        in_specs=[tile(D_MODEL), next_halo, tile(IN_COLS), prev_halo] + [whole(a) for a in consts],
        out_specs=[tile(D_MODEL), tile(IN_COLS)] + [resident(s) for s in small_shapes],
        out_shape=[jax.ShapeDtypeStruct((tokens, D_MODEL), F32), jax.ShapeDtypeStruct((tokens, IN_COLS), BF16)]
                  + [jax.ShapeDtypeStruct(s, F32) for s in small_shapes],
        scratch_shapes=[pltpu.VMEM((tt, SGU_WIDTH), F32), pltpu.VMEM((tt, SGU_WIDTH), F32)],
        compiler_params=pltpu.CompilerParams(dimension_semantics=("arbitrary",), vmem_limit_bytes=VMEM_LIMIT_BYTES),
    )(dr1, dr1, proj, proj, *consts)


def _chip_sums(kept, from_sibling, place, name):
    na = len(kept)

    def body(place_ref, *refs):
        kept_refs, sib_refs = refs[:na], refs[na:2 * na]
        bf_refs, own_refs = refs[2 * na:3 * na], refs[3 * na:]
        q = pl.program_id(0)
        for a in range(na):
            s = kept_refs[a][...] + sib_refs[a][...]
            bf_refs[a][...] = s.astype(BF16)

            @pl.when(q == place_ref[1])
            def _(a=a, s=s):
                own_refs[a][...] = s

    by_chip = [pl.BlockSpec((None,) + k.shape[1:], lambda q, pr: (q, 0, 0)) for k in kept]
    grid_spec = pltpu.PrefetchScalarGridSpec(
        num_scalar_prefetch=1, grid=(N_CHIPS,), in_specs=by_chip + by_chip,
        out_specs=by_chip + [pl.BlockSpec(k.shape[1:], lambda q, pr: (0, 0)) for k in kept])
    return pl.pallas_call(
        body, name=name, grid_spec=grid_spec,
        out_shape=[jax.ShapeDtypeStruct(k.shape, BF16) for k in kept]
                  + [jax.ShapeDtypeStruct(k.shape[1:], F32) for k in kept],
        compiler_params=pltpu.CompilerParams(dimension_semantics=("arbitrary",), vmem_limit_bytes=VMEM_LIMIT_BYTES),
    )(place, *kept, *from_sibling)


def _owner_copies(src_ref, land_ref, sems):
    x, y, c = _mesh_position()
    return [pltpu.make_async_remote_copy(
        src_ref=src_ref.at[2 * cx + cy], dst_ref=land_ref.at[j], send_sem=sems[j], recv_sem=sems[3 + j],
        device_id=(cx, cy, c), device_id_type=MESH) for j, (cx, cy) in enumerate(_other_chips(x, y))]


def _send_to_owners_start(chip_partial, name):
    land = _received_shape(chip_partial)

    def body(src_ref, land_ref, *rest):
        for cp in _owner_copies(src_ref, land_ref, rest[:6]):
            cp.start()
        rest[8][...] = jnp.zeros_like(rest[8])

    hbm = pl.BlockSpec(memory_space=pltpu.HBM)
    sem = pl.BlockSpec(memory_space=pltpu.SEMAPHORE)
    outs = pl.pallas_call(
        body, name=name,
        out_shape=[pltpu.SemaphoreType.DMA(())] * 6 + [pltpu.HBM(chip_partial.shape, chip_partial.dtype),
                                                       pltpu.HBM(land.shape, land.dtype),
                                                       jax.ShapeDtypeStruct((8, GROUP), F32)],
        in_specs=[hbm, hbm], out_specs=[sem] * 6 + [hbm, hbm, pl.BlockSpec(memory_space=pltpu.VMEM)],
        input_output_aliases={0: 6, 1: 7},
        compiler_params=pltpu.CompilerParams(has_side_effects=pltpu.SideEffectType.DATAFLOW_SIDE_EFFECTING),
    )(pltpu.with_memory_space_constraint(chip_partial, pltpu.HBM),
      pltpu.with_memory_space_constraint(lax.empty(land.shape, land.dtype), pltpu.HBM))
    return outs[:6], outs[6], outs[7], outs[8]


def _send_to_owners_wait(sems, src_thru, land_thru, after, name):
    def body(src_ref, land_ref, *rest):
        for cp in _owner_copies(src_ref, land_ref, rest[:6]):
            cp.wait_send()
            cp.wait_recv()

    hbm = pl.BlockSpec(memory_space=pltpu.HBM)
    sem = pl.BlockSpec(memory_space=pltpu.SEMAPHORE)
    return pl.pallas_call(
        body, name=name,
        out_shape=[pltpu.HBM(src_thru.shape, src_thru.dtype), pltpu.HBM(land_thru.shape, land_thru.dtype)],
        in_specs=[hbm, hbm] + [sem] * 6 + [pl.BlockSpec(memory_space=pl.ANY)], out_specs=[hbm, hbm],
        input_output_aliases={0: 0, 1: 1},
        compiler_params=pltpu.CompilerParams(has_side_effects=pltpu.SideEffectType.DATAFLOW_SIDE_EFFECTING),
    )(src_thru, land_thru, *sems, after)[1]


def _adamw_math(w, g, m, v):
    m = ADAM_B1 * m + (1.0 - ADAM_B1) * g
    v = ADAM_B2 * v + (1.0 - ADAM_B2) * (g * g)
    m_hat = m / (1.0 - ADAM_B1 ** ADAM_STEP)
    v_hat = v / (1.0 - ADAM_B2 ** ADAM_STEP)
    delta = -ADAM_LR * (m_hat / (jnp.sqrt(v_hat) + ADAM_EPS) + ADAM_WD * w)
    return delta, m, v


def _sum_adamw(w, m, v, own, received, name, after=None):
    rows, cols = w.shape
    rb = _row_block(rows, 256)

    def body(w_ref, m_ref, v_ref, own_ref, rec_ref, *rest):
        g_ref, d_ref, mo_ref, vo_ref = rest[-4:]
        g = own_ref[...]
        for j in range(3):
            g = g + rec_ref[j].astype(F32)
        g_ref[...] = g
        d_ref[...], mo_ref[...], vo_ref[...] = _adamw_math(w_ref[...], g, m_ref[...], v_ref[...])

    spec = pl.BlockSpec((rb, cols), lambda r: (r, 0))
    anchors = [] if after is None else [after]
    return pl.pallas_call(
        body, name=name, grid=(rows // rb,),
        in_specs=[spec] * 4 + [pl.BlockSpec((3, rb, cols), lambda r: (0, r, 0))]
                 + [pl.BlockSpec(memory_space=pl.ANY)] * len(anchors),
        out_specs=[spec] * 4, out_shape=[jax.ShapeDtypeStruct((rows, cols), F32)] * 4,
        compiler_params=pltpu.CompilerParams(dimension_semantics=("arbitrary",)),
    )(w, m, v, own, received, *anchors)


SMALL_NAMES = ("pool_w", "sgu_w", "pool_scale", "sgu_ln_g", "sgu_ln_b", "sgu_b", "ln1_g", "ln1_b", "ln2_g", "ln2_b")
_SMALL_VEC_ROWS = {"pool_scale": (ROW_POOL_SCALE, POOL_WIDTH), "sgu_ln_g": (ROW_SLN_G, SGU_WIDTH),
                   "sgu_ln_b": (ROW_SLN_B, SGU_WIDTH), "ln1_g": (ROW_LN1_G, D_MODEL), "ln1_b": (ROW_LN1_B, D_MODEL),
                   "ln2_g": (ROW_LN2_G, D_MODEL), "ln2_b": (ROW_LN2_B, D_MODEL)}
_SMALL_MAT_FIRST = {"pool_w": 0, "sgu_w": len(POOL_WINDOWS)}


def _small_sum_adamw(mats_all, vecs_all, w, m, v):
    n = len(SMALL_NAMES)

    def body(mats_ref, vecs_ref, *refs):
        w_refs, m_refs, v_refs = refs[:n], refs[n:2 * n], refs[2 * n:3 * n]
        loss_ref = refs[3 * n]
        g_refs, d_refs, mo_refs, vo_refs = (refs[3 * n + 1 + k * n:3 * n + 1 + (k + 1) * n] for k in range(4))
        vec_scr = refs[7 * n + 1]

        def update(k, idx, g):
            d, mo, vo = _adamw_math(w_refs[k][idx], g, m_refs[k][idx], v_refs[k][idx])
            g_refs[k][idx], d_refs[k][idx], mo_refs[k][idx], vo_refs[k][idx] = g, d, mo, vo

        total = vecs_ref[0]
        for dev in range(1, N_DEV):
            total = total + vecs_ref[dev]
        vec_scr[...] = total
        for k, name in enumerate(SMALL_NAMES):
            if name in _SMALL_MAT_FIRST:
                for b in range(4):
                    g = mats_ref[0, _SMALL_MAT_FIRST[name] + b]
                    for dev in range(1, N_DEV):
                        g = g + mats_ref[dev, _SMALL_MAT_FIRST[name] + b]
                    update(k, (0, b), g)
            elif name == "sgu_b":
                update(k, (0,), vec_scr[ROW_SGU_B:ROW_SGU_B + N_HEADS, 0:GROUP])
            else:
                row, width = _SMALL_VEC_ROWS[name]
                update(k, (slice(None), slice(None)), vec_scr[row:row + 1, 0:width])
        loss = jnp.sum(vec_scr[ROW_LOSS:ROW_LOSS + 1, :], axis=1, keepdims=True) * (0.5 / D_MODEL)
        loss_ref[...] = jnp.broadcast_to(loss, loss_ref.shape)

    vmem = pl.BlockSpec(memory_space=pltpu.VMEM)
    shapes = [jax.ShapeDtypeStruct(w[k].shape, F32) for k in SMALL_NAMES]
    outs = pl.pallas_call(
        body, name="small_sum_adamw",
        out_shape=[jax.ShapeDtypeStruct((8, GROUP), F32)] + shapes * 4,
        in_specs=[vmem] * (2 + 3 * n), out_specs=[vmem] * (1 + 4 * n),
        scratch_shapes=[pltpu.VMEM((VEC_ROWS, D_MODEL), F32)],
    )(mats_all, vecs_all, *[w[k] for k in SMALL_NAMES], *[m[k] for k in SMALL_NAMES], *[v[k] for k in SMALL_NAMES])
    return outs[0], outs[1:1 + n], outs[1 + n:1 + 2 * n], outs[1 + 2 * n:1 + 3 * n], outs[1 + 3 * n:]


def kernel(x, w_in, pool_w, pool_scale, sgu_ln_g, sgu_ln_b, sgu_w, sgu_b, w_out, ln1_g, ln1_b, w_gate_up, w_down, ln2_g, ln2_b, loss_target, m_w_in, m_pool_w, m_pool_scale, m_sgu_ln_g, m_sgu_ln_b, m_sgu_w, m_sgu_b, m_w_out, m_ln1_g, m_ln1_b, m_w_gate_up, m_w_down, m_ln2_g, m_ln2_b, v_w_in, v_pool_w, v_pool_scale, v_sgu_ln_g, v_sgu_ln_b, v_sgu_w, v_sgu_b, v_w_out, v_ln1_g, v_ln1_b, v_w_gate_up, v_w_down, v_ln2_g, v_ln2_b):
    bl, seq, _ = x.shape
    tokens = bl * seq
    x2d = x.reshape(tokens, D_MODEL)
    tgt2d = loss_target.reshape(tokens, D_MODEL)
    my_c = lax.axis_index("c")
    place = jnp.stack([my_c, 2 * lax.axis_index("x") + lax.axis_index("y")]).astype(jnp.int32)

    win_t, wout = _all_gather_rows([w_in[0].T.astype(BF16), w_out[0].astype(BF16)], "weight_all_gather")

    pool_w3, sgu_w3 = pool_w[0], sgu_w[0]
    sgu_b_t = sgu_b[0].T
    proj, xhat1, rstd1, mix_bf, x_bf, wgu_t, wdown = _mixer_fwd(
        x2d, win_t, wout, pool_w3, pool_scale, sgu_ln_g, sgu_ln_b, sgu_w3, sgu_b_t, ln1_g, ln1_b,
        [w_gate_up[0].T.astype(BF16), w_down[0].astype(BF16)], seq)
    dr1, dr1_bf, h_bf, dr2_bf, a_bf, dgu_bf, stats = _ffn_fwd_bwd(
        xhat1, rstd1, tgt2d, wgu_t, wdown, ln1_g, ln1_b, ln2_g, ln2_b)

    kept_gu, sib_gu = _wgrad_exchange(dgu_bf, h_bf, 1, "wgrad_gate_up")
    bf_gu, own_gu = _chip_sums([kept_gu], [sib_gu], place, "chip_sum_w_gate_up")
    sent_gu = _send_to_owners_start(bf_gu, "grad_scatter_w_gate_up_start")
    kept_dn, sib_dn = _wgrad_exchange(a_bf, dr2_bf, 2, "wgrad_down", after=sent_gu[3])
    bf_dn, own_dn = _chip_sums([kept_dn], [sib_dn], place, "chip_sum_w_down")
    sent_dn = _send_to_owners_start(bf_dn, "grad_scatter_w_down_start")
    kept_out, sib_out = _wgrad_exchange(mix_bf, dr1_bf, N_CHIPS, "wgrad_out", after=sent_dn[3])
    bf_out, own_out = _chip_sums([kept_out], [sib_out], place, "chip_sum_w_out")
    sent_out = _send_to_owners_start(bf_out, "grad_scatter_w_out_start")
    gx, dproj_bf, d_mats, d_vecs = _mixer_bwd(
        dr1, proj, win_t, wout, pool_w3, pool_scale, sgu_ln_g, sgu_ln_b, sgu_w3, sgu_b_t, stats, seq)
    kept_in, sib_in, mats_all, vecs_all = _wgrad_exchange(
        dproj_bf, x_bf, N_CHIPS, "wgrad_in", gather_rows=[d_mats.reshape(-1, GROUP), d_vecs], after=sent_out[3])
    bf_in, own_in = _chip_sums([kept_in], [sib_in], place, "chip_sum_w_in")
    sent_in = _send_to_owners_start(bf_in, "grad_scatter_w_in_start")

    grads, deltas, new_m, new_v = {}, {}, {}, {}
    after = sent_in[3]
    for nm, w, m, v, own, sent, transposed in (("w_gate_up", w_gate_up, m_w_gate_up, v_w_gate_up, own_gu, sent_gu, True),
                                               ("w_down", w_down, m_w_down, v_w_down, own_dn, sent_dn, False),
                                               ("w_out", w_out, m_w_out, v_w_out, own_out, sent_out, False),
                                               ("w_in", w_in, m_w_in, v_w_in, own_in, sent_in, True)):
        rows = (lambda a: a[0].T) if transposed else (lambda a: a[0])
        back = (lambda a: a.T[None]) if transposed else (lambda a: a[None])
        rec = _send_to_owners_wait(*sent[:3], after, "grad_scatter_" + nm + "_wait")
        g, d, mo, vo = _sum_adamw(rows(w), rows(m), rows(v), own, rec, "adamw_" + nm)
        after = vo
        grads[nm], deltas[nm], new_m[nm], new_v[nm] = back(g), back(d), back(mo), back(vo)

    small_w = {"pool_w": pool_w, "pool_scale": pool_scale, "sgu_ln_g": sgu_ln_g, "sgu_ln_b": sgu_ln_b, "sgu_w": sgu_w,
               "sgu_b": sgu_b, "ln1_g": ln1_g, "ln1_b": ln1_b, "ln2_g": ln2_g, "ln2_b": ln2_b}
    small_m = {"pool_w": m_pool_w, "pool_scale": m_pool_scale, "sgu_ln_g": m_sgu_ln_g, "sgu_ln_b": m_sgu_ln_b,
               "sgu_w": m_sgu_w, "sgu_b": m_sgu_b, "ln1_g": m_ln1_g, "ln1_b": m_ln1_b, "ln2_g": m_ln2_g, "ln2_b": m_ln2_b}
    small_v = {"pool_w": v_pool_w, "pool_scale": v_pool_scale, "sgu_ln_g": v_sgu_ln_g, "sgu_ln_b": v_sgu_ln_b,
               "sgu_w": v_sgu_w, "sgu_b": v_sgu_b, "ln1_g": v_ln1_g, "ln1_b": v_ln1_b, "ln2_g": v_ln2_g, "ln2_b": v_ln2_b}
    loss_blk, g_small, d_small, m_small, v_small = _small_sum_adamw(
        mats_all.reshape(N_DEV, 2 * N_HEADS, GROUP, GROUP), vecs_all.reshape(N_DEV, VEC_ROWS, D_MODEL),
        small_w, small_m, small_v)
    for vals, dst in ((g_small, grads), (d_small, deltas), (m_small, new_m), (v_small, new_v)):
        dst.update(zip(SMALL_NAMES, vals))

    order = ["w_in", "pool_w", "pool_scale", "sgu_ln_g", "sgu_ln_b", "sgu_w", "sgu_b", "w_out", "ln1_g", "ln1_b",
             "w_gate_up", "w_down", "ln2_g", "ln2_b"]
    return (loss_blk[0, 0], gx.reshape(bl, seq, D_MODEL), *[grads[k] for k in order], *[deltas[k] for k in order],
            *[new_m[k] for k in order], *[new_v[k] for k in order])
```

```python
import jax
import jax.numpy as jnp
from jax import lax
from jax.experimental import pallas as pl
from jax.experimental.pallas import tpu as pltpu

F32 = jnp.float32
BF16 = jnp.bfloat16
MESH = pl.DeviceIdType.MESH

D_MODEL = 1024
POOL_WIDTH = 512
SGU_WIDTH = 512
POOL_WINDOWS = (2, 4, 8, 16)
GROUP = 128
N_HEADS = 4
IN_COLS = POOL_WIDTH + 2 * SGU_WIDTH
D_FF = 2816
LN_EPS = 1e-5
ALPHA = float(2.0 ** 0.25)
HALO = 16
N_DEV = 8
N_CHIPS = 4

ADAM_LR = 0.001
ADAM_B1 = 0.9
ADAM_B2 = 0.999
ADAM_EPS = 1e-08
ADAM_WD = 0.01
ADAM_STEP = 10

VMEM_LIMIT_BYTES = 56 * 1024 * 1024

_SQRT_HALF = 0.7071067811865476
_INV_SQRT_2PI = 0.3989422804014327


def _dot_nn(a, b):
    return lax.dot_general(a, b, (((1,), (0,)), ((), ())), preferred_element_type=F32)


def _dot_nt(a, b):
    return lax.dot_general(a, b, (((1,), (1,)), ((), ())), preferred_element_type=F32)


def _dot_tn(a, b):
    return lax.dot_general(a, b, (((0,), (0,)), ((), ())), preferred_element_type=F32)


def _gelu(x):
    return 0.5 * x * (1.0 + lax.erf(x * _SQRT_HALF))


def _gelu_grad(x):
    return 0.5 * (1.0 + lax.erf(x * _SQRT_HALF)) + x * jnp.exp(-0.5 * x * x) * _INV_SQRT_2PI


def _ln_stats(r):
    mu = jnp.mean(r, axis=-1, keepdims=True)
    d = r - mu
    var = jnp.mean(d * d, axis=-1, keepdims=True)
    rstd = lax.rsqrt(var + LN_EPS)
    return d * rstd, rstd


def _ln_bwd(dout, xhat, rstd, g):
    dxh = dout * g
    m1 = jnp.mean(dxh, axis=-1, keepdims=True)
    m2 = jnp.mean(dxh * xhat, axis=-1, keepdims=True)
    return rstd * (dxh - m1 - xhat * m2)


def _rowsum(a):
    return jnp.sum(a, axis=0, keepdims=True)


def _pool_fwd(xp, xp_prev, inv_cnt, w):
    s = jnp.concatenate([xp_prev, xp], axis=0)
    k = 1
    while k < w:
        s = s + pltpu.roll(s, k, 0)
        k *= 2
    return s[HALO:, :] * inv_cnt - xp


def _pool_bwd(dpooled, dpooled_next, inv_cnt, inv_cnt_next, w):
    n = dpooled.shape[0] + HALO
    s = jnp.concatenate([dpooled * inv_cnt, dpooled_next * inv_cnt_next], axis=0)
    k = 1
    while k < w:
        s = s + pltpu.roll(s, n - k, 0)
        k *= 2
    return s[: dpooled.shape[0], :] - dpooled


def _inv_count(pos, w):
    return 1.0 / jnp.minimum(pos + 1, w).astype(F32)


def _to_head_major(a, h, nc):
    return jnp.concatenate(
        [a[c * GROUP:(c + 1) * GROUP, h * GROUP:(h + 1) * GROUP] for c in range(nc)], axis=1)


def _masked_sgu_w(sw_ref, h):
    row = lax.broadcasted_iota(jnp.int32, (GROUP, GROUP), 0)
    col = lax.broadcasted_iota(jnp.int32, (GROUP, GROUP), 1)
    return jnp.where(row >= col, sw_ref[h], 0.0)


def _row_block(rows, limit):
    return max(b for b in range(16, min(rows, limit) + 1, 16) if rows % b == 0)


def _mesh_position():
    return lax.axis_index("x"), lax.axis_index("y"), lax.axis_index("c")


def _other_chips(x, y):
    return [(1 - x, y), (x, 1 - y), (1 - x, 1 - y)]


class _TwoLevelGather:
    def __init__(self, ins, outs, send_sems, recv_sems, local_sems):
        self.ins, self.outs = ins, outs
        self.send_sems, self.recv_sems, self.local_sems = send_sems, recv_sems, local_sems
        self.na = len(ins)
        x, y, c = _mesh_position()
        self.c = c
        self.me, self.sibling = (x, y, c), (x, y, 1 - c)
        self.chips = _other_chips(x, y)
        self.relay_from = (x + (1 - c) * (1 - 2 * x), y + c * (1 - 2 * y))
        self.relay_to = (x + c * (1 - 2 * x), y + (1 - c) * (1 - 2 * y))

    def _rows(self, a, px, py, pc):
        n = self.ins[a].shape[0]
        return self.outs[a].at[pl.ds((4 * px + 2 * py + pc) * n, n), :]

    def _copy(self, a, k, block, to, src=None):
        return pltpu.make_async_remote_copy(
            src_ref=self._rows(a, *block) if src is None else src, dst_ref=self._rows(a, *block),
            send_sem=self.send_sems.at[a * 7 + k], recv_sem=self.recv_sems.at[a * 7 + k],
            device_id=to, device_id_type=MESH)

    def _mine(self, a):
        return pltpu.make_async_copy(self.ins[a], self._rows(a, *self.me), self.local_sems.at[a])

    def start(self):
        for a in range(self.na):
            self._mine(a).start()
        for a in range(self.na):
            self._copy(a, 0, self.me, self.sibling, src=self.ins[a]).start()
            for j, chip in enumerate(self.chips[:2]):
                self._copy(a, 1 + j, self.me, (*chip, self.c), src=self.ins[a]).start()

    def relay(self, a):
        c, block = self.c, (*self.relay_from, self.c)
        self._copy(a, 1 + c, block, self.me).wait_recv()
        self._copy(a, 3, block, (*self.relay_to, c)).start()
        self._copy(a, 4 + c, block, self.sibling).start()

    def pass_on(self, a):
        c = self.c
        self._copy(a, 2 - c, (*self.relay_to, c), self.me).wait_recv()
        self._copy(a, 5 - c, (*self.relay_to, c), self.sibling).start()
        self._copy(a, 3, (*self.chips[2], c), self.me).wait_recv()
        self._copy(a, 6, (*self.chips[2], c), self.sibling).start()

    def finish(self):
        for a in range(self.na):
            self._copy(a, 0, self.sibling, self.me).wait_recv()
            for j, chip in enumerate(self.chips):
                self._copy(a, 4 + j, (*chip, 1 - self.c), self.me).wait_recv()
        for a in range(self.na):
            for k in range(7):
                self._copy(a, k, self.me, self.sibling, src=self.ins[a]).wait_send()
            self._mine(a).wait()

    @staticmethod
    def scratch(na):
        return [pltpu.SemaphoreType.DMA((7 * na,)), pltpu.SemaphoreType.DMA((7 * na,)), pltpu.SemaphoreType.DMA((na,))]


def _gathered_shape(s):
    return jax.ShapeDtypeStruct((N_DEV * s.shape[0], s.shape[1]), s.dtype)


def _all_gather_rows(shards, name):
    na = len(shards)

    def body(*refs):
        gather = _TwoLevelGather(refs[:na], refs[na:2 * na], *refs[2 * na:])
        gather.start()
        for a in range(na):
            gather.relay(a)
        for a in range(na):
            gather.pass_on(a)
        gather.finish()

    any_spec = pl.BlockSpec(memory_space=pl.ANY)
    return pl.pallas_call(
        body, name=name, out_shape=[_gathered_shape(s) for s in shards],
        in_specs=[any_spec] * na, out_specs=[any_spec] * na, scratch_shapes=_TwoLevelGather.scratch(na),
    )(*shards)


def _mixer_fwd(x2d, win_t, wout, pool_w, pool_scale, sln_g, sln_b, sgu_w, sgu_b_t, ln1_g, ln1_b, later_shards, seq):
    tokens = x2d.shape[0]
    tt = min(512, seq)
    tiles_per_seq = seq // tt
    nc = tt // GROUP
    n_tiles = tokens // tt
    n_later = len(later_shards)

    def body(x_ref, xh_ref, win_ref, wout_ref, pw_ref, ps_ref, lg_ref, lb_ref, sw_ref, sb_ref, g1_ref, b1_ref, *rest):
        shard_refs, rest = rest[:n_later], rest[n_later:]
        proj_ref, xhat_ref, rstd_ref, mix_ref, xbf_ref = rest[:5]
        gathered_refs, rest = rest[5:5 + n_later], rest[5 + n_later:]
        mix_scr, send_sems, recv_sems, local_sems = rest
        i = pl.program_id(0)
        gather = _TwoLevelGather(shard_refs, gathered_refs, send_sems, recv_sems, local_sems)

        @pl.when(i == 0)
        def _():
            gather.start()

        tile_in_seq = i % tiles_per_seq
        x = x_ref[...]
        xb = x.astype(BF16)
        xbf_ref[...] = xb
        proj = _dot_nt(xb, win_ref[...])
        proj_ref[...] = proj
        xp_prev = _dot_nt(xh_ref[...].astype(BF16), win_ref[0:POOL_WIDTH, :])
        xp_prev = jnp.where(tile_in_seq == 0, 0.0, xp_prev)
        pos = tile_in_seq * tt + lax.broadcasted_iota(jnp.int32, (tt, 1), 0)
        for g, w in enumerate(POOL_WINDOWS):
            sl = slice(g * GROUP, (g + 1) * GROUP)
            pooled = _pool_fwd(proj[:, sl], xp_prev[:, sl], _inv_count(pos, w), w)
            pre = _dot_nn(pooled.astype(BF16), pw_ref[g].astype(BF16))
            mix_scr[:, sl] = pre * ps_ref[:, sl]
        u = _gelu(proj[:, POOL_WIDTH:POOL_WIDTH + SGU_WIDTH])
        v = _gelu(proj[:, POOL_WIDTH + SGU_WIDTH:])
        vhat, _ = _ln_stats(v)
        v_ln = vhat * lg_ref[...] + lb_ref[...]
        for h in range(N_HEADS):
            ws = _masked_sgu_w(sw_ref, h).astype(BF16)
            mixed = _dot_nn(ws, _to_head_major(v_ln, h, nc).astype(BF16)) + sb_ref[:, h:h + 1]
            for c in range(nc):
                rs = slice(c * GROUP, (c + 1) * GROUP)
                mix_scr[rs, POOL_WIDTH + h * GROUP:POOL_WIDTH + (h + 1) * GROUP] = (
                    u[rs, h * GROUP:(h + 1) * GROUP] * mixed[:, c * GROUP:(c + 1) * GROUP])
        mixb = mix_scr[...].astype(BF16)
        mix_ref[...] = mixb
        r1 = ALPHA * x + _dot_nn(mixb, wout_ref[...])
        xhat, rstd = _ln_stats(r1)
        xhat_ref[...] = xhat
        rstd_ref[...] = rstd

        for a in range(n_later):
            relay_tile = min(n_tiles // 2 + a, n_tiles - 1)

            @pl.when(i == relay_tile)
            def _(a=a):
                gather.relay(a)

            @pl.when(i == max(n_tiles - n_later + a, relay_tile))
            def _(a=a):
                gather.pass_on(a)

        @pl.when(i == n_tiles - 1)
        def _():
            gather.finish()

    def tile(cols):
        return pl.BlockSpec((tt, cols), lambda i: (i, 0))

    def whole(a):
        nd = a.ndim
        return pl.BlockSpec(a.shape, lambda i: (0,) * nd)

    any_spec = pl.BlockSpec(memory_space=pl.ANY)
    halo = pl.BlockSpec((HALO, D_MODEL), lambda i: (jnp.maximum(i * (tt // HALO) - 1, 0), 0))
    consts = [win_t, wout, pool_w, pool_scale, sln_g, sln_b, sgu_w, sgu_b_t, ln1_g, ln1_b]
    return pl.pallas_call(
        body, name="mixer_fwd", grid=(n_tiles,),
        in_specs=[tile(D_MODEL), halo] + [whole(a) for a in consts] + [any_spec] * n_later,
        out_specs=[tile(IN_COLS), tile(D_MODEL), tile(1), tile(D_MODEL), tile(D_MODEL)] + [any_spec] * n_later,
        out_shape=[jax.ShapeDtypeStruct((tokens, IN_COLS), F32), jax.ShapeDtypeStruct((tokens, D_MODEL), F32),
                   jax.ShapeDtypeStruct((tokens, 1), F32), jax.ShapeDtypeStruct((tokens, D_MODEL), BF16),
                   jax.ShapeDtypeStruct((tokens, D_MODEL), BF16)] + [_gathered_shape(s) for s in later_shards],
        scratch_shapes=[pltpu.VMEM((tt, D_MODEL), F32)] + _TwoLevelGather.scratch(n_later),
        compiler_params=pltpu.CompilerParams(dimension_semantics=("arbitrary",), vmem_limit_bytes=VMEM_LIMIT_BYTES),
    )(x2d, x2d, *consts, *later_shards)


def _ffn_fwd_bwd(xhat1, rstd1, target, wgu_t, wdown, ln1_g, ln1_b, ln2_g, ln2_b):
    tokens = xhat1.shape[0]
    tt = min(256, tokens)

    def body(xhat_ref, rstd_ref, tgt_ref, wgu_hbm, wd_hbm, g1_ref, b1_ref, g2_ref, b2_ref,
             dr1_ref, dr1bf_ref, hbf_ref, dr2bf_ref, a_ref, dgu_ref, stats_ref, wgu_ref, wd_ref, gu_scr, sems):
        i = pl.program_id(0)

        @pl.when(i == 0)
        def _():
            loads = [pltpu.make_async_copy(wgu_hbm, wgu_ref, sems.at[0]),
                     pltpu.make_async_copy(wd_hbm, wd_ref, sems.at[1])]
            for cp in loads:
                cp.start()
            stats_ref[...] = jnp.zeros_like(stats_ref)
            for cp in loads:
                cp.wait()

        xhat1_t = xhat_ref[...]
        h = xhat1_t * g1_ref[...] + b1_ref[...]
        hb = h.astype(BF16)
        hbf_ref[...] = hb
        gate = _dot_nt(hb, wgu_ref[0:D_FF, :])
        up = _dot_nt(hb, wgu_ref[D_FF:, :])
        gu_scr[:, 0:D_FF] = gate
        gu_scr[:, D_FF:] = up
        ab = (gate * jax.nn.sigmoid(gate) * up).astype(BF16)
        a_ref[...] = ab
        xhat2, rstd2 = _ln_stats(ALPHA * h + _dot_nn(ab, wd_ref[...]))
        err = xhat2 * g2_ref[...] + b2_ref[...] - tgt_ref[...]
        dy = err * (1.0 / D_MODEL)
        stats_ref[0:1, :] += _rowsum(dy * xhat2)
        stats_ref[1:2, :] += _rowsum(dy)
        stats_ref[4:5, :] += _rowsum(err * err)
        dr2 = _ln_bwd(dy, xhat2, rstd2, g2_ref[...])
        dr2b = dr2.astype(BF16)
        dr2bf_ref[...] = dr2b
        da = _dot_nt(dr2b, wd_ref[...])
        gate = gu_scr[:, 0:D_FF]
        up = gu_scr[:, D_FF:]
        sg = jax.nn.sigmoid(gate)
        dgate = (da * up * (sg * (1.0 + gate * (1.0 - sg)))).astype(BF16)
        dup = (da * (gate * sg)).astype(BF16)
        dgu_ref[:, 0:D_FF] = dgate
        dgu_ref[:, D_FF:] = dup
        dh = ALPHA * dr2 + _dot_nn(dgate, wgu_ref[0:D_FF, :]) + _dot_nn(dup, wgu_ref[D_FF:, :])
        stats_ref[2:3, :] += _rowsum(dh * xhat1_t)
        stats_ref[3:4, :] += _rowsum(dh)
        dr1 = _ln_bwd(dh, xhat1_t, rstd_ref[...], g1_ref[...])
        dr1_ref[...] = dr1
        dr1bf_ref[...] = dr1.astype(BF16)

    def tile(cols):
        return pl.BlockSpec((tt, cols), lambda i: (i, 0))

    def whole(a):
        nd = a.ndim
        return pl.BlockSpec(a.shape, lambda i: (0,) * nd)

    any_spec = pl.BlockSpec(memory_space=pl.ANY)
    vecs = [ln1_g, ln1_b, ln2_g, ln2_b]
    return pl.pallas_call(
        body, name="ffn_fwd_bwd", grid=(tokens // tt,),
        in_specs=[tile(D_MODEL), tile(1), tile(D_MODEL), any_spec, any_spec] + [whole(a) for a in vecs],
        out_specs=[tile(D_MODEL), tile(D_MODEL), tile(D_MODEL), tile(D_MODEL), tile(D_FF), tile(2 * D_FF),
                   pl.BlockSpec((8, D_MODEL), lambda i: (0, 0))],
        out_shape=[jax.ShapeDtypeStruct((tokens, D_MODEL), F32), jax.ShapeDtypeStruct((tokens, D_MODEL), BF16),
                   jax.ShapeDtypeStruct((tokens, D_MODEL), BF16),
                   jax.ShapeDtypeStruct((tokens, D_MODEL), BF16), jax.ShapeDtypeStruct((tokens, D_FF), BF16),
                   jax.ShapeDtypeStruct((tokens, 2 * D_FF), BF16), jax.ShapeDtypeStruct((8, D_MODEL), F32)],
        scratch_shapes=[pltpu.VMEM(wgu_t.shape, BF16), pltpu.VMEM(wdown.shape, BF16),
                        pltpu.VMEM((tt, 2 * D_FF), F32), pltpu.SemaphoreType.DMA((2,))],
        compiler_params=pltpu.CompilerParams(dimension_semantics=("arbitrary",), vmem_limit_bytes=VMEM_LIMIT_BYTES),
    )(xhat1, rstd1, target, wgu_t, wdown, *vecs)


def _wgrad_exchange(lhs, rhs, chips_per_block, name, gather_rows=(), after=None, collective_id=None):
    assert collective_id is None or not gather_rows
    tokens, n_all = lhs.shape
    m = rhs.shape[1]
    n = n_all // N_DEV
    tw = min(2048, tokens)
    nt = tokens // tw
    cpb = chips_per_block
    nj = N_CHIPS // cpb
    ng = len(gather_rows)
    anchors = [] if after is None else [after]

    def body(l_ref, r_ref, *rest):
        small_refs, rest = rest[:ng], rest[ng + len(anchors):]
        kept_ref, sib_ref = rest[:2]
        gathered_refs, rest = rest[2:2 + ng], rest[2 + ng:]
        acc, sendbuf, send_sems, recv_sems = rest[:4]
        j, t = pl.program_id(0), pl.program_id(1)
        first, last = (j == 0) & (t == 0), (j == nj - 1) & (t == nt - 1)
        x, y, c = _mesh_position()
        if collective_id is not None:
            @pl.when(first)
            def _():
                barrier = pltpu.get_barrier_semaphore()
                pl.semaphore_signal(barrier, inc=1, device_id=(x, y, 1 - c), device_id_type=MESH)
                pl.semaphore_wait(barrier, 1)
        if ng:
            gather = _TwoLevelGather(small_refs, gathered_refs, *rest[4:7])

            @pl.when(first)
            def _():
                gather.start()

        def copy(q):
            return pltpu.make_async_remote_copy(
                src_ref=sendbuf.at[q], dst_ref=sib_ref.at[q], send_sem=send_sems.at[q], recv_sem=recv_sems.at[q],
                device_id=(x, y, 1 - c), device_id_type=MESH)

        @pl.when(t == 0)
        def _():
            acc[...] = jnp.zeros_like(acc)

        acc[...] += _dot_tn(l_ref[...], r_ref[...])

        @pl.when(t == nt - 1)
        def _():
            for qq in range(cpb):
                q = j * cpb + qq
                kept_ref[qq] = acc[pl.ds(pl.multiple_of(qq * 2 * n + c * n, 8), n), :]
                sendbuf[q] = acc[pl.ds(pl.multiple_of(qq * 2 * n + (1 - c) * n, 8), n), :]
                copy(q).start()

        if ng:
            @pl.when((j == nj - 1) & (t == nt // 2))
            def _():
                for a in range(ng):
                    gather.relay(a)

            @pl.when(last)
            def _():
                for a in range(ng):
                    gather.pass_on(a)
                gather.finish()

        @pl.when(last)
        def _():
            for q in range(N_CHIPS):
                copy(q).wait_send()
                copy(q).wait_recv()

    shard4 = jax.ShapeDtypeStruct((N_CHIPS, n, m), F32)
    any_spec = pl.BlockSpec(memory_space=pl.ANY)
    return pl.pallas_call(
        body, name=name, grid=(nj, nt),
        in_specs=[pl.BlockSpec((tw, 2 * n * cpb), lambda j, t: (t, j)), pl.BlockSpec((tw, m), lambda j, t: (t, 0))]
                 + [any_spec] * (ng + len(anchors)),
        out_specs=[pl.BlockSpec((cpb, n, m), lambda j, t: (j, 0, 0)), any_spec] + [any_spec] * ng,
        out_shape=[shard4, shard4] + [_gathered_shape(s) for s in gather_rows],
        scratch_shapes=[pltpu.VMEM((2 * n * cpb, m), F32), pltpu.VMEM((N_CHIPS, n, m), F32),
                        pltpu.SemaphoreType.DMA((N_CHIPS,)), pltpu.SemaphoreType.DMA((N_CHIPS,))]
                       + (_TwoLevelGather.scratch(ng) if ng else []),
        compiler_params=pltpu.CompilerParams(dimension_semantics=("arbitrary", "arbitrary"),
                                             vmem_limit_bytes=VMEM_LIMIT_BYTES, collective_id=collective_id),
    )(lhs, rhs, *gather_rows, *anchors)


def _received_shape(p):
    return jax.ShapeDtypeStruct((3,) + p.shape[1:], p.dtype)


ROW_POOL_SCALE, ROW_SLN_G, ROW_SLN_B, ROW_SGU_B = 0, 1, 2, 3
ROW_LN2_G, ROW_LN2_B, ROW_LN1_G, ROW_LN1_B, ROW_LOSS = 8, 9, 10, 11, 12
VEC_ROWS = 16


def _mixer_bwd(dr1, proj, win_t, wout, pool_w, pool_scale, sln_g, sln_b, sgu_w, sgu_b_t, stats, seq):
    tokens = dr1.shape[0]
    tt = min(512, seq)
    tiles_per_seq = seq // tt
    nc = tt // GROUP
    n_halo_blocks = tokens // HALO
    n_tiles = tokens // tt

    def body(dr1_ref, dr1n_ref, proj_ref, projh_ref, win_ref, wout_ref, pw_ref, ps_ref, lg_ref, lb_ref, sw_ref, sb_ref,
             stats_ref, gx_ref, dproj_ref, dmat_ref, dvec_ref, du_scr, dv_scr):
        i = pl.program_id(0)
        tile_in_seq = i % tiles_per_seq

        @pl.when(i == 0)
        def _():
            dmat_ref[...] = jnp.zeros_like(dmat_ref)
            dvec_ref[0:8, :] = jnp.zeros((8, D_MODEL), F32)
            dvec_ref[8:16, :] = stats_ref[...]

        dr1_t = dr1_ref[...]
        dr1b = dr1_t.astype(BF16)
        dmix = _dot_nt(dr1b, wout_ref[...])
        dpo_next = _dot_nt(dr1n_ref[...].astype(BF16), wout_ref[0:POOL_WIDTH, :])
        dpo_next = jnp.where(tile_in_seq == tiles_per_seq - 1, 0.0, dpo_next)
        proj = proj_ref[...]
        xp_prev = jnp.where(tile_in_seq == 0, 0.0, projh_ref[...])
        pos = tile_in_seq * tt + lax.broadcasted_iota(jnp.int32, (tt, 1), 0)
        pos_next = (tile_in_seq + 1) * tt + lax.broadcasted_iota(jnp.int32, (HALO, 1), 0)

        for g, w in enumerate(POOL_WINDOWS):
            sl = slice(g * GROUP, (g + 1) * GROUP)
            inv_cnt = _inv_count(pos, w)
            pwb = pw_ref[g].astype(BF16)
            pooledb = _pool_fwd(proj[:, sl], xp_prev[:, sl], inv_cnt, w).astype(BF16)
            pre = _dot_nn(pooledb, pwb)
            dpo = dmix[:, sl]
            dvec_ref[ROW_POOL_SCALE:ROW_POOL_SCALE + 1, sl] += _rowsum(dpo * pre)
            dsb = (dpo * ps_ref[:, sl]).astype(BF16)
            dmat_ref[g] += _dot_tn(pooledb, dsb)
            dpooled = _dot_nt(dsb, pwb)
            dpooled_next = _dot_nt((dpo_next[:, sl] * ps_ref[:, sl]).astype(BF16), pwb)
            dxp = _pool_bwd(dpooled, dpooled_next, inv_cnt, _inv_count(pos_next, w), w)
            dproj_ref[:, sl] = dxp.astype(BF16)

        zu = proj[:, POOL_WIDTH:POOL_WIDTH + SGU_WIDTH]
        zv = proj[:, POOL_WIDTH + SGU_WIDTH:]
        u = _gelu(zu)
        vhat, rstd_v = _ln_stats(_gelu(zv))
        v_ln = vhat * lg_ref[...] + lb_ref[...]
        dsg = dmix[:, POOL_WIDTH:]
        row = lax.broadcasted_iota(jnp.int32, (GROUP, GROUP), 0)
        col = lax.broadcasted_iota(jnp.int32, (GROUP, GROUP), 1)
        for h in range(N_HEADS):
            ws = _masked_sgu_w(sw_ref, h).astype(BF16)
            vh = _to_head_major(v_ln, h, nc).astype(BF16)
            mixed = _dot_nn(ws, vh) + sb_ref[:, h:h + 1]
            dsg_h = _to_head_major(dsg, h, nc)
            du_h = dsg_h * mixed
            dm_h = dsg_h * _to_head_major(u, h, nc)
            pos_sums = lax.dot_general(jnp.ones((8, nc * GROUP), F32), dm_h, (((1,), (1,)), ((), ())),
                                       precision=lax.Precision.HIGH, preferred_element_type=F32)
            dvec_ref[ROW_SGU_B + h:ROW_SGU_B + h + 1, 0:GROUP] += pos_sums[0:1, :]
            dmb = dm_h.astype(BF16)
            dmat_ref[len(POOL_WINDOWS) + h] += jnp.where(row >= col, _dot_nt(dmb, vh), 0.0)
            dv_h = _dot_tn(ws, dmb)
            for c in range(nc):
                rs = slice(c * GROUP, (c + 1) * GROUP)
                cs = slice(h * GROUP, (h + 1) * GROUP)
                du_scr[rs, cs] = du_h[:, c * GROUP:(c + 1) * GROUP]
                dv_scr[rs, cs] = dv_h[:, c * GROUP:(c + 1) * GROUP]
        dv_ln = dv_scr[...]
        dvec_ref[ROW_SLN_B:ROW_SLN_B + 1, 0:SGU_WIDTH] += _rowsum(dv_ln)
        dvec_ref[ROW_SLN_G:ROW_SLN_G + 1, 0:SGU_WIDTH] += _rowsum(dv_ln * vhat)
        dv = _ln_bwd(dv_ln, vhat, rstd_v, lg_ref[...])
        dproj_ref[:, POOL_WIDTH:POOL_WIDTH + SGU_WIDTH] = (du_scr[...] * _gelu_grad(zu)).astype(BF16)
        dproj_ref[:, POOL_WIDTH + SGU_WIDTH:] = (dv * _gelu_grad(zv)).astype(BF16)
        gx_ref[...] = ALPHA * dr1_t + _dot_nn(dproj_ref[...], win_ref[...])

    def tile(cols):
        return pl.BlockSpec((tt, cols), lambda i: (i, 0))

    def whole(a):
        nd = a.ndim
        return pl.BlockSpec(a.shape, lambda i: (0,) * nd)

    def resident(shape):
        nd = len(shape)
        return pl.BlockSpec(shape, lambda i: (0,) * nd)

    next_halo = pl.BlockSpec((HALO, D_MODEL), lambda i: (jnp.minimum((i + 1) * (tt // HALO), n_halo_blocks - 1), 0))
    prev_halo = pl.BlockSpec((HALO, POOL_WIDTH), lambda i: (jnp.maximum(i * (tt // HALO) - 1, 0), 0))
    consts = [win_t, wout, pool_w, pool_scale, sln_g, sln_b, sgu_w, sgu_b_t, stats]
    small_shapes = [(len(POOL_WINDOWS) + N_HEADS, GROUP, GROUP), (VEC_ROWS, D_MODEL)]
    return pl.pallas_call(
        body, name="mixer_bwd", grid=(n_tiles,),
        in_specs=[tile(D_MODEL), next_halo, tile(IN_COLS), prev_halo] + [whole(a) for a in consts],
        out_specs=[tile(D_MODEL), tile(IN_COLS)] + [resident(s) for s in small_shapes],
        out_shape=[jax.ShapeDtypeStruct((tokens, D_MODEL), F32), jax.ShapeDtypeStruct((tokens, IN_COLS), BF16)]
                  + [jax.ShapeDtypeStruct(s, F32) for s in small_shapes],
        scratch_shapes=[pltpu.VMEM((tt, SGU_WIDTH), F32), pltpu.VMEM((tt, SGU_WIDTH), F32)],
        compiler_params=pltpu.CompilerParams(dimension_semantics=("arbitrary",), vmem_limit_bytes=VMEM_LIMIT_BYTES),
    )(dr1, dr1, proj, proj, *consts)


def _chip_sums(kept, from_sibling, place, name):
    na = len(kept)

    def body(place_ref, *refs):
        kept_refs, sib_refs = refs[:na], refs[na:2 * na]
        bf_refs, own_refs = refs[2 * na:3 * na], refs[3 * na:]
        q = pl.program_id(0)
        for a in range(na):
            s = kept_refs[a][...] + sib_refs[a][...]
            bf_refs[a][...] = s.astype(BF16)

            @pl.when(q == place_ref[1])
            def _(a=a, s=s):
                own_refs[a][...] = s

    by_chip = [pl.BlockSpec((None,) + k.shape[1:], lambda q, pr: (q, 0, 0)) for k in kept]
    grid_spec = pltpu.PrefetchScalarGridSpec(
        num_scalar_prefetch=1, grid=(N_CHIPS,), in_specs=by_chip + by_chip,
        out_specs=by_chip + [pl.BlockSpec(k.shape[1:], lambda q, pr: (0, 0)) for k in kept])
    return pl.pallas_call(
        body, name=name, grid_spec=grid_spec,
        out_shape=[jax.ShapeDtypeStruct(k.shape, BF16) for k in kept]
                  + [jax.ShapeDtypeStruct(k.shape[1:], F32) for k in kept],
        compiler_params=pltpu.CompilerParams(dimension_semantics=("arbitrary",), vmem_limit_bytes=VMEM_LIMIT_BYTES),
    )(place, *kept, *from_sibling)


def _owner_copies(src_ref, land_ref, sems):
    x, y, c = _mesh_position()
    return [pltpu.make_async_remote_copy(
        src_ref=src_ref.at[2 * cx + cy], dst_ref=land_ref.at[j], send_sem=sems[j], recv_sem=sems[3 + j],
        device_id=(cx, cy, c), device_id_type=MESH) for j, (cx, cy) in enumerate(_other_chips(x, y))]


def _send_to_owners_start(chip_partial, name, collective_id):
    land = _received_shape(chip_partial)

    def body(src_ref, land_ref, *rest):
        x, y, c = _mesh_position()
        barrier = pltpu.get_barrier_semaphore()
        for cx, cy in _other_chips(x, y):
            pl.semaphore_signal(barrier, inc=1, device_id=(cx, cy, c), device_id_type=MESH)
        pl.semaphore_wait(barrier, 3)
        for cp in _owner_copies(src_ref, land_ref, rest[:6]):
            cp.start()
        rest[8][...] = jnp.zeros_like(rest[8])

    hbm = pl.BlockSpec(memory_space=pltpu.HBM)
    sem = pl.BlockSpec(memory_space=pltpu.SEMAPHORE)
    outs = pl.pallas_call(
        body, name=name,
        out_shape=[pltpu.SemaphoreType.DMA(())] * 6 + [pltpu.HBM(chip_partial.shape, chip_partial.dtype),
                                                       pltpu.HBM(land.shape, land.dtype),
                                                       jax.ShapeDtypeStruct((8, GROUP), F32)],
        in_specs=[hbm, hbm], out_specs=[sem] * 6 + [hbm, hbm, pl.BlockSpec(memory_space=pltpu.VMEM)],
        input_output_aliases={0: 6, 1: 7},
        compiler_params=pltpu.CompilerParams(has_side_effects=pltpu.SideEffectType.DATAFLOW_SIDE_EFFECTING,
                                             collective_id=collective_id),
    )(pltpu.with_memory_space_constraint(chip_partial, pltpu.HBM),
      pltpu.with_memory_space_constraint(lax.empty(land.shape, land.dtype), pltpu.HBM))
    return outs[:6], outs[6], outs[7], outs[8]


def _send_to_owners_wait(sems, src_thru, land_thru, after, name):
    def body(src_ref, land_ref, *rest):
        for cp in _owner_copies(src_ref, land_ref, rest[:6]):
            cp.wait_send()
            cp.wait_recv()

    hbm = pl.BlockSpec(memory_space=pltpu.HBM)
    sem = pl.BlockSpec(memory_space=pltpu.SEMAPHORE)
    return pl.pallas_call(
        body, name=name,
        out_shape=[pltpu.HBM(src_thru.shape, src_thru.dtype), pltpu.HBM(land_thru.shape, land_thru.dtype)],
        in_specs=[hbm, hbm] + [sem] * 6 + [pl.BlockSpec(memory_space=pl.ANY)], out_specs=[hbm, hbm],
        input_output_aliases={0: 0, 1: 1},
        compiler_params=pltpu.CompilerParams(has_side_effects=pltpu.SideEffectType.DATAFLOW_SIDE_EFFECTING),
    )(src_thru, land_thru, *sems, after)[1]


def _adamw_math(w, g, m, v):
    m = ADAM_B1 * m + (1.0 - ADAM_B1) * g
    v = ADAM_B2 * v + (1.0 - ADAM_B2) * (g * g)
    m_hat = m / (1.0 - ADAM_B1 ** ADAM_STEP)
    v_hat = v / (1.0 - ADAM_B2 ** ADAM_STEP)
    delta = -ADAM_LR * (m_hat / (jnp.sqrt(v_hat) + ADAM_EPS) + ADAM_WD * w)
    return delta, m, v


def _sum_adamw(w, m, v, own, received, name):
    rows, cols = w.shape
    rb = _row_block(rows, 256)

    def body(w_ref, m_ref, v_ref, own_ref, rec_ref, g_ref, d_ref, mo_ref, vo_ref):
        g = own_ref[...]
        for j in range(3):
            g = g + rec_ref[j].astype(F32)
        g_ref[...] = g
        d_ref[...], mo_ref[...], vo_ref[...] = _adamw_math(w_ref[...], g, m_ref[...], v_ref[...])

    spec = pl.BlockSpec((rb, cols), lambda r: (r, 0))
    return pl.pallas_call(
        body, name=name, grid=(rows // rb,),
        in_specs=[spec] * 4 + [pl.BlockSpec((3, rb, cols), lambda r: (0, r, 0))],
        out_specs=[spec] * 4, out_shape=[jax.ShapeDtypeStruct((rows, cols), F32)] * 4,
        compiler_params=pltpu.CompilerParams(dimension_semantics=("arbitrary",)),
    )(w, m, v, own, received)


SMALL_NAMES = ("pool_w", "sgu_w", "pool_scale", "sgu_ln_g", "sgu_ln_b", "sgu_b", "ln1_g", "ln1_b", "ln2_g", "ln2_b")
_SMALL_VEC_ROWS = {"pool_scale": (ROW_POOL_SCALE, POOL_WIDTH), "sgu_ln_g": (ROW_SLN_G, SGU_WIDTH),
                   "sgu_ln_b": (ROW_SLN_B, SGU_WIDTH), "ln1_g": (ROW_LN1_G, D_MODEL), "ln1_b": (ROW_LN1_B, D_MODEL),
                   "ln2_g": (ROW_LN2_G, D_MODEL), "ln2_b": (ROW_LN2_B, D_MODEL)}
_SMALL_MAT_FIRST = {"pool_w": 0, "sgu_w": len(POOL_WINDOWS)}


def _small_sum_adamw(mats_all, vecs_all, w, m, v):
    n = len(SMALL_NAMES)

    def body(mats_ref, vecs_ref, *refs):
        w_refs, m_refs, v_refs = refs[:n], refs[n:2 * n], refs[2 * n:3 * n]
        loss_ref = refs[3 * n]
        g_refs, d_refs, mo_refs, vo_refs = (refs[3 * n + 1 + k * n:3 * n + 1 + (k + 1) * n] for k in range(4))
        vec_scr = refs[7 * n + 1]

        def update(k, idx, g):
            d, mo, vo = _adamw_math(w_refs[k][idx], g, m_refs[k][idx], v_refs[k][idx])
            g_refs[k][idx], d_refs[k][idx], mo_refs[k][idx], vo_refs[k][idx] = g, d, mo, vo

        total = vecs_ref[0]
        for dev in range(1, N_DEV):
            total = total + vecs_ref[dev]
        vec_scr[...] = total
        for k, name in enumerate(SMALL_NAMES):
            if name in _SMALL_MAT_FIRST:
                for b in range(4):
                    g = mats_ref[0, _SMALL_MAT_FIRST[name] + b]
                    for dev in range(1, N_DEV):
                        g = g + mats_ref[dev, _SMALL_MAT_FIRST[name] + b]
                    update(k, (0, b), g)
            elif name == "sgu_b":
                update(k, (0,), vec_scr[ROW_SGU_B:ROW_SGU_B + N_HEADS, 0:GROUP])
            else:
                row, width = _SMALL_VEC_ROWS[name]
                update(k, (slice(None), slice(None)), vec_scr[row:row + 1, 0:width])
        loss = jnp.sum(vec_scr[ROW_LOSS:ROW_LOSS + 1, :], axis=1, keepdims=True) * (0.5 / D_MODEL)
        loss_ref[...] = jnp.broadcast_to(loss, loss_ref.shape)

    vmem = pl.BlockSpec(memory_space=pltpu.VMEM)
    shapes = [jax.ShapeDtypeStruct(w[k].shape, F32) for k in SMALL_NAMES]
    outs = pl.pallas_call(
        body, name="small_sum_adamw",
        out_shape=[jax.ShapeDtypeStruct((8, GROUP), F32)] + shapes * 4,
        in_specs=[vmem] * (2 + 3 * n), out_specs=[vmem] * (1 + 4 * n),
        scratch_shapes=[pltpu.VMEM((VEC_ROWS, D_MODEL), F32)],
    )(mats_all, vecs_all, *[w[k] for k in SMALL_NAMES], *[m[k] for k in SMALL_NAMES], *[v[k] for k in SMALL_NAMES])
    return outs[0], outs[1:1 + n], outs[1 + n:1 + 2 * n], outs[1 + 2 * n:1 + 3 * n], outs[1 + 3 * n:]


def kernel(x, w_in, pool_w, pool_scale, sgu_ln_g, sgu_ln_b, sgu_w, sgu_b, w_out, ln1_g, ln1_b, w_gate_up, w_down, ln2_g, ln2_b, loss_target, m_w_in, m_pool_w, m_pool_scale, m_sgu_ln_g, m_sgu_ln_b, m_sgu_w, m_sgu_b, m_w_out, m_ln1_g, m_ln1_b, m_w_gate_up, m_w_down, m_ln2_g, m_ln2_b, v_w_in, v_pool_w, v_pool_scale, v_sgu_ln_g, v_sgu_ln_b, v_sgu_w, v_sgu_b, v_w_out, v_ln1_g, v_ln1_b, v_w_gate_up, v_w_down, v_ln2_g, v_ln2_b):
    bl, seq, _ = x.shape
    tokens = bl * seq
    x2d = x.reshape(tokens, D_MODEL)
    tgt2d = loss_target.reshape(tokens, D_MODEL)
    my_c = lax.axis_index("c")
    place = jnp.stack([my_c, 2 * lax.axis_index("x") + lax.axis_index("y")]).astype(jnp.int32)

    win_t, wout = _all_gather_rows([w_in[0].T.astype(BF16), w_out[0].astype(BF16)], "weight_all_gather")

    pool_w3, sgu_w3 = pool_w[0], sgu_w[0]
    sgu_b_t = sgu_b[0].T
    proj, xhat1, rstd1, mix_bf, x_bf, wgu_t, wdown = _mixer_fwd(
        x2d, win_t, wout, pool_w3, pool_scale, sgu_ln_g, sgu_ln_b, sgu_w3, sgu_b_t, ln1_g, ln1_b,
        [w_gate_up[0].T.astype(BF16), w_down[0].astype(BF16)], seq)
    dr1, dr1_bf, h_bf, dr2_bf, a_bf, dgu_bf, stats = _ffn_fwd_bwd(
        xhat1, rstd1, tgt2d, wgu_t, wdown, ln1_g, ln1_b, ln2_g, ln2_b)

    kept_gu, sib_gu = _wgrad_exchange(dgu_bf, h_bf, 1, "wgrad_gate_up", collective_id=1)
    bf_gu, own_gu = _chip_sums([kept_gu], [sib_gu], place, "chip_sum_w_gate_up")
    sent_gu = _send_to_owners_start(bf_gu, "grad_scatter_w_gate_up_start", collective_id=4)
    kept_dn, sib_dn = _wgrad_exchange(a_bf, dr2_bf, 2, "wgrad_down", after=sent_gu[3], collective_id=2)
    bf_dn, own_dn = _chip_sums([kept_dn], [sib_dn], place, "chip_sum_w_down")
    sent_dn = _send_to_owners_start(bf_dn, "grad_scatter_w_down_start", collective_id=5)
    kept_out, sib_out = _wgrad_exchange(mix_bf, dr1_bf, N_CHIPS, "wgrad_out", after=sent_dn[3], collective_id=3)
    bf_out, own_out = _chip_sums([kept_out], [sib_out], place, "chip_sum_w_out")
    sent_out = _send_to_owners_start(bf_out, "grad_scatter_w_out_start", collective_id=6)
    gx, dproj_bf, d_mats, d_vecs = _mixer_bwd(
        dr1, proj, win_t, wout, pool_w3, pool_scale, sgu_ln_g, sgu_ln_b, sgu_w3, sgu_b_t, stats, seq)
    kept_in, sib_in, mats_all, vecs_all = _wgrad_exchange(
        dproj_bf, x_bf, N_CHIPS, "wgrad_in", gather_rows=[d_mats.reshape(-1, GROUP), d_vecs], after=sent_out[3])
    bf_in, own_in = _chip_sums([kept_in], [sib_in], place, "chip_sum_w_in")
    sent_in = _send_to_owners_start(bf_in, "grad_scatter_w_in_start", collective_id=7)

    grads, deltas, new_m, new_v = {}, {}, {}, {}
    after = sent_in[3]
    for nm, w, m, v, own, sent, transposed in (("w_gate_up", w_gate_up, m_w_gate_up, v_w_gate_up, own_gu, sent_gu, True),
                                               ("w_down", w_down, m_w_down, v_w_down, own_dn, sent_dn, False),
                                               ("w_out", w_out, m_w_out, v_w_out, own_out, sent_out, False),
                                               ("w_in", w_in, m_w_in, v_w_in, own_in, sent_in, True)):
        rows = (lambda a: a[0].T) if transposed else (lambda a: a[0])
        back = (lambda a: a.T[None]) if transposed else (lambda a: a[None])
        rec = _send_to_owners_wait(*sent[:3], after, "grad_scatter_" + nm + "_wait")
        g, d, mo, vo = _sum_adamw(rows(w), rows(m), rows(v), own, rec, "adamw_" + nm)
        after = vo
        grads[nm], deltas[nm], new_m[nm], new_v[nm] = back(g), back(d), back(mo), back(vo)

    small_w = {"pool_w": pool_w, "pool_scale": pool_scale, "sgu_ln_g": sgu_ln_g, "sgu_ln_b": sgu_ln_b, "sgu_w": sgu_w,
               "sgu_b": sgu_b, "ln1_g": ln1_g, "ln1_b": ln1_b, "ln2_g": ln2_g, "ln2_b": ln2_b}
    small_m = {"pool_w": m_pool_w, "pool_scale": m_pool_scale, "sgu_ln_g": m_sgu_ln_g, "sgu_ln_b": m_sgu_ln_b,
               "sgu_w": m_sgu_w, "sgu_b": m_sgu_b, "ln1_g": m_ln1_g, "ln1_b": m_ln1_b, "ln2_g": m_ln2_g, "ln2_b": m_ln2_b}
    small_v = {"pool_w": v_pool_w, "pool_scale": v_pool_scale, "sgu_ln_g": v_sgu_ln_g, "sgu_ln_b": v_sgu_ln_b,
               "sgu_w": v_sgu_w, "sgu_b": v_sgu_b, "ln1_g": v_ln1_g, "ln1_b": v_ln1_b, "ln2_g": v_ln2_g, "ln2_b": v_ln2_b}
    loss_blk, g_small, d_small, m_small, v_small = _small_sum_adamw(
        mats_all.reshape(N_DEV, 2 * N_HEADS, GROUP, GROUP), vecs_all.reshape(N_DEV, VEC_ROWS, D_MODEL),
        small_w, small_m, small_v)
    for vals, dst in ((g_small, grads), (d_small, deltas), (m_small, new_m), (v_small, new_v)):
        dst.update(zip(SMALL_NAMES, vals))

    order = ["w_in", "pool_w", "pool_scale", "sgu_ln_g", "sgu_ln_b", "sgu_w", "sgu_b", "w_out", "ln1_g", "ln1_b",
             "w_gate_up", "w_down", "ln2_g", "ln2_b"]
    return (loss_blk[0, 0], gx.reshape(bl, seq, D_MODEL), *[grads[k] for k in order], *[deltas[k] for k in order],
            *[new_m[k] for k in order], *[new_v[k] for k in order])
```

```python
import jax
import jax.numpy as jnp
from jax import lax
from jax.experimental import pallas as pl
from jax.experimental.pallas import tpu as pltpu

F32 = jnp.float32
BF16 = jnp.bfloat16
MESH = pl.DeviceIdType.MESH

D_MODEL = 1024
POOL_WIDTH = 512
SGU_WIDTH = 512
POOL_WINDOWS = (2, 4, 8, 16)
GROUP = 128
N_HEADS = 4
IN_COLS = POOL_WIDTH + 2 * SGU_WIDTH
D_FF = 2816
LN_EPS = 1e-5
ALPHA = float(2.0 ** 0.25)
HALO = 16
N_DEV = 8
N_CHIPS = 4

ADAM_LR = 0.001
ADAM_B1 = 0.9
ADAM_B2 = 0.999
ADAM_EPS = 1e-08
ADAM_WD = 0.01
ADAM_STEP = 10

VMEM_LIMIT_BYTES = 56 * 1024 * 1024

_SQRT_HALF = 0.7071067811865476
_INV_SQRT_2PI = 0.3989422804014327


def _dot_nn(a, b):
    return lax.dot_general(a, b, (((1,), (0,)), ((), ())), preferred_element_type=F32)


def _dot_nt(a, b):
    return lax.dot_general(a, b, (((1,), (1,)), ((), ())), preferred_element_type=F32)


def _dot_tn(a, b):
    return lax.dot_general(a, b, (((0,), (0,)), ((), ())), preferred_element_type=F32)


def _gelu(x):
    return 0.5 * x * (1.0 + lax.erf(x * _SQRT_HALF))


def _gelu_grad(x):
    return 0.5 * (1.0 + lax.erf(x * _SQRT_HALF)) + x * jnp.exp(-0.5 * x * x) * _INV_SQRT_2PI


def _ln_stats(r):
    mu = jnp.mean(r, axis=-1, keepdims=True)
    d = r - mu
    var = jnp.mean(d * d, axis=-1, keepdims=True)
    rstd = lax.rsqrt(var + LN_EPS)
    return d * rstd, rstd


def _ln_bwd(dout, xhat, rstd, g):
    dxh = dout * g
    m1 = jnp.mean(dxh, axis=-1, keepdims=True)
    m2 = jnp.mean(dxh * xhat, axis=-1, keepdims=True)
    return rstd * (dxh - m1 - xhat * m2)


def _rowsum(a):
    return jnp.sum(a, axis=0, keepdims=True)


def _pool_fwd(xp, xp_prev, inv_cnt, w):
    s = jnp.concatenate([xp_prev, xp], axis=0)
    k = 1
    while k < w:
        s = s + pltpu.roll(s, k, 0)
        k *= 2
    return s[HALO:, :] * inv_cnt - xp


def _pool_bwd(dpooled, dpooled_next, inv_cnt, inv_cnt_next, w):
    n = dpooled.shape[0] + HALO
    s = jnp.concatenate([dpooled * inv_cnt, dpooled_next * inv_cnt_next], axis=0)
    k = 1
    while k < w:
        s = s + pltpu.roll(s, n - k, 0)
        k *= 2
    return s[: dpooled.shape[0], :] - dpooled


def _inv_count(pos, w):
    return 1.0 / jnp.minimum(pos + 1, w).astype(F32)


def _to_head_major(a, h, nc):
    return jnp.concatenate(
        [a[c * GROUP:(c + 1) * GROUP, h * GROUP:(h + 1) * GROUP] for c in range(nc)], axis=1)


def _masked_sgu_w(sw_ref, h):
    row = lax.broadcasted_iota(jnp.int32, (GROUP, GROUP), 0)
    col = lax.broadcasted_iota(jnp.int32, (GROUP, GROUP), 1)
    return jnp.where(row >= col, sw_ref[h], 0.0)


def _row_block(rows, limit):
    return max(b for b in range(16, min(rows, limit) + 1, 16) if rows % b == 0)


def _mesh_position():
    return lax.axis_index("x"), lax.axis_index("y"), lax.axis_index("c")


def _other_chips(x, y):
    return [(1 - x, y), (x, 1 - y), (1 - x, 1 - y)]


class _TwoLevelGather:
    def __init__(self, ins, outs, send_sems, recv_sems, local_sems):
        self.ins, self.outs = ins, outs
        self.send_sems, self.recv_sems, self.local_sems = send_sems, recv_sems, local_sems
        self.na = len(ins)
        x, y, c = _mesh_position()
        self.c = c
        self.me, self.sibling = (x, y, c), (x, y, 1 - c)
        self.chips = _other_chips(x, y)
        self.relay_from = (x + (1 - c) * (1 - 2 * x), y + c * (1 - 2 * y))
        self.relay_to = (x + c * (1 - 2 * x), y + (1 - c) * (1 - 2 * y))

    def _rows(self, a, px, py, pc):
        n = self.ins[a].shape[0]
        return self.outs[a].at[pl.ds((4 * px + 2 * py + pc) * n, n), :]

    def _copy(self, a, k, block, to, src=None):
        return pltpu.make_async_remote_copy(
            src_ref=self._rows(a, *block) if src is None else src, dst_ref=self._rows(a, *block),
            send_sem=self.send_sems.at[a * 7 + k], recv_sem=self.recv_sems.at[a * 7 + k],
            device_id=to, device_id_type=MESH)

    def _mine(self, a):
        return pltpu.make_async_copy(self.ins[a], self._rows(a, *self.me), self.local_sems.at[a])

    def start(self):
        for a in range(self.na):
            self._mine(a).start()
        for a in range(self.na):
            self._copy(a, 0, self.me, self.sibling, src=self.ins[a]).start()
            for j, chip in enumerate(self.chips[:2]):
                self._copy(a, 1 + j, self.me, (*chip, self.c), src=self.ins[a]).start()

    def relay(self, a):
        c, block = self.c, (*self.relay_from, self.c)
        self._copy(a, 1 + c, block, self.me).wait_recv()
        self._copy(a, 3, block, (*self.relay_to, c)).start()
        self._copy(a, 4 + c, block, self.sibling).start()

    def pass_on(self, a):
        c = self.c
        self._copy(a, 2 - c, (*self.relay_to, c), self.me).wait_recv()
        self._copy(a, 5 - c, (*self.relay_to, c), self.sibling).start()
        self._copy(a, 3, (*self.chips[2], c), self.me).wait_recv()
        self._copy(a, 6, (*self.chips[2], c), self.sibling).start()

    def finish(self):
        for a in range(self.na):
            self._copy(a, 0, self.sibling, self.me).wait_recv()
            for j, chip in enumerate(self.chips):
                self._copy(a, 4 + j, (*chip, 1 - self.c), self.me).wait_recv()
        for a in range(self.na):
            for k in range(7):
                self._copy(a, k, self.me, self.sibling, src=self.ins[a]).wait_send()
            self._mine(a).wait()

    @staticmethod
    def scratch(na):
        return [pltpu.SemaphoreType.DMA((7 * na,)), pltpu.SemaphoreType.DMA((7 * na,)), pltpu.SemaphoreType.DMA((na,))]


def _gathered_shape(s):
    return jax.ShapeDtypeStruct((N_DEV * s.shape[0], s.shape[1]), s.dtype)


def _all_gather_rows(shards, name):
    na = len(shards)

    def body(*refs):
        gather = _TwoLevelGather(refs[:na], refs[na:2 * na], *refs[2 * na:])
        gather.start()
        for a in range(na):
            gather.relay(a)
        for a in range(na):
            gather.pass_on(a)
        gather.finish()

    any_spec = pl.BlockSpec(memory_space=pl.ANY)
    return pl.pallas_call(
        body, name=name, out_shape=[_gathered_shape(s) for s in shards],
        in_specs=[any_spec] * na, out_specs=[any_spec] * na, scratch_shapes=_TwoLevelGather.scratch(na),
    )(*shards)


def _mixer_fwd(x2d, win_t, wout, pool_w, pool_scale, sln_g, sln_b, sgu_w, sgu_b_t, ln1_g, ln1_b, later_shards, seq):
    tokens = x2d.shape[0]
    tt = min(512, seq)
    tiles_per_seq = seq // tt
    nc = tt // GROUP
    n_tiles = tokens // tt
    n_later = len(later_shards)

    def body(x_ref, xh_ref, win_ref, wout_ref, pw_ref, ps_ref, lg_ref, lb_ref, sw_ref, sb_ref, g1_ref, b1_ref, *rest):
        shard_refs, rest = rest[:n_later], rest[n_later:]
        proj_ref, xhat_ref, rstd_ref, mix_ref, xbf_ref = rest[:5]
        gathered_refs, rest = rest[5:5 + n_later], rest[5 + n_later:]
        mix_scr, send_sems, recv_sems, local_sems = rest
        i = pl.program_id(0)
        gather = _TwoLevelGather(shard_refs, gathered_refs, send_sems, recv_sems, local_sems)

        @pl.when(i == 0)
        def _():
            gather.start()

        tile_in_seq = i % tiles_per_seq
        x = x_ref[...]
        xb = x.astype(BF16)
        xbf_ref[...] = xb
        proj = _dot_nt(xb, win_ref[...])
        proj_ref[...] = proj
        xp_prev = _dot_nt(xh_ref[...].astype(BF16), win_ref[0:POOL_WIDTH, :])
        xp_prev = jnp.where(tile_in_seq == 0, 0.0, xp_prev)
        pos = tile_in_seq * tt + lax.broadcasted_iota(jnp.int32, (tt, 1), 0)
        for g, w in enumerate(POOL_WINDOWS):
            sl = slice(g * GROUP, (g + 1) * GROUP)
            pooled = _pool_fwd(proj[:, sl], xp_prev[:, sl], _inv_count(pos, w), w)
            pre = _dot_nn(pooled.astype(BF16), pw_ref[g].astype(BF16))
            mix_scr[:, sl] = pre * ps_ref[:, sl]
        u = _gelu(proj[:, POOL_WIDTH:POOL_WIDTH + SGU_WIDTH])
        v = _gelu(proj[:, POOL_WIDTH + SGU_WIDTH:])
        vhat, _ = _ln_stats(v)
        v_ln = vhat * lg_ref[...] + lb_ref[...]
        for h in range(N_HEADS):
            ws = _masked_sgu_w(sw_ref, h).astype(BF16)
            mixed = _dot_nn(ws, _to_head_major(v_ln, h, nc).astype(BF16)) + sb_ref[:, h:h + 1]
            for c in range(nc):
                rs = slice(c * GROUP, (c + 1) * GROUP)
                mix_scr[rs, POOL_WIDTH + h * GROUP:POOL_WIDTH + (h + 1) * GROUP] = (
                    u[rs, h * GROUP:(h + 1) * GROUP] * mixed[:, c * GROUP:(c + 1) * GROUP])
        mixb = mix_scr[...].astype(BF16)
        mix_ref[...] = mixb
        r1 = ALPHA * x + _dot_nn(mixb, wout_ref[...])
        xhat, rstd = _ln_stats(r1)
        xhat_ref[...] = xhat
        rstd_ref[...] = rstd

        for a in range(n_later):
            relay_tile = min(n_tiles // 2 + a, n_tiles - 1)

            @pl.when(i == relay_tile)
            def _(a=a):
                gather.relay(a)

            @pl.when(i == max(n_tiles - n_later + a, relay_tile))
            def _(a=a):
                gather.pass_on(a)

        @pl.when(i == n_tiles - 1)
        def _():
            gather.finish()

    def tile(cols):
        return pl.BlockSpec((tt, cols), lambda i: (i, 0))

    def whole(a):
        nd = a.ndim
        return pl.BlockSpec(a.shape, lambda i: (0,) * nd)

    any_spec = pl.BlockSpec(memory_space=pl.ANY)
    halo = pl.BlockSpec((HALO, D_MODEL), lambda i: (jnp.maximum(i * (tt // HALO) - 1, 0), 0))
    consts = [win_t, wout, pool_w, pool_scale, sln_g, sln_b, sgu_w, sgu_b_t, ln1_g, ln1_b]
    return pl.pallas_call(
        body, name="mixer_fwd", grid=(n_tiles,),
        in_specs=[tile(D_MODEL), halo] + [whole(a) for a in consts] + [any_spec] * n_later,
        out_specs=[tile(IN_COLS), tile(D_MODEL), tile(1), tile(D_MODEL), tile(D_MODEL)] + [any_spec] * n_later,
        out_shape=[jax.ShapeDtypeStruct((tokens, IN_COLS), F32), jax.ShapeDtypeStruct((tokens, D_MODEL), F32),
                   jax.ShapeDtypeStruct((tokens, 1), F32), jax.ShapeDtypeStruct((tokens, D_MODEL), BF16),
                   jax.ShapeDtypeStruct((tokens, D_MODEL), BF16)] + [_gathered_shape(s) for s in later_shards],
        scratch_shapes=[pltpu.VMEM((tt, D_MODEL), F32)] + _TwoLevelGather.scratch(n_later),
        compiler_params=pltpu.CompilerParams(dimension_semantics=("arbitrary",), vmem_limit_bytes=VMEM_LIMIT_BYTES),
    )(x2d, x2d, *consts, *later_shards)


def _ffn_fwd_bwd(xhat1, rstd1, target, wgu_t, wdown, ln1_g, ln1_b, ln2_g, ln2_b):
    tokens = xhat1.shape[0]
    tt = min(256, tokens)

    def body(xhat_ref, rstd_ref, tgt_ref, wgu_hbm, wd_hbm, g1_ref, b1_ref, g2_ref, b2_ref,
             dr1_ref, dr1bf_ref, hbf_ref, dr2bf_ref, a_ref, dgu_ref, stats_ref, wgu_ref, wd_ref, gu_scr, sems):
        i = pl.program_id(0)

        @pl.when(i == 0)
        def _():
            loads = [pltpu.make_async_copy(wgu_hbm, wgu_ref, sems.at[0]),
                     pltpu.make_async_copy(wd_hbm, wd_ref, sems.at[1])]
            for cp in loads:
                cp.start()
            stats_ref[...] = jnp.zeros_like(stats_ref)
            for cp in loads:
                cp.wait()

        xhat1_t = xhat_ref[...]
        h = xhat1_t * g1_ref[...] + b1_ref[...]
        hb = h.astype(BF16)
        hbf_ref[...] = hb
        gate = _dot_nt(hb, wgu_ref[0:D_FF, :])
        up = _dot_nt(hb, wgu_ref[D_FF:, :])
        gu_scr[:, 0:D_FF] = gate
        gu_scr[:, D_FF:] = up
        ab = (gate * jax.nn.sigmoid(gate) * up).astype(BF16)
        a_ref[...] = ab
        xhat2, rstd2 = _ln_stats(ALPHA * h + _dot_nn(ab, wd_ref[...]))
        err = xhat2 * g2_ref[...] + b2_ref[...] - tgt_ref[...]
        dy = err * (1.0 / D_MODEL)
        stats_ref[0:1, :] += _rowsum(dy * xhat2)
        stats_ref[1:2, :] += _rowsum(dy)
        stats_ref[4:5, :] += _rowsum(err * err)
        dr2 = _ln_bwd(dy, xhat2, rstd2, g2_ref[...])
        dr2b = dr2.astype(BF16)
        dr2bf_ref[...] = dr2b
        da = _dot_nt(dr2b, wd_ref[...])
        gate = gu_scr[:, 0:D_FF]
        up = gu_scr[:, D_FF:]
        sg = jax.nn.sigmoid(gate)
        dgate = (da * up * (sg * (1.0 + gate * (1.0 - sg)))).astype(BF16)
        dup = (da * (gate * sg)).astype(BF16)
        dgu_ref[:, 0:D_FF] = dgate
        dgu_ref[:, D_FF:] = dup
        dh = ALPHA * dr2 + _dot_nn(dgate, wgu_ref[0:D_FF, :]) + _dot_nn(dup, wgu_ref[D_FF:, :])
        stats_ref[2:3, :] += _rowsum(dh * xhat1_t)
        stats_ref[3:4, :] += _rowsum(dh)
        dr1 = _ln_bwd(dh, xhat1_t, rstd_ref[...], g1_ref[...])
        dr1_ref[...] = dr1
        dr1bf_ref[...] = dr1.astype(BF16)

    def tile(cols):
        return pl.BlockSpec((tt, cols), lambda i: (i, 0))

    def whole(a):
        nd = a.ndim
        return pl.BlockSpec(a.shape, lambda i: (0,) * nd)

    any_spec = pl.BlockSpec(memory_space=pl.ANY)
    vecs = [ln1_g, ln1_b, ln2_g, ln2_b]
    return pl.pallas_call(
        body, name="ffn_fwd_bwd", grid=(tokens // tt,),
        in_specs=[tile(D_MODEL), tile(1), tile(D_MODEL), any_spec, any_spec] + [whole(a) for a in vecs],
        out_specs=[tile(D_MODEL), tile(D_MODEL), tile(D_MODEL), tile(D_MODEL), tile(D_FF), tile(2 * D_FF),
                   pl.BlockSpec((8, D_MODEL), lambda i: (0, 0))],
        out_shape=[jax.ShapeDtypeStruct((tokens, D_MODEL), F32), jax.ShapeDtypeStruct((tokens, D_MODEL), BF16),
                   jax.ShapeDtypeStruct((tokens, D_MODEL), BF16),
                   jax.ShapeDtypeStruct((tokens, D_MODEL), BF16), jax.ShapeDtypeStruct((tokens, D_FF), BF16),
                   jax.ShapeDtypeStruct((tokens, 2 * D_FF), BF16), jax.ShapeDtypeStruct((8, D_MODEL), F32)],
        scratch_shapes=[pltpu.VMEM(wgu_t.shape, BF16), pltpu.VMEM(wdown.shape, BF16),
                        pltpu.VMEM((tt, 2 * D_FF), F32), pltpu.SemaphoreType.DMA((2,))],
        compiler_params=pltpu.CompilerParams(dimension_semantics=("arbitrary",), vmem_limit_bytes=VMEM_LIMIT_BYTES),
    )(xhat1, rstd1, target, wgu_t, wdown, *vecs)


def _wgrad_exchange(lhs, rhs, chips_per_block, name, gather_rows=(), after=None, collective_id=None):
    assert collective_id is None or not gather_rows
    tokens, n_all = lhs.shape
    m = rhs.shape[1]
    n = n_all // N_DEV
    tw = min(2048, tokens)
    nt = tokens // tw
    cpb = chips_per_block
    nj = N_CHIPS // cpb
    ng = len(gather_rows)
    anchors = [] if after is None else [after]

    def body(l_ref, r_ref, *rest):
        small_refs, rest = rest[:ng], rest[ng + len(anchors):]
        kept_ref, sib_ref = rest[:2]
        gathered_refs, rest = rest[2:2 + ng], rest[2 + ng:]
        acc, sendbuf, send_sems, recv_sems = rest[:4]
        j, t = pl.program_id(0), pl.program_id(1)
        first, last = (j == 0) & (t == 0), (j == nj - 1) & (t == nt - 1)
        x, y, c = _mesh_position()
        if collective_id is not None:
            @pl.when(first)
            def _():
                barrier = pltpu.get_barrier_semaphore()
                pl.semaphore_signal(barrier, inc=1, device_id=(x, y, 1 - c), device_id_type=MESH)
                pl.semaphore_wait(barrier, 1)
        if ng:
            gather = _TwoLevelGather(small_refs, gathered_refs, *rest[4:7])

            @pl.when(first)
            def _():
                gather.start()

        def copy(q):
            return pltpu.make_async_remote_copy(
                src_ref=sendbuf.at[q], dst_ref=sib_ref.at[q], send_sem=send_sems.at[q], recv_sem=recv_sems.at[q],
                device_id=(x, y, 1 - c), device_id_type=MESH)

        @pl.when(t == 0)
        def _():
            acc[...] = jnp.zeros_like(acc)

        acc[...] += _dot_tn(l_ref[...], r_ref[...])

        @pl.when(t == nt - 1)
        def _():
            for qq in range(cpb):
                q = j * cpb + qq
                kept_ref[qq] = acc[pl.ds(pl.multiple_of(qq * 2 * n + c * n, 8), n), :]
                sendbuf[q] = acc[pl.ds(pl.multiple_of(qq * 2 * n + (1 - c) * n, 8), n), :].astype(BF16)
                copy(q).start()

        if ng:
            @pl.when((j == nj - 1) & (t == nt // 2))
            def _():
                for a in range(ng):
                    gather.relay(a)

            @pl.when(last)
            def _():
                for a in range(ng):
                    gather.pass_on(a)
                gather.finish()

        @pl.when(last)
        def _():
            for q in range(N_CHIPS):
                copy(q).wait_send()
                copy(q).wait_recv()

    any_spec = pl.BlockSpec(memory_space=pl.ANY)
    return pl.pallas_call(
        body, name=name, grid=(nj, nt),
        in_specs=[pl.BlockSpec((tw, 2 * n * cpb), lambda j, t: (t, j)), pl.BlockSpec((tw, m), lambda j, t: (t, 0))]
                 + [any_spec] * (ng + len(anchors)),
        out_specs=[pl.BlockSpec((cpb, n, m), lambda j, t: (j, 0, 0)), any_spec] + [any_spec] * ng,
        out_shape=[jax.ShapeDtypeStruct((N_CHIPS, n, m), F32), jax.ShapeDtypeStruct((N_CHIPS, n, m), BF16)]
                  + [_gathered_shape(s) for s in gather_rows],
        scratch_shapes=[pltpu.VMEM((2 * n * cpb, m), F32), pltpu.VMEM((N_CHIPS, n, m), BF16),
                        pltpu.SemaphoreType.DMA((N_CHIPS,)), pltpu.SemaphoreType.DMA((N_CHIPS,))]
                       + (_TwoLevelGather.scratch(ng) if ng else []),
        compiler_params=pltpu.CompilerParams(dimension_semantics=("arbitrary", "arbitrary"),
                                             vmem_limit_bytes=VMEM_LIMIT_BYTES, collective_id=collective_id),
    )(lhs, rhs, *gather_rows, *anchors)


def _received_shape(p):
    return jax.ShapeDtypeStruct((3,) + p.shape[1:], p.dtype)


ROW_POOL_SCALE, ROW_SLN_G, ROW_SLN_B, ROW_SGU_B = 0, 1, 2, 3
ROW_LN2_G, ROW_LN2_B, ROW_LN1_G, ROW_LN1_B, ROW_LOSS = 8, 9, 10, 11, 12
VEC_ROWS = 16


def _mixer_bwd(dr1, proj, win_t, wout, pool_w, pool_scale, sln_g, sln_b, sgu_w, sgu_b_t, stats, seq):
    tokens = dr1.shape[0]
    tt = min(512, seq)
    tiles_per_seq = seq // tt
    nc = tt // GROUP
    n_halo_blocks = tokens // HALO
    n_tiles = tokens // tt

    def body(dr1_ref, dr1n_ref, proj_ref, projh_ref, win_ref, wout_ref, pw_ref, ps_ref, lg_ref, lb_ref, sw_ref, sb_ref,
             stats_ref, gx_ref, dproj_ref, dmat_ref, dvec_ref, du_scr, dv_scr):
        i = pl.program_id(0)
        tile_in_seq = i % tiles_per_seq

        @pl.when(i == 0)
        def _():
            dmat_ref[...] = jnp.zeros_like(dmat_ref)
            dvec_ref[0:8, :] = jnp.zeros((8, D_MODEL), F32)
            dvec_ref[8:16, :] = stats_ref[...]

        dr1_t = dr1_ref[...]
        dr1b = dr1_t.astype(BF16)
        dmix = _dot_nt(dr1b, wout_ref[...])
        dpo_next = _dot_nt(dr1n_ref[...].astype(BF16), wout_ref[0:POOL_WIDTH, :])
        dpo_next = jnp.where(tile_in_seq == tiles_per_seq - 1, 0.0, dpo_next)
        proj = proj_ref[...]
        xp_prev = jnp.where(tile_in_seq == 0, 0.0, projh_ref[...])
        pos = tile_in_seq * tt + lax.broadcasted_iota(jnp.int32, (tt, 1), 0)
        pos_next = (tile_in_seq + 1) * tt + lax.broadcasted_iota(jnp.int32, (HALO, 1), 0)

        for g, w in enumerate(POOL_WINDOWS):
            sl = slice(g * GROUP, (g + 1) * GROUP)
            inv_cnt = _inv_count(pos, w)
            pwb = pw_ref[g].astype(BF16)
            pooledb = _pool_fwd(proj[:, sl], xp_prev[:, sl], inv_cnt, w).astype(BF16)
            pre = _dot_nn(pooledb, pwb)
            dpo = dmix[:, sl]
            dvec_ref[ROW_POOL_SCALE:ROW_POOL_SCALE + 1, sl] += _rowsum(dpo * pre)
            dsb = (dpo * ps_ref[:, sl]).astype(BF16)
            dmat_ref[g] += _dot_tn(pooledb, dsb)
            dpooled = _dot_nt(dsb, pwb)
            dpooled_next = _dot_nt((dpo_next[:, sl] * ps_ref[:, sl]).astype(BF16), pwb)
            dxp = _pool_bwd(dpooled, dpooled_next, inv_cnt, _inv_count(pos_next, w), w)
            dproj_ref[:, sl] = dxp.astype(BF16)

        zu = proj[:, POOL_WIDTH:POOL_WIDTH + SGU_WIDTH]
        zv = proj[:, POOL_WIDTH + SGU_WIDTH:]
        u = _gelu(zu)
        vhat, rstd_v = _ln_stats(_gelu(zv))
        v_ln = vhat * lg_ref[...] + lb_ref[...]
        dsg = dmix[:, POOL_WIDTH:]
        row = lax.broadcasted_iota(jnp.int32, (GROUP, GROUP), 0)
        col = lax.broadcasted_iota(jnp.int32, (GROUP, GROUP), 1)
        for h in range(N_HEADS):
            ws = _masked_sgu_w(sw_ref, h).astype(BF16)
            vh = _to_head_major(v_ln, h, nc).astype(BF16)
            mixed = _dot_nn(ws, vh) + sb_ref[:, h:h + 1]
            dsg_h = _to_head_major(dsg, h, nc)
            du_h = dsg_h * mixed
            dm_h = dsg_h * _to_head_major(u, h, nc)
            pos_sums = lax.dot_general(jnp.ones((8, nc * GROUP), F32), dm_h, (((1,), (1,)), ((), ())),
                                       precision=lax.Precision.HIGH, preferred_element_type=F32)
            dvec_ref[ROW_SGU_B + h:ROW_SGU_B + h + 1, 0:GROUP] += pos_sums[0:1, :]
            dmb = dm_h.astype(BF16)
            dmat_ref[len(POOL_WINDOWS) + h] += jnp.where(row >= col, _dot_nt(dmb, vh), 0.0)
            dv_h = _dot_tn(ws, dmb)
            for c in range(nc):
                rs = slice(c * GROUP, (c + 1) * GROUP)
                cs = slice(h * GROUP, (h + 1) * GROUP)
                du_scr[rs, cs] = du_h[:, c * GROUP:(c + 1) * GROUP]
                dv_scr[rs, cs] = dv_h[:, c * GROUP:(c + 1) * GROUP]
        dv_ln = dv_scr[...]
        dvec_ref[ROW_SLN_B:ROW_SLN_B + 1, 0:SGU_WIDTH] += _rowsum(dv_ln)
        dvec_ref[ROW_SLN_G:ROW_SLN_G + 1, 0:SGU_WIDTH] += _rowsum(dv_ln * vhat)
        dv = _ln_bwd(dv_ln, vhat, rstd_v, lg_ref[...])
        dproj_ref[:, POOL_WIDTH:POOL_WIDTH + SGU_WIDTH] = (du_scr[...] * _gelu_grad(zu)).astype(BF16)
        dproj_ref[:, POOL_WIDTH + SGU_WIDTH:] = (dv * _gelu_grad(zv)).astype(BF16)
        gx_ref[...] = ALPHA * dr1_t + _dot_nn(dproj_ref[...], win_ref[...])

    def tile(cols):
        return pl.BlockSpec((tt, cols), lambda i: (i, 0))

    def whole(a):
        nd = a.ndim
        return pl.BlockSpec(a.shape, lambda i: (0,) * nd)

    def resident(shape):
        nd = len(shape)
        return pl.BlockSpec(shape, lambda i: (0,) * nd)

    next_halo = pl.BlockSpec((HALO, D_MODEL), lambda i: (jnp.minimum((i + 1) * (tt // HALO), n_halo_blocks - 1), 0))
    prev_halo = pl.BlockSpec((HALO, POOL_WIDTH), lambda i: (jnp.maximum(i * (tt // HALO) - 1, 0), 0))
    consts = [win_t, wout, pool_w, pool_scale, sln_g, sln_b, sgu_w, sgu_b_t, stats]
    small_shapes = [(len(POOL_WINDOWS) + N_HEADS, GROUP, GROUP), (VEC_ROWS, D_MODEL)]
    return pl.pallas_call(
        body, name="mixer_bwd", grid=(n_tiles,),
        in_specs=[tile(D_MODEL), next_halo, tile(IN_COLS), prev_halo] + [whole(a) for a in consts],
        out_specs=[tile(D_MODEL), tile(IN_COLS)] + [resident(s) for s in small_shapes],
        out_shape=[jax.ShapeDtypeStruct((tokens, D_MODEL), F32), jax.ShapeDtypeStruct((tokens, IN_COLS), BF16)]
                  + [jax.ShapeDtypeStruct(s, F32) for s in small_shapes],
        scratch_shapes=[pltpu.VMEM((tt, SGU_WIDTH), F32), pltpu.VMEM((tt, SGU_WIDTH), F32)],
        compiler_params=pltpu.CompilerParams(dimension_semantics=("arbitrary",), vmem_limit_bytes=VMEM_LIMIT_BYTES),
    )(dr1, dr1, proj, proj, *consts)


def _chip_sums(kept, from_sibling, place, name):
    na = len(kept)

    def body(place_ref, *refs):
        kept_refs, sib_refs = refs[:na], refs[na:2 * na]
        bf_refs, own_refs = refs[2 * na:3 * na], refs[3 * na:]
        q = pl.program_id(0)
        for a in range(na):
            s = kept_refs[a][...] + sib_refs[a][...].astype(F32)
            bf_refs[a][...] = s.astype(BF16)

            @pl.when(q == place_ref[1])
            def _(a=a, s=s):
                own_refs[a][...] = s

    by_chip = [pl.BlockSpec((None,) + k.shape[1:], lambda q, pr: (q, 0, 0)) for k in kept]
    grid_spec = pltpu.PrefetchScalarGridSpec(
        num_scalar_prefetch=1, grid=(N_CHIPS,), in_specs=by_chip + by_chip,
        out_specs=by_chip + [pl.BlockSpec(k.shape[1:], lambda q, pr: (0, 0)) for k in kept])
    return pl.pallas_call(
        body, name=name, grid_spec=grid_spec,
        out_shape=[jax.ShapeDtypeStruct(k.shape, BF16) for k in kept]
                  + [jax.ShapeDtypeStruct(k.shape[1:], F32) for k in kept],
        compiler_params=pltpu.CompilerParams(dimension_semantics=("arbitrary",), vmem_limit_bytes=VMEM_LIMIT_BYTES),
    )(place, *kept, *from_sibling)


def _owner_copies(src_ref, land_ref, sems):
    x, y, c = _mesh_position()
    return [pltpu.make_async_remote_copy(
        src_ref=src_ref.at[2 * cx + cy], dst_ref=land_ref.at[j], send_sem=sems[j], recv_sem=sems[3 + j],
        device_id=(cx, cy, c), device_id_type=MESH) for j, (cx, cy) in enumerate(_other_chips(x, y))]


def _send_to_owners_start(chip_partial, name, collective_id):
    land = _received_shape(chip_partial)

    def body(src_ref, land_ref, *rest):
        x, y, c = _mesh_position()
        barrier = pltpu.get_barrier_semaphore()
        for cx, cy in _other_chips(x, y):
            pl.semaphore_signal(barrier, inc=1, device_id=(cx, cy, c), device_id_type=MESH)
        pl.semaphore_wait(barrier, 3)
        for cp in _owner_copies(src_ref, land_ref, rest[:6]):
            cp.start()
        rest[8][...] = jnp.zeros_like(rest[8])

    hbm = pl.BlockSpec(memory_space=pltpu.HBM)
    sem = pl.BlockSpec(memory_space=pltpu.SEMAPHORE)
    outs = pl.pallas_call(
        body, name=name,
        out_shape=[pltpu.SemaphoreType.DMA(())] * 6 + [pltpu.HBM(chip_partial.shape, chip_partial.dtype),
                                                       pltpu.HBM(land.shape, land.dtype),
                                                       jax.ShapeDtypeStruct((8, GROUP), F32)],
        in_specs=[hbm, hbm], out_specs=[sem] * 6 + [hbm, hbm, pl.BlockSpec(memory_space=pltpu.VMEM)],
        input_output_aliases={0: 6, 1: 7},
        compiler_params=pltpu.CompilerParams(has_side_effects=pltpu.SideEffectType.DATAFLOW_SIDE_EFFECTING,
                                             collective_id=collective_id),
    )(pltpu.with_memory_space_constraint(chip_partial, pltpu.HBM),
      pltpu.with_memory_space_constraint(lax.empty(land.shape, land.dtype), pltpu.HBM))
    return outs[:6], outs[6], outs[7], outs[8]


def _send_to_owners_wait(sems, src_thru, land_thru, after, name):
    def body(src_ref, land_ref, *rest):
        for cp in _owner_copies(src_ref, land_ref, rest[:6]):
            cp.wait_send()
            cp.wait_recv()

    hbm = pl.BlockSpec(memory_space=pltpu.HBM)
    sem = pl.BlockSpec(memory_space=pltpu.SEMAPHORE)
    return pl.pallas_call(
        body, name=name,
        out_shape=[pltpu.HBM(src_thru.shape, src_thru.dtype), pltpu.HBM(land_thru.shape, land_thru.dtype)],
        in_specs=[hbm, hbm] + [sem] * 6 + [pl.BlockSpec(memory_space=pl.ANY)], out_specs=[hbm, hbm],
        input_output_aliases={0: 0, 1: 1},
        compiler_params=pltpu.CompilerParams(has_side_effects=pltpu.SideEffectType.DATAFLOW_SIDE_EFFECTING),
    )(src_thru, land_thru, *sems, after)[1]


def _adamw_math(w, g, m, v):
    m = ADAM_B1 * m + (1.0 - ADAM_B1) * g
    v = ADAM_B2 * v + (1.0 - ADAM_B2) * (g * g)
    m_hat = m / (1.0 - ADAM_B1 ** ADAM_STEP)
    v_hat = v / (1.0 - ADAM_B2 ** ADAM_STEP)
    delta = -ADAM_LR * (m_hat / (jnp.sqrt(v_hat) + ADAM_EPS) + ADAM_WD * w)
    return delta, m, v


def _sum_adamw(w, m, v, own, received, name):
    rows, cols = w.shape
    rb = _row_block(rows, 256)

    def body(w_ref, m_ref, v_ref, own_ref, rec_ref, g_ref, d_ref, mo_ref, vo_ref):
        g = own_ref[...]
        for j in range(3):
            g = g + rec_ref[j].astype(F32)
        g_ref[...] = g
        d_ref[...], mo_ref[...], vo_ref[...] = _adamw_math(w_ref[...], g, m_ref[...], v_ref[...])

    spec = pl.BlockSpec((rb, cols), lambda r: (r, 0))
    return pl.pallas_call(
        body, name=name, grid=(rows // rb,),
        in_specs=[spec] * 4 + [pl.BlockSpec((3, rb, cols), lambda r: (0, r, 0))],
        out_specs=[spec] * 4, out_shape=[jax.ShapeDtypeStruct((rows, cols), F32)] * 4,
        compiler_params=pltpu.CompilerParams(dimension_semantics=("arbitrary",)),
    )(w, m, v, own, received)


SMALL_NAMES = ("pool_w", "sgu_w", "pool_scale", "sgu_ln_g", "sgu_ln_b", "sgu_b", "ln1_g", "ln1_b", "ln2_g", "ln2_b")
_SMALL_VEC_ROWS = {"pool_scale": (ROW_POOL_SCALE, POOL_WIDTH), "sgu_ln_g": (ROW_SLN_G, SGU_WIDTH),
                   "sgu_ln_b": (ROW_SLN_B, SGU_WIDTH), "ln1_g": (ROW_LN1_G, D_MODEL), "ln1_b": (ROW_LN1_B, D_MODEL),
                   "ln2_g": (ROW_LN2_G, D_MODEL), "ln2_b": (ROW_LN2_B, D_MODEL)}
_SMALL_MAT_FIRST = {"pool_w": 0, "sgu_w": len(POOL_WINDOWS)}


def _small_sum_adamw(mats_all, vecs_all, w, m, v):
    n = len(SMALL_NAMES)

    def body(mats_ref, vecs_ref, *refs):
        w_refs, m_refs, v_refs = refs[:n], refs[n:2 * n], refs[2 * n:3 * n]
        loss_ref = refs[3 * n]
        g_refs, d_refs, mo_refs, vo_refs = (refs[3 * n + 1 + k * n:3 * n + 1 + (k + 1) * n] for k in range(4))
        vec_scr = refs[7 * n + 1]

        def update(k, idx, g):
            d, mo, vo = _adamw_math(w_refs[k][idx], g, m_refs[k][idx], v_refs[k][idx])
            g_refs[k][idx], d_refs[k][idx], mo_refs[k][idx], vo_refs[k][idx] = g, d, mo, vo

        total = vecs_ref[0]
        for dev in range(1, N_DEV):
            total = total + vecs_ref[dev]
        vec_scr[...] = total
        for k, name in enumerate(SMALL_NAMES):
            if name in _SMALL_MAT_FIRST:
                for b in range(4):
                    g = mats_ref[0, _SMALL_MAT_FIRST[name] + b]
                    for dev in range(1, N_DEV):
                        g = g + mats_ref[dev, _SMALL_MAT_FIRST[name] + b]
                    update(k, (0, b), g)
            elif name == "sgu_b":
                update(k, (0,), vec_scr[ROW_SGU_B:ROW_SGU_B + N_HEADS, 0:GROUP])
            else:
                row, width = _SMALL_VEC_ROWS[name]
                update(k, (slice(None), slice(None)), vec_scr[row:row + 1, 0:width])
        loss = jnp.sum(vec_scr[ROW_LOSS:ROW_LOSS + 1, :], axis=1, keepdims=True) * (0.5 / D_MODEL)
        loss_ref[...] = jnp.broadcast_to(loss, loss_ref.shape)

    vmem = pl.BlockSpec(memory_space=pltpu.VMEM)
    shapes = [jax.ShapeDtypeStruct(w[k].shape, F32) for k in SMALL_NAMES]
    outs = pl.pallas_call(
        body, name="small_sum_adamw",
        out_shape=[jax.ShapeDtypeStruct((8, GROUP), F32)] + shapes * 4,
        in_specs=[vmem] * (2 + 3 * n), out_specs=[vmem] * (1 + 4 * n),
        scratch_shapes=[pltpu.VMEM((VEC_ROWS, D_MODEL), F32)],
    )(mats_all, vecs_all, *[w[k] for k in SMALL_NAMES], *[m[k] for k in SMALL_NAMES], *[v[k] for k in SMALL_NAMES])
    return outs[0], outs[1:1 + n], outs[1 + n:1 + 2 * n], outs[1 + 2 * n:1 + 3 * n], outs[1 + 3 * n:]


def kernel(x, w_in, pool_w, pool_scale, sgu_ln_g, sgu_ln_b, sgu_w, sgu_b, w_out, ln1_g, ln1_b, w_gate_up, w_down, ln2_g, ln2_b, loss_target, m_w_in, m_pool_w, m_pool_scale, m_sgu_ln_g, m_sgu_ln_b, m_sgu_w, m_sgu_b, m_w_out, m_ln1_g, m_ln1_b, m_w_gate_up, m_w_down, m_ln2_g, m_ln2_b, v_w_in, v_pool_w, v_pool_scale, v_sgu_ln_g, v_sgu_ln_b, v_sgu_w, v_sgu_b, v_w_out, v_ln1_g, v_ln1_b, v_w_gate_up, v_w_down, v_ln2_g, v_ln2_b):
    bl, seq, _ = x.shape
    tokens = bl * seq
    x2d = x.reshape(tokens, D_MODEL)
    tgt2d = loss_target.reshape(tokens, D_MODEL)
    my_c = lax.axis_index("c")
    place = jnp.stack([my_c, 2 * lax.axis_index("x") + lax.axis_index("y")]).astype(jnp.int32)

    win_t, wout = _all_gather_rows([w_in[0].T.astype(BF16), w_out[0].astype(BF16)], "weight_all_gather")

    pool_w3, sgu_w3 = pool_w[0], sgu_w[0]
    sgu_b_t = sgu_b[0].T
    proj, xhat1, rstd1, mix_bf, x_bf, wgu_t, wdown = _mixer_fwd(
        x2d, win_t, wout, pool_w3, pool_scale, sgu_ln_g, sgu_ln_b, sgu_w3, sgu_b_t, ln1_g, ln1_b,
        [w_gate_up[0].T.astype(BF16), w_down[0].astype(BF16)], seq)
    dr1, dr1_bf, h_bf, dr2_bf, a_bf, dgu_bf, stats = _ffn_fwd_bwd(
        xhat1, rstd1, tgt2d, wgu_t, wdown, ln1_g, ln1_b, ln2_g, ln2_b)

    kept_gu, sib_gu = _wgrad_exchange(dgu_bf, h_bf, 1, "wgrad_gate_up", collective_id=1)
    bf_gu, own_gu = _chip_sums([kept_gu], [sib_gu], place, "chip_sum_w_gate_up")
    sent_gu = _send_to_owners_start(bf_gu, "grad_scatter_w_gate_up_start", collective_id=4)
    kept_dn, sib_dn = _wgrad_exchange(a_bf, dr2_bf, 2, "wgrad_down", after=sent_gu[3], collective_id=2)
    bf_dn, own_dn = _chip_sums([kept_dn], [sib_dn], place, "chip_sum_w_down")
    sent_dn = _send_to_owners_start(bf_dn, "grad_scatter_w_down_start", collective_id=5)
    kept_out, sib_out = _wgrad_exchange(mix_bf, dr1_bf, N_CHIPS, "wgrad_out", after=sent_dn[3], collective_id=3)
    bf_out, own_out = _chip_sums([kept_out], [sib_out], place, "chip_sum_w_out")
    sent_out = _send_to_owners_start(bf_out, "grad_scatter_w_out_start", collective_id=6)
    gx, dproj_bf, d_mats, d_vecs = _mixer_bwd(
        dr1, proj, win_t, wout, pool_w3, pool_scale, sgu_ln_g, sgu_ln_b, sgu_w3, sgu_b_t, stats, seq)
    kept_in, sib_in, mats_all, vecs_all = _wgrad_exchange(
        dproj_bf, x_bf, N_CHIPS, "wgrad_in", gather_rows=[d_mats.reshape(-1, GROUP), d_vecs], after=sent_out[3])
    bf_in, own_in = _chip_sums([kept_in], [sib_in], place, "chip_sum_w_in")
    sent_in = _send_to_owners_start(bf_in, "grad_scatter_w_in_start", collective_id=7)

    grads, deltas, new_m, new_v = {}, {}, {}, {}
    after = sent_in[3]
    for nm, w, m, v, own, sent, transposed in (("w_gate_up", w_gate_up, m_w_gate_up, v_w_gate_up, own_gu, sent_gu, True),
                                               ("w_down", w_down, m_w_down, v_w_down, own_dn, sent_dn, False),
                                               ("w_out", w_out, m_w_out, v_w_out, own_out, sent_out, False),
                                               ("w_in", w_in, m_w_in, v_w_in, own_in, sent_in, True)):
        rows = (lambda a: a[0].T) if transposed else (lambda a: a[0])
        back = (lambda a: a.T[None]) if transposed else (lambda a: a[None])
        rec = _send_to_owners_wait(*sent[:3], after, "grad_scatter_" + nm + "_wait")
        g, d, mo, vo = _sum_adamw(rows(w), rows(m), rows(v), own, rec, "adamw_" + nm)
        after = vo
        grads[nm], deltas[nm], new_m[nm], new_v[nm] = back(g), back(d), back(mo), back(vo)

    small_w = {"pool_w": pool_w, "pool_scale": pool_scale, "sgu_ln_g": sgu_ln_g, "sgu_ln_b": sgu_ln_b, "sgu_w": sgu_w,
               "sgu_b": sgu_b, "ln1_g": ln1_g, "ln1_b": ln1_b, "ln2_g": ln2_g, "ln2_b": ln2_b}
    small_m = {"pool_w": m_pool_w, "pool_scale": m_pool_scale, "sgu_ln_g": m_sgu_ln_g, "sgu_ln_b": m_sgu_ln_b,
               "sgu_w": m_sgu_w, "sgu_b": m_sgu_b, "ln1_g": m_ln1_g, "ln1_b": m_ln1_b, "ln2_g": m_ln2_g, "ln2_b": m_ln2_b}
    small_v = {"pool_w": v_pool_w, "pool_scale": v_pool_scale, "sgu_ln_g": v_sgu_ln_g, "sgu_ln_b": v_sgu_ln_b,
               "sgu_w": v_sgu_w, "sgu_b": v_sgu_b, "ln1_g": v_ln1_g, "ln1_b": v_ln1_b, "ln2_g": v_ln2_g, "ln2_b": v_ln2_b}
    loss_blk, g_small, d_small, m_small, v_small = _small_sum_adamw(
        mats_all.reshape(N_DEV, 2 * N_HEADS, GROUP, GROUP), vecs_all.reshape(N_DEV, VEC_ROWS, D_MODEL),
        small_w, small_m, small_v)
    for vals, dst in ((g_small, grads), (d_small, deltas), (m_small, new_m), (v_small, new_v)):
        dst.update(zip(SMALL_NAMES, vals))

    order = ["w_in", "pool_w", "pool_scale", "sgu_ln_g", "sgu_ln_b", "sgu_w", "sgu_b", "w_out", "ln1_g", "ln1_b",
             "w_gate_up", "w_down", "ln2_g", "ln2_b"]
    return (loss_blk[0, 0], gx.reshape(bl, seq, D_MODEL), *[grads[k] for k in order], *[deltas[k] for k in order],
            *[new_m[k] for k in order], *[new_v[k] for k in order])
```

```python
import jax
import jax.numpy as jnp
from jax import lax
from jax.experimental import pallas as pl
from jax.experimental.pallas import tpu as pltpu

F32 = jnp.float32
BF16 = jnp.bfloat16
MESH = pl.DeviceIdType.MESH

D_MODEL = 1024
POOL_WIDTH = 512
SGU_WIDTH = 512
POOL_WINDOWS = (2, 4, 8, 16)
GROUP = 128
N_HEADS = 4
IN_COLS = POOL_WIDTH + 2 * SGU_WIDTH
D_FF = 2816
LN_EPS = 1e-5
ALPHA = float(2.0 ** 0.25)
HALO = 16
N_DEV = 8
N_CHIPS = 4

ADAM_LR = 0.001
ADAM_B1 = 0.9
ADAM_B2 = 0.999
ADAM_EPS = 1e-08
ADAM_WD = 0.01
ADAM_STEP = 10

VMEM_LIMIT_BYTES = 56 * 1024 * 1024

_SQRT_HALF = 0.7071067811865476
_INV_SQRT_2PI = 0.3989422804014327


def _dot_nn(a, b):
    return lax.dot_general(a, b, (((1,), (0,)), ((), ())), preferred_element_type=F32)


def _dot_nt(a, b):
    return lax.dot_general(a, b, (((1,), (1,)), ((), ())), preferred_element_type=F32)


def _dot_tn(a, b):
    return lax.dot_general(a, b, (((0,), (0,)), ((), ())), preferred_element_type=F32)


def _gelu(x):
    return 0.5 * x * (1.0 + lax.erf(x * _SQRT_HALF))


def _gelu_grad(x):
    return 0.5 * (1.0 + lax.erf(x * _SQRT_HALF)) + x * jnp.exp(-0.5 * x * x) * _INV_SQRT_2PI


def _ln_stats(r):
    mu = jnp.mean(r, axis=-1, keepdims=True)
    d = r - mu
    var = jnp.mean(d * d, axis=-1, keepdims=True)
    rstd = lax.rsqrt(var + LN_EPS)
    return d * rstd, rstd


def _ln_bwd(dout, xhat, rstd, g):
    dxh = dout * g
    m1 = jnp.mean(dxh, axis=-1, keepdims=True)
    m2 = jnp.mean(dxh * xhat, axis=-1, keepdims=True)
    return rstd * (dxh - m1 - xhat * m2)


def _rowsum(a):
    return jnp.sum(a, axis=0, keepdims=True)


def _pool_fwd(xp, xp_prev, inv_cnt, w):
    s = jnp.concatenate([xp_prev, xp], axis=0)
    k = 1
    while k < w:
        s = s + pltpu.roll(s, k, 0)
        k *= 2
    return s[HALO:, :] * inv_cnt - xp


def _pool_bwd(dpooled, dpooled_next, inv_cnt, inv_cnt_next, w):
    n = dpooled.shape[0] + HALO
    s = jnp.concatenate([dpooled * inv_cnt, dpooled_next * inv_cnt_next], axis=0)
    k = 1
    while k < w:
        s = s + pltpu.roll(s, n - k, 0)
        k *= 2
    return s[: dpooled.shape[0], :] - dpooled


def _inv_count(pos, w):
    return 1.0 / jnp.minimum(pos + 1, w).astype(F32)


def _to_head_major(a, h, nc):
    return jnp.concatenate(
        [a[c * GROUP:(c + 1) * GROUP, h * GROUP:(h + 1) * GROUP] for c in range(nc)], axis=1)


def _masked_sgu_w(sw_ref, h):
    row = lax.broadcasted_iota(jnp.int32, (GROUP, GROUP), 0)
    col = lax.broadcasted_iota(jnp.int32, (GROUP, GROUP), 1)
    return jnp.where(row >= col, sw_ref[h], 0.0)


def _row_block(rows, limit):
    return max(b for b in range(16, min(rows, limit) + 1, 16) if rows % b == 0)


def _mesh_position():
    return lax.axis_index("x"), lax.axis_index("y"), lax.axis_index("c")


def _other_chips(x, y):
    return [(1 - x, y), (x, 1 - y), (1 - x, 1 - y)]


class _TwoLevelGather:
    def __init__(self, ins, outs, send_sems, recv_sems, local_sems):
        self.ins, self.outs = ins, outs
        self.send_sems, self.recv_sems, self.local_sems = send_sems, recv_sems, local_sems
        self.na = len(ins)
        x, y, c = _mesh_position()
        self.c = c
        self.me, self.sibling = (x, y, c), (x, y, 1 - c)
        self.chips = _other_chips(x, y)
        self.relay_from = (x + (1 - c) * (1 - 2 * x), y + c * (1 - 2 * y))
        self.relay_to = (x + c * (1 - 2 * x), y + (1 - c) * (1 - 2 * y))

    def _rows(self, a, px, py, pc):
        n = self.ins[a].shape[0]
        return self.outs[a].at[pl.ds((4 * px + 2 * py + pc) * n, n), :]

    def _copy(self, a, k, block, to, src=None):
        return pltpu.make_async_remote_copy(
            src_ref=self._rows(a, *block) if src is None else src, dst_ref=self._rows(a, *block),
            send_sem=self.send_sems.at[a * 7 + k], recv_sem=self.recv_sems.at[a * 7 + k],
            device_id=to, device_id_type=MESH)

    def _mine(self, a):
        return pltpu.make_async_copy(self.ins[a], self._rows(a, *self.me), self.local_sems.at[a])

    def start(self):
        for a in range(self.na):
            self._mine(a).start()
        for a in range(self.na):
            self._copy(a, 0, self.me, self.sibling, src=self.ins[a]).start()
            for j, chip in enumerate(self.chips[:2]):
                self._copy(a, 1 + j, self.me, (*chip, self.c), src=self.ins[a]).start()

    def relay(self, a):
        c, block = self.c, (*self.relay_from, self.c)
        self._copy(a, 1 + c, block, self.me).wait_recv()
        self._copy(a, 3, block, (*self.relay_to, c)).start()
        self._copy(a, 4 + c, block, self.sibling).start()

    def pass_on(self, a):
        c = self.c
        self._copy(a, 2 - c, (*self.relay_to, c), self.me).wait_recv()
        self._copy(a, 5 - c, (*self.relay_to, c), self.sibling).start()
        self._copy(a, 3, (*self.chips[2], c), self.me).wait_recv()
        self._copy(a, 6, (*self.chips[2], c), self.sibling).start()

    def finish(self):
        for a in range(self.na):
            self._copy(a, 0, self.sibling, self.me).wait_recv()
            for j, chip in enumerate(self.chips):
                self._copy(a, 4 + j, (*chip, 1 - self.c), self.me).wait_recv()
        for a in range(self.na):
            for k in range(7):
                self._copy(a, k, self.me, self.sibling, src=self.ins[a]).wait_send()
            self._mine(a).wait()

    @staticmethod
    def scratch(na):
        return [pltpu.SemaphoreType.DMA((7 * na,)), pltpu.SemaphoreType.DMA((7 * na,)), pltpu.SemaphoreType.DMA((na,))]


def _gathered_shape(s):
    return jax.ShapeDtypeStruct((N_DEV * s.shape[0], s.shape[1]), s.dtype)


def _gathered_bf16(s):
    return jax.ShapeDtypeStruct((N_DEV * s.shape[0], s.shape[1]), BF16)


def _all_gather_rows(shards, name):
    na = len(shards)

    def body(*refs):
        bf_refs = refs[2 * na:3 * na]
        for a in range(na):
            bf_refs[a][...] = refs[a][...].astype(BF16)
        gather = _TwoLevelGather(bf_refs, refs[na:2 * na], *refs[3 * na:])
        gather.start()
        for a in range(na):
            gather.relay(a)
        for a in range(na):
            gather.pass_on(a)
        gather.finish()

    return pl.pallas_call(
        body, name=name, out_shape=[_gathered_bf16(s) for s in shards],
        in_specs=[pl.BlockSpec(memory_space=pltpu.VMEM)] * na, out_specs=[pl.BlockSpec(memory_space=pl.ANY)] * na,
        scratch_shapes=[pltpu.VMEM(s.shape, BF16) for s in shards] + _TwoLevelGather.scratch(na),
    )(*shards)


def _mixer_fwd(x2d, win_t, wout, pool_w, pool_scale, sln_g, sln_b, sgu_w, sgu_b_t, ln1_g, ln1_b, later_shards, seq):
    tokens = x2d.shape[0]
    tt = min(512, seq)
    tiles_per_seq = seq // tt
    nc = tt // GROUP
    n_tiles = tokens // tt
    n_later = len(later_shards)

    def body(x_ref, xh_ref, win_ref, wout_ref, pw_ref, ps_ref, lg_ref, lb_ref, sw_ref, sb_ref, g1_ref, b1_ref, *rest):
        shard_refs, rest = rest[:n_later], rest[n_later:]
        proj_ref, xhat_ref, rstd_ref, mix_ref, xbf_ref = rest[:5]
        gathered_refs, rest = rest[5:5 + n_later], rest[5 + n_later:]
        mix_scr, shard_bf_refs, (send_sems, recv_sems, local_sems) = rest[0], rest[1:1 + n_later], rest[1 + n_later:]
        i = pl.program_id(0)
        gather = _TwoLevelGather(shard_bf_refs, gathered_refs, send_sems, recv_sems, local_sems)

        @pl.when(i == 0)
        def _():
            for a in range(n_later):
                shard_bf_refs[a][...] = shard_refs[a][...].astype(BF16)
            gather.start()

        tile_in_seq = i % tiles_per_seq
        x = x_ref[...]
        xb = x.astype(BF16)
        xbf_ref[...] = xb
        proj = _dot_nt(xb, win_ref[...])
        proj_ref[...] = proj
        xp_prev = _dot_nt(xh_ref[...].astype(BF16), win_ref[0:POOL_WIDTH, :])
        xp_prev = jnp.where(tile_in_seq == 0, 0.0, xp_prev)
        pos = tile_in_seq * tt + lax.broadcasted_iota(jnp.int32, (tt, 1), 0)
        for g, w in enumerate(POOL_WINDOWS):
            sl = slice(g * GROUP, (g + 1) * GROUP)
            pooled = _pool_fwd(proj[:, sl], xp_prev[:, sl], _inv_count(pos, w), w)
            pre = _dot_nn(pooled.astype(BF16), pw_ref[g].astype(BF16))
            mix_scr[:, sl] = pre * ps_ref[:, sl]
        u = _gelu(proj[:, POOL_WIDTH:POOL_WIDTH + SGU_WIDTH])
        v = _gelu(proj[:, POOL_WIDTH + SGU_WIDTH:])
        vhat, _ = _ln_stats(v)
        v_ln = vhat * lg_ref[...] + lb_ref[...]
        for h in range(N_HEADS):
            ws = _masked_sgu_w(sw_ref, h).astype(BF16)
            mixed = _dot_nn(ws, _to_head_major(v_ln, h, nc).astype(BF16)) + sb_ref[:, h:h + 1]
            for c in range(nc):
                rs = slice(c * GROUP, (c + 1) * GROUP)
                mix_scr[rs, POOL_WIDTH + h * GROUP:POOL_WIDTH + (h + 1) * GROUP] = (
                    u[rs, h * GROUP:(h + 1) * GROUP] * mixed[:, c * GROUP:(c + 1) * GROUP])
        mixb = mix_scr[...].astype(BF16)
        mix_ref[...] = mixb
        r1 = ALPHA * x + _dot_nn(mixb, wout_ref[...])
        xhat, rstd = _ln_stats(r1)
        xhat_ref[...] = xhat
        rstd_ref[...] = rstd

        for a in range(n_later):
            relay_tile = min(n_tiles // 2 + a, n_tiles - 1)

            @pl.when(i == relay_tile)
            def _(a=a):
                gather.relay(a)

            @pl.when(i == max(n_tiles - n_later + a, relay_tile))
            def _(a=a):
                gather.pass_on(a)

        @pl.when(i == n_tiles - 1)
        def _():
            gather.finish()

    def tile(cols):
        return pl.BlockSpec((tt, cols), lambda i: (i, 0))

    def whole(a):
        nd = a.ndim
        return pl.BlockSpec(a.shape, lambda i: (0,) * nd)

    any_spec = pl.BlockSpec(memory_space=pl.ANY)
    halo = pl.BlockSpec((HALO, D_MODEL), lambda i: (jnp.maximum(i * (tt // HALO) - 1, 0), 0))
    consts = [win_t, wout, pool_w, pool_scale, sln_g, sln_b, sgu_w, sgu_b_t, ln1_g, ln1_b]
    return pl.pallas_call(
        body, name="mixer_fwd", grid=(n_tiles,),
        in_specs=[tile(D_MODEL), halo] + [whole(a) for a in consts] + [whole(s) for s in later_shards],
        out_specs=[tile(IN_COLS), tile(D_MODEL), tile(1), tile(D_MODEL), tile(D_MODEL)] + [any_spec] * n_later,
        out_shape=[jax.ShapeDtypeStruct((tokens, IN_COLS), F32), jax.ShapeDtypeStruct((tokens, D_MODEL), F32),
                   jax.ShapeDtypeStruct((tokens, 1), F32), jax.ShapeDtypeStruct((tokens, D_MODEL), BF16),
                   jax.ShapeDtypeStruct((tokens, D_MODEL), BF16)] + [_gathered_bf16(s) for s in later_shards],
        scratch_shapes=[pltpu.VMEM((tt, D_MODEL), F32)] + [pltpu.VMEM(s.shape, BF16) for s in later_shards]
                       + _TwoLevelGather.scratch(n_later),
        compiler_params=pltpu.CompilerParams(dimension_semantics=("arbitrary",), vmem_limit_bytes=VMEM_LIMIT_BYTES),
    )(x2d, x2d, *consts, *later_shards)


def _ffn_fwd_bwd(xhat1, rstd1, target, wgu_t, wdown, ln1_g, ln1_b, ln2_g, ln2_b):
    tokens = xhat1.shape[0]
    tt = min(256, tokens)

    def body(xhat_ref, rstd_ref, tgt_ref, wgu_hbm, wd_hbm, g1_ref, b1_ref, g2_ref, b2_ref,
             dr1_ref, dr1bf_ref, hbf_ref, dr2bf_ref, a_ref, dgu_ref, stats_ref, wgu_ref, wd_ref, gu_scr, sems):
        i = pl.program_id(0)

        @pl.when(i == 0)
        def _():
            loads = [pltpu.make_async_copy(wgu_hbm, wgu_ref, sems.at[0]),
                     pltpu.make_async_copy(wd_hbm, wd_ref, sems.at[1])]
            for cp in loads:
                cp.start()
            stats_ref[...] = jnp.zeros_like(stats_ref)
            for cp in loads:
                cp.wait()

        xhat1_t = xhat_ref[...]
        h = xhat1_t * g1_ref[...] + b1_ref[...]
        hb = h.astype(BF16)
        hbf_ref[...] = hb
        gate = _dot_nt(hb, wgu_ref[0:D_FF, :])
        up = _dot_nt(hb, wgu_ref[D_FF:, :])
        gu_scr[:, 0:D_FF] = gate
        gu_scr[:, D_FF:] = up
        ab = (gate * jax.nn.sigmoid(gate) * up).astype(BF16)
        a_ref[...] = ab
        xhat2, rstd2 = _ln_stats(ALPHA * h + _dot_nn(ab, wd_ref[...]))
        err = xhat2 * g2_ref[...] + b2_ref[...] - tgt_ref[...]
        dy = err * (1.0 / D_MODEL)
        stats_ref[0:1, :] += _rowsum(dy * xhat2)
        stats_ref[1:2, :] += _rowsum(dy)
        stats_ref[4:5, :] += _rowsum(err * err)
        dr2 = _ln_bwd(dy, xhat2, rstd2, g2_ref[...])
        dr2b = dr2.astype(BF16)
        dr2bf_ref[...] = dr2b
        da = _dot_nt(dr2b, wd_ref[...])
        gate = gu_scr[:, 0:D_FF]
        up = gu_scr[:, D_FF:]
        sg = jax.nn.sigmoid(gate)
        dgate = (da * up * (sg * (1.0 + gate * (1.0 - sg)))).astype(BF16)
        dup = (da * (gate * sg)).astype(BF16)
        dgu_ref[:, 0:D_FF] = dgate
        dgu_ref[:, D_FF:] = dup
        dh = ALPHA * dr2 + _dot_nn(dgate, wgu_ref[0:D_FF, :]) + _dot_nn(dup, wgu_ref[D_FF:, :])
        stats_ref[2:3, :] += _rowsum(dh * xhat1_t)
        stats_ref[3:4, :] += _rowsum(dh)
        dr1 = _ln_bwd(dh, xhat1_t, rstd_ref[...], g1_ref[...])
        dr1_ref[...] = dr1
        dr1bf_ref[...] = dr1.astype(BF16)

    def tile(cols):
        return pl.BlockSpec((tt, cols), lambda i: (i, 0))

    def whole(a):
        nd = a.ndim
        return pl.BlockSpec(a.shape, lambda i: (0,) * nd)

    any_spec = pl.BlockSpec(memory_space=pl.ANY)
    vecs = [ln1_g, ln1_b, ln2_g, ln2_b]
    return pl.pallas_call(
        body, name="ffn_fwd_bwd", grid=(tokens // tt,),
        in_specs=[tile(D_MODEL), tile(1), tile(D_MODEL), any_spec, any_spec] + [whole(a) for a in vecs],
        out_specs=[tile(D_MODEL), tile(D_MODEL), tile(D_MODEL), tile(D_MODEL), tile(D_FF), tile(2 * D_FF),
                   pl.BlockSpec((8, D_MODEL), lambda i: (0, 0))],
        out_shape=[jax.ShapeDtypeStruct((tokens, D_MODEL), F32), jax.ShapeDtypeStruct((tokens, D_MODEL), BF16),
                   jax.ShapeDtypeStruct((tokens, D_MODEL), BF16),
                   jax.ShapeDtypeStruct((tokens, D_MODEL), BF16), jax.ShapeDtypeStruct((tokens, D_FF), BF16),
                   jax.ShapeDtypeStruct((tokens, 2 * D_FF), BF16), jax.ShapeDtypeStruct((8, D_MODEL), F32)],
        scratch_shapes=[pltpu.VMEM(wgu_t.shape, BF16), pltpu.VMEM(wdown.shape, BF16),
                        pltpu.VMEM((tt, 2 * D_FF), F32), pltpu.SemaphoreType.DMA((2,))],
        compiler_params=pltpu.CompilerParams(dimension_semantics=("arbitrary",), vmem_limit_bytes=VMEM_LIMIT_BYTES),
    )(xhat1, rstd1, target, wgu_t, wdown, *vecs)


def _wgrad_exchange(lhs, rhs, chips_per_block, name, gather_rows=(), after=None, collective_id=None):
    assert collective_id is None or not gather_rows
    tokens, n_all = lhs.shape
    m = rhs.shape[1]
    n = n_all // N_DEV
    tw = min(2048, tokens)
    nt = tokens // tw
    cpb = chips_per_block
    nj = N_CHIPS // cpb
    ng = len(gather_rows)
    anchors = [] if after is None else [after]

    def body(l_ref, r_ref, *rest):
        small_refs, rest = rest[:ng], rest[ng + len(anchors):]
        kept_ref, sib_ref = rest[:2]
        gathered_refs, rest = rest[2:2 + ng], rest[2 + ng:]
        acc, sendbuf, send_sems, recv_sems = rest[:4]
        j, t = pl.program_id(0), pl.program_id(1)
        first, last = (j == 0) & (t == 0), (j == nj - 1) & (t == nt - 1)
        x, y, c = _mesh_position()
        if collective_id is not None:
            @pl.when(first)
            def _():
                barrier = pltpu.get_barrier_semaphore()
                pl.semaphore_signal(barrier, inc=1, device_id=(x, y, 1 - c), device_id_type=MESH)
                pl.semaphore_wait(barrier, 1)
        if ng:
            gather = _TwoLevelGather(small_refs, gathered_refs, *rest[4:7])

            @pl.when(first)
            def _():
                gather.start()

        def copy(q):
            return pltpu.make_async_remote_copy(
                src_ref=sendbuf.at[q], dst_ref=sib_ref.at[q], send_sem=send_sems.at[q], recv_sem=recv_sems.at[q],
                device_id=(x, y, 1 - c), device_id_type=MESH)

        @pl.when(t == 0)
        def _():
            acc[...] = jnp.zeros_like(acc)

        acc[...] += _dot_tn(l_ref[...], r_ref[...])

        @pl.when(t == nt - 1)
        def _():
            for qq in range(cpb):
                q = j * cpb + qq
                kept_ref[qq] = acc[pl.ds(pl.multiple_of(qq * 2 * n + c * n, 8), n), :]
                sendbuf[q] = acc[pl.ds(pl.multiple_of(qq * 2 * n + (1 - c) * n, 8), n), :].astype(BF16)
                copy(q).start()

        if ng:
            @pl.when((j == nj - 1) & (t == nt // 2))
            def _():
                for a in range(ng):
                    gather.relay(a)

            @pl.when(last)
            def _():
                for a in range(ng):
                    gather.pass_on(a)
                gather.finish()

        @pl.when(last)
        def _():
            for q in range(N_CHIPS):
                copy(q).wait_send()
                copy(q).wait_recv()

    any_spec = pl.BlockSpec(memory_space=pl.ANY)
    return pl.pallas_call(
        body, name=name, grid=(nj, nt),
        in_specs=[pl.BlockSpec((tw, 2 * n * cpb), lambda j, t: (t, j)), pl.BlockSpec((tw, m), lambda j, t: (t, 0))]
                 + [any_spec] * (ng + len(anchors)),
        out_specs=[pl.BlockSpec((cpb, n, m), lambda j, t: (j, 0, 0)), any_spec] + [any_spec] * ng,
        out_shape=[jax.ShapeDtypeStruct((N_CHIPS, n, m), F32), jax.ShapeDtypeStruct((N_CHIPS, n, m), BF16)]
                  + [_gathered_shape(s) for s in gather_rows],
        scratch_shapes=[pltpu.VMEM((2 * n * cpb, m), F32), pltpu.VMEM((N_CHIPS, n, m), BF16),
                        pltpu.SemaphoreType.DMA((N_CHIPS,)), pltpu.SemaphoreType.DMA((N_CHIPS,))]
                       + (_TwoLevelGather.scratch(ng) if ng else []),
        compiler_params=pltpu.CompilerParams(dimension_semantics=("arbitrary", "arbitrary"),
                                             vmem_limit_bytes=VMEM_LIMIT_BYTES, collective_id=collective_id),
    )(lhs, rhs, *gather_rows, *anchors)


def _received_shape(p):
    return jax.ShapeDtypeStruct((3,) + p.shape[1:], p.dtype)


ROW_POOL_SCALE, ROW_SLN_G, ROW_SLN_B, ROW_SGU_B = 0, 1, 2, 3
ROW_LN2_G, ROW_LN2_B, ROW_LN1_G, ROW_LN1_B, ROW_LOSS = 8, 9, 10, 11, 12
VEC_ROWS = 16


def _mixer_bwd(dr1, proj, win_t, wout, pool_w, pool_scale, sln_g, sln_b, sgu_w, sgu_b_t, stats, seq):
    tokens = dr1.shape[0]
    tt = min(512, seq)
    tiles_per_seq = seq // tt
    nc = tt // GROUP
    n_halo_blocks = tokens // HALO
    n_tiles = tokens // tt

    def body(dr1_ref, dr1n_ref, proj_ref, projh_ref, win_ref, wout_ref, pw_ref, ps_ref, lg_ref, lb_ref, sw_ref, sb_ref,
             stats_ref, gx_ref, dproj_ref, dmat_ref, dvec_ref, du_scr, dv_scr):
        i = pl.program_id(0)
        tile_in_seq = i % tiles_per_seq

        @pl.when(i == 0)
        def _():
            dmat_ref[...] = jnp.zeros_like(dmat_ref)
            dvec_ref[0:8, :] = jnp.zeros((8, D_MODEL), F32)
            dvec_ref[8:16, :] = stats_ref[...]

        dr1_t = dr1_ref[...]
        dr1b = dr1_t.astype(BF16)
        dmix = _dot_nt(dr1b, wout_ref[...])
        dpo_next = _dot_nt(dr1n_ref[...].astype(BF16), wout_ref[0:POOL_WIDTH, :])
        dpo_next = jnp.where(tile_in_seq == tiles_per_seq - 1, 0.0, dpo_next)
        proj = proj_ref[...]
        xp_prev = jnp.where(tile_in_seq == 0, 0.0, projh_ref[...])
        pos = tile_in_seq * tt + lax.broadcasted_iota(jnp.int32, (tt, 1), 0)
        pos_next = (tile_in_seq + 1) * tt + lax.broadcasted_iota(jnp.int32, (HALO, 1), 0)

        for g, w in enumerate(POOL_WINDOWS):
            sl = slice(g * GROUP, (g + 1) * GROUP)
            inv_cnt = _inv_count(pos, w)
            pwb = pw_ref[g].astype(BF16)
            pooledb = _pool_fwd(proj[:, sl], xp_prev[:, sl], inv_cnt, w).astype(BF16)
            pre = _dot_nn(pooledb, pwb)
            dpo = dmix[:, sl]
            dvec_ref[ROW_POOL_SCALE:ROW_POOL_SCALE + 1, sl] += _rowsum(dpo * pre)
            dsb = (dpo * ps_ref[:, sl]).astype(BF16)
            dmat_ref[g] += _dot_tn(pooledb, dsb)
            dpooled = _dot_nt(dsb, pwb)
            dpooled_next = _dot_nt((dpo_next[:, sl] * ps_ref[:, sl]).astype(BF16), pwb)
            dxp = _pool_bwd(dpooled, dpooled_next, inv_cnt, _inv_count(pos_next, w), w)
            dproj_ref[:, sl] = dxp.astype(BF16)

        zu = proj[:, POOL_WIDTH:POOL_WIDTH + SGU_WIDTH]
        zv = proj[:, POOL_WIDTH + SGU_WIDTH:]
        u = _gelu(zu)
        vhat, rstd_v = _ln_stats(_gelu(zv))
        v_ln = vhat * lg_ref[...] + lb_ref[...]
        dsg = dmix[:, POOL_WIDTH:]
        row = lax.broadcasted_iota(jnp.int32, (GROUP, GROUP), 0)
        col = lax.broadcasted_iota(jnp.int32, (GROUP, GROUP), 1)
        for h in range(N_HEADS):
            ws = _masked_sgu_w(sw_ref, h).astype(BF16)
            vh = _to_head_major(v_ln, h, nc).astype(BF16)
            mixed = _dot_nn(ws, vh) + sb_ref[:, h:h + 1]
            dsg_h = _to_head_major(dsg, h, nc)
            du_h = dsg_h * mixed
            dm_h = dsg_h * _to_head_major(u, h, nc)
            pos_sums = lax.dot_general(jnp.ones((8, nc * GROUP), F32), dm_h, (((1,), (1,)), ((), ())),
                                       precision=lax.Precision.HIGH, preferred_element_type=F32)
            dvec_ref[ROW_SGU_B + h:ROW_SGU_B + h + 1, 0:GROUP] += pos_sums[0:1, :]
            dmb = dm_h.astype(BF16)
            dmat_ref[len(POOL_WINDOWS) + h] += jnp.where(row >= col, _dot_nt(dmb, vh), 0.0)
            dv_h = _dot_tn(ws, dmb)
            for c in range(nc):
                rs = slice(c * GROUP, (c + 1) * GROUP)
                cs = slice(h * GROUP, (h + 1) * GROUP)
                du_scr[rs, cs] = du_h[:, c * GROUP:(c + 1) * GROUP]
                dv_scr[rs, cs] = dv_h[:, c * GROUP:(c + 1) * GROUP]
        dv_ln = dv_scr[...]
        dvec_ref[ROW_SLN_B:ROW_SLN_B + 1, 0:SGU_WIDTH] += _rowsum(dv_ln)
        dvec_ref[ROW_SLN_G:ROW_SLN_G + 1, 0:SGU_WIDTH] += _rowsum(dv_ln * vhat)
        dv = _ln_bwd(dv_ln, vhat, rstd_v, lg_ref[...])
        dproj_ref[:, POOL_WIDTH:POOL_WIDTH + SGU_WIDTH] = (du_scr[...] * _gelu_grad(zu)).astype(BF16)
        dproj_ref[:, POOL_WIDTH + SGU_WIDTH:] = (dv * _gelu_grad(zv)).astype(BF16)
        gx_ref[...] = ALPHA * dr1_t + _dot_nn(dproj_ref[...], win_ref[...])

    def tile(cols):
        return pl.BlockSpec((tt, cols), lambda i: (i, 0))

    def whole(a):
        nd = a.ndim
        return pl.BlockSpec(a.shape, lambda i: (0,) * nd)

    def resident(shape):
        nd = len(shape)
        return pl.BlockSpec(shape, lambda i: (0,) * nd)

    next_halo = pl.BlockSpec((HALO, D_MODEL), lambda i: (jnp.minimum((i + 1) * (tt // HALO), n_halo_blocks - 1), 0))
    prev_halo = pl.BlockSpec((HALO, POOL_WIDTH), lambda i: (jnp.maximum(i * (tt // HALO) - 1, 0), 0))
    consts = [win_t, wout, pool_w, pool_scale, sln_g, sln_b, sgu_w, sgu_b_t, stats]
    small_shapes = [(len(POOL_WINDOWS) + N_HEADS, GROUP, GROUP), (VEC_ROWS, D_MODEL)]
    return pl.pallas_call(
        body, name="mixer_bwd", grid=(n_tiles,),
        in_specs=[tile(D_MODEL), next_halo, tile(IN_COLS), prev_halo] + [whole(a) for a in consts],
        out_specs=[tile(D_MODEL), tile(IN_COLS)] + [resident(s) for s in small_shapes],
        out_shape=[jax.ShapeDtypeStruct((tokens, D_MODEL), F32), jax.ShapeDtypeStruct((tokens, IN_COLS), BF16)]
                  + [jax.ShapeDtypeStruct(s, F32) for s in small_shapes],
        scratch_shapes=[pltpu.VMEM((tt, SGU_WIDTH), F32), pltpu.VMEM((tt, SGU_WIDTH), F32)],
        compiler_params=pltpu.CompilerParams(dimension_semantics=("arbitrary",), vmem_limit_bytes=VMEM_LIMIT_BYTES),
    )(dr1, dr1, proj, proj, *consts)


def _chip_sums(kept, from_sibling, place, name):
    na = len(kept)

    def body(place_ref, *refs):
        kept_refs, sib_refs = refs[:na], refs[na:2 * na]
        bf_refs, own_refs = refs[2 * na:3 * na], refs[3 * na:]
        q = pl.program_id(0)
        for a in range(na):
            s = kept_refs[a][...] + sib_refs[a][...].astype(F32)
            bf_refs[a][...] = s.astype(BF16)

            @pl.when(q == place_ref[1])
            def _(a=a, s=s):
                own_refs[a][...] = s

    by_chip = [pl.BlockSpec((None,) + k.shape[1:], lambda q, pr: (q, 0, 0)) for k in kept]
    grid_spec = pltpu.PrefetchScalarGridSpec(
        num_scalar_prefetch=1, grid=(N_CHIPS,), in_specs=by_chip + by_chip,
        out_specs=by_chip + [pl.BlockSpec(k.shape[1:], lambda q, pr: (0, 0)) for k in kept])
    return pl.pallas_call(
        body, name=name, grid_spec=grid_spec,
        out_shape=[jax.ShapeDtypeStruct(k.shape, BF16) for k in kept]
                  + [jax.ShapeDtypeStruct(k.shape[1:], F32) for k in kept],
        compiler_params=pltpu.CompilerParams(dimension_semantics=("arbitrary",), vmem_limit_bytes=VMEM_LIMIT_BYTES),
    )(place, *kept, *from_sibling)


def _owner_copies(src_ref, land_ref, sems):
    x, y, c = _mesh_position()
    return [pltpu.make_async_remote_copy(
        src_ref=src_ref.at[2 * cx + cy], dst_ref=land_ref.at[j], send_sem=sems[j], recv_sem=sems[3 + j],
        device_id=(cx, cy, c), device_id_type=MESH) for j, (cx, cy) in enumerate(_other_chips(x, y))]


def _send_to_owners_start(chip_partial, name, collective_id):
    land = _received_shape(chip_partial)

    def body(src_ref, land_ref, *rest):
        x, y, c = _mesh_position()
        barrier = pltpu.get_barrier_semaphore()
        for cx, cy in _other_chips(x, y):
            pl.semaphore_signal(barrier, inc=1, device_id=(cx, cy, c), device_id_type=MESH)
        pl.semaphore_wait(barrier, 3)
        for cp in _owner_copies(src_ref, land_ref, rest[:6]):
            cp.start()
        rest[8][...] = jnp.zeros_like(rest[8])

    hbm = pl.BlockSpec(memory_space=pltpu.HBM)
    sem = pl.BlockSpec(memory_space=pltpu.SEMAPHORE)
    outs = pl.pallas_call(
        body, name=name,
        out_shape=[pltpu.SemaphoreType.DMA(())] * 6 + [pltpu.HBM(chip_partial.shape, chip_partial.dtype),
                                                       pltpu.HBM(land.shape, land.dtype),
                                                       jax.ShapeDtypeStruct((8, GROUP), F32)],
        in_specs=[hbm, hbm], out_specs=[sem] * 6 + [hbm, hbm, pl.BlockSpec(memory_space=pltpu.VMEM)],
        input_output_aliases={0: 6, 1: 7},
        compiler_params=pltpu.CompilerParams(has_side_effects=pltpu.SideEffectType.DATAFLOW_SIDE_EFFECTING,
                                             collective_id=collective_id),
    )(pltpu.with_memory_space_constraint(chip_partial, pltpu.HBM),
      pltpu.with_memory_space_constraint(lax.empty(land.shape, land.dtype), pltpu.HBM))
    return outs[:6], outs[6], outs[7], outs[8]


def _send_to_owners_wait(sems, src_thru, land_thru, after, name):
    def body(src_ref, land_ref, *rest):
        for cp in _owner_copies(src_ref, land_ref, rest[:6]):
            cp.wait_send()
            cp.wait_recv()

    hbm = pl.BlockSpec(memory_space=pltpu.HBM)
    sem = pl.BlockSpec(memory_space=pltpu.SEMAPHORE)
    return pl.pallas_call(
        body, name=name,
        out_shape=[pltpu.HBM(src_thru.shape, src_thru.dtype), pltpu.HBM(land_thru.shape, land_thru.dtype)],
        in_specs=[hbm, hbm] + [sem] * 6 + [pl.BlockSpec(memory_space=pl.ANY)], out_specs=[hbm, hbm],
        input_output_aliases={0: 0, 1: 1},
        compiler_params=pltpu.CompilerParams(has_side_effects=pltpu.SideEffectType.DATAFLOW_SIDE_EFFECTING),
    )(src_thru, land_thru, *sems, after)[1]


def _adamw_math(w, g, m, v):
    m = ADAM_B1 * m + (1.0 - ADAM_B1) * g
    v = ADAM_B2 * v + (1.0 - ADAM_B2) * (g * g)
    m_hat = m / (1.0 - ADAM_B1 ** ADAM_STEP)
    v_hat = v / (1.0 - ADAM_B2 ** ADAM_STEP)
    delta = -ADAM_LR * (m_hat / (jnp.sqrt(v_hat) + ADAM_EPS) + ADAM_WD * w)
    return delta, m, v


def _sum_adamw(w, m, v, own, received, name):
    rows, cols = w.shape
    rb = _row_block(rows, 256)

    def body(w_ref, m_ref, v_ref, own_ref, rec_ref, g_ref, d_ref, mo_ref, vo_ref):
        g = own_ref[...]
        for j in range(3):
            g = g + rec_ref[j].astype(F32)
        g_ref[...] = g
        d_ref[...], mo_ref[...], vo_ref[...] = _adamw_math(w_ref[...], g, m_ref[...], v_ref[...])

    spec = pl.BlockSpec((rb, cols), lambda r: (r, 0))
    return pl.pallas_call(
        body, name=name, grid=(rows // rb,),
        in_specs=[spec] * 4 + [pl.BlockSpec((3, rb, cols), lambda r: (0, r, 0))],
        out_specs=[spec] * 4, out_shape=[jax.ShapeDtypeStruct((rows, cols), F32)] * 4,
        compiler_params=pltpu.CompilerParams(dimension_semantics=("arbitrary",)),
    )(w, m, v, own, received)


SMALL_NAMES = ("pool_w", "sgu_w", "pool_scale", "sgu_ln_g", "sgu_ln_b", "sgu_b", "ln1_g", "ln1_b", "ln2_g", "ln2_b")
_SMALL_VEC_ROWS = {"pool_scale": (ROW_POOL_SCALE, POOL_WIDTH), "sgu_ln_g": (ROW_SLN_G, SGU_WIDTH),
                   "sgu_ln_b": (ROW_SLN_B, SGU_WIDTH), "ln1_g": (ROW_LN1_G, D_MODEL), "ln1_b": (ROW_LN1_B, D_MODEL),
                   "ln2_g": (ROW_LN2_G, D_MODEL), "ln2_b": (ROW_LN2_B, D_MODEL)}
_SMALL_MAT_FIRST = {"pool_w": 0, "sgu_w": len(POOL_WINDOWS)}


def _small_sum_adamw(mats_all, vecs_all, w, m, v):
    n = len(SMALL_NAMES)

    def body(mats_ref, vecs_ref, *refs):
        w_refs, m_refs, v_refs = refs[:n], refs[n:2 * n], refs[2 * n:3 * n]
        loss_ref = refs[3 * n]
        g_refs, d_refs, mo_refs, vo_refs = (refs[3 * n + 1 + k * n:3 * n + 1 + (k + 1) * n] for k in range(4))
        vec_scr = refs[7 * n + 1]

        def update(k, idx, g):
            d, mo, vo = _adamw_math(w_refs[k][idx], g, m_refs[k][idx], v_refs[k][idx])
            g_refs[k][idx], d_refs[k][idx], mo_refs[k][idx], vo_refs[k][idx] = g, d, mo, vo

        total = vecs_ref[0]
        for dev in range(1, N_DEV):
            total = total + vecs_ref[dev]
        vec_scr[...] = total
        for k, name in enumerate(SMALL_NAMES):
            if name in _SMALL_MAT_FIRST:
                for b in range(4):
                    g = mats_ref[0, _SMALL_MAT_FIRST[name] + b]
                    for dev in range(1, N_DEV):
                        g = g + mats_ref[dev, _SMALL_MAT_FIRST[name] + b]
                    update(k, (0, b), g)
            elif name == "sgu_b":
                update(k, (0,), vec_scr[ROW_SGU_B:ROW_SGU_B + N_HEADS, 0:GROUP])
            else:
                row, width = _SMALL_VEC_ROWS[name]
                update(k, (slice(None), slice(None)), vec_scr[row:row + 1, 0:width])
        loss = jnp.sum(vec_scr[ROW_LOSS:ROW_LOSS + 1, :], axis=1, keepdims=True) * (0.5 / D_MODEL)
        loss_ref[...] = jnp.broadcast_to(loss, loss_ref.shape)

    vmem = pl.BlockSpec(memory_space=pltpu.VMEM)
    shapes = [jax.ShapeDtypeStruct(w[k].shape, F32) for k in SMALL_NAMES]
    outs = pl.pallas_call(
        body, name="small_sum_adamw",
        out_shape=[jax.ShapeDtypeStruct((8, GROUP), F32)] + shapes * 4,
        in_specs=[vmem] * (2 + 3 * n), out_specs=[vmem] * (1 + 4 * n),
        scratch_shapes=[pltpu.VMEM((VEC_ROWS, D_MODEL), F32)],
    )(mats_all, vecs_all, *[w[k] for k in SMALL_NAMES], *[m[k] for k in SMALL_NAMES], *[v[k] for k in SMALL_NAMES])
    return outs[0], outs[1:1 + n], outs[1 + n:1 + 2 * n], outs[1 + 2 * n:1 + 3 * n], outs[1 + 3 * n:]


def kernel(x, w_in, pool_w, pool_scale, sgu_ln_g, sgu_ln_b, sgu_w, sgu_b, w_out, ln1_g, ln1_b, w_gate_up, w_down, ln2_g, ln2_b, loss_target, m_w_in, m_pool_w, m_pool_scale, m_sgu_ln_g, m_sgu_ln_b, m_sgu_w, m_sgu_b, m_w_out, m_ln1_g, m_ln1_b, m_w_gate_up, m_w_down, m_ln2_g, m_ln2_b, v_w_in, v_pool_w, v_pool_scale, v_sgu_ln_g, v_sgu_ln_b, v_sgu_w, v_sgu_b, v_w_out, v_ln1_g, v_ln1_b, v_w_gate_up, v_w_down, v_ln2_g, v_ln2_b):
    bl, seq, _ = x.shape
    tokens = bl * seq
    x2d = x.reshape(tokens, D_MODEL)
    tgt2d = loss_target.reshape(tokens, D_MODEL)
    my_c = lax.axis_index("c")
    place = jnp.stack([my_c, 2 * lax.axis_index("x") + lax.axis_index("y")]).astype(jnp.int32)

    win_t, wout = _all_gather_rows([w_in[0].T, w_out[0]], "weight_all_gather")

    pool_w3, sgu_w3 = pool_w[0], sgu_w[0]
    sgu_b_t = sgu_b[0].T
    proj, xhat1, rstd1, mix_bf, x_bf, wgu_t, wdown = _mixer_fwd(
        x2d, win_t, wout, pool_w3, pool_scale, sgu_ln_g, sgu_ln_b, sgu_w3, sgu_b_t, ln1_g, ln1_b,
        [w_gate_up[0].T, w_down[0]], seq)
    dr1, dr1_bf, h_bf, dr2_bf, a_bf, dgu_bf, stats = _ffn_fwd_bwd(
        xhat1, rstd1, tgt2d, wgu_t, wdown, ln1_g, ln1_b, ln2_g, ln2_b)

    kept_gu, sib_gu = _wgrad_exchange(dgu_bf, h_bf, 1, "wgrad_gate_up", collective_id=1)
    bf_gu, own_gu = _chip_sums([kept_gu], [sib_gu], place, "chip_sum_w_gate_up")
    sent_gu = _send_to_owners_start(bf_gu, "grad_scatter_w_gate_up_start", collective_id=4)
    kept_dn, sib_dn = _wgrad_exchange(a_bf, dr2_bf, 2, "wgrad_down", after=sent_gu[3], collective_id=2)
    bf_dn, own_dn = _chip_sums([kept_dn], [sib_dn], place, "chip_sum_w_down")
    sent_dn = _send_to_owners_start(bf_dn, "grad_scatter_w_down_start", collective_id=5)
    kept_out, sib_out = _wgrad_exchange(mix_bf, dr1_bf, N_CHIPS, "wgrad_out", after=sent_dn[3], collective_id=3)
    bf_out, own_out = _chip_sums([kept_out], [sib_out], place, "chip_sum_w_out")
    sent_out = _send_to_owners_start(bf_out, "grad_scatter_w_out_start", collective_id=6)
    gx, dproj_bf, d_mats, d_vecs = _mixer_bwd(
        dr1, proj, win_t, wout, pool_w3, pool_scale, sgu_ln_g, sgu_ln_b, sgu_w3, sgu_b_t, stats, seq)
    kept_in, sib_in, mats_all, vecs_all = _wgrad_exchange(
        dproj_bf, x_bf, N_CHIPS, "wgrad_in", gather_rows=[d_mats.reshape(-1, GROUP), d_vecs], after=sent_out[3])
    bf_in, own_in = _chip_sums([kept_in], [sib_in], place, "chip_sum_w_in")
    sent_in = _send_to_owners_start(bf_in, "grad_scatter_w_in_start", collective_id=7)

    grads, deltas, new_m, new_v = {}, {}, {}, {}
    after = sent_in[3]
    for nm, w, m, v, own, sent, transposed in (("w_gate_up", w_gate_up, m_w_gate_up, v_w_gate_up, own_gu, sent_gu, True),
                                               ("w_down", w_down, m_w_down, v_w_down, own_dn, sent_dn, False),
                                               ("w_out", w_out, m_w_out, v_w_out, own_out, sent_out, False),
                                               ("w_in", w_in, m_w_in, v_w_in, own_in, sent_in, True)):
        rows = (lambda a: a[0].T) if transposed else (lambda a: a[0])
        back = (lambda a: a.T[None]) if transposed else (lambda a: a[None])
        rec = _send_to_owners_wait(*sent[:3], after, "grad_scatter_" + nm + "_wait")
        g, d, mo, vo = _sum_adamw(rows(w), rows(m), rows(v), own, rec, "adamw_" + nm)
        after = vo
        grads[nm], deltas[nm], new_m[nm], new_v[nm] = back(g), back(d), back(mo), back(vo)

    small_w = {"pool_w": pool_w, "pool_scale": pool_scale, "sgu_ln_g": sgu_ln_g, "sgu_ln_b": sgu_ln_b, "sgu_w": sgu_w,
               "sgu_b": sgu_b, "ln1_g": ln1_g, "ln1_b": ln1_b, "ln2_g": ln2_g, "ln2_b": ln2_b}
    small_m = {"pool_w": m_pool_w, "pool_scale": m_pool_scale, "sgu_ln_g": m_sgu_ln_g, "sgu_ln_b": m_sgu_ln_b,
               "sgu_w": m_sgu_w, "sgu_b": m_sgu_b, "ln1_g": m_ln1_g, "ln1_b": m_ln1_b, "ln2_g": m_ln2_g, "ln2_b": m_ln2_b}
    small_v = {"pool_w": v_pool_w, "pool_scale": v_pool_scale, "sgu_ln_g": v_sgu_ln_g, "sgu_ln_b": v_sgu_ln_b,
               "sgu_w": v_sgu_w, "sgu_b": v_sgu_b, "ln1_g": v_ln1_g, "ln1_b": v_ln1_b, "ln2_g": v_ln2_g, "ln2_b": v_ln2_b}
    loss_blk, g_small, d_small, m_small, v_small = _small_sum_adamw(
        mats_all.reshape(N_DEV, 2 * N_HEADS, GROUP, GROUP), vecs_all.reshape(N_DEV, VEC_ROWS, D_MODEL),
        small_w, small_m, small_v)
    for vals, dst in ((g_small, grads), (d_small, deltas), (m_small, new_m), (v_small, new_v)):
        dst.update(zip(SMALL_NAMES, vals))

    order = ["w_in", "pool_w", "pool_scale", "sgu_ln_g", "sgu_ln_b", "sgu_w", "sgu_b", "w_out", "ln1_g", "ln1_b",
             "w_gate_up", "w_down", "ln2_g", "ln2_b"]
    return (loss_blk[0, 0], gx.reshape(bl, seq, D_MODEL), *[grads[k] for k in order], *[deltas[k] for k in order],
            *[new_m[k] for k in order], *[new_v[k] for k in order])
```

```python
import jax
import jax.numpy as jnp
from jax import lax
from jax.experimental import pallas as pl
from jax.experimental.pallas import tpu as pltpu

F32 = jnp.float32
BF16 = jnp.bfloat16
MESH = pl.DeviceIdType.MESH

D_MODEL = 1024
POOL_WIDTH = 512
SGU_WIDTH = 512
POOL_WINDOWS = (2, 4, 8, 16)
GROUP = 128
N_HEADS = 4
IN_COLS = POOL_WIDTH + 2 * SGU_WIDTH
D_FF = 2816
LN_EPS = 1e-5
ALPHA = float(2.0 ** 0.25)
HALO = 16
N_DEV = 8
N_CHIPS = 4

ADAM_LR = 0.001
ADAM_B1 = 0.9
ADAM_B2 = 0.999
ADAM_EPS = 1e-08
ADAM_WD = 0.01
ADAM_STEP = 10

VMEM_LIMIT_BYTES = 56 * 1024 * 1024

_SQRT_HALF = 0.7071067811865476
_INV_SQRT_2PI = 0.3989422804014327


def _dot_nn(a, b):
    return lax.dot_general(a, b, (((1,), (0,)), ((), ())), preferred_element_type=F32)


def _dot_nt(a, b):
    return lax.dot_general(a, b, (((1,), (1,)), ((), ())), preferred_element_type=F32)


def _dot_tn(a, b):
    return lax.dot_general(a, b, (((0,), (0,)), ((), ())), preferred_element_type=F32)


def _gelu(x):
    return 0.5 * x * (1.0 + lax.erf(x * _SQRT_HALF))


def _gelu_grad(x):
    return 0.5 * (1.0 + lax.erf(x * _SQRT_HALF)) + x * jnp.exp(-0.5 * x * x) * _INV_SQRT_2PI


def _ln_stats(r):
    mu = jnp.mean(r, axis=-1, keepdims=True)
    d = r - mu
    var = jnp.mean(d * d, axis=-1, keepdims=True)
    rstd = lax.rsqrt(var + LN_EPS)
    return d * rstd, rstd


def _ln_bwd(dout, xhat, rstd, g):
    dxh = dout * g
    m1 = jnp.mean(dxh, axis=-1, keepdims=True)
    m2 = jnp.mean(dxh * xhat, axis=-1, keepdims=True)
    return rstd * (dxh - m1 - xhat * m2)


def _rowsum(a):
    return jnp.sum(a, axis=0, keepdims=True)


def _pool_fwd(xp, xp_prev, inv_cnt, w):
    s = jnp.concatenate([xp_prev, xp], axis=0)
    k = 1
    while k < w:
        s = s + pltpu.roll(s, k, 0)
        k *= 2
    return s[HALO:, :] * inv_cnt - xp


def _pool_bwd(dpooled, dpooled_next, inv_cnt, inv_cnt_next, w):
    n = dpooled.shape[0] + HALO
    s = jnp.concatenate([dpooled * inv_cnt, dpooled_next * inv_cnt_next], axis=0)
    k = 1
    while k < w:
        s = s + pltpu.roll(s, n - k, 0)
        k *= 2
    return s[: dpooled.shape[0], :] - dpooled


def _inv_count(pos, w):
    return 1.0 / jnp.minimum(pos + 1, w).astype(F32)


def _to_head_major(a, h, nc):
    return jnp.concatenate(
        [a[c * GROUP:(c + 1) * GROUP, h * GROUP:(h + 1) * GROUP] for c in range(nc)], axis=1)


def _masked_sgu_w(sw_ref, h):
    row = lax.broadcasted_iota(jnp.int32, (GROUP, GROUP), 0)
    col = lax.broadcasted_iota(jnp.int32, (GROUP, GROUP), 1)
    return jnp.where(row >= col, sw_ref[h], 0.0)


def _row_block(rows, limit):
    return max(b for b in range(16, min(rows, limit) + 1, 16) if rows % b == 0)


def _mesh_position():
    return lax.axis_index("x"), lax.axis_index("y"), lax.axis_index("c")


def _other_chips(x, y):
    return [(1 - x, y), (x, 1 - y), (1 - x, 1 - y)]


class _TwoLevelGather:
    def __init__(self, ins, outs, send_sems, recv_sems, local_sems):
        self.ins, self.outs = ins, outs
        self.send_sems, self.recv_sems, self.local_sems = send_sems, recv_sems, local_sems
        self.na = len(ins)
        x, y, c = _mesh_position()
        self.c = c
        self.me, self.sibling = (x, y, c), (x, y, 1 - c)
        self.chips = _other_chips(x, y)
        self.relay_from = (x + (1 - c) * (1 - 2 * x), y + c * (1 - 2 * y))
        self.relay_to = (x + c * (1 - 2 * x), y + (1 - c) * (1 - 2 * y))

    def _rows(self, a, px, py, pc):
        n = self.ins[a].shape[0]
        return self.outs[a].at[pl.ds((4 * px + 2 * py + pc) * n, n), :]

    def _copy(self, a, k, block, to, src=None):
        return pltpu.make_async_remote_copy(
            src_ref=self._rows(a, *block) if src is None else src, dst_ref=self._rows(a, *block),
            send_sem=self.send_sems.at[a * 7 + k], recv_sem=self.recv_sems.at[a * 7 + k],
            device_id=to, device_id_type=MESH)

    def _mine(self, a):
        return pltpu.make_async_copy(self.ins[a], self._rows(a, *self.me), self.local_sems.at[a])

    def start(self):
        for a in range(self.na):
            self._mine(a).start()
        for a in range(self.na):
            self._copy(a, 0, self.me, self.sibling, src=self.ins[a]).start()
            for j, chip in enumerate(self.chips[:2]):
                self._copy(a, 1 + j, self.me, (*chip, self.c), src=self.ins[a]).start()

    def relay(self, a):
        c, block = self.c, (*self.relay_from, self.c)
        self._copy(a, 1 + c, block, self.me).wait_recv()
        self._copy(a, 3, block, (*self.relay_to, c)).start()
        self._copy(a, 4 + c, block, self.sibling).start()

    def pass_on(self, a):
        c = self.c
        self._copy(a, 2 - c, (*self.relay_to, c), self.me).wait_recv()
        self._copy(a, 5 - c, (*self.relay_to, c), self.sibling).start()
        self._copy(a, 3, (*self.chips[2], c), self.me).wait_recv()
        self._copy(a, 6, (*self.chips[2], c), self.sibling).start()

    def finish(self):
        for a in range(self.na):
            self._copy(a, 0, self.sibling, self.me).wait_recv()
            for j, chip in enumerate(self.chips):
                self._copy(a, 4 + j, (*chip, 1 - self.c), self.me).wait_recv()
        for a in range(self.na):
            for k in range(7):
                self._copy(a, k, self.me, self.sibling, src=self.ins[a]).wait_send()
            self._mine(a).wait()

    @staticmethod
    def scratch(na):
        return [pltpu.SemaphoreType.DMA((7 * na,)), pltpu.SemaphoreType.DMA((7 * na,)), pltpu.SemaphoreType.DMA((na,))]


def _gathered_shape(s):
    return jax.ShapeDtypeStruct((N_DEV * s.shape[0], s.shape[1]), s.dtype)


def _gathered_bf16(s):
    return jax.ShapeDtypeStruct((N_DEV * s.shape[0], s.shape[1]), BF16)


def _all_gather_rows(shards, name):
    na = len(shards)

    def body(*refs):
        bf_refs = refs[2 * na:3 * na]
        for a in range(na):
            bf_refs[a][...] = refs[a][...].astype(BF16)
        gather = _TwoLevelGather(bf_refs, refs[na:2 * na], *refs[3 * na:])
        gather.start()
        for a in range(na):
            gather.relay(a)
        for a in range(na):
            gather.pass_on(a)
        gather.finish()

    return pl.pallas_call(
        body, name=name, out_shape=[_gathered_bf16(s) for s in shards],
        in_specs=[pl.BlockSpec(memory_space=pltpu.VMEM)] * na, out_specs=[pl.BlockSpec(memory_space=pl.ANY)] * na,
        scratch_shapes=[pltpu.VMEM(s.shape, BF16) for s in shards] + _TwoLevelGather.scratch(na),
    )(*shards)


def _mixer_fwd(x2d, win_t, wout, pool_w, pool_scale, sln_g, sln_b, sgu_w, sgu_b_t, ln1_g, ln1_b, later_shards, seq):
    tokens = x2d.shape[0]
    tt = min(512, seq)
    tiles_per_seq = seq // tt
    nc = tt // GROUP
    n_tiles = tokens // tt
    n_later = len(later_shards)

    def body(x_ref, xh_ref, win_ref, wout_ref, pw_ref, ps_ref, lg_ref, lb_ref, sw_ref, sb_ref, g1_ref, b1_ref, *rest):
        shard_refs, rest = rest[:n_later], rest[n_later:]
        proj_ref, xhat_ref, rstd_ref, mix_ref, xbf_ref = rest[:5]
        gathered_refs, rest = rest[5:5 + n_later], rest[5 + n_later:]
        mix_scr, shard_bf_refs, (send_sems, recv_sems, local_sems) = rest[0], rest[1:1 + n_later], rest[1 + n_later:]
        i = pl.program_id(0)
        gather = _TwoLevelGather(shard_bf_refs, gathered_refs, send_sems, recv_sems, local_sems)

        @pl.when(i == 0)
        def _():
            for a in range(n_later):
                shard_bf_refs[a][...] = shard_refs[a][...].astype(BF16)
            gather.start()

        tile_in_seq = i % tiles_per_seq
        x = x_ref[...]
        xb = x.astype(BF16)
        xbf_ref[...] = xb
        proj = _dot_nt(xb, win_ref[...])
        proj_ref[...] = proj
        xp_prev = _dot_nt(xh_ref[...].astype(BF16), win_ref[0:POOL_WIDTH, :])
        xp_prev = jnp.where(tile_in_seq == 0, 0.0, xp_prev)
        pos = tile_in_seq * tt + lax.broadcasted_iota(jnp.int32, (tt, 1), 0)
        for g, w in enumerate(POOL_WINDOWS):
            sl = slice(g * GROUP, (g + 1) * GROUP)
            pooled = _pool_fwd(proj[:, sl], xp_prev[:, sl], _inv_count(pos, w), w)
            pre = _dot_nn(pooled.astype(BF16), pw_ref[g].astype(BF16))
            mix_scr[:, sl] = pre * ps_ref[:, sl]
        u = _gelu(proj[:, POOL_WIDTH:POOL_WIDTH + SGU_WIDTH])
        v = _gelu(proj[:, POOL_WIDTH + SGU_WIDTH:])
        vhat, _ = _ln_stats(v)
        v_ln = vhat * lg_ref[...] + lb_ref[...]
        for h in range(N_HEADS):
            ws = _masked_sgu_w(sw_ref, h).astype(BF16)
            mixed = _dot_nn(ws, _to_head_major(v_ln, h, nc).astype(BF16)) + sb_ref[:, h:h + 1]
            for c in range(nc):
                rs = slice(c * GROUP, (c + 1) * GROUP)
                mix_scr[rs, POOL_WIDTH + h * GROUP:POOL_WIDTH + (h + 1) * GROUP] = (
                    u[rs, h * GROUP:(h + 1) * GROUP] * mixed[:, c * GROUP:(c + 1) * GROUP])
        mixb = mix_scr[...].astype(BF16)
        mix_ref[...] = mixb
        r1 = ALPHA * x + _dot_nn(mixb, wout_ref[...])
        xhat, rstd = _ln_stats(r1)
        xhat_ref[...] = xhat
        rstd_ref[...] = rstd

        for a in range(n_later):
            relay_tile = min(n_tiles // 2 + a, n_tiles - 1)

            @pl.when(i == relay_tile)
            def _(a=a):
                gather.relay(a)

            @pl.when(i == max(n_tiles - n_later + a, relay_tile))
            def _(a=a):
                gather.pass_on(a)

        @pl.when(i == n_tiles - 1)
        def _():
            gather.finish()

    def tile(cols):
        return pl.BlockSpec((tt, cols), lambda i: (i, 0))

    def whole(a):
        nd = a.ndim
        return pl.BlockSpec(a.shape, lambda i: (0,) * nd)

    any_spec = pl.BlockSpec(memory_space=pl.ANY)
    halo = pl.BlockSpec((HALO, D_MODEL), lambda i: (jnp.maximum(i * (tt // HALO) - 1, 0), 0))
    consts = [win_t, wout, pool_w, pool_scale, sln_g, sln_b, sgu_w, sgu_b_t, ln1_g, ln1_b]
    return pl.pallas_call(
        body, name="mixer_fwd", grid=(n_tiles,),
        in_specs=[tile(D_MODEL), halo] + [whole(a) for a in consts] + [whole(s) for s in later_shards],
        out_specs=[tile(IN_COLS), tile(D_MODEL), tile(1), tile(D_MODEL), tile(D_MODEL)] + [any_spec] * n_later,
        out_shape=[jax.ShapeDtypeStruct((tokens, IN_COLS), F32), jax.ShapeDtypeStruct((tokens, D_MODEL), F32),
                   jax.ShapeDtypeStruct((tokens, 1), F32), jax.ShapeDtypeStruct((tokens, D_MODEL), BF16),
                   jax.ShapeDtypeStruct((tokens, D_MODEL), BF16)] + [_gathered_bf16(s) for s in later_shards],
        scratch_shapes=[pltpu.VMEM((tt, D_MODEL), F32)] + [pltpu.VMEM(s.shape, BF16) for s in later_shards]
                       + _TwoLevelGather.scratch(n_later),
        compiler_params=pltpu.CompilerParams(dimension_semantics=("arbitrary",), vmem_limit_bytes=VMEM_LIMIT_BYTES),
    )(x2d, x2d, *consts, *later_shards)


def _ffn_fwd_bwd(xhat1, rstd1, target, wgu_t, wdown, ln1_g, ln1_b, ln2_g, ln2_b):
    tokens = xhat1.shape[0]
    tt = min(256, tokens)

    def body(xhat_ref, rstd_ref, tgt_ref, wgu_hbm, wd_hbm, g1_ref, b1_ref, g2_ref, b2_ref,
             dr1_ref, acts_ref, grads_ref, stats_ref, wgu_ref, wd_ref, gu_scr, sems):
        i = pl.program_id(0)

        @pl.when(i == 0)
        def _():
            loads = [pltpu.make_async_copy(wgu_hbm, wgu_ref, sems.at[0]),
                     pltpu.make_async_copy(wd_hbm, wd_ref, sems.at[1])]
            for cp in loads:
                cp.start()
            stats_ref[...] = jnp.zeros_like(stats_ref)
            for cp in loads:
                cp.wait()

        xhat1_t = xhat_ref[...]
        h = xhat1_t * g1_ref[...] + b1_ref[...]
        hb = h.astype(BF16)
        acts_ref[:, 0:D_MODEL] = hb
        gate = _dot_nt(hb, wgu_ref[0:D_FF, :])
        up = _dot_nt(hb, wgu_ref[D_FF:, :])
        gu_scr[:, 0:D_FF] = gate
        gu_scr[:, D_FF:] = up
        ab = (gate * jax.nn.sigmoid(gate) * up).astype(BF16)
        grads_ref[:, 2 * D_FF:] = ab
        xhat2, rstd2 = _ln_stats(ALPHA * h + _dot_nn(ab, wd_ref[...]))
        err = xhat2 * g2_ref[...] + b2_ref[...] - tgt_ref[...]
        dy = err * (1.0 / D_MODEL)
        stats_ref[0:1, :] += _rowsum(dy * xhat2)
        stats_ref[1:2, :] += _rowsum(dy)
        stats_ref[4:5, :] += _rowsum(err * err)
        dr2 = _ln_bwd(dy, xhat2, rstd2, g2_ref[...])
        dr2b = dr2.astype(BF16)
        acts_ref[:, D_MODEL:2 * D_MODEL] = dr2b
        da = _dot_nt(dr2b, wd_ref[...])
        gate = gu_scr[:, 0:D_FF]
        up = gu_scr[:, D_FF:]
        sg = jax.nn.sigmoid(gate)
        dgate = (da * up * (sg * (1.0 + gate * (1.0 - sg)))).astype(BF16)
        dup = (da * (gate * sg)).astype(BF16)
        grads_ref[:, 0:D_FF] = dgate
        grads_ref[:, D_FF:2 * D_FF] = dup
        dh = ALPHA * dr2 + _dot_nn(dgate, wgu_ref[0:D_FF, :]) + _dot_nn(dup, wgu_ref[D_FF:, :])
        stats_ref[2:3, :] += _rowsum(dh * xhat1_t)
        stats_ref[3:4, :] += _rowsum(dh)
        dr1 = _ln_bwd(dh, xhat1_t, rstd_ref[...], g1_ref[...])
        dr1_ref[...] = dr1
        acts_ref[:, 2 * D_MODEL:] = dr1.astype(BF16)

    def tile(cols):
        return pl.BlockSpec((tt, cols), lambda i: (i, 0))

    def whole(a):
        nd = a.ndim
        return pl.BlockSpec(a.shape, lambda i: (0,) * nd)

    any_spec = pl.BlockSpec(memory_space=pl.ANY)
    vecs = [ln1_g, ln1_b, ln2_g, ln2_b]
    return pl.pallas_call(
        body, name="ffn_fwd_bwd", grid=(tokens // tt,),
        in_specs=[tile(D_MODEL), tile(1), tile(D_MODEL), any_spec, any_spec] + [whole(a) for a in vecs],
        out_specs=[tile(D_MODEL), tile(3 * D_MODEL), tile(3 * D_FF), pl.BlockSpec((8, D_MODEL), lambda i: (0, 0))],
        out_shape=[jax.ShapeDtypeStruct((tokens, D_MODEL), F32), jax.ShapeDtypeStruct((tokens, 3 * D_MODEL), BF16),
                   jax.ShapeDtypeStruct((tokens, 3 * D_FF), BF16), jax.ShapeDtypeStruct((8, D_MODEL), F32)],
        scratch_shapes=[pltpu.VMEM(wgu_t.shape, BF16), pltpu.VMEM(wdown.shape, BF16),
                        pltpu.VMEM((tt, 2 * D_FF), F32), pltpu.SemaphoreType.DMA((2,))],
        compiler_params=pltpu.CompilerParams(dimension_semantics=("arbitrary",), vmem_limit_bytes=VMEM_LIMIT_BYTES),
    )(xhat1, rstd1, target, wgu_t, wdown, *vecs)


def _wgrad_exchange(lhs, rhs, chips_per_block, name, gather_rows=(), after=None, collective_id=None,
                    lhs_cols=None, rhs_block=None):
    assert collective_id is None or not gather_rows
    tokens = lhs.shape[0]
    first_col, n_all = (0, lhs.shape[1]) if lhs_cols is None else lhs_cols
    m = rhs.shape[1] if rhs_block is None else D_MODEL
    n = n_all // N_DEV
    tw = min(2048, tokens)
    nt = tokens // tw
    cpb = chips_per_block
    nj = N_CHIPS // cpb
    lhs_block0 = first_col // (2 * n * cpb)
    assert lhs_block0 * 2 * n * cpb == first_col
    rhs_col = 0 if rhs_block is None else rhs_block
    ng = len(gather_rows)
    anchors = [] if after is None else [after]

    def body(l_ref, r_ref, *rest):
        small_refs, rest = rest[:ng], rest[ng + len(anchors):]
        kept_ref, sib_ref = rest[:2]
        gathered_refs, rest = rest[2:2 + ng], rest[2 + ng:]
        acc, sendbuf, send_sems, recv_sems = rest[:4]
        j, t = pl.program_id(0), pl.program_id(1)
        first, last = (j == 0) & (t == 0), (j == nj - 1) & (t == nt - 1)
        x, y, c = _mesh_position()
        if collective_id is not None:
            @pl.when(first)
            def _():
                barrier = pltpu.get_barrier_semaphore()
                pl.semaphore_signal(barrier, inc=1, device_id=(x, y, 1 - c), device_id_type=MESH)
                pl.semaphore_wait(barrier, 1)
        if ng:
            gather = _TwoLevelGather(small_refs, gathered_refs, *rest[4:7])

            @pl.when(first)
            def _():
                gather.start()

        def copy(q):
            return pltpu.make_async_remote_copy(
                src_ref=sendbuf.at[q], dst_ref=sib_ref.at[q], send_sem=send_sems.at[q], recv_sem=recv_sems.at[q],
                device_id=(x, y, 1 - c), device_id_type=MESH)

        @pl.when(t == 0)
        def _():
            acc[...] = jnp.zeros_like(acc)

        acc[...] += _dot_tn(l_ref[...], r_ref[...])

        @pl.when(t == nt - 1)
        def _():
            for qq in range(cpb):
                q = j * cpb + qq
                kept_ref[qq] = acc[pl.ds(pl.multiple_of(qq * 2 * n + c * n, 8), n), :]
                sendbuf[q] = acc[pl.ds(pl.multiple_of(qq * 2 * n + (1 - c) * n, 8), n), :].astype(BF16)
                copy(q).start()

        if ng:
            @pl.when((j == nj - 1) & (t == nt // 2))
            def _():
                for a in range(ng):
                    gather.relay(a)

            @pl.when(last)
            def _():
                for a in range(ng):
                    gather.pass_on(a)
                gather.finish()

        @pl.when(last)
        def _():
            for q in range(N_CHIPS):
                copy(q).wait_send()
                copy(q).wait_recv()

    any_spec = pl.BlockSpec(memory_space=pl.ANY)
    return pl.pallas_call(
        body, name=name, grid=(nj, nt),
        in_specs=[pl.BlockSpec((tw, 2 * n * cpb), lambda j, t: (t, lhs_block0 + j)),
                  pl.BlockSpec((tw, m), lambda j, t: (t, rhs_col))] + [any_spec] * (ng + len(anchors)),
        out_specs=[pl.BlockSpec((cpb, n, m), lambda j, t: (j, 0, 0)), any_spec] + [any_spec] * ng,
        out_shape=[jax.ShapeDtypeStruct((N_CHIPS, n, m), F32), jax.ShapeDtypeStruct((N_CHIPS, n, m), BF16)]
                  + [_gathered_shape(s) for s in gather_rows],
        scratch_shapes=[pltpu.VMEM((2 * n * cpb, m), F32), pltpu.VMEM((N_CHIPS, n, m), BF16),
                        pltpu.SemaphoreType.DMA((N_CHIPS,)), pltpu.SemaphoreType.DMA((N_CHIPS,))]
                       + (_TwoLevelGather.scratch(ng) if ng else []),
        compiler_params=pltpu.CompilerParams(dimension_semantics=("arbitrary", "arbitrary"),
                                             vmem_limit_bytes=VMEM_LIMIT_BYTES, collective_id=collective_id),
    )(lhs, rhs, *gather_rows, *anchors)


def _received_shape(p):
    return jax.ShapeDtypeStruct((3,) + p.shape[1:], p.dtype)


ROW_POOL_SCALE, ROW_SLN_G, ROW_SLN_B, ROW_SGU_B = 0, 1, 2, 3
ROW_LN2_G, ROW_LN2_B, ROW_LN1_G, ROW_LN1_B, ROW_LOSS = 8, 9, 10, 11, 12
VEC_ROWS = 16


def _mixer_bwd(dr1, proj, win_t, wout, pool_w, pool_scale, sln_g, sln_b, sgu_w, sgu_b_t, stats, seq):
    tokens = dr1.shape[0]
    tt = min(512, seq)
    tiles_per_seq = seq // tt
    nc = tt // GROUP
    n_halo_blocks = tokens // HALO
    n_tiles = tokens // tt

    def body(dr1_ref, dr1n_ref, proj_ref, projh_ref, win_ref, wout_ref, pw_ref, ps_ref, lg_ref, lb_ref, sw_ref, sb_ref,
             stats_ref, gx_ref, dproj_ref, dmat_ref, dvec_ref, du_scr, dv_scr):
        i = pl.program_id(0)
        tile_in_seq = i % tiles_per_seq

        @pl.when(i == 0)
        def _():
            dmat_ref[...] = jnp.zeros_like(dmat_ref)
            dvec_ref[0:8, :] = jnp.zeros((8, D_MODEL), F32)
            dvec_ref[8:16, :] = stats_ref[...]

        dr1_t = dr1_ref[...]
        dr1b = dr1_t.astype(BF16)
        dmix = _dot_nt(dr1b, wout_ref[...])
        dpo_next = _dot_nt(dr1n_ref[...].astype(BF16), wout_ref[0:POOL_WIDTH, :])
        dpo_next = jnp.where(tile_in_seq == tiles_per_seq - 1, 0.0, dpo_next)
        proj = proj_ref[...]
        xp_prev = jnp.where(tile_in_seq == 0, 0.0, projh_ref[...])
        pos = tile_in_seq * tt + lax.broadcasted_iota(jnp.int32, (tt, 1), 0)
        pos_next = (tile_in_seq + 1) * tt + lax.broadcasted_iota(jnp.int32, (HALO, 1), 0)

        for g, w in enumerate(POOL_WINDOWS):
            sl = slice(g * GROUP, (g + 1) * GROUP)
            inv_cnt = _inv_count(pos, w)
            pwb = pw_ref[g].astype(BF16)
            pooledb = _pool_fwd(proj[:, sl], xp_prev[:, sl], inv_cnt, w).astype(BF16)
            pre = _dot_nn(pooledb, pwb)
            dpo = dmix[:, sl]
            dvec_ref[ROW_POOL_SCALE:ROW_POOL_SCALE + 1, sl] += _rowsum(dpo * pre)
            dsb = (dpo * ps_ref[:, sl]).astype(BF16)
            dmat_ref[g] += _dot_tn(pooledb, dsb)
            dpooled = _dot_nt(dsb, pwb)
            dpooled_next = _dot_nt((dpo_next[:, sl] * ps_ref[:, sl]).astype(BF16), pwb)
            dxp = _pool_bwd(dpooled, dpooled_next, inv_cnt, _inv_count(pos_next, w), w)
            dproj_ref[:, sl] = dxp.astype(BF16)

        zu = proj[:, POOL_WIDTH:POOL_WIDTH + SGU_WIDTH]
        zv = proj[:, POOL_WIDTH + SGU_WIDTH:]
        u = _gelu(zu)
        vhat, rstd_v = _ln_stats(_gelu(zv))
        v_ln = vhat * lg_ref[...] + lb_ref[...]
        dsg = dmix[:, POOL_WIDTH:]
        row = lax.broadcasted_iota(jnp.int32, (GROUP, GROUP), 0)
        col = lax.broadcasted_iota(jnp.int32, (GROUP, GROUP), 1)
        for h in range(N_HEADS):
            ws = _masked_sgu_w(sw_ref, h).astype(BF16)
            vh = _to_head_major(v_ln, h, nc).astype(BF16)
            mixed = _dot_nn(ws, vh) + sb_ref[:, h:h + 1]
            dsg_h = _to_head_major(dsg, h, nc)
            du_h = dsg_h * mixed
            dm_h = dsg_h * _to_head_major(u, h, nc)
            pos_sums = lax.dot_general(jnp.ones((8, nc * GROUP), F32), dm_h, (((1,), (1,)), ((), ())),
                                       precision=lax.Precision.HIGH, preferred_element_type=F32)
            dvec_ref[ROW_SGU_B + h:ROW_SGU_B + h + 1, 0:GROUP] += pos_sums[0:1, :]
            dmb = dm_h.astype(BF16)
            dmat_ref[len(POOL_WINDOWS) + h] += jnp.where(row >= col, _dot_nt(dmb, vh), 0.0)
            dv_h = _dot_tn(ws, dmb)
            for c in range(nc):
                rs = slice(c * GROUP, (c + 1) * GROUP)
                cs = slice(h * GROUP, (h + 1) * GROUP)
                du_scr[rs, cs] = du_h[:, c * GROUP:(c + 1) * GROUP]
                dv_scr[rs, cs] = dv_h[:, c * GROUP:(c + 1) * GROUP]
        dv_ln = dv_scr[...]
        dvec_ref[ROW_SLN_B:ROW_SLN_B + 1, 0:SGU_WIDTH] += _rowsum(dv_ln)
        dvec_ref[ROW_SLN_G:ROW_SLN_G + 1, 0:SGU_WIDTH] += _rowsum(dv_ln * vhat)
        dv = _ln_bwd(dv_ln, vhat, rstd_v, lg_ref[...])
        dproj_ref[:, POOL_WIDTH:POOL_WIDTH + SGU_WIDTH] = (du_scr[...] * _gelu_grad(zu)).astype(BF16)
        dproj_ref[:, POOL_WIDTH + SGU_WIDTH:] = (dv * _gelu_grad(zv)).astype(BF16)
        gx_ref[...] = ALPHA * dr1_t + _dot_nn(dproj_ref[...], win_ref[...])

    def tile(cols):
        return pl.BlockSpec((tt, cols), lambda i: (i, 0))

    def whole(a):
        nd = a.ndim
        return pl.BlockSpec(a.shape, lambda i: (0,) * nd)

    def resident(shape):
        nd = len(shape)
        return pl.BlockSpec(shape, lambda i: (0,) * nd)

    next_halo = pl.BlockSpec((HALO, D_MODEL), lambda i: (jnp.minimum((i + 1) * (tt // HALO), n_halo_blocks - 1), 0))
    prev_halo = pl.BlockSpec((HALO, POOL_WIDTH), lambda i: (jnp.maximum(i * (tt // HALO) - 1, 0), 0))
    consts = [win_t, wout, pool_w, pool_scale, sln_g, sln_b, sgu_w, sgu_b_t, stats]
    small_shapes = [(len(POOL_WINDOWS) + N_HEADS, GROUP, GROUP), (VEC_ROWS, D_MODEL)]
    return pl.pallas_call(
        body, name="mixer_bwd", grid=(n_tiles,),
        in_specs=[tile(D_MODEL), next_halo, tile(IN_COLS), prev_halo] + [whole(a) for a in consts],
        out_specs=[tile(D_MODEL), tile(IN_COLS)] + [resident(s) for s in small_shapes],
        out_shape=[jax.ShapeDtypeStruct((tokens, D_MODEL), F32), jax.ShapeDtypeStruct((tokens, IN_COLS), BF16)]
                  + [jax.ShapeDtypeStruct(s, F32) for s in small_shapes],
        scratch_shapes=[pltpu.VMEM((tt, SGU_WIDTH), F32), pltpu.VMEM((tt, SGU_WIDTH), F32)],
        compiler_params=pltpu.CompilerParams(dimension_semantics=("arbitrary",), vmem_limit_bytes=VMEM_LIMIT_BYTES),
    )(dr1, dr1, proj, proj, *consts)


def _chip_sums(kept, from_sibling, place, name):
    na = len(kept)

    def body(place_ref, *refs):
        kept_refs, sib_refs = refs[:na], refs[na:2 * na]
        bf_refs, own_refs = refs[2 * na:3 * na], refs[3 * na:]
        q = pl.program_id(0)
        for a in range(na):
            s = kept_refs[a][...] + sib_refs[a][...].astype(F32)
            bf_refs[a][...] = s.astype(BF16)

            @pl.when(q == place_ref[1])
            def _(a=a, s=s):
                own_refs[a][...] = s

    by_chip = [pl.BlockSpec((None,) + k.shape[1:], lambda q, pr: (q, 0, 0)) for k in kept]
    grid_spec = pltpu.PrefetchScalarGridSpec(
        num_scalar_prefetch=1, grid=(N_CHIPS,), in_specs=by_chip + by_chip,
        out_specs=by_chip + [pl.BlockSpec(k.shape[1:], lambda q, pr: (0, 0)) for k in kept])
    return pl.pallas_call(
        body, name=name, grid_spec=grid_spec,
        out_shape=[jax.ShapeDtypeStruct(k.shape, BF16) for k in kept]
                  + [jax.ShapeDtypeStruct(k.shape[1:], F32) for k in kept],
        compiler_params=pltpu.CompilerParams(dimension_semantics=("arbitrary",), vmem_limit_bytes=VMEM_LIMIT_BYTES),
    )(place, *kept, *from_sibling)


def _owner_copies(src_ref, land_ref, sems):
    x, y, c = _mesh_position()
    return [pltpu.make_async_remote_copy(
        src_ref=src_ref.at[2 * cx + cy], dst_ref=land_ref.at[j], send_sem=sems[j], recv_sem=sems[3 + j],
        device_id=(cx, cy, c), device_id_type=MESH) for j, (cx, cy) in enumerate(_other_chips(x, y))]


def _send_to_owners_start(chip_partial, name, collective_id):
    land = _received_shape(chip_partial)

    def body(src_ref, land_ref, *rest):
        x, y, c = _mesh_position()
        barrier = pltpu.get_barrier_semaphore()
        for cx, cy in _other_chips(x, y):
            pl.semaphore_signal(barrier, inc=1, device_id=(cx, cy, c), device_id_type=MESH)
        pl.semaphore_wait(barrier, 3)
        for cp in _owner_copies(src_ref, land_ref, rest[:6]):
            cp.start()
        rest[8][...] = jnp.zeros_like(rest[8])

    hbm = pl.BlockSpec(memory_space=pltpu.HBM)
    sem = pl.BlockSpec(memory_space=pltpu.SEMAPHORE)
    outs = pl.pallas_call(
        body, name=name,
        out_shape=[pltpu.SemaphoreType.DMA(())] * 6 + [pltpu.HBM(chip_partial.shape, chip_partial.dtype),
                                                       pltpu.HBM(land.shape, land.dtype),
                                                       jax.ShapeDtypeStruct((8, GROUP), F32)],
        in_specs=[hbm, hbm], out_specs=[sem] * 6 + [hbm, hbm, pl.BlockSpec(memory_space=pltpu.VMEM)],
        input_output_aliases={0: 6, 1: 7},
        compiler_params=pltpu.CompilerParams(has_side_effects=pltpu.SideEffectType.DATAFLOW_SIDE_EFFECTING,
                                             collective_id=collective_id),
    )(pltpu.with_memory_space_constraint(chip_partial, pltpu.HBM),
      pltpu.with_memory_space_constraint(lax.empty(land.shape, land.dtype), pltpu.HBM))
    return outs[:6], outs[6], outs[7], outs[8]


def _send_to_owners_wait(sems, src_thru, land_thru, after, name):
    def body(src_ref, land_ref, *rest):
        for cp in _owner_copies(src_ref, land_ref, rest[:6]):
            cp.wait_send()
            cp.wait_recv()

    hbm = pl.BlockSpec(memory_space=pltpu.HBM)
    sem = pl.BlockSpec(memory_space=pltpu.SEMAPHORE)
    return pl.pallas_call(
        body, name=name,
        out_shape=[pltpu.HBM(src_thru.shape, src_thru.dtype), pltpu.HBM(land_thru.shape, land_thru.dtype)],
        in_specs=[hbm, hbm] + [sem] * 6 + [pl.BlockSpec(memory_space=pl.ANY)], out_specs=[hbm, hbm],
        input_output_aliases={0: 0, 1: 1},
        compiler_params=pltpu.CompilerParams(has_side_effects=pltpu.SideEffectType.DATAFLOW_SIDE_EFFECTING),
    )(src_thru, land_thru, *sems, after)[1]


def _adamw_math(w, g, m, v):
    m = ADAM_B1 * m + (1.0 - ADAM_B1) * g
    v = ADAM_B2 * v + (1.0 - ADAM_B2) * (g * g)
    m_hat = m / (1.0 - ADAM_B1 ** ADAM_STEP)
    v_hat = v / (1.0 - ADAM_B2 ** ADAM_STEP)
    delta = -ADAM_LR * (m_hat / (jnp.sqrt(v_hat) + ADAM_EPS) + ADAM_WD * w)
    return delta, m, v


def _sum_adamw(w, m, v, own, received, name):
    rows, cols = w.shape
    rb = _row_block(rows, 256)

    def body(w_ref, m_ref, v_ref, own_ref, rec_ref, g_ref, d_ref, mo_ref, vo_ref):
        g = own_ref[...]
        for j in range(3):
            g = g + rec_ref[j].astype(F32)
        g_ref[...] = g
        d_ref[...], mo_ref[...], vo_ref[...] = _adamw_math(w_ref[...], g, m_ref[...], v_ref[...])

    spec = pl.BlockSpec((rb, cols), lambda r: (r, 0))
    return pl.pallas_call(
        body, name=name, grid=(rows // rb,),
        in_specs=[spec] * 4 + [pl.BlockSpec((3, rb, cols), lambda r: (0, r, 0))],
        out_specs=[spec] * 4, out_shape=[jax.ShapeDtypeStruct((rows, cols), F32)] * 4,
        compiler_params=pltpu.CompilerParams(dimension_semantics=("arbitrary",)),
    )(w, m, v, own, received)


SMALL_NAMES = ("pool_w", "sgu_w", "pool_scale", "sgu_ln_g", "sgu_ln_b", "sgu_b", "ln1_g", "ln1_b", "ln2_g", "ln2_b")
_SMALL_VEC_ROWS = {"pool_scale": (ROW_POOL_SCALE, POOL_WIDTH), "sgu_ln_g": (ROW_SLN_G, SGU_WIDTH),
                   "sgu_ln_b": (ROW_SLN_B, SGU_WIDTH), "ln1_g": (ROW_LN1_G, D_MODEL), "ln1_b": (ROW_LN1_B, D_MODEL),
                   "ln2_g": (ROW_LN2_G, D_MODEL), "ln2_b": (ROW_LN2_B, D_MODEL)}
_SMALL_MAT_FIRST = {"pool_w": 0, "sgu_w": len(POOL_WINDOWS)}


def _small_sum_adamw(mats_all, vecs_all, w, m, v):
    n = len(SMALL_NAMES)

    def body(mats_ref, vecs_ref, *refs):
        w_refs, m_refs, v_refs = refs[:n], refs[n:2 * n], refs[2 * n:3 * n]
        loss_ref = refs[3 * n]
        g_refs, d_refs, mo_refs, vo_refs = (refs[3 * n + 1 + k * n:3 * n + 1 + (k + 1) * n] for k in range(4))
        vec_scr = refs[7 * n + 1]

        def update(k, idx, g):
            d, mo, vo = _adamw_math(w_refs[k][idx], g, m_refs[k][idx], v_refs[k][idx])
            g_refs[k][idx], d_refs[k][idx], mo_refs[k][idx], vo_refs[k][idx] = g, d, mo, vo

        total = vecs_ref[0]
        for dev in range(1, N_DEV):
            total = total + vecs_ref[dev]
        vec_scr[...] = total
        for k, name in enumerate(SMALL_NAMES):
            if name in _SMALL_MAT_FIRST:
                for b in range(4):
                    g = mats_ref[0, _SMALL_MAT_FIRST[name] + b]
                    for dev in range(1, N_DEV):
                        g = g + mats_ref[dev, _SMALL_MAT_FIRST[name] + b]
                    update(k, (0, b), g)
            elif name == "sgu_b":
                update(k, (0,), vec_scr[ROW_SGU_B:ROW_SGU_B + N_HEADS, 0:GROUP])
            else:
                row, width = _SMALL_VEC_ROWS[name]
                update(k, (slice(None), slice(None)), vec_scr[row:row + 1, 0:width])
        loss = jnp.sum(vec_scr[ROW_LOSS:ROW_LOSS + 1, :], axis=1, keepdims=True) * (0.5 / D_MODEL)
        loss_ref[...] = jnp.broadcast_to(loss, loss_ref.shape)

    vmem = pl.BlockSpec(memory_space=pltpu.VMEM)
    shapes = [jax.ShapeDtypeStruct(w[k].shape, F32) for k in SMALL_NAMES]
    outs = pl.pallas_call(
        body, name="small_sum_adamw",
        out_shape=[jax.ShapeDtypeStruct((8, GROUP), F32)] + shapes * 4,
        in_specs=[vmem] * (2 + 3 * n), out_specs=[vmem] * (1 + 4 * n),
        scratch_shapes=[pltpu.VMEM((VEC_ROWS, D_MODEL), F32)],
    )(mats_all, vecs_all, *[w[k] for k in SMALL_NAMES], *[m[k] for k in SMALL_NAMES], *[v[k] for k in SMALL_NAMES])
    return outs[0], outs[1:1 + n], outs[1 + n:1 + 2 * n], outs[1 + 2 * n:1 + 3 * n], outs[1 + 3 * n:]


def kernel(x, w_in, pool_w, pool_scale, sgu_ln_g, sgu_ln_b, sgu_w, sgu_b, w_out, ln1_g, ln1_b, w_gate_up, w_down, ln2_g, ln2_b, loss_target, m_w_in, m_pool_w, m_pool_scale, m_sgu_ln_g, m_sgu_ln_b, m_sgu_w, m_sgu_b, m_w_out, m_ln1_g, m_ln1_b, m_w_gate_up, m_w_down, m_ln2_g, m_ln2_b, v_w_in, v_pool_w, v_pool_scale, v_sgu_ln_g, v_sgu_ln_b, v_sgu_w, v_sgu_b, v_w_out, v_ln1_g, v_ln1_b, v_w_gate_up, v_w_down, v_ln2_g, v_ln2_b):
    bl, seq, _ = x.shape
    tokens = bl * seq
    x2d = x.reshape(tokens, D_MODEL)
    tgt2d = loss_target.reshape(tokens, D_MODEL)
    my_c = lax.axis_index("c")
    place = jnp.stack([my_c, 2 * lax.axis_index("x") + lax.axis_index("y")]).astype(jnp.int32)

    win_t, wout = _all_gather_rows([w_in[0].T, w_out[0]], "weight_all_gather")

    pool_w3, sgu_w3 = pool_w[0], sgu_w[0]
    sgu_b_t = sgu_b[0].T
    proj, xhat1, rstd1, mix_bf, x_bf, wgu_t, wdown = _mixer_fwd(
        x2d, win_t, wout, pool_w3, pool_scale, sgu_ln_g, sgu_ln_b, sgu_w3, sgu_b_t, ln1_g, ln1_b,
        [w_gate_up[0].T, w_down[0]], seq)
    dr1, acts_bf, grads_bf, stats = _ffn_fwd_bwd(xhat1, rstd1, tgt2d, wgu_t, wdown, ln1_g, ln1_b, ln2_g, ln2_b)

    kept_gu, sib_gu = _wgrad_exchange(grads_bf, acts_bf, 1, "wgrad_gate_up", collective_id=1,
                                      lhs_cols=(0, 2 * D_FF), rhs_block=0)
    bf_gu, own_gu = _chip_sums([kept_gu], [sib_gu], place, "chip_sum_w_gate_up")
    sent_gu = _send_to_owners_start(bf_gu, "grad_scatter_w_gate_up_start", collective_id=4)
    kept_dn, sib_dn = _wgrad_exchange(grads_bf, acts_bf, 2, "wgrad_down", after=sent_gu[3], collective_id=2,
                                      lhs_cols=(2 * D_FF, D_FF), rhs_block=1)
    bf_dn, own_dn = _chip_sums([kept_dn], [sib_dn], place, "chip_sum_w_down")
    sent_dn = _send_to_owners_start(bf_dn, "grad_scatter_w_down_start", collective_id=5)
    kept_out, sib_out = _wgrad_exchange(mix_bf, acts_bf, N_CHIPS, "wgrad_out", after=sent_dn[3], collective_id=3,
                                        rhs_block=2)
    bf_out, own_out = _chip_sums([kept_out], [sib_out], place, "chip_sum_w_out")
    sent_out = _send_to_owners_start(bf_out, "grad_scatter_w_out_start", collective_id=6)
    gx, dproj_bf, d_mats, d_vecs = _mixer_bwd(
        dr1, proj, win_t, wout, pool_w3, pool_scale, sgu_ln_g, sgu_ln_b, sgu_w3, sgu_b_t, stats, seq)
    kept_in, sib_in, mats_all, vecs_all = _wgrad_exchange(
        dproj_bf, x_bf, N_CHIPS, "wgrad_in", gather_rows=[d_mats.reshape(-1, GROUP), d_vecs], after=sent_out[3])
    bf_in, own_in = _chip_sums([kept_in], [sib_in], place, "chip_sum_w_in")
    sent_in = _send_to_owners_start(bf_in, "grad_scatter_w_in_start", collective_id=7)

    grads, deltas, new_m, new_v = {}, {}, {}, {}
    after = sent_in[3]
    for nm, w, m, v, own, sent, transposed in (("w_gate_up", w_gate_up, m_w_gate_up, v_w_gate_up, own_gu, sent_gu, True),
                                               ("w_down", w_down, m_w_down, v_w_down, own_dn, sent_dn, False),
                                               ("w_out", w_out, m_w_out, v_w_out, own_out, sent_out, False),
                                               ("w_in", w_in, m_w_in, v_w_in, own_in, sent_in, True)):
        rows = (lambda a: a[0].T) if transposed else (lambda a: a[0])
        back = (lambda a: a.T[None]) if transposed else (lambda a: a[None])
        rec = _send_to_owners_wait(*sent[:3], after, "grad_scatter_" + nm + "_wait")
        g, d, mo, vo = _sum_adamw(rows(w), rows(m), rows(v), own, rec, "adamw_" + nm)
        after = vo
        grads[nm], deltas[nm], new_m[nm], new_v[nm] = back(g), back(d), back(mo), back(vo)

    small_w = {"pool_w": pool_w, "pool_scale": pool_scale, "sgu_ln_g": sgu_ln_g, "sgu_ln_b": sgu_ln_b, "sgu_w": sgu_w,
               "sgu_b": sgu_b, "ln1_g": ln1_g, "ln1_b": ln1_b, "ln2_g": ln2_g, "ln2_b": ln2_b}
    small_m = {"pool_w": m_pool_w, "pool_scale": m_pool_scale, "sgu_ln_g": m_sgu_ln_g, "sgu_ln_b": m_sgu_ln_b,
               "sgu_w": m_sgu_w, "sgu_b": m_sgu_b, "ln1_g": m_ln1_g, "ln1_b": m_ln1_b, "ln2_g": m_ln2_g, "ln2_b": m_ln2_b}
    small_v = {"pool_w": v_pool_w, "pool_scale": v_pool_scale, "sgu_ln_g": v_sgu_ln_g, "sgu_ln_b": v_sgu_ln_b,
               "sgu_w": v_sgu_w, "sgu_b": v_sgu_b, "ln1_g": v_ln1_g, "ln1_b": v_ln1_b, "ln2_g": v_ln2_g, "ln2_b": v_ln2_b}
    loss_blk, g_small, d_small, m_small, v_small = _small_sum_adamw(
        mats_all.reshape(N_DEV, 2 * N_HEADS, GROUP, GROUP), vecs_all.reshape(N_DEV, VEC_ROWS, D_MODEL),
        small_w, small_m, small_v)
    for vals, dst in ((g_small, grads), (d_small, deltas), (m_small, new_m), (v_small, new_v)):
        dst.update(zip(SMALL_NAMES, vals))

    order = ["w_in", "pool_w", "pool_scale", "sgu_ln_g", "sgu_ln_b", "sgu_w", "sgu_b", "w_out", "ln1_g", "ln1_b",
             "w_gate_up", "w_down", "ln2_g", "ln2_b"]
    return (loss_blk[0, 0], gx.reshape(bl, seq, D_MODEL), *[grads[k] for k in order], *[deltas[k] for k in order],
            *[new_m[k] for k in order], *[new_v[k] for k in order])
```

```python
import jax
import jax.numpy as jnp
from jax import lax
from jax.experimental import pallas as pl
from jax.experimental.pallas import tpu as pltpu

F32 = jnp.float32
BF16 = jnp.bfloat16
MESH = pl.DeviceIdType.MESH

D_MODEL = 1024
POOL_WIDTH = 512
SGU_WIDTH = 512
POOL_WINDOWS = (2, 4, 8, 16)
GROUP = 128
N_HEADS = 4
IN_COLS = POOL_WIDTH + 2 * SGU_WIDTH
D_FF = 2816
LN_EPS = 1e-5
ALPHA = float(2.0 ** 0.25)
HALO = 16
N_DEV = 8
N_CHIPS = 4

ADAM_LR = 0.001
ADAM_B1 = 0.9
ADAM_B2 = 0.999
ADAM_EPS = 1e-08
ADAM_WD = 0.01
ADAM_STEP = 10

VMEM_LIMIT_BYTES = 56 * 1024 * 1024

_SQRT_HALF = 0.7071067811865476
_INV_SQRT_2PI = 0.3989422804014327


def _dot_nn(a, b):
    return lax.dot_general(a, b, (((1,), (0,)), ((), ())), preferred_element_type=F32)


def _dot_nt(a, b):
    return lax.dot_general(a, b, (((1,), (1,)), ((), ())), preferred_element_type=F32)


def _dot_tn(a, b):
    return lax.dot_general(a, b, (((0,), (0,)), ((), ())), preferred_element_type=F32)


def _gelu(x):
    return 0.5 * x * (1.0 + lax.erf(x * _SQRT_HALF))


def _gelu_grad(x):
    return 0.5 * (1.0 + lax.erf(x * _SQRT_HALF)) + x * jnp.exp(-0.5 * x * x) * _INV_SQRT_2PI


def _ln_stats(r):
    mu = jnp.mean(r, axis=-1, keepdims=True)
    d = r - mu
    var = jnp.mean(d * d, axis=-1, keepdims=True)
    rstd = lax.rsqrt(var + LN_EPS)
    return d * rstd, rstd


def _ln_bwd(dout, xhat, rstd, g):
    dxh = dout * g
    m1 = jnp.mean(dxh, axis=-1, keepdims=True)
    m2 = jnp.mean(dxh * xhat, axis=-1, keepdims=True)
    return rstd * (dxh - m1 - xhat * m2)


def _rowsum(a):
    return jnp.sum(a, axis=0, keepdims=True)


def _pool_fwd(xp, xp_prev, inv_cnt, w):
    s = jnp.concatenate([xp_prev, xp], axis=0)
    k = 1
    while k < w:
        s = s + pltpu.roll(s, k, 0)
        k *= 2
    return s[HALO:, :] * inv_cnt - xp


def _pool_bwd(dpooled, dpooled_next, inv_cnt, inv_cnt_next, w):
    n = dpooled.shape[0] + HALO
    s = jnp.concatenate([dpooled * inv_cnt, dpooled_next * inv_cnt_next], axis=0)
    k = 1
    while k < w:
        s = s + pltpu.roll(s, n - k, 0)
        k *= 2
    return s[: dpooled.shape[0], :] - dpooled


def _inv_count(pos, w):
    return 1.0 / jnp.minimum(pos + 1, w).astype(F32)


def _to_head_major(a, h, nc):
    return jnp.concatenate(
        [a[c * GROUP:(c + 1) * GROUP, h * GROUP:(h + 1) * GROUP] for c in range(nc)], axis=1)


def _masked_sgu_w(sw_ref, h):
    row = lax.broadcasted_iota(jnp.int32, (GROUP, GROUP), 0)
    col = lax.broadcasted_iota(jnp.int32, (GROUP, GROUP), 1)
    return jnp.where(row >= col, sw_ref[h], 0.0)


def _row_block(rows, limit):
    return max(b for b in range(16, min(rows, limit) + 1, 16) if rows % b == 0)


def _mesh_position():
    return lax.axis_index("x"), lax.axis_index("y"), lax.axis_index("c")


def _other_chips(x, y):
    return [(1 - x, y), (x, 1 - y), (1 - x, 1 - y)]


class _TwoLevelGather:
    def __init__(self, ins, outs, send_sems, recv_sems, local_sems):
        self.ins, self.outs = ins, outs
        self.send_sems, self.recv_sems, self.local_sems = send_sems, recv_sems, local_sems
        self.na = len(ins)
        x, y, c = _mesh_position()
        self.c = c
        self.me, self.sibling = (x, y, c), (x, y, 1 - c)
        self.chips = _other_chips(x, y)
        self.relay_from = (x + (1 - c) * (1 - 2 * x), y + c * (1 - 2 * y))
        self.relay_to = (x + c * (1 - 2 * x), y + (1 - c) * (1 - 2 * y))

    def _rows(self, a, px, py, pc):
        n = self.ins[a].shape[0]
        return self.outs[a].at[pl.ds((4 * px + 2 * py + pc) * n, n), :]

    def _copy(self, a, k, block, to, src=None):
        return pltpu.make_async_remote_copy(
            src_ref=self._rows(a, *block) if src is None else src, dst_ref=self._rows(a, *block),
            send_sem=self.send_sems.at[a * 7 + k], recv_sem=self.recv_sems.at[a * 7 + k],
            device_id=to, device_id_type=MESH)

    def _mine(self, a):
        return pltpu.make_async_copy(self.ins[a], self._rows(a, *self.me), self.local_sems.at[a])

    def start(self):
        for a in range(self.na):
            self._mine(a).start()
        for a in range(self.na):
            self._copy(a, 0, self.me, self.sibling, src=self.ins[a]).start()
            for j, chip in enumerate(self.chips[:2]):
                self._copy(a, 1 + j, self.me, (*chip, self.c), src=self.ins[a]).start()

    def relay(self, a):
        c, block = self.c, (*self.relay_from, self.c)
        self._copy(a, 1 + c, block, self.me).wait_recv()
        self._copy(a, 3, block, (*self.relay_to, c)).start()
        self._copy(a, 4 + c, block, self.sibling).start()

    def pass_on(self, a):
        c = self.c
        self._copy(a, 2 - c, (*self.relay_to, c), self.me).wait_recv()
        self._copy(a, 5 - c, (*self.relay_to, c), self.sibling).start()
        self._copy(a, 3, (*self.chips[2], c), self.me).wait_recv()
        self._copy(a, 6, (*self.chips[2], c), self.sibling).start()

    def finish(self):
        for a in range(self.na):
            self._copy(a, 0, self.sibling, self.me).wait_recv()
            for j, chip in enumerate(self.chips):
                self._copy(a, 4 + j, (*chip, 1 - self.c), self.me).wait_recv()
        for a in range(self.na):
            for k in range(7):
                self._copy(a, k, self.me, self.sibling, src=self.ins[a]).wait_send()
            self._mine(a).wait()

    @staticmethod
    def scratch(na):
        return [pltpu.SemaphoreType.DMA((7 * na,)), pltpu.SemaphoreType.DMA((7 * na,)), pltpu.SemaphoreType.DMA((na,))]


def _gathered_shape(s):
    return jax.ShapeDtypeStruct((N_DEV * s.shape[0], s.shape[1]), s.dtype)


def _gathered_bf16(s):
    return jax.ShapeDtypeStruct((N_DEV * s.shape[0], s.shape[1]), BF16)


def _all_gather_rows(shards, name):
    na = len(shards)

    def body(*refs):
        bf_refs = refs[2 * na:3 * na]
        for a in range(na):
            bf_refs[a][...] = refs[a][...].astype(BF16)
        gather = _TwoLevelGather(bf_refs, refs[na:2 * na], *refs[3 * na:])
        gather.start()
        for a in range(na):
            gather.relay(a)
        for a in range(na):
            gather.pass_on(a)
        gather.finish()

    return pl.pallas_call(
        body, name=name, out_shape=[_gathered_bf16(s) for s in shards],
        in_specs=[pl.BlockSpec(memory_space=pltpu.VMEM)] * na, out_specs=[pl.BlockSpec(memory_space=pl.ANY)] * na,
        scratch_shapes=[pltpu.VMEM(s.shape, BF16) for s in shards] + _TwoLevelGather.scratch(na),
    )(*shards)


def _mixer_fwd(x2d, win_t, wout, pool_w, pool_scale, sln_g, sln_b, sgu_w, sgu_b_t, ln1_g, ln1_b, later_shards, seq):
    tokens = x2d.shape[0]
    tt = min(512, seq)
    tiles_per_seq = seq // tt
    nc = tt // GROUP
    n_tiles = tokens // tt
    n_later = len(later_shards)

    def body(x_ref, xh_ref, win_ref, wout_ref, pw_ref, ps_ref, lg_ref, lb_ref, sw_ref, sb_ref, g1_ref, b1_ref, *rest):
        shard_refs, rest = rest[:n_later], rest[n_later:]
        proj_ref, xhat_ref, rstd_ref, mix_ref, xbf_ref = rest[:5]
        gathered_refs, rest = rest[5:5 + n_later], rest[5 + n_later:]
        mix_scr, shard_bf_refs, (send_sems, recv_sems, local_sems) = rest[0], rest[1:1 + n_later], rest[1 + n_later:]
        i = pl.program_id(0)
        gather = _TwoLevelGather(shard_bf_refs, gathered_refs, send_sems, recv_sems, local_sems)

        @pl.when(i == 0)
        def _():
            for a in range(n_later):
                shard_bf_refs[a][...] = shard_refs[a][...].astype(BF16)
            gather.start()

        tile_in_seq = i % tiles_per_seq
        x = x_ref[...]
        xb = x.astype(BF16)
        xbf_ref[...] = xb
        proj = _dot_nt(xb, win_ref[...])
        proj_ref[...] = proj
        xp_prev = _dot_nt(xh_ref[...].astype(BF16), win_ref[0:POOL_WIDTH, :])
        xp_prev = jnp.where(tile_in_seq == 0, 0.0, xp_prev)
        pos = tile_in_seq * tt + lax.broadcasted_iota(jnp.int32, (tt, 1), 0)
        for g, w in enumerate(POOL_WINDOWS):
            sl = slice(g * GROUP, (g + 1) * GROUP)
            pooled = _pool_fwd(proj[:, sl], xp_prev[:, sl], _inv_count(pos, w), w)
            pre = _dot_nn(pooled.astype(BF16), pw_ref[g].astype(BF16))
            mix_scr[:, sl] = pre * ps_ref[:, sl]
        u = _gelu(proj[:, POOL_WIDTH:POOL_WIDTH + SGU_WIDTH])
        v = _gelu(proj[:, POOL_WIDTH + SGU_WIDTH:])
        vhat, _ = _ln_stats(v)
        v_ln = vhat * lg_ref[...] + lb_ref[...]
        for h in range(N_HEADS):
            ws = _masked_sgu_w(sw_ref, h).astype(BF16)
            mixed = _dot_nn(ws, _to_head_major(v_ln, h, nc).astype(BF16)) + sb_ref[:, h:h + 1]
            for c in range(nc):
                rs = slice(c * GROUP, (c + 1) * GROUP)
                mix_scr[rs, POOL_WIDTH + h * GROUP:POOL_WIDTH + (h + 1) * GROUP] = (
                    u[rs, h * GROUP:(h + 1) * GROUP] * mixed[:, c * GROUP:(c + 1) * GROUP])
        mixb = mix_scr[...].astype(BF16)
        mix_ref[...] = mixb
        r1 = ALPHA * x + _dot_nn(mixb, wout_ref[...])
        xhat, rstd = _ln_stats(r1)
        xhat_ref[...] = xhat
        rstd_ref[...] = rstd

        for a in range(n_later):
            relay_tile = min(n_tiles // 2 + a, n_tiles - 1)

            @pl.when(i == relay_tile)
            def _(a=a):
                gather.relay(a)

            @pl.when(i == max(n_tiles - n_later + a, relay_tile))
            def _(a=a):
                gather.pass_on(a)

        @pl.when(i == n_tiles - 1)
        def _():
            gather.finish()

    def tile(cols):
        return pl.BlockSpec((tt, cols), lambda i: (i, 0))

    def whole(a):
        nd = a.ndim
        return pl.BlockSpec(a.shape, lambda i: (0,) * nd)

    any_spec = pl.BlockSpec(memory_space=pl.ANY)
    halo = pl.BlockSpec((HALO, D_MODEL), lambda i: (jnp.maximum(i * (tt // HALO) - 1, 0), 0))
    consts = [win_t, wout, pool_w, pool_scale, sln_g, sln_b, sgu_w, sgu_b_t, ln1_g, ln1_b]
    return pl.pallas_call(
        body, name="mixer_fwd", grid=(n_tiles,),
        in_specs=[tile(D_MODEL), halo] + [whole(a) for a in consts] + [whole(s) for s in later_shards],
        out_specs=[tile(IN_COLS), tile(D_MODEL), tile(1), tile(D_MODEL), tile(D_MODEL)] + [any_spec] * n_later,
        out_shape=[jax.ShapeDtypeStruct((tokens, IN_COLS), F32), jax.ShapeDtypeStruct((tokens, D_MODEL), F32),
                   jax.ShapeDtypeStruct((tokens, 1), F32), jax.ShapeDtypeStruct((tokens, D_MODEL), BF16),
                   jax.ShapeDtypeStruct((tokens, D_MODEL), BF16)] + [_gathered_bf16(s) for s in later_shards],
        scratch_shapes=[pltpu.VMEM((tt, D_MODEL), F32)] + [pltpu.VMEM(s.shape, BF16) for s in later_shards]
                       + _TwoLevelGather.scratch(n_later),
        compiler_params=pltpu.CompilerParams(dimension_semantics=("arbitrary",), vmem_limit_bytes=VMEM_LIMIT_BYTES),
    )(x2d, x2d, *consts, *later_shards)


def _ffn_fwd_bwd(xhat1, rstd1, target, wgu_t, wdown, ln1_g, ln1_b, ln2_g, ln2_b):
    tokens = xhat1.shape[0]
    tt = min(256, tokens)

    def body(xhat_ref, rstd_ref, tgt_ref, wgu_hbm, wd_hbm, g1_ref, b1_ref, g2_ref, b2_ref,
             dr1_ref, acts_ref, grads_ref, stats_ref, wgu_ref, wd_ref, gu_scr, sems):
        i = pl.program_id(0)

        @pl.when(i == 0)
        def _():
            loads = [pltpu.make_async_copy(wgu_hbm, wgu_ref, sems.at[0]),
                     pltpu.make_async_copy(wd_hbm, wd_ref, sems.at[1])]
            for cp in loads:
                cp.start()
            stats_ref[...] = jnp.zeros_like(stats_ref)
            for cp in loads:
                cp.wait()

        xhat1_t = xhat_ref[...]
        h = xhat1_t * g1_ref[...] + b1_ref[...]
        hb = h.astype(BF16)
        acts_ref[:, 0:D_MODEL] = hb
        gate = _dot_nt(hb, wgu_ref[0:D_FF, :])
        up = _dot_nt(hb, wgu_ref[D_FF:, :])
        gu_scr[:, 0:D_FF] = gate
        gu_scr[:, D_FF:] = up
        ab = (gate * jax.nn.sigmoid(gate) * up).astype(BF16)
        grads_ref[:, 2 * D_FF:] = ab
        xhat2, rstd2 = _ln_stats(ALPHA * h + _dot_nn(ab, wd_ref[...]))
        err = xhat2 * g2_ref[...] + b2_ref[...] - tgt_ref[...]
        dy = err * (1.0 / D_MODEL)
        stats_ref[0:1, :] += _rowsum(dy * xhat2)
        stats_ref[1:2, :] += _rowsum(dy)
        stats_ref[4:5, :] += _rowsum(err * err)
        dr2 = _ln_bwd(dy, xhat2, rstd2, g2_ref[...])
        dr2b = dr2.astype(BF16)
        acts_ref[:, D_MODEL:2 * D_MODEL] = dr2b
        da = _dot_nt(dr2b, wd_ref[...])
        gate = gu_scr[:, 0:D_FF]
        up = gu_scr[:, D_FF:]
        sg = jax.nn.sigmoid(gate)
        dgate = (da * up * (sg * (1.0 + gate * (1.0 - sg)))).astype(BF16)
        dup = (da * (gate * sg)).astype(BF16)
        grads_ref[:, 0:D_FF] = dgate
        grads_ref[:, D_FF:2 * D_FF] = dup
        dh = ALPHA * dr2 + _dot_nn(dgate, wgu_ref[0:D_FF, :]) + _dot_nn(dup, wgu_ref[D_FF:, :])
        stats_ref[2:3, :] += _rowsum(dh * xhat1_t)
        stats_ref[3:4, :] += _rowsum(dh)
        dr1 = _ln_bwd(dh, xhat1_t, rstd_ref[...], g1_ref[...])
        dr1_ref[...] = dr1
        acts_ref[:, 2 * D_MODEL:] = dr1.astype(BF16)

    def tile(cols):
        return pl.BlockSpec((tt, cols), lambda i: (i, 0))

    def whole(a):
        nd = a.ndim
        return pl.BlockSpec(a.shape, lambda i: (0,) * nd)

    any_spec = pl.BlockSpec(memory_space=pl.ANY)
    vecs = [ln1_g, ln1_b, ln2_g, ln2_b]
    return pl.pallas_call(
        body, name="ffn_fwd_bwd", grid=(tokens // tt,),
        in_specs=[tile(D_MODEL), tile(1), tile(D_MODEL), any_spec, any_spec] + [whole(a) for a in vecs],
        out_specs=[tile(D_MODEL), tile(3 * D_MODEL), tile(3 * D_FF), pl.BlockSpec((8, D_MODEL), lambda i: (0, 0))],
        out_shape=[jax.ShapeDtypeStruct((tokens, D_MODEL), F32), jax.ShapeDtypeStruct((tokens, 3 * D_MODEL), BF16),
                   jax.ShapeDtypeStruct((tokens, 3 * D_FF), BF16), jax.ShapeDtypeStruct((8, D_MODEL), F32)],
        scratch_shapes=[pltpu.VMEM(wgu_t.shape, BF16), pltpu.VMEM(wdown.shape, BF16),
                        pltpu.VMEM((tt, 2 * D_FF), F32), pltpu.SemaphoreType.DMA((2,))],
        compiler_params=pltpu.CompilerParams(dimension_semantics=("arbitrary",), vmem_limit_bytes=VMEM_LIMIT_BYTES),
    )(xhat1, rstd1, target, wgu_t, wdown, *vecs)


def _wgrad_exchange(lhs, rhs, chips_per_block, name, gather_rows=(), after=None, collective_id=None,
                    lhs_cols=None, rhs_block=None, token_tile=2048):
    assert collective_id is None or not gather_rows
    tokens = lhs.shape[0]
    first_col, n_all = (0, lhs.shape[1]) if lhs_cols is None else lhs_cols
    m = rhs.shape[1] if rhs_block is None else D_MODEL
    n = n_all // N_DEV
    tw = min(token_tile, tokens)
    nt = tokens // tw
    cpb = chips_per_block
    nj = N_CHIPS // cpb
    lhs_block0 = first_col // (2 * n * cpb)
    assert lhs_block0 * 2 * n * cpb == first_col
    rhs_col = 0 if rhs_block is None else rhs_block
    ng = len(gather_rows)
    anchors = [] if after is None else [after]

    def body(l_ref, r_ref, *rest):
        small_refs, rest = rest[:ng], rest[ng + len(anchors):]
        kept_ref, sib_ref = rest[:2]
        gathered_refs, rest = rest[2:2 + ng], rest[2 + ng:]
        acc, sendbuf, send_sems, recv_sems = rest[:4]
        j, t = pl.program_id(0), pl.program_id(1)
        first, last = (j == 0) & (t == 0), (j == nj - 1) & (t == nt - 1)
        x, y, c = _mesh_position()
        if collective_id is not None:
            @pl.when(first)
            def _():
                barrier = pltpu.get_barrier_semaphore()
                pl.semaphore_signal(barrier, inc=1, device_id=(x, y, 1 - c), device_id_type=MESH)
                pl.semaphore_wait(barrier, 1)
        if ng:
            gather = _TwoLevelGather(small_refs, gathered_refs, *rest[4:7])

            @pl.when(first)
            def _():
                gather.start()

        def copy(q):
            return pltpu.make_async_remote_copy(
                src_ref=sendbuf.at[q], dst_ref=sib_ref.at[q], send_sem=send_sems.at[q], recv_sem=recv_sems.at[q],
                device_id=(x, y, 1 - c), device_id_type=MESH)

        @pl.when(t == 0)
        def _():
            acc[...] = jnp.zeros_like(acc)

        acc[...] += _dot_tn(l_ref[...], r_ref[...])

        @pl.when(t == nt - 1)
        def _():
            for qq in range(cpb):
                q = j * cpb + qq
                kept_ref[qq] = acc[pl.ds(pl.multiple_of(qq * 2 * n + c * n, 8), n), :]
                sendbuf[q] = acc[pl.ds(pl.multiple_of(qq * 2 * n + (1 - c) * n, 8), n), :].astype(BF16)
                copy(q).start()

        if ng:
            @pl.when((j == nj - 1) & (t == nt // 2))
            def _():
                for a in range(ng):
                    gather.relay(a)

            @pl.when(last)
            def _():
                for a in range(ng):
                    gather.pass_on(a)
                gather.finish()

        @pl.when(last)
        def _():
            for q in range(N_CHIPS):
                copy(q).wait_send()
                copy(q).wait_recv()

    any_spec = pl.BlockSpec(memory_space=pl.ANY)
    return pl.pallas_call(
        body, name=name, grid=(nj, nt),
        in_specs=[pl.BlockSpec((tw, 2 * n * cpb), lambda j, t: (t, lhs_block0 + j)),
                  pl.BlockSpec((tw, m), lambda j, t: (t, rhs_col))] + [any_spec] * (ng + len(anchors)),
        out_specs=[pl.BlockSpec((cpb, n, m), lambda j, t: (j, 0, 0)), any_spec] + [any_spec] * ng,
        out_shape=[jax.ShapeDtypeStruct((N_CHIPS, n, m), F32), jax.ShapeDtypeStruct((N_CHIPS, n, m), BF16)]
                  + [_gathered_shape(s) for s in gather_rows],
        scratch_shapes=[pltpu.VMEM((2 * n * cpb, m), F32), pltpu.VMEM((N_CHIPS, n, m), BF16),
                        pltpu.SemaphoreType.DMA((N_CHIPS,)), pltpu.SemaphoreType.DMA((N_CHIPS,))]
                       + (_TwoLevelGather.scratch(ng) if ng else []),
        compiler_params=pltpu.CompilerParams(dimension_semantics=("arbitrary", "arbitrary"),
                                             vmem_limit_bytes=VMEM_LIMIT_BYTES, collective_id=collective_id),
    )(lhs, rhs, *gather_rows, *anchors)


def _received_shape(p):
    return jax.ShapeDtypeStruct((3,) + p.shape[1:], p.dtype)


ROW_POOL_SCALE, ROW_SLN_G, ROW_SLN_B, ROW_SGU_B = 0, 1, 2, 3
ROW_LN2_G, ROW_LN2_B, ROW_LN1_G, ROW_LN1_B, ROW_LOSS = 8, 9, 10, 11, 12
VEC_ROWS = 16


def _mixer_bwd(dr1, proj, win_t, wout, pool_w, pool_scale, sln_g, sln_b, sgu_w, sgu_b_t, stats, seq):
    tokens = dr1.shape[0]
    tt = min(512, seq)
    tiles_per_seq = seq // tt
    nc = tt // GROUP
    n_halo_blocks = tokens // HALO
    n_tiles = tokens // tt

    def body(dr1_ref, dr1n_ref, proj_ref, projh_ref, win_ref, wout_ref, pw_ref, ps_ref, lg_ref, lb_ref, sw_ref, sb_ref,
             stats_ref, gx_ref, dproj_ref, dmat_ref, dvec_ref, du_scr, dv_scr):
        i = pl.program_id(0)
        tile_in_seq = i % tiles_per_seq

        @pl.when(i == 0)
        def _():
            dmat_ref[...] = jnp.zeros_like(dmat_ref)
            dvec_ref[0:8, :] = jnp.zeros((8, D_MODEL), F32)
            dvec_ref[8:16, :] = stats_ref[...]

        dr1_t = dr1_ref[...]
        dr1b = dr1_t.astype(BF16)
        dmix = _dot_nt(dr1b, wout_ref[...])
        dpo_next = _dot_nt(dr1n_ref[...].astype(BF16), wout_ref[0:POOL_WIDTH, :])
        dpo_next = jnp.where(tile_in_seq == tiles_per_seq - 1, 0.0, dpo_next)
        proj = proj_ref[...]
        xp_prev = jnp.where(tile_in_seq == 0, 0.0, projh_ref[...])
        pos = tile_in_seq * tt + lax.broadcasted_iota(jnp.int32, (tt, 1), 0)
        pos_next = (tile_in_seq + 1) * tt + lax.broadcasted_iota(jnp.int32, (HALO, 1), 0)

        for g, w in enumerate(POOL_WINDOWS):
            sl = slice(g * GROUP, (g + 1) * GROUP)
            inv_cnt = _inv_count(pos, w)
            pwb = pw_ref[g].astype(BF16)
            pooledb = _pool_fwd(proj[:, sl], xp_prev[:, sl], inv_cnt, w).astype(BF16)
            pre = _dot_nn(pooledb, pwb)
            dpo = dmix[:, sl]
            dvec_ref[ROW_POOL_SCALE:ROW_POOL_SCALE + 1, sl] += _rowsum(dpo * pre)
            dsb = (dpo * ps_ref[:, sl]).astype(BF16)
            dmat_ref[g] += _dot_tn(pooledb, dsb)
            dpooled = _dot_nt(dsb, pwb)
            dpooled_next = _dot_nt((dpo_next[:, sl] * ps_ref[:, sl]).astype(BF16), pwb)
            dxp = _pool_bwd(dpooled, dpooled_next, inv_cnt, _inv_count(pos_next, w), w)
            dproj_ref[:, sl] = dxp.astype(BF16)

        zu = proj[:, POOL_WIDTH:POOL_WIDTH + SGU_WIDTH]
        zv = proj[:, POOL_WIDTH + SGU_WIDTH:]
        u = _gelu(zu)
        vhat, rstd_v = _ln_stats(_gelu(zv))
        v_ln = vhat * lg_ref[...] + lb_ref[...]
        dsg = dmix[:, POOL_WIDTH:]
        row = lax.broadcasted_iota(jnp.int32, (GROUP, GROUP), 0)
        col = lax.broadcasted_iota(jnp.int32, (GROUP, GROUP), 1)
        for h in range(N_HEADS):
            ws = _masked_sgu_w(sw_ref, h).astype(BF16)
            vh = _to_head_major(v_ln, h, nc).astype(BF16)
            mixed = _dot_nn(ws, vh) + sb_ref[:, h:h + 1]
            dsg_h = _to_head_major(dsg, h, nc)
            du_h = dsg_h * mixed
            dm_h = dsg_h * _to_head_major(u, h, nc)
            pos_sums = lax.dot_general(jnp.ones((8, nc * GROUP), F32), dm_h, (((1,), (1,)), ((), ())),
                                       precision=lax.Precision.HIGH, preferred_element_type=F32)
            dvec_ref[ROW_SGU_B + h:ROW_SGU_B + h + 1, 0:GROUP] += pos_sums[0:1, :]
            dmb = dm_h.astype(BF16)
            dmat_ref[len(POOL_WINDOWS) + h] += jnp.where(row >= col, _dot_nt(dmb, vh), 0.0)
            dv_h = _dot_tn(ws, dmb)
            for c in range(nc):
                rs = slice(c * GROUP, (c + 1) * GROUP)
                cs = slice(h * GROUP, (h + 1) * GROUP)
                du_scr[rs, cs] = du_h[:, c * GROUP:(c + 1) * GROUP]
                dv_scr[rs, cs] = dv_h[:, c * GROUP:(c + 1) * GROUP]
        dv_ln = dv_scr[...]
        dvec_ref[ROW_SLN_B:ROW_SLN_B + 1, 0:SGU_WIDTH] += _rowsum(dv_ln)
        dvec_ref[ROW_SLN_G:ROW_SLN_G + 1, 0:SGU_WIDTH] += _rowsum(dv_ln * vhat)
        dv = _ln_bwd(dv_ln, vhat, rstd_v, lg_ref[...])
        dproj_ref[:, POOL_WIDTH:POOL_WIDTH + SGU_WIDTH] = (du_scr[...] * _gelu_grad(zu)).astype(BF16)
        dproj_ref[:, POOL_WIDTH + SGU_WIDTH:] = (dv * _gelu_grad(zv)).astype(BF16)
        gx_ref[...] = ALPHA * dr1_t + _dot_nn(dproj_ref[...], win_ref[...])

    def tile(cols):
        return pl.BlockSpec((tt, cols), lambda i: (i, 0))

    def whole(a):
        nd = a.ndim
        return pl.BlockSpec(a.shape, lambda i: (0,) * nd)

    def resident(shape):
        nd = len(shape)
        return pl.BlockSpec(shape, lambda i: (0,) * nd)

    next_halo = pl.BlockSpec((HALO, D_MODEL), lambda i: (jnp.minimum((i + 1) * (tt // HALO), n_halo_blocks - 1), 0))
    prev_halo = pl.BlockSpec((HALO, POOL_WIDTH), lambda i: (jnp.maximum(i * (tt // HALO) - 1, 0), 0))
    consts = [win_t, wout, pool_w, pool_scale, sln_g, sln_b, sgu_w, sgu_b_t, stats]
    small_shapes = [(len(POOL_WINDOWS) + N_HEADS, GROUP, GROUP), (VEC_ROWS, D_MODEL)]
    return pl.pallas_call(
        body, name="mixer_bwd", grid=(n_tiles,),
        in_specs=[tile(D_MODEL), next_halo, tile(IN_COLS), prev_halo] + [whole(a) for a in consts],
        out_specs=[tile(D_MODEL), tile(IN_COLS)] + [resident(s) for s in small_shapes],
        out_shape=[jax.ShapeDtypeStruct((tokens, D_MODEL), F32), jax.ShapeDtypeStruct((tokens, IN_COLS), BF16)]
                  + [jax.ShapeDtypeStruct(s, F32) for s in small_shapes],
        scratch_shapes=[pltpu.VMEM((tt, SGU_WIDTH), F32), pltpu.VMEM((tt, SGU_WIDTH), F32)],
        compiler_params=pltpu.CompilerParams(dimension_semantics=("arbitrary",), vmem_limit_bytes=VMEM_LIMIT_BYTES),
    )(dr1, dr1, proj, proj, *consts)


def _chip_sums(kept, from_sibling, place, name):
    na = len(kept)

    def body(place_ref, *refs):
        kept_refs, sib_refs = refs[:na], refs[na:2 * na]
        bf_refs, own_refs = refs[2 * na:3 * na], refs[3 * na:]
        q = pl.program_id(0)
        for a in range(na):
            s = kept_refs[a][...] + sib_refs[a][...].astype(F32)
            bf_refs[a][...] = s.astype(BF16)

            @pl.when(q == place_ref[1])
            def _(a=a, s=s):
                own_refs[a][...] = s

    by_chip = [pl.BlockSpec((None,) + k.shape[1:], lambda q, pr: (q, 0, 0)) for k in kept]
    grid_spec = pltpu.PrefetchScalarGridSpec(
        num_scalar_prefetch=1, grid=(N_CHIPS,), in_specs=by_chip + by_chip,
        out_specs=by_chip + [pl.BlockSpec(k.shape[1:], lambda q, pr: (0, 0)) for k in kept])
    return pl.pallas_call(
        body, name=name, grid_spec=grid_spec,
        out_shape=[jax.ShapeDtypeStruct(k.shape, BF16) for k in kept]
                  + [jax.ShapeDtypeStruct(k.shape[1:], F32) for k in kept],
        compiler_params=pltpu.CompilerParams(dimension_semantics=("arbitrary",), vmem_limit_bytes=VMEM_LIMIT_BYTES),
    )(place, *kept, *from_sibling)


def _owner_copies(src_ref, land_ref, sems):
    x, y, c = _mesh_position()
    return [pltpu.make_async_remote_copy(
        src_ref=src_ref.at[2 * cx + cy], dst_ref=land_ref.at[j], send_sem=sems[j], recv_sem=sems[3 + j],
        device_id=(cx, cy, c), device_id_type=MESH) for j, (cx, cy) in enumerate(_other_chips(x, y))]


def _send_to_owners_start(chip_partial, name, collective_id):
    land = _received_shape(chip_partial)

    def body(src_ref, land_ref, *rest):
        x, y, c = _mesh_position()
        barrier = pltpu.get_barrier_semaphore()
        for cx, cy in _other_chips(x, y):
            pl.semaphore_signal(barrier, inc=1, device_id=(cx, cy, c), device_id_type=MESH)
        pl.semaphore_wait(barrier, 3)
        for cp in _owner_copies(src_ref, land_ref, rest[:6]):
            cp.start()
        rest[8][...] = jnp.zeros_like(rest[8])

    hbm = pl.BlockSpec(memory_space=pltpu.HBM)
    sem = pl.BlockSpec(memory_space=pltpu.SEMAPHORE)
    outs = pl.pallas_call(
        body, name=name,
        out_shape=[pltpu.SemaphoreType.DMA(())] * 6 + [pltpu.HBM(chip_partial.shape, chip_partial.dtype),
                                                       pltpu.HBM(land.shape, land.dtype),
                                                       jax.ShapeDtypeStruct((8, GROUP), F32)],
        in_specs=[hbm, hbm], out_specs=[sem] * 6 + [hbm, hbm, pl.BlockSpec(memory_space=pltpu.VMEM)],
        input_output_aliases={0: 6, 1: 7},
        compiler_params=pltpu.CompilerParams(has_side_effects=pltpu.SideEffectType.DATAFLOW_SIDE_EFFECTING,
                                             collective_id=collective_id),
    )(pltpu.with_memory_space_constraint(chip_partial, pltpu.HBM),
      pltpu.with_memory_space_constraint(lax.empty(land.shape, land.dtype), pltpu.HBM))
    return outs[:6], outs[6], outs[7], outs[8]


def _send_to_owners_wait(sems, src_thru, land_thru, after, name):
    def body(src_ref, land_ref, *rest):
        for cp in _owner_copies(src_ref, land_ref, rest[:6]):
            cp.wait_send()
            cp.wait_recv()

    hbm = pl.BlockSpec(memory_space=pltpu.HBM)
    sem = pl.BlockSpec(memory_space=pltpu.SEMAPHORE)
    return pl.pallas_call(
        body, name=name,
        out_shape=[pltpu.HBM(src_thru.shape, src_thru.dtype), pltpu.HBM(land_thru.shape, land_thru.dtype)],
        in_specs=[hbm, hbm] + [sem] * 6 + [pl.BlockSpec(memory_space=pl.ANY)], out_specs=[hbm, hbm],
        input_output_aliases={0: 0, 1: 1},
        compiler_params=pltpu.CompilerParams(has_side_effects=pltpu.SideEffectType.DATAFLOW_SIDE_EFFECTING),
    )(src_thru, land_thru, *sems, after)[1]


def _adamw_math(w, g, m, v):
    m = ADAM_B1 * m + (1.0 - ADAM_B1) * g
    v = ADAM_B2 * v + (1.0 - ADAM_B2) * (g * g)
    m_hat = m / (1.0 - ADAM_B1 ** ADAM_STEP)
    v_hat = v / (1.0 - ADAM_B2 ** ADAM_STEP)
    delta = -ADAM_LR * (m_hat / (jnp.sqrt(v_hat) + ADAM_EPS) + ADAM_WD * w)
    return delta, m, v


def _sum_adamw(w, m, v, own, received, name):
    rows, cols = w.shape
    rb = _row_block(rows, 256)

    def body(w_ref, m_ref, v_ref, own_ref, rec_ref, g_ref, d_ref, mo_ref, vo_ref):
        g = own_ref[...]
        for j in range(3):
            g = g + rec_ref[j].astype(F32)
        g_ref[...] = g
        d_ref[...], mo_ref[...], vo_ref[...] = _adamw_math(w_ref[...], g, m_ref[...], v_ref[...])

    spec = pl.BlockSpec((rb, cols), lambda r: (r, 0))
    return pl.pallas_call(
        body, name=name, grid=(rows // rb,),
        in_specs=[spec] * 4 + [pl.BlockSpec((3, rb, cols), lambda r: (0, r, 0))],
        out_specs=[spec] * 4, out_shape=[jax.ShapeDtypeStruct((rows, cols), F32)] * 4,
        compiler_params=pltpu.CompilerParams(dimension_semantics=("arbitrary",)),
    )(w, m, v, own, received)


SMALL_NAMES = ("pool_w", "sgu_w", "pool_scale", "sgu_ln_g", "sgu_ln_b", "sgu_b", "ln1_g", "ln1_b", "ln2_g", "ln2_b")
_SMALL_VEC_ROWS = {"pool_scale": (ROW_POOL_SCALE, POOL_WIDTH), "sgu_ln_g": (ROW_SLN_G, SGU_WIDTH),
                   "sgu_ln_b": (ROW_SLN_B, SGU_WIDTH), "ln1_g": (ROW_LN1_G, D_MODEL), "ln1_b": (ROW_LN1_B, D_MODEL),
                   "ln2_g": (ROW_LN2_G, D_MODEL), "ln2_b": (ROW_LN2_B, D_MODEL)}
_SMALL_MAT_FIRST = {"pool_w": 0, "sgu_w": len(POOL_WINDOWS)}


def _small_sum_adamw(mats_all, vecs_all, w, m, v):
    n = len(SMALL_NAMES)

    def body(mats_ref, vecs_ref, *refs):
        w_refs, m_refs, v_refs = refs[:n], refs[n:2 * n], refs[2 * n:3 * n]
        loss_ref = refs[3 * n]
        g_refs, d_refs, mo_refs, vo_refs = (refs[3 * n + 1 + k * n:3 * n + 1 + (k + 1) * n] for k in range(4))
        vec_scr = refs[7 * n + 1]

        def update(k, idx, g):
            d, mo, vo = _adamw_math(w_refs[k][idx], g, m_refs[k][idx], v_refs[k][idx])
            g_refs[k][idx], d_refs[k][idx], mo_refs[k][idx], vo_refs[k][idx] = g, d, mo, vo

        total = vecs_ref[0]
        for dev in range(1, N_DEV):
            total = total + vecs_ref[dev]
        vec_scr[...] = total
        for k, name in enumerate(SMALL_NAMES):
            if name in _SMALL_MAT_FIRST:
                for b in range(4):
                    g = mats_ref[0, _SMALL_MAT_FIRST[name] + b]
                    for dev in range(1, N_DEV):
                        g = g + mats_ref[dev, _SMALL_MAT_FIRST[name] + b]
                    update(k, (0, b), g)
            elif name == "sgu_b":
                update(k, (0,), vec_scr[ROW_SGU_B:ROW_SGU_B + N_HEADS, 0:GROUP])
            else:
                row, width = _SMALL_VEC_ROWS[name]
                update(k, (slice(None), slice(None)), vec_scr[row:row + 1, 0:width])
        loss = jnp.sum(vec_scr[ROW_LOSS:ROW_LOSS + 1, :], axis=1, keepdims=True) * (0.5 / D_MODEL)
        loss_ref[...] = jnp.broadcast_to(loss, loss_ref.shape)

    vmem = pl.BlockSpec(memory_space=pltpu.VMEM)
    shapes = [jax.ShapeDtypeStruct(w[k].shape, F32) for k in SMALL_NAMES]
    outs = pl.pallas_call(
        body, name="small_sum_adamw",
        out_shape=[jax.ShapeDtypeStruct((8, GROUP), F32)] + shapes * 4,
        in_specs=[vmem] * (2 + 3 * n), out_specs=[vmem] * (1 + 4 * n),
        scratch_shapes=[pltpu.VMEM((VEC_ROWS, D_MODEL), F32)],
    )(mats_all, vecs_all, *[w[k] for k in SMALL_NAMES], *[m[k] for k in SMALL_NAMES], *[v[k] for k in SMALL_NAMES])
    return outs[0], outs[1:1 + n], outs[1 + n:1 + 2 * n], outs[1 + 2 * n:1 + 3 * n], outs[1 + 3 * n:]


def kernel(x, w_in, pool_w, pool_scale, sgu_ln_g, sgu_ln_b, sgu_w, sgu_b, w_out, ln1_g, ln1_b, w_gate_up, w_down, ln2_g, ln2_b, loss_target, m_w_in, m_pool_w, m_pool_scale, m_sgu_ln_g, m_sgu_ln_b, m_sgu_w, m_sgu_b, m_w_out, m_ln1_g, m_ln1_b, m_w_gate_up, m_w_down, m_ln2_g, m_ln2_b, v_w_in, v_pool_w, v_pool_scale, v_sgu_ln_g, v_sgu_ln_b, v_sgu_w, v_sgu_b, v_w_out, v_ln1_g, v_ln1_b, v_w_gate_up, v_w_down, v_ln2_g, v_ln2_b):
    bl, seq, _ = x.shape
    tokens = bl * seq
    x2d = x.reshape(tokens, D_MODEL)
    tgt2d = loss_target.reshape(tokens, D_MODEL)
    my_c = lax.axis_index("c")
    place = jnp.stack([my_c, 2 * lax.axis_index("x") + lax.axis_index("y")]).astype(jnp.int32)

    win_t, wout = _all_gather_rows([w_in[0].T, w_out[0]], "weight_all_gather")

    pool_w3, sgu_w3 = pool_w[0], sgu_w[0]
    sgu_b_t = sgu_b[0].T
    proj, xhat1, rstd1, mix_bf, x_bf, wgu_t, wdown = _mixer_fwd(
        x2d, win_t, wout, pool_w3, pool_scale, sgu_ln_g, sgu_ln_b, sgu_w3, sgu_b_t, ln1_g, ln1_b,
        [w_gate_up[0].T, w_down[0]], seq)
    dr1, acts_bf, grads_bf, stats = _ffn_fwd_bwd(xhat1, rstd1, tgt2d, wgu_t, wdown, ln1_g, ln1_b, ln2_g, ln2_b)

    kept_gu, sib_gu = _wgrad_exchange(grads_bf, acts_bf, 2, "wgrad_gate_up", collective_id=1,
                                      lhs_cols=(0, 2 * D_FF), rhs_block=0, token_tile=1024)
    bf_gu, own_gu = _chip_sums([kept_gu], [sib_gu], place, "chip_sum_w_gate_up")
    sent_gu = _send_to_owners_start(bf_gu, "grad_scatter_w_gate_up_start", collective_id=4)
    kept_dn, sib_dn = _wgrad_exchange(grads_bf, acts_bf, 2, "wgrad_down", after=sent_gu[3], collective_id=2,
                                      lhs_cols=(2 * D_FF, D_FF), rhs_block=1)
    bf_dn, own_dn = _chip_sums([kept_dn], [sib_dn], place, "chip_sum_w_down")
    sent_dn = _send_to_owners_start(bf_dn, "grad_scatter_w_down_start", collective_id=5)
    kept_out, sib_out = _wgrad_exchange(mix_bf, acts_bf, N_CHIPS, "wgrad_out", after=sent_dn[3], collective_id=3,
                                        rhs_block=2)
    bf_out, own_out = _chip_sums([kept_out], [sib_out], place, "chip_sum_w_out")
    sent_out = _send_to_owners_start(bf_out, "grad_scatter_w_out_start", collective_id=6)
    gx, dproj_bf, d_mats, d_vecs = _mixer_bwd(
        dr1, proj, win_t, wout, pool_w3, pool_scale, sgu_ln_g, sgu_ln_b, sgu_w3, sgu_b_t, stats, seq)
    kept_in, sib_in, mats_all, vecs_all = _wgrad_exchange(
        dproj_bf, x_bf, N_CHIPS, "wgrad_in", gather_rows=[d_mats.reshape(-1, GROUP), d_vecs], after=sent_out[3])
    bf_in, own_in = _chip_sums([kept_in], [sib_in], place, "chip_sum_w_in")
    sent_in = _send_to_owners_start(bf_in, "grad_scatter_w_in_start", collective_id=7)

    grads, deltas, new_m, new_v = {}, {}, {}, {}
    after = sent_in[3]
    for nm, w, m, v, own, sent, transposed in (("w_gate_up", w_gate_up, m_w_gate_up, v_w_gate_up, own_gu, sent_gu, True),
                                               ("w_down", w_down, m_w_down, v_w_down, own_dn, sent_dn, False),
                                               ("w_out", w_out, m_w_out, v_w_out, own_out, sent_out, False),
                                               ("w_in", w_in, m_w_in, v_w_in, own_in, sent_in, True)):
        rows = (lambda a: a[0].T) if transposed else (lambda a: a[0])
        back = (lambda a: a.T[None]) if transposed else (lambda a: a[None])
        rec = _send_to_owners_wait(*sent[:3], after, "grad_scatter_" + nm + "_wait")
        g, d, mo, vo = _sum_adamw(rows(w), rows(m), rows(v), own, rec, "adamw_" + nm)
        after = vo
        grads[nm], deltas[nm], new_m[nm], new_v[nm] = back(g), back(d), back(mo), back(vo)

    small_w = {"pool_w": pool_w, "pool_scale": pool_scale, "sgu_ln_g": sgu_ln_g, "sgu_ln_b": sgu_ln_b, "sgu_w": sgu_w,
               "sgu_b": sgu_b, "ln1_g": ln1_g, "ln1_b": ln1_b, "ln2_g": ln2_g, "ln2_b": ln2_b}
    small_m = {"pool_w": m_pool_w, "pool_scale": m_pool_scale, "sgu_ln_g": m_sgu_ln_g, "sgu_ln_b": m_sgu_ln_b,
               "sgu_w": m_sgu_w, "sgu_b": m_sgu_b, "ln1_g": m_ln1_g, "ln1_b": m_ln1_b, "ln2_g": m_ln2_g, "ln2_b": m_ln2_b}
    small_v = {"pool_w": v_pool_w, "pool_scale": v_pool_scale, "sgu_ln_g": v_sgu_ln_g, "sgu_ln_b": v_sgu_ln_b,
               "sgu_w": v_sgu_w, "sgu_b": v_sgu_b, "ln1_g": v_ln1_g, "ln1_b": v_ln1_b, "ln2_g": v_ln2_g, "ln2_b": v_ln2_b}
    loss_blk, g_small, d_small, m_small, v_small = _small_sum_adamw(
        mats_all.reshape(N_DEV, 2 * N_HEADS, GROUP, GROUP), vecs_all.reshape(N_DEV, VEC_ROWS, D_MODEL),
        small_w, small_m, small_v)
    for vals, dst in ((g_small, grads), (d_small, deltas), (m_small, new_m), (v_small, new_v)):
        dst.update(zip(SMALL_NAMES, vals))

    order = ["w_in", "pool_w", "pool_scale", "sgu_ln_g", "sgu_ln_b", "sgu_w", "sgu_b", "w_out", "ln1_g", "ln1_b",
             "w_gate_up", "w_down", "ln2_g", "ln2_b"]
    return (loss_blk[0, 0], gx.reshape(bl, seq, D_MODEL), *[grads[k] for k in order], *[deltas[k] for k in order],
            *[new_m[k] for k in order], *[new_v[k] for k in order])
```

```python
import jax
import jax.numpy as jnp
from jax import lax
from jax.experimental import pallas as pl
from jax.experimental.pallas import tpu as pltpu

F32 = jnp.float32
BF16 = jnp.bfloat16
MESH = pl.DeviceIdType.MESH

D_MODEL = 1024
POOL_WIDTH = 512
SGU_WIDTH = 512
POOL_WINDOWS = (2, 4, 8, 16)
GROUP = 128
N_HEADS = 4
IN_COLS = POOL_WIDTH + 2 * SGU_WIDTH
D_FF = 2816
LN_EPS = 1e-5
ALPHA = float(2.0 ** 0.25)
HALO = 16
N_DEV = 8
N_CHIPS = 4

ADAM_LR = 0.001
ADAM_B1 = 0.9
ADAM_B2 = 0.999
ADAM_EPS = 1e-08
ADAM_WD = 0.01
ADAM_STEP = 10

VMEM_LIMIT_BYTES = 56 * 1024 * 1024

_SQRT_HALF = 0.7071067811865476
_INV_SQRT_2PI = 0.3989422804014327


def _dot_nn(a, b):
    return lax.dot_general(a, b, (((1,), (0,)), ((), ())), preferred_element_type=F32)


def _dot_nt(a, b):
    return lax.dot_general(a, b, (((1,), (1,)), ((), ())), preferred_element_type=F32)


def _dot_tn(a, b):
    return lax.dot_general(a, b, (((0,), (0,)), ((), ())), preferred_element_type=F32)


def _gelu(x):
    return 0.5 * x * (1.0 + lax.erf(x * _SQRT_HALF))


def _gelu_grad(x):
    return 0.5 * (1.0 + lax.erf(x * _SQRT_HALF)) + x * jnp.exp(-0.5 * x * x) * _INV_SQRT_2PI


def _ln_stats(r):
    mu = jnp.mean(r, axis=-1, keepdims=True)
    d = r - mu
    var = jnp.mean(d * d, axis=-1, keepdims=True)
    rstd = lax.rsqrt(var + LN_EPS)
    return d * rstd, rstd


def _ln_bwd(dout, xhat, rstd, g):
    dxh = dout * g
    m1 = jnp.mean(dxh, axis=-1, keepdims=True)
    m2 = jnp.mean(dxh * xhat, axis=-1, keepdims=True)
    return rstd * (dxh - m1 - xhat * m2)


def _rowsum(a):
    return jnp.sum(a, axis=0, keepdims=True)


def _pool_fwd(xp, xp_prev, inv_cnt, w):
    s = jnp.concatenate([xp_prev, xp], axis=0)
    k = 1
    while k < w:
        s = s + pltpu.roll(s, k, 0)
        k *= 2
    return s[HALO:, :] * inv_cnt - xp


def _pool_bwd(dpooled, dpooled_next, inv_cnt, inv_cnt_next, w):
    n = dpooled.shape[0] + HALO
    s = jnp.concatenate([dpooled * inv_cnt, dpooled_next * inv_cnt_next], axis=0)
    k = 1
    while k < w:
        s = s + pltpu.roll(s, n - k, 0)
        k *= 2
    return s[: dpooled.shape[0], :] - dpooled


def _inv_count(pos, w):
    return 1.0 / jnp.minimum(pos + 1, w).astype(F32)


def _to_head_major(a, h, nc):
    return jnp.concatenate(
        [a[c * GROUP:(c + 1) * GROUP, h * GROUP:(h + 1) * GROUP] for c in range(nc)], axis=1)


def _masked_sgu_w(sw_ref, h):
    row = lax.broadcasted_iota(jnp.int32, (GROUP, GROUP), 0)
    col = lax.broadcasted_iota(jnp.int32, (GROUP, GROUP), 1)
    return jnp.where(row >= col, sw_ref[h], 0.0)


def _row_block(rows, limit):
    return max(b for b in range(16, min(rows, limit) + 1, 16) if rows % b == 0)


def _mesh_position():
    return lax.axis_index("x"), lax.axis_index("y"), lax.axis_index("c")


def _other_chips(x, y):
    return [(1 - x, y), (x, 1 - y), (1 - x, 1 - y)]


class _TwoLevelGather:
    def __init__(self, ins, outs, send_sems, recv_sems, local_sems):
        self.ins, self.outs = ins, outs
        self.send_sems, self.recv_sems, self.local_sems = send_sems, recv_sems, local_sems
        self.na = len(ins)
        x, y, c = _mesh_position()
        self.c = c
        self.me, self.sibling = (x, y, c), (x, y, 1 - c)
        self.chips = _other_chips(x, y)
        self.relay_from = (x + (1 - c) * (1 - 2 * x), y + c * (1 - 2 * y))
        self.relay_to = (x + c * (1 - 2 * x), y + (1 - c) * (1 - 2 * y))

    def _rows(self, a, px, py, pc):
        n = self.ins[a].shape[0]
        return self.outs[a].at[pl.ds((4 * px + 2 * py + pc) * n, n), :]

    def _copy(self, a, k, block, to, src=None):
        return pltpu.make_async_remote_copy(
            src_ref=self._rows(a, *block) if src is None else src, dst_ref=self._rows(a, *block),
            send_sem=self.send_sems.at[a * 7 + k], recv_sem=self.recv_sems.at[a * 7 + k],
            device_id=to, device_id_type=MESH)

    def _mine(self, a):
        return pltpu.make_async_copy(self.ins[a], self._rows(a, *self.me), self.local_sems.at[a])

    def start(self):
        for a in range(self.na):
            self._mine(a).start()
        for a in range(self.na):
            self._copy(a, 0, self.me, self.sibling, src=self.ins[a]).start()
            for j, chip in enumerate(self.chips[:2]):
                self._copy(a, 1 + j, self.me, (*chip, self.c), src=self.ins[a]).start()

    def relay(self, a):
        c, block = self.c, (*self.relay_from, self.c)
        self._copy(a, 1 + c, block, self.me).wait_recv()
        self._copy(a, 3, block, (*self.relay_to, c)).start()
        self._copy(a, 4 + c, block, self.sibling).start()

    def pass_on(self, a):
        c = self.c
        self._copy(a, 2 - c, (*self.relay_to, c), self.me).wait_recv()
        self._copy(a, 5 - c, (*self.relay_to, c), self.sibling).start()
        self._copy(a, 3, (*self.chips[2], c), self.me).wait_recv()
        self._copy(a, 6, (*self.chips[2], c), self.sibling).start()

    def finish(self):
        for a in range(self.na):
            self._copy(a, 0, self.sibling, self.me).wait_recv()
            for j, chip in enumerate(self.chips):
                self._copy(a, 4 + j, (*chip, 1 - self.c), self.me).wait_recv()
        for a in range(self.na):
            for k in range(7):
                self._copy(a, k, self.me, self.sibling, src=self.ins[a]).wait_send()
            self._mine(a).wait()

    @staticmethod
    def scratch(na):
        return [pltpu.SemaphoreType.DMA((7 * na,)), pltpu.SemaphoreType.DMA((7 * na,)), pltpu.SemaphoreType.DMA((na,))]


def _gathered_shape(s):
    return jax.ShapeDtypeStruct((N_DEV * s.shape[0], s.shape[1]), s.dtype)


def _gathered_bf16(s):
    return jax.ShapeDtypeStruct((N_DEV * s.shape[0], s.shape[1]), BF16)


def _all_gather_rows(shards, name):
    na = len(shards)

    def body(*refs):
        bf_refs = refs[2 * na:3 * na]
        for a in range(na):
            bf_refs[a][...] = refs[a][...].astype(BF16)
        gather = _TwoLevelGather(bf_refs, refs[na:2 * na], *refs[3 * na:])
        gather.start()
        for a in range(na):
            gather.relay(a)
        for a in range(na):
            gather.pass_on(a)
        gather.finish()

    return pl.pallas_call(
        body, name=name, out_shape=[_gathered_bf16(s) for s in shards],
        in_specs=[pl.BlockSpec(memory_space=pltpu.VMEM)] * na, out_specs=[pl.BlockSpec(memory_space=pl.ANY)] * na,
        scratch_shapes=[pltpu.VMEM(s.shape, BF16) for s in shards] + _TwoLevelGather.scratch(na),
    )(*shards)


def _mixer_fwd(x2d, win_t, wout, pool_w, pool_scale, sln_g, sln_b, sgu_w, sgu_b_t, ln1_g, ln1_b, later_shards, seq):
    tokens = x2d.shape[0]
    tt = min(512, seq)
    tiles_per_seq = seq // tt
    nc = tt // GROUP
    n_tiles = tokens // tt
    n_later = len(later_shards)

    def body(x_ref, xh_ref, win_ref, wout_ref, pw_ref, ps_ref, lg_ref, lb_ref, sw_ref, sb_ref, g1_ref, b1_ref, *rest):
        shard_refs, rest = rest[:n_later], rest[n_later:]
        proj_ref, xhat_ref, rstd_ref, mix_ref, xbf_ref = rest[:5]
        gathered_refs, rest = rest[5:5 + n_later], rest[5 + n_later:]
        mix_scr, shard_bf_refs, (send_sems, recv_sems, local_sems) = rest[0], rest[1:1 + n_later], rest[1 + n_later:]
        i = pl.program_id(0)
        gather = _TwoLevelGather(shard_bf_refs, gathered_refs, send_sems, recv_sems, local_sems)

        @pl.when(i == 0)
        def _():
            for a in range(n_later):
                shard_bf_refs[a][...] = shard_refs[a][...].astype(BF16)
            gather.start()

        tile_in_seq = i % tiles_per_seq
        x = x_ref[...]
        xb = x.astype(BF16)
        xbf_ref[...] = xb
        proj = _dot_nt(xb, win_ref[...])
        proj_ref[...] = proj
        xp_prev = _dot_nt(xh_ref[...].astype(BF16), win_ref[0:POOL_WIDTH, :])
        xp_prev = jnp.where(tile_in_seq == 0, 0.0, xp_prev)
        pos = tile_in_seq * tt + lax.broadcasted_iota(jnp.int32, (tt, 1), 0)
        for g, w in enumerate(POOL_WINDOWS):
            sl = slice(g * GROUP, (g + 1) * GROUP)
            pooled = _pool_fwd(proj[:, sl], xp_prev[:, sl], _inv_count(pos, w), w)
            pre = _dot_nn(pooled.astype(BF16), pw_ref[g].astype(BF16))
            mix_scr[:, sl] = pre * ps_ref[:, sl]
        u = _gelu(proj[:, POOL_WIDTH:POOL_WIDTH + SGU_WIDTH])
        v = _gelu(proj[:, POOL_WIDTH + SGU_WIDTH:])
        vhat, _ = _ln_stats(v)
        v_ln = vhat * lg_ref[...] + lb_ref[...]
        for h in range(N_HEADS):
            ws = _masked_sgu_w(sw_ref, h).astype(BF16)
            mixed = _dot_nn(ws, _to_head_major(v_ln, h, nc).astype(BF16)) + sb_ref[:, h:h + 1]
            for c in range(nc):
                rs = slice(c * GROUP, (c + 1) * GROUP)
                mix_scr[rs, POOL_WIDTH + h * GROUP:POOL_WIDTH + (h + 1) * GROUP] = (
                    u[rs, h * GROUP:(h + 1) * GROUP] * mixed[:, c * GROUP:(c + 1) * GROUP])
        mixb = mix_scr[...].astype(BF16)
        mix_ref[...] = mixb
        r1 = ALPHA * x + _dot_nn(mixb, wout_ref[...])
        xhat, rstd = _ln_stats(r1)
        xhat_ref[...] = xhat
        rstd_ref[...] = rstd

        for a in range(n_later):
            relay_tile = min(n_tiles // 2 + a, n_tiles - 1)

            @pl.when(i == relay_tile)
            def _(a=a):
                gather.relay(a)

            @pl.when(i == max(n_tiles - n_later + a, relay_tile))
            def _(a=a):
                gather.pass_on(a)

        @pl.when(i == n_tiles - 1)
        def _():
            gather.finish()

    def tile(cols):
        return pl.BlockSpec((tt, cols), lambda i: (i, 0))

    def whole(a):
        nd = a.ndim
        return pl.BlockSpec(a.shape, lambda i: (0,) * nd)

    any_spec = pl.BlockSpec(memory_space=pl.ANY)
    halo = pl.BlockSpec((HALO, D_MODEL), lambda i: (jnp.maximum(i * (tt // HALO) - 1, 0), 0))
    consts = [win_t, wout, pool_w, pool_scale, sln_g, sln_b, sgu_w, sgu_b_t, ln1_g, ln1_b]
    return pl.pallas_call(
        body, name="mixer_fwd", grid=(n_tiles,),
        in_specs=[tile(D_MODEL), halo] + [whole(a) for a in consts] + [whole(s) for s in later_shards],
        out_specs=[tile(IN_COLS), tile(D_MODEL), tile(1), tile(D_MODEL), tile(D_MODEL)] + [any_spec] * n_later,
        out_shape=[jax.ShapeDtypeStruct((tokens, IN_COLS), F32), jax.ShapeDtypeStruct((tokens, D_MODEL), F32),
                   jax.ShapeDtypeStruct((tokens, 1), F32), jax.ShapeDtypeStruct((tokens, D_MODEL), BF16),
                   jax.ShapeDtypeStruct((tokens, D_MODEL), BF16)] + [_gathered_bf16(s) for s in later_shards],
        scratch_shapes=[pltpu.VMEM((tt, D_MODEL), F32)] + [pltpu.VMEM(s.shape, BF16) for s in later_shards]
                       + _TwoLevelGather.scratch(n_later),
        compiler_params=pltpu.CompilerParams(dimension_semantics=("arbitrary",), vmem_limit_bytes=VMEM_LIMIT_BYTES),
    )(x2d, x2d, *consts, *later_shards)


def _ffn_fwd_bwd(xhat1, rstd1, target, wgu_t, wdown, ln1_g, ln1_b, ln2_g, ln2_b):
    tokens = xhat1.shape[0]
    tt = min(256, tokens)

    def body(xhat_ref, rstd_ref, tgt_ref, wgu_hbm, wd_hbm, g1_ref, b1_ref, g2_ref, b2_ref,
             dr1_ref, acts_ref, grads_ref, stats_ref, wgu_ref, wd_ref, gu_scr, sems):
        i = pl.program_id(0)

        @pl.when(i == 0)
        def _():
            loads = [pltpu.make_async_copy(wgu_hbm, wgu_ref, sems.at[0]),
                     pltpu.make_async_copy(wd_hbm, wd_ref, sems.at[1])]
            for cp in loads:
                cp.start()
            stats_ref[...] = jnp.zeros_like(stats_ref)
            for cp in loads:
                cp.wait()

        xhat1_t = xhat_ref[...]
        h = xhat1_t * g1_ref[...] + b1_ref[...]
        hb = h.astype(BF16)
        acts_ref[:, 0:D_MODEL] = hb
        gate = _dot_nt(hb, wgu_ref[0:D_FF, :])
        up = _dot_nt(hb, wgu_ref[D_FF:, :])
        gu_scr[:, 0:D_FF] = gate
        gu_scr[:, D_FF:] = up
        ab = (gate * jax.nn.sigmoid(gate) * up).astype(BF16)
        grads_ref[:, 2 * D_FF:] = ab
        xhat2, rstd2 = _ln_stats(ALPHA * h + _dot_nn(ab, wd_ref[...]))
        err = xhat2 * g2_ref[...] + b2_ref[...] - tgt_ref[...]
        dy = err * (1.0 / D_MODEL)
        stats_ref[0:1, :] += _rowsum(dy * xhat2)
        stats_ref[1:2, :] += _rowsum(dy)
        stats_ref[4:5, :] += _rowsum(err * err)
        dr2 = _ln_bwd(dy, xhat2, rstd2, g2_ref[...])
        dr2b = dr2.astype(BF16)
        acts_ref[:, D_MODEL:2 * D_MODEL] = dr2b
        da = _dot_nt(dr2b, wd_ref[...])
        gate = gu_scr[:, 0:D_FF]
        up = gu_scr[:, D_FF:]
        sg = jax.nn.sigmoid(gate)
        dgate = (da * up * (sg * (1.0 + gate * (1.0 - sg)))).astype(BF16)
        dup = (da * (gate * sg)).astype(BF16)
        grads_ref[:, 0:D_FF] = dgate
        grads_ref[:, D_FF:2 * D_FF] = dup
        dh = ALPHA * dr2 + _dot_nn(dgate, wgu_ref[0:D_FF, :]) + _dot_nn(dup, wgu_ref[D_FF:, :])
        stats_ref[2:3, :] += _rowsum(dh * xhat1_t)
        stats_ref[3:4, :] += _rowsum(dh)
        dr1 = _ln_bwd(dh, xhat1_t, rstd_ref[...], g1_ref[...])
        dr1_ref[...] = dr1
        acts_ref[:, 2 * D_MODEL:] = dr1.astype(BF16)

    def tile(cols):
        return pl.BlockSpec((tt, cols), lambda i: (i, 0))

    def whole(a):
        nd = a.ndim
        return pl.BlockSpec(a.shape, lambda i: (0,) * nd)

    any_spec = pl.BlockSpec(memory_space=pl.ANY)
    vecs = [ln1_g, ln1_b, ln2_g, ln2_b]
    return pl.pallas_call(
        body, name="ffn_fwd_bwd", grid=(tokens // tt,),
        in_specs=[tile(D_MODEL), tile(1), tile(D_MODEL), any_spec, any_spec] + [whole(a) for a in vecs],
        out_specs=[tile(D_MODEL), tile(3 * D_MODEL), tile(3 * D_FF), pl.BlockSpec((8, D_MODEL), lambda i: (0, 0))],
        out_shape=[jax.ShapeDtypeStruct((tokens, D_MODEL), F32), jax.ShapeDtypeStruct((tokens, 3 * D_MODEL), BF16),
                   jax.ShapeDtypeStruct((tokens, 3 * D_FF), BF16), jax.ShapeDtypeStruct((8, D_MODEL), F32)],
        scratch_shapes=[pltpu.VMEM(wgu_t.shape, BF16), pltpu.VMEM(wdown.shape, BF16),
                        pltpu.VMEM((tt, 2 * D_FF), F32), pltpu.SemaphoreType.DMA((2,))],
        compiler_params=pltpu.CompilerParams(dimension_semantics=("arbitrary",), vmem_limit_bytes=VMEM_LIMIT_BYTES),
    )(xhat1, rstd1, target, wgu_t, wdown, *vecs)


def _wgrad_exchange(lhs, rhs, chips_per_block, name, gather_rows=(), after=None, collective_id=None,
                    lhs_cols=None, rhs_block=None, token_tile=2048):
    assert collective_id is None or not gather_rows
    tokens = lhs.shape[0]
    first_col, n_all = (0, lhs.shape[1]) if lhs_cols is None else lhs_cols
    m = rhs.shape[1] if rhs_block is None else D_MODEL
    n = n_all // N_DEV
    tw = min(token_tile, tokens)
    nt = tokens // tw
    cpb = chips_per_block
    nj = N_CHIPS // cpb
    lhs_block0 = first_col // (2 * n * cpb)
    assert lhs_block0 * 2 * n * cpb == first_col
    rhs_col = 0 if rhs_block is None else rhs_block
    ng = len(gather_rows)
    anchors = [] if after is None else [after]

    def body(l_ref, r_ref, *rest):
        small_refs, rest = rest[:ng], rest[ng + len(anchors):]
        bf_ref, own_ref = rest[:2]
        gathered_refs, rest = rest[2:2 + ng], rest[2 + ng:]
        acc, kept, sendbuf, landed, send_sems, recv_sems = rest[:6]
        j, t = pl.program_id(0), pl.program_id(1)
        first, last = (j == 0) & (t == 0), (j == nj - 1) & (t == nt - 1)
        x, y, c = _mesh_position()
        if collective_id is not None:
            @pl.when(first)
            def _():
                barrier = pltpu.get_barrier_semaphore()
                pl.semaphore_signal(barrier, inc=1, device_id=(x, y, 1 - c), device_id_type=MESH)
                pl.semaphore_wait(barrier, 1)
        if ng:
            gather = _TwoLevelGather(small_refs, gathered_refs, *rest[6:9])

            @pl.when(first)
            def _():
                gather.start()

        def copy(q):
            return pltpu.make_async_remote_copy(
                src_ref=sendbuf.at[q], dst_ref=landed.at[q], send_sem=send_sems.at[q], recv_sem=recv_sems.at[q],
                device_id=(x, y, 1 - c), device_id_type=MESH)

        @pl.when(t == 0)
        def _():
            acc[...] = jnp.zeros_like(acc)

        acc[...] += _dot_tn(l_ref[...], r_ref[...])

        @pl.when(t == nt - 1)
        def _():
            for qq in range(cpb):
                q = j * cpb + qq
                kept[q] = acc[pl.ds(pl.multiple_of(qq * 2 * n + c * n, 8), n), :]
                sendbuf[q] = acc[pl.ds(pl.multiple_of(qq * 2 * n + (1 - c) * n, 8), n), :].astype(BF16)
                copy(q).start()

        if ng:
            @pl.when((j == nj - 1) & (t == nt // 2))
            def _():
                for a in range(ng):
                    gather.relay(a)

            @pl.when(last)
            def _():
                for a in range(ng):
                    gather.pass_on(a)
                gather.finish()

        @pl.when(last)
        def _():
            for q in range(N_CHIPS):
                copy(q).wait_send()
                copy(q).wait_recv()
            for q in range(N_CHIPS):
                s = kept[q] + landed[q].astype(F32)
                bf_ref[q] = s.astype(BF16)

                @pl.when(q == 2 * x + y)
                def _(s=s):
                    own_ref[...] = s

    any_spec = pl.BlockSpec(memory_space=pl.ANY)
    by_chip = (N_CHIPS, n, m)
    return pl.pallas_call(
        body, name=name, grid=(nj, nt),
        in_specs=[pl.BlockSpec((tw, 2 * n * cpb), lambda j, t: (t, lhs_block0 + j)),
                  pl.BlockSpec((tw, m), lambda j, t: (t, rhs_col))] + [any_spec] * (ng + len(anchors)),
        out_specs=[pl.BlockSpec(by_chip, lambda j, t: (0, 0, 0)), pl.BlockSpec((n, m), lambda j, t: (0, 0))]
                  + [any_spec] * ng,
        out_shape=[jax.ShapeDtypeStruct(by_chip, BF16), jax.ShapeDtypeStruct((n, m), F32)]
                  + [_gathered_shape(s) for s in gather_rows],
        scratch_shapes=[pltpu.VMEM((2 * n * cpb, m), F32), pltpu.VMEM(by_chip, F32), pltpu.VMEM(by_chip, BF16),
                        pltpu.VMEM(by_chip, BF16), pltpu.SemaphoreType.DMA((N_CHIPS,)),
                        pltpu.SemaphoreType.DMA((N_CHIPS,))] + (_TwoLevelGather.scratch(ng) if ng else []),
        compiler_params=pltpu.CompilerParams(dimension_semantics=("arbitrary", "arbitrary"),
                                             vmem_limit_bytes=VMEM_LIMIT_BYTES + 4 * 1024 * 1024,
                                             collective_id=collective_id),
    )(lhs, rhs, *gather_rows, *anchors)


def _received_shape(p):
    return jax.ShapeDtypeStruct((3,) + p.shape[1:], p.dtype)


ROW_POOL_SCALE, ROW_SLN_G, ROW_SLN_B, ROW_SGU_B = 0, 1, 2, 3
ROW_LN2_G, ROW_LN2_B, ROW_LN1_G, ROW_LN1_B, ROW_LOSS = 8, 9, 10, 11, 12
VEC_ROWS = 16


def _mixer_bwd(dr1, proj, win_t, wout, pool_w, pool_scale, sln_g, sln_b, sgu_w, sgu_b_t, stats, seq):
    tokens = dr1.shape[0]
    tt = min(512, seq)
    tiles_per_seq = seq // tt
    nc = tt // GROUP
    n_halo_blocks = tokens // HALO
    n_tiles = tokens // tt

    def body(dr1_ref, dr1n_ref, proj_ref, projh_ref, win_ref, wout_ref, pw_ref, ps_ref, lg_ref, lb_ref, sw_ref, sb_ref,
             stats_ref, gx_ref, dproj_ref, dmat_ref, dvec_ref, du_scr, dv_scr):
        i = pl.program_id(0)
        tile_in_seq = i % tiles_per_seq

        @pl.when(i == 0)
        def _():
            dmat_ref[...] = jnp.zeros_like(dmat_ref)
            dvec_ref[0:8, :] = jnp.zeros((8, D_MODEL), F32)
            dvec_ref[8:16, :] = stats_ref[...]

        dr1_t = dr1_ref[...]
        dr1b = dr1_t.astype(BF16)
        dmix = _dot_nt(dr1b, wout_ref[...])
        dpo_next = _dot_nt(dr1n_ref[...].astype(BF16), wout_ref[0:POOL_WIDTH, :])
        dpo_next = jnp.where(tile_in_seq == tiles_per_seq - 1, 0.0, dpo_next)
        proj = proj_ref[...]
        xp_prev = jnp.where(tile_in_seq == 0, 0.0, projh_ref[...])
        pos = tile_in_seq * tt + lax.broadcasted_iota(jnp.int32, (tt, 1), 0)
        pos_next = (tile_in_seq + 1) * tt + lax.broadcasted_iota(jnp.int32, (HALO, 1), 0)

        for g, w in enumerate(POOL_WINDOWS):
            sl = slice(g * GROUP, (g + 1) * GROUP)
            inv_cnt = _inv_count(pos, w)
            pwb = pw_ref[g].astype(BF16)
            pooledb = _pool_fwd(proj[:, sl], xp_prev[:, sl], inv_cnt, w).astype(BF16)
            pre = _dot_nn(pooledb, pwb)
            dpo = dmix[:, sl]
            dvec_ref[ROW_POOL_SCALE:ROW_POOL_SCALE + 1, sl] += _rowsum(dpo * pre)
            dsb = (dpo * ps_ref[:, sl]).astype(BF16)
            dmat_ref[g] += _dot_tn(pooledb, dsb)
            dpooled = _dot_nt(dsb, pwb)
            dpooled_next = _dot_nt((dpo_next[:, sl] * ps_ref[:, sl]).astype(BF16), pwb)
            dxp = _pool_bwd(dpooled, dpooled_next, inv_cnt, _inv_count(pos_next, w), w)
            dproj_ref[:, sl] = dxp.astype(BF16)

        zu = proj[:, POOL_WIDTH:POOL_WIDTH + SGU_WIDTH]
        zv = proj[:, POOL_WIDTH + SGU_WIDTH:]
        u = _gelu(zu)
        vhat, rstd_v = _ln_stats(_gelu(zv))
        v_ln = vhat * lg_ref[...] + lb_ref[...]
        dsg = dmix[:, POOL_WIDTH:]
        row = lax.broadcasted_iota(jnp.int32, (GROUP, GROUP), 0)
        col = lax.broadcasted_iota(jnp.int32, (GROUP, GROUP), 1)
        for h in range(N_HEADS):
            ws = _masked_sgu_w(sw_ref, h).astype(BF16)
            vh = _to_head_major(v_ln, h, nc).astype(BF16)
            mixed = _dot_nn(ws, vh) + sb_ref[:, h:h + 1]
            dsg_h = _to_head_major(dsg, h, nc)
            du_h = dsg_h * mixed
            dm_h = dsg_h * _to_head_major(u, h, nc)
            pos_sums = lax.dot_general(jnp.ones((8, nc * GROUP), F32), dm_h, (((1,), (1,)), ((), ())),
                                       precision=lax.Precision.HIGH, preferred_element_type=F32)
            dvec_ref[ROW_SGU_B + h:ROW_SGU_B + h + 1, 0:GROUP] += pos_sums[0:1, :]
            dmb = dm_h.astype(BF16)
            dmat_ref[len(POOL_WINDOWS) + h] += jnp.where(row >= col, _dot_nt(dmb, vh), 0.0)
            dv_h = _dot_tn(ws, dmb)
            for c in range(nc):
                rs = slice(c * GROUP, (c + 1) * GROUP)
                cs = slice(h * GROUP, (h + 1) * GROUP)
                du_scr[rs, cs] = du_h[:, c * GROUP:(c + 1) * GROUP]
                dv_scr[rs, cs] = dv_h[:, c * GROUP:(c + 1) * GROUP]
        dv_ln = dv_scr[...]
        dvec_ref[ROW_SLN_B:ROW_SLN_B + 1, 0:SGU_WIDTH] += _rowsum(dv_ln)
        dvec_ref[ROW_SLN_G:ROW_SLN_G + 1, 0:SGU_WIDTH] += _rowsum(dv_ln * vhat)
        dv = _ln_bwd(dv_ln, vhat, rstd_v, lg_ref[...])
        dproj_ref[:, POOL_WIDTH:POOL_WIDTH + SGU_WIDTH] = (du_scr[...] * _gelu_grad(zu)).astype(BF16)
        dproj_ref[:, POOL_WIDTH + SGU_WIDTH:] = (dv * _gelu_grad(zv)).astype(BF16)
        gx_ref[...] = ALPHA * dr1_t + _dot_nn(dproj_ref[...], win_ref[...])

    def tile(cols):
        return pl.BlockSpec((tt, cols), lambda i: (i, 0))

    def whole(a):
        nd = a.ndim
        return pl.BlockSpec(a.shape, lambda i: (0,) * nd)

    def resident(shape):
        nd = len(shape)
        return pl.BlockSpec(shape, lambda i: (0,) * nd)

    next_halo = pl.BlockSpec((HALO, D_MODEL), lambda i: (jnp.minimum((i + 1) * (tt // HALO), n_halo_blocks - 1), 0))
    prev_halo = pl.BlockSpec((HALO, POOL_WIDTH), lambda i: (jnp.maximum(i * (tt // HALO) - 1, 0), 0))
    consts = [win_t, wout, pool_w, pool_scale, sln_g, sln_b, sgu_w, sgu_b_t, stats]
    small_shapes = [(len(POOL_WINDOWS) + N_HEADS, GROUP, GROUP), (VEC_ROWS, D_MODEL)]
    return pl.pallas_call(
        body, name="mixer_bwd", grid=(n_tiles,),
        in_specs=[tile(D_MODEL), next_halo, tile(IN_COLS), prev_halo] + [whole(a) for a in consts],
        out_specs=[tile(D_MODEL), tile(IN_COLS)] + [resident(s) for s in small_shapes],
        out_shape=[jax.ShapeDtypeStruct((tokens, D_MODEL), F32), jax.ShapeDtypeStruct((tokens, IN_COLS), BF16)]
                  + [jax.ShapeDtypeStruct(s, F32) for s in small_shapes],
        scratch_shapes=[pltpu.VMEM((tt, SGU_WIDTH), F32), pltpu.VMEM((tt, SGU_WIDTH), F32)],
        compiler_params=pltpu.CompilerParams(dimension_semantics=("arbitrary",), vmem_limit_bytes=VMEM_LIMIT_BYTES),
    )(dr1, dr1, proj, proj, *consts)


def _owner_copies(src_ref, land_ref, sems):
    x, y, c = _mesh_position()
    return [pltpu.make_async_remote_copy(
        src_ref=src_ref.at[2 * cx + cy], dst_ref=land_ref.at[j], send_sem=sems[j], recv_sem=sems[3 + j],
        device_id=(cx, cy, c), device_id_type=MESH) for j, (cx, cy) in enumerate(_other_chips(x, y))]


def _send_to_owners_start(chip_partial, name, collective_id):
    land = _received_shape(chip_partial)

    def body(src_ref, land_ref, *rest):
        x, y, c = _mesh_position()
        barrier = pltpu.get_barrier_semaphore()
        for cx, cy in _other_chips(x, y):
            pl.semaphore_signal(barrier, inc=1, device_id=(cx, cy, c), device_id_type=MESH)
        pl.semaphore_wait(barrier, 3)
        for cp in _owner_copies(src_ref, land_ref, rest[:6]):
            cp.start()
        rest[8][...] = jnp.zeros_like(rest[8])

    hbm = pl.BlockSpec(memory_space=pltpu.HBM)
    sem = pl.BlockSpec(memory_space=pltpu.SEMAPHORE)
    outs = pl.pallas_call(
        body, name=name,
        out_shape=[pltpu.SemaphoreType.DMA(())] * 6 + [pltpu.HBM(chip_partial.shape, chip_partial.dtype),
                                                       pltpu.HBM(land.shape, land.dtype),
                                                       jax.ShapeDtypeStruct((8, GROUP), F32)],
        in_specs=[hbm, hbm], out_specs=[sem] * 6 + [hbm, hbm, pl.BlockSpec(memory_space=pltpu.VMEM)],
        input_output_aliases={0: 6, 1: 7},
        compiler_params=pltpu.CompilerParams(has_side_effects=pltpu.SideEffectType.DATAFLOW_SIDE_EFFECTING,
                                             collective_id=collective_id),
    )(pltpu.with_memory_space_constraint(chip_partial, pltpu.HBM),
      pltpu.with_memory_space_constraint(lax.empty(land.shape, land.dtype), pltpu.HBM))
    return outs[:6], outs[6], outs[7], outs[8]


def _send_to_owners_wait(sems, src_thru, land_thru, after, name):
    def body(src_ref, land_ref, *rest):
        for cp in _owner_copies(src_ref, land_ref, rest[:6]):
            cp.wait_send()
            cp.wait_recv()

    hbm = pl.BlockSpec(memory_space=pltpu.HBM)
    sem = pl.BlockSpec(memory_space=pltpu.SEMAPHORE)
    return pl.pallas_call(
        body, name=name,
        out_shape=[pltpu.HBM(src_thru.shape, src_thru.dtype), pltpu.HBM(land_thru.shape, land_thru.dtype)],
        in_specs=[hbm, hbm] + [sem] * 6 + [pl.BlockSpec(memory_space=pl.ANY)], out_specs=[hbm, hbm],
        input_output_aliases={0: 0, 1: 1},
        compiler_params=pltpu.CompilerParams(has_side_effects=pltpu.SideEffectType.DATAFLOW_SIDE_EFFECTING),
    )(src_thru, land_thru, *sems, after)[1]


def _adamw_math(w, g, m, v):
    m = ADAM_B1 * m + (1.0 - ADAM_B1) * g
    v = ADAM_B2 * v + (1.0 - ADAM_B2) * (g * g)
    m_hat = m / (1.0 - ADAM_B1 ** ADAM_STEP)
    v_hat = v / (1.0 - ADAM_B2 ** ADAM_STEP)
    delta = -ADAM_LR * (m_hat / (jnp.sqrt(v_hat) + ADAM_EPS) + ADAM_WD * w)
    return delta, m, v


def _sum_adamw(w, m, v, own, received, name):
    rows, cols = w.shape
    rb = _row_block(rows, 256)

    def body(w_ref, m_ref, v_ref, own_ref, rec_ref, g_ref, d_ref, mo_ref, vo_ref):
        g = own_ref[...]
        for j in range(3):
            g = g + rec_ref[j].astype(F32)
        g_ref[...] = g
        d_ref[...], mo_ref[...], vo_ref[...] = _adamw_math(w_ref[...], g, m_ref[...], v_ref[...])

    spec = pl.BlockSpec((rb, cols), lambda r: (r, 0))
    return pl.pallas_call(
        body, name=name, grid=(rows // rb,),
        in_specs=[spec] * 4 + [pl.BlockSpec((3, rb, cols), lambda r: (0, r, 0))],
        out_specs=[spec] * 4, out_shape=[jax.ShapeDtypeStruct((rows, cols), F32)] * 4,
        compiler_params=pltpu.CompilerParams(dimension_semantics=("arbitrary",)),
    )(w, m, v, own, received)


SMALL_NAMES = ("pool_w", "sgu_w", "pool_scale", "sgu_ln_g", "sgu_ln_b", "sgu_b", "ln1_g", "ln1_b", "ln2_g", "ln2_b")
_SMALL_VEC_ROWS = {"pool_scale": (ROW_POOL_SCALE, POOL_WIDTH), "sgu_ln_g": (ROW_SLN_G, SGU_WIDTH),
                   "sgu_ln_b": (ROW_SLN_B, SGU_WIDTH), "ln1_g": (ROW_LN1_G, D_MODEL), "ln1_b": (ROW_LN1_B, D_MODEL),
                   "ln2_g": (ROW_LN2_G, D_MODEL), "ln2_b": (ROW_LN2_B, D_MODEL)}
_SMALL_MAT_FIRST = {"pool_w": 0, "sgu_w": len(POOL_WINDOWS)}


def _small_sum_adamw(mats_all, vecs_all, w, m, v):
    n = len(SMALL_NAMES)

    def body(mats_ref, vecs_ref, *refs):
        w_refs, m_refs, v_refs = refs[:n], refs[n:2 * n], refs[2 * n:3 * n]
        loss_ref = refs[3 * n]
        g_refs, d_refs, mo_refs, vo_refs = (refs[3 * n + 1 + k * n:3 * n + 1 + (k + 1) * n] for k in range(4))
        vec_scr = refs[7 * n + 1]

        def update(k, idx, g):
            d, mo, vo = _adamw_math(w_refs[k][idx], g, m_refs[k][idx], v_refs[k][idx])
            g_refs[k][idx], d_refs[k][idx], mo_refs[k][idx], vo_refs[k][idx] = g, d, mo, vo

        total = vecs_ref[0]
        for dev in range(1, N_DEV):
            total = total + vecs_ref[dev]
        vec_scr[...] = total
        for k, name in enumerate(SMALL_NAMES):
            if name in _SMALL_MAT_FIRST:
                for b in range(4):
                    g = mats_ref[0, _SMALL_MAT_FIRST[name] + b]
                    for dev in range(1, N_DEV):
                        g = g + mats_ref[dev, _SMALL_MAT_FIRST[name] + b]
                    update(k, (0, b), g)
            elif name == "sgu_b":
                update(k, (0,), vec_scr[ROW_SGU_B:ROW_SGU_B + N_HEADS, 0:GROUP])
            else:
                row, width = _SMALL_VEC_ROWS[name]
                update(k, (slice(None), slice(None)), vec_scr[row:row + 1, 0:width])
        loss = jnp.sum(vec_scr[ROW_LOSS:ROW_LOSS + 1, :], axis=1, keepdims=True) * (0.5 / D_MODEL)
        loss_ref[...] = jnp.broadcast_to(loss, loss_ref.shape)

    vmem = pl.BlockSpec(memory_space=pltpu.VMEM)
    shapes = [jax.ShapeDtypeStruct(w[k].shape, F32) for k in SMALL_NAMES]
    outs = pl.pallas_call(
        body, name="small_sum_adamw",
        out_shape=[jax.ShapeDtypeStruct((8, GROUP), F32)] + shapes * 4,
        in_specs=[vmem] * (2 + 3 * n), out_specs=[vmem] * (1 + 4 * n),
        scratch_shapes=[pltpu.VMEM((VEC_ROWS, D_MODEL), F32)],
    )(mats_all, vecs_all, *[w[k] for k in SMALL_NAMES], *[m[k] for k in SMALL_NAMES], *[v[k] for k in SMALL_NAMES])
    return outs[0], outs[1:1 + n], outs[1 + n:1 + 2 * n], outs[1 + 2 * n:1 + 3 * n], outs[1 + 3 * n:]


def kernel(x, w_in, pool_w, pool_scale, sgu_ln_g, sgu_ln_b, sgu_w, sgu_b, w_out, ln1_g, ln1_b, w_gate_up, w_down, ln2_g, ln2_b, loss_target, m_w_in, m_pool_w, m_pool_scale, m_sgu_ln_g, m_sgu_ln_b, m_sgu_w, m_sgu_b, m_w_out, m_ln1_g, m_ln1_b, m_w_gate_up, m_w_down, m_ln2_g, m_ln2_b, v_w_in, v_pool_w, v_pool_scale, v_sgu_ln_g, v_sgu_ln_b, v_sgu_w, v_sgu_b, v_w_out, v_ln1_g, v_ln1_b, v_w_gate_up, v_w_down, v_ln2_g, v_ln2_b):
    bl, seq, _ = x.shape
    tokens = bl * seq
    x2d = x.reshape(tokens, D_MODEL)
    tgt2d = loss_target.reshape(tokens, D_MODEL)

    win_t, wout = _all_gather_rows([w_in[0].T, w_out[0]], "weight_all_gather")

    pool_w3, sgu_w3 = pool_w[0], sgu_w[0]
    sgu_b_t = sgu_b[0].T
    proj, xhat1, rstd1, mix_bf, x_bf, wgu_t, wdown = _mixer_fwd(
        x2d, win_t, wout, pool_w3, pool_scale, sgu_ln_g, sgu_ln_b, sgu_w3, sgu_b_t, ln1_g, ln1_b,
        [w_gate_up[0].T, w_down[0]], seq)
    dr1, acts_bf, grads_bf, stats = _ffn_fwd_bwd(xhat1, rstd1, tgt2d, wgu_t, wdown, ln1_g, ln1_b, ln2_g, ln2_b)

    bf_gu, own_gu = _wgrad_exchange(grads_bf, acts_bf, 1, "wgrad_gate_up", collective_id=1,
                                    lhs_cols=(0, 2 * D_FF), rhs_block=0, token_tile=1024)
    sent_gu = _send_to_owners_start(bf_gu, "grad_scatter_w_gate_up_start", collective_id=4)
    bf_dn, own_dn = _wgrad_exchange(grads_bf, acts_bf, 2, "wgrad_down", after=sent_gu[3], collective_id=2,
                                    lhs_cols=(2 * D_FF, D_FF), rhs_block=1)
    sent_dn = _send_to_owners_start(bf_dn, "grad_scatter_w_down_start", collective_id=5)
    bf_out, own_out = _wgrad_exchange(mix_bf, acts_bf, N_CHIPS, "wgrad_out", after=sent_dn[3], collective_id=3,
                                      rhs_block=2)
    sent_out = _send_to_owners_start(bf_out, "grad_scatter_w_out_start", collective_id=6)
    gx, dproj_bf, d_mats, d_vecs = _mixer_bwd(
        dr1, proj, win_t, wout, pool_w3, pool_scale, sgu_ln_g, sgu_ln_b, sgu_w3, sgu_b_t, stats, seq)
    bf_in, own_in, mats_all, vecs_all = _wgrad_exchange(
        dproj_bf, x_bf, N_CHIPS, "wgrad_in", gather_rows=[d_mats.reshape(-1, GROUP), d_vecs], after=sent_out[3])
    sent_in = _send_to_owners_start(bf_in, "grad_scatter_w_in_start", collective_id=7)

    grads, deltas, new_m, new_v = {}, {}, {}, {}
    after = sent_in[3]
    for nm, w, m, v, own, sent, transposed in (("w_gate_up", w_gate_up, m_w_gate_up, v_w_gate_up, own_gu, sent_gu, True),
                                               ("w_down", w_down, m_w_down, v_w_down, own_dn, sent_dn, False),
                                               ("w_out", w_out, m_w_out, v_w_out, own_out, sent_out, False),
                                               ("w_in", w_in, m_w_in, v_w_in, own_in, sent_in, True)):
        rows = (lambda a: a[0].T) if transposed else (lambda a: a[0])
        back = (lambda a: a.T[None]) if transposed else (lambda a: a[None])
        rec = _send_to_owners_wait(*sent[:3], after, "grad_scatter_" + nm + "_wait")
        g, d, mo, vo = _sum_adamw(rows(w), rows(m), rows(v), own, rec, "adamw_" + nm)
        after = vo
        grads[nm], deltas[nm], new_m[nm], new_v[nm] = back(g), back(d), back(mo), back(vo)

    small_w = {"pool_w": pool_w, "pool_scale": pool_scale, "sgu_ln_g": sgu_ln_g, "sgu_ln_b": sgu_ln_b, "sgu_w": sgu_w,
               "sgu_b": sgu_b, "ln1_g": ln1_g, "ln1_b": ln1_b, "ln2_g": ln2_g, "ln2_b": ln2_b}
    small_m = {"pool_w": m_pool_w, "pool_scale": m_pool_scale, "sgu_ln_g": m_sgu_ln_g, "sgu_ln_b": m_sgu_ln_b,
               "sgu_w": m_sgu_w, "sgu_b": m_sgu_b, "ln1_g": m_ln1_g, "ln1_b": m_ln1_b, "ln2_g": m_ln2_g, "ln2_b": m_ln2_b}
    small_v = {"pool_w": v_pool_w, "pool_scale": v_pool_scale, "sgu_ln_g": v_sgu_ln_g, "sgu_ln_b": v_sgu_ln_b,
               "sgu_w": v_sgu_w, "sgu_b": v_sgu_b, "ln1_g": v_ln1_g, "ln1_b": v_ln1_b, "ln2_g": v_ln2_g, "ln2_b": v_ln2_b}
    loss_blk, g_small, d_small, m_small, v_small = _small_sum_adamw(
        mats_all.reshape(N_DEV, 2 * N_HEADS, GROUP, GROUP), vecs_all.reshape(N_DEV, VEC_ROWS, D_MODEL),
        small_w, small_m, small_v)
    for vals, dst in ((g_small, grads), (d_small, deltas), (m_small, new_m), (v_small, new_v)):
        dst.update(zip(SMALL_NAMES, vals))

    order = ["w_in", "pool_w", "pool_scale", "sgu_ln_g", "sgu_ln_b", "sgu_w", "sgu_b", "w_out", "ln1_g", "ln1_b",
             "w_gate_up", "w_down", "ln2_g", "ln2_b"]
    return (loss_blk[0, 0], gx.reshape(bl, seq, D_MODEL), *[grads[k] for k in order], *[deltas[k] for k in order],
            *[new_m[k] for k in order], *[new_v[k] for k in order])
```

```python
import jax
import jax.numpy as jnp
from jax import lax
from jax.experimental import pallas as pl
from jax.experimental.pallas import tpu as pltpu

F32 = jnp.float32
BF16 = jnp.bfloat16
MESH = pl.DeviceIdType.MESH

D_MODEL = 1024
POOL_WIDTH = 512
SGU_WIDTH = 512
POOL_WINDOWS = (2, 4, 8, 16)
GROUP = 128
N_HEADS = 4
IN_COLS = POOL_WIDTH + 2 * SGU_WIDTH
D_FF = 2816
LN_EPS = 1e-5
ALPHA = float(2.0 ** 0.25)
HALO = 16
N_DEV = 8
N_CHIPS = 4

ADAM_LR = 0.001
ADAM_B1 = 0.9
ADAM_B2 = 0.999
ADAM_EPS = 1e-08
ADAM_WD = 0.01
ADAM_STEP = 10

VMEM_LIMIT_BYTES = 56 * 1024 * 1024

_SQRT_HALF = 0.7071067811865476
_INV_SQRT_2PI = 0.3989422804014327


def _dot_nn(a, b):
    return lax.dot_general(a, b, (((1,), (0,)), ((), ())), preferred_element_type=F32)


def _dot_nt(a, b):
    return lax.dot_general(a, b, (((1,), (1,)), ((), ())), preferred_element_type=F32)


def _dot_tn(a, b):
    return lax.dot_general(a, b, (((0,), (0,)), ((), ())), preferred_element_type=F32)


def _gelu(x):
    return 0.5 * x * (1.0 + lax.erf(x * _SQRT_HALF))


def _gelu_grad(x):
    return 0.5 * (1.0 + lax.erf(x * _SQRT_HALF)) + x * jnp.exp(-0.5 * x * x) * _INV_SQRT_2PI


def _ln_stats(r):
    mu = jnp.mean(r, axis=-1, keepdims=True)
    d = r - mu
    var = jnp.mean(d * d, axis=-1, keepdims=True)
    rstd = lax.rsqrt(var + LN_EPS)
    return d * rstd, rstd


def _ln_bwd(dout, xhat, rstd, g):
    dxh = dout * g
    m1 = jnp.mean(dxh, axis=-1, keepdims=True)
    m2 = jnp.mean(dxh * xhat, axis=-1, keepdims=True)
    return rstd * (dxh - m1 - xhat * m2)


def _rowsum(a):
    return jnp.sum(a, axis=0, keepdims=True)


def _pool_fwd(xp, xp_prev, inv_cnt, w):
    s = jnp.concatenate([xp_prev, xp], axis=0)
    k = 1
    while k < w:
        s = s + pltpu.roll(s, k, 0)
        k *= 2
    return s[HALO:, :] * inv_cnt - xp


def _pool_bwd(dpooled, dpooled_next, inv_cnt, inv_cnt_next, w):
    n = dpooled.shape[0] + HALO
    s = jnp.concatenate([dpooled * inv_cnt, dpooled_next * inv_cnt_next], axis=0)
    k = 1
    while k < w:
        s = s + pltpu.roll(s, n - k, 0)
        k *= 2
    return s[: dpooled.shape[0], :] - dpooled


def _inv_count(pos, w):
    return 1.0 / jnp.minimum(pos + 1, w).astype(F32)


def _to_head_major(a, h, nc):
    return jnp.concatenate(
        [a[c * GROUP:(c + 1) * GROUP, h * GROUP:(h + 1) * GROUP] for c in range(nc)], axis=1)


def _masked_sgu_w(sw_ref, h):
    row = lax.broadcasted_iota(jnp.int32, (GROUP, GROUP), 0)
    col = lax.broadcasted_iota(jnp.int32, (GROUP, GROUP), 1)
    return jnp.where(row >= col, sw_ref[h], 0.0)


def _row_block(rows, limit):
    return max(b for b in range(16, min(rows, limit) + 1, 16) if rows % b == 0)


def _mesh_position():
    return lax.axis_index("x"), lax.axis_index("y"), lax.axis_index("c")


def _other_chips(x, y):
    return [(1 - x, y), (x, 1 - y), (1 - x, 1 - y)]


class _TwoLevelGather:
    def __init__(self, ins, outs, send_sems, recv_sems, local_sems):
        self.ins, self.outs = ins, outs
        self.send_sems, self.recv_sems, self.local_sems = send_sems, recv_sems, local_sems
        self.na = len(ins)
        x, y, c = _mesh_position()
        self.c = c
        self.me, self.sibling = (x, y, c), (x, y, 1 - c)
        self.chips = _other_chips(x, y)
        self.relay_from = (x + (1 - c) * (1 - 2 * x), y + c * (1 - 2 * y))
        self.relay_to = (x + c * (1 - 2 * x), y + (1 - c) * (1 - 2 * y))

    def _rows(self, a, px, py, pc):
        n = self.ins[a].shape[0]
        return self.outs[a].at[pl.ds((4 * px + 2 * py + pc) * n, n), :]

    def _copy(self, a, k, block, to, src=None):
        return pltpu.make_async_remote_copy(
            src_ref=self._rows(a, *block) if src is None else src, dst_ref=self._rows(a, *block),
            send_sem=self.send_sems.at[a * 7 + k], recv_sem=self.recv_sems.at[a * 7 + k],
            device_id=to, device_id_type=MESH)

    def _mine(self, a):
        return pltpu.make_async_copy(self.ins[a], self._rows(a, *self.me), self.local_sems.at[a])

    def start(self):
        for a in range(self.na):
            self._mine(a).start()
        for a in range(self.na):
            self._copy(a, 0, self.me, self.sibling, src=self.ins[a]).start()
            for j, chip in enumerate(self.chips[:2]):
                self._copy(a, 1 + j, self.me, (*chip, self.c), src=self.ins[a]).start()

    def relay(self, a):
        c, block = self.c, (*self.relay_from, self.c)
        self._copy(a, 1 + c, block, self.me).wait_recv()
        self._copy(a, 3, block, (*self.relay_to, c)).start()
        self._copy(a, 4 + c, block, self.sibling).start()

    def pass_on(self, a):
        c = self.c
        self._copy(a, 2 - c, (*self.relay_to, c), self.me).wait_recv()
        self._copy(a, 5 - c, (*self.relay_to, c), self.sibling).start()
        self._copy(a, 3, (*self.chips[2], c), self.me).wait_recv()
        self._copy(a, 6, (*self.chips[2], c), self.sibling).start()

    def finish(self):
        for a in range(self.na):
            self._copy(a, 0, self.sibling, self.me).wait_recv()
            for j, chip in enumerate(self.chips):
                self._copy(a, 4 + j, (*chip, 1 - self.c), self.me).wait_recv()
        for a in range(self.na):
            for k in range(7):
                self._copy(a, k, self.me, self.sibling, src=self.ins[a]).wait_send()
            self._mine(a).wait()

    @staticmethod
    def scratch(na):
        return [pltpu.SemaphoreType.DMA((7 * na,)), pltpu.SemaphoreType.DMA((7 * na,)), pltpu.SemaphoreType.DMA((na,))]


def _gathered_shape(s):
    return jax.ShapeDtypeStruct((N_DEV * s.shape[0], s.shape[1]), s.dtype)


def _gathered_bf16(s):
    return jax.ShapeDtypeStruct((N_DEV * s.shape[0], s.shape[1]), BF16)


def _all_gather_rows(shards, name):
    na = len(shards)

    def body(*refs):
        bf_refs = refs[2 * na:3 * na]
        for a in range(na):
            bf_refs[a][...] = refs[a][...].astype(BF16)
        gather = _TwoLevelGather(bf_refs, refs[na:2 * na], *refs[3 * na:])
        gather.start()
        for a in range(na):
            gather.relay(a)
        for a in range(na):
            gather.pass_on(a)
        gather.finish()

    return pl.pallas_call(
        body, name=name, out_shape=[_gathered_bf16(s) for s in shards],
        in_specs=[pl.BlockSpec(memory_space=pltpu.VMEM)] * na, out_specs=[pl.BlockSpec(memory_space=pl.ANY)] * na,
        scratch_shapes=[pltpu.VMEM(s.shape, BF16) for s in shards] + _TwoLevelGather.scratch(na),
    )(*shards)


def _mixer_fwd(x2d, win_t, wout, pool_w, pool_scale, sln_g, sln_b, sgu_w, sgu_b_t, ln1_g, ln1_b, later_shards, seq):
    tokens = x2d.shape[0]
    tt = min(512, seq)
    tiles_per_seq = seq // tt
    nc = tt // GROUP
    n_tiles = tokens // tt
    n_later = len(later_shards)

    def body(x_ref, xh_ref, win_ref, wout_ref, pw_ref, ps_ref, lg_ref, lb_ref, sw_ref, sb_ref, g1_ref, b1_ref, *rest):
        shard_refs, rest = rest[:n_later], rest[n_later:]
        proj_ref, xhat_ref, rstd_ref, mix_ref, xbf_ref = rest[:5]
        gathered_refs, rest = rest[5:5 + n_later], rest[5 + n_later:]
        mix_scr, shard_bf_refs, (send_sems, recv_sems, local_sems) = rest[0], rest[1:1 + n_later], rest[1 + n_later:]
        i = pl.program_id(0)
        gather = _TwoLevelGather(shard_bf_refs, gathered_refs, send_sems, recv_sems, local_sems)

        @pl.when(i == 0)
        def _():
            for a in range(n_later):
                shard_bf_refs[a][...] = shard_refs[a][...].astype(BF16)
            gather.start()

        tile_in_seq = i % tiles_per_seq
        x = x_ref[...]
        xb = x.astype(BF16)
        xbf_ref[...] = xb
        proj = _dot_nt(xb, win_ref[...])
        proj_ref[...] = proj
        xp_prev = _dot_nt(xh_ref[...].astype(BF16), win_ref[0:POOL_WIDTH, :])
        xp_prev = jnp.where(tile_in_seq == 0, 0.0, xp_prev)
        pos = tile_in_seq * tt + lax.broadcasted_iota(jnp.int32, (tt, 1), 0)
        for g, w in enumerate(POOL_WINDOWS):
            sl = slice(g * GROUP, (g + 1) * GROUP)
            pooled = _pool_fwd(proj[:, sl], xp_prev[:, sl], _inv_count(pos, w), w)
            pre = _dot_nn(pooled.astype(BF16), pw_ref[g].astype(BF16))
            mix_scr[:, sl] = pre * ps_ref[:, sl]
        u = _gelu(proj[:, POOL_WIDTH:POOL_WIDTH + SGU_WIDTH])
        v = _gelu(proj[:, POOL_WIDTH + SGU_WIDTH:])
        vhat, _ = _ln_stats(v)
        v_ln = vhat * lg_ref[...] + lb_ref[...]
        for h in range(N_HEADS):
            ws = _masked_sgu_w(sw_ref, h).astype(BF16)
            mixed = _dot_nn(ws, _to_head_major(v_ln, h, nc).astype(BF16)) + sb_ref[:, h:h + 1]
            for c in range(nc):
                rs = slice(c * GROUP, (c + 1) * GROUP)
                mix_scr[rs, POOL_WIDTH + h * GROUP:POOL_WIDTH + (h + 1) * GROUP] = (
                    u[rs, h * GROUP:(h + 1) * GROUP] * mixed[:, c * GROUP:(c + 1) * GROUP])
        mixb = mix_scr[...].astype(BF16)
        mix_ref[...] = mixb
        r1 = ALPHA * x + _dot_nn(mixb, wout_ref[...])
        xhat, rstd = _ln_stats(r1)
        xhat_ref[...] = xhat
        rstd_ref[...] = rstd

        for a in range(n_later):
            relay_tile = min(n_tiles // 2 + a, n_tiles - 1)

            @pl.when(i == relay_tile)
            def _(a=a):
                gather.relay(a)

            @pl.when(i == max(n_tiles - n_later + a, relay_tile))
            def _(a=a):
                gather.pass_on(a)

        @pl.when(i == n_tiles - 1)
        def _():
            gather.finish()

    def tile(cols):
        return pl.BlockSpec((tt, cols), lambda i: (i, 0))

    def whole(a):
        nd = a.ndim
        return pl.BlockSpec(a.shape, lambda i: (0,) * nd)

    any_spec = pl.BlockSpec(memory_space=pl.ANY)
    halo = pl.BlockSpec((HALO, D_MODEL), lambda i: (jnp.maximum(i * (tt // HALO) - 1, 0), 0))
    consts = [win_t, wout, pool_w, pool_scale, sln_g, sln_b, sgu_w, sgu_b_t, ln1_g, ln1_b]
    return pl.pallas_call(
        body, name="mixer_fwd", grid=(n_tiles,),
        in_specs=[tile(D_MODEL), halo] + [whole(a) for a in consts] + [whole(s) for s in later_shards],
        out_specs=[tile(IN_COLS), tile(D_MODEL), tile(1), tile(D_MODEL), tile(D_MODEL)] + [any_spec] * n_later,
        out_shape=[jax.ShapeDtypeStruct((tokens, IN_COLS), F32), jax.ShapeDtypeStruct((tokens, D_MODEL), F32),
                   jax.ShapeDtypeStruct((tokens, 1), F32), jax.ShapeDtypeStruct((tokens, D_MODEL), BF16),
                   jax.ShapeDtypeStruct((tokens, D_MODEL), BF16)] + [_gathered_bf16(s) for s in later_shards],
        scratch_shapes=[pltpu.VMEM((tt, D_MODEL), F32)] + [pltpu.VMEM(s.shape, BF16) for s in later_shards]
                       + _TwoLevelGather.scratch(n_later),
        compiler_params=pltpu.CompilerParams(dimension_semantics=("arbitrary",), vmem_limit_bytes=VMEM_LIMIT_BYTES),
    )(x2d, x2d, *consts, *later_shards)


def _ffn_fwd_bwd(xhat1, rstd1, target, wgu_t, wdown, ln1_g, ln1_b, ln2_g, ln2_b):
    tokens = xhat1.shape[0]
    tt = min(256, tokens)

    def body(xhat_ref, rstd_ref, tgt_ref, wgu_hbm, wd_hbm, g1_ref, b1_ref, g2_ref, b2_ref,
             dr1_ref, acts_ref, grads_ref, stats_ref, wgu_ref, wd_ref, gu_scr, sems):
        i = pl.program_id(0)

        @pl.when(i == 0)
        def _():
            loads = [pltpu.make_async_copy(wgu_hbm, wgu_ref, sems.at[0]),
                     pltpu.make_async_copy(wd_hbm, wd_ref, sems.at[1])]
            for cp in loads:
                cp.start()
            stats_ref[...] = jnp.zeros_like(stats_ref)
            for cp in loads:
                cp.wait()

        xhat1_t = xhat_ref[...]
        h = xhat1_t * g1_ref[...] + b1_ref[...]
        hb = h.astype(BF16)
        acts_ref[:, 0:D_MODEL] = hb
        gate = _dot_nt(hb, wgu_ref[0:D_FF, :])
        up = _dot_nt(hb, wgu_ref[D_FF:, :])
        gu_scr[:, 0:D_FF] = gate
        gu_scr[:, D_FF:] = up
        ab = (gate * jax.nn.sigmoid(gate) * up).astype(BF16)
        grads_ref[:, 2 * D_FF:] = ab
        xhat2, rstd2 = _ln_stats(ALPHA * h + _dot_nn(ab, wd_ref[...]))
        err = xhat2 * g2_ref[...] + b2_ref[...] - tgt_ref[...]
        dy = err * (1.0 / D_MODEL)
        stats_ref[0:1, :] += _rowsum(dy * xhat2)
        stats_ref[1:2, :] += _rowsum(dy)
        stats_ref[4:5, :] += _rowsum(err * err)
        dr2 = _ln_bwd(dy, xhat2, rstd2, g2_ref[...])
        dr2b = dr2.astype(BF16)
        acts_ref[:, D_MODEL:2 * D_MODEL] = dr2b
        da = _dot_nt(dr2b, wd_ref[...])
        gate = gu_scr[:, 0:D_FF]
        up = gu_scr[:, D_FF:]
        sg = jax.nn.sigmoid(gate)
        dgate = (da * up * (sg * (1.0 + gate * (1.0 - sg)))).astype(BF16)
        dup = (da * (gate * sg)).astype(BF16)
        grads_ref[:, 0:D_FF] = dgate
        grads_ref[:, D_FF:2 * D_FF] = dup
        dh = ALPHA * dr2 + _dot_nn(dgate, wgu_ref[0:D_FF, :]) + _dot_nn(dup, wgu_ref[D_FF:, :])
        stats_ref[2:3, :] += _rowsum(dh * xhat1_t)
        stats_ref[3:4, :] += _rowsum(dh)
        dr1 = _ln_bwd(dh, xhat1_t, rstd_ref[...], g1_ref[...])
        dr1_ref[...] = dr1
        acts_ref[:, 2 * D_MODEL:] = dr1.astype(BF16)

    def tile(cols):
        return pl.BlockSpec((tt, cols), lambda i: (i, 0))

    def whole(a):
        nd = a.ndim
        return pl.BlockSpec(a.shape, lambda i: (0,) * nd)

    any_spec = pl.BlockSpec(memory_space=pl.ANY)
    vecs = [ln1_g, ln1_b, ln2_g, ln2_b]
    return pl.pallas_call(
        body, name="ffn_fwd_bwd", grid=(tokens // tt,),
        in_specs=[tile(D_MODEL), tile(1), tile(D_MODEL), any_spec, any_spec] + [whole(a) for a in vecs],
        out_specs=[tile(D_MODEL), tile(3 * D_MODEL), tile(3 * D_FF), pl.BlockSpec((8, D_MODEL), lambda i: (0, 0))],
        out_shape=[jax.ShapeDtypeStruct((tokens, D_MODEL), F32), jax.ShapeDtypeStruct((tokens, 3 * D_MODEL), BF16),
                   jax.ShapeDtypeStruct((tokens, 3 * D_FF), BF16), jax.ShapeDtypeStruct((8, D_MODEL), F32)],
        scratch_shapes=[pltpu.VMEM(wgu_t.shape, BF16), pltpu.VMEM(wdown.shape, BF16),
                        pltpu.VMEM((tt, 2 * D_FF), F32), pltpu.SemaphoreType.DMA((2,))],
        compiler_params=pltpu.CompilerParams(dimension_semantics=("arbitrary",), vmem_limit_bytes=VMEM_LIMIT_BYTES),
    )(xhat1, rstd1, target, wgu_t, wdown, *vecs)


def _wgrad_exchange(lhs, rhs, chips_per_block, name, gather_rows=(), after=None, collective_id=None,
                    lhs_cols=None, rhs_block=None, token_tile=2048):
    assert collective_id is None or not gather_rows
    tokens = lhs.shape[0]
    first_col, n_all = (0, lhs.shape[1]) if lhs_cols is None else lhs_cols
    m = rhs.shape[1] if rhs_block is None else D_MODEL
    n = n_all // N_DEV
    tw = min(token_tile, tokens)
    nt = tokens // tw
    cpb = chips_per_block
    nj = N_CHIPS // cpb
    lhs_block0 = first_col // (2 * n * cpb)
    assert lhs_block0 * 2 * n * cpb == first_col
    rhs_col = 0 if rhs_block is None else rhs_block
    ng = len(gather_rows)
    anchors = [] if after is None else [after]

    def body(l_ref, r_ref, *rest):
        small_refs, rest = rest[:ng], rest[ng + len(anchors):]
        bf_ref, own_ref = rest[:2]
        gathered_refs, rest = rest[2:2 + ng], rest[2 + ng:]
        acc, kept, landed, send_sems, recv_sems = rest[:5]
        sendbuf = bf_ref
        j, t = pl.program_id(0), pl.program_id(1)
        first, last = (j == 0) & (t == 0), (j == nj - 1) & (t == nt - 1)
        x, y, c = _mesh_position()
        if collective_id is not None:
            @pl.when(first)
            def _():
                barrier = pltpu.get_barrier_semaphore()
                pl.semaphore_signal(barrier, inc=1, device_id=(x, y, 1 - c), device_id_type=MESH)
                pl.semaphore_wait(barrier, 1)
        if ng:
            gather = _TwoLevelGather(small_refs, gathered_refs, *rest[5:8])

            @pl.when(first)
            def _():
                gather.start()

        def copy(q):
            return pltpu.make_async_remote_copy(
                src_ref=sendbuf.at[q], dst_ref=landed.at[q], send_sem=send_sems.at[q], recv_sem=recv_sems.at[q],
                device_id=(x, y, 1 - c), device_id_type=MESH)

        @pl.when(t == 0)
        def _():
            acc[...] = jnp.zeros_like(acc)

        acc[...] += _dot_tn(l_ref[...], r_ref[...])

        @pl.when(t == nt - 1)
        def _():
            for qq in range(cpb):
                q = j * cpb + qq
                kept[q] = acc[pl.ds(pl.multiple_of(qq * 2 * n + c * n, 8), n), :]
                sendbuf[q] = acc[pl.ds(pl.multiple_of(qq * 2 * n + (1 - c) * n, 8), n), :].astype(BF16)
                copy(q).start()

        if ng:
            @pl.when((j == nj - 1) & (t == nt // 2))
            def _():
                for a in range(ng):
                    gather.relay(a)

            @pl.when(last)
            def _():
                for a in range(ng):
                    gather.pass_on(a)
                gather.finish()

        @pl.when(last)
        def _():
            for q in range(N_CHIPS):
                copy(q).wait_send()
                copy(q).wait_recv()
            for q in range(N_CHIPS):
                s = kept[q] + landed[q].astype(F32)
                bf_ref[q] = s.astype(BF16)

                @pl.when(q == 2 * x + y)
                def _(s=s):
                    own_ref[...] = s

    any_spec = pl.BlockSpec(memory_space=pl.ANY)
    by_chip = (N_CHIPS, n, m)
    return pl.pallas_call(
        body, name=name, grid=(nj, nt),
        in_specs=[pl.BlockSpec((tw, 2 * n * cpb), lambda j, t: (t, lhs_block0 + j)),
                  pl.BlockSpec((tw, m), lambda j, t: (t, rhs_col))] + [any_spec] * (ng + len(anchors)),
        out_specs=[pl.BlockSpec(by_chip, lambda j, t: (0, 0, 0)), pl.BlockSpec((n, m), lambda j, t: (0, 0))]
                  + [any_spec] * ng,
        out_shape=[jax.ShapeDtypeStruct(by_chip, BF16), jax.ShapeDtypeStruct((n, m), F32)]
                  + [_gathered_shape(s) for s in gather_rows],
        scratch_shapes=[pltpu.VMEM((2 * n * cpb, m), F32), pltpu.VMEM(by_chip, F32), pltpu.VMEM(by_chip, BF16),
                        pltpu.SemaphoreType.DMA((N_CHIPS,)), pltpu.SemaphoreType.DMA((N_CHIPS,))]
                       + (_TwoLevelGather.scratch(ng) if ng else []),
        compiler_params=pltpu.CompilerParams(dimension_semantics=("arbitrary", "arbitrary"),
                                             vmem_limit_bytes=VMEM_LIMIT_BYTES + 4 * 1024 * 1024,
                                             collective_id=collective_id),
    )(lhs, rhs, *gather_rows, *anchors)


def _received_shape(p):
    return jax.ShapeDtypeStruct((3,) + p.shape[1:], p.dtype)


ROW_POOL_SCALE, ROW_SLN_G, ROW_SLN_B, ROW_SGU_B = 0, 1, 2, 3
ROW_LN2_G, ROW_LN2_B, ROW_LN1_G, ROW_LN1_B, ROW_LOSS = 8, 9, 10, 11, 12
VEC_ROWS = 16


def _mixer_bwd(dr1, proj, win_t, wout, pool_w, pool_scale, sln_g, sln_b, sgu_w, sgu_b_t, stats, seq):
    tokens = dr1.shape[0]
    tt = min(512, seq)
    tiles_per_seq = seq // tt
    nc = tt // GROUP
    n_halo_blocks = tokens // HALO
    n_tiles = tokens // tt

    def body(dr1_ref, dr1n_ref, proj_ref, projh_ref, win_ref, wout_ref, pw_ref, ps_ref, lg_ref, lb_ref, sw_ref, sb_ref,
             stats_ref, gx_ref, dproj_ref, dmat_ref, dvec_ref, du_scr, dv_scr):
        i = pl.program_id(0)
        tile_in_seq = i % tiles_per_seq

        @pl.when(i == 0)
        def _():
            dmat_ref[...] = jnp.zeros_like(dmat_ref)
            dvec_ref[0:8, :] = jnp.zeros((8, D_MODEL), F32)
            dvec_ref[8:16, :] = stats_ref[...]

        dr1_t = dr1_ref[...]
        dr1b = dr1_t.astype(BF16)
        dmix = _dot_nt(dr1b, wout_ref[...])
        dpo_next = _dot_nt(dr1n_ref[...].astype(BF16), wout_ref[0:POOL_WIDTH, :])
        dpo_next = jnp.where(tile_in_seq == tiles_per_seq - 1, 0.0, dpo_next)
        proj = proj_ref[...]
        xp_prev = jnp.where(tile_in_seq == 0, 0.0, projh_ref[...])
        pos = tile_in_seq * tt + lax.broadcasted_iota(jnp.int32, (tt, 1), 0)
        pos_next = (tile_in_seq + 1) * tt + lax.broadcasted_iota(jnp.int32, (HALO, 1), 0)

        for g, w in enumerate(POOL_WINDOWS):
            sl = slice(g * GROUP, (g + 1) * GROUP)
            inv_cnt = _inv_count(pos, w)
            pwb = pw_ref[g].astype(BF16)
            pooledb = _pool_fwd(proj[:, sl], xp_prev[:, sl], inv_cnt, w).astype(BF16)
            pre = _dot_nn(pooledb, pwb)
            dpo = dmix[:, sl]
            dvec_ref[ROW_POOL_SCALE:ROW_POOL_SCALE + 1, sl] += _rowsum(dpo * pre)
            dsb = (dpo * ps_ref[:, sl]).astype(BF16)
            dmat_ref[g] += _dot_tn(pooledb, dsb)
            dpooled = _dot_nt(dsb, pwb)
            dpooled_next = _dot_nt((dpo_next[:, sl] * ps_ref[:, sl]).astype(BF16), pwb)
            dxp = _pool_bwd(dpooled, dpooled_next, inv_cnt, _inv_count(pos_next, w), w)
            dproj_ref[:, sl] = dxp.astype(BF16)

        zu = proj[:, POOL_WIDTH:POOL_WIDTH + SGU_WIDTH]
        zv = proj[:, POOL_WIDTH + SGU_WIDTH:]
        u = _gelu(zu)
        vhat, rstd_v = _ln_stats(_gelu(zv))
        v_ln = vhat * lg_ref[...] + lb_ref[...]
        dsg = dmix[:, POOL_WIDTH:]
        row = lax.broadcasted_iota(jnp.int32, (GROUP, GROUP), 0)
        col = lax.broadcasted_iota(jnp.int32, (GROUP, GROUP), 1)
        for h in range(N_HEADS):
            ws = _masked_sgu_w(sw_ref, h).astype(BF16)
            vh = _to_head_major(v_ln, h, nc).astype(BF16)
            mixed = _dot_nn(ws, vh) + sb_ref[:, h:h + 1]
            dsg_h = _to_head_major(dsg, h, nc)
            du_h = dsg_h * mixed
            dm_h = dsg_h * _to_head_major(u, h, nc)
            pos_sums = lax.dot_general(jnp.ones((8, nc * GROUP), F32), dm_h, (((1,), (1,)), ((), ())),
                                       precision=lax.Precision.HIGH, preferred_element_type=F32)
            dvec_ref[ROW_SGU_B + h:ROW_SGU_B + h + 1, 0:GROUP] += pos_sums[0:1, :]
            dmb = dm_h.astype(BF16)
            dmat_ref[len(POOL_WINDOWS) + h] += jnp.where(row >= col, _dot_nt(dmb, vh), 0.0)
            dv_h = _dot_tn(ws, dmb)
            for c in range(nc):
                rs = slice(c * GROUP, (c + 1) * GROUP)
                cs = slice(h * GROUP, (h + 1) * GROUP)
                du_scr[rs, cs] = du_h[:, c * GROUP:(c + 1) * GROUP]
                dv_scr[rs, cs] = dv_h[:, c * GROUP:(c + 1) * GROUP]
        dv_ln = dv_scr[...]
        dvec_ref[ROW_SLN_B:ROW_SLN_B + 1, 0:SGU_WIDTH] += _rowsum(dv_ln)
        dvec_ref[ROW_SLN_G:ROW_SLN_G + 1, 0:SGU_WIDTH] += _rowsum(dv_ln * vhat)
        dv = _ln_bwd(dv_ln, vhat, rstd_v, lg_ref[...])
        dproj_ref[:, POOL_WIDTH:POOL_WIDTH + SGU_WIDTH] = (du_scr[...] * _gelu_grad(zu)).astype(BF16)
        dproj_ref[:, POOL_WIDTH + SGU_WIDTH:] = (dv * _gelu_grad(zv)).astype(BF16)
        gx_ref[...] = ALPHA * dr1_t + _dot_nn(dproj_ref[...], win_ref[...])

    def tile(cols):
        return pl.BlockSpec((tt, cols), lambda i: (i, 0))

    def whole(a):
        nd = a.ndim
        return pl.BlockSpec(a.shape, lambda i: (0,) * nd)

    def resident(shape):
        nd = len(shape)
        return pl.BlockSpec(shape, lambda i: (0,) * nd)

    next_halo = pl.BlockSpec((HALO, D_MODEL), lambda i: (jnp.minimum((i + 1) * (tt // HALO), n_halo_blocks - 1), 0))
    prev_halo = pl.BlockSpec((HALO, POOL_WIDTH), lambda i: (jnp.maximum(i * (tt // HALO) - 1, 0), 0))
    consts = [win_t, wout, pool_w, pool_scale, sln_g, sln_b, sgu_w, sgu_b_t, stats]
    small_shapes = [(len(POOL_WINDOWS) + N_HEADS, GROUP, GROUP), (VEC_ROWS, D_MODEL)]
    return pl.pallas_call(
        body, name="mixer_bwd", grid=(n_tiles,),
        in_specs=[tile(D_MODEL), next_halo, tile(IN_COLS), prev_halo] + [whole(a) for a in consts],
        out_specs=[tile(D_MODEL), tile(IN_COLS)] + [resident(s) for s in small_shapes],
        out_shape=[jax.ShapeDtypeStruct((tokens, D_MODEL), F32), jax.ShapeDtypeStruct((tokens, IN_COLS), BF16)]
                  + [jax.ShapeDtypeStruct(s, F32) for s in small_shapes],
        scratch_shapes=[pltpu.VMEM((tt, SGU_WIDTH), F32), pltpu.VMEM((tt, SGU_WIDTH), F32)],
        compiler_params=pltpu.CompilerParams(dimension_semantics=("arbitrary",), vmem_limit_bytes=VMEM_LIMIT_BYTES),
    )(dr1, dr1, proj, proj, *consts)


def _owner_copies(src_ref, land_ref, sems):
    x, y, c = _mesh_position()
    return [pltpu.make_async_remote_copy(
        src_ref=src_ref.at[2 * cx + cy], dst_ref=land_ref.at[j], send_sem=sems[j], recv_sem=sems[3 + j],
        device_id=(cx, cy, c), device_id_type=MESH) for j, (cx, cy) in enumerate(_other_chips(x, y))]


def _send_to_owners_start(chip_partial, name, collective_id):
    land = _received_shape(chip_partial)

    def body(src_ref, land_ref, *rest):
        x, y, c = _mesh_position()
        barrier = pltpu.get_barrier_semaphore()
        for cx, cy in _other_chips(x, y):
            pl.semaphore_signal(barrier, inc=1, device_id=(cx, cy, c), device_id_type=MESH)
        pl.semaphore_wait(barrier, 3)
        for cp in _owner_copies(src_ref, land_ref, rest[:6]):
            cp.start()
        rest[8][...] = jnp.zeros_like(rest[8])

    hbm = pl.BlockSpec(memory_space=pltpu.HBM)
    sem = pl.BlockSpec(memory_space=pltpu.SEMAPHORE)
    outs = pl.pallas_call(
        body, name=name,
        out_shape=[pltpu.SemaphoreType.DMA(())] * 6 + [pltpu.HBM(chip_partial.shape, chip_partial.dtype),
                                                       pltpu.HBM(land.shape, land.dtype),
                                                       jax.ShapeDtypeStruct((8, GROUP), F32)],
        in_specs=[hbm, hbm], out_specs=[sem] * 6 + [hbm, hbm, pl.BlockSpec(memory_space=pltpu.VMEM)],
        input_output_aliases={0: 6, 1: 7},
        compiler_params=pltpu.CompilerParams(has_side_effects=pltpu.SideEffectType.DATAFLOW_SIDE_EFFECTING,
                                             collective_id=collective_id),
    )(pltpu.with_memory_space_constraint(chip_partial, pltpu.HBM),
      pltpu.with_memory_space_constraint(lax.empty(land.shape, land.dtype), pltpu.HBM))
    return outs[:6], outs[6], outs[7], outs[8]


def _send_to_owners_wait(sems, src_thru, land_thru, after, name):
    def body(src_ref, land_ref, *rest):
        for cp in _owner_copies(src_ref, land_ref, rest[:6]):
            cp.wait_send()
            cp.wait_recv()

    hbm = pl.BlockSpec(memory_space=pltpu.HBM)
    sem = pl.BlockSpec(memory_space=pltpu.SEMAPHORE)
    return pl.pallas_call(
        body, name=name,
        out_shape=[pltpu.HBM(src_thru.shape, src_thru.dtype), pltpu.HBM(land_thru.shape, land_thru.dtype)],
        in_specs=[hbm, hbm] + [sem] * 6 + [pl.BlockSpec(memory_space=pl.ANY)], out_specs=[hbm, hbm],
        input_output_aliases={0: 0, 1: 1},
        compiler_params=pltpu.CompilerParams(has_side_effects=pltpu.SideEffectType.DATAFLOW_SIDE_EFFECTING),
    )(src_thru, land_thru, *sems, after)[1]


def _adamw_math(w, g, m, v):
    m = ADAM_B1 * m + (1.0 - ADAM_B1) * g
    v = ADAM_B2 * v + (1.0 - ADAM_B2) * (g * g)
    m_hat = m / (1.0 - ADAM_B1 ** ADAM_STEP)
    v_hat = v / (1.0 - ADAM_B2 ** ADAM_STEP)
    delta = -ADAM_LR * (m_hat / (jnp.sqrt(v_hat) + ADAM_EPS) + ADAM_WD * w)
    return delta, m, v


def _sum_adamw(w, m, v, own, received, name):
    rows, cols = w.shape
    rb = _row_block(rows, 256)

    def body(w_ref, m_ref, v_ref, own_ref, rec_ref, g_ref, d_ref, mo_ref, vo_ref):
        g = own_ref[...]
        for j in range(3):
            g = g + rec_ref[j].astype(F32)
        g_ref[...] = g
        d_ref[...], mo_ref[...], vo_ref[...] = _adamw_math(w_ref[...], g, m_ref[...], v_ref[...])

    spec = pl.BlockSpec((rb, cols), lambda r: (r, 0))
    return pl.pallas_call(
        body, name=name, grid=(rows // rb,),
        in_specs=[spec] * 4 + [pl.BlockSpec((3, rb, cols), lambda r: (0, r, 0))],
        out_specs=[spec] * 4, out_shape=[jax.ShapeDtypeStruct((rows, cols), F32)] * 4,
        compiler_params=pltpu.CompilerParams(dimension_semantics=("arbitrary",)),
    )(w, m, v, own, received)


SMALL_NAMES = ("pool_w", "sgu_w", "pool_scale", "sgu_ln_g", "sgu_ln_b", "sgu_b", "ln1_g", "ln1_b", "ln2_g", "ln2_b")
_SMALL_VEC_ROWS = {"pool_scale": (ROW_POOL_SCALE, POOL_WIDTH), "sgu_ln_g": (ROW_SLN_G, SGU_WIDTH),
                   "sgu_ln_b": (ROW_SLN_B, SGU_WIDTH), "ln1_g": (ROW_LN1_G, D_MODEL), "ln1_b": (ROW_LN1_B, D_MODEL),
                   "ln2_g": (ROW_LN2_G, D_MODEL), "ln2_b": (ROW_LN2_B, D_MODEL)}
_SMALL_MAT_FIRST = {"pool_w": 0, "sgu_w": len(POOL_WINDOWS)}


def _small_sum_adamw(mats_all, vecs_all, w, m, v):
    n = len(SMALL_NAMES)

    def body(mats_ref, vecs_ref, *refs):
        w_refs, m_refs, v_refs = refs[:n], refs[n:2 * n], refs[2 * n:3 * n]
        loss_ref = refs[3 * n]
        g_refs, d_refs, mo_refs, vo_refs = (refs[3 * n + 1 + k * n:3 * n + 1 + (k + 1) * n] for k in range(4))
        vec_scr = refs[7 * n + 1]

        def update(k, idx, g):
            d, mo, vo = _adamw_math(w_refs[k][idx], g, m_refs[k][idx], v_refs[k][idx])
            g_refs[k][idx], d_refs[k][idx], mo_refs[k][idx], vo_refs[k][idx] = g, d, mo, vo

        total = vecs_ref[0]
        for dev in range(1, N_DEV):
            total = total + vecs_ref[dev]
        vec_scr[...] = total
        for k, name in enumerate(SMALL_NAMES):
            if name in _SMALL_MAT_FIRST:
                for b in range(4):
                    g = mats_ref[0, _SMALL_MAT_FIRST[name] + b]
                    for dev in range(1, N_DEV):
                        g = g + mats_ref[dev, _SMALL_MAT_FIRST[name] + b]
                    update(k, (0, b), g)
            elif name == "sgu_b":
                update(k, (0,), vec_scr[ROW_SGU_B:ROW_SGU_B + N_HEADS, 0:GROUP])
            else:
                row, width = _SMALL_VEC_ROWS[name]
                update(k, (slice(None), slice(None)), vec_scr[row:row + 1, 0:width])
        loss = jnp.sum(vec_scr[ROW_LOSS:ROW_LOSS + 1, :], axis=1, keepdims=True) * (0.5 / D_MODEL)
        loss_ref[...] = jnp.broadcast_to(loss, loss_ref.shape)

    vmem = pl.BlockSpec(memory_space=pltpu.VMEM)
    shapes = [jax.ShapeDtypeStruct(w[k].shape, F32) for k in SMALL_NAMES]
    outs = pl.pallas_call(
        body, name="small_sum_adamw",
        out_shape=[jax.ShapeDtypeStruct((8, GROUP), F32)] + shapes * 4,
        in_specs=[vmem] * (2 + 3 * n), out_specs=[vmem] * (1 + 4 * n),
        scratch_shapes=[pltpu.VMEM((VEC_ROWS, D_MODEL), F32)],
    )(mats_all, vecs_all, *[w[k] for k in SMALL_NAMES], *[m[k] for k in SMALL_NAMES], *[v[k] for k in SMALL_NAMES])
    return outs[0], outs[1:1 + n], outs[1 + n:1 + 2 * n], outs[1 + 2 * n:1 + 3 * n], outs[1 + 3 * n:]


def kernel(x, w_in, pool_w, pool_scale, sgu_ln_g, sgu_ln_b, sgu_w, sgu_b, w_out, ln1_g, ln1_b, w_gate_up, w_down, ln2_g, ln2_b, loss_target, m_w_in, m_pool_w, m_pool_scale, m_sgu_ln_g, m_sgu_ln_b, m_sgu_w, m_sgu_b, m_w_out, m_ln1_g, m_ln1_b, m_w_gate_up, m_w_down, m_ln2_g, m_ln2_b, v_w_in, v_pool_w, v_pool_scale, v_sgu_ln_g, v_sgu_ln_b, v_sgu_w, v_sgu_b, v_w_out, v_ln1_g, v_ln1_b, v_w_gate_up, v_w_down, v_ln2_g, v_ln2_b):
    bl, seq, _ = x.shape
    tokens = bl * seq
    x2d = x.reshape(tokens, D_MODEL)
    tgt2d = loss_target.reshape(tokens, D_MODEL)

    win_t, wout = _all_gather_rows([w_in[0].T, w_out[0]], "weight_all_gather")

    pool_w3, sgu_w3 = pool_w[0], sgu_w[0]
    sgu_b_t = sgu_b[0].T
    proj, xhat1, rstd1, mix_bf, x_bf, wgu_t, wdown = _mixer_fwd(
        x2d, win_t, wout, pool_w3, pool_scale, sgu_ln_g, sgu_ln_b, sgu_w3, sgu_b_t, ln1_g, ln1_b,
        [w_gate_up[0].T, w_down[0]], seq)
    dr1, acts_bf, grads_bf, stats = _ffn_fwd_bwd(xhat1, rstd1, tgt2d, wgu_t, wdown, ln1_g, ln1_b, ln2_g, ln2_b)

    bf_gu, own_gu = _wgrad_exchange(grads_bf, acts_bf, 1, "wgrad_gate_up", collective_id=1,
                                    lhs_cols=(0, 2 * D_FF), rhs_block=0)
    sent_gu = _send_to_owners_start(bf_gu, "grad_scatter_w_gate_up_start", collective_id=4)
    bf_dn, own_dn = _wgrad_exchange(grads_bf, acts_bf, 2, "wgrad_down", after=sent_gu[3], collective_id=2,
                                    lhs_cols=(2 * D_FF, D_FF), rhs_block=1)
    sent_dn = _send_to_owners_start(bf_dn, "grad_scatter_w_down_start", collective_id=5)
    bf_out, own_out = _wgrad_exchange(mix_bf, acts_bf, N_CHIPS, "wgrad_out", after=sent_dn[3], collective_id=3,
                                      rhs_block=2)
    sent_out = _send_to_owners_start(bf_out, "grad_scatter_w_out_start", collective_id=6)
    gx, dproj_bf, d_mats, d_vecs = _mixer_bwd(
        dr1, proj, win_t, wout, pool_w3, pool_scale, sgu_ln_g, sgu_ln_b, sgu_w3, sgu_b_t, stats, seq)
    bf_in, own_in, mats_all, vecs_all = _wgrad_exchange(
        dproj_bf, x_bf, N_CHIPS, "wgrad_in", gather_rows=[d_mats.reshape(-1, GROUP), d_vecs], after=sent_out[3])
    sent_in = _send_to_owners_start(bf_in, "grad_scatter_w_in_start", collective_id=7)

    grads, deltas, new_m, new_v = {}, {}, {}, {}
    after = sent_in[3]
    for nm, w, m, v, own, sent, transposed in (("w_gate_up", w_gate_up, m_w_gate_up, v_w_gate_up, own_gu, sent_gu, True),
                                               ("w_down", w_down, m_w_down, v_w_down, own_dn, sent_dn, False),
                                               ("w_out", w_out, m_w_out, v_w_out, own_out, sent_out, False),
                                               ("w_in", w_in, m_w_in, v_w_in, own_in, sent_in, True)):
        rows = (lambda a: a[0].T) if transposed else (lambda a: a[0])
        back = (lambda a: a.T[None]) if transposed else (lambda a: a[None])
        rec = _send_to_owners_wait(*sent[:3], after, "grad_scatter_" + nm + "_wait")
        g, d, mo, vo = _sum_adamw(rows(w), rows(m), rows(v), own, rec, "adamw_" + nm)
        after = vo
        grads[nm], deltas[nm], new_m[nm], new_v[nm] = back(g), back(d), back(mo), back(vo)

    small_w = {"pool_w": pool_w, "pool_scale": pool_scale, "sgu_ln_g": sgu_ln_g, "sgu_ln_b": sgu_ln_b, "sgu_w": sgu_w,
               "sgu_b": sgu_b, "ln1_g": ln1_g, "ln1_b": ln1_b, "ln2_g": ln2_g, "ln2_b": ln2_b}
    small_m = {"pool_w": m_pool_w, "pool_scale": m_pool_scale, "sgu_ln_g": m_sgu_ln_g, "sgu_ln_b": m_sgu_ln_b,
               "sgu_w": m_sgu_w, "sgu_b": m_sgu_b, "ln1_g": m_ln1_g, "ln1_b": m_ln1_b, "ln2_g": m_ln2_g, "ln2_b": m_ln2_b}
    small_v = {"pool_w": v_pool_w, "pool_scale": v_pool_scale, "sgu_ln_g": v_sgu_ln_g, "sgu_ln_b": v_sgu_ln_b,
               "sgu_w": v_sgu_w, "sgu_b": v_sgu_b, "ln1_g": v_ln1_g, "ln1_b": v_ln1_b, "ln2_g": v_ln2_g, "ln2_b": v_ln2_b}
    loss_blk, g_small, d_small, m_small, v_small = _small_sum_adamw(
        mats_all.reshape(N_DEV, 2 * N_HEADS, GROUP, GROUP), vecs_all.reshape(N_DEV, VEC_ROWS, D_MODEL),
        small_w, small_m, small_v)
    for vals, dst in ((g_small, grads), (d_small, deltas), (m_small, new_m), (v_small, new_v)):
        dst.update(zip(SMALL_NAMES, vals))

    order = ["w_in", "pool_w", "pool_scale", "sgu_ln_g", "sgu_ln_b", "sgu_w", "sgu_b", "w_out", "ln1_g", "ln1_b",
             "w_gate_up", "w_down", "ln2_g", "ln2_b"]
    return (loss_blk[0, 0], gx.reshape(bl, seq, D_MODEL), *[grads[k] for k in order], *[deltas[k] for k in order],
            *[new_m[k] for k in order], *[new_v[k] for k in order])
```

```python
import jax
import jax.numpy as jnp
from jax import lax
from jax.experimental import pallas as pl
from jax.experimental.pallas import tpu as pltpu

F32 = jnp.float32
BF16 = jnp.bfloat16
MESH = pl.DeviceIdType.MESH

D_MODEL = 1024
POOL_WIDTH = 512
SGU_WIDTH = 512
POOL_WINDOWS = (2, 4, 8, 16)
GROUP = 128
N_HEADS = 4
IN_COLS = POOL_WIDTH + 2 * SGU_WIDTH
D_FF = 2816
LN_EPS = 1e-5
ALPHA = float(2.0 ** 0.25)
HALO = 16
N_DEV = 8
N_CHIPS = 4

ADAM_LR = 0.001
ADAM_B1 = 0.9
ADAM_B2 = 0.999
ADAM_EPS = 1e-08
ADAM_WD = 0.01
ADAM_STEP = 10

VMEM_LIMIT_BYTES = 56 * 1024 * 1024

_SQRT_HALF = 0.7071067811865476
_INV_SQRT_2PI = 0.3989422804014327


def _dot_nn(a, b):
    return lax.dot_general(a, b, (((1,), (0,)), ((), ())), preferred_element_type=F32)


def _dot_nt(a, b):
    return lax.dot_general(a, b, (((1,), (1,)), ((), ())), preferred_element_type=F32)


def _dot_tn(a, b):
    return lax.dot_general(a, b, (((0,), (0,)), ((), ())), preferred_element_type=F32)


def _gelu(x):
    return 0.5 * x * (1.0 + lax.erf(x * _SQRT_HALF))


def _gelu_grad(x):
    return 0.5 * (1.0 + lax.erf(x * _SQRT_HALF)) + x * jnp.exp(-0.5 * x * x) * _INV_SQRT_2PI


def _ln_stats(r):
    mu = jnp.mean(r, axis=-1, keepdims=True)
    d = r - mu
    var = jnp.mean(d * d, axis=-1, keepdims=True)
    rstd = lax.rsqrt(var + LN_EPS)
    return d * rstd, rstd


def _ln_bwd(dout, xhat, rstd, g):
    dxh = dout * g
    m1 = jnp.mean(dxh, axis=-1, keepdims=True)
    m2 = jnp.mean(dxh * xhat, axis=-1, keepdims=True)
    return rstd * (dxh - m1 - xhat * m2)


def _rowsum(a):
    return jnp.sum(a, axis=0, keepdims=True)


def _pool_fwd(xp, xp_prev, inv_cnt, w):
    s = jnp.concatenate([xp_prev, xp], axis=0)
    k = 1
    while k < w:
        s = s + pltpu.roll(s, k, 0)
        k *= 2
    return s[HALO:, :] * inv_cnt - xp


def _pool_bwd(dpooled, dpooled_next, inv_cnt, inv_cnt_next, w):
    n = dpooled.shape[0] + HALO
    s = jnp.concatenate([dpooled * inv_cnt, dpooled_next * inv_cnt_next], axis=0)
    k = 1
    while k < w:
        s = s + pltpu.roll(s, n - k, 0)
        k *= 2
    return s[: dpooled.shape[0], :] - dpooled


def _inv_count(pos, w):
    return 1.0 / jnp.minimum(pos + 1, w).astype(F32)


def _to_head_major(a, h, nc):
    return jnp.concatenate(
        [a[c * GROUP:(c + 1) * GROUP, h * GROUP:(h + 1) * GROUP] for c in range(nc)], axis=1)


def _masked_sgu_w(sw_ref, h):
    row = lax.broadcasted_iota(jnp.int32, (GROUP, GROUP), 0)
    col = lax.broadcasted_iota(jnp.int32, (GROUP, GROUP), 1)
    return jnp.where(row >= col, sw_ref[h], 0.0)


def _row_block(rows, limit):
    return max(b for b in range(16, min(rows, limit) + 1, 16) if rows % b == 0)


def _mesh_position():
    return lax.axis_index("x"), lax.axis_index("y"), lax.axis_index("c")


def _other_chips(x, y):
    return [(1 - x, y), (x, 1 - y), (1 - x, 1 - y)]


class _TwoLevelGather:
    def __init__(self, ins, outs, send_sems, recv_sems, local_sems):
        self.ins, self.outs = ins, outs
        self.send_sems, self.recv_sems, self.local_sems = send_sems, recv_sems, local_sems
        self.na = len(ins)
        x, y, c = _mesh_position()
        self.c = c
        self.me, self.sibling = (x, y, c), (x, y, 1 - c)
        self.chips = _other_chips(x, y)
        self.relay_from = (x + (1 - c) * (1 - 2 * x), y + c * (1 - 2 * y))
        self.relay_to = (x + c * (1 - 2 * x), y + (1 - c) * (1 - 2 * y))

    def _rows(self, a, px, py, pc):
        n = self.ins[a].shape[0]
        return self.outs[a].at[pl.ds((4 * px + 2 * py + pc) * n, n), :]

    def _copy(self, a, k, block, to, src=None):
        return pltpu.make_async_remote_copy(
            src_ref=self._rows(a, *block) if src is None else src, dst_ref=self._rows(a, *block),
            send_sem=self.send_sems.at[a * 7 + k], recv_sem=self.recv_sems.at[a * 7 + k],
            device_id=to, device_id_type=MESH)

    def _mine(self, a):
        return pltpu.make_async_copy(self.ins[a], self._rows(a, *self.me), self.local_sems.at[a])

    def start(self):
        for a in range(self.na):
            self._mine(a).start()
        for a in range(self.na):
            self._copy(a, 0, self.me, self.sibling, src=self.ins[a]).start()
            for j, chip in enumerate(self.chips[:2]):
                self._copy(a, 1 + j, self.me, (*chip, self.c), src=self.ins[a]).start()

    def relay(self, a):
        c, block = self.c, (*self.relay_from, self.c)
        self._copy(a, 1 + c, block, self.me).wait_recv()
        self._copy(a, 3, block, (*self.relay_to, c)).start()
        self._copy(a, 4 + c, block, self.sibling).start()

    def pass_on(self, a):
        c = self.c
        self._copy(a, 2 - c, (*self.relay_to, c), self.me).wait_recv()
        self._copy(a, 5 - c, (*self.relay_to, c), self.sibling).start()
        self._copy(a, 3, (*self.chips[2], c), self.me).wait_recv()
        self._copy(a, 6, (*self.chips[2], c), self.sibling).start()

    def finish(self):
        for a in range(self.na):
            self._copy(a, 0, self.sibling, self.me).wait_recv()
            for j, chip in enumerate(self.chips):
                self._copy(a, 4 + j, (*chip, 1 - self.c), self.me).wait_recv()
        for a in range(self.na):
            for k in range(7):
                self._copy(a, k, self.me, self.sibling, src=self.ins[a]).wait_send()
            self._mine(a).wait()

    @staticmethod
    def scratch(na):
        return [pltpu.SemaphoreType.DMA((7 * na,)), pltpu.SemaphoreType.DMA((7 * na,)), pltpu.SemaphoreType.DMA((na,))]


def _gathered_shape(s):
    return jax.ShapeDtypeStruct((N_DEV * s.shape[0], s.shape[1]), s.dtype)


def _gathered_bf16(s):
    return jax.ShapeDtypeStruct((N_DEV * s.shape[0], s.shape[1]), BF16)


def _all_gather_rows(shards, name):
    na = len(shards)

    def body(*refs):
        bf_refs = refs[2 * na:3 * na]
        for a in range(na):
            bf_refs[a][...] = refs[a][...].astype(BF16)
        gather = _TwoLevelGather(bf_refs, refs[na:2 * na], *refs[3 * na:])
        gather.start()
        for a in range(na):
            gather.relay(a)
        for a in range(na):
            gather.pass_on(a)
        gather.finish()

    return pl.pallas_call(
        body, name=name, out_shape=[_gathered_bf16(s) for s in shards],
        in_specs=[pl.BlockSpec(memory_space=pltpu.VMEM)] * na, out_specs=[pl.BlockSpec(memory_space=pl.ANY)] * na,
        scratch_shapes=[pltpu.VMEM(s.shape, BF16) for s in shards] + _TwoLevelGather.scratch(na),
    )(*shards)


def _mixer_fwd(x2d, win_t, wout, pool_w, pool_scale, sln_g, sln_b, sgu_w, sgu_b_t, ln1_g, ln1_b, later_shards, seq):
    tokens = x2d.shape[0]
    tt = min(512, seq)
    tiles_per_seq = seq // tt
    nc = tt // GROUP
    n_tiles = tokens // tt
    n_later = len(later_shards)

    def body(x_ref, xh_ref, win_ref, wout_ref, pw_ref, ps_ref, lg_ref, lb_ref, sw_ref, sb_ref, g1_ref, b1_ref, *rest):
        shard_refs, rest = rest[:n_later], rest[n_later:]
        proj_ref, xhat_ref, rstd_ref, mix_ref, xbf_ref = rest[:5]
        gathered_refs, rest = rest[5:5 + n_later], rest[5 + n_later:]
        mix_scr, shard_bf_refs, (send_sems, recv_sems, local_sems) = rest[0], rest[1:1 + n_later], rest[1 + n_later:]
        i = pl.program_id(0)
        gather = _TwoLevelGather(shard_bf_refs, gathered_refs, send_sems, recv_sems, local_sems)

        @pl.when(i == 0)
        def _():
            for a in range(n_later):
                shard_bf_refs[a][...] = shard_refs[a][...].astype(BF16)
            gather.start()

        tile_in_seq = i % tiles_per_seq
        x = x_ref[...]
        xb = x.astype(BF16)
        xbf_ref[...] = xb
        proj = _dot_nt(xb, win_ref[...])
        proj_ref[...] = proj
        xp_prev = _dot_nt(xh_ref[...].astype(BF16), win_ref[0:POOL_WIDTH, :])
        xp_prev = jnp.where(tile_in_seq == 0, 0.0, xp_prev)
        pos = tile_in_seq * tt + lax.broadcasted_iota(jnp.int32, (tt, 1), 0)
        for g, w in enumerate(POOL_WINDOWS):
            sl = slice(g * GROUP, (g + 1) * GROUP)
            pooled = _pool_fwd(proj[:, sl], xp_prev[:, sl], _inv_count(pos, w), w)
            pre = _dot_nn(pooled.astype(BF16), pw_ref[g].astype(BF16))
            mix_scr[:, sl] = pre * ps_ref[:, sl]
        u = _gelu(proj[:, POOL_WIDTH:POOL_WIDTH + SGU_WIDTH])
        v = _gelu(proj[:, POOL_WIDTH + SGU_WIDTH:])
        vhat, _ = _ln_stats(v)
        v_ln = vhat * lg_ref[...] + lb_ref[...]
        for h in range(N_HEADS):
            ws = _masked_sgu_w(sw_ref, h).astype(BF16)
            mixed = _dot_nn(ws, _to_head_major(v_ln, h, nc).astype(BF16)) + sb_ref[:, h:h + 1]
            for c in range(nc):
                rs = slice(c * GROUP, (c + 1) * GROUP)
                mix_scr[rs, POOL_WIDTH + h * GROUP:POOL_WIDTH + (h + 1) * GROUP] = (
                    u[rs, h * GROUP:(h + 1) * GROUP] * mixed[:, c * GROUP:(c + 1) * GROUP])
        mixb = mix_scr[...].astype(BF16)
        mix_ref[...] = mixb
        r1 = ALPHA * x + _dot_nn(mixb, wout_ref[...])
        xhat, rstd = _ln_stats(r1)
        xhat_ref[...] = xhat
        rstd_ref[...] = rstd

        for a in range(n_later):
            relay_tile = min(n_tiles // 2 + a, n_tiles - 1)

            @pl.when(i == relay_tile)
            def _(a=a):
                gather.relay(a)

            @pl.when(i == max(n_tiles - n_later + a, relay_tile))
            def _(a=a):
                gather.pass_on(a)

        @pl.when(i == n_tiles - 1)
        def _():
            gather.finish()

    def tile(cols):
        return pl.BlockSpec((tt, cols), lambda i: (i, 0))

    def whole(a):
        nd = a.ndim
        return pl.BlockSpec(a.shape, lambda i: (0,) * nd)

    any_spec = pl.BlockSpec(memory_space=pl.ANY)
    halo = pl.BlockSpec((HALO, D_MODEL), lambda i: (jnp.maximum(i * (tt // HALO) - 1, 0), 0))
    consts = [win_t, wout, pool_w, pool_scale, sln_g, sln_b, sgu_w, sgu_b_t, ln1_g, ln1_b]
    return pl.pallas_call(
        body, name="mixer_fwd", grid=(n_tiles,),
        in_specs=[tile(D_MODEL), halo] + [whole(a) for a in consts] + [whole(s) for s in later_shards],
        out_specs=[tile(IN_COLS), tile(D_MODEL), tile(1), tile(D_MODEL), tile(D_MODEL)] + [any_spec] * n_later,
        out_shape=[jax.ShapeDtypeStruct((tokens, IN_COLS), F32), jax.ShapeDtypeStruct((tokens, D_MODEL), F32),
                   jax.ShapeDtypeStruct((tokens, 1), F32), jax.ShapeDtypeStruct((tokens, D_MODEL), BF16),
                   jax.ShapeDtypeStruct((tokens, D_MODEL), BF16)] + [_gathered_bf16(s) for s in later_shards],
        scratch_shapes=[pltpu.VMEM((tt, D_MODEL), F32)] + [pltpu.VMEM(s.shape, BF16) for s in later_shards]
                       + _TwoLevelGather.scratch(n_later),
        compiler_params=pltpu.CompilerParams(dimension_semantics=("arbitrary",), vmem_limit_bytes=VMEM_LIMIT_BYTES),
    )(x2d, x2d, *consts, *later_shards)


def _ffn_fwd_bwd(xhat1, rstd1, target, wgu_t, wdown, ln1_g, ln1_b, ln2_g, ln2_b):
    tokens = xhat1.shape[0]
    tt = min(256, tokens)

    def body(xhat_ref, rstd_ref, tgt_ref, wgu_hbm, wd_hbm, g1_ref, b1_ref, g2_ref, b2_ref,
             dr1_ref, acts_ref, grads_ref, stats_ref, wgu_ref, wd_ref, gu_scr, sems):
        i = pl.program_id(0)

        @pl.when(i == 0)
        def _():
            loads = [pltpu.make_async_copy(wgu_hbm, wgu_ref, sems.at[0]),
                     pltpu.make_async_copy(wd_hbm, wd_ref, sems.at[1])]
            for cp in loads:
                cp.start()
            stats_ref[...] = jnp.zeros_like(stats_ref)
            for cp in loads:
                cp.wait()

        xhat1_t = xhat_ref[...]
        h = xhat1_t * g1_ref[...] + b1_ref[...]
        hb = h.astype(BF16)
        acts_ref[:, 0:D_MODEL] = hb
        gate = _dot_nt(hb, wgu_ref[0:D_FF, :])
        up = _dot_nt(hb, wgu_ref[D_FF:, :])
        gu_scr[:, 0:D_FF] = gate
        gu_scr[:, D_FF:] = up
        ab = (gate * jax.nn.sigmoid(gate) * up).astype(BF16)
        grads_ref[:, 2 * D_FF:] = ab
        xhat2, rstd2 = _ln_stats(ALPHA * h + _dot_nn(ab, wd_ref[...]))
        err = xhat2 * g2_ref[...] + b2_ref[...] - tgt_ref[...]
        dy = err * (1.0 / D_MODEL)
        stats_ref[0:1, :] += _rowsum(dy * xhat2)
        stats_ref[1:2, :] += _rowsum(dy)
        stats_ref[4:5, :] += _rowsum(err * err)
        dr2 = _ln_bwd(dy, xhat2, rstd2, g2_ref[...])
        dr2b = dr2.astype(BF16)
        acts_ref[:, D_MODEL:2 * D_MODEL] = dr2b
        da = _dot_nt(dr2b, wd_ref[...])
        gate = gu_scr[:, 0:D_FF]
        up = gu_scr[:, D_FF:]
        sg = jax.nn.sigmoid(gate)
        dgate = (da * up * (sg * (1.0 + gate * (1.0 - sg)))).astype(BF16)
        dup = (da * (gate * sg)).astype(BF16)
        grads_ref[:, 0:D_FF] = dgate
        grads_ref[:, D_FF:2 * D_FF] = dup
        dh = ALPHA * dr2 + _dot_nn(dgate, wgu_ref[0:D_FF, :]) + _dot_nn(dup, wgu_ref[D_FF:, :])
        stats_ref[2:3, :] += _rowsum(dh * xhat1_t)
        stats_ref[3:4, :] += _rowsum(dh)
        dr1 = _ln_bwd(dh, xhat1_t, rstd_ref[...], g1_ref[...])
        dr1_ref[...] = dr1
        acts_ref[:, 2 * D_MODEL:] = dr1.astype(BF16)

    def tile(cols):
        return pl.BlockSpec((tt, cols), lambda i: (i, 0))

    def whole(a):
        nd = a.ndim
        return pl.BlockSpec(a.shape, lambda i: (0,) * nd)

    any_spec = pl.BlockSpec(memory_space=pl.ANY)
    vecs = [ln1_g, ln1_b, ln2_g, ln2_b]
    return pl.pallas_call(
        body, name="ffn_fwd_bwd", grid=(tokens // tt,),
        in_specs=[tile(D_MODEL), tile(1), tile(D_MODEL), any_spec, any_spec] + [whole(a) for a in vecs],
        out_specs=[tile(D_MODEL), tile(3 * D_MODEL), tile(3 * D_FF), pl.BlockSpec((8, D_MODEL), lambda i: (0, 0))],
        out_shape=[jax.ShapeDtypeStruct((tokens, D_MODEL), F32), jax.ShapeDtypeStruct((tokens, 3 * D_MODEL), BF16),
                   jax.ShapeDtypeStruct((tokens, 3 * D_FF), BF16), jax.ShapeDtypeStruct((8, D_MODEL), F32)],
        scratch_shapes=[pltpu.VMEM(wgu_t.shape, BF16), pltpu.VMEM(wdown.shape, BF16),
                        pltpu.VMEM((tt, 2 * D_FF), F32), pltpu.SemaphoreType.DMA((2,))],
        compiler_params=pltpu.CompilerParams(dimension_semantics=("arbitrary",), vmem_limit_bytes=VMEM_LIMIT_BYTES),
    )(xhat1, rstd1, target, wgu_t, wdown, *vecs)


def _wgrad_exchange(lhs, rhs, chips_per_block, name, gather_rows=(), after=None, collective_id=None,
                    lhs_cols=None, rhs_block=None, token_tile=2048):
    assert collective_id is None or not gather_rows
    tokens = lhs.shape[0]
    first_col, n_all = (0, lhs.shape[1]) if lhs_cols is None else lhs_cols
    m = rhs.shape[1] if rhs_block is None else D_MODEL
    n = n_all // N_DEV
    tw = min(token_tile, tokens)
    nt = tokens // tw
    cpb = chips_per_block
    nj = N_CHIPS // cpb
    lhs_block0 = first_col // (2 * n * cpb)
    assert lhs_block0 * 2 * n * cpb == first_col
    rhs_col = 0 if rhs_block is None else rhs_block
    ng = len(gather_rows)
    anchors = [] if after is None else [after]

    def body(l_ref, r_ref, *rest):
        small_refs, rest = rest[:ng], rest[ng + len(anchors):]
        bf_ref, own_ref = rest[:2]
        gathered_refs, rest = rest[2:2 + ng], rest[2 + ng:]
        acc, kept, landed, send_sems, recv_sems = rest[:5]
        sendbuf = bf_ref
        j, t = pl.program_id(0), pl.program_id(1)
        first, last = (j == 0) & (t == 0), (j == nj - 1) & (t == nt - 1)
        x, y, c = _mesh_position()
        if collective_id is not None:
            @pl.when(first)
            def _():
                barrier = pltpu.get_barrier_semaphore()
                pl.semaphore_signal(barrier, inc=1, device_id=(x, y, 1 - c), device_id_type=MESH)
                pl.semaphore_wait(barrier, 1)
        if ng:
            gather = _TwoLevelGather(small_refs, gathered_refs, *rest[5:8])

            @pl.when(first)
            def _():
                gather.start()

        def copy(q):
            return pltpu.make_async_remote_copy(
                src_ref=sendbuf.at[q], dst_ref=landed.at[q], send_sem=send_sems.at[q], recv_sem=recv_sems.at[q],
                device_id=(x, y, 1 - c), device_id_type=MESH)

        @pl.when(t == 0)
        def _():
            acc[...] = _dot_tn(l_ref[...], r_ref[...])

        @pl.when(t > 0)
        def _():
            acc[...] += _dot_tn(l_ref[...], r_ref[...])

        @pl.when(t == nt - 1)
        def _():
            for qq in range(cpb):
                q = j * cpb + qq
                kept[q] = acc[pl.ds(pl.multiple_of(qq * 2 * n + c * n, 8), n), :]
                sendbuf[q] = acc[pl.ds(pl.multiple_of(qq * 2 * n + (1 - c) * n, 8), n), :].astype(BF16)
                copy(q).start()

        if ng:
            @pl.when((j == nj - 1) & (t == nt // 2))
            def _():
                for a in range(ng):
                    gather.relay(a)

            @pl.when(last)
            def _():
                for a in range(ng):
                    gather.pass_on(a)
                gather.finish()

        @pl.when(last)
        def _():
            for q in range(N_CHIPS):
                copy(q).wait_send()
                copy(q).wait_recv()
            for q in range(N_CHIPS):
                s = kept[q] + landed[q].astype(F32)
                bf_ref[q] = s.astype(BF16)

                @pl.when(q == 2 * x + y)
                def _(s=s):
                    own_ref[...] = s

    any_spec = pl.BlockSpec(memory_space=pl.ANY)
    by_chip = (N_CHIPS, n, m)
    return pl.pallas_call(
        body, name=name, grid=(nj, nt),
        in_specs=[pl.BlockSpec((tw, 2 * n * cpb), lambda j, t: (t, lhs_block0 + j)),
                  pl.BlockSpec((tw, m), lambda j, t: (t, rhs_col))] + [any_spec] * (ng + len(anchors)),
        out_specs=[pl.BlockSpec(by_chip, lambda j, t: (0, 0, 0)), pl.BlockSpec((n, m), lambda j, t: (0, 0))]
                  + [any_spec] * ng,
        out_shape=[jax.ShapeDtypeStruct(by_chip, BF16), jax.ShapeDtypeStruct((n, m), F32)]
                  + [_gathered_shape(s) for s in gather_rows],
        scratch_shapes=[pltpu.VMEM((2 * n * cpb, m), F32), pltpu.VMEM(by_chip, F32), pltpu.VMEM(by_chip, BF16),
                        pltpu.SemaphoreType.DMA((N_CHIPS,)), pltpu.SemaphoreType.DMA((N_CHIPS,))]
                       + (_TwoLevelGather.scratch(ng) if ng else []),
        compiler_params=pltpu.CompilerParams(dimension_semantics=("arbitrary", "arbitrary"),
                                             vmem_limit_bytes=VMEM_LIMIT_BYTES + 4 * 1024 * 1024,
                                             collective_id=collective_id),
    )(lhs, rhs, *gather_rows, *anchors)


def _received_shape(p):
    return jax.ShapeDtypeStruct((3,) + p.shape[1:], p.dtype)


ROW_POOL_SCALE, ROW_SLN_G, ROW_SLN_B, ROW_SGU_B = 0, 1, 2, 3
ROW_LN2_G, ROW_LN2_B, ROW_LN1_G, ROW_LN1_B, ROW_LOSS = 8, 9, 10, 11, 12
VEC_ROWS = 16


def _mixer_bwd(dr1, proj, win_t, wout, pool_w, pool_scale, sln_g, sln_b, sgu_w, sgu_b_t, stats, seq):
    tokens = dr1.shape[0]
    tt = min(512, seq)
    tiles_per_seq = seq // tt
    nc = tt // GROUP
    n_halo_blocks = tokens // HALO
    n_tiles = tokens // tt

    def body(dr1_ref, dr1n_ref, proj_ref, projh_ref, win_ref, wout_ref, pw_ref, ps_ref, lg_ref, lb_ref, sw_ref, sb_ref,
             stats_ref, gx_ref, dproj_ref, dmat_ref, dvec_ref, du_scr, dv_scr):
        i = pl.program_id(0)
        tile_in_seq = i % tiles_per_seq

        @pl.when(i == 0)
        def _():
            dmat_ref[...] = jnp.zeros_like(dmat_ref)
            dvec_ref[0:8, :] = jnp.zeros((8, D_MODEL), F32)
            dvec_ref[8:16, :] = stats_ref[...]

        dr1_t = dr1_ref[...]
        dr1b = dr1_t.astype(BF16)
        dmix = _dot_nt(dr1b, wout_ref[...])
        dpo_next = _dot_nt(dr1n_ref[...].astype(BF16), wout_ref[0:POOL_WIDTH, :])
        dpo_next = jnp.where(tile_in_seq == tiles_per_seq - 1, 0.0, dpo_next)
        proj = proj_ref[...]
        xp_prev = jnp.where(tile_in_seq == 0, 0.0, projh_ref[...])
        pos = tile_in_seq * tt + lax.broadcasted_iota(jnp.int32, (tt, 1), 0)
        pos_next = (tile_in_seq + 1) * tt + lax.broadcasted_iota(jnp.int32, (HALO, 1), 0)

        for g, w in enumerate(POOL_WINDOWS):
            sl = slice(g * GROUP, (g + 1) * GROUP)
            inv_cnt = _inv_count(pos, w)
            pwb = pw_ref[g].astype(BF16)
            pooledb = _pool_fwd(proj[:, sl], xp_prev[:, sl], inv_cnt, w).astype(BF16)
            pre = _dot_nn(pooledb, pwb)
            dpo = dmix[:, sl]
            dvec_ref[ROW_POOL_SCALE:ROW_POOL_SCALE + 1, sl] += _rowsum(dpo * pre)
            dsb = (dpo * ps_ref[:, sl]).astype(BF16)
            dmat_ref[g] += _dot_tn(pooledb, dsb)
            dpooled = _dot_nt(dsb, pwb)
            dpooled_next = _dot_nt((dpo_next[:, sl] * ps_ref[:, sl]).astype(BF16), pwb)
            dxp = _pool_bwd(dpooled, dpooled_next, inv_cnt, _inv_count(pos_next, w), w)
            dproj_ref[:, sl] = dxp.astype(BF16)

        zu = proj[:, POOL_WIDTH:POOL_WIDTH + SGU_WIDTH]
        zv = proj[:, POOL_WIDTH + SGU_WIDTH:]
        u = _gelu(zu)
        vhat, rstd_v = _ln_stats(_gelu(zv))
        v_ln = vhat * lg_ref[...] + lb_ref[...]
        dsg = dmix[:, POOL_WIDTH:]
        row = lax.broadcasted_iota(jnp.int32, (GROUP, GROUP), 0)
        col = lax.broadcasted_iota(jnp.int32, (GROUP, GROUP), 1)
        for h in range(N_HEADS):
            ws = _masked_sgu_w(sw_ref, h).astype(BF16)
            vh = _to_head_major(v_ln, h, nc).astype(BF16)
            mixed = _dot_nn(ws, vh) + sb_ref[:, h:h + 1]
            dsg_h = _to_head_major(dsg, h, nc)
            du_h = dsg_h * mixed
            dm_h = dsg_h * _to_head_major(u, h, nc)
            pos_sums = lax.dot_general(jnp.ones((8, nc * GROUP), F32), dm_h, (((1,), (1,)), ((), ())),
                                       precision=lax.Precision.HIGH, preferred_element_type=F32)
            dvec_ref[ROW_SGU_B + h:ROW_SGU_B + h + 1, 0:GROUP] += pos_sums[0:1, :]
            dmb = dm_h.astype(BF16)
            dmat_ref[len(POOL_WINDOWS) + h] += jnp.where(row >= col, _dot_nt(dmb, vh), 0.0)
            dv_h = _dot_tn(ws, dmb)
            for c in range(nc):
                rs = slice(c * GROUP, (c + 1) * GROUP)
                cs = slice(h * GROUP, (h + 1) * GROUP)
                du_scr[rs, cs] = du_h[:, c * GROUP:(c + 1) * GROUP]
                dv_scr[rs, cs] = dv_h[:, c * GROUP:(c + 1) * GROUP]
        dv_ln = dv_scr[...]
        dvec_ref[ROW_SLN_B:ROW_SLN_B + 1, 0:SGU_WIDTH] += _rowsum(dv_ln)
        dvec_ref[ROW_SLN_G:ROW_SLN_G + 1, 0:SGU_WIDTH] += _rowsum(dv_ln * vhat)
        dv = _ln_bwd(dv_ln, vhat, rstd_v, lg_ref[...])
        dproj_ref[:, POOL_WIDTH:POOL_WIDTH + SGU_WIDTH] = (du_scr[...] * _gelu_grad(zu)).astype(BF16)
        dproj_ref[:, POOL_WIDTH + SGU_WIDTH:] = (dv * _gelu_grad(zv)).astype(BF16)
        gx_ref[...] = ALPHA * dr1_t + _dot_nn(dproj_ref[...], win_ref[...])

    def tile(cols):
        return pl.BlockSpec((tt, cols), lambda i: (i, 0))

    def whole(a):
        nd = a.ndim
        return pl.BlockSpec(a.shape, lambda i: (0,) * nd)

    def resident(shape):
        nd = len(shape)
        return pl.BlockSpec(shape, lambda i: (0,) * nd)

    next_halo = pl.BlockSpec((HALO, D_MODEL), lambda i: (jnp.minimum((i + 1) * (tt // HALO), n_halo_blocks - 1), 0))
    prev_halo = pl.BlockSpec((HALO, POOL_WIDTH), lambda i: (jnp.maximum(i * (tt // HALO) - 1, 0), 0))
    consts = [win_t, wout, pool_w, pool_scale, sln_g, sln_b, sgu_w, sgu_b_t, stats]
    small_shapes = [(len(POOL_WINDOWS) + N_HEADS, GROUP, GROUP), (VEC_ROWS, D_MODEL)]
    return pl.pallas_call(
        body, name="mixer_bwd", grid=(n_tiles,),
        in_specs=[tile(D_MODEL), next_halo, tile(IN_COLS), prev_halo] + [whole(a) for a in consts],
        out_specs=[tile(D_MODEL), tile(IN_COLS)] + [resident(s) for s in small_shapes],
        out_shape=[jax.ShapeDtypeStruct((tokens, D_MODEL), F32), jax.ShapeDtypeStruct((tokens, IN_COLS), BF16)]
                  + [jax.ShapeDtypeStruct(s, F32) for s in small_shapes],
        scratch_shapes=[pltpu.VMEM((tt, SGU_WIDTH), F32), pltpu.VMEM((tt, SGU_WIDTH), F32)],
        compiler_params=pltpu.CompilerParams(dimension_semantics=("arbitrary",), vmem_limit_bytes=VMEM_LIMIT_BYTES),
    )(dr1, dr1, proj, proj, *consts)


def _owner_copies(src_ref, land_ref, sems):
    x, y, c = _mesh_position()
    return [pltpu.make_async_remote_copy(
        src_ref=src_ref.at[2 * cx + cy], dst_ref=land_ref.at[j], send_sem=sems[j], recv_sem=sems[3 + j],
        device_id=(cx, cy, c), device_id_type=MESH) for j, (cx, cy) in enumerate(_other_chips(x, y))]


def _send_to_owners_start(chip_partial, name, collective_id):
    land = _received_shape(chip_partial)

    def body(src_ref, land_ref, *rest):
        x, y, c = _mesh_position()
        barrier = pltpu.get_barrier_semaphore()
        for cx, cy in _other_chips(x, y):
            pl.semaphore_signal(barrier, inc=1, device_id=(cx, cy, c), device_id_type=MESH)
        pl.semaphore_wait(barrier, 3)
        for cp in _owner_copies(src_ref, land_ref, rest[:6]):
            cp.start()
        rest[8][...] = jnp.zeros_like(rest[8])

    hbm = pl.BlockSpec(memory_space=pltpu.HBM)
    sem = pl.BlockSpec(memory_space=pltpu.SEMAPHORE)
    outs = pl.pallas_call(
        body, name=name,
        out_shape=[pltpu.SemaphoreType.DMA(())] * 6 + [pltpu.HBM(chip_partial.shape, chip_partial.dtype),
                                                       pltpu.HBM(land.shape, land.dtype),
                                                       jax.ShapeDtypeStruct((8, GROUP), F32)],
        in_specs=[hbm, hbm], out_specs=[sem] * 6 + [hbm, hbm, pl.BlockSpec(memory_space=pltpu.VMEM)],
        input_output_aliases={0: 6, 1: 7},
        compiler_params=pltpu.CompilerParams(has_side_effects=pltpu.SideEffectType.DATAFLOW_SIDE_EFFECTING,
                                             collective_id=collective_id),
    )(pltpu.with_memory_space_constraint(chip_partial, pltpu.HBM),
      pltpu.with_memory_space_constraint(lax.empty(land.shape, land.dtype), pltpu.HBM))
    return outs[:6], outs[6], outs[7], outs[8]


def _send_to_owners_wait(sems, src_thru, land_thru, after, name):
    def body(src_ref, land_ref, *rest):
        for cp in _owner_copies(src_ref, land_ref, rest[:6]):
            cp.wait_send()
            cp.wait_recv()

    hbm = pl.BlockSpec(memory_space=pltpu.HBM)
    sem = pl.BlockSpec(memory_space=pltpu.SEMAPHORE)
    return pl.pallas_call(
        body, name=name,
        out_shape=[pltpu.HBM(src_thru.shape, src_thru.dtype), pltpu.HBM(land_thru.shape, land_thru.dtype)],
        in_specs=[hbm, hbm] + [sem] * 6 + [pl.BlockSpec(memory_space=pl.ANY)], out_specs=[hbm, hbm],
        input_output_aliases={0: 0, 1: 1},
        compiler_params=pltpu.CompilerParams(has_side_effects=pltpu.SideEffectType.DATAFLOW_SIDE_EFFECTING),
    )(src_thru, land_thru, *sems, after)[1]


def _adamw_math(w, g, m, v):
    m = ADAM_B1 * m + (1.0 - ADAM_B1) * g
    v = ADAM_B2 * v + (1.0 - ADAM_B2) * (g * g)
    m_hat = m / (1.0 - ADAM_B1 ** ADAM_STEP)
    v_hat = v / (1.0 - ADAM_B2 ** ADAM_STEP)
    delta = -ADAM_LR * (m_hat / (jnp.sqrt(v_hat) + ADAM_EPS) + ADAM_WD * w)
    return delta, m, v


def _sum_adamw(w, m, v, own, received, name):
    rows, cols = w.shape
    rb = _row_block(rows, 256)

    def body(w_ref, m_ref, v_ref, own_ref, rec_ref, g_ref, d_ref, mo_ref, vo_ref):
        g = own_ref[...]
        for j in range(3):
            g = g + rec_ref[j].astype(F32)
        g_ref[...] = g
        d_ref[...], mo_ref[...], vo_ref[...] = _adamw_math(w_ref[...], g, m_ref[...], v_ref[...])

    spec = pl.BlockSpec((rb, cols), lambda r: (r, 0))
    return pl.pallas_call(
        body, name=name, grid=(rows // rb,),
        in_specs=[spec] * 4 + [pl.BlockSpec((3, rb, cols), lambda r: (0, r, 0))],
        out_specs=[spec] * 4, out_shape=[jax.ShapeDtypeStruct((rows, cols), F32)] * 4,
        compiler_params=pltpu.CompilerParams(dimension_semantics=("arbitrary",)),
    )(w, m, v, own, received)


SMALL_NAMES = ("pool_w", "sgu_w", "pool_scale", "sgu_ln_g", "sgu_ln_b", "sgu_b", "ln1_g", "ln1_b", "ln2_g", "ln2_b")
_SMALL_VEC_ROWS = {"pool_scale": (ROW_POOL_SCALE, POOL_WIDTH), "sgu_ln_g": (ROW_SLN_G, SGU_WIDTH),
                   "sgu_ln_b": (ROW_SLN_B, SGU_WIDTH), "ln1_g": (ROW_LN1_G, D_MODEL), "ln1_b": (ROW_LN1_B, D_MODEL),
                   "ln2_g": (ROW_LN2_G, D_MODEL), "ln2_b": (ROW_LN2_B, D_MODEL)}
_SMALL_MAT_FIRST = {"pool_w": 0, "sgu_w": len(POOL_WINDOWS)}


def _small_sum_adamw(mats_all, vecs_all, w, m, v):
    n = len(SMALL_NAMES)

    def body(mats_ref, vecs_ref, *refs):
        w_refs, m_refs, v_refs = refs[:n], refs[n:2 * n], refs[2 * n:3 * n]
        loss_ref = refs[3 * n]
        g_refs, d_refs, mo_refs, vo_refs = (refs[3 * n + 1 + k * n:3 * n + 1 + (k + 1) * n] for k in range(4))
        vec_scr = refs[7 * n + 1]

        def update(k, idx, g):
            d, mo, vo = _adamw_math(w_refs[k][idx], g, m_refs[k][idx], v_refs[k][idx])
            g_refs[k][idx], d_refs[k][idx], mo_refs[k][idx], vo_refs[k][idx] = g, d, mo, vo

        total = vecs_ref[0]
        for dev in range(1, N_DEV):
            total = total + vecs_ref[dev]
        vec_scr[...] = total
        for k, name in enumerate(SMALL_NAMES):
            if name in _SMALL_MAT_FIRST:
                for b in range(4):
                    g = mats_ref[0, _SMALL_MAT_FIRST[name] + b]
                    for dev in range(1, N_DEV):
                        g = g + mats_ref[dev, _SMALL_MAT_FIRST[name] + b]
                    update(k, (0, b), g)
            elif name == "sgu_b":
                update(k, (0,), vec_scr[ROW_SGU_B:ROW_SGU_B + N_HEADS, 0:GROUP])
            else:
                row, width = _SMALL_VEC_ROWS[name]
                update(k, (slice(None), slice(None)), vec_scr[row:row + 1, 0:width])
        loss = jnp.sum(vec_scr[ROW_LOSS:ROW_LOSS + 1, :], axis=1, keepdims=True) * (0.5 / D_MODEL)
        loss_ref[...] = jnp.broadcast_to(loss, loss_ref.shape)

    vmem = pl.BlockSpec(memory_space=pltpu.VMEM)
    shapes = [jax.ShapeDtypeStruct(w[k].shape, F32) for k in SMALL_NAMES]
    outs = pl.pallas_call(
        body, name="small_sum_adamw",
        out_shape=[jax.ShapeDtypeStruct((8, GROUP), F32)] + shapes * 4,
        in_specs=[vmem] * (2 + 3 * n), out_specs=[vmem] * (1 + 4 * n),
        scratch_shapes=[pltpu.VMEM((VEC_ROWS, D_MODEL), F32)],
    )(mats_all, vecs_all, *[w[k] for k in SMALL_NAMES], *[m[k] for k in SMALL_NAMES], *[v[k] for k in SMALL_NAMES])
    return outs[0], outs[1:1 + n], outs[1 + n:1 + 2 * n], outs[1 + 2 * n:1 + 3 * n], outs[1 + 3 * n:]


def kernel(x, w_in, pool_w, pool_scale, sgu_ln_g, sgu_ln_b, sgu_w, sgu_b, w_out, ln1_g, ln1_b, w_gate_up, w_down, ln2_g, ln2_b, loss_target, m_w_in, m_pool_w, m_pool_scale, m_sgu_ln_g, m_sgu_ln_b, m_sgu_w, m_sgu_b, m_w_out, m_ln1_g, m_ln1_b, m_w_gate_up, m_w_down, m_ln2_g, m_ln2_b, v_w_in, v_pool_w, v_pool_scale, v_sgu_ln_g, v_sgu_ln_b, v_sgu_w, v_sgu_b, v_w_out, v_ln1_g, v_ln1_b, v_w_gate_up, v_w_down, v_ln2_g, v_ln2_b):
    bl, seq, _ = x.shape
    tokens = bl * seq
    x2d = x.reshape(tokens, D_MODEL)
    tgt2d = loss_target.reshape(tokens, D_MODEL)

    win_t, wout = _all_gather_rows([w_in[0].T, w_out[0]], "weight_all_gather")

    pool_w3, sgu_w3 = pool_w[0], sgu_w[0]
    sgu_b_t = sgu_b[0].T
    proj, xhat1, rstd1, mix_bf, x_bf, wgu_t, wdown = _mixer_fwd(
        x2d, win_t, wout, pool_w3, pool_scale, sgu_ln_g, sgu_ln_b, sgu_w3, sgu_b_t, ln1_g, ln1_b,
        [w_gate_up[0].T, w_down[0]], seq)
    dr1, acts_bf, grads_bf, stats = _ffn_fwd_bwd(xhat1, rstd1, tgt2d, wgu_t, wdown, ln1_g, ln1_b, ln2_g, ln2_b)

    bf_gu, own_gu = _wgrad_exchange(grads_bf, acts_bf, 1, "wgrad_gate_up", collective_id=1,
                                    lhs_cols=(0, 2 * D_FF), rhs_block=0)
    sent_gu = _send_to_owners_start(bf_gu, "grad_scatter_w_gate_up_start", collective_id=4)
    bf_dn, own_dn = _wgrad_exchange(grads_bf, acts_bf, 2, "wgrad_down", after=sent_gu[3], collective_id=2,
                                    lhs_cols=(2 * D_FF, D_FF), rhs_block=1)
    sent_dn = _send_to_owners_start(bf_dn, "grad_scatter_w_down_start", collective_id=5)
    bf_out, own_out = _wgrad_exchange(mix_bf, acts_bf, N_CHIPS, "wgrad_out", after=sent_dn[3], collective_id=3,
                                      rhs_block=2)
    sent_out = _send_to_owners_start(bf_out, "grad_scatter_w_out_start", collective_id=6)
    gx, dproj_bf, d_mats, d_vecs = _mixer_bwd(
        dr1, proj, win_t, wout, pool_w3, pool_scale, sgu_ln_g, sgu_ln_b, sgu_w3, sgu_b_t, stats, seq)
    bf_in, own_in, mats_all, vecs_all = _wgrad_exchange(
        dproj_bf, x_bf, N_CHIPS, "wgrad_in", gather_rows=[d_mats.reshape(-1, GROUP), d_vecs], after=sent_out[3])
    sent_in = _send_to_owners_start(bf_in, "grad_scatter_w_in_start", collective_id=7)

    grads, deltas, new_m, new_v = {}, {}, {}, {}
    after = sent_in[3]
    for nm, w, m, v, own, sent, transposed in (("w_gate_up", w_gate_up, m_w_gate_up, v_w_gate_up, own_gu, sent_gu, True),
                                               ("w_down", w_down, m_w_down, v_w_down, own_dn, sent_dn, False),
                                               ("w_out", w_out, m_w_out, v_w_out, own_out, sent_out, False),
                                               ("w_in", w_in, m_w_in, v_w_in, own_in, sent_in, True)):
        rows = (lambda a: a[0].T) if transposed else (lambda a: a[0])
        back = (lambda a: a.T[None]) if transposed else (lambda a: a[None])
        rec = _send_to_owners_wait(*sent[:3], after, "grad_scatter_" + nm + "_wait")
        g, d, mo, vo = _sum_adamw(rows(w), rows(m), rows(v), own, rec, "adamw_" + nm)
        after = vo
        grads[nm], deltas[nm], new_m[nm], new_v[nm] = back(g), back(d), back(mo), back(vo)

    small_w = {"pool_w": pool_w, "pool_scale": pool_scale, "sgu_ln_g": sgu_ln_g, "sgu_ln_b": sgu_ln_b, "sgu_w": sgu_w,
               "sgu_b": sgu_b, "ln1_g": ln1_g, "ln1_b": ln1_b, "ln2_g": ln2_g, "ln2_b": ln2_b}
    small_m = {"pool_w": m_pool_w, "pool_scale": m_pool_scale, "sgu_ln_g": m_sgu_ln_g, "sgu_ln_b": m_sgu_ln_b,
               "sgu_w": m_sgu_w, "sgu_b": m_sgu_b, "ln1_g": m_ln1_g, "ln1_b": m_ln1_b, "ln2_g": m_ln2_g, "ln2_b": m_ln2_b}
    small_v = {"pool_w": v_pool_w, "pool_scale": v_pool_scale, "sgu_ln_g": v_sgu_ln_g, "sgu_ln_b": v_sgu_ln_b,
               "sgu_w": v_sgu_w, "sgu_b": v_sgu_b, "ln1_g": v_ln1_g, "ln1_b": v_ln1_b, "ln2_g": v_ln2_g, "ln2_b": v_ln2_b}
    loss_blk, g_small, d_small, m_small, v_small = _small_sum_adamw(
        mats_all.reshape(N_DEV, 2 * N_HEADS, GROUP, GROUP), vecs_all.reshape(N_DEV, VEC_ROWS, D_MODEL),
        small_w, small_m, small_v)
    for vals, dst in ((g_small, grads), (d_small, deltas), (m_small, new_m), (v_small, new_v)):
        dst.update(zip(SMALL_NAMES, vals))

    order = ["w_in", "pool_w", "pool_scale", "sgu_ln_g", "sgu_ln_b", "sgu_w", "sgu_b", "w_out", "ln1_g", "ln1_b",
             "w_gate_up", "w_down", "ln2_g", "ln2_b"]
    return (loss_blk[0, 0], gx.reshape(bl, seq, D_MODEL), *[grads[k] for k in order], *[deltas[k] for k in order],
            *[new_m[k] for k in order], *[new_v[k] for k in order])
```

```python
import jax
import jax.numpy as jnp
from jax import lax
from jax.experimental import pallas as pl
from jax.experimental.pallas import tpu as pltpu

F32 = jnp.float32
BF16 = jnp.bfloat16
MESH = pl.DeviceIdType.MESH

D_MODEL = 1024
POOL_WIDTH = 512
SGU_WIDTH = 512
POOL_WINDOWS = (2, 4, 8, 16)
GROUP = 128
N_HEADS = 4
IN_COLS = POOL_WIDTH + 2 * SGU_WIDTH
D_FF = 2816
LN_EPS = 1e-5
ALPHA = float(2.0 ** 0.25)
HALO = 16
N_DEV = 8
N_CHIPS = 4

ADAM_LR = 0.001
ADAM_B1 = 0.9
ADAM_B2 = 0.999
ADAM_EPS = 1e-08
ADAM_WD = 0.01
ADAM_STEP = 10

VMEM_LIMIT_BYTES = 56 * 1024 * 1024

_SQRT_HALF = 0.7071067811865476
_INV_SQRT_2PI = 0.3989422804014327


def _dot_nn(a, b):
    return lax.dot_general(a, b, (((1,), (0,)), ((), ())), preferred_element_type=F32)


def _dot_nt(a, b):
    return lax.dot_general(a, b, (((1,), (1,)), ((), ())), preferred_element_type=F32)


def _dot_tn(a, b):
    return lax.dot_general(a, b, (((0,), (0,)), ((), ())), preferred_element_type=F32)


def _gelu(x):
    return 0.5 * x * (1.0 + lax.erf(x * _SQRT_HALF))


def _gelu_grad(x):
    return 0.5 * (1.0 + lax.erf(x * _SQRT_HALF)) + x * jnp.exp(-0.5 * x * x) * _INV_SQRT_2PI


def _ln_stats(r):
    mu = jnp.mean(r, axis=-1, keepdims=True)
    d = r - mu
    var = jnp.mean(d * d, axis=-1, keepdims=True)
    rstd = lax.rsqrt(var + LN_EPS)
    return d * rstd, rstd


def _ln_bwd(dout, xhat, rstd, g):
    dxh = dout * g
    m1 = jnp.mean(dxh, axis=-1, keepdims=True)
    m2 = jnp.mean(dxh * xhat, axis=-1, keepdims=True)
    return rstd * (dxh - m1 - xhat * m2)


def _rowsum(a):
    return jnp.sum(a, axis=0, keepdims=True)


def _pool_fwd(xp, xp_prev, inv_cnt, w):
    s = jnp.concatenate([xp_prev, xp], axis=0)
    k = 1
    while k < w:
        s = s + pltpu.roll(s, k, 0)
        k *= 2
    return s[HALO:, :] * inv_cnt - xp


def _pool_bwd(dpooled, dpooled_next, inv_cnt, inv_cnt_next, w):
    n = dpooled.shape[0] + HALO
    s = jnp.concatenate([dpooled * inv_cnt, dpooled_next * inv_cnt_next], axis=0)
    k = 1
    while k < w:
        s = s + pltpu.roll(s, n - k, 0)
        k *= 2
    return s[: dpooled.shape[0], :] - dpooled


def _inv_count(pos, w):
    return 1.0 / jnp.minimum(pos + 1, w).astype(F32)


def _to_head_major(a, h, nc):
    return jnp.concatenate(
        [a[c * GROUP:(c + 1) * GROUP, h * GROUP:(h + 1) * GROUP] for c in range(nc)], axis=1)


def _masked_sgu_w(sw_ref, h):
    row = lax.broadcasted_iota(jnp.int32, (GROUP, GROUP), 0)
    col = lax.broadcasted_iota(jnp.int32, (GROUP, GROUP), 1)
    return jnp.where(row >= col, sw_ref[h], 0.0)


def _row_block(rows, limit):
    return max(b for b in range(16, min(rows, limit) + 1, 16) if rows % b == 0)


def _mesh_position():
    return lax.axis_index("x"), lax.axis_index("y"), lax.axis_index("c")


def _other_chips(x, y):
    return [(1 - x, y), (x, 1 - y), (1 - x, 1 - y)]


class _TwoLevelGather:
    def __init__(self, ins, outs, send_sems, recv_sems, local_sems):
        self.ins, self.outs = ins, outs
        self.send_sems, self.recv_sems, self.local_sems = send_sems, recv_sems, local_sems
        self.na = len(ins)
        x, y, c = _mesh_position()
        self.c = c
        self.me, self.sibling = (x, y, c), (x, y, 1 - c)
        self.chips = _other_chips(x, y)
        self.relay_from = (x + (1 - c) * (1 - 2 * x), y + c * (1 - 2 * y))
        self.relay_to = (x + c * (1 - 2 * x), y + (1 - c) * (1 - 2 * y))

    def _rows(self, a, px, py, pc):
        n = self.ins[a].shape[0]
        return self.outs[a].at[pl.ds((4 * px + 2 * py + pc) * n, n), :]

    def _copy(self, a, k, block, to, src=None):
        return pltpu.make_async_remote_copy(
            src_ref=self._rows(a, *block) if src is None else src, dst_ref=self._rows(a, *block),
            send_sem=self.send_sems.at[a * 7 + k], recv_sem=self.recv_sems.at[a * 7 + k],
            device_id=to, device_id_type=MESH)

    def _mine(self, a):
        return pltpu.make_async_copy(self.ins[a], self._rows(a, *self.me), self.local_sems.at[a])

    def start(self):
        for a in range(self.na):
            self._mine(a).start()
        for a in range(self.na):
            self._copy(a, 0, self.me, self.sibling, src=self.ins[a]).start()
            for j, chip in enumerate(self.chips[:2]):
                self._copy(a, 1 + j, self.me, (*chip, self.c), src=self.ins[a]).start()

    def relay(self, a):
        c, block = self.c, (*self.relay_from, self.c)
        self._copy(a, 1 + c, block, self.me).wait_recv()
        self._copy(a, 3, block, (*self.relay_to, c)).start()
        self._copy(a, 4 + c, block, self.sibling).start()

    def pass_on(self, a):
        c = self.c
        self._copy(a, 2 - c, (*self.relay_to, c), self.me).wait_recv()
        self._copy(a, 5 - c, (*self.relay_to, c), self.sibling).start()
        self._copy(a, 3, (*self.chips[2], c), self.me).wait_recv()
        self._copy(a, 6, (*self.chips[2], c), self.sibling).start()

    def finish(self):
        for a in range(self.na):
            self._copy(a, 0, self.sibling, self.me).wait_recv()
            for j, chip in enumerate(self.chips):
                self._copy(a, 4 + j, (*chip, 1 - self.c), self.me).wait_recv()
        for a in range(self.na):
            for k in range(7):
                self._copy(a, k, self.me, self.sibling, src=self.ins[a]).wait_send()
            self._mine(a).wait()

    @staticmethod
    def scratch(na):
        return [pltpu.SemaphoreType.DMA((7 * na,)), pltpu.SemaphoreType.DMA((7 * na,)), pltpu.SemaphoreType.DMA((na,))]


def _gathered_shape(s):
    return jax.ShapeDtypeStruct((N_DEV * s.shape[0], s.shape[1]), s.dtype)


def _gathered_bf16(s):
    return jax.ShapeDtypeStruct((N_DEV * s.shape[0], s.shape[1]), BF16)


def _all_gather_rows(shards, name):
    na = len(shards)

    def body(*refs):
        bf_refs = refs[2 * na:3 * na]
        for a in range(na):
            bf_refs[a][...] = refs[a][...].astype(BF16)
        gather = _TwoLevelGather(bf_refs, refs[na:2 * na], *refs[3 * na:])
        gather.start()
        for a in range(na):
            gather.relay(a)
        for a in range(na):
            gather.pass_on(a)
        gather.finish()

    return pl.pallas_call(
        body, name=name, out_shape=[_gathered_bf16(s) for s in shards],
        in_specs=[pl.BlockSpec(memory_space=pltpu.VMEM)] * na, out_specs=[pl.BlockSpec(memory_space=pl.ANY)] * na,
        scratch_shapes=[pltpu.VMEM(s.shape, BF16) for s in shards] + _TwoLevelGather.scratch(na),
    )(*shards)


def _mixer_fwd(x2d, win_t, wout, pool_w, pool_scale, sln_g, sln_b, sgu_w, sgu_b_t, ln1_g, ln1_b, later_shards, seq):
    tokens = x2d.shape[0]
    tt = min(512, seq)
    tiles_per_seq = seq // tt
    nc = tt // GROUP
    n_tiles = tokens // tt
    n_later = len(later_shards)

    def body(x_ref, xh_ref, win_ref, wout_ref, pw_ref, ps_ref, lg_ref, lb_ref, sw_ref, sb_ref, g1_ref, b1_ref, *rest):
        shard_refs, rest = rest[:n_later], rest[n_later:]
        proj_ref, xhat_ref, rstd_ref, mix_ref, xbf_ref = rest[:5]
        gathered_refs, rest = rest[5:5 + n_later], rest[5 + n_later:]
        mix_scr, shard_bf_refs, (send_sems, recv_sems, local_sems) = rest[0], rest[1:1 + n_later], rest[1 + n_later:]
        i = pl.program_id(0)
        gather = _TwoLevelGather(shard_bf_refs, gathered_refs, send_sems, recv_sems, local_sems)

        @pl.when(i == 0)
        def _():
            for a in range(n_later):
                shard_bf_refs[a][...] = shard_refs[a][...].astype(BF16)
            gather.start()

        tile_in_seq = i % tiles_per_seq
        x = x_ref[...]
        xb = x.astype(BF16)
        xbf_ref[...] = xb
        proj = _dot_nt(xb, win_ref[...])
        proj_ref[...] = proj
        xp_prev = _dot_nt(xh_ref[...].astype(BF16), win_ref[0:POOL_WIDTH, :])
        xp_prev = jnp.where(tile_in_seq == 0, 0.0, xp_prev)
        pos = tile_in_seq * tt + lax.broadcasted_iota(jnp.int32, (tt, 1), 0)
        for g, w in enumerate(POOL_WINDOWS):
            sl = slice(g * GROUP, (g + 1) * GROUP)
            pooled = _pool_fwd(proj[:, sl], xp_prev[:, sl], _inv_count(pos, w), w)
            pre = _dot_nn(pooled.astype(BF16), pw_ref[g].astype(BF16))
            mix_scr[:, sl] = pre * ps_ref[:, sl]
        u = _gelu(proj[:, POOL_WIDTH:POOL_WIDTH + SGU_WIDTH])
        v = _gelu(proj[:, POOL_WIDTH + SGU_WIDTH:])
        vhat, _ = _ln_stats(v)
        v_ln = vhat * lg_ref[...] + lb_ref[...]
        for h in range(N_HEADS):
            ws = _masked_sgu_w(sw_ref, h).astype(BF16)
            mixed = _dot_nn(ws, _to_head_major(v_ln, h, nc).astype(BF16)) + sb_ref[:, h:h + 1]
            for c in range(nc):
                rs = slice(c * GROUP, (c + 1) * GROUP)
                mix_scr[rs, POOL_WIDTH + h * GROUP:POOL_WIDTH + (h + 1) * GROUP] = (
                    u[rs, h * GROUP:(h + 1) * GROUP] * mixed[:, c * GROUP:(c + 1) * GROUP])
        mixb = mix_scr[...].astype(BF16)
        mix_ref[...] = mixb
        r1 = ALPHA * x + _dot_nn(mixb, wout_ref[...])
        xhat, rstd = _ln_stats(r1)
        xhat_ref[...] = xhat
        rstd_ref[...] = rstd

        for a in range(n_later):
            relay_tile = min(n_tiles // 2 + a, n_tiles - 1)

            @pl.when(i == relay_tile)
            def _(a=a):
                gather.relay(a)

            @pl.when(i == max(n_tiles - n_later + a, relay_tile))
            def _(a=a):
                gather.pass_on(a)

        @pl.when(i == n_tiles - 1)
        def _():
            gather.finish()

    def tile(cols):
        return pl.BlockSpec((tt, cols), lambda i: (i, 0))

    def whole(a):
        nd = a.ndim
        return pl.BlockSpec(a.shape, lambda i: (0,) * nd)

    any_spec = pl.BlockSpec(memory_space=pl.ANY)
    halo = pl.BlockSpec((HALO, D_MODEL), lambda i: (jnp.maximum(i * (tt // HALO) - 1, 0), 0))
    consts = [win_t, wout, pool_w, pool_scale, sln_g, sln_b, sgu_w, sgu_b_t, ln1_g, ln1_b]
    return pl.pallas_call(
        body, name="mixer_fwd", grid=(n_tiles,),
        in_specs=[tile(D_MODEL), halo] + [whole(a) for a in consts] + [whole(s) for s in later_shards],
        out_specs=[tile(IN_COLS), tile(D_MODEL), tile(1), tile(D_MODEL), tile(D_MODEL)] + [any_spec] * n_later,
        out_shape=[jax.ShapeDtypeStruct((tokens, IN_COLS), F32), jax.ShapeDtypeStruct((tokens, D_MODEL), F32),
                   jax.ShapeDtypeStruct((tokens, 1), F32), jax.ShapeDtypeStruct((tokens, D_MODEL), BF16),
                   jax.ShapeDtypeStruct((tokens, D_MODEL), BF16)] + [_gathered_bf16(s) for s in later_shards],
        scratch_shapes=[pltpu.VMEM((tt, D_MODEL), F32)] + [pltpu.VMEM(s.shape, BF16) for s in later_shards]
                       + _TwoLevelGather.scratch(n_later),
        compiler_params=pltpu.CompilerParams(dimension_semantics=("arbitrary",), vmem_limit_bytes=VMEM_LIMIT_BYTES),
    )(x2d, x2d, *consts, *later_shards)


def _ffn_fwd_bwd(xhat1, rstd1, target, wgu_t, wdown, ln1_g, ln1_b, ln2_g, ln2_b):
    tokens = xhat1.shape[0]
    tt = min(256, tokens)

    def body(xhat_ref, rstd_ref, tgt_ref, wgu_hbm, wd_hbm, g1_ref, b1_ref, g2_ref, b2_ref,
             dr1_ref, acts_ref, grads_ref, stats_ref, wgu_ref, wd_ref, gu_scr, sems):
        i = pl.program_id(0)

        @pl.when(i == 0)
        def _():
            loads = [pltpu.make_async_copy(wgu_hbm, wgu_ref, sems.at[0]),
                     pltpu.make_async_copy(wd_hbm, wd_ref, sems.at[1])]
            for cp in loads:
                cp.start()
            stats_ref[...] = jnp.zeros_like(stats_ref)
            for cp in loads:
                cp.wait()

        xhat1_t = xhat_ref[...]
        h = xhat1_t * g1_ref[...] + b1_ref[...]
        hb = h.astype(BF16)
        acts_ref[:, 0:D_MODEL] = hb
        gate = _dot_nt(hb, wgu_ref[0:D_FF, :])
        up = _dot_nt(hb, wgu_ref[D_FF:, :])
        gu_scr[:, 0:D_FF] = gate
        gu_scr[:, D_FF:] = up
        ab = (gate * jax.nn.sigmoid(gate) * up).astype(BF16)
        grads_ref[:, 2 * D_FF:] = ab
        xhat2, rstd2 = _ln_stats(ALPHA * h + _dot_nn(ab, wd_ref[...]))
        err = xhat2 * g2_ref[...] + b2_ref[...] - tgt_ref[...]
        dy = err * (1.0 / D_MODEL)
        stats_ref[0:1, :] += _rowsum(dy * xhat2)
        stats_ref[1:2, :] += _rowsum(dy)
        stats_ref[4:5, :] += _rowsum(err * err)
        dr2 = _ln_bwd(dy, xhat2, rstd2, g2_ref[...])
        dr2b = dr2.astype(BF16)
        acts_ref[:, D_MODEL:2 * D_MODEL] = dr2b
        da = _dot_nt(dr2b, wd_ref[...])
        gate = gu_scr[:, 0:D_FF]
        up = gu_scr[:, D_FF:]
        sg = jax.nn.sigmoid(gate)
        dgate = (da * up * (sg * (1.0 + gate * (1.0 - sg)))).astype(BF16)
        dup = (da * (gate * sg)).astype(BF16)
        grads_ref[:, 0:D_FF] = dgate
        grads_ref[:, D_FF:2 * D_FF] = dup
        dh = ALPHA * dr2 + _dot_nn(dgate, wgu_ref[0:D_FF, :]) + _dot_nn(dup, wgu_ref[D_FF:, :])
        stats_ref[2:3, :] += _rowsum(dh * xhat1_t)
        stats_ref[3:4, :] += _rowsum(dh)
        dr1 = _ln_bwd(dh, xhat1_t, rstd_ref[...], g1_ref[...])
        dr1_ref[...] = dr1
        acts_ref[:, 2 * D_MODEL:] = dr1.astype(BF16)

    def tile(cols):
        return pl.BlockSpec((tt, cols), lambda i: (i, 0))

    def whole(a):
        nd = a.ndim
        return pl.BlockSpec(a.shape, lambda i: (0,) * nd)

    any_spec = pl.BlockSpec(memory_space=pl.ANY)
    vecs = [ln1_g, ln1_b, ln2_g, ln2_b]
    return pl.pallas_call(
        body, name="ffn_fwd_bwd", grid=(tokens // tt,),
        in_specs=[tile(D_MODEL), tile(1), tile(D_MODEL), any_spec, any_spec] + [whole(a) for a in vecs],
        out_specs=[tile(D_MODEL), tile(3 * D_MODEL), tile(3 * D_FF), pl.BlockSpec((8, D_MODEL), lambda i: (0, 0))],
        out_shape=[jax.ShapeDtypeStruct((tokens, D_MODEL), F32), jax.ShapeDtypeStruct((tokens, 3 * D_MODEL), BF16),
                   jax.ShapeDtypeStruct((tokens, 3 * D_FF), BF16), jax.ShapeDtypeStruct((8, D_MODEL), F32)],
        scratch_shapes=[pltpu.VMEM(wgu_t.shape, BF16), pltpu.VMEM(wdown.shape, BF16),
                        pltpu.VMEM((tt, 2 * D_FF), F32), pltpu.SemaphoreType.DMA((2,))],
        compiler_params=pltpu.CompilerParams(dimension_semantics=("arbitrary",), vmem_limit_bytes=VMEM_LIMIT_BYTES),
    )(xhat1, rstd1, target, wgu_t, wdown, *vecs)


def _wgrad_exchange(lhs, rhs, chips_per_block, name, gather_rows=(), after=None, collective_id=None,
                    lhs_cols=None, rhs_block=None, token_tile=2048):
    assert collective_id is None or not gather_rows
    tokens = lhs.shape[0]
    first_col, n_all = (0, lhs.shape[1]) if lhs_cols is None else lhs_cols
    m = rhs.shape[1] if rhs_block is None else D_MODEL
    n = n_all // N_DEV
    tw = min(token_tile, tokens)
    nt = tokens // tw
    cpb = chips_per_block
    nj = N_CHIPS // cpb
    lhs_block0 = first_col // (2 * n * cpb)
    assert lhs_block0 * 2 * n * cpb == first_col
    rhs_col = 0 if rhs_block is None else rhs_block
    ng = len(gather_rows)
    anchors = [] if after is None else [after]

    def body(l_ref, r_ref, *rest):
        small_refs, rest = rest[:ng], rest[ng + len(anchors):]
        bf_ref, own_ref = rest[:2]
        gathered_refs, rest = rest[2:2 + ng], rest[2 + ng:]
        acc, kept, landed, send_sems, recv_sems = rest[:5]
        sendbuf = bf_ref
        j, t = pl.program_id(0), pl.program_id(1)
        first, last = (j == 0) & (t == 0), (j == nj - 1) & (t == nt - 1)
        x, y, c = _mesh_position()
        if collective_id is not None:
            @pl.when(first)
            def _():
                barrier = pltpu.get_barrier_semaphore()
                pl.semaphore_signal(barrier, inc=1, device_id=(x, y, 1 - c), device_id_type=MESH)
                pl.semaphore_wait(barrier, 1)
        if ng:
            gather = _TwoLevelGather(small_refs, gathered_refs, *rest[5:8])

            @pl.when(first)
            def _():
                gather.start()

        def copy(q):
            return pltpu.make_async_remote_copy(
                src_ref=sendbuf.at[q], dst_ref=landed.at[q], send_sem=send_sems.at[q], recv_sem=recv_sems.at[q],
                device_id=(x, y, 1 - c), device_id_type=MESH)

        @pl.when(t == 0)
        def _():
            acc[...] = _dot_tn(l_ref[...], r_ref[...])

        @pl.when(t > 0)
        def _():
            acc[...] += _dot_tn(l_ref[...], r_ref[...])

        @pl.when(t == nt - 1)
        def _():
            for qq in range(cpb):
                q = j * cpb + qq
                kept[q] = acc[pl.ds(pl.multiple_of(qq * 2 * n + c * n, 8), n), :]
                sendbuf[q] = acc[pl.ds(pl.multiple_of(qq * 2 * n + (1 - c) * n, 8), n), :].astype(BF16)
                copy(q).start()

        if ng:
            @pl.when((j == nj - 1) & (t == nt // 2))
            def _():
                for a in range(ng):
                    gather.relay(a)

            @pl.when(last)
            def _():
                for a in range(ng):
                    gather.pass_on(a)
                gather.finish()

        @pl.when(last)
        def _():
            for q in range(N_CHIPS):
                copy(q).wait_send()
                copy(q).wait_recv()
            for q in range(N_CHIPS):
                s = kept[q] + landed[q].astype(F32)
                bf_ref[q] = s.astype(BF16)

                @pl.when(q == 2 * x + y)
                def _(s=s):
                    own_ref[...] = s

    any_spec = pl.BlockSpec(memory_space=pl.ANY)
    by_chip = (N_CHIPS, n, m)
    return pl.pallas_call(
        body, name=name, grid=(nj, nt),
        in_specs=[pl.BlockSpec((tw, 2 * n * cpb), lambda j, t: (t, lhs_block0 + j)),
                  pl.BlockSpec((tw, m), lambda j, t: (t, rhs_col))] + [any_spec] * (ng + len(anchors)),
        out_specs=[pl.BlockSpec(by_chip, lambda j, t: (0, 0, 0)), pl.BlockSpec((n, m), lambda j, t: (0, 0))]
                  + [any_spec] * ng,
        out_shape=[jax.ShapeDtypeStruct(by_chip, BF16), jax.ShapeDtypeStruct((n, m), F32)]
                  + [_gathered_shape(s) for s in gather_rows],
        scratch_shapes=[pltpu.VMEM((2 * n * cpb, m), F32), pltpu.VMEM(by_chip, F32), pltpu.VMEM(by_chip, BF16),
                        pltpu.SemaphoreType.DMA((N_CHIPS,)), pltpu.SemaphoreType.DMA((N_CHIPS,))]
                       + (_TwoLevelGather.scratch(ng) if ng else []),
        compiler_params=pltpu.CompilerParams(dimension_semantics=("arbitrary", "arbitrary"),
                                             vmem_limit_bytes=VMEM_LIMIT_BYTES + 4 * 1024 * 1024,
                                             collective_id=collective_id),
    )(lhs, rhs, *gather_rows, *anchors)


def _received_shape(p):
    return jax.ShapeDtypeStruct((3,) + p.shape[1:], p.dtype)


ROW_POOL_SCALE, ROW_SLN_G, ROW_SLN_B, ROW_SGU_B = 0, 1, 2, 3
ROW_LN2_G, ROW_LN2_B, ROW_LN1_G, ROW_LN1_B, ROW_LOSS = 8, 9, 10, 11, 12
VEC_ROWS = 16


def _mixer_bwd(dr1, proj, win_t, wout, pool_w, pool_scale, sln_g, sln_b, sgu_w, sgu_b_t, stats, seq):
    tokens = dr1.shape[0]
    tt = min(1024, seq)
    tiles_per_seq = seq // tt
    nc = tt // GROUP
    n_halo_blocks = tokens // HALO
    n_tiles = tokens // tt

    def body(dr1_ref, dr1n_ref, proj_ref, projh_ref, win_ref, wout_ref, pw_ref, ps_ref, lg_ref, lb_ref, sw_ref, sb_ref,
             stats_ref, gx_ref, dproj_ref, dmat_ref, dvec_ref, du_scr, dv_scr):
        i = pl.program_id(0)
        tile_in_seq = i % tiles_per_seq

        @pl.when(i == 0)
        def _():
            dmat_ref[...] = jnp.zeros_like(dmat_ref)
            dvec_ref[0:8, :] = jnp.zeros((8, D_MODEL), F32)
            dvec_ref[8:16, :] = stats_ref[...]

        dr1_t = dr1_ref[...]
        dr1b = dr1_t.astype(BF16)
        dmix = _dot_nt(dr1b, wout_ref[...])
        dpo_next = _dot_nt(dr1n_ref[...].astype(BF16), wout_ref[0:POOL_WIDTH, :])
        dpo_next = jnp.where(tile_in_seq == tiles_per_seq - 1, 0.0, dpo_next)
        proj = proj_ref[...]
        xp_prev = jnp.where(tile_in_seq == 0, 0.0, projh_ref[...])
        pos = tile_in_seq * tt + lax.broadcasted_iota(jnp.int32, (tt, 1), 0)
        pos_next = (tile_in_seq + 1) * tt + lax.broadcasted_iota(jnp.int32, (HALO, 1), 0)

        for g, w in enumerate(POOL_WINDOWS):
            sl = slice(g * GROUP, (g + 1) * GROUP)
            inv_cnt = _inv_count(pos, w)
            pwb = pw_ref[g].astype(BF16)
            pooledb = _pool_fwd(proj[:, sl], xp_prev[:, sl], inv_cnt, w).astype(BF16)
            pre = _dot_nn(pooledb, pwb)
            dpo = dmix[:, sl]
            dvec_ref[ROW_POOL_SCALE:ROW_POOL_SCALE + 1, sl] += _rowsum(dpo * pre)
            dsb = (dpo * ps_ref[:, sl]).astype(BF16)
            dmat_ref[g] += _dot_tn(pooledb, dsb)
            dpooled = _dot_nt(dsb, pwb)
            dpooled_next = _dot_nt((dpo_next[:, sl] * ps_ref[:, sl]).astype(BF16), pwb)
            dxp = _pool_bwd(dpooled, dpooled_next, inv_cnt, _inv_count(pos_next, w), w)
            dproj_ref[:, sl] = dxp.astype(BF16)

        zu = proj[:, POOL_WIDTH:POOL_WIDTH + SGU_WIDTH]
        zv = proj[:, POOL_WIDTH + SGU_WIDTH:]
        u = _gelu(zu)
        vhat, rstd_v = _ln_stats(_gelu(zv))
        v_ln = vhat * lg_ref[...] + lb_ref[...]
        dsg = dmix[:, POOL_WIDTH:]
        row = lax.broadcasted_iota(jnp.int32, (GROUP, GROUP), 0)
        col = lax.broadcasted_iota(jnp.int32, (GROUP, GROUP), 1)
        for h in range(N_HEADS):
            ws = _masked_sgu_w(sw_ref, h).astype(BF16)
            vh = _to_head_major(v_ln, h, nc).astype(BF16)
            mixed = _dot_nn(ws, vh) + sb_ref[:, h:h + 1]
            dsg_h = _to_head_major(dsg, h, nc)
            du_h = dsg_h * mixed
            dm_h = dsg_h * _to_head_major(u, h, nc)
            pos_sums = lax.dot_general(jnp.ones((8, nc * GROUP), F32), dm_h, (((1,), (1,)), ((), ())),
                                       precision=lax.Precision.HIGH, preferred_element_type=F32)
            dvec_ref[ROW_SGU_B + h:ROW_SGU_B + h + 1, 0:GROUP] += pos_sums[0:1, :]
            dmb = dm_h.astype(BF16)
            dmat_ref[len(POOL_WINDOWS) + h] += jnp.where(row >= col, _dot_nt(dmb, vh), 0.0)
            dv_h = _dot_tn(ws, dmb)
            for c in range(nc):
                rs = slice(c * GROUP, (c + 1) * GROUP)
                cs = slice(h * GROUP, (h + 1) * GROUP)
                du_scr[rs, cs] = du_h[:, c * GROUP:(c + 1) * GROUP]
                dv_scr[rs, cs] = dv_h[:, c * GROUP:(c + 1) * GROUP]
        dv_ln = dv_scr[...]
        dvec_ref[ROW_SLN_B:ROW_SLN_B + 1, 0:SGU_WIDTH] += _rowsum(dv_ln)
        dvec_ref[ROW_SLN_G:ROW_SLN_G + 1, 0:SGU_WIDTH] += _rowsum(dv_ln * vhat)
        dv = _ln_bwd(dv_ln, vhat, rstd_v, lg_ref[...])
        dproj_ref[:, POOL_WIDTH:POOL_WIDTH + SGU_WIDTH] = (du_scr[...] * _gelu_grad(zu)).astype(BF16)
        dproj_ref[:, POOL_WIDTH + SGU_WIDTH:] = (dv * _gelu_grad(zv)).astype(BF16)
        gx_ref[...] = ALPHA * dr1_t + _dot_nn(dproj_ref[...], win_ref[...])

    def tile(cols):
        return pl.BlockSpec((tt, cols), lambda i: (i, 0))

    def whole(a):
        nd = a.ndim
        return pl.BlockSpec(a.shape, lambda i: (0,) * nd)

    def resident(shape):
        nd = len(shape)
        return pl.BlockSpec(shape, lambda i: (0,) * nd)

    next_halo = pl.BlockSpec((HALO, D_MODEL), lambda i: (jnp.minimum((i + 1) * (tt // HALO), n_halo_blocks - 1), 0))
    prev_halo = pl.BlockSpec((HALO, POOL_WIDTH), lambda i: (jnp.maximum(i * (tt // HALO) - 1, 0), 0))
    consts = [win_t, wout, pool_w, pool_scale, sln_g, sln_b, sgu_w, sgu_b_t, stats]
    small_shapes = [(len(POOL_WINDOWS) + N_HEADS, GROUP, GROUP), (VEC_ROWS, D_MODEL)]
    return pl.pallas_call(
        body, name="mixer_bwd", grid=(n_tiles,),
        in_specs=[tile(D_MODEL), next_halo, tile(IN_COLS), prev_halo] + [whole(a) for a in consts],
        out_specs=[tile(D_MODEL), tile(IN_COLS)] + [resident(s) for s in small_shapes],
        out_shape=[jax.ShapeDtypeStruct((tokens, D_MODEL), F32), jax.ShapeDtypeStruct((tokens, IN_COLS), BF16)]
                  + [jax.ShapeDtypeStruct(s, F32) for s in small_shapes],
        scratch_shapes=[pltpu.VMEM((tt, SGU_WIDTH), F32), pltpu.VMEM((tt, SGU_WIDTH), F32)],
        compiler_params=pltpu.CompilerParams(dimension_semantics=("arbitrary",), vmem_limit_bytes=VMEM_LIMIT_BYTES),
    )(dr1, dr1, proj, proj, *consts)


def _owner_copies(src_ref, land_ref, sems):
    x, y, c = _mesh_position()
    return [pltpu.make_async_remote_copy(
        src_ref=src_ref.at[2 * cx + cy], dst_ref=land_ref.at[j], send_sem=sems[j], recv_sem=sems[3 + j],
        device_id=(cx, cy, c), device_id_type=MESH) for j, (cx, cy) in enumerate(_other_chips(x, y))]


def _send_to_owners_start(chip_partial, name, collective_id):
    land = _received_shape(chip_partial)

    def body(src_ref, land_ref, *rest):
        x, y, c = _mesh_position()
        barrier = pltpu.get_barrier_semaphore()
        for cx, cy in _other_chips(x, y):
            pl.semaphore_signal(barrier, inc=1, device_id=(cx, cy, c), device_id_type=MESH)
        pl.semaphore_wait(barrier, 3)
        for cp in _owner_copies(src_ref, land_ref, rest[:6]):
            cp.start()
        rest[8][...] = jnp.zeros_like(rest[8])

    hbm = pl.BlockSpec(memory_space=pltpu.HBM)
    sem = pl.BlockSpec(memory_space=pltpu.SEMAPHORE)
    outs = pl.pallas_call(
        body, name=name,
        out_shape=[pltpu.SemaphoreType.DMA(())] * 6 + [pltpu.HBM(chip_partial.shape, chip_partial.dtype),
                                                       pltpu.HBM(land.shape, land.dtype),
                                                       jax.ShapeDtypeStruct((8, GROUP), F32)],
        in_specs=[hbm, hbm], out_specs=[sem] * 6 + [hbm, hbm, pl.BlockSpec(memory_space=pltpu.VMEM)],
        input_output_aliases={0: 6, 1: 7},
        compiler_params=pltpu.CompilerParams(has_side_effects=pltpu.SideEffectType.DATAFLOW_SIDE_EFFECTING,
                                             collective_id=collective_id),
    )(pltpu.with_memory_space_constraint(chip_partial, pltpu.HBM),
      pltpu.with_memory_space_constraint(lax.empty(land.shape, land.dtype), pltpu.HBM))
    return outs[:6], outs[6], outs[7], outs[8]


def _send_to_owners_wait(sems, src_thru, land_thru, after, name):
    def body(src_ref, land_ref, *rest):
        for cp in _owner_copies(src_ref, land_ref, rest[:6]):
            cp.wait_send()
            cp.wait_recv()

    hbm = pl.BlockSpec(memory_space=pltpu.HBM)
    sem = pl.BlockSpec(memory_space=pltpu.SEMAPHORE)
    return pl.pallas_call(
        body, name=name,
        out_shape=[pltpu.HBM(src_thru.shape, src_thru.dtype), pltpu.HBM(land_thru.shape, land_thru.dtype)],
        in_specs=[hbm, hbm] + [sem] * 6 + [pl.BlockSpec(memory_space=pl.ANY)], out_specs=[hbm, hbm],
        input_output_aliases={0: 0, 1: 1},
        compiler_params=pltpu.CompilerParams(has_side_effects=pltpu.SideEffectType.DATAFLOW_SIDE_EFFECTING),
    )(src_thru, land_thru, *sems, after)[1]


def _adamw_math(w, g, m, v):
    m = ADAM_B1 * m + (1.0 - ADAM_B1) * g
    v = ADAM_B2 * v + (1.0 - ADAM_B2) * (g * g)
    m_hat = m / (1.0 - ADAM_B1 ** ADAM_STEP)
    v_hat = v / (1.0 - ADAM_B2 ** ADAM_STEP)
    delta = -ADAM_LR * (m_hat / (jnp.sqrt(v_hat) + ADAM_EPS) + ADAM_WD * w)
    return delta, m, v


def _sum_adamw(w, m, v, own, received, name):
    rows, cols = w.shape
    rb = _row_block(rows, 256)

    def body(w_ref, m_ref, v_ref, own_ref, rec_ref, g_ref, d_ref, mo_ref, vo_ref):
        g = own_ref[...]
        for j in range(3):
            g = g + rec_ref[j].astype(F32)
        g_ref[...] = g
        d_ref[...], mo_ref[...], vo_ref[...] = _adamw_math(w_ref[...], g, m_ref[...], v_ref[...])

    spec = pl.BlockSpec((rb, cols), lambda r: (r, 0))
    return pl.pallas_call(
        body, name=name, grid=(rows // rb,),
        in_specs=[spec] * 4 + [pl.BlockSpec((3, rb, cols), lambda r: (0, r, 0))],
        out_specs=[spec] * 4, out_shape=[jax.ShapeDtypeStruct((rows, cols), F32)] * 4,
        compiler_params=pltpu.CompilerParams(dimension_semantics=("arbitrary",)),
    )(w, m, v, own, received)


SMALL_NAMES = ("pool_w", "sgu_w", "pool_scale", "sgu_ln_g", "sgu_ln_b", "sgu_b", "ln1_g", "ln1_b", "ln2_g", "ln2_b")
_SMALL_VEC_ROWS = {"pool_scale": (ROW_POOL_SCALE, POOL_WIDTH), "sgu_ln_g": (ROW_SLN_G, SGU_WIDTH),
                   "sgu_ln_b": (ROW_SLN_B, SGU_WIDTH), "ln1_g": (ROW_LN1_G, D_MODEL), "ln1_b": (ROW_LN1_B, D_MODEL),
                   "ln2_g": (ROW_LN2_G, D_MODEL), "ln2_b": (ROW_LN2_B, D_MODEL)}
_SMALL_MAT_FIRST = {"pool_w": 0, "sgu_w": len(POOL_WINDOWS)}


def _small_sum_adamw(mats_all, vecs_all, w, m, v):
    n = len(SMALL_NAMES)

    def body(mats_ref, vecs_ref, *refs):
        w_refs, m_refs, v_refs = refs[:n], refs[n:2 * n], refs[2 * n:3 * n]
        loss_ref = refs[3 * n]
        g_refs, d_refs, mo_refs, vo_refs = (refs[3 * n + 1 + k * n:3 * n + 1 + (k + 1) * n] for k in range(4))
        vec_scr = refs[7 * n + 1]

        def update(k, idx, g):
            d, mo, vo = _adamw_math(w_refs[k][idx], g, m_refs[k][idx], v_refs[k][idx])
            g_refs[k][idx], d_refs[k][idx], mo_refs[k][idx], vo_refs[k][idx] = g, d, mo, vo

        total = vecs_ref[0]
        for dev in range(1, N_DEV):
            total = total + vecs_ref[dev]
        vec_scr[...] = total
        for k, name in enumerate(SMALL_NAMES):
            if name in _SMALL_MAT_FIRST:
                for b in range(4):
                    g = mats_ref[0, _SMALL_MAT_FIRST[name] + b]
                    for dev in range(1, N_DEV):
                        g = g + mats_ref[dev, _SMALL_MAT_FIRST[name] + b]
                    update(k, (0, b), g)
            elif name == "sgu_b":
                update(k, (0,), vec_scr[ROW_SGU_B:ROW_SGU_B + N_HEADS, 0:GROUP])
            else:
                row, width = _SMALL_VEC_ROWS[name]
                update(k, (slice(None), slice(None)), vec_scr[row:row + 1, 0:width])
        loss = jnp.sum(vec_scr[ROW_LOSS:ROW_LOSS + 1, :], axis=1, keepdims=True) * (0.5 / D_MODEL)
        loss_ref[...] = jnp.broadcast_to(loss, loss_ref.shape)

    vmem = pl.BlockSpec(memory_space=pltpu.VMEM)
    shapes = [jax.ShapeDtypeStruct(w[k].shape, F32) for k in SMALL_NAMES]
    outs = pl.pallas_call(
        body, name="small_sum_adamw",
        out_shape=[jax.ShapeDtypeStruct((8, GROUP), F32)] + shapes * 4,
        in_specs=[vmem] * (2 + 3 * n), out_specs=[vmem] * (1 + 4 * n),
        scratch_shapes=[pltpu.VMEM((VEC_ROWS, D_MODEL), F32)],
    )(mats_all, vecs_all, *[w[k] for k in SMALL_NAMES], *[m[k] for k in SMALL_NAMES], *[v[k] for k in SMALL_NAMES])
    return outs[0], outs[1:1 + n], outs[1 + n:1 + 2 * n], outs[1 + 2 * n:1 + 3 * n], outs[1 + 3 * n:]


def kernel(x, w_in, pool_w, pool_scale, sgu_ln_g, sgu_ln_b, sgu_w, sgu_b, w_out, ln1_g, ln1_b, w_gate_up, w_down, ln2_g, ln2_b, loss_target, m_w_in, m_pool_w, m_pool_scale, m_sgu_ln_g, m_sgu_ln_b, m_sgu_w, m_sgu_b, m_w_out, m_ln1_g, m_ln1_b, m_w_gate_up, m_w_down, m_ln2_g, m_ln2_b, v_w_in, v_pool_w, v_pool_scale, v_sgu_ln_g, v_sgu_ln_b, v_sgu_w, v_sgu_b, v_w_out, v_ln1_g, v_ln1_b, v_w_gate_up, v_w_down, v_ln2_g, v_ln2_b):
    bl, seq, _ = x.shape
    tokens = bl * seq
    x2d = x.reshape(tokens, D_MODEL)
    tgt2d = loss_target.reshape(tokens, D_MODEL)

    win_t, wout = _all_gather_rows([w_in[0].T, w_out[0]], "weight_all_gather")

    pool_w3, sgu_w3 = pool_w[0], sgu_w[0]
    sgu_b_t = sgu_b[0].T
    proj, xhat1, rstd1, mix_bf, x_bf, wgu_t, wdown = _mixer_fwd(
        x2d, win_t, wout, pool_w3, pool_scale, sgu_ln_g, sgu_ln_b, sgu_w3, sgu_b_t, ln1_g, ln1_b,
        [w_gate_up[0].T, w_down[0]], seq)
    dr1, acts_bf, grads_bf, stats = _ffn_fwd_bwd(xhat1, rstd1, tgt2d, wgu_t, wdown, ln1_g, ln1_b, ln2_g, ln2_b)

    bf_gu, own_gu = _wgrad_exchange(grads_bf, acts_bf, 1, "wgrad_gate_up", collective_id=1,
                                    lhs_cols=(0, 2 * D_FF), rhs_block=0)
    sent_gu = _send_to_owners_start(bf_gu, "grad_scatter_w_gate_up_start", collective_id=4)
    bf_dn, own_dn = _wgrad_exchange(grads_bf, acts_bf, 2, "wgrad_down", after=sent_gu[3], collective_id=2,
                                    lhs_cols=(2 * D_FF, D_FF), rhs_block=1)
    sent_dn = _send_to_owners_start(bf_dn, "grad_scatter_w_down_start", collective_id=5)
    bf_out, own_out = _wgrad_exchange(mix_bf, acts_bf, N_CHIPS, "wgrad_out", after=sent_dn[3], collective_id=3,
                                      rhs_block=2)
    sent_out = _send_to_owners_start(bf_out, "grad_scatter_w_out_start", collective_id=6)
    gx, dproj_bf, d_mats, d_vecs = _mixer_bwd(
        dr1, proj, win_t, wout, pool_w3, pool_scale, sgu_ln_g, sgu_ln_b, sgu_w3, sgu_b_t, stats, seq)
    bf_in, own_in, mats_all, vecs_all = _wgrad_exchange(
        dproj_bf, x_bf, N_CHIPS, "wgrad_in", gather_rows=[d_mats.reshape(-1, GROUP), d_vecs], after=sent_out[3])
    sent_in = _send_to_owners_start(bf_in, "grad_scatter_w_in_start", collective_id=7)

    grads, deltas, new_m, new_v = {}, {}, {}, {}
    after = sent_in[3]
    for nm, w, m, v, own, sent, transposed in (("w_gate_up", w_gate_up, m_w_gate_up, v_w_gate_up, own_gu, sent_gu, True),
                                               ("w_down", w_down, m_w_down, v_w_down, own_dn, sent_dn, False),
                                               ("w_out", w_out, m_w_out, v_w_out, own_out, sent_out, False),
                                               ("w_in", w_in, m_w_in, v_w_in, own_in, sent_in, True)):
        rows = (lambda a: a[0].T) if transposed else (lambda a: a[0])
        back = (lambda a: a.T[None]) if transposed else (lambda a: a[None])
        rec = _send_to_owners_wait(*sent[:3], after, "grad_scatter_" + nm + "_wait")
        g, d, mo, vo = _sum_adamw(rows(w), rows(m), rows(v), own, rec, "adamw_" + nm)
        after = vo
        grads[nm], deltas[nm], new_m[nm], new_v[nm] = back(g), back(d), back(mo), back(vo)

    small_w = {"pool_w": pool_w, "pool_scale": pool_scale, "sgu_ln_g": sgu_ln_g, "sgu_ln_b": sgu_ln_b, "sgu_w": sgu_w,
               "sgu_b": sgu_b, "ln1_g": ln1_g, "ln1_b": ln1_b, "ln2_g": ln2_g, "ln2_b": ln2_b}
    small_m = {"pool_w": m_pool_w, "pool_scale": m_pool_scale, "sgu_ln_g": m_sgu_ln_g, "sgu_ln_b": m_sgu_ln_b,
               "sgu_w": m_sgu_w, "sgu_b": m_sgu_b, "ln1_g": m_ln1_g, "ln1_b": m_ln1_b, "ln2_g": m_ln2_g, "ln2_b": m_ln2_b}
    small_v = {"pool_w": v_pool_w, "pool_scale": v_pool_scale, "sgu_ln_g": v_sgu_ln_g, "sgu_ln_b": v_sgu_ln_b,
               "sgu_w": v_sgu_w, "sgu_b": v_sgu_b, "ln1_g": v_ln1_g, "ln1_b": v_ln1_b, "ln2_g": v_ln2_g, "ln2_b": v_ln2_b}
    loss_blk, g_small, d_small, m_small, v_small = _small_sum_adamw(
        mats_all.reshape(N_DEV, 2 * N_HEADS, GROUP, GROUP), vecs_all.reshape(N_DEV, VEC_ROWS, D_MODEL),
        small_w, small_m, small_v)
    for vals, dst in ((g_small, grads), (d_small, deltas), (m_small, new_m), (v_small, new_v)):
        dst.update(zip(SMALL_NAMES, vals))

    order = ["w_in", "pool_w", "pool_scale", "sgu_ln_g", "sgu_ln_b", "sgu_w", "sgu_b", "w_out", "ln1_g", "ln1_b",
             "w_gate_up", "w_down", "ln2_g", "ln2_b"]
    return (loss_blk[0, 0], gx.reshape(bl, seq, D_MODEL), *[grads[k] for k in order], *[deltas[k] for k in order],
            *[new_m[k] for k in order], *[new_v[k] for k in order])
```

```python
import jax
import jax.numpy as jnp
from jax import lax
from jax.experimental import pallas as pl
from jax.experimental.pallas import tpu as pltpu

F32 = jnp.float32
BF16 = jnp.bfloat16
MESH = pl.DeviceIdType.MESH

D_MODEL = 1024
POOL_WIDTH = 512
SGU_WIDTH = 512
POOL_WINDOWS = (2, 4, 8, 16)
GROUP = 128
N_HEADS = 4
IN_COLS = POOL_WIDTH + 2 * SGU_WIDTH
D_FF = 2816
LN_EPS = 1e-5
ALPHA = float(2.0 ** 0.25)
HALO = 16
N_DEV = 8
N_CHIPS = 4

ADAM_LR = 0.001
ADAM_B1 = 0.9
ADAM_B2 = 0.999
ADAM_EPS = 1e-08
ADAM_WD = 0.01
ADAM_STEP = 10

VMEM_LIMIT_BYTES = 56 * 1024 * 1024

_SQRT_HALF = 0.7071067811865476
_INV_SQRT_2PI = 0.3989422804014327


def _dot_nn(a, b):
    return lax.dot_general(a, b, (((1,), (0,)), ((), ())), preferred_element_type=F32)


def _dot_nt(a, b):
    return lax.dot_general(a, b, (((1,), (1,)), ((), ())), preferred_element_type=F32)


def _dot_tn(a, b):
    return lax.dot_general(a, b, (((0,), (0,)), ((), ())), preferred_element_type=F32)


def _gelu(x):
    return 0.5 * x * (1.0 + lax.erf(x * _SQRT_HALF))


def _gelu_grad(x):
    return 0.5 * (1.0 + lax.erf(x * _SQRT_HALF)) + x * jnp.exp(-0.5 * x * x) * _INV_SQRT_2PI


def _ln_stats(r):
    mu = jnp.mean(r, axis=-1, keepdims=True)
    d = r - mu
    var = jnp.mean(d * d, axis=-1, keepdims=True)
    rstd = lax.rsqrt(var + LN_EPS)
    return d * rstd, rstd


def _ln_bwd(dout, xhat, rstd, g):
    dxh = dout * g
    m1 = jnp.mean(dxh, axis=-1, keepdims=True)
    m2 = jnp.mean(dxh * xhat, axis=-1, keepdims=True)
    return rstd * (dxh - m1 - xhat * m2)


def _rowsum(a):
    return jnp.sum(a, axis=0, keepdims=True)


def _pool_fwd(xp, xp_prev, inv_cnt, w):
    s = jnp.concatenate([xp_prev, xp], axis=0)
    k = 1
    while k < w:
        s = s + pltpu.roll(s, k, 0)
        k *= 2
    return s[HALO:, :] * inv_cnt - xp


def _pool_bwd(dpooled, dpooled_next, inv_cnt, inv_cnt_next, w):
    n = dpooled.shape[0] + HALO
    s = jnp.concatenate([dpooled * inv_cnt, dpooled_next * inv_cnt_next], axis=0)
    k = 1
    while k < w:
        s = s + pltpu.roll(s, n - k, 0)
        k *= 2
    return s[: dpooled.shape[0], :] - dpooled


def _inv_count(pos, w):
    return 1.0 / jnp.minimum(pos + 1, w).astype(F32)


def _to_head_major(a, h, nc):
    return jnp.concatenate(
        [a[c * GROUP:(c + 1) * GROUP, h * GROUP:(h + 1) * GROUP] for c in range(nc)], axis=1)


def _masked_sgu_w(sw_ref, h):
    row = lax.broadcasted_iota(jnp.int32, (GROUP, GROUP), 0)
    col = lax.broadcasted_iota(jnp.int32, (GROUP, GROUP), 1)
    return jnp.where(row >= col, sw_ref[h], 0.0)


def _row_block(rows, limit):
    return max(b for b in range(16, min(rows, limit) + 1, 16) if rows % b == 0)


def _mesh_position():
    return lax.axis_index("x"), lax.axis_index("y"), lax.axis_index("c")


def _other_chips(x, y):
    return [(1 - x, y), (x, 1 - y), (1 - x, 1 - y)]


class _TwoLevelGather:
    def __init__(self, ins, outs, send_sems, recv_sems, local_sems):
        self.ins, self.outs = ins, outs
        self.send_sems, self.recv_sems, self.local_sems = send_sems, recv_sems, local_sems
        self.na = len(ins)
        x, y, c = _mesh_position()
        self.c = c
        self.me, self.sibling = (x, y, c), (x, y, 1 - c)
        self.chips = _other_chips(x, y)
        self.relay_from = (x + (1 - c) * (1 - 2 * x), y + c * (1 - 2 * y))
        self.relay_to = (x + c * (1 - 2 * x), y + (1 - c) * (1 - 2 * y))

    def handshake(self):
        peers = [self.sibling] + [(*chip, self.c) for chip in self.chips[:2]]
        barrier = pltpu.get_barrier_semaphore()
        for peer in peers:
            pl.semaphore_signal(barrier, inc=1, device_id=peer, device_id_type=MESH)
        pl.semaphore_wait(barrier, len(peers))

    def _rows(self, a, px, py, pc):
        n = self.ins[a].shape[0]
        return self.outs[a].at[pl.ds((4 * px + 2 * py + pc) * n, n), :]

    def _copy(self, a, k, block, to, src=None):
        return pltpu.make_async_remote_copy(
            src_ref=self._rows(a, *block) if src is None else src, dst_ref=self._rows(a, *block),
            send_sem=self.send_sems.at[a * 7 + k], recv_sem=self.recv_sems.at[a * 7 + k],
            device_id=to, device_id_type=MESH)

    def _mine(self, a):
        return pltpu.make_async_copy(self.ins[a], self._rows(a, *self.me), self.local_sems.at[a])

    def start(self):
        for a in range(self.na):
            self._mine(a).start()
        for a in range(self.na):
            self._copy(a, 0, self.me, self.sibling, src=self.ins[a]).start()
            for j, chip in enumerate(self.chips[:2]):
                self._copy(a, 1 + j, self.me, (*chip, self.c), src=self.ins[a]).start()

    def relay(self, a):
        c, block = self.c, (*self.relay_from, self.c)
        self._copy(a, 1 + c, block, self.me).wait_recv()
        self._copy(a, 3, block, (*self.relay_to, c)).start()
        self._copy(a, 4 + c, block, self.sibling).start()

    def pass_on(self, a):
        c = self.c
        self._copy(a, 2 - c, (*self.relay_to, c), self.me).wait_recv()
        self._copy(a, 5 - c, (*self.relay_to, c), self.sibling).start()
        self._copy(a, 3, (*self.chips[2], c), self.me).wait_recv()
        self._copy(a, 6, (*self.chips[2], c), self.sibling).start()

    def finish(self):
        for a in range(self.na):
            self._copy(a, 0, self.sibling, self.me).wait_recv()
            for j, chip in enumerate(self.chips):
                self._copy(a, 4 + j, (*chip, 1 - self.c), self.me).wait_recv()
        for a in range(self.na):
            for k in range(7):
                self._copy(a, k, self.me, self.sibling, src=self.ins[a]).wait_send()
            self._mine(a).wait()

    @staticmethod
    def scratch(na):
        return [pltpu.SemaphoreType.DMA((7 * na,)), pltpu.SemaphoreType.DMA((7 * na,)), pltpu.SemaphoreType.DMA((na,))]


def _gathered_shape(s):
    return jax.ShapeDtypeStruct((N_DEV * s.shape[0], s.shape[1]), s.dtype)


def _gathered_bf16(s):
    return jax.ShapeDtypeStruct((N_DEV * s.shape[0], s.shape[1]), BF16)


def _all_gather_rows(shards, name):
    na = len(shards)

    def body(*refs):
        bf_refs = refs[2 * na:3 * na]
        for a in range(na):
            bf_refs[a][...] = refs[a][...].astype(BF16)
        gather = _TwoLevelGather(bf_refs, refs[na:2 * na], *refs[3 * na:])
        gather.handshake()
        gather.start()
        for a in range(na):
            gather.relay(a)
        for a in range(na):
            gather.pass_on(a)
        gather.finish()

    return pl.pallas_call(
        body, name=name, out_shape=[_gathered_bf16(s) for s in shards],
        in_specs=[pl.BlockSpec(memory_space=pltpu.VMEM)] * na, out_specs=[pl.BlockSpec(memory_space=pl.ANY)] * na,
        scratch_shapes=[pltpu.VMEM(s.shape, BF16) for s in shards] + _TwoLevelGather.scratch(na),
        compiler_params=pltpu.CompilerParams(collective_id=8),
    )(*shards)


def _mixer_fwd(x2d, win_t, wout, pool_w, pool_scale, sln_g, sln_b, sgu_w, sgu_b_t, ln1_g, ln1_b, later_shards, seq):
    tokens = x2d.shape[0]
    tt = min(512, seq)
    tiles_per_seq = seq // tt
    nc = tt // GROUP
    n_tiles = tokens // tt
    n_later = len(later_shards)

    def body(x_ref, xh_ref, win_ref, wout_ref, pw_ref, ps_ref, lg_ref, lb_ref, sw_ref, sb_ref, g1_ref, b1_ref, *rest):
        shard_refs, rest = rest[:n_later], rest[n_later:]
        proj_ref, xhat_ref, rstd_ref, mix_ref, xbf_ref = rest[:5]
        gathered_refs, rest = rest[5:5 + n_later], rest[5 + n_later:]
        mix_scr, shard_bf_refs, (send_sems, recv_sems, local_sems) = rest[0], rest[1:1 + n_later], rest[1 + n_later:]
        i = pl.program_id(0)
        gather = _TwoLevelGather(shard_bf_refs, gathered_refs, send_sems, recv_sems, local_sems)

        @pl.when(i == 0)
        def _():
            gather.handshake()
            for a in range(n_later):
                shard_bf_refs[a][...] = shard_refs[a][...].astype(BF16)
            gather.start()

        tile_in_seq = i % tiles_per_seq
        x = x_ref[...]
        xb = x.astype(BF16)
        xbf_ref[...] = xb
        proj = _dot_nt(xb, win_ref[...])
        proj_ref[...] = proj
        xp_prev = _dot_nt(xh_ref[...].astype(BF16), win_ref[0:POOL_WIDTH, :])
        xp_prev = jnp.where(tile_in_seq == 0, 0.0, xp_prev)
        pos = tile_in_seq * tt + lax.broadcasted_iota(jnp.int32, (tt, 1), 0)
        for g, w in enumerate(POOL_WINDOWS):
            sl = slice(g * GROUP, (g + 1) * GROUP)
            pooled = _pool_fwd(proj[:, sl], xp_prev[:, sl], _inv_count(pos, w), w)
            pre = _dot_nn(pooled.astype(BF16), pw_ref[g].astype(BF16))
            mix_scr[:, sl] = pre * ps_ref[:, sl]
        u = _gelu(proj[:, POOL_WIDTH:POOL_WIDTH + SGU_WIDTH])
        v = _gelu(proj[:, POOL_WIDTH + SGU_WIDTH:])
        vhat, _ = _ln_stats(v)
        v_ln = vhat * lg_ref[...] + lb_ref[...]
        for h in range(N_HEADS):
            ws = _masked_sgu_w(sw_ref, h).astype(BF16)
            mixed = _dot_nn(ws, _to_head_major(v_ln, h, nc).astype(BF16)) + sb_ref[:, h:h + 1]
            for c in range(nc):
                rs = slice(c * GROUP, (c + 1) * GROUP)
                mix_scr[rs, POOL_WIDTH + h * GROUP:POOL_WIDTH + (h + 1) * GROUP] = (
                    u[rs, h * GROUP:(h + 1) * GROUP] * mixed[:, c * GROUP:(c + 1) * GROUP])
        mixb = mix_scr[...].astype(BF16)
        mix_ref[...] = mixb
        r1 = ALPHA * x + _dot_nn(mixb, wout_ref[...])
        xhat, rstd = _ln_stats(r1)
        xhat_ref[...] = xhat
        rstd_ref[...] = rstd

        for a in range(n_later):
            relay_tile = min(n_tiles // 2 + a, n_tiles - 1)

            @pl.when(i == relay_tile)
            def _(a=a):
                gather.relay(a)

            @pl.when(i == max(n_tiles - n_later + a, relay_tile))
            def _(a=a):
                gather.pass_on(a)

        @pl.when(i == n_tiles - 1)
        def _():
            gather.finish()

    def tile(cols):
        return pl.BlockSpec((tt, cols), lambda i: (i, 0))

    def whole(a):
        nd = a.ndim
        return pl.BlockSpec(a.shape, lambda i: (0,) * nd)

    any_spec = pl.BlockSpec(memory_space=pl.ANY)
    halo = pl.BlockSpec((HALO, D_MODEL), lambda i: (jnp.maximum(i * (tt // HALO) - 1, 0), 0))
    consts = [win_t, wout, pool_w, pool_scale, sln_g, sln_b, sgu_w, sgu_b_t, ln1_g, ln1_b]
    return pl.pallas_call(
        body, name="mixer_fwd", grid=(n_tiles,),
        in_specs=[tile(D_MODEL), halo] + [whole(a) for a in consts] + [whole(s) for s in later_shards],
        out_specs=[tile(IN_COLS), tile(D_MODEL), tile(1), tile(D_MODEL), tile(D_MODEL)] + [any_spec] * n_later,
        out_shape=[jax.ShapeDtypeStruct((tokens, IN_COLS), F32), jax.ShapeDtypeStruct((tokens, D_MODEL), F32),
                   jax.ShapeDtypeStruct((tokens, 1), F32), jax.ShapeDtypeStruct((tokens, D_MODEL), BF16),
                   jax.ShapeDtypeStruct((tokens, D_MODEL), BF16)] + [_gathered_bf16(s) for s in later_shards],
        scratch_shapes=[pltpu.VMEM((tt, D_MODEL), F32)] + [pltpu.VMEM(s.shape, BF16) for s in later_shards]
                       + _TwoLevelGather.scratch(n_later),
        compiler_params=pltpu.CompilerParams(dimension_semantics=("arbitrary",), vmem_limit_bytes=VMEM_LIMIT_BYTES,
                                             collective_id=9),
    )(x2d, x2d, *consts, *later_shards)


def _ffn_fwd_bwd(xhat1, rstd1, target, wgu_t, wdown, ln1_g, ln1_b, ln2_g, ln2_b):
    tokens = xhat1.shape[0]
    tt = min(256, tokens)

    def body(xhat_ref, rstd_ref, tgt_ref, wgu_hbm, wd_hbm, g1_ref, b1_ref, g2_ref, b2_ref,
             dr1_ref, acts_ref, grads_ref, stats_ref, wgu_ref, wd_ref, gu_scr, sems):
        i = pl.program_id(0)

        @pl.when(i == 0)
        def _():
            loads = [pltpu.make_async_copy(wgu_hbm, wgu_ref, sems.at[0]),
                     pltpu.make_async_copy(wd_hbm, wd_ref, sems.at[1])]
            for cp in loads:
                cp.start()
            stats_ref[...] = jnp.zeros_like(stats_ref)
            for cp in loads:
                cp.wait()

        xhat1_t = xhat_ref[...]
        h = xhat1_t * g1_ref[...] + b1_ref[...]
        hb = h.astype(BF16)
        acts_ref[:, 0:D_MODEL] = hb
        gate = _dot_nt(hb, wgu_ref[0:D_FF, :])
        up = _dot_nt(hb, wgu_ref[D_FF:, :])
        gu_scr[:, 0:D_FF] = gate
        gu_scr[:, D_FF:] = up
        ab = (gate * jax.nn.sigmoid(gate) * up).astype(BF16)
        grads_ref[:, 2 * D_FF:] = ab
        xhat2, rstd2 = _ln_stats(ALPHA * h + _dot_nn(ab, wd_ref[...]))
        err = xhat2 * g2_ref[...] + b2_ref[...] - tgt_ref[...]
        dy = err * (1.0 / D_MODEL)
        stats_ref[0:1, :] += _rowsum(dy * xhat2)
        stats_ref[1:2, :] += _rowsum(dy)
        stats_ref[4:5, :] += _rowsum(err * err)
        dr2 = _ln_bwd(dy, xhat2, rstd2, g2_ref[...])
        dr2b = dr2.astype(BF16)
        acts_ref[:, D_MODEL:2 * D_MODEL] = dr2b
        da = _dot_nt(dr2b, wd_ref[...])
        gate = gu_scr[:, 0:D_FF]
        up = gu_scr[:, D_FF:]
        sg = jax.nn.sigmoid(gate)
        dgate = (da * up * (sg * (1.0 + gate * (1.0 - sg)))).astype(BF16)
        dup = (da * (gate * sg)).astype(BF16)
        grads_ref[:, 0:D_FF] = dgate
        grads_ref[:, D_FF:2 * D_FF] = dup
        dh = ALPHA * dr2 + _dot_nn(dgate, wgu_ref[0:D_FF, :]) + _dot_nn(dup, wgu_ref[D_FF:, :])
        stats_ref[2:3, :] += _rowsum(dh * xhat1_t)
        stats_ref[3:4, :] += _rowsum(dh)
        dr1 = _ln_bwd(dh, xhat1_t, rstd_ref[...], g1_ref[...])
        dr1_ref[...] = dr1
        acts_ref[:, 2 * D_MODEL:] = dr1.astype(BF16)

    def tile(cols):
        return pl.BlockSpec((tt, cols), lambda i: (i, 0))

    def whole(a):
        nd = a.ndim
        return pl.BlockSpec(a.shape, lambda i: (0,) * nd)

    any_spec = pl.BlockSpec(memory_space=pl.ANY)
    vecs = [ln1_g, ln1_b, ln2_g, ln2_b]
    return pl.pallas_call(
        body, name="ffn_fwd_bwd", grid=(tokens // tt,),
        in_specs=[tile(D_MODEL), tile(1), tile(D_MODEL), any_spec, any_spec] + [whole(a) for a in vecs],
        out_specs=[tile(D_MODEL), tile(3 * D_MODEL), tile(3 * D_FF), pl.BlockSpec((8, D_MODEL), lambda i: (0, 0))],
        out_shape=[jax.ShapeDtypeStruct((tokens, D_MODEL), F32), jax.ShapeDtypeStruct((tokens, 3 * D_MODEL), BF16),
                   jax.ShapeDtypeStruct((tokens, 3 * D_FF), BF16), jax.ShapeDtypeStruct((8, D_MODEL), F32)],
        scratch_shapes=[pltpu.VMEM(wgu_t.shape, BF16), pltpu.VMEM(wdown.shape, BF16),
                        pltpu.VMEM((tt, 2 * D_FF), F32), pltpu.SemaphoreType.DMA((2,))],
        compiler_params=pltpu.CompilerParams(dimension_semantics=("arbitrary",), vmem_limit_bytes=VMEM_LIMIT_BYTES),
    )(xhat1, rstd1, target, wgu_t, wdown, *vecs)


def _wgrad_exchange(lhs, rhs, chips_per_block, name, gather_rows=(), after=None, collective_id=None,
                    lhs_cols=None, rhs_block=None, token_tile=2048):
    tokens = lhs.shape[0]
    first_col, n_all = (0, lhs.shape[1]) if lhs_cols is None else lhs_cols
    m = rhs.shape[1] if rhs_block is None else D_MODEL
    n = n_all // N_DEV
    tw = min(token_tile, tokens)
    nt = tokens // tw
    cpb = chips_per_block
    nj = N_CHIPS // cpb
    lhs_block0 = first_col // (2 * n * cpb)
    assert lhs_block0 * 2 * n * cpb == first_col
    rhs_col = 0 if rhs_block is None else rhs_block
    ng = len(gather_rows)
    anchors = [] if after is None else [after]

    def body(l_ref, r_ref, *rest):
        small_refs, rest = rest[:ng], rest[ng + len(anchors):]
        bf_ref, own_ref = rest[:2]
        gathered_refs, rest = rest[2:2 + ng], rest[2 + ng:]
        acc, kept, landed, send_sems, recv_sems = rest[:5]
        sendbuf = bf_ref
        j, t = pl.program_id(0), pl.program_id(1)
        first, last = (j == 0) & (t == 0), (j == nj - 1) & (t == nt - 1)
        x, y, c = _mesh_position()
        if ng:
            gather = _TwoLevelGather(small_refs, gathered_refs, *rest[5:8])

            @pl.when(first)
            def _():
                gather.handshake()
                gather.start()
        else:
            @pl.when(first)
            def _():
                barrier = pltpu.get_barrier_semaphore()
                pl.semaphore_signal(barrier, inc=1, device_id=(x, y, 1 - c), device_id_type=MESH)
                pl.semaphore_wait(barrier, 1)

        def copy(q):
            return pltpu.make_async_remote_copy(
                src_ref=sendbuf.at[q], dst_ref=landed.at[q], send_sem=send_sems.at[q], recv_sem=recv_sems.at[q],
                device_id=(x, y, 1 - c), device_id_type=MESH)

        @pl.when(t == 0)
        def _():
            acc[...] = _dot_tn(l_ref[...], r_ref[...])

        @pl.when(t > 0)
        def _():
            acc[...] += _dot_tn(l_ref[...], r_ref[...])

        @pl.when(t == nt - 1)
        def _():
            for qq in range(cpb):
                q = j * cpb + qq
                kept[q] = acc[pl.ds(pl.multiple_of(qq * 2 * n + c * n, 8), n), :]
                sendbuf[q] = acc[pl.ds(pl.multiple_of(qq * 2 * n + (1 - c) * n, 8), n), :].astype(BF16)
                copy(q).start()

        if ng:
            @pl.when((j == nj - 1) & (t == nt // 2))
            def _():
                for a in range(ng):
                    gather.relay(a)

            @pl.when(last)
            def _():
                for a in range(ng):
                    gather.pass_on(a)
                gather.finish()

        @pl.when(last)
        def _():
            for q in range(N_CHIPS):
                copy(q).wait_send()
                copy(q).wait_recv()
            for q in range(N_CHIPS):
                s = kept[q] + landed[q].astype(F32)
                bf_ref[q] = s.astype(BF16)

                @pl.when(q == 2 * x + y)
                def _(s=s):
                    own_ref[...] = s

    any_spec = pl.BlockSpec(memory_space=pl.ANY)
    by_chip = (N_CHIPS, n, m)
    return pl.pallas_call(
        body, name=name, grid=(nj, nt),
        in_specs=[pl.BlockSpec((tw, 2 * n * cpb), lambda j, t: (t, lhs_block0 + j)),
                  pl.BlockSpec((tw, m), lambda j, t: (t, rhs_col))] + [any_spec] * (ng + len(anchors)),
        out_specs=[pl.BlockSpec(by_chip, lambda j, t: (0, 0, 0)), pl.BlockSpec((n, m), lambda j, t: (0, 0))]
                  + [any_spec] * ng,
        out_shape=[jax.ShapeDtypeStruct(by_chip, BF16), jax.ShapeDtypeStruct((n, m), F32)]
                  + [_gathered_shape(s) for s in gather_rows],
        scratch_shapes=[pltpu.VMEM((2 * n * cpb, m), F32), pltpu.VMEM(by_chip, F32), pltpu.VMEM(by_chip, BF16),
                        pltpu.SemaphoreType.DMA((N_CHIPS,)), pltpu.SemaphoreType.DMA((N_CHIPS,))]
                       + (_TwoLevelGather.scratch(ng) if ng else []),
        compiler_params=pltpu.CompilerParams(dimension_semantics=("arbitrary", "arbitrary"),
                                             vmem_limit_bytes=VMEM_LIMIT_BYTES + 4 * 1024 * 1024,
                                             collective_id=collective_id),
    )(lhs, rhs, *gather_rows, *anchors)


def _received_shape(p):
    return jax.ShapeDtypeStruct((3,) + p.shape[1:], p.dtype)


ROW_POOL_SCALE, ROW_SLN_G, ROW_SLN_B, ROW_SGU_B = 0, 1, 2, 3
ROW_LN2_G, ROW_LN2_B, ROW_LN1_G, ROW_LN1_B, ROW_LOSS = 8, 9, 10, 11, 12
VEC_ROWS = 16


def _mixer_bwd(dr1, proj, win_t, wout, pool_w, pool_scale, sln_g, sln_b, sgu_w, sgu_b_t, stats, seq):
    tokens = dr1.shape[0]
    tt = min(1024, seq)
    tiles_per_seq = seq // tt
    nc = tt // GROUP
    n_halo_blocks = tokens // HALO
    n_tiles = tokens // tt

    def body(dr1_ref, dr1n_ref, proj_ref, projh_ref, win_ref, wout_ref, pw_ref, ps_ref, lg_ref, lb_ref, sw_ref, sb_ref,
             stats_ref, gx_ref, dproj_ref, dmat_ref, dvec_ref, du_scr, dv_scr):
        i = pl.program_id(0)
        tile_in_seq = i % tiles_per_seq

        @pl.when(i == 0)
        def _():
            dmat_ref[...] = jnp.zeros_like(dmat_ref)
            dvec_ref[0:8, :] = jnp.zeros((8, D_MODEL), F32)
            dvec_ref[8:16, :] = stats_ref[...]

        dr1_t = dr1_ref[...]
        dr1b = dr1_t.astype(BF16)
        dmix = _dot_nt(dr1b, wout_ref[...])
        dpo_next = _dot_nt(dr1n_ref[...].astype(BF16), wout_ref[0:POOL_WIDTH, :])
        dpo_next = jnp.where(tile_in_seq == tiles_per_seq - 1, 0.0, dpo_next)
        proj = proj_ref[...]
        xp_prev = jnp.where(tile_in_seq == 0, 0.0, projh_ref[...])
        pos = tile_in_seq * tt + lax.broadcasted_iota(jnp.int32, (tt, 1), 0)
        pos_next = (tile_in_seq + 1) * tt + lax.broadcasted_iota(jnp.int32, (HALO, 1), 0)

        for g, w in enumerate(POOL_WINDOWS):
            sl = slice(g * GROUP, (g + 1) * GROUP)
            inv_cnt = _inv_count(pos, w)
            pwb = pw_ref[g].astype(BF16)
            pooledb = _pool_fwd(proj[:, sl], xp_prev[:, sl], inv_cnt, w).astype(BF16)
            pre = _dot_nn(pooledb, pwb)
            dpo = dmix[:, sl]
            dvec_ref[ROW_POOL_SCALE:ROW_POOL_SCALE + 1, sl] += _rowsum(dpo * pre)
            dsb = (dpo * ps_ref[:, sl]).astype(BF16)
            dmat_ref[g] += _dot_tn(pooledb, dsb)
            dpooled = _dot_nt(dsb, pwb)
            dpooled_next = _dot_nt((dpo_next[:, sl] * ps_ref[:, sl]).astype(BF16), pwb)
            dxp = _pool_bwd(dpooled, dpooled_next, inv_cnt, _inv_count(pos_next, w), w)
            dproj_ref[:, sl] = dxp.astype(BF16)

        zu = proj[:, POOL_WIDTH:POOL_WIDTH + SGU_WIDTH]
        zv = proj[:, POOL_WIDTH + SGU_WIDTH:]
        u = _gelu(zu)
        vhat, rstd_v = _ln_stats(_gelu(zv))
        v_ln = vhat * lg_ref[...] + lb_ref[...]
        dsg = dmix[:, POOL_WIDTH:]
        row = lax.broadcasted_iota(jnp.int32, (GROUP, GROUP), 0)
        col = lax.broadcasted_iota(jnp.int32, (GROUP, GROUP), 1)
        for h in range(N_HEADS):
            ws = _masked_sgu_w(sw_ref, h).astype(BF16)
            vh = _to_head_major(v_ln, h, nc).astype(BF16)
            mixed = _dot_nn(ws, vh) + sb_ref[:, h:h + 1]
            dsg_h = _to_head_major(dsg, h, nc)
            du_h = dsg_h * mixed
            dm_h = dsg_h * _to_head_major(u, h, nc)
            pos_sums = lax.dot_general(jnp.ones((8, nc * GROUP), F32), dm_h, (((1,), (1,)), ((), ())),
                                       precision=lax.Precision.HIGH, preferred_element_type=F32)
            dvec_ref[ROW_SGU_B + h:ROW_SGU_B + h + 1, 0:GROUP] += pos_sums[0:1, :]
            dmb = dm_h.astype(BF16)
            dmat_ref[len(POOL_WINDOWS) + h] += jnp.where(row >= col, _dot_nt(dmb, vh), 0.0)
            dv_h = _dot_tn(ws, dmb)
            for c in range(nc):
                rs = slice(c * GROUP, (c + 1) * GROUP)
                cs = slice(h * GROUP, (h + 1) * GROUP)
                du_scr[rs, cs] = du_h[:, c * GROUP:(c + 1) * GROUP]
                dv_scr[rs, cs] = dv_h[:, c * GROUP:(c + 1) * GROUP]
        dv_ln = dv_scr[...]
        dvec_ref[ROW_SLN_B:ROW_SLN_B + 1, 0:SGU_WIDTH] += _rowsum(dv_ln)
        dvec_ref[ROW_SLN_G:ROW_SLN_G + 1, 0:SGU_WIDTH] += _rowsum(dv_ln * vhat)
        dv = _ln_bwd(dv_ln, vhat, rstd_v, lg_ref[...])
        dproj_ref[:, POOL_WIDTH:POOL_WIDTH + SGU_WIDTH] = (du_scr[...] * _gelu_grad(zu)).astype(BF16)
        dproj_ref[:, POOL_WIDTH + SGU_WIDTH:] = (dv * _gelu_grad(zv)).astype(BF16)
        gx_ref[...] = ALPHA * dr1_t + _dot_nn(dproj_ref[...], win_ref[...])

    def tile(cols):
        return pl.BlockSpec((tt, cols), lambda i: (i, 0))

    def whole(a):
        nd = a.ndim
        return pl.BlockSpec(a.shape, lambda i: (0,) * nd)

    def resident(shape):
        nd = len(shape)
        return pl.BlockSpec(shape, lambda i: (0,) * nd)

    next_halo = pl.BlockSpec((HALO, D_MODEL), lambda i: (jnp.minimum((i + 1) * (tt // HALO), n_halo_blocks - 1), 0))
    prev_halo = pl.BlockSpec((HALO, POOL_WIDTH), lambda i: (jnp.maximum(i * (tt // HALO) - 1, 0), 0))
    consts = [win_t, wout, pool_w, pool_scale, sln_g, sln_b, sgu_w, sgu_b_t, stats]
    small_shapes = [(len(POOL_WINDOWS) + N_HEADS, GROUP, GROUP), (VEC_ROWS, D_MODEL)]
    return pl.pallas_call(
        body, name="mixer_bwd", grid=(n_tiles,),
        in_specs=[tile(D_MODEL), next_halo, tile(IN_COLS), prev_halo] + [whole(a) for a in consts],
        out_specs=[tile(D_MODEL), tile(IN_COLS)] + [resident(s) for s in small_shapes],
        out_shape=[jax.ShapeDtypeStruct((tokens, D_MODEL), F32), jax.ShapeDtypeStruct((tokens, IN_COLS), BF16)]
                  + [jax.ShapeDtypeStruct(s, F32) for s in small_shapes],
        scratch_shapes=[pltpu.VMEM((tt, SGU_WIDTH), F32), pltpu.VMEM((tt, SGU_WIDTH), F32)],
        compiler_params=pltpu.CompilerParams(dimension_semantics=("arbitrary",), vmem_limit_bytes=VMEM_LIMIT_BYTES),
    )(dr1, dr1, proj, proj, *consts)


def _owner_copies(src_ref, land_ref, sems):
    x, y, c = _mesh_position()
    return [pltpu.make_async_remote_copy(
        src_ref=src_ref.at[2 * cx + cy], dst_ref=land_ref.at[j], send_sem=sems[j], recv_sem=sems[3 + j],
        device_id=(cx, cy, c), device_id_type=MESH) for j, (cx, cy) in enumerate(_other_chips(x, y))]


def _send_to_owners_start(chip_partial, name, collective_id):
    land = _received_shape(chip_partial)

    def body(src_ref, land_ref, *rest):
        x, y, c = _mesh_position()
        barrier = pltpu.get_barrier_semaphore()
        for cx, cy in _other_chips(x, y):
            pl.semaphore_signal(barrier, inc=1, device_id=(cx, cy, c), device_id_type=MESH)
        pl.semaphore_wait(barrier, 3)
        for cp in _owner_copies(src_ref, land_ref, rest[:6]):
            cp.start()
        rest[8][...] = jnp.zeros_like(rest[8])

    hbm = pl.BlockSpec(memory_space=pltpu.HBM)
    sem = pl.BlockSpec(memory_space=pltpu.SEMAPHORE)
    outs = pl.pallas_call(
        body, name=name,
        out_shape=[pltpu.SemaphoreType.DMA(())] * 6 + [pltpu.HBM(chip_partial.shape, chip_partial.dtype),
                                                       pltpu.HBM(land.shape, land.dtype),
                                                       jax.ShapeDtypeStruct((8, GROUP), F32)],
        in_specs=[hbm, hbm], out_specs=[sem] * 6 + [hbm, hbm, pl.BlockSpec(memory_space=pltpu.VMEM)],
        input_output_aliases={0: 6, 1: 7},
        compiler_params=pltpu.CompilerParams(has_side_effects=pltpu.SideEffectType.DATAFLOW_SIDE_EFFECTING,
                                             collective_id=collective_id),
    )(pltpu.with_memory_space_constraint(chip_partial, pltpu.HBM),
      pltpu.with_memory_space_constraint(lax.empty(land.shape, land.dtype), pltpu.HBM))
    return outs[:6], outs[6], outs[7], outs[8]


def _send_to_owners_wait(sems, src_thru, land_thru, after, name):
    def body(src_ref, land_ref, *rest):
        for cp in _owner_copies(src_ref, land_ref, rest[:6]):
            cp.wait_send()
            cp.wait_recv()

    hbm = pl.BlockSpec(memory_space=pltpu.HBM)
    sem = pl.BlockSpec(memory_space=pltpu.SEMAPHORE)
    return pl.pallas_call(
        body, name=name,
        out_shape=[pltpu.HBM(src_thru.shape, src_thru.dtype), pltpu.HBM(land_thru.shape, land_thru.dtype)],
        in_specs=[hbm, hbm] + [sem] * 6 + [pl.BlockSpec(memory_space=pl.ANY)], out_specs=[hbm, hbm],
        input_output_aliases={0: 0, 1: 1},
        compiler_params=pltpu.CompilerParams(has_side_effects=pltpu.SideEffectType.DATAFLOW_SIDE_EFFECTING),
    )(src_thru, land_thru, *sems, after)[1]


def _adamw_math(w, g, m, v):
    m = ADAM_B1 * m + (1.0 - ADAM_B1) * g
    v = ADAM_B2 * v + (1.0 - ADAM_B2) * (g * g)
    m_hat = m / (1.0 - ADAM_B1 ** ADAM_STEP)
    v_hat = v / (1.0 - ADAM_B2 ** ADAM_STEP)
    delta = -ADAM_LR * (m_hat / (jnp.sqrt(v_hat) + ADAM_EPS) + ADAM_WD * w)
    return delta, m, v


def _sum_adamw(w, m, v, own, received, name):
    rows, cols = w.shape
    rb = _row_block(rows, 256)

    def body(w_ref, m_ref, v_ref, own_ref, rec_ref, g_ref, d_ref, mo_ref, vo_ref):
        g = own_ref[...]
        for j in range(3):
            g = g + rec_ref[j].astype(F32)
        g_ref[...] = g
        d_ref[...], mo_ref[...], vo_ref[...] = _adamw_math(w_ref[...], g, m_ref[...], v_ref[...])

    spec = pl.BlockSpec((rb, cols), lambda r: (r, 0))
    return pl.pallas_call(
        body, name=name, grid=(rows // rb,),
        in_specs=[spec] * 4 + [pl.BlockSpec((3, rb, cols), lambda r: (0, r, 0))],
        out_specs=[spec] * 4, out_shape=[jax.ShapeDtypeStruct((rows, cols), F32)] * 4,
        compiler_params=pltpu.CompilerParams(dimension_semantics=("arbitrary",)),
    )(w, m, v, own, received)


SMALL_NAMES = ("pool_w", "sgu_w", "pool_scale", "sgu_ln_g", "sgu_ln_b", "sgu_b", "ln1_g", "ln1_b", "ln2_g", "ln2_b")
_SMALL_VEC_ROWS = {"pool_scale": (ROW_POOL_SCALE, POOL_WIDTH), "sgu_ln_g": (ROW_SLN_G, SGU_WIDTH),
                   "sgu_ln_b": (ROW_SLN_B, SGU_WIDTH), "ln1_g": (ROW_LN1_G, D_MODEL), "ln1_b": (ROW_LN1_B, D_MODEL),
                   "ln2_g": (ROW_LN2_G, D_MODEL), "ln2_b": (ROW_LN2_B, D_MODEL)}
_SMALL_MAT_FIRST = {"pool_w": 0, "sgu_w": len(POOL_WINDOWS)}


def _small_sum_adamw(mats_all, vecs_all, w, m, v):
    n = len(SMALL_NAMES)

    def body(mats_ref, vecs_ref, *refs):
        w_refs, m_refs, v_refs = refs[:n], refs[n:2 * n], refs[2 * n:3 * n]
        loss_ref = refs[3 * n]
        g_refs, d_refs, mo_refs, vo_refs = (refs[3 * n + 1 + k * n:3 * n + 1 + (k + 1) * n] for k in range(4))
        vec_scr = refs[7 * n + 1]

        def update(k, idx, g):
            d, mo, vo = _adamw_math(w_refs[k][idx], g, m_refs[k][idx], v_refs[k][idx])
            g_refs[k][idx], d_refs[k][idx], mo_refs[k][idx], vo_refs[k][idx] = g, d, mo, vo

        total = vecs_ref[0]
        for dev in range(1, N_DEV):
            total = total + vecs_ref[dev]
        vec_scr[...] = total
        for k, name in enumerate(SMALL_NAMES):
            if name in _SMALL_MAT_FIRST:
                for b in range(4):
                    g = mats_ref[0, _SMALL_MAT_FIRST[name] + b]
                    for dev in range(1, N_DEV):
                        g = g + mats_ref[dev, _SMALL_MAT_FIRST[name] + b]
                    update(k, (0, b), g)
            elif name == "sgu_b":
                update(k, (0,), vec_scr[ROW_SGU_B:ROW_SGU_B + N_HEADS, 0:GROUP])
            else:
                row, width = _SMALL_VEC_ROWS[name]
                update(k, (slice(None), slice(None)), vec_scr[row:row + 1, 0:width])
        loss = jnp.sum(vec_scr[ROW_LOSS:ROW_LOSS + 1, :], axis=1, keepdims=True) * (0.5 / D_MODEL)
        loss_ref[...] = jnp.broadcast_to(loss, loss_ref.shape)

    vmem = pl.BlockSpec(memory_space=pltpu.VMEM)
    shapes = [jax.ShapeDtypeStruct(w[k].shape, F32) for k in SMALL_NAMES]
    outs = pl.pallas_call(
        body, name="small_sum_adamw",
        out_shape=[jax.ShapeDtypeStruct((8, GROUP), F32)] + shapes * 4,
        in_specs=[vmem] * (2 + 3 * n), out_specs=[vmem] * (1 + 4 * n),
        scratch_shapes=[pltpu.VMEM((VEC_ROWS, D_MODEL), F32)],
    )(mats_all, vecs_all, *[w[k] for k in SMALL_NAMES], *[m[k] for k in SMALL_NAMES], *[v[k] for k in SMALL_NAMES])
    return outs[0], outs[1:1 + n], outs[1 + n:1 + 2 * n], outs[1 + 2 * n:1 + 3 * n], outs[1 + 3 * n:]


def kernel(x, w_in, pool_w, pool_scale, sgu_ln_g, sgu_ln_b, sgu_w, sgu_b, w_out, ln1_g, ln1_b, w_gate_up, w_down, ln2_g, ln2_b, loss_target, m_w_in, m_pool_w, m_pool_scale, m_sgu_ln_g, m_sgu_ln_b, m_sgu_w, m_sgu_b, m_w_out, m_ln1_g, m_ln1_b, m_w_gate_up, m_w_down, m_ln2_g, m_ln2_b, v_w_in, v_pool_w, v_pool_scale, v_sgu_ln_g, v_sgu_ln_b, v_sgu_w, v_sgu_b, v_w_out, v_ln1_g, v_ln1_b, v_w_gate_up, v_w_down, v_ln2_g, v_ln2_b):
    bl, seq, _ = x.shape
    tokens = bl * seq
    x2d = x.reshape(tokens, D_MODEL)
    tgt2d = loss_target.reshape(tokens, D_MODEL)

    win_t, wout = _all_gather_rows([w_in[0].T, w_out[0]], "weight_all_gather")

    pool_w3, sgu_w3 = pool_w[0], sgu_w[0]
    sgu_b_t = sgu_b[0].T
    proj, xhat1, rstd1, mix_bf, x_bf, wgu_t, wdown = _mixer_fwd(
        x2d, win_t, wout, pool_w3, pool_scale, sgu_ln_g, sgu_ln_b, sgu_w3, sgu_b_t, ln1_g, ln1_b,
        [w_gate_up[0].T, w_down[0]], seq)
    dr1, acts_bf, grads_bf, stats = _ffn_fwd_bwd(xhat1, rstd1, tgt2d, wgu_t, wdown, ln1_g, ln1_b, ln2_g, ln2_b)

    bf_gu, own_gu = _wgrad_exchange(grads_bf, acts_bf, 1, "wgrad_gate_up", collective_id=1,
                                    lhs_cols=(0, 2 * D_FF), rhs_block=0)
    sent_gu = _send_to_owners_start(bf_gu, "grad_scatter_w_gate_up_start", collective_id=4)
    bf_dn, own_dn = _wgrad_exchange(grads_bf, acts_bf, 2, "wgrad_down", after=sent_gu[3], collective_id=2,
                                    lhs_cols=(2 * D_FF, D_FF), rhs_block=1)
    sent_dn = _send_to_owners_start(bf_dn, "grad_scatter_w_down_start", collective_id=5)
    bf_out, own_out = _wgrad_exchange(mix_bf, acts_bf, N_CHIPS, "wgrad_out", after=sent_dn[3], collective_id=3,
                                      rhs_block=2)
    sent_out = _send_to_owners_start(bf_out, "grad_scatter_w_out_start", collective_id=6)
    gx, dproj_bf, d_mats, d_vecs = _mixer_bwd(
        dr1, proj, win_t, wout, pool_w3, pool_scale, sgu_ln_g, sgu_ln_b, sgu_w3, sgu_b_t, stats, seq)
    bf_in, own_in, mats_all, vecs_all = _wgrad_exchange(
        dproj_bf, x_bf, N_CHIPS, "wgrad_in", gather_rows=[d_mats.reshape(-1, GROUP), d_vecs], after=sent_out[3],
        collective_id=10)
    sent_in = _send_to_owners_start(bf_in, "grad_scatter_w_in_start", collective_id=7)

    grads, deltas, new_m, new_v = {}, {}, {}, {}
    after = sent_in[3]
    for nm, w, m, v, own, sent, transposed in (("w_gate_up", w_gate_up, m_w_gate_up, v_w_gate_up, own_gu, sent_gu, True),
                                               ("w_down", w_down, m_w_down, v_w_down, own_dn, sent_dn, False),
                                               ("w_out", w_out, m_w_out, v_w_out, own_out, sent_out, False),
                                               ("w_in", w_in, m_w_in, v_w_in, own_in, sent_in, True)):
        rows = (lambda a: a[0].T) if transposed else (lambda a: a[0])
        back = (lambda a: a.T[None]) if transposed else (lambda a: a[None])
        rec = _send_to_owners_wait(*sent[:3], after, "grad_scatter_" + nm + "_wait")
        g, d, mo, vo = _sum_adamw(rows(w), rows(m), rows(v), own, rec, "adamw_" + nm)
        after = vo
        grads[nm], deltas[nm], new_m[nm], new_v[nm] = back(g), back(d), back(mo), back(vo)

    small_w = {"pool_w": pool_w, "pool_scale": pool_scale, "sgu_ln_g": sgu_ln_g, "sgu_ln_b": sgu_ln_b, "sgu_w": sgu_w,
               "sgu_b": sgu_b, "ln1_g": ln1_g, "ln1_b": ln1_b, "ln2_g": ln2_g, "ln2_b": ln2_b}
    small_m = {"pool_w": m_pool_w, "pool_scale": m_pool_scale, "sgu_ln_g": m_sgu_ln_g, "sgu_ln_b": m_sgu_ln_b,
               "sgu_w": m_sgu_w, "sgu_b": m_sgu_b, "ln1_g": m_ln1_g, "ln1_b": m_ln1_b, "ln2_g": m_ln2_g, "ln2_b": m_ln2_b}
    small_v = {"pool_w": v_pool_w, "pool_scale": v_pool_scale, "sgu_ln_g": v_sgu_ln_g, "sgu_ln_b": v_sgu_ln_b,
               "sgu_w": v_sgu_w, "sgu_b": v_sgu_b, "ln1_g": v_ln1_g, "ln1_b": v_ln1_b, "ln2_g": v_ln2_g, "ln2_b": v_ln2_b}
    loss_blk, g_small, d_small, m_small, v_small = _small_sum_adamw(
        mats_all.reshape(N_DEV, 2 * N_HEADS, GROUP, GROUP), vecs_all.reshape(N_DEV, VEC_ROWS, D_MODEL),
        small_w, small_m, small_v)
    for vals, dst in ((g_small, grads), (d_small, deltas), (m_small, new_m), (v_small, new_v)):
        dst.update(zip(SMALL_NAMES, vals))

    order = ["w_in", "pool_w", "pool_scale", "sgu_ln_g", "sgu_ln_b", "sgu_w", "sgu_b", "w_out", "ln1_g", "ln1_b",
             "w_gate_up", "w_down", "ln2_g", "ln2_b"]
    return (loss_blk[0, 0], gx.reshape(bl, seq, D_MODEL), *[grads[k] for k in order], *[deltas[k] for k in order],
            *[new_m[k] for k in order], *[new_v[k] for k in order])
```

```python
import jax
import jax.numpy as jnp
from jax import lax
from jax.experimental import pallas as pl
from jax.experimental.pallas import tpu as pltpu

F32 = jnp.float32
BF16 = jnp.bfloat16
MESH = pl.DeviceIdType.MESH

D_MODEL = 1024
POOL_WIDTH = 512
SGU_WIDTH = 512
POOL_WINDOWS = (2, 4, 8, 16)
GROUP = 128
N_HEADS = 4
IN_COLS = POOL_WIDTH + 2 * SGU_WIDTH
D_FF = 2816
LN_EPS = 1e-5
ALPHA = float(2.0 ** 0.25)
HALO = 16
N_DEV = 8
N_CHIPS = 4

ADAM_LR = 0.001
ADAM_B1 = 0.9
ADAM_B2 = 0.999
ADAM_EPS = 1e-08
ADAM_WD = 0.01
ADAM_STEP = 10

VMEM_LIMIT_BYTES = 56 * 1024 * 1024

_SQRT_HALF = 0.7071067811865476
_INV_SQRT_2PI = 0.3989422804014327


def _dot_nn(a, b):
    return lax.dot_general(a, b, (((1,), (0,)), ((), ())), preferred_element_type=F32)


def _dot_nt(a, b):
    return lax.dot_general(a, b, (((1,), (1,)), ((), ())), preferred_element_type=F32)


def _dot_tn(a, b):
    return lax.dot_general(a, b, (((0,), (0,)), ((), ())), preferred_element_type=F32)


def _gelu(x):
    return 0.5 * x * (1.0 + lax.erf(x * _SQRT_HALF))


def _gelu_grad(x):
    return 0.5 * (1.0 + lax.erf(x * _SQRT_HALF)) + x * jnp.exp(-0.5 * x * x) * _INV_SQRT_2PI


def _ln_stats(r):
    mu = jnp.mean(r, axis=-1, keepdims=True)
    d = r - mu
    var = jnp.mean(d * d, axis=-1, keepdims=True)
    rstd = lax.rsqrt(var + LN_EPS)
    return d * rstd, rstd


def _ln_bwd(dout, xhat, rstd, g):
    dxh = dout * g
    m1 = jnp.mean(dxh, axis=-1, keepdims=True)
    m2 = jnp.mean(dxh * xhat, axis=-1, keepdims=True)
    return rstd * (dxh - m1 - xhat * m2)


def _rowsum(a):
    return jnp.sum(a, axis=0, keepdims=True)


def _pool_fwd(xp, xp_prev, inv_cnt, w):
    s = jnp.concatenate([xp_prev, xp], axis=0)
    k = 1
    while k < w:
        s = s + pltpu.roll(s, k, 0)
        k *= 2
    return s[HALO:, :] * inv_cnt - xp


def _pool_bwd(dpooled, dpooled_next, inv_cnt, inv_cnt_next, w):
    n = dpooled.shape[0] + HALO
    s = jnp.concatenate([dpooled * inv_cnt, dpooled_next * inv_cnt_next], axis=0)
    k = 1
    while k < w:
        s = s + pltpu.roll(s, n - k, 0)
        k *= 2
    return s[: dpooled.shape[0], :] - dpooled


def _inv_count(pos, w):
    return 1.0 / jnp.minimum(pos + 1, w).astype(F32)


def _to_head_major(a, h, nc, first_col=0):
    lo = first_col + h * GROUP
    return jnp.concatenate([a[c * GROUP:(c + 1) * GROUP, lo:lo + GROUP] for c in range(nc)], axis=1)


def _masked_sgu_w(sw_ref, h):
    row = lax.broadcasted_iota(jnp.int32, (GROUP, GROUP), 0)
    col = lax.broadcasted_iota(jnp.int32, (GROUP, GROUP), 1)
    return jnp.where(row >= col, sw_ref[h], 0.0)


def _row_block(rows, limit):
    return max(b for b in range(16, min(rows, limit) + 1, 16) if rows % b == 0)


def _mesh_position():
    return lax.axis_index("x"), lax.axis_index("y"), lax.axis_index("c")


def _other_chips(x, y):
    return [(1 - x, y), (x, 1 - y), (1 - x, 1 - y)]


class _TwoLevelGather:
    def __init__(self, ins, outs, send_sems, recv_sems, local_sems):
        self.ins, self.outs = ins, outs
        self.send_sems, self.recv_sems, self.local_sems = send_sems, recv_sems, local_sems
        self.na = len(ins)
        x, y, c = _mesh_position()
        self.c = c
        self.me, self.sibling = (x, y, c), (x, y, 1 - c)
        self.chips = _other_chips(x, y)
        self.relay_from = (x + (1 - c) * (1 - 2 * x), y + c * (1 - 2 * y))
        self.relay_to = (x + c * (1 - 2 * x), y + (1 - c) * (1 - 2 * y))

    def handshake(self):
        peers = [self.sibling] + [(*chip, self.c) for chip in self.chips[:2]]
        barrier = pltpu.get_barrier_semaphore()
        for peer in peers:
            pl.semaphore_signal(barrier, inc=1, device_id=peer, device_id_type=MESH)
        pl.semaphore_wait(barrier, len(peers))

    def _rows(self, a, px, py, pc):
        n = self.ins[a].shape[0]
        return self.outs[a].at[pl.ds((4 * px + 2 * py + pc) * n, n), :]

    def _copy(self, a, k, block, to, src=None):
        return pltpu.make_async_remote_copy(
            src_ref=self._rows(a, *block) if src is None else src, dst_ref=self._rows(a, *block),
            send_sem=self.send_sems.at[a * 7 + k], recv_sem=self.recv_sems.at[a * 7 + k],
            device_id=to, device_id_type=MESH)

    def _mine(self, a):
        return pltpu.make_async_copy(self.ins[a], self._rows(a, *self.me), self.local_sems.at[a])

    def start(self, arrays=None):
        arrays = range(self.na) if arrays is None else arrays
        for a in arrays:
            self._mine(a).start()
        for a in arrays:
            self._copy(a, 0, self.me, self.sibling, src=self.ins[a]).start()
            for j, chip in enumerate(self.chips[:2]):
                self._copy(a, 1 + j, self.me, (*chip, self.c), src=self.ins[a]).start()

    def relay(self, a):
        c, block = self.c, (*self.relay_from, self.c)
        self._copy(a, 1 + c, block, self.me).wait_recv()
        self._copy(a, 3, block, (*self.relay_to, c)).start()
        self._copy(a, 4 + c, block, self.sibling).start()

    def pass_on(self, a):
        c = self.c
        self._copy(a, 2 - c, (*self.relay_to, c), self.me).wait_recv()
        self._copy(a, 5 - c, (*self.relay_to, c), self.sibling).start()
        self._copy(a, 3, (*self.chips[2], c), self.me).wait_recv()
        self._copy(a, 6, (*self.chips[2], c), self.sibling).start()

    def finish(self, arrays=None):
        arrays = range(self.na) if arrays is None else arrays
        for a in arrays:
            self._copy(a, 0, self.sibling, self.me).wait_recv()
            for j, chip in enumerate(self.chips):
                self._copy(a, 4 + j, (*chip, 1 - self.c), self.me).wait_recv()
        for a in arrays:
            for k in range(7):
                self._copy(a, k, self.me, self.sibling, src=self.ins[a]).wait_send()
            self._mine(a).wait()

    @staticmethod
    def scratch(na):
        return [pltpu.SemaphoreType.DMA((7 * na,)), pltpu.SemaphoreType.DMA((7 * na,)), pltpu.SemaphoreType.DMA((na,))]


def _gathered_shape(s):
    return jax.ShapeDtypeStruct((N_DEV * s.shape[0], s.shape[1]), s.dtype)


def _gathered_bf16(s):
    return jax.ShapeDtypeStruct((N_DEV * s.shape[0], s.shape[1]), BF16)


def _mixer_fwd(x2d, own_shards, pool_w, pool_scale, sln_g, sln_b, sgu_w, sgu_b_t, ln1_g, ln1_b, later_shards, seq):
    tokens = x2d.shape[0]
    tt = min(512, seq)
    tiles_per_seq = seq // tt
    nc = tt // GROUP
    n_tiles = tokens // tt
    n_later = len(later_shards)
    shards = list(own_shards) + list(later_shards)
    ns = len(shards)

    def body(x_ref, xh_ref, pw_ref, ps_ref, lg_ref, lb_ref, sw_ref, sb_ref, g1_ref, b1_ref, *rest):
        shard_refs, rest = rest[:ns], rest[ns:]
        proj_ref, xhat_ref, rstd_ref, mix_ref, xbf_ref = rest[:5]
        win_out, wout_out = rest[5:7]
        later_refs, rest = rest[7:7 + n_later], rest[7 + n_later:]
        mix_scr, win_ref, wout_ref = rest[:3]
        shard_bf_refs, (send_sems, recv_sems, local_sems, copy_sems) = rest[3:3 + ns], rest[3 + ns:]
        i = pl.program_id(0)
        gather = _TwoLevelGather(shard_bf_refs, [win_ref, wout_ref, *later_refs], send_sems, recv_sems, local_sems)
        exports = [pltpu.make_async_copy(win_ref, win_out, copy_sems.at[0]),
                   pltpu.make_async_copy(wout_ref, wout_out, copy_sems.at[1])]

        @pl.when(i == 0)
        def _():
            gather.handshake()
            for a in range(ns):
                shard_bf_refs[a][...] = shard_refs[a][...].astype(BF16)
            gather.start([0, 1])
            gather.relay(0)
            gather.relay(1)
            gather.start(range(2, ns))
            gather.pass_on(0)
            gather.pass_on(1)
            gather.finish([0, 1])
            for cp in exports:
                cp.start()

        tile_in_seq = i % tiles_per_seq
        x = x_ref[...]
        xb = x.astype(BF16)
        xbf_ref[...] = xb
        proj = _dot_nt(xb, win_ref[...])
        proj_ref[...] = proj
        xp_prev = _dot_nt(xh_ref[...].astype(BF16), win_ref[0:POOL_WIDTH, :])
        xp_prev = jnp.where(tile_in_seq == 0, 0.0, xp_prev)
        pos = tile_in_seq * tt + lax.broadcasted_iota(jnp.int32, (tt, 1), 0)
        for g, w in enumerate(POOL_WINDOWS):
            sl = slice(g * GROUP, (g + 1) * GROUP)
            pooled = _pool_fwd(proj[:, sl], xp_prev[:, sl], _inv_count(pos, w), w)
            pre = _dot_nn(pooled.astype(BF16), pw_ref[g].astype(BF16))
            mix_scr[:, sl] = pre * ps_ref[:, sl]
        u = _gelu(proj[:, POOL_WIDTH:POOL_WIDTH + SGU_WIDTH])
        v = _gelu(proj[:, POOL_WIDTH + SGU_WIDTH:])
        vhat, _ = _ln_stats(v)
        v_ln = vhat * lg_ref[...] + lb_ref[...]
        for h in range(N_HEADS):
            ws = _masked_sgu_w(sw_ref, h).astype(BF16)
            mixed = _dot_nn(ws, _to_head_major(v_ln, h, nc).astype(BF16)) + sb_ref[:, h:h + 1]
            for c in range(nc):
                rs = slice(c * GROUP, (c + 1) * GROUP)
                mix_scr[rs, POOL_WIDTH + h * GROUP:POOL_WIDTH + (h + 1) * GROUP] = (
                    u[rs, h * GROUP:(h + 1) * GROUP] * mixed[:, c * GROUP:(c + 1) * GROUP])
        mixb = mix_scr[...].astype(BF16)
        mix_ref[...] = mixb
        r1 = ALPHA * x + _dot_nn(mixb, wout_ref[...])
        xhat, rstd = _ln_stats(r1)
        xhat_ref[...] = xhat
        rstd_ref[...] = rstd

        for a in range(2, ns):
            relay_tile = min(n_tiles // 2 + a - 2, n_tiles - 1)

            @pl.when(i == relay_tile)
            def _(a=a):
                gather.relay(a)

            @pl.when(i == max(n_tiles - ns + a, relay_tile))
            def _(a=a):
                gather.pass_on(a)

        @pl.when(i == n_tiles - 1)
        def _():
            gather.finish(range(2, ns))
            for cp in exports:
                cp.wait()

    def tile(cols):
        return pl.BlockSpec((tt, cols), lambda i: (i, 0))

    def whole(a):
        nd = a.ndim
        return pl.BlockSpec(a.shape, lambda i: (0,) * nd)

    any_spec = pl.BlockSpec(memory_space=pl.ANY)
    halo = pl.BlockSpec((HALO, D_MODEL), lambda i: (jnp.maximum(i * (tt // HALO) - 1, 0), 0))
    consts = [pool_w, pool_scale, sln_g, sln_b, sgu_w, sgu_b_t, ln1_g, ln1_b]
    gathered = [_gathered_bf16(s) for s in shards]
    return pl.pallas_call(
        body, name="mixer_fwd", grid=(n_tiles,),
        in_specs=[tile(D_MODEL), halo] + [whole(a) for a in consts] + [whole(s) for s in shards],
        out_specs=[tile(IN_COLS), tile(D_MODEL), tile(1), tile(D_MODEL), tile(D_MODEL)] + [any_spec] * ns,
        out_shape=[jax.ShapeDtypeStruct((tokens, IN_COLS), F32), jax.ShapeDtypeStruct((tokens, D_MODEL), F32),
                   jax.ShapeDtypeStruct((tokens, 1), F32), jax.ShapeDtypeStruct((tokens, D_MODEL), BF16),
                   jax.ShapeDtypeStruct((tokens, D_MODEL), BF16)] + gathered,
        scratch_shapes=[pltpu.VMEM((tt, D_MODEL), F32)] + [pltpu.VMEM(g.shape, BF16) for g in gathered[:2]]
                       + [pltpu.VMEM(s.shape, BF16) for s in shards] + _TwoLevelGather.scratch(ns)
                       + [pltpu.SemaphoreType.DMA((2,))],
        compiler_params=pltpu.CompilerParams(dimension_semantics=("arbitrary",), vmem_limit_bytes=VMEM_LIMIT_BYTES,
                                             collective_id=9),
    )(x2d, x2d, *consts, *shards)


def _ffn_fwd_bwd(xhat1, rstd1, target, wgu_t, wdown, ln1_g, ln1_b, ln2_g, ln2_b):
    tokens = xhat1.shape[0]
    tt = min(256, tokens)

    def body(xhat_ref, rstd_ref, tgt_ref, wgu_hbm, wd_hbm, g1_ref, b1_ref, g2_ref, b2_ref,
             dr1_ref, acts_ref, grads_ref, stats_ref, wgu_ref, wd_ref, gu_scr, sems):
        i = pl.program_id(0)

        @pl.when(i == 0)
        def _():
            loads = [pltpu.make_async_copy(wgu_hbm, wgu_ref, sems.at[0]),
                     pltpu.make_async_copy(wd_hbm, wd_ref, sems.at[1])]
            for cp in loads:
                cp.start()
            stats_ref[...] = jnp.zeros_like(stats_ref)
            for cp in loads:
                cp.wait()

        xhat1_t = xhat_ref[...]
        h = xhat1_t * g1_ref[...] + b1_ref[...]
        hb = h.astype(BF16)
        acts_ref[:, 0:D_MODEL] = hb
        gate = _dot_nt(hb, wgu_ref[0:D_FF, :])
        up = _dot_nt(hb, wgu_ref[D_FF:, :])
        gu_scr[:, 0:D_FF] = gate
        gu_scr[:, D_FF:] = up
        ab = (gate * jax.nn.sigmoid(gate) * up).astype(BF16)
        grads_ref[:, 2 * D_FF:] = ab
        xhat2, rstd2 = _ln_stats(ALPHA * h + _dot_nn(ab, wd_ref[...]))
        err = xhat2 * g2_ref[...] + b2_ref[...] - tgt_ref[...]
        dy = err * (1.0 / D_MODEL)
        stats_ref[0:1, :] += _rowsum(dy * xhat2)
        stats_ref[1:2, :] += _rowsum(dy)
        stats_ref[4:5, :] += _rowsum(err * err)
        dr2 = _ln_bwd(dy, xhat2, rstd2, g2_ref[...])
        dr2b = dr2.astype(BF16)
        acts_ref[:, D_MODEL:2 * D_MODEL] = dr2b
        da = _dot_nt(dr2b, wd_ref[...])
        gate = gu_scr[:, 0:D_FF]
        up = gu_scr[:, D_FF:]
        sg = jax.nn.sigmoid(gate)
        dgate = (da * up * (sg * (1.0 + gate * (1.0 - sg)))).astype(BF16)
        dup = (da * (gate * sg)).astype(BF16)
        grads_ref[:, 0:D_FF] = dgate
        grads_ref[:, D_FF:2 * D_FF] = dup
        dh = ALPHA * dr2 + _dot_nn(dgate, wgu_ref[0:D_FF, :]) + _dot_nn(dup, wgu_ref[D_FF:, :])
        stats_ref[2:3, :] += _rowsum(dh * xhat1_t)
        stats_ref[3:4, :] += _rowsum(dh)
        dr1 = _ln_bwd(dh, xhat1_t, rstd_ref[...], g1_ref[...])
        dr1_ref[...] = dr1
        acts_ref[:, 2 * D_MODEL:] = dr1.astype(BF16)

    def tile(cols):
        return pl.BlockSpec((tt, cols), lambda i: (i, 0))

    def whole(a):
        nd = a.ndim
        return pl.BlockSpec(a.shape, lambda i: (0,) * nd)

    any_spec = pl.BlockSpec(memory_space=pl.ANY)
    vecs = [ln1_g, ln1_b, ln2_g, ln2_b]
    return pl.pallas_call(
        body, name="ffn_fwd_bwd", grid=(tokens // tt,),
        in_specs=[tile(D_MODEL), tile(1), tile(D_MODEL), any_spec, any_spec] + [whole(a) for a in vecs],
        out_specs=[tile(D_MODEL), tile(3 * D_MODEL), tile(3 * D_FF), pl.BlockSpec((8, D_MODEL), lambda i: (0, 0))],
        out_shape=[jax.ShapeDtypeStruct((tokens, D_MODEL), F32), jax.ShapeDtypeStruct((tokens, 3 * D_MODEL), BF16),
                   jax.ShapeDtypeStruct((tokens, 3 * D_FF), BF16), jax.ShapeDtypeStruct((8, D_MODEL), F32)],
        scratch_shapes=[pltpu.VMEM(wgu_t.shape, BF16), pltpu.VMEM(wdown.shape, BF16),
                        pltpu.VMEM((tt, 2 * D_FF), F32), pltpu.SemaphoreType.DMA((2,))],
        compiler_params=pltpu.CompilerParams(dimension_semantics=("arbitrary",), vmem_limit_bytes=VMEM_LIMIT_BYTES),
    )(xhat1, rstd1, target, wgu_t, wdown, *vecs)


def _wgrad_exchange(lhs, rhs, chips_per_block, name, gather_rows=(), after=None, collective_id=None,
                    lhs_cols=None, rhs_block=None, token_tile=2048):
    tokens = lhs.shape[0]
    first_col, n_all = (0, lhs.shape[1]) if lhs_cols is None else lhs_cols
    m = rhs.shape[1] if rhs_block is None else D_MODEL
    n = n_all // N_DEV
    tw = min(token_tile, tokens)
    nt = tokens // tw
    cpb = chips_per_block
    nj = N_CHIPS // cpb
    lhs_block0 = first_col // (2 * n * cpb)
    assert lhs_block0 * 2 * n * cpb == first_col
    rhs_col = 0 if rhs_block is None else rhs_block
    ng = len(gather_rows)
    anchors = [] if after is None else [after]

    def body(l_ref, r_ref, *rest):
        small_refs, rest = rest[:ng], rest[ng + len(anchors):]
        bf_ref, own_ref = rest[:2]
        gathered_refs, rest = rest[2:2 + ng], rest[2 + ng:]
        acc, kept, landed, send_sems, recv_sems = rest[:5]
        sendbuf = bf_ref
        j, t = pl.program_id(0), pl.program_id(1)
        first, last = (j == 0) & (t == 0), (j == nj - 1) & (t == nt - 1)
        x, y, c = _mesh_position()
        if ng:
            gather = _TwoLevelGather(small_refs, gathered_refs, *rest[5:8])

            @pl.when(first)
            def _():
                gather.handshake()
                gather.start()
        else:
            @pl.when(first)
            def _():
                barrier = pltpu.get_barrier_semaphore()
                pl.semaphore_signal(barrier, inc=1, device_id=(x, y, 1 - c), device_id_type=MESH)
                pl.semaphore_wait(barrier, 1)

        def copy(q):
            return pltpu.make_async_remote_copy(
                src_ref=sendbuf.at[q], dst_ref=landed.at[q], send_sem=send_sems.at[q], recv_sem=recv_sems.at[q],
                device_id=(x, y, 1 - c), device_id_type=MESH)

        @pl.when(t == 0)
        def _():
            acc[...] = _dot_tn(l_ref[...], r_ref[...])

        @pl.when(t > 0)
        def _():
            acc[...] += _dot_tn(l_ref[...], r_ref[...])

        @pl.when(t == nt - 1)
        def _():
            for qq in range(cpb):
                q = j * cpb + qq
                kept[q] = acc[pl.ds(pl.multiple_of(qq * 2 * n + c * n, 8), n), :]
                sendbuf[q] = acc[pl.ds(pl.multiple_of(qq * 2 * n + (1 - c) * n, 8), n), :].astype(BF16)
                copy(q).start()

        if ng:
            @pl.when((j == nj - 1) & (t == nt // 2))
            def _():
                for a in range(ng):
                    gather.relay(a)

            @pl.when(last)
            def _():
                for a in range(ng):
                    gather.pass_on(a)
                gather.finish()

        @pl.when(last)
        def _():
            for q in range(N_CHIPS):
                copy(q).wait_send()
                copy(q).wait_recv()
            for q in range(N_CHIPS):
                s = kept[q] + landed[q].astype(F32)
                bf_ref[q] = s.astype(BF16)

                @pl.when(q == 2 * x + y)
                def _(s=s):
                    own_ref[...] = s

    any_spec = pl.BlockSpec(memory_space=pl.ANY)
    by_chip = (N_CHIPS, n, m)
    return pl.pallas_call(
        body, name=name, grid=(nj, nt),
        in_specs=[pl.BlockSpec((tw, 2 * n * cpb), lambda j, t: (t, lhs_block0 + j)),
                  pl.BlockSpec((tw, m), lambda j, t: (t, rhs_col))] + [any_spec] * (ng + len(anchors)),
        out_specs=[pl.BlockSpec(by_chip, lambda j, t: (0, 0, 0)), pl.BlockSpec((n, m), lambda j, t: (0, 0))]
                  + [any_spec] * ng,
        out_shape=[jax.ShapeDtypeStruct(by_chip, BF16), jax.ShapeDtypeStruct((n, m), F32)]
                  + [_gathered_shape(s) for s in gather_rows],
        scratch_shapes=[pltpu.VMEM((2 * n * cpb, m), F32), pltpu.VMEM(by_chip, F32), pltpu.VMEM(by_chip, BF16),
                        pltpu.SemaphoreType.DMA((N_CHIPS,)), pltpu.SemaphoreType.DMA((N_CHIPS,))]
                       + (_TwoLevelGather.scratch(ng) if ng else []),
        compiler_params=pltpu.CompilerParams(dimension_semantics=("arbitrary", "arbitrary"),
                                             vmem_limit_bytes=VMEM_LIMIT_BYTES + 4 * 1024 * 1024,
                                             collective_id=collective_id),
    )(lhs, rhs, *gather_rows, *anchors)


def _received_shape(p):
    return jax.ShapeDtypeStruct((3,) + p.shape[1:], p.dtype)


ROW_POOL_SCALE, ROW_SLN_G, ROW_SLN_B, ROW_SGU_B = 0, 1, 2, 3
ROW_LN2_G, ROW_LN2_B, ROW_LN1_G, ROW_LN1_B, ROW_LOSS = 8, 9, 10, 11, 12
VEC_ROWS = 16


def _mixer_bwd(dr1, proj, win_t, wout, pool_w, pool_scale, sln_g, sln_b, sgu_w, sgu_b_t, stats, seq):
    tokens = dr1.shape[0]
    tt = min(1024, seq)
    tiles_per_seq = seq // tt
    nc = tt // GROUP
    n_halo_blocks = tokens // HALO
    n_tiles = tokens // tt

    def body(dr1_ref, dr1n_ref, proj_ref, projh_ref, win_ref, wout_ref, pw_ref, ps_ref, lg_ref, lb_ref, sw_ref, sb_ref,
             stats_ref, gx_ref, dproj_ref, dmat_ref, dvec_ref, du_scr, dv_scr, dmix_scr, u_scr, vln_scr, vhat_scr):
        i = pl.program_id(0)
        tile_in_seq = i % tiles_per_seq

        @pl.when(i == 0)
        def _():
            dmat_ref[...] = jnp.zeros_like(dmat_ref)
            dvec_ref[0:8, :] = jnp.zeros((8, D_MODEL), F32)
            dvec_ref[8:16, :] = stats_ref[...]

        dmix_scr[...] = _dot_nt(dr1_ref[...].astype(BF16), wout_ref[...])
        dpo_next = _dot_nt(dr1n_ref[...].astype(BF16), wout_ref[0:POOL_WIDTH, :])
        dpo_next = jnp.where(tile_in_seq == tiles_per_seq - 1, 0.0, dpo_next)
        xp_prev = jnp.where(tile_in_seq == 0, 0.0, projh_ref[...])
        pos = tile_in_seq * tt + lax.broadcasted_iota(jnp.int32, (tt, 1), 0)
        pos_next = (tile_in_seq + 1) * tt + lax.broadcasted_iota(jnp.int32, (HALO, 1), 0)

        for g, w in enumerate(POOL_WINDOWS):
            sl = slice(g * GROUP, (g + 1) * GROUP)
            inv_cnt = _inv_count(pos, w)
            pwb = pw_ref[g].astype(BF16)
            pooledb = _pool_fwd(proj_ref[:, sl], xp_prev[:, sl], inv_cnt, w).astype(BF16)
            pre = _dot_nn(pooledb, pwb)
            dpo = dmix_scr[:, sl]
            dvec_ref[ROW_POOL_SCALE:ROW_POOL_SCALE + 1, sl] += _rowsum(dpo * pre)
            dsb = (dpo * ps_ref[:, sl]).astype(BF16)
            dmat_ref[g] += _dot_tn(pooledb, dsb)
            dpooled = _dot_nt(dsb, pwb)
            dpooled_next = _dot_nt((dpo_next[:, sl] * ps_ref[:, sl]).astype(BF16), pwb)
            dxp = _pool_bwd(dpooled, dpooled_next, inv_cnt, _inv_count(pos_next, w), w)
            dproj_ref[:, sl] = dxp.astype(BF16)

        u_scr[...] = _gelu(proj_ref[:, POOL_WIDTH:POOL_WIDTH + SGU_WIDTH])
        vhat, rstd_v = _ln_stats(_gelu(proj_ref[:, POOL_WIDTH + SGU_WIDTH:]))
        vhat_scr[...] = vhat
        vln_scr[...] = vhat * lg_ref[...] + lb_ref[...]
        row = lax.broadcasted_iota(jnp.int32, (GROUP, GROUP), 0)
        col = lax.broadcasted_iota(jnp.int32, (GROUP, GROUP), 1)
        for h in range(N_HEADS):
            ws = _masked_sgu_w(sw_ref, h).astype(BF16)
            vh = _to_head_major(vln_scr, h, nc).astype(BF16)
            mixed = _dot_nn(ws, vh) + sb_ref[:, h:h + 1]
            dsg_h = _to_head_major(dmix_scr, h, nc, POOL_WIDTH)
            du_h = dsg_h * mixed
            dm_h = dsg_h * _to_head_major(u_scr, h, nc)
            pos_sums = lax.dot_general(jnp.ones((8, nc * GROUP), F32), dm_h, (((1,), (1,)), ((), ())),
                                       precision=lax.Precision.HIGH, preferred_element_type=F32)
            dvec_ref[ROW_SGU_B + h:ROW_SGU_B + h + 1, 0:GROUP] += pos_sums[0:1, :]
            dmb = dm_h.astype(BF16)
            dmat_ref[len(POOL_WINDOWS) + h] += jnp.where(row >= col, _dot_nt(dmb, vh), 0.0)
            dv_h = _dot_tn(ws, dmb)
            for c in range(nc):
                rs = slice(c * GROUP, (c + 1) * GROUP)
                cs = slice(h * GROUP, (h + 1) * GROUP)
                du_scr[rs, cs] = du_h[:, c * GROUP:(c + 1) * GROUP]
                dv_scr[rs, cs] = dv_h[:, c * GROUP:(c + 1) * GROUP]
        dv_ln = dv_scr[...]
        vhat = vhat_scr[...]
        dvec_ref[ROW_SLN_B:ROW_SLN_B + 1, 0:SGU_WIDTH] += _rowsum(dv_ln)
        dvec_ref[ROW_SLN_G:ROW_SLN_G + 1, 0:SGU_WIDTH] += _rowsum(dv_ln * vhat)
        dv = _ln_bwd(dv_ln, vhat, rstd_v, lg_ref[...])
        dproj_ref[:, POOL_WIDTH:POOL_WIDTH + SGU_WIDTH] = (
            du_scr[...] * _gelu_grad(proj_ref[:, POOL_WIDTH:POOL_WIDTH + SGU_WIDTH])).astype(BF16)
        dproj_ref[:, POOL_WIDTH + SGU_WIDTH:] = (dv * _gelu_grad(proj_ref[:, POOL_WIDTH + SGU_WIDTH:])).astype(BF16)
        gx_ref[...] = ALPHA * dr1_ref[...] + _dot_nn(dproj_ref[...], win_ref[...])

    def tile(cols):
        return pl.BlockSpec((tt, cols), lambda i: (i, 0))

    def whole(a):
        nd = a.ndim
        return pl.BlockSpec(a.shape, lambda i: (0,) * nd)

    def resident(shape):
        nd = len(shape)
        return pl.BlockSpec(shape, lambda i: (0,) * nd)

    next_halo = pl.BlockSpec((HALO, D_MODEL), lambda i: (jnp.minimum((i + 1) * (tt // HALO), n_halo_blocks - 1), 0))
    prev_halo = pl.BlockSpec((HALO, POOL_WIDTH), lambda i: (jnp.maximum(i * (tt // HALO) - 1, 0), 0))
    consts = [win_t, wout, pool_w, pool_scale, sln_g, sln_b, sgu_w, sgu_b_t, stats]
    small_shapes = [(len(POOL_WINDOWS) + N_HEADS, GROUP, GROUP), (VEC_ROWS, D_MODEL)]
    return pl.pallas_call(
        body, name="mixer_bwd", grid=(n_tiles,),
        in_specs=[tile(D_MODEL), next_halo, tile(IN_COLS), prev_halo] + [whole(a) for a in consts],
        out_specs=[tile(D_MODEL), tile(IN_COLS)] + [resident(s) for s in small_shapes],
        out_shape=[jax.ShapeDtypeStruct((tokens, D_MODEL), F32), jax.ShapeDtypeStruct((tokens, IN_COLS), BF16)]
                  + [jax.ShapeDtypeStruct(s, F32) for s in small_shapes],
        scratch_shapes=[pltpu.VMEM((tt, SGU_WIDTH), F32), pltpu.VMEM((tt, SGU_WIDTH), F32),
                        pltpu.VMEM((tt, D_MODEL), F32), pltpu.VMEM((tt, SGU_WIDTH), F32),
                        pltpu.VMEM((tt, SGU_WIDTH), F32), pltpu.VMEM((tt, SGU_WIDTH), F32)],
        compiler_params=pltpu.CompilerParams(dimension_semantics=("arbitrary",), vmem_limit_bytes=VMEM_LIMIT_BYTES),
    )(dr1, dr1, proj, proj, *consts)


def _owner_copies(src_ref, land_ref, sems):
    x, y, c = _mesh_position()
    return [pltpu.make_async_remote_copy(
        src_ref=src_ref.at[2 * cx + cy], dst_ref=land_ref.at[j], send_sem=sems[j], recv_sem=sems[3 + j],
        device_id=(cx, cy, c), device_id_type=MESH) for j, (cx, cy) in enumerate(_other_chips(x, y))]


def _send_to_owners_start(chip_partial, name, collective_id):
    land = _received_shape(chip_partial)

    def body(src_ref, land_ref, *rest):
        x, y, c = _mesh_position()
        barrier = pltpu.get_barrier_semaphore()
        for cx, cy in _other_chips(x, y):
            pl.semaphore_signal(barrier, inc=1, device_id=(cx, cy, c), device_id_type=MESH)
        pl.semaphore_wait(barrier, 3)
        for cp in _owner_copies(src_ref, land_ref, rest[:6]):
            cp.start()
        rest[8][...] = jnp.zeros_like(rest[8])

    hbm = pl.BlockSpec(memory_space=pltpu.HBM)
    sem = pl.BlockSpec(memory_space=pltpu.SEMAPHORE)
    outs = pl.pallas_call(
        body, name=name,
        out_shape=[pltpu.SemaphoreType.DMA(())] * 6 + [pltpu.HBM(chip_partial.shape, chip_partial.dtype),
                                                       pltpu.HBM(land.shape, land.dtype),
                                                       jax.ShapeDtypeStruct((8, GROUP), F32)],
        in_specs=[hbm, hbm], out_specs=[sem] * 6 + [hbm, hbm, pl.BlockSpec(memory_space=pltpu.VMEM)],
        input_output_aliases={0: 6, 1: 7},
        compiler_params=pltpu.CompilerParams(has_side_effects=pltpu.SideEffectType.DATAFLOW_SIDE_EFFECTING,
                                             collective_id=collective_id),
    )(pltpu.with_memory_space_constraint(chip_partial, pltpu.HBM),
      pltpu.with_memory_space_constraint(lax.empty(land.shape, land.dtype), pltpu.HBM))
    return outs[:6], outs[6], outs[7], outs[8]


def _send_to_owners_wait(sems, src_thru, land_thru, after, name):
    def body(src_ref, land_ref, *rest):
        for cp in _owner_copies(src_ref, land_ref, rest[:6]):
            cp.wait_send()
            cp.wait_recv()

    hbm = pl.BlockSpec(memory_space=pltpu.HBM)
    sem = pl.BlockSpec(memory_space=pltpu.SEMAPHORE)
    return pl.pallas_call(
        body, name=name,
        out_shape=[pltpu.HBM(src_thru.shape, src_thru.dtype), pltpu.HBM(land_thru.shape, land_thru.dtype)],
        in_specs=[hbm, hbm] + [sem] * 6 + [pl.BlockSpec(memory_space=pl.ANY)], out_specs=[hbm, hbm],
        input_output_aliases={0: 0, 1: 1},
        compiler_params=pltpu.CompilerParams(has_side_effects=pltpu.SideEffectType.DATAFLOW_SIDE_EFFECTING),
    )(src_thru, land_thru, *sems, after)[1]


def _adamw_math(w, g, m, v):
    m = ADAM_B1 * m + (1.0 - ADAM_B1) * g
    v = ADAM_B2 * v + (1.0 - ADAM_B2) * (g * g)
    m_hat = m / (1.0 - ADAM_B1 ** ADAM_STEP)
    v_hat = v / (1.0 - ADAM_B2 ** ADAM_STEP)
    delta = -ADAM_LR * (m_hat / (jnp.sqrt(v_hat) + ADAM_EPS) + ADAM_WD * w)
    return delta, m, v


def _sum_adamw(w, m, v, own, received, name):
    rows, cols = w.shape
    rb = _row_block(rows, 256)

    def body(w_ref, m_ref, v_ref, own_ref, rec_ref, g_ref, d_ref, mo_ref, vo_ref):
        g = own_ref[...]
        for j in range(3):
            g = g + rec_ref[j].astype(F32)
        g_ref[...] = g
        d_ref[...], mo_ref[...], vo_ref[...] = _adamw_math(w_ref[...], g, m_ref[...], v_ref[...])

    spec = pl.BlockSpec((rb, cols), lambda r: (r, 0))
    return pl.pallas_call(
        body, name=name, grid=(rows // rb,),
        in_specs=[spec] * 4 + [pl.BlockSpec((3, rb, cols), lambda r: (0, r, 0))],
        out_specs=[spec] * 4, out_shape=[jax.ShapeDtypeStruct((rows, cols), F32)] * 4,
        compiler_params=pltpu.CompilerParams(dimension_semantics=("arbitrary",)),
    )(w, m, v, own, received)


SMALL_NAMES = ("pool_w", "sgu_w", "pool_scale", "sgu_ln_g", "sgu_ln_b", "sgu_b", "ln1_g", "ln1_b", "ln2_g", "ln2_b")
_SMALL_VEC_ROWS = {"pool_scale": (ROW_POOL_SCALE, POOL_WIDTH), "sgu_ln_g": (ROW_SLN_G, SGU_WIDTH),
                   "sgu_ln_b": (ROW_SLN_B, SGU_WIDTH), "ln1_g": (ROW_LN1_G, D_MODEL), "ln1_b": (ROW_LN1_B, D_MODEL),
                   "ln2_g": (ROW_LN2_G, D_MODEL), "ln2_b": (ROW_LN2_B, D_MODEL)}
_SMALL_MAT_FIRST = {"pool_w": 0, "sgu_w": len(POOL_WINDOWS)}


def _small_sum_adamw(mats_all, vecs_all, w, m, v):
    n = len(SMALL_NAMES)

    def body(mats_ref, vecs_ref, *refs):
        w_refs, m_refs, v_refs = refs[:n], refs[n:2 * n], refs[2 * n:3 * n]
        loss_ref = refs[3 * n]
        g_refs, d_refs, mo_refs, vo_refs = (refs[3 * n + 1 + k * n:3 * n + 1 + (k + 1) * n] for k in range(4))
        vec_scr = refs[7 * n + 1]

        def update(k, idx, g):
            d, mo, vo = _adamw_math(w_refs[k][idx], g, m_refs[k][idx], v_refs[k][idx])
            g_refs[k][idx], d_refs[k][idx], mo_refs[k][idx], vo_refs[k][idx] = g, d, mo, vo

        total = vecs_ref[0]
        for dev in range(1, N_DEV):
            total = total + vecs_ref[dev]
        vec_scr[...] = total
        for k, name in enumerate(SMALL_NAMES):
            if name in _SMALL_MAT_FIRST:
                for b in range(4):
                    g = mats_ref[0, _SMALL_MAT_FIRST[name] + b]
                    for dev in range(1, N_DEV):
                        g = g + mats_ref[dev, _SMALL_MAT_FIRST[name] + b]
                    update(k, (0, b), g)
            elif name == "sgu_b":
                update(k, (0,), vec_scr[ROW_SGU_B:ROW_SGU_B + N_HEADS, 0:GROUP])
            else:
                row, width = _SMALL_VEC_ROWS[name]
                update(k, (slice(None), slice(None)), vec_scr[row:row + 1, 0:width])
        loss = jnp.sum(vec_scr[ROW_LOSS:ROW_LOSS + 1, :], axis=1, keepdims=True) * (0.5 / D_MODEL)
        loss_ref[...] = jnp.broadcast_to(loss, loss_ref.shape)

    vmem = pl.BlockSpec(memory_space=pltpu.VMEM)
    shapes = [jax.ShapeDtypeStruct(w[k].shape, F32) for k in SMALL_NAMES]
    outs = pl.pallas_call(
        body, name="small_sum_adamw",
        out_shape=[jax.ShapeDtypeStruct((8, GROUP), F32)] + shapes * 4,
        in_specs=[vmem] * (2 + 3 * n), out_specs=[vmem] * (1 + 4 * n),
        scratch_shapes=[pltpu.VMEM((VEC_ROWS, D_MODEL), F32)],
    )(mats_all, vecs_all, *[w[k] for k in SMALL_NAMES], *[m[k] for k in SMALL_NAMES], *[v[k] for k in SMALL_NAMES])
    return outs[0], outs[1:1 + n], outs[1 + n:1 + 2 * n], outs[1 + 2 * n:1 + 3 * n], outs[1 + 3 * n:]


def kernel(x, w_in, pool_w, pool_scale, sgu_ln_g, sgu_ln_b, sgu_w, sgu_b, w_out, ln1_g, ln1_b, w_gate_up, w_down, ln2_g, ln2_b, loss_target, m_w_in, m_pool_w, m_pool_scale, m_sgu_ln_g, m_sgu_ln_b, m_sgu_w, m_sgu_b, m_w_out, m_ln1_g, m_ln1_b, m_w_gate_up, m_w_down, m_ln2_g, m_ln2_b, v_w_in, v_pool_w, v_pool_scale, v_sgu_ln_g, v_sgu_ln_b, v_sgu_w, v_sgu_b, v_w_out, v_ln1_g, v_ln1_b, v_w_gate_up, v_w_down, v_ln2_g, v_ln2_b):
    bl, seq, _ = x.shape
    tokens = bl * seq
    x2d = x.reshape(tokens, D_MODEL)
    tgt2d = loss_target.reshape(tokens, D_MODEL)

    pool_w3, sgu_w3 = pool_w[0], sgu_w[0]
    sgu_b_t = sgu_b[0].T
    proj, xhat1, rstd1, mix_bf, x_bf, win_t, wout, wgu_t, wdown = _mixer_fwd(
        x2d, [w_in[0].T, w_out[0]], pool_w3, pool_scale, sgu_ln_g, sgu_ln_b, sgu_w3, sgu_b_t, ln1_g, ln1_b,
        [w_gate_up[0].T, w_down[0]], seq)
    dr1, acts_bf, grads_bf, stats = _ffn_fwd_bwd(xhat1, rstd1, tgt2d, wgu_t, wdown, ln1_g, ln1_b, ln2_g, ln2_b)

    bf_gu, own_gu = _wgrad_exchange(grads_bf, acts_bf, 1, "wgrad_gate_up", collective_id=1,
                                    lhs_cols=(0, 2 * D_FF), rhs_block=0)
    sent_gu = _send_to_owners_start(bf_gu, "grad_scatter_w_gate_up_start", collective_id=4)
    bf_dn, own_dn = _wgrad_exchange(grads_bf, acts_bf, 2, "wgrad_down", after=sent_gu[3], collective_id=2,
                                    lhs_cols=(2 * D_FF, D_FF), rhs_block=1)
    sent_dn = _send_to_owners_start(bf_dn, "grad_scatter_w_down_start", collective_id=5)
    bf_out, own_out = _wgrad_exchange(mix_bf, acts_bf, N_CHIPS, "wgrad_out", after=sent_dn[3], collective_id=3,
                                      rhs_block=2)
    sent_out = _send_to_owners_start(bf_out, "grad_scatter_w_out_start", collective_id=6)
    gx, dproj_bf, d_mats, d_vecs = _mixer_bwd(
        dr1, proj, win_t, wout, pool_w3, pool_scale, sgu_ln_g, sgu_ln_b, sgu_w3, sgu_b_t, stats, seq)
    bf_in, own_in, mats_all, vecs_all = _wgrad_exchange(
        dproj_bf, x_bf, N_CHIPS, "wgrad_in", gather_rows=[d_mats.reshape(-1, GROUP), d_vecs], after=sent_out[3],
        collective_id=10)
    sent_in = _send_to_owners_start(bf_in, "grad_scatter_w_in_start", collective_id=7)

    grads, deltas, new_m, new_v = {}, {}, {}, {}
    after = sent_in[3]
    for nm, w, m, v, own, sent, transposed in (("w_gate_up", w_gate_up, m_w_gate_up, v_w_gate_up, own_gu, sent_gu, True),
                                               ("w_down", w_down, m_w_down, v_w_down, own_dn, sent_dn, False),
                                               ("w_out", w_out, m_w_out, v_w_out, own_out, sent_out, False),
                                               ("w_in", w_in, m_w_in, v_w_in, own_in, sent_in, True)):
        rows = (lambda a: a[0].T) if transposed else (lambda a: a[0])
        back = (lambda a: a.T[None]) if transposed else (lambda a: a[None])
        rec = _send_to_owners_wait(*sent[:3], after, "grad_scatter_" + nm + "_wait")
        g, d, mo, vo = _sum_adamw(rows(w), rows(m), rows(v), own, rec, "adamw_" + nm)
        after = vo
        grads[nm], deltas[nm], new_m[nm], new_v[nm] = back(g), back(d), back(mo), back(vo)

    small_w = {"pool_w": pool_w, "pool_scale": pool_scale, "sgu_ln_g": sgu_ln_g, "sgu_ln_b": sgu_ln_b, "sgu_w": sgu_w,
               "sgu_b": sgu_b, "ln1_g": ln1_g, "ln1_b": ln1_b, "ln2_g": ln2_g, "ln2_b": ln2_b}
    small_m = {"pool_w": m_pool_w, "pool_scale": m_pool_scale, "sgu_ln_g": m_sgu_ln_g, "sgu_ln_b": m_sgu_ln_b,
               "sgu_w": m_sgu_w, "sgu_b": m_sgu_b, "ln1_g": m_ln1_g, "ln1_b": m_ln1_b, "ln2_g": m_ln2_g, "ln2_b": m_ln2_b}
    small_v = {"pool_w": v_pool_w, "pool_scale": v_pool_scale, "sgu_ln_g": v_sgu_ln_g, "sgu_ln_b": v_sgu_ln_b,
               "sgu_w": v_sgu_w, "sgu_b": v_sgu_b, "ln1_g": v_ln1_g, "ln1_b": v_ln1_b, "ln2_g": v_ln2_g, "ln2_b": v_ln2_b}
    loss_blk, g_small, d_small, m_small, v_small = _small_sum_adamw(
        mats_all.reshape(N_DEV, 2 * N_HEADS, GROUP, GROUP), vecs_all.reshape(N_DEV, VEC_ROWS, D_MODEL),
        small_w, small_m, small_v)
    for vals, dst in ((g_small, grads), (d_small, deltas), (m_small, new_m), (v_small, new_v)):
        dst.update(zip(SMALL_NAMES, vals))

    order = ["w_in", "pool_w", "pool_scale", "sgu_ln_g", "sgu_ln_b", "sgu_w", "sgu_b", "w_out", "ln1_g", "ln1_b",
             "w_gate_up", "w_down", "ln2_g", "ln2_b"]
    return (loss_blk[0, 0], gx.reshape(bl, seq, D_MODEL), *[grads[k] for k in order], *[deltas[k] for k in order],
            *[new_m[k] for k in order], *[new_v[k] for k in order])
```

```python
import jax
import jax.numpy as jnp
from jax import lax
from jax.experimental import pallas as pl
from jax.experimental.pallas import tpu as pltpu

F32 = jnp.float32
BF16 = jnp.bfloat16
MESH = pl.DeviceIdType.MESH

D_MODEL = 1024
POOL_WIDTH = 512
SGU_WIDTH = 512
POOL_WINDOWS = (2, 4, 8, 16)
GROUP = 128
N_HEADS = 4
IN_COLS = POOL_WIDTH + 2 * SGU_WIDTH
D_FF = 2816
LN_EPS = 1e-5
ALPHA = float(2.0 ** 0.25)
HALO = 16
N_DEV = 8
N_CHIPS = 4

ADAM_LR = 0.001
ADAM_B1 = 0.9
ADAM_B2 = 0.999
ADAM_EPS = 1e-08
ADAM_WD = 0.01
ADAM_STEP = 10

VMEM_LIMIT_BYTES = 56 * 1024 * 1024

_SQRT_HALF = 0.7071067811865476
_INV_SQRT_2PI = 0.3989422804014327


def _dot_nn(a, b):
    return lax.dot_general(a, b, (((1,), (0,)), ((), ())), preferred_element_type=F32)


def _dot_nt(a, b):
    return lax.dot_general(a, b, (((1,), (1,)), ((), ())), preferred_element_type=F32)


def _dot_tn(a, b):
    return lax.dot_general(a, b, (((0,), (0,)), ((), ())), preferred_element_type=F32)


def _gelu(x):
    return 0.5 * x * (1.0 + lax.erf(x * _SQRT_HALF))


def _gelu_grad(x):
    return 0.5 * (1.0 + lax.erf(x * _SQRT_HALF)) + x * jnp.exp(-0.5 * x * x) * _INV_SQRT_2PI


def _ln_stats(r):
    mu = jnp.mean(r, axis=-1, keepdims=True)
    d = r - mu
    var = jnp.mean(d * d, axis=-1, keepdims=True)
    rstd = lax.rsqrt(var + LN_EPS)
    return d * rstd, rstd


def _ln_bwd(dout, xhat, rstd, g):
    dxh = dout * g
    m1 = jnp.mean(dxh, axis=-1, keepdims=True)
    m2 = jnp.mean(dxh * xhat, axis=-1, keepdims=True)
    return rstd * (dxh - m1 - xhat * m2)


def _rowsum(a):
    return jnp.sum(a, axis=0, keepdims=True)


def _pool_fwd(xp, xp_prev, inv_cnt, w):
    s = jnp.concatenate([xp_prev, xp], axis=0)
    k = 1
    while k < w:
        s = s + pltpu.roll(s, k, 0)
        k *= 2
    return s[HALO:, :] * inv_cnt - xp


def _pool_bwd(dpooled, dpooled_next, inv_cnt, inv_cnt_next, w):
    n = dpooled.shape[0] + HALO
    s = jnp.concatenate([dpooled * inv_cnt, dpooled_next * inv_cnt_next], axis=0)
    k = 1
    while k < w:
        s = s + pltpu.roll(s, n - k, 0)
        k *= 2
    return s[: dpooled.shape[0], :] - dpooled


def _inv_count(pos, w):
    return 1.0 / jnp.minimum(pos + 1, w).astype(F32)


def _to_head_major(a, h, nc, first_col=0):
    lo = first_col + h * GROUP
    return jnp.concatenate([a[c * GROUP:(c + 1) * GROUP, lo:lo + GROUP] for c in range(nc)], axis=1)


def _masked_sgu_w(sw_ref, h):
    row = lax.broadcasted_iota(jnp.int32, (GROUP, GROUP), 0)
    col = lax.broadcasted_iota(jnp.int32, (GROUP, GROUP), 1)
    return jnp.where(row >= col, sw_ref[h], 0.0)


def _row_block(rows, limit):
    return max(b for b in range(16, min(rows, limit) + 1, 16) if rows % b == 0)


def _mesh_position():
    return lax.axis_index("x"), lax.axis_index("y"), lax.axis_index("c")


def _other_chips(x, y):
    return [(1 - x, y), (x, 1 - y), (1 - x, 1 - y)]


class _TwoLevelGather:
    def __init__(self, ins, outs, send_sems, recv_sems, local_sems):
        self.ins, self.outs = ins, outs
        self.send_sems, self.recv_sems, self.local_sems = send_sems, recv_sems, local_sems
        self.na = len(ins)
        x, y, c = _mesh_position()
        self.c = c
        self.me, self.sibling = (x, y, c), (x, y, 1 - c)
        self.chips = _other_chips(x, y)
        self.relay_from = (x + (1 - c) * (1 - 2 * x), y + c * (1 - 2 * y))
        self.relay_to = (x + c * (1 - 2 * x), y + (1 - c) * (1 - 2 * y))

    def handshake(self):
        peers = [self.sibling] + [(*chip, self.c) for chip in self.chips[:2]]
        barrier = pltpu.get_barrier_semaphore()
        for peer in peers:
            pl.semaphore_signal(barrier, inc=1, device_id=peer, device_id_type=MESH)
        pl.semaphore_wait(barrier, len(peers))

    def _rows(self, a, px, py, pc):
        n = self.ins[a].shape[0]
        return self.outs[a].at[pl.ds((4 * px + 2 * py + pc) * n, n), :]

    def _copy(self, a, k, block, to, src=None):
        return pltpu.make_async_remote_copy(
            src_ref=self._rows(a, *block) if src is None else src, dst_ref=self._rows(a, *block),
            send_sem=self.send_sems.at[a * 7 + k], recv_sem=self.recv_sems.at[a * 7 + k],
            device_id=to, device_id_type=MESH)

    def _mine(self, a):
        return pltpu.make_async_copy(self.ins[a], self._rows(a, *self.me), self.local_sems.at[a])

    def start(self, arrays=None):
        arrays = range(self.na) if arrays is None else arrays
        for a in arrays:
            self._mine(a).start()
        for a in arrays:
            self._copy(a, 0, self.me, self.sibling, src=self.ins[a]).start()
            for j, chip in enumerate(self.chips[:2]):
                self._copy(a, 1 + j, self.me, (*chip, self.c), src=self.ins[a]).start()

    def relay(self, a):
        c, block = self.c, (*self.relay_from, self.c)
        self._copy(a, 1 + c, block, self.me).wait_recv()
        self._copy(a, 3, block, (*self.relay_to, c)).start()
        self._copy(a, 4 + c, block, self.sibling).start()

    def pass_on(self, a):
        c = self.c
        self._copy(a, 2 - c, (*self.relay_to, c), self.me).wait_recv()
        self._copy(a, 5 - c, (*self.relay_to, c), self.sibling).start()
        self._copy(a, 3, (*self.chips[2], c), self.me).wait_recv()
        self._copy(a, 6, (*self.chips[2], c), self.sibling).start()

    def finish(self, arrays=None):
        arrays = range(self.na) if arrays is None else arrays
        for a in arrays:
            self._copy(a, 0, self.sibling, self.me).wait_recv()
            for j, chip in enumerate(self.chips):
                self._copy(a, 4 + j, (*chip, 1 - self.c), self.me).wait_recv()
        for a in arrays:
            for k in range(7):
                self._copy(a, k, self.me, self.sibling, src=self.ins[a]).wait_send()
            self._mine(a).wait()

    @staticmethod
    def scratch(na):
        return [pltpu.SemaphoreType.DMA((7 * na,)), pltpu.SemaphoreType.DMA((7 * na,)), pltpu.SemaphoreType.DMA((na,))]


def _gathered_shape(s):
    return jax.ShapeDtypeStruct((N_DEV * s.shape[0], s.shape[1]), s.dtype)


def _gathered_bf16(s):
    return jax.ShapeDtypeStruct((N_DEV * s.shape[0], s.shape[1]), BF16)


def _mixer_fwd(x2d, own_shards, pool_w, pool_scale, sln_g, sln_b, sgu_w, sgu_b_t, ln1_g, ln1_b, later_shards, seq):
    tokens = x2d.shape[0]
    tt = min(512, seq)
    tiles_per_seq = seq // tt
    nc = tt // GROUP
    n_tiles = tokens // tt
    n_later = len(later_shards)
    shards = list(own_shards) + list(later_shards)
    ns = len(shards)

    def body(x_ref, xh_ref, pw_ref, ps_ref, lg_ref, lb_ref, sw_ref, sb_ref, g1_ref, b1_ref, *rest):
        shard_refs, rest = rest[:ns], rest[ns:]
        proj_ref, xhat_ref, rstd_ref, mix_ref, xbf_ref = rest[:5]
        win_out, wout_out = rest[5:7]
        later_refs, rest = rest[7:7 + n_later], rest[7 + n_later:]
        mix_scr, win_ref, wout_ref = rest[:3]
        shard_bf_refs, (send_sems, recv_sems, local_sems, copy_sems) = rest[3:3 + ns], rest[3 + ns:]
        i = pl.program_id(0)
        gather = _TwoLevelGather(shard_bf_refs, [win_ref, wout_ref, *later_refs], send_sems, recv_sems, local_sems)
        exports = [pltpu.make_async_copy(win_ref, win_out, copy_sems.at[0]),
                   pltpu.make_async_copy(wout_ref, wout_out, copy_sems.at[1])]

        @pl.when(i == 0)
        def _():
            gather.handshake()
            for a in range(ns):
                shard_bf_refs[a][...] = shard_refs[a][...].astype(BF16)
            gather.start([0, 1])
            gather.relay(0)
            gather.relay(1)
            gather.start(range(2, ns))
            gather.pass_on(0)
            gather.pass_on(1)
            gather.finish([0, 1])
            for cp in exports:
                cp.start()

        tile_in_seq = i % tiles_per_seq
        x = x_ref[...]
        xb = x.astype(BF16)
        xbf_ref[...] = xb
        proj = _dot_nt(xb, win_ref[...])
        proj_ref[...] = proj
        xp_prev = _dot_nt(xh_ref[...].astype(BF16), win_ref[0:POOL_WIDTH, :])
        xp_prev = jnp.where(tile_in_seq == 0, 0.0, xp_prev)
        pos = tile_in_seq * tt + lax.broadcasted_iota(jnp.int32, (tt, 1), 0)
        for g, w in enumerate(POOL_WINDOWS):
            sl = slice(g * GROUP, (g + 1) * GROUP)
            pooled = _pool_fwd(proj[:, sl], xp_prev[:, sl], _inv_count(pos, w), w)
            pre = _dot_nn(pooled.astype(BF16), pw_ref[g].astype(BF16))
            mix_scr[:, sl] = pre * ps_ref[:, sl]
        u = _gelu(proj[:, POOL_WIDTH:POOL_WIDTH + SGU_WIDTH])
        v = _gelu(proj[:, POOL_WIDTH + SGU_WIDTH:])
        vhat, _ = _ln_stats(v)
        v_ln = vhat * lg_ref[...] + lb_ref[...]
        for h in range(N_HEADS):
            ws = _masked_sgu_w(sw_ref, h).astype(BF16)
            mixed = _dot_nn(ws, _to_head_major(v_ln, h, nc).astype(BF16)) + sb_ref[:, h:h + 1]
            for c in range(nc):
                rs = slice(c * GROUP, (c + 1) * GROUP)
                mix_scr[rs, POOL_WIDTH + h * GROUP:POOL_WIDTH + (h + 1) * GROUP] = (
                    u[rs, h * GROUP:(h + 1) * GROUP] * mixed[:, c * GROUP:(c + 1) * GROUP])
        mixb = mix_scr[...].astype(BF16)
        mix_ref[...] = mixb
        r1 = ALPHA * x + _dot_nn(mixb, wout_ref[...])
        xhat, rstd = _ln_stats(r1)
        xhat_ref[...] = xhat
        rstd_ref[...] = rstd

        for a in range(2, ns):
            relay_tile = min(n_tiles // 2 + 1 + 2 * (a - 2), n_tiles - 1)

            @pl.when(i == relay_tile)
            def _(a=a):
                gather.relay(a)

            @pl.when(i == max(n_tiles - ns + a, relay_tile))
            def _(a=a):
                gather.pass_on(a)

        @pl.when(i == n_tiles - 1)
        def _():
            gather.finish(range(2, ns))
            for cp in exports:
                cp.wait()

    def tile(cols):
        return pl.BlockSpec((tt, cols), lambda i: (i, 0))

    def whole(a):
        nd = a.ndim
        return pl.BlockSpec(a.shape, lambda i: (0,) * nd)

    any_spec = pl.BlockSpec(memory_space=pl.ANY)
    halo = pl.BlockSpec((HALO, D_MODEL), lambda i: (jnp.maximum(i * (tt // HALO) - 1, 0), 0))
    consts = [pool_w, pool_scale, sln_g, sln_b, sgu_w, sgu_b_t, ln1_g, ln1_b]
    gathered = [_gathered_bf16(s) for s in shards]
    return pl.pallas_call(
        body, name="mixer_fwd", grid=(n_tiles,),
        in_specs=[tile(D_MODEL), halo] + [whole(a) for a in consts] + [whole(s) for s in shards],
        out_specs=[tile(IN_COLS), tile(D_MODEL), tile(1), tile(D_MODEL), tile(D_MODEL)] + [any_spec] * ns,
        out_shape=[jax.ShapeDtypeStruct((tokens, IN_COLS), F32), jax.ShapeDtypeStruct((tokens, D_MODEL), F32),
                   jax.ShapeDtypeStruct((tokens, 1), F32), jax.ShapeDtypeStruct((tokens, D_MODEL), BF16),
                   jax.ShapeDtypeStruct((tokens, D_MODEL), BF16)] + gathered,
        scratch_shapes=[pltpu.VMEM((tt, D_MODEL), F32)] + [pltpu.VMEM(g.shape, BF16) for g in gathered[:2]]
                       + [pltpu.VMEM(s.shape, BF16) for s in shards] + _TwoLevelGather.scratch(ns)
                       + [pltpu.SemaphoreType.DMA((2,))],
        compiler_params=pltpu.CompilerParams(dimension_semantics=("arbitrary",), vmem_limit_bytes=VMEM_LIMIT_BYTES,
                                             collective_id=9),
    )(x2d, x2d, *consts, *shards)


def _ffn_fwd_bwd(xhat1, rstd1, target, wgu_t, wdown, ln1_g, ln1_b, ln2_g, ln2_b):
    tokens = xhat1.shape[0]
    tt = min(256, tokens)

    def body(xhat_ref, rstd_ref, tgt_ref, wgu_hbm, wd_hbm, g1_ref, b1_ref, g2_ref, b2_ref,
             dr1_ref, acts_ref, grads_ref, stats_ref, wgu_ref, wd_ref, gu_scr, sems):
        i = pl.program_id(0)

        @pl.when(i == 0)
        def _():
            loads = [pltpu.make_async_copy(wgu_hbm, wgu_ref, sems.at[0]),
                     pltpu.make_async_copy(wd_hbm, wd_ref, sems.at[1])]
            for cp in loads:
                cp.start()
            stats_ref[...] = jnp.zeros_like(stats_ref)
            for cp in loads:
                cp.wait()

        xhat1_t = xhat_ref[...]
        h = xhat1_t * g1_ref[...] + b1_ref[...]
        hb = h.astype(BF16)
        acts_ref[:, 0:D_MODEL] = hb
        gate = _dot_nt(hb, wgu_ref[0:D_FF, :])
        up = _dot_nt(hb, wgu_ref[D_FF:, :])
        gu_scr[:, 0:D_FF] = gate
        gu_scr[:, D_FF:] = up
        ab = (gate * jax.nn.sigmoid(gate) * up).astype(BF16)
        grads_ref[:, 2 * D_FF:] = ab
        xhat2, rstd2 = _ln_stats(ALPHA * h + _dot_nn(ab, wd_ref[...]))
        err = xhat2 * g2_ref[...] + b2_ref[...] - tgt_ref[...]
        dy = err * (1.0 / D_MODEL)
        stats_ref[0:1, :] += _rowsum(dy * xhat2)
        stats_ref[1:2, :] += _rowsum(dy)
        stats_ref[4:5, :] += _rowsum(err * err)
        dr2 = _ln_bwd(dy, xhat2, rstd2, g2_ref[...])
        dr2b = dr2.astype(BF16)
        acts_ref[:, D_MODEL:2 * D_MODEL] = dr2b
        da = _dot_nt(dr2b, wd_ref[...])
        gate = gu_scr[:, 0:D_FF]
        up = gu_scr[:, D_FF:]
        sg = jax.nn.sigmoid(gate)
        dgate = (da * up * (sg * (1.0 + gate * (1.0 - sg)))).astype(BF16)
        dup = (da * (gate * sg)).astype(BF16)
        grads_ref[:, 0:D_FF] = dgate
        grads_ref[:, D_FF:2 * D_FF] = dup
        dh = ALPHA * dr2 + _dot_nn(dgate, wgu_ref[0:D_FF, :]) + _dot_nn(dup, wgu_ref[D_FF:, :])
        stats_ref[2:3, :] += _rowsum(dh * xhat1_t)
        stats_ref[3:4, :] += _rowsum(dh)
        dr1 = _ln_bwd(dh, xhat1_t, rstd_ref[...], g1_ref[...])
        dr1_ref[...] = dr1
        acts_ref[:, 2 * D_MODEL:] = dr1.astype(BF16)

    def tile(cols):
        return pl.BlockSpec((tt, cols), lambda i: (i, 0))

    def whole(a):
        nd = a.ndim
        return pl.BlockSpec(a.shape, lambda i: (0,) * nd)

    any_spec = pl.BlockSpec(memory_space=pl.ANY)
    vecs = [ln1_g, ln1_b, ln2_g, ln2_b]
    return pl.pallas_call(
        body, name="ffn_fwd_bwd", grid=(tokens // tt,),
        in_specs=[tile(D_MODEL), tile(1), tile(D_MODEL), any_spec, any_spec] + [whole(a) for a in vecs],
        out_specs=[tile(D_MODEL), tile(3 * D_MODEL), tile(3 * D_FF), pl.BlockSpec((8, D_MODEL), lambda i: (0, 0))],
        out_shape=[jax.ShapeDtypeStruct((tokens, D_MODEL), F32), jax.ShapeDtypeStruct((tokens, 3 * D_MODEL), BF16),
                   jax.ShapeDtypeStruct((tokens, 3 * D_FF), BF16), jax.ShapeDtypeStruct((8, D_MODEL), F32)],
        scratch_shapes=[pltpu.VMEM(wgu_t.shape, BF16), pltpu.VMEM(wdown.shape, BF16),
                        pltpu.VMEM((tt, 2 * D_FF), F32), pltpu.SemaphoreType.DMA((2,))],
        compiler_params=pltpu.CompilerParams(dimension_semantics=("arbitrary",), vmem_limit_bytes=VMEM_LIMIT_BYTES),
    )(xhat1, rstd1, target, wgu_t, wdown, *vecs)


def _wgrad_exchange(lhs, rhs, chips_per_block, name, gather_rows=(), after=None, collective_id=None,
                    lhs_cols=None, rhs_block=None, token_tile=2048):
    tokens = lhs.shape[0]
    first_col, n_all = (0, lhs.shape[1]) if lhs_cols is None else lhs_cols
    m = rhs.shape[1] if rhs_block is None else D_MODEL
    n = n_all // N_DEV
    tw = min(token_tile, tokens)
    nt = tokens // tw
    cpb = chips_per_block
    nj = N_CHIPS // cpb
    lhs_block0 = first_col // (2 * n * cpb)
    assert lhs_block0 * 2 * n * cpb == first_col
    rhs_col = 0 if rhs_block is None else rhs_block
    ng = len(gather_rows)
    anchors = [] if after is None else [after]

    def body(l_ref, r_ref, *rest):
        small_refs, rest = rest[:ng], rest[ng + len(anchors):]
        bf_ref, own_ref = rest[:2]
        gathered_refs, rest = rest[2:2 + ng], rest[2 + ng:]
        acc, kept, landed, send_sems, recv_sems = rest[:5]
        sendbuf = bf_ref
        j, t = pl.program_id(0), pl.program_id(1)
        first, last = (j == 0) & (t == 0), (j == nj - 1) & (t == nt - 1)
        x, y, c = _mesh_position()
        if ng:
            gather = _TwoLevelGather(small_refs, gathered_refs, *rest[5:8])

            @pl.when(first)
            def _():
                gather.handshake()
                gather.start()
        else:
            @pl.when(first)
            def _():
                barrier = pltpu.get_barrier_semaphore()
                pl.semaphore_signal(barrier, inc=1, device_id=(x, y, 1 - c), device_id_type=MESH)
                pl.semaphore_wait(barrier, 1)

        def copy(q):
            return pltpu.make_async_remote_copy(
                src_ref=sendbuf.at[q], dst_ref=landed.at[q], send_sem=send_sems.at[q], recv_sem=recv_sems.at[q],
                device_id=(x, y, 1 - c), device_id_type=MESH)

        @pl.when(t == 0)
        def _():
            acc[...] = _dot_tn(l_ref[...], r_ref[...])

        @pl.when(t > 0)
        def _():
            acc[...] += _dot_tn(l_ref[...], r_ref[...])

        @pl.when(t == nt - 1)
        def _():
            for qq in range(cpb):
                q = j * cpb + qq
                kept[q] = acc[pl.ds(pl.multiple_of(qq * 2 * n + c * n, 8), n), :]
                sendbuf[q] = acc[pl.ds(pl.multiple_of(qq * 2 * n + (1 - c) * n, 8), n), :].astype(BF16)
                copy(q).start()

        if ng:
            @pl.when((j == nj - 1) & (t == nt // 2))
            def _():
                for a in range(ng):
                    gather.relay(a)

            @pl.when(last)
            def _():
                for a in range(ng):
                    gather.pass_on(a)
                gather.finish()

        @pl.when(last)
        def _():
            for q in range(N_CHIPS):
                copy(q).wait_send()
                copy(q).wait_recv()
            for q in range(N_CHIPS):
                s = kept[q] + landed[q].astype(F32)
                bf_ref[q] = s.astype(BF16)

                @pl.when(q == 2 * x + y)
                def _(s=s):
                    own_ref[...] = s

    any_spec = pl.BlockSpec(memory_space=pl.ANY)
    by_chip = (N_CHIPS, n, m)
    return pl.pallas_call(
        body, name=name, grid=(nj, nt),
        in_specs=[pl.BlockSpec((tw, 2 * n * cpb), lambda j, t: (t, lhs_block0 + j)),
                  pl.BlockSpec((tw, m), lambda j, t: (t, rhs_col))] + [any_spec] * (ng + len(anchors)),
        out_specs=[pl.BlockSpec(by_chip, lambda j, t: (0, 0, 0)), pl.BlockSpec((n, m), lambda j, t: (0, 0))]
                  + [any_spec] * ng,
        out_shape=[jax.ShapeDtypeStruct(by_chip, BF16), jax.ShapeDtypeStruct((n, m), F32)]
                  + [_gathered_shape(s) for s in gather_rows],
        scratch_shapes=[pltpu.VMEM((2 * n * cpb, m), F32), pltpu.VMEM(by_chip, F32), pltpu.VMEM(by_chip, BF16),
                        pltpu.SemaphoreType.DMA((N_CHIPS,)), pltpu.SemaphoreType.DMA((N_CHIPS,))]
                       + (_TwoLevelGather.scratch(ng) if ng else []),
        compiler_params=pltpu.CompilerParams(dimension_semantics=("arbitrary", "arbitrary"),
                                             vmem_limit_bytes=VMEM_LIMIT_BYTES + 4 * 1024 * 1024,
                                             collective_id=collective_id),
    )(lhs, rhs, *gather_rows, *anchors)


def _received_shape(p):
    return jax.ShapeDtypeStruct((3,) + p.shape[1:], p.dtype)


ROW_POOL_SCALE, ROW_SLN_G, ROW_SLN_B, ROW_SGU_B = 0, 1, 2, 3
ROW_LN2_G, ROW_LN2_B, ROW_LN1_G, ROW_LN1_B, ROW_LOSS = 8, 9, 10, 11, 12
VEC_ROWS = 16


def _mixer_bwd(dr1, proj, win_t, wout, pool_w, pool_scale, sln_g, sln_b, sgu_w, sgu_b_t, stats, seq):
    tokens = dr1.shape[0]
    tt = min(1024, seq)
    tiles_per_seq = seq // tt
    nc = tt // GROUP
    n_halo_blocks = tokens // HALO
    n_tiles = tokens // tt

    def body(dr1_ref, dr1n_ref, proj_ref, projh_ref, win_ref, wout_ref, pw_ref, ps_ref, lg_ref, lb_ref, sw_ref, sb_ref,
             stats_ref, gx_ref, dproj_ref, dmat_ref, dvec_ref, du_scr, dv_scr, dmix_scr, u_scr, vln_scr, vhat_scr):
        i = pl.program_id(0)
        tile_in_seq = i % tiles_per_seq

        @pl.when(i == 0)
        def _():
            dmat_ref[...] = jnp.zeros_like(dmat_ref)
            dvec_ref[0:8, :] = jnp.zeros((8, D_MODEL), F32)
            dvec_ref[8:16, :] = stats_ref[...]

        dmix_scr[...] = _dot_nt(dr1_ref[...].astype(BF16), wout_ref[...])
        dpo_next = _dot_nt(dr1n_ref[...].astype(BF16), wout_ref[0:POOL_WIDTH, :])
        dpo_next = jnp.where(tile_in_seq == tiles_per_seq - 1, 0.0, dpo_next)
        xp_prev = jnp.where(tile_in_seq == 0, 0.0, projh_ref[...])
        pos = tile_in_seq * tt + lax.broadcasted_iota(jnp.int32, (tt, 1), 0)
        pos_next = (tile_in_seq + 1) * tt + lax.broadcasted_iota(jnp.int32, (HALO, 1), 0)

        for g, w in enumerate(POOL_WINDOWS):
            sl = slice(g * GROUP, (g + 1) * GROUP)
            inv_cnt = _inv_count(pos, w)
            pwb = pw_ref[g].astype(BF16)
            pooledb = _pool_fwd(proj_ref[:, sl], xp_prev[:, sl], inv_cnt, w).astype(BF16)
            pre = _dot_nn(pooledb, pwb)
            dpo = dmix_scr[:, sl]
            dvec_ref[ROW_POOL_SCALE:ROW_POOL_SCALE + 1, sl] += _rowsum(dpo * pre)
            dsb = (dpo * ps_ref[:, sl]).astype(BF16)
            dmat_ref[g] += _dot_tn(pooledb, dsb)
            dpooled = _dot_nt(dsb, pwb)
            dpooled_next = _dot_nt((dpo_next[:, sl] * ps_ref[:, sl]).astype(BF16), pwb)
            dxp = _pool_bwd(dpooled, dpooled_next, inv_cnt, _inv_count(pos_next, w), w)
            dproj_ref[:, sl] = dxp.astype(BF16)

        u_scr[...] = _gelu(proj_ref[:, POOL_WIDTH:POOL_WIDTH + SGU_WIDTH])
        vhat, rstd_v = _ln_stats(_gelu(proj_ref[:, POOL_WIDTH + SGU_WIDTH:]))
        vhat_scr[...] = vhat
        vln_scr[...] = vhat * lg_ref[...] + lb_ref[...]
        row = lax.broadcasted_iota(jnp.int32, (GROUP, GROUP), 0)
        col = lax.broadcasted_iota(jnp.int32, (GROUP, GROUP), 1)
        for h in range(N_HEADS):
            ws = _masked_sgu_w(sw_ref, h).astype(BF16)
            vh = _to_head_major(vln_scr, h, nc).astype(BF16)
            mixed = _dot_nn(ws, vh) + sb_ref[:, h:h + 1]
            dsg_h = _to_head_major(dmix_scr, h, nc, POOL_WIDTH)
            du_h = dsg_h * mixed
            dm_h = dsg_h * _to_head_major(u_scr, h, nc)
            pos_sums = lax.dot_general(jnp.ones((8, nc * GROUP), F32), dm_h, (((1,), (1,)), ((), ())),
                                       precision=lax.Precision.HIGH, preferred_element_type=F32)
            dvec_ref[ROW_SGU_B + h:ROW_SGU_B + h + 1, 0:GROUP] += pos_sums[0:1, :]
            dmb = dm_h.astype(BF16)
            dmat_ref[len(POOL_WINDOWS) + h] += jnp.where(row >= col, _dot_nt(dmb, vh), 0.0)
            dv_h = _dot_tn(ws, dmb)
            for c in range(nc):
                rs = slice(c * GROUP, (c + 1) * GROUP)
                cs = slice(h * GROUP, (h + 1) * GROUP)
                du_scr[rs, cs] = du_h[:, c * GROUP:(c + 1) * GROUP]
                dv_scr[rs, cs] = dv_h[:, c * GROUP:(c + 1) * GROUP]
        dv_ln = dv_scr[...]
        vhat = vhat_scr[...]
        dvec_ref[ROW_SLN_B:ROW_SLN_B + 1, 0:SGU_WIDTH] += _rowsum(dv_ln)
        dvec_ref[ROW_SLN_G:ROW_SLN_G + 1, 0:SGU_WIDTH] += _rowsum(dv_ln * vhat)
        dv = _ln_bwd(dv_ln, vhat, rstd_v, lg_ref[...])
        dproj_ref[:, POOL_WIDTH:POOL_WIDTH + SGU_WIDTH] = (
            du_scr[...] * _gelu_grad(proj_ref[:, POOL_WIDTH:POOL_WIDTH + SGU_WIDTH])).astype(BF16)
        dproj_ref[:, POOL_WIDTH + SGU_WIDTH:] = (dv * _gelu_grad(proj_ref[:, POOL_WIDTH + SGU_WIDTH:])).astype(BF16)
        gx_ref[...] = ALPHA * dr1_ref[...] + _dot_nn(dproj_ref[...], win_ref[...])

    def tile(cols):
        return pl.BlockSpec((tt, cols), lambda i: (i, 0))

    def whole(a):
        nd = a.ndim
        return pl.BlockSpec(a.shape, lambda i: (0,) * nd)

    def resident(shape):
        nd = len(shape)
        return pl.BlockSpec(shape, lambda i: (0,) * nd)

    next_halo = pl.BlockSpec((HALO, D_MODEL), lambda i: (jnp.minimum((i + 1) * (tt // HALO), n_halo_blocks - 1), 0))
    prev_halo = pl.BlockSpec((HALO, POOL_WIDTH), lambda i: (jnp.maximum(i * (tt // HALO) - 1, 0), 0))
    consts = [win_t, wout, pool_w, pool_scale, sln_g, sln_b, sgu_w, sgu_b_t, stats]
    small_shapes = [(len(POOL_WINDOWS) + N_HEADS, GROUP, GROUP), (VEC_ROWS, D_MODEL)]
    return pl.pallas_call(
        body, name="mixer_bwd", grid=(n_tiles,),
        in_specs=[tile(D_MODEL), next_halo, tile(IN_COLS), prev_halo] + [whole(a) for a in consts],
        out_specs=[tile(D_MODEL), tile(IN_COLS)] + [resident(s) for s in small_shapes],
        out_shape=[jax.ShapeDtypeStruct((tokens, D_MODEL), F32), jax.ShapeDtypeStruct((tokens, IN_COLS), BF16)]
                  + [jax.ShapeDtypeStruct(s, F32) for s in small_shapes],
        scratch_shapes=[pltpu.VMEM((tt, SGU_WIDTH), F32), pltpu.VMEM((tt, SGU_WIDTH), F32),
                        pltpu.VMEM((tt, D_MODEL), F32), pltpu.VMEM((tt, SGU_WIDTH), F32),
                        pltpu.VMEM((tt, SGU_WIDTH), F32), pltpu.VMEM((tt, SGU_WIDTH), F32)],
        compiler_params=pltpu.CompilerParams(dimension_semantics=("arbitrary",), vmem_limit_bytes=VMEM_LIMIT_BYTES),
    )(dr1, dr1, proj, proj, *consts)


def _owner_copies(src_ref, land_ref, sems):
    x, y, c = _mesh_position()
    return [pltpu.make_async_remote_copy(
        src_ref=src_ref.at[2 * cx + cy], dst_ref=land_ref.at[j], send_sem=sems[j], recv_sem=sems[3 + j],
        device_id=(cx, cy, c), device_id_type=MESH) for j, (cx, cy) in enumerate(_other_chips(x, y))]


def _send_to_owners_start(chip_partial, name, collective_id):
    land = _received_shape(chip_partial)

    def body(src_ref, land_ref, *rest):
        x, y, c = _mesh_position()
        barrier = pltpu.get_barrier_semaphore()
        for cx, cy in _other_chips(x, y):
            pl.semaphore_signal(barrier, inc=1, device_id=(cx, cy, c), device_id_type=MESH)
        pl.semaphore_wait(barrier, 3)
        for cp in _owner_copies(src_ref, land_ref, rest[:6]):
            cp.start()
        rest[8][...] = jnp.zeros_like(rest[8])

    hbm = pl.BlockSpec(memory_space=pltpu.HBM)
    sem = pl.BlockSpec(memory_space=pltpu.SEMAPHORE)
    outs = pl.pallas_call(
        body, name=name,
        out_shape=[pltpu.SemaphoreType.DMA(())] * 6 + [pltpu.HBM(chip_partial.shape, chip_partial.dtype),
                                                       pltpu.HBM(land.shape, land.dtype),
                                                       jax.ShapeDtypeStruct((8, GROUP), F32)],
        in_specs=[hbm, hbm], out_specs=[sem] * 6 + [hbm, hbm, pl.BlockSpec(memory_space=pltpu.VMEM)],
        input_output_aliases={0: 6, 1: 7},
        compiler_params=pltpu.CompilerParams(has_side_effects=pltpu.SideEffectType.DATAFLOW_SIDE_EFFECTING,
                                             collective_id=collective_id),
    )(pltpu.with_memory_space_constraint(chip_partial, pltpu.HBM),
      pltpu.with_memory_space_constraint(lax.empty(land.shape, land.dtype), pltpu.HBM))
    return outs[:6], outs[6], outs[7], outs[8]


def _send_to_owners_wait(sems, src_thru, land_thru, after, name):
    def body(src_ref, land_ref, *rest):
        for cp in _owner_copies(src_ref, land_ref, rest[:6]):
            cp.wait_send()
            cp.wait_recv()

    hbm = pl.BlockSpec(memory_space=pltpu.HBM)
    sem = pl.BlockSpec(memory_space=pltpu.SEMAPHORE)
    return pl.pallas_call(
        body, name=name,
        out_shape=[pltpu.HBM(src_thru.shape, src_thru.dtype), pltpu.HBM(land_thru.shape, land_thru.dtype)],
        in_specs=[hbm, hbm] + [sem] * 6 + [pl.BlockSpec(memory_space=pl.ANY)], out_specs=[hbm, hbm],
        input_output_aliases={0: 0, 1: 1},
        compiler_params=pltpu.CompilerParams(has_side_effects=pltpu.SideEffectType.DATAFLOW_SIDE_EFFECTING),
    )(src_thru, land_thru, *sems, after)[1]


def _adamw_math(w, g, m, v):
    m = ADAM_B1 * m + (1.0 - ADAM_B1) * g
    v = ADAM_B2 * v + (1.0 - ADAM_B2) * (g * g)
    m_hat = m / (1.0 - ADAM_B1 ** ADAM_STEP)
    v_hat = v / (1.0 - ADAM_B2 ** ADAM_STEP)
    delta = -ADAM_LR * (m_hat / (jnp.sqrt(v_hat) + ADAM_EPS) + ADAM_WD * w)
    return delta, m, v


def _sum_adamw(w, m, v, own, received, name):
    rows, cols = w.shape
    rb = _row_block(rows, 256)

    def body(w_ref, m_ref, v_ref, own_ref, rec_ref, g_ref, d_ref, mo_ref, vo_ref):
        g = own_ref[...]
        for j in range(3):
            g = g + rec_ref[j].astype(F32)
        g_ref[...] = g
        d_ref[...], mo_ref[...], vo_ref[...] = _adamw_math(w_ref[...], g, m_ref[...], v_ref[...])

    spec = pl.BlockSpec((rb, cols), lambda r: (r, 0))
    return pl.pallas_call(
        body, name=name, grid=(rows // rb,),
        in_specs=[spec] * 4 + [pl.BlockSpec((3, rb, cols), lambda r: (0, r, 0))],
        out_specs=[spec] * 4, out_shape=[jax.ShapeDtypeStruct((rows, cols), F32)] * 4,
        compiler_params=pltpu.CompilerParams(dimension_semantics=("arbitrary",)),
    )(w, m, v, own, received)


SMALL_NAMES = ("pool_w", "sgu_w", "pool_scale", "sgu_ln_g", "sgu_ln_b", "sgu_b", "ln1_g", "ln1_b", "ln2_g", "ln2_b")
_SMALL_VEC_ROWS = {"pool_scale": (ROW_POOL_SCALE, POOL_WIDTH), "sgu_ln_g": (ROW_SLN_G, SGU_WIDTH),
                   "sgu_ln_b": (ROW_SLN_B, SGU_WIDTH), "ln1_g": (ROW_LN1_G, D_MODEL), "ln1_b": (ROW_LN1_B, D_MODEL),
                   "ln2_g": (ROW_LN2_G, D_MODEL), "ln2_b": (ROW_LN2_B, D_MODEL)}
_SMALL_MAT_FIRST = {"pool_w": 0, "sgu_w": len(POOL_WINDOWS)}


def _small_sum_adamw(mats_all, vecs_all, w, m, v):
    n = len(SMALL_NAMES)

    def body(mats_ref, vecs_ref, *refs):
        w_refs, m_refs, v_refs = refs[:n], refs[n:2 * n], refs[2 * n:3 * n]
        loss_ref = refs[3 * n]
        g_refs, d_refs, mo_refs, vo_refs = (refs[3 * n + 1 + k * n:3 * n + 1 + (k + 1) * n] for k in range(4))
        vec_scr = refs[7 * n + 1]

        def update(k, idx, g):
            d, mo, vo = _adamw_math(w_refs[k][idx], g, m_refs[k][idx], v_refs[k][idx])
            g_refs[k][idx], d_refs[k][idx], mo_refs[k][idx], vo_refs[k][idx] = g, d, mo, vo

        total = vecs_ref[0]
        for dev in range(1, N_DEV):
            total = total + vecs_ref[dev]
        vec_scr[...] = total
        for k, name in enumerate(SMALL_NAMES):
            if name in _SMALL_MAT_FIRST:
                for b in range(4):
                    g = mats_ref[0, _SMALL_MAT_FIRST[name] + b]
                    for dev in range(1, N_DEV):
                        g = g + mats_ref[dev, _SMALL_MAT_FIRST[name] + b]
                    update(k, (0, b), g)
            elif name == "sgu_b":
                update(k, (0,), vec_scr[ROW_SGU_B:ROW_SGU_B + N_HEADS, 0:GROUP])
            else:
                row, width = _SMALL_VEC_ROWS[name]
                update(k, (slice(None), slice(None)), vec_scr[row:row + 1, 0:width])
        loss = jnp.sum(vec_scr[ROW_LOSS:ROW_LOSS + 1, :], axis=1, keepdims=True) * (0.5 / D_MODEL)
        loss_ref[...] = jnp.broadcast_to(loss, loss_ref.shape)

    vmem = pl.BlockSpec(memory_space=pltpu.VMEM)
    shapes = [jax.ShapeDtypeStruct(w[k].shape, F32) for k in SMALL_NAMES]
    outs = pl.pallas_call(
        body, name="small_sum_adamw",
        out_shape=[jax.ShapeDtypeStruct((8, GROUP), F32)] + shapes * 4,
        in_specs=[vmem] * (2 + 3 * n), out_specs=[vmem] * (1 + 4 * n),
        scratch_shapes=[pltpu.VMEM((VEC_ROWS, D_MODEL), F32)],
    )(mats_all, vecs_all, *[w[k] for k in SMALL_NAMES], *[m[k] for k in SMALL_NAMES], *[v[k] for k in SMALL_NAMES])
    return outs[0], outs[1:1 + n], outs[1 + n:1 + 2 * n], outs[1 + 2 * n:1 + 3 * n], outs[1 + 3 * n:]


def kernel(x, w_in, pool_w, pool_scale, sgu_ln_g, sgu_ln_b, sgu_w, sgu_b, w_out, ln1_g, ln1_b, w_gate_up, w_down, ln2_g, ln2_b, loss_target, m_w_in, m_pool_w, m_pool_scale, m_sgu_ln_g, m_sgu_ln_b, m_sgu_w, m_sgu_b, m_w_out, m_ln1_g, m_ln1_b, m_w_gate_up, m_w_down, m_ln2_g, m_ln2_b, v_w_in, v_pool_w, v_pool_scale, v_sgu_ln_g, v_sgu_ln_b, v_sgu_w, v_sgu_b, v_w_out, v_ln1_g, v_ln1_b, v_w_gate_up, v_w_down, v_ln2_g, v_ln2_b):
    bl, seq, _ = x.shape
    tokens = bl * seq
    x2d = x.reshape(tokens, D_MODEL)
    tgt2d = loss_target.reshape(tokens, D_MODEL)

    pool_w3, sgu_w3 = pool_w[0], sgu_w[0]
    sgu_b_t = sgu_b[0].T
    proj, xhat1, rstd1, mix_bf, x_bf, win_t, wout, wgu_t, wdown = _mixer_fwd(
        x2d, [w_in[0].T, w_out[0]], pool_w3, pool_scale, sgu_ln_g, sgu_ln_b, sgu_w3, sgu_b_t, ln1_g, ln1_b,
        [w_gate_up[0].T, w_down[0]], seq)
    dr1, acts_bf, grads_bf, stats = _ffn_fwd_bwd(xhat1, rstd1, tgt2d, wgu_t, wdown, ln1_g, ln1_b, ln2_g, ln2_b)

    bf_gu, own_gu = _wgrad_exchange(grads_bf, acts_bf, 1, "wgrad_gate_up", collective_id=1,
                                    lhs_cols=(0, 2 * D_FF), rhs_block=0)
    sent_gu = _send_to_owners_start(bf_gu, "grad_scatter_w_gate_up_start", collective_id=4)
    bf_dn, own_dn = _wgrad_exchange(grads_bf, acts_bf, 2, "wgrad_down", after=sent_gu[3], collective_id=2,
                                    lhs_cols=(2 * D_FF, D_FF), rhs_block=1)
    sent_dn = _send_to_owners_start(bf_dn, "grad_scatter_w_down_start", collective_id=5)
    bf_out, own_out = _wgrad_exchange(mix_bf, acts_bf, N_CHIPS, "wgrad_out", after=sent_dn[3], collective_id=3,
                                      rhs_block=2)
    sent_out = _send_to_owners_start(bf_out, "grad_scatter_w_out_start", collective_id=6)
    gx, dproj_bf, d_mats, d_vecs = _mixer_bwd(
        dr1, proj, win_t, wout, pool_w3, pool_scale, sgu_ln_g, sgu_ln_b, sgu_w3, sgu_b_t, stats, seq)
    bf_in, own_in, mats_all, vecs_all = _wgrad_exchange(
        dproj_bf, x_bf, N_CHIPS, "wgrad_in", gather_rows=[d_mats.reshape(-1, GROUP), d_vecs], after=sent_out[3],
        collective_id=10)
    sent_in = _send_to_owners_start(bf_in, "grad_scatter_w_in_start", collective_id=7)

    grads, deltas, new_m, new_v = {}, {}, {}, {}
    after = sent_in[3]
    for nm, w, m, v, own, sent, transposed in (("w_gate_up", w_gate_up, m_w_gate_up, v_w_gate_up, own_gu, sent_gu, True),
                                               ("w_down", w_down, m_w_down, v_w_down, own_dn, sent_dn, False),
                                               ("w_out", w_out, m_w_out, v_w_out, own_out, sent_out, False),
                                               ("w_in", w_in, m_w_in, v_w_in, own_in, sent_in, True)):
        rows = (lambda a: a[0].T) if transposed else (lambda a: a[0])
        back = (lambda a: a.T[None]) if transposed else (lambda a: a[None])
        rec = _send_to_owners_wait(*sent[:3], after, "grad_scatter_" + nm + "_wait")
        g, d, mo, vo = _sum_adamw(rows(w), rows(m), rows(v), own, rec, "adamw_" + nm)
        after = vo
        grads[nm], deltas[nm], new_m[nm], new_v[nm] = back(g), back(d), back(mo), back(vo)

    small_w = {"pool_w": pool_w, "pool_scale": pool_scale, "sgu_ln_g": sgu_ln_g, "sgu_ln_b": sgu_ln_b, "sgu_w": sgu_w,
               "sgu_b": sgu_b, "ln1_g": ln1_g, "ln1_b": ln1_b, "ln2_g": ln2_g, "ln2_b": ln2_b}
    small_m = {"pool_w": m_pool_w, "pool_scale": m_pool_scale, "sgu_ln_g": m_sgu_ln_g, "sgu_ln_b": m_sgu_ln_b,
               "sgu_w": m_sgu_w, "sgu_b": m_sgu_b, "ln1_g": m_ln1_g, "ln1_b": m_ln1_b, "ln2_g": m_ln2_g, "ln2_b": m_ln2_b}
    small_v = {"pool_w": v_pool_w, "pool_scale": v_pool_scale, "sgu_ln_g": v_sgu_ln_g, "sgu_ln_b": v_sgu_ln_b,
               "sgu_w": v_sgu_w, "sgu_b": v_sgu_b, "ln1_g": v_ln1_g, "ln1_b": v_ln1_b, "ln2_g": v_ln2_g, "ln2_b": v_ln2_b}
    loss_blk, g_small, d_small, m_small, v_small = _small_sum_adamw(
        mats_all.reshape(N_DEV, 2 * N_HEADS, GROUP, GROUP), vecs_all.reshape(N_DEV, VEC_ROWS, D_MODEL),
        small_w, small_m, small_v)
    for vals, dst in ((g_small, grads), (d_small, deltas), (m_small, new_m), (v_small, new_v)):
        dst.update(zip(SMALL_NAMES, vals))

    order = ["w_in", "pool_w", "pool_scale", "sgu_ln_g", "sgu_ln_b", "sgu_w", "sgu_b", "w_out", "ln1_g", "ln1_b",
             "w_gate_up", "w_down", "ln2_g", "ln2_b"]
    return (loss_blk[0, 0], gx.reshape(bl, seq, D_MODEL), *[grads[k] for k in order], *[deltas[k] for k in order],
            *[new_m[k] for k in order], *[new_v[k] for k in order])
```

```python
import jax
import jax.numpy as jnp
from jax import lax
from jax.experimental import pallas as pl
from jax.experimental.pallas import tpu as pltpu

F32 = jnp.float32
BF16 = jnp.bfloat16
MESH = pl.DeviceIdType.MESH

D_MODEL = 1024
POOL_WIDTH = 512
SGU_WIDTH = 512
POOL_WINDOWS = (2, 4, 8, 16)
GROUP = 128
N_HEADS = 4
IN_COLS = POOL_WIDTH + 2 * SGU_WIDTH
D_FF = 2816
LN_EPS = 1e-5
ALPHA = float(2.0 ** 0.25)
HALO = 16
N_DEV = 8
N_CHIPS = 4

ADAM_LR = 0.001
ADAM_B1 = 0.9
ADAM_B2 = 0.999
ADAM_EPS = 1e-08
ADAM_WD = 0.01
ADAM_STEP = 10

VMEM_LIMIT_BYTES = 56 * 1024 * 1024

_SQRT_HALF = 0.7071067811865476
_INV_SQRT_2PI = 0.3989422804014327


def _dot_nn(a, b):
    return lax.dot_general(a, b, (((1,), (0,)), ((), ())), preferred_element_type=F32)


def _dot_nt(a, b):
    return lax.dot_general(a, b, (((1,), (1,)), ((), ())), preferred_element_type=F32)


def _dot_tn(a, b):
    return lax.dot_general(a, b, (((0,), (0,)), ((), ())), preferred_element_type=F32)


def _gelu(x):
    return 0.5 * x * (1.0 + lax.erf(x * _SQRT_HALF))


def _gelu_grad(x):
    return 0.5 * (1.0 + lax.erf(x * _SQRT_HALF)) + x * jnp.exp(-0.5 * x * x) * _INV_SQRT_2PI


def _ln_stats(r):
    mu = jnp.mean(r, axis=-1, keepdims=True)
    d = r - mu
    var = jnp.mean(d * d, axis=-1, keepdims=True)
    rstd = lax.rsqrt(var + LN_EPS)
    return d * rstd, rstd


def _ln_bwd(dout, xhat, rstd, g):
    dxh = dout * g
    m1 = jnp.mean(dxh, axis=-1, keepdims=True)
    m2 = jnp.mean(dxh * xhat, axis=-1, keepdims=True)
    return rstd * (dxh - m1 - xhat * m2)


def _rowsum(a):
    return jnp.sum(a, axis=0, keepdims=True)


def _pool_fwd(xp, xp_prev, inv_cnt, w):
    s = jnp.concatenate([xp_prev, xp], axis=0)
    k = 1
    while k < w:
        s = s + pltpu.roll(s, k, 0)
        k *= 2
    return s[HALO:, :] * inv_cnt - xp


def _pool_bwd(dpooled, dpooled_next, inv_cnt, inv_cnt_next, w):
    n = dpooled.shape[0] + HALO
    s = jnp.concatenate([dpooled * inv_cnt, dpooled_next * inv_cnt_next], axis=0)
    k = 1
    while k < w:
        s = s + pltpu.roll(s, n - k, 0)
        k *= 2
    return s[: dpooled.shape[0], :] - dpooled


def _inv_count(pos, w):
    return 1.0 / jnp.minimum(pos + 1, w).astype(F32)


def _to_head_major(a, h, nc, first_col=0):
    lo = first_col + h * GROUP
    return jnp.concatenate([a[c * GROUP:(c + 1) * GROUP, lo:lo + GROUP] for c in range(nc)], axis=1)


def _masked_sgu_w(sw_ref, h):
    row = lax.broadcasted_iota(jnp.int32, (GROUP, GROUP), 0)
    col = lax.broadcasted_iota(jnp.int32, (GROUP, GROUP), 1)
    return jnp.where(row >= col, sw_ref[h], 0.0)


def _row_block(rows, limit):
    return max(b for b in range(16, min(rows, limit) + 1, 16) if rows % b == 0)


def _mesh_position():
    return lax.axis_index("x"), lax.axis_index("y"), lax.axis_index("c")


def _other_chips(x, y):
    return [(1 - x, y), (x, 1 - y), (1 - x, 1 - y)]


class _TwoLevelGather:
    def __init__(self, ins, outs, send_sems, recv_sems, local_sems):
        self.ins, self.outs = ins, outs
        self.send_sems, self.recv_sems, self.local_sems = send_sems, recv_sems, local_sems
        self.na = len(ins)
        x, y, c = _mesh_position()
        self.c = c
        self.me, self.sibling = (x, y, c), (x, y, 1 - c)
        self.chips = _other_chips(x, y)
        self.relay_from = (x + (1 - c) * (1 - 2 * x), y + c * (1 - 2 * y))
        self.relay_to = (x + c * (1 - 2 * x), y + (1 - c) * (1 - 2 * y))

    def handshake(self):
        peers = [self.sibling] + [(*chip, self.c) for chip in self.chips[:2]]
        barrier = pltpu.get_barrier_semaphore()
        for peer in peers:
            pl.semaphore_signal(barrier, inc=1, device_id=peer, device_id_type=MESH)
        pl.semaphore_wait(barrier, len(peers))

    def _rows(self, a, px, py, pc):
        n = self.ins[a].shape[0]
        return self.outs[a].at[pl.ds((4 * px + 2 * py + pc) * n, n), :]

    def _copy(self, a, k, block, to, src=None):
        return pltpu.make_async_remote_copy(
            src_ref=self._rows(a, *block) if src is None else src, dst_ref=self._rows(a, *block),
            send_sem=self.send_sems.at[a * 7 + k], recv_sem=self.recv_sems.at[a * 7 + k],
            device_id=to, device_id_type=MESH)

    def _mine(self, a):
        return pltpu.make_async_copy(self.ins[a], self._rows(a, *self.me), self.local_sems.at[a])

    def start(self, arrays=None):
        arrays = range(self.na) if arrays is None else arrays
        for a in arrays:
            self._mine(a).start()
        for a in arrays:
            self._copy(a, 0, self.me, self.sibling, src=self.ins[a]).start()
            for j, chip in enumerate(self.chips[:2]):
                self._copy(a, 1 + j, self.me, (*chip, self.c), src=self.ins[a]).start()

    def relay(self, a):
        c, block = self.c, (*self.relay_from, self.c)
        self._copy(a, 1 + c, block, self.me).wait_recv()
        self._copy(a, 3, block, (*self.relay_to, c)).start()
        self._copy(a, 4 + c, block, self.sibling).start()

    def pass_on(self, a):
        c = self.c
        self._copy(a, 2 - c, (*self.relay_to, c), self.me).wait_recv()
        self._copy(a, 5 - c, (*self.relay_to, c), self.sibling).start()
        self._copy(a, 3, (*self.chips[2], c), self.me).wait_recv()
        self._copy(a, 6, (*self.chips[2], c), self.sibling).start()

    def finish(self, arrays=None):
        arrays = range(self.na) if arrays is None else arrays
        for a in arrays:
            self._copy(a, 0, self.sibling, self.me).wait_recv()
            for j, chip in enumerate(self.chips):
                self._copy(a, 4 + j, (*chip, 1 - self.c), self.me).wait_recv()
        for a in arrays:
            for k in range(7):
                self._copy(a, k, self.me, self.sibling, src=self.ins[a]).wait_send()
            self._mine(a).wait()

    @staticmethod
    def scratch(na):
        return [pltpu.SemaphoreType.DMA((7 * na,)), pltpu.SemaphoreType.DMA((7 * na,)), pltpu.SemaphoreType.DMA((na,))]


def _gathered_shape(s):
    return jax.ShapeDtypeStruct((N_DEV * s.shape[0], s.shape[1]), s.dtype)


def _gathered_bf16(s):
    return jax.ShapeDtypeStruct((N_DEV * s.shape[0], s.shape[1]), BF16)


def _mixer_fwd(x2d, own_shards, pool_w, pool_scale, sln_g, sln_b, sgu_w, sgu_b_t, ln1_g, ln1_b, later_shards, seq):
    tokens = x2d.shape[0]
    tt = min(512, seq)
    tiles_per_seq = seq // tt
    nc = tt // GROUP
    n_tiles = tokens // tt
    n_later = len(later_shards)
    shards = list(own_shards) + list(later_shards)
    ns = len(shards)

    def body(x_ref, xh_ref, pw_ref, ps_ref, lg_ref, lb_ref, sw_ref, sb_ref, g1_ref, b1_ref, *rest):
        shard_refs, rest = rest[:ns], rest[ns:]
        proj_ref, xhat_ref, rstd_ref, mix_ref, xbf_ref = rest[:5]
        win_out, wout_out = rest[5:7]
        later_refs, rest = rest[7:7 + n_later], rest[7 + n_later:]
        mix_scr, win_ref, wout_ref = rest[:3]
        shard_bf_refs, (send_sems, recv_sems, local_sems, copy_sems) = rest[3:3 + ns], rest[3 + ns:]
        i = pl.program_id(0)
        gather = _TwoLevelGather(shard_bf_refs, [win_ref, wout_ref, *later_refs], send_sems, recv_sems, local_sems)
        exports = [pltpu.make_async_copy(win_ref, win_out, copy_sems.at[0]),
                   pltpu.make_async_copy(wout_ref, wout_out, copy_sems.at[1])]

        @pl.when(i == 0)
        def _():
            gather.handshake()
            for a in range(ns):
                shard_bf_refs[a][...] = shard_refs[a][...].astype(BF16)
            gather.start([0, 1])
            gather.relay(0)
            gather.relay(1)
            gather.start(range(2, ns))
            gather.pass_on(0)
            gather.pass_on(1)
            gather.finish([0, 1])
            for cp in exports:
                cp.start()

        tile_in_seq = i % tiles_per_seq
        x = x_ref[...]
        xb = x.astype(BF16)
        xbf_ref[...] = xb
        proj = _dot_nt(xb, win_ref[...])
        proj_ref[...] = proj
        xp_prev = _dot_nt(xh_ref[...].astype(BF16), win_ref[0:POOL_WIDTH, :])
        xp_prev = jnp.where(tile_in_seq == 0, 0.0, xp_prev)
        pos = tile_in_seq * tt + lax.broadcasted_iota(jnp.int32, (tt, 1), 0)
        for g, w in enumerate(POOL_WINDOWS):
            sl = slice(g * GROUP, (g + 1) * GROUP)
            pooled = _pool_fwd(proj[:, sl], xp_prev[:, sl], _inv_count(pos, w), w)
            pre = _dot_nn(pooled.astype(BF16), pw_ref[g].astype(BF16))
            mix_scr[:, sl] = pre * ps_ref[:, sl]
        u = _gelu(proj[:, POOL_WIDTH:POOL_WIDTH + SGU_WIDTH])
        v = _gelu(proj[:, POOL_WIDTH + SGU_WIDTH:])
        vhat, _ = _ln_stats(v)
        v_ln = vhat * lg_ref[...] + lb_ref[...]
        for h in range(N_HEADS):
            ws = _masked_sgu_w(sw_ref, h).astype(BF16)
            mixed = _dot_nn(ws, _to_head_major(v_ln, h, nc).astype(BF16)) + sb_ref[:, h:h + 1]
            for c in range(nc):
                rs = slice(c * GROUP, (c + 1) * GROUP)
                mix_scr[rs, POOL_WIDTH + h * GROUP:POOL_WIDTH + (h + 1) * GROUP] = (
                    u[rs, h * GROUP:(h + 1) * GROUP] * mixed[:, c * GROUP:(c + 1) * GROUP])
        mixb = mix_scr[...].astype(BF16)
        mix_ref[...] = mixb
        r1 = ALPHA * x + _dot_nn(mixb, wout_ref[...])
        xhat, rstd = _ln_stats(r1)
        xhat_ref[...] = xhat
        rstd_ref[...] = jnp.broadcast_to(rstd, rstd_ref.shape)

        for a in range(2, ns):
            relay_tile = min(n_tiles // 2 + 1 + 2 * (a - 2), n_tiles - 1)

            @pl.when(i == relay_tile)
            def _(a=a):
                gather.relay(a)

            @pl.when(i == max(n_tiles - ns + a, relay_tile))
            def _(a=a):
                gather.pass_on(a)

        @pl.when(i == n_tiles - 1)
        def _():
            gather.finish(range(2, ns))
            for cp in exports:
                cp.wait()

    def tile(cols):
        return pl.BlockSpec((tt, cols), lambda i: (i, 0))

    def whole(a):
        nd = a.ndim
        return pl.BlockSpec(a.shape, lambda i: (0,) * nd)

    any_spec = pl.BlockSpec(memory_space=pl.ANY)
    halo = pl.BlockSpec((HALO, D_MODEL), lambda i: (jnp.maximum(i * (tt // HALO) - 1, 0), 0))
    consts = [pool_w, pool_scale, sln_g, sln_b, sgu_w, sgu_b_t, ln1_g, ln1_b]
    gathered = [_gathered_bf16(s) for s in shards]
    return pl.pallas_call(
        body, name="mixer_fwd", grid=(n_tiles,),
        in_specs=[tile(D_MODEL), halo] + [whole(a) for a in consts] + [whole(s) for s in shards],
        out_specs=[tile(IN_COLS), tile(D_MODEL), tile(GROUP), tile(D_MODEL), tile(D_MODEL)] + [any_spec] * ns,
        out_shape=[jax.ShapeDtypeStruct((tokens, IN_COLS), F32), jax.ShapeDtypeStruct((tokens, D_MODEL), F32),
                   jax.ShapeDtypeStruct((tokens, GROUP), F32), jax.ShapeDtypeStruct((tokens, D_MODEL), BF16),
                   jax.ShapeDtypeStruct((tokens, D_MODEL), BF16)] + gathered,
        scratch_shapes=[pltpu.VMEM((tt, D_MODEL), F32)] + [pltpu.VMEM(g.shape, BF16) for g in gathered[:2]]
                       + [pltpu.VMEM(s.shape, BF16) for s in shards] + _TwoLevelGather.scratch(ns)
                       + [pltpu.SemaphoreType.DMA((2,))],
        compiler_params=pltpu.CompilerParams(dimension_semantics=("arbitrary",), vmem_limit_bytes=VMEM_LIMIT_BYTES,
                                             collective_id=9),
    )(x2d, x2d, *consts, *shards)


def _ffn_fwd_bwd(xhat1, rstd1, target, wgu_t, wdown, ln1_g, ln1_b, ln2_g, ln2_b):
    tokens = xhat1.shape[0]
    tt = min(256, tokens)

    def body(xhat_ref, rstd_ref, tgt_ref, wgu_hbm, wd_hbm, g1_ref, b1_ref, g2_ref, b2_ref,
             dr1_ref, acts_ref, grads_ref, stats_ref, wgu_ref, wd_ref, gu_scr, sems):
        i = pl.program_id(0)

        @pl.when(i == 0)
        def _():
            loads = [pltpu.make_async_copy(wgu_hbm, wgu_ref, sems.at[0]),
                     pltpu.make_async_copy(wd_hbm, wd_ref, sems.at[1])]
            for cp in loads:
                cp.start()
            stats_ref[...] = jnp.zeros_like(stats_ref)
            for cp in loads:
                cp.wait()

        xhat1_t = xhat_ref[...]
        h = xhat1_t * g1_ref[...] + b1_ref[...]
        hb = h.astype(BF16)
        acts_ref[:, 0:D_MODEL] = hb
        gate = _dot_nt(hb, wgu_ref[0:D_FF, :])
        up = _dot_nt(hb, wgu_ref[D_FF:, :])
        gu_scr[:, 0:D_FF] = gate
        gu_scr[:, D_FF:] = up
        ab = (gate * jax.nn.sigmoid(gate) * up).astype(BF16)
        grads_ref[:, 2 * D_FF:] = ab
        xhat2, rstd2 = _ln_stats(ALPHA * h + _dot_nn(ab, wd_ref[...]))
        err = xhat2 * g2_ref[...] + b2_ref[...] - tgt_ref[...]
        dy = err * (1.0 / D_MODEL)
        stats_ref[0:1, :] += _rowsum(dy * xhat2)
        stats_ref[1:2, :] += _rowsum(dy)
        stats_ref[4:5, :] += _rowsum(err * err)
        dr2 = _ln_bwd(dy, xhat2, rstd2, g2_ref[...])
        dr2b = dr2.astype(BF16)
        acts_ref[:, D_MODEL:2 * D_MODEL] = dr2b
        da = _dot_nt(dr2b, wd_ref[...])
        gate = gu_scr[:, 0:D_FF]
        up = gu_scr[:, D_FF:]
        sg = jax.nn.sigmoid(gate)
        dgate = (da * up * (sg * (1.0 + gate * (1.0 - sg)))).astype(BF16)
        dup = (da * (gate * sg)).astype(BF16)
        grads_ref[:, 0:D_FF] = dgate
        grads_ref[:, D_FF:2 * D_FF] = dup
        dh = ALPHA * dr2 + _dot_nn(dgate, wgu_ref[0:D_FF, :]) + _dot_nn(dup, wgu_ref[D_FF:, :])
        stats_ref[2:3, :] += _rowsum(dh * xhat1_t)
        stats_ref[3:4, :] += _rowsum(dh)
        dr1 = _ln_bwd(dh, xhat1_t, rstd_ref[:, 0:1], g1_ref[...])
        dr1_ref[...] = dr1
        acts_ref[:, 2 * D_MODEL:] = dr1.astype(BF16)

    def tile(cols):
        return pl.BlockSpec((tt, cols), lambda i: (i, 0))

    def whole(a):
        nd = a.ndim
        return pl.BlockSpec(a.shape, lambda i: (0,) * nd)

    any_spec = pl.BlockSpec(memory_space=pl.ANY)
    vecs = [ln1_g, ln1_b, ln2_g, ln2_b]
    return pl.pallas_call(
        body, name="ffn_fwd_bwd", grid=(tokens // tt,),
        in_specs=[tile(D_MODEL), tile(GROUP), tile(D_MODEL), any_spec, any_spec] + [whole(a) for a in vecs],
        out_specs=[tile(D_MODEL), tile(3 * D_MODEL), tile(3 * D_FF), pl.BlockSpec((8, D_MODEL), lambda i: (0, 0))],
        out_shape=[jax.ShapeDtypeStruct((tokens, D_MODEL), F32), jax.ShapeDtypeStruct((tokens, 3 * D_MODEL), BF16),
                   jax.ShapeDtypeStruct((tokens, 3 * D_FF), BF16), jax.ShapeDtypeStruct((8, D_MODEL), F32)],
        scratch_shapes=[pltpu.VMEM(wgu_t.shape, BF16), pltpu.VMEM(wdown.shape, BF16),
                        pltpu.VMEM((tt, 2 * D_FF), F32), pltpu.SemaphoreType.DMA((2,))],
        compiler_params=pltpu.CompilerParams(dimension_semantics=("arbitrary",), vmem_limit_bytes=VMEM_LIMIT_BYTES),
    )(xhat1, rstd1, target, wgu_t, wdown, *vecs)


def _wgrad_exchange(lhs, rhs, chips_per_block, name, gather_rows=(), after=None, collective_id=None,
                    lhs_cols=None, rhs_block=None, token_tile=2048):
    tokens = lhs.shape[0]
    first_col, n_all = (0, lhs.shape[1]) if lhs_cols is None else lhs_cols
    m = rhs.shape[1] if rhs_block is None else D_MODEL
    n = n_all // N_DEV
    tw = min(token_tile, tokens)
    nt = tokens // tw
    cpb = chips_per_block
    nj = N_CHIPS // cpb
    lhs_block0 = first_col // (2 * n * cpb)
    assert lhs_block0 * 2 * n * cpb == first_col
    rhs_col = 0 if rhs_block is None else rhs_block
    ng = len(gather_rows)
    anchors = [] if after is None else [after]

    def body(l_ref, r_ref, *rest):
        small_refs, rest = rest[:ng], rest[ng + len(anchors):]
        bf_ref, own_ref = rest[:2]
        gathered_refs, rest = rest[2:2 + ng], rest[2 + ng:]
        acc, kept, landed, send_sems, recv_sems = rest[:5]
        sendbuf = bf_ref
        j, t = pl.program_id(0), pl.program_id(1)
        first, last = (j == 0) & (t == 0), (j == nj - 1) & (t == nt - 1)
        x, y, c = _mesh_position()
        if ng:
            gather = _TwoLevelGather(small_refs, gathered_refs, *rest[5:8])

            @pl.when(first)
            def _():
                gather.handshake()
                gather.start()
        else:
            @pl.when(first)
            def _():
                barrier = pltpu.get_barrier_semaphore()
                pl.semaphore_signal(barrier, inc=1, device_id=(x, y, 1 - c), device_id_type=MESH)
                pl.semaphore_wait(barrier, 1)

        def copy(q):
            return pltpu.make_async_remote_copy(
                src_ref=sendbuf.at[q], dst_ref=landed.at[q], send_sem=send_sems.at[q], recv_sem=recv_sems.at[q],
                device_id=(x, y, 1 - c), device_id_type=MESH)

        @pl.when(t == 0)
        def _():
            acc[...] = _dot_tn(l_ref[...], r_ref[...])

        @pl.when(t > 0)
        def _():
            acc[...] += _dot_tn(l_ref[...], r_ref[...])

        @pl.when(t == nt - 1)
        def _():
            for qq in range(cpb):
                q = j * cpb + qq
                kept[q] = acc[pl.ds(pl.multiple_of(qq * 2 * n + c * n, 8), n), :]
                sendbuf[q] = acc[pl.ds(pl.multiple_of(qq * 2 * n + (1 - c) * n, 8), n), :].astype(BF16)
                copy(q).start()

        if ng:
            @pl.when((j == nj - 1) & (t == nt // 2))
            def _():
                for a in range(ng):
                    gather.relay(a)

            @pl.when(last)
            def _():
                for a in range(ng):
                    gather.pass_on(a)
                gather.finish()

        @pl.when(last)
        def _():
            for q in range(N_CHIPS):
                copy(q).wait_send()
                copy(q).wait_recv()
            for q in range(N_CHIPS):
                s = kept[q] + landed[q].astype(F32)
                bf_ref[q] = s.astype(BF16)

                @pl.when(q == 2 * x + y)
                def _(s=s):
                    own_ref[...] = s

    any_spec = pl.BlockSpec(memory_space=pl.ANY)
    by_chip = (N_CHIPS, n, m)
    return pl.pallas_call(
        body, name=name, grid=(nj, nt),
        in_specs=[pl.BlockSpec((tw, 2 * n * cpb), lambda j, t: (t, lhs_block0 + j)),
                  pl.BlockSpec((tw, m), lambda j, t: (t, rhs_col))] + [any_spec] * (ng + len(anchors)),
        out_specs=[pl.BlockSpec(by_chip, lambda j, t: (0, 0, 0)), pl.BlockSpec((n, m), lambda j, t: (0, 0))]
                  + [any_spec] * ng,
        out_shape=[jax.ShapeDtypeStruct(by_chip, BF16), jax.ShapeDtypeStruct((n, m), F32)]
                  + [_gathered_shape(s) for s in gather_rows],
        scratch_shapes=[pltpu.VMEM((2 * n * cpb, m), F32), pltpu.VMEM(by_chip, F32), pltpu.VMEM(by_chip, BF16),
                        pltpu.SemaphoreType.DMA((N_CHIPS,)), pltpu.SemaphoreType.DMA((N_CHIPS,))]
                       + (_TwoLevelGather.scratch(ng) if ng else []),
        compiler_params=pltpu.CompilerParams(dimension_semantics=("arbitrary", "arbitrary"),
                                             vmem_limit_bytes=VMEM_LIMIT_BYTES + 4 * 1024 * 1024,
                                             collective_id=collective_id),
    )(lhs, rhs, *gather_rows, *anchors)


def _received_shape(p):
    return jax.ShapeDtypeStruct((3,) + p.shape[1:], p.dtype)


ROW_POOL_SCALE, ROW_SLN_G, ROW_SLN_B, ROW_SGU_B = 0, 1, 2, 3
ROW_LN2_G, ROW_LN2_B, ROW_LN1_G, ROW_LN1_B, ROW_LOSS = 8, 9, 10, 11, 12
VEC_ROWS = 16


def _mixer_bwd(dr1, proj, win_t, wout, pool_w, pool_scale, sln_g, sln_b, sgu_w, sgu_b_t, stats, seq):
    tokens = dr1.shape[0]
    tt = min(1024, seq)
    tiles_per_seq = seq // tt
    nc = tt // GROUP
    n_halo_blocks = tokens // HALO
    n_tiles = tokens // tt

    def body(dr1_ref, dr1n_ref, proj_ref, projh_ref, win_ref, wout_ref, pw_ref, ps_ref, lg_ref, lb_ref, sw_ref, sb_ref,
             stats_ref, gx_ref, dproj_ref, dmat_ref, dvec_ref, du_scr, dv_scr, dmix_scr, u_scr, vln_scr, vhat_scr):
        i = pl.program_id(0)
        tile_in_seq = i % tiles_per_seq

        @pl.when(i == 0)
        def _():
            dmat_ref[...] = jnp.zeros_like(dmat_ref)
            dvec_ref[0:8, :] = jnp.zeros((8, D_MODEL), F32)
            dvec_ref[8:16, :] = stats_ref[...]

        dmix_scr[...] = _dot_nt(dr1_ref[...].astype(BF16), wout_ref[...])
        dpo_next = _dot_nt(dr1n_ref[...].astype(BF16), wout_ref[0:POOL_WIDTH, :])
        dpo_next = jnp.where(tile_in_seq == tiles_per_seq - 1, 0.0, dpo_next)
        xp_prev = jnp.where(tile_in_seq == 0, 0.0, projh_ref[...])
        pos = tile_in_seq * tt + lax.broadcasted_iota(jnp.int32, (tt, 1), 0)
        pos_next = (tile_in_seq + 1) * tt + lax.broadcasted_iota(jnp.int32, (HALO, 1), 0)

        for g, w in enumerate(POOL_WINDOWS):
            sl = slice(g * GROUP, (g + 1) * GROUP)
            inv_cnt = _inv_count(pos, w)
            pwb = pw_ref[g].astype(BF16)
            pooledb = _pool_fwd(proj_ref[:, sl], xp_prev[:, sl], inv_cnt, w).astype(BF16)
            pre = _dot_nn(pooledb, pwb)
            dpo = dmix_scr[:, sl]
            dvec_ref[ROW_POOL_SCALE:ROW_POOL_SCALE + 1, sl] += _rowsum(dpo * pre)
            dsb = (dpo * ps_ref[:, sl]).astype(BF16)
            dmat_ref[g] += _dot_tn(pooledb, dsb)
            dpooled = _dot_nt(dsb, pwb)
            dpooled_next = _dot_nt((dpo_next[:, sl] * ps_ref[:, sl]).astype(BF16), pwb)
            dxp = _pool_bwd(dpooled, dpooled_next, inv_cnt, _inv_count(pos_next, w), w)
            dproj_ref[:, sl] = dxp.astype(BF16)

        u_scr[...] = _gelu(proj_ref[:, POOL_WIDTH:POOL_WIDTH + SGU_WIDTH])
        vhat, rstd_v = _ln_stats(_gelu(proj_ref[:, POOL_WIDTH + SGU_WIDTH:]))
        vhat_scr[...] = vhat
        vln_scr[...] = vhat * lg_ref[...] + lb_ref[...]
        row = lax.broadcasted_iota(jnp.int32, (GROUP, GROUP), 0)
        col = lax.broadcasted_iota(jnp.int32, (GROUP, GROUP), 1)
        for h in range(N_HEADS):
            ws = _masked_sgu_w(sw_ref, h).astype(BF16)
            vh = _to_head_major(vln_scr, h, nc).astype(BF16)
            mixed = _dot_nn(ws, vh) + sb_ref[:, h:h + 1]
            dsg_h = _to_head_major(dmix_scr, h, nc, POOL_WIDTH)
            du_h = dsg_h * mixed
            dm_h = dsg_h * _to_head_major(u_scr, h, nc)
            pos_sums = lax.dot_general(jnp.ones((8, nc * GROUP), F32), dm_h, (((1,), (1,)), ((), ())),
                                       precision=lax.Precision.HIGH, preferred_element_type=F32)
            dvec_ref[ROW_SGU_B + h:ROW_SGU_B + h + 1, 0:GROUP] += pos_sums[0:1, :]
            dmb = dm_h.astype(BF16)
            dmat_ref[len(POOL_WINDOWS) + h] += jnp.where(row >= col, _dot_nt(dmb, vh), 0.0)
            dv_h = _dot_tn(ws, dmb)
            for c in range(nc):
                rs = slice(c * GROUP, (c + 1) * GROUP)
                cs = slice(h * GROUP, (h + 1) * GROUP)
                du_scr[rs, cs] = du_h[:, c * GROUP:(c + 1) * GROUP]
                dv_scr[rs, cs] = dv_h[:, c * GROUP:(c + 1) * GROUP]
        dv_ln = dv_scr[...]
        vhat = vhat_scr[...]
        dvec_ref[ROW_SLN_B:ROW_SLN_B + 1, 0:SGU_WIDTH] += _rowsum(dv_ln)
        dvec_ref[ROW_SLN_G:ROW_SLN_G + 1, 0:SGU_WIDTH] += _rowsum(dv_ln * vhat)
        dv = _ln_bwd(dv_ln, vhat, rstd_v, lg_ref[...])
        dproj_ref[:, POOL_WIDTH:POOL_WIDTH + SGU_WIDTH] = (
            du_scr[...] * _gelu_grad(proj_ref[:, POOL_WIDTH:POOL_WIDTH + SGU_WIDTH])).astype(BF16)
        dproj_ref[:, POOL_WIDTH + SGU_WIDTH:] = (dv * _gelu_grad(proj_ref[:, POOL_WIDTH + SGU_WIDTH:])).astype(BF16)
        gx_ref[...] = ALPHA * dr1_ref[...] + _dot_nn(dproj_ref[...], win_ref[...])

    def tile(cols):
        return pl.BlockSpec((tt, cols), lambda i: (i, 0))

    def whole(a):
        nd = a.ndim
        return pl.BlockSpec(a.shape, lambda i: (0,) * nd)

    def resident(shape):
        nd = len(shape)
        return pl.BlockSpec(shape, lambda i: (0,) * nd)

    next_halo = pl.BlockSpec((HALO, D_MODEL), lambda i: (jnp.minimum((i + 1) * (tt // HALO), n_halo_blocks - 1), 0))
    prev_halo = pl.BlockSpec((HALO, POOL_WIDTH), lambda i: (jnp.maximum(i * (tt // HALO) - 1, 0), 0))
    consts = [win_t, wout, pool_w, pool_scale, sln_g, sln_b, sgu_w, sgu_b_t, stats]
    small_shapes = [(len(POOL_WINDOWS) + N_HEADS, GROUP, GROUP), (VEC_ROWS, D_MODEL)]
    return pl.pallas_call(
        body, name="mixer_bwd", grid=(n_tiles,),
        in_specs=[tile(D_MODEL), next_halo, tile(IN_COLS), prev_halo] + [whole(a) for a in consts],
        out_specs=[tile(D_MODEL), tile(IN_COLS)] + [resident(s) for s in small_shapes],
        out_shape=[jax.ShapeDtypeStruct((tokens, D_MODEL), F32), jax.ShapeDtypeStruct((tokens, IN_COLS), BF16)]
                  + [jax.ShapeDtypeStruct(s, F32) for s in small_shapes],
        scratch_shapes=[pltpu.VMEM((tt, SGU_WIDTH), F32), pltpu.VMEM((tt, SGU_WIDTH), F32),
                        pltpu.VMEM((tt, D_MODEL), F32), pltpu.VMEM((tt, SGU_WIDTH), F32),
                        pltpu.VMEM((tt, SGU_WIDTH), F32), pltpu.VMEM((tt, SGU_WIDTH), F32)],
        compiler_params=pltpu.CompilerParams(dimension_semantics=("arbitrary",), vmem_limit_bytes=VMEM_LIMIT_BYTES),
    )(dr1, dr1, proj, proj, *consts)


def _owner_copies(src_ref, land_ref, sems):
    x, y, c = _mesh_position()
    return [pltpu.make_async_remote_copy(
        src_ref=src_ref.at[2 * cx + cy], dst_ref=land_ref.at[j], send_sem=sems[j], recv_sem=sems[3 + j],
        device_id=(cx, cy, c), device_id_type=MESH) for j, (cx, cy) in enumerate(_other_chips(x, y))]


def _send_to_owners_start(chip_partial, name, collective_id):
    land = _received_shape(chip_partial)

    def body(src_ref, land_ref, *rest):
        x, y, c = _mesh_position()
        barrier = pltpu.get_barrier_semaphore()
        for cx, cy in _other_chips(x, y):
            pl.semaphore_signal(barrier, inc=1, device_id=(cx, cy, c), device_id_type=MESH)
        pl.semaphore_wait(barrier, 3)
        for cp in _owner_copies(src_ref, land_ref, rest[:6]):
            cp.start()
        rest[8][...] = jnp.zeros_like(rest[8])

    hbm = pl.BlockSpec(memory_space=pltpu.HBM)
    sem = pl.BlockSpec(memory_space=pltpu.SEMAPHORE)
    outs = pl.pallas_call(
        body, name=name,
        out_shape=[pltpu.SemaphoreType.DMA(())] * 6 + [pltpu.HBM(chip_partial.shape, chip_partial.dtype),
                                                       pltpu.HBM(land.shape, land.dtype),
                                                       jax.ShapeDtypeStruct((8, GROUP), F32)],
        in_specs=[hbm, hbm], out_specs=[sem] * 6 + [hbm, hbm, pl.BlockSpec(memory_space=pltpu.VMEM)],
        input_output_aliases={0: 6, 1: 7},
        compiler_params=pltpu.CompilerParams(has_side_effects=pltpu.SideEffectType.DATAFLOW_SIDE_EFFECTING,
                                             collective_id=collective_id),
    )(pltpu.with_memory_space_constraint(chip_partial, pltpu.HBM),
      pltpu.with_memory_space_constraint(lax.empty(land.shape, land.dtype), pltpu.HBM))
    return outs[:6], outs[6], outs[7], outs[8]


def _send_to_owners_wait(sems, src_thru, land_thru, after, name):
    def body(src_ref, land_ref, *rest):
        for cp in _owner_copies(src_ref, land_ref, rest[:6]):
            cp.wait_send()
            cp.wait_recv()

    hbm = pl.BlockSpec(memory_space=pltpu.HBM)
    sem = pl.BlockSpec(memory_space=pltpu.SEMAPHORE)
    return pl.pallas_call(
        body, name=name,
        out_shape=[pltpu.HBM(src_thru.shape, src_thru.dtype), pltpu.HBM(land_thru.shape, land_thru.dtype)],
        in_specs=[hbm, hbm] + [sem] * 6 + [pl.BlockSpec(memory_space=pl.ANY)], out_specs=[hbm, hbm],
        input_output_aliases={0: 0, 1: 1},
        compiler_params=pltpu.CompilerParams(has_side_effects=pltpu.SideEffectType.DATAFLOW_SIDE_EFFECTING),
    )(src_thru, land_thru, *sems, after)[1]


def _adamw_math(w, g, m, v):
    m = ADAM_B1 * m + (1.0 - ADAM_B1) * g
    v = ADAM_B2 * v + (1.0 - ADAM_B2) * (g * g)
    m_hat = m / (1.0 - ADAM_B1 ** ADAM_STEP)
    v_hat = v / (1.0 - ADAM_B2 ** ADAM_STEP)
    delta = -ADAM_LR * (m_hat / (jnp.sqrt(v_hat) + ADAM_EPS) + ADAM_WD * w)
    return delta, m, v


def _sum_adamw(w, m, v, own, received, name):
    rows, cols = w.shape
    rb = _row_block(rows, 256)

    def body(w_ref, m_ref, v_ref, own_ref, rec_ref, g_ref, d_ref, mo_ref, vo_ref):
        g = own_ref[...]
        for j in range(3):
            g = g + rec_ref[j].astype(F32)
        g_ref[...] = g
        d_ref[...], mo_ref[...], vo_ref[...] = _adamw_math(w_ref[...], g, m_ref[...], v_ref[...])

    spec = pl.BlockSpec((rb, cols), lambda r: (r, 0))
    return pl.pallas_call(
        body, name=name, grid=(rows // rb,),
        in_specs=[spec] * 4 + [pl.BlockSpec((3, rb, cols), lambda r: (0, r, 0))],
        out_specs=[spec] * 4, out_shape=[jax.ShapeDtypeStruct((rows, cols), F32)] * 4,
        compiler_params=pltpu.CompilerParams(dimension_semantics=("arbitrary",)),
    )(w, m, v, own, received)


SMALL_NAMES = ("pool_w", "sgu_w", "pool_scale", "sgu_ln_g", "sgu_ln_b", "sgu_b", "ln1_g", "ln1_b", "ln2_g", "ln2_b")
_SMALL_VEC_ROWS = {"pool_scale": (ROW_POOL_SCALE, POOL_WIDTH), "sgu_ln_g": (ROW_SLN_G, SGU_WIDTH),
                   "sgu_ln_b": (ROW_SLN_B, SGU_WIDTH), "ln1_g": (ROW_LN1_G, D_MODEL), "ln1_b": (ROW_LN1_B, D_MODEL),
                   "ln2_g": (ROW_LN2_G, D_MODEL), "ln2_b": (ROW_LN2_B, D_MODEL)}
_SMALL_MAT_FIRST = {"pool_w": 0, "sgu_w": len(POOL_WINDOWS)}


def _small_sum_adamw(mats_all, vecs_all, w, m, v):
    n = len(SMALL_NAMES)

    def body(mats_ref, vecs_ref, *refs):
        w_refs, m_refs, v_refs = refs[:n], refs[n:2 * n], refs[2 * n:3 * n]
        loss_ref = refs[3 * n]
        g_refs, d_refs, mo_refs, vo_refs = (refs[3 * n + 1 + k * n:3 * n + 1 + (k + 1) * n] for k in range(4))
        vec_scr = refs[7 * n + 1]

        def update(k, idx, g):
            d, mo, vo = _adamw_math(w_refs[k][idx], g, m_refs[k][idx], v_refs[k][idx])
            g_refs[k][idx], d_refs[k][idx], mo_refs[k][idx], vo_refs[k][idx] = g, d, mo, vo

        total = vecs_ref[0]
        for dev in range(1, N_DEV):
            total = total + vecs_ref[dev]
        vec_scr[...] = total
        for k, name in enumerate(SMALL_NAMES):
            if name in _SMALL_MAT_FIRST:
                for b in range(4):
                    g = mats_ref[0, _SMALL_MAT_FIRST[name] + b]
                    for dev in range(1, N_DEV):
                        g = g + mats_ref[dev, _SMALL_MAT_FIRST[name] + b]
                    update(k, (0, b), g)
            elif name == "sgu_b":
                update(k, (0,), vec_scr[ROW_SGU_B:ROW_SGU_B + N_HEADS, 0:GROUP])
            else:
                row, width = _SMALL_VEC_ROWS[name]
                update(k, (slice(None), slice(None)), vec_scr[row:row + 1, 0:width])
        loss = jnp.sum(vec_scr[ROW_LOSS:ROW_LOSS + 1, :], axis=1, keepdims=True) * (0.5 / D_MODEL)
        loss_ref[...] = jnp.broadcast_to(loss, loss_ref.shape)

    vmem = pl.BlockSpec(memory_space=pltpu.VMEM)
    shapes = [jax.ShapeDtypeStruct(w[k].shape, F32) for k in SMALL_NAMES]
    outs = pl.pallas_call(
        body, name="small_sum_adamw",
        out_shape=[jax.ShapeDtypeStruct((8, GROUP), F32)] + shapes * 4,
        in_specs=[vmem] * (2 + 3 * n), out_specs=[vmem] * (1 + 4 * n),
        scratch_shapes=[pltpu.VMEM((VEC_ROWS, D_MODEL), F32)],
    )(mats_all, vecs_all, *[w[k] for k in SMALL_NAMES], *[m[k] for k in SMALL_NAMES], *[v[k] for k in SMALL_NAMES])
    return outs[0], outs[1:1 + n], outs[1 + n:1 + 2 * n], outs[1 + 2 * n:1 + 3 * n], outs[1 + 3 * n:]


def kernel(x, w_in, pool_w, pool_scale, sgu_ln_g, sgu_ln_b, sgu_w, sgu_b, w_out, ln1_g, ln1_b, w_gate_up, w_down, ln2_g, ln2_b, loss_target, m_w_in, m_pool_w, m_pool_scale, m_sgu_ln_g, m_sgu_ln_b, m_sgu_w, m_sgu_b, m_w_out, m_ln1_g, m_ln1_b, m_w_gate_up, m_w_down, m_ln2_g, m_ln2_b, v_w_in, v_pool_w, v_pool_scale, v_sgu_ln_g, v_sgu_ln_b, v_sgu_w, v_sgu_b, v_w_out, v_ln1_g, v_ln1_b, v_w_gate_up, v_w_down, v_ln2_g, v_ln2_b):
    bl, seq, _ = x.shape
    tokens = bl * seq
    x2d = x.reshape(tokens, D_MODEL)
    tgt2d = loss_target.reshape(tokens, D_MODEL)

    pool_w3, sgu_w3 = pool_w[0], sgu_w[0]
    sgu_b_t = sgu_b[0].T
    proj, xhat1, rstd1, mix_bf, x_bf, win_t, wout, wgu_t, wdown = _mixer_fwd(
        x2d, [w_in[0].T, w_out[0]], pool_w3, pool_scale, sgu_ln_g, sgu_ln_b, sgu_w3, sgu_b_t, ln1_g, ln1_b,
        [w_gate_up[0].T, w_down[0]], seq)
    dr1, acts_bf, grads_bf, stats = _ffn_fwd_bwd(xhat1, rstd1, tgt2d, wgu_t, wdown, ln1_g, ln1_b, ln2_g, ln2_b)

    bf_gu, own_gu = _wgrad_exchange(grads_bf, acts_bf, 1, "wgrad_gate_up", collective_id=1,
                                    lhs_cols=(0, 2 * D_FF), rhs_block=0)
    sent_gu = _send_to_owners_start(bf_gu, "grad_scatter_w_gate_up_start", collective_id=4)
    bf_dn, own_dn = _wgrad_exchange(grads_bf, acts_bf, 2, "wgrad_down", after=sent_gu[3], collective_id=2,
                                    lhs_cols=(2 * D_FF, D_FF), rhs_block=1)
    sent_dn = _send_to_owners_start(bf_dn, "grad_scatter_w_down_start", collective_id=5)
    bf_out, own_out = _wgrad_exchange(mix_bf, acts_bf, N_CHIPS, "wgrad_out", after=sent_dn[3], collective_id=3,
                                      rhs_block=2)
    sent_out = _send_to_owners_start(bf_out, "grad_scatter_w_out_start", collective_id=6)
    gx, dproj_bf, d_mats, d_vecs = _mixer_bwd(
        dr1, proj, win_t, wout, pool_w3, pool_scale, sgu_ln_g, sgu_ln_b, sgu_w3, sgu_b_t, stats, seq)
    bf_in, own_in, mats_all, vecs_all = _wgrad_exchange(
        dproj_bf, x_bf, N_CHIPS, "wgrad_in", gather_rows=[d_mats.reshape(-1, GROUP), d_vecs], after=sent_out[3],
        collective_id=10)
    sent_in = _send_to_owners_start(bf_in, "grad_scatter_w_in_start", collective_id=7)

    grads, deltas, new_m, new_v = {}, {}, {}, {}
    after = sent_in[3]
    for nm, w, m, v, own, sent, transposed in (("w_gate_up", w_gate_up, m_w_gate_up, v_w_gate_up, own_gu, sent_gu, True),
                                               ("w_down", w_down, m_w_down, v_w_down, own_dn, sent_dn, False),
                                               ("w_out", w_out, m_w_out, v_w_out, own_out, sent_out, False),
                                               ("w_in", w_in, m_w_in, v_w_in, own_in, sent_in, True)):
        rows = (lambda a: a[0].T) if transposed else (lambda a: a[0])
        back = (lambda a: a.T[None]) if transposed else (lambda a: a[None])
        rec = _send_to_owners_wait(*sent[:3], after, "grad_scatter_" + nm + "_wait")
        g, d, mo, vo = _sum_adamw(rows(w), rows(m), rows(v), own, rec, "adamw_" + nm)
        after = vo
        grads[nm], deltas[nm], new_m[nm], new_v[nm] = back(g), back(d), back(mo), back(vo)

    small_w = {"pool_w": pool_w, "pool_scale": pool_scale, "sgu_ln_g": sgu_ln_g, "sgu_ln_b": sgu_ln_b, "sgu_w": sgu_w,
               "sgu_b": sgu_b, "ln1_g": ln1_g, "ln1_b": ln1_b, "ln2_g": ln2_g, "ln2_b": ln2_b}
    small_m = {"pool_w": m_pool_w, "pool_scale": m_pool_scale, "sgu_ln_g": m_sgu_ln_g, "sgu_ln_b": m_sgu_ln_b,
               "sgu_w": m_sgu_w, "sgu_b": m_sgu_b, "ln1_g": m_ln1_g, "ln1_b": m_ln1_b, "ln2_g": m_ln2_g, "ln2_b": m_ln2_b}
    small_v = {"pool_w": v_pool_w, "pool_scale": v_pool_scale, "sgu_ln_g": v_sgu_ln_g, "sgu_ln_b": v_sgu_ln_b,
               "sgu_w": v_sgu_w, "sgu_b": v_sgu_b, "ln1_g": v_ln1_g, "ln1_b": v_ln1_b, "ln2_g": v_ln2_g, "ln2_b": v_ln2_b}
    loss_blk, g_small, d_small, m_small, v_small = _small_sum_adamw(
        mats_all.reshape(N_DEV, 2 * N_HEADS, GROUP, GROUP), vecs_all.reshape(N_DEV, VEC_ROWS, D_MODEL),
        small_w, small_m, small_v)
    for vals, dst in ((g_small, grads), (d_small, deltas), (m_small, new_m), (v_small, new_v)):
        dst.update(zip(SMALL_NAMES, vals))

    order = ["w_in", "pool_w", "pool_scale", "sgu_ln_g", "sgu_ln_b", "sgu_w", "sgu_b", "w_out", "ln1_g", "ln1_b",
             "w_gate_up", "w_down", "ln2_g", "ln2_b"]
    return (loss_blk[0, 0], gx.reshape(bl, seq, D_MODEL), *[grads[k] for k in order], *[deltas[k] for k in order],
            *[new_m[k] for k in order], *[new_v[k] for k in order])
```

```python
import jax
import jax.numpy as jnp
from jax import lax
from jax.experimental import pallas as pl
from jax.experimental.pallas import tpu as pltpu

F32 = jnp.float32
BF16 = jnp.bfloat16
MESH = pl.DeviceIdType.MESH

D_MODEL = 1024
POOL_WIDTH = 512
SGU_WIDTH = 512
POOL_WINDOWS = (2, 4, 8, 16)
GROUP = 128
N_HEADS = 4
IN_COLS = POOL_WIDTH + 2 * SGU_WIDTH
D_FF = 2816
LN_EPS = 1e-5
ALPHA = float(2.0 ** 0.25)
HALO = 16
N_DEV = 8
N_CHIPS = 4

ADAM_LR = 0.001
ADAM_B1 = 0.9
ADAM_B2 = 0.999
ADAM_EPS = 1e-08
ADAM_WD = 0.01
ADAM_STEP = 10

VMEM_LIMIT_BYTES = 56 * 1024 * 1024

_SQRT_HALF = 0.7071067811865476
_INV_SQRT_2PI = 0.3989422804014327


def _dot_nn(a, b):
    return lax.dot_general(a, b, (((1,), (0,)), ((), ())), preferred_element_type=F32)


def _dot_nt(a, b):
    return lax.dot_general(a, b, (((1,), (1,)), ((), ())), preferred_element_type=F32)


def _dot_tn(a, b):
    return lax.dot_general(a, b, (((0,), (0,)), ((), ())), preferred_element_type=F32)


def _gelu(x):
    return 0.5 * x * (1.0 + lax.erf(x * _SQRT_HALF))


def _gelu_grad(x):
    return 0.5 * (1.0 + lax.erf(x * _SQRT_HALF)) + x * jnp.exp(-0.5 * x * x) * _INV_SQRT_2PI


def _ln_stats(r):
    mu = jnp.mean(r, axis=-1, keepdims=True)
    d = r - mu
    var = jnp.mean(d * d, axis=-1, keepdims=True)
    rstd = lax.rsqrt(var + LN_EPS)
    return d * rstd, rstd


def _ln_bwd(dout, xhat, rstd, g):
    dxh = dout * g
    m1 = jnp.mean(dxh, axis=-1, keepdims=True)
    m2 = jnp.mean(dxh * xhat, axis=-1, keepdims=True)
    return rstd * (dxh - m1 - xhat * m2)


def _rowsum(a):
    return jnp.sum(a, axis=0, keepdims=True)


def _pool_fwd(xp, xp_prev, inv_cnt, w):
    s = jnp.concatenate([xp_prev, xp], axis=0)
    k = 1
    while k < w:
        s = s + pltpu.roll(s, k, 0)
        k *= 2
    return s[HALO:, :] * inv_cnt - xp


def _pool_bwd(dpooled, dpooled_next, inv_cnt, inv_cnt_next, w):
    n = dpooled.shape[0] + HALO
    s = jnp.concatenate([dpooled * inv_cnt, dpooled_next * inv_cnt_next], axis=0)
    k = 1
    while k < w:
        s = s + pltpu.roll(s, n - k, 0)
        k *= 2
    return s[: dpooled.shape[0], :] - dpooled


def _inv_count(pos, w):
    return 1.0 / jnp.minimum(pos + 1, w).astype(F32)


def _to_head_major(a, h, nc):
    return jnp.concatenate(
        [a[c * GROUP:(c + 1) * GROUP, h * GROUP:(h + 1) * GROUP] for c in range(nc)], axis=1)


def _masked_sgu_w(sw_ref, h):
    row = lax.broadcasted_iota(jnp.int32, (GROUP, GROUP), 0)
    col = lax.broadcasted_iota(jnp.int32, (GROUP, GROUP), 1)
    return jnp.where(row >= col, sw_ref[h], 0.0)


def _row_block(rows, limit):
    return max(b for b in range(16, min(rows, limit) + 1, 16) if rows % b == 0)


def _mesh_position():
    return lax.axis_index("x"), lax.axis_index("y"), lax.axis_index("c")


def _other_chips(x, y):
    return [(1 - x, y), (x, 1 - y), (1 - x, 1 - y)]


class _TwoLevelGather:
    def __init__(self, ins, outs, send_sems, recv_sems, local_sems):
        self.ins, self.outs = ins, outs
        self.send_sems, self.recv_sems, self.local_sems = send_sems, recv_sems, local_sems
        self.na = len(ins)
        x, y, c = _mesh_position()
        self.c = c
        self.me, self.sibling = (x, y, c), (x, y, 1 - c)
        self.chips = _other_chips(x, y)
        self.relay_from = (x + (1 - c) * (1 - 2 * x), y + c * (1 - 2 * y))
        self.relay_to = (x + c * (1 - 2 * x), y + (1 - c) * (1 - 2 * y))

    def handshake(self):
        peers = [self.sibling] + [(*chip, self.c) for chip in self.chips[:2]]
        barrier = pltpu.get_barrier_semaphore()
        for peer in peers:
            pl.semaphore_signal(barrier, inc=1, device_id=peer, device_id_type=MESH)
        pl.semaphore_wait(barrier, len(peers))

    def _rows(self, a, px, py, pc):
        n = self.ins[a].shape[0]
        return self.outs[a].at[pl.ds((4 * px + 2 * py + pc) * n, n), :]

    def _copy(self, a, k, block, to, src=None):
        return pltpu.make_async_remote_copy(
            src_ref=self._rows(a, *block) if src is None else src, dst_ref=self._rows(a, *block),
            send_sem=self.send_sems.at[a * 7 + k], recv_sem=self.recv_sems.at[a * 7 + k],
            device_id=to, device_id_type=MESH)

    def _mine(self, a):
        return pltpu.make_async_copy(self.ins[a], self._rows(a, *self.me), self.local_sems.at[a])

    def start(self):
        for a in range(self.na):
            self._mine(a).start()
        for a in range(self.na):
            self._copy(a, 0, self.me, self.sibling, src=self.ins[a]).start()
            for j, chip in enumerate(self.chips[:2]):
                self._copy(a, 1 + j, self.me, (*chip, self.c), src=self.ins[a]).start()

    def relay(self, a):
        c, block = self.c, (*self.relay_from, self.c)
        self._copy(a, 1 + c, block, self.me).wait_recv()
        self._copy(a, 3, block, (*self.relay_to, c)).start()
        self._copy(a, 4 + c, block, self.sibling).start()

    def pass_on(self, a):
        c = self.c
        self._copy(a, 2 - c, (*self.relay_to, c), self.me).wait_recv()
        self._copy(a, 5 - c, (*self.relay_to, c), self.sibling).start()
        self._copy(a, 3, (*self.chips[2], c), self.me).wait_recv()
        self._copy(a, 6, (*self.chips[2], c), self.sibling).start()

    def finish(self):
        for a in range(self.na):
            self._copy(a, 0, self.sibling, self.me).wait_recv()
            for j, chip in enumerate(self.chips):
                self._copy(a, 4 + j, (*chip, 1 - self.c), self.me).wait_recv()
        for a in range(self.na):
            for k in range(7):
                self._copy(a, k, self.me, self.sibling, src=self.ins[a]).wait_send()
            self._mine(a).wait()

    @staticmethod
    def scratch(na):
        return [pltpu.SemaphoreType.DMA((7 * na,)), pltpu.SemaphoreType.DMA((7 * na,)), pltpu.SemaphoreType.DMA((na,))]


def _gathered_shape(s):
    return jax.ShapeDtypeStruct((N_DEV * s.shape[0], s.shape[1]), s.dtype)


def _gathered_bf16(s):
    return jax.ShapeDtypeStruct((N_DEV * s.shape[0], s.shape[1]), BF16)


def _all_gather_rows(shards, name):
    na = len(shards)

    def body(*refs):
        bf_refs = refs[2 * na:3 * na]
        for a in range(na):
            bf_refs[a][...] = refs[a][...].astype(BF16)
        gather = _TwoLevelGather(bf_refs, refs[na:2 * na], *refs[3 * na:])
        gather.handshake()
        gather.start()
        for a in range(na):
            gather.relay(a)
        for a in range(na):
            gather.pass_on(a)
        gather.finish()

    return pl.pallas_call(
        body, name=name, out_shape=[_gathered_bf16(s) for s in shards],
        in_specs=[pl.BlockSpec(memory_space=pltpu.VMEM)] * na, out_specs=[pl.BlockSpec(memory_space=pl.ANY)] * na,
        scratch_shapes=[pltpu.VMEM(s.shape, BF16) for s in shards] + _TwoLevelGather.scratch(na),
        compiler_params=pltpu.CompilerParams(collective_id=8),
    )(*shards)


def _mixer_fwd(x2d, win_t, wout, pool_w, pool_scale, sln_g, sln_b, sgu_w, sgu_b_t, ln1_g, ln1_b, later_shards, seq):
    tokens = x2d.shape[0]
    tt = min(512, seq)
    tiles_per_seq = seq // tt
    nc = tt // GROUP
    n_tiles = tokens // tt
    n_later = len(later_shards)

    def body(x_ref, xh_ref, win_ref, wout_ref, pw_ref, ps_ref, lg_ref, lb_ref, sw_ref, sb_ref, g1_ref, b1_ref, *rest):
        shard_refs, rest = rest[:n_later], rest[n_later:]
        proj_ref, xhat_ref, rstd_ref, mix_ref, xbf_ref = rest[:5]
        gathered_refs, rest = rest[5:5 + n_later], rest[5 + n_later:]
        mix_scr, shard_bf_refs, (send_sems, recv_sems, local_sems) = rest[0], rest[1:1 + n_later], rest[1 + n_later:]
        i = pl.program_id(0)
        gather = _TwoLevelGather(shard_bf_refs, gathered_refs, send_sems, recv_sems, local_sems)

        @pl.when(i == 0)
        def _():
            gather.handshake()
            for a in range(n_later):
                shard_bf_refs[a][...] = shard_refs[a][...].astype(BF16)
            gather.start()

        tile_in_seq = i % tiles_per_seq
        x = x_ref[...]
        xb = x.astype(BF16)
        xbf_ref[...] = xb
        proj = _dot_nt(xb, win_ref[...])
        proj_ref[...] = proj
        xp_prev = _dot_nt(xh_ref[...].astype(BF16), win_ref[0:POOL_WIDTH, :])
        xp_prev = jnp.where(tile_in_seq == 0, 0.0, xp_prev)
        pos = tile_in_seq * tt + lax.broadcasted_iota(jnp.int32, (tt, 1), 0)
        for g, w in enumerate(POOL_WINDOWS):
            sl = slice(g * GROUP, (g + 1) * GROUP)
            pooled = _pool_fwd(proj[:, sl], xp_prev[:, sl], _inv_count(pos, w), w)
            pre = _dot_nn(pooled.astype(BF16), pw_ref[g].astype(BF16))
            mix_scr[:, sl] = pre * ps_ref[:, sl]
        u = _gelu(proj[:, POOL_WIDTH:POOL_WIDTH + SGU_WIDTH])
        v = _gelu(proj[:, POOL_WIDTH + SGU_WIDTH:])
        vhat, _ = _ln_stats(v)
        v_ln = vhat * lg_ref[...] + lb_ref[...]
        for h in range(N_HEADS):
            ws = _masked_sgu_w(sw_ref, h).astype(BF16)
            mixed = _dot_nn(ws, _to_head_major(v_ln, h, nc).astype(BF16)) + sb_ref[:, h:h + 1]
            for c in range(nc):
                rs = slice(c * GROUP, (c + 1) * GROUP)
                mix_scr[rs, POOL_WIDTH + h * GROUP:POOL_WIDTH + (h + 1) * GROUP] = (
                    u[rs, h * GROUP:(h + 1) * GROUP] * mixed[:, c * GROUP:(c + 1) * GROUP])
        mixb = mix_scr[...].astype(BF16)
        mix_ref[...] = mixb
        r1 = ALPHA * x + _dot_nn(mixb, wout_ref[...])
        xhat, rstd = _ln_stats(r1)
        xhat_ref[...] = xhat
        rstd_ref[...] = rstd

        for a in range(n_later):
            relay_tile = min(n_tiles // 2 + a, n_tiles - 1)

            @pl.when(i == relay_tile)
            def _(a=a):
                gather.relay(a)

            @pl.when(i == max(n_tiles - n_later + a, relay_tile))
            def _(a=a):
                gather.pass_on(a)

        @pl.when(i == n_tiles - 1)
        def _():
            gather.finish()

    def tile(cols):
        return pl.BlockSpec((tt, cols), lambda i: (i, 0))

    def whole(a):
        nd = a.ndim
        return pl.BlockSpec(a.shape, lambda i: (0,) * nd)

    any_spec = pl.BlockSpec(memory_space=pl.ANY)
    halo = pl.BlockSpec((HALO, D_MODEL), lambda i: (jnp.maximum(i * (tt // HALO) - 1, 0), 0))
    consts = [win_t, wout, pool_w, pool_scale, sln_g, sln_b, sgu_w, sgu_b_t, ln1_g, ln1_b]
    return pl.pallas_call(
        body, name="mixer_fwd", grid=(n_tiles,),
        in_specs=[tile(D_MODEL), halo] + [whole(a) for a in consts] + [whole(s) for s in later_shards],
        out_specs=[tile(IN_COLS), tile(D_MODEL), tile(1), tile(D_MODEL), tile(D_MODEL)] + [any_spec] * n_later,
        out_shape=[jax.ShapeDtypeStruct((tokens, IN_COLS), F32), jax.ShapeDtypeStruct((tokens, D_MODEL), F32),
                   jax.ShapeDtypeStruct((tokens, 1), F32), jax.ShapeDtypeStruct((tokens, D_MODEL), BF16),
                   jax.ShapeDtypeStruct((tokens, D_MODEL), BF16)] + [_gathered_bf16(s) for s in later_shards],
        scratch_shapes=[pltpu.VMEM((tt, D_MODEL), F32)] + [pltpu.VMEM(s.shape, BF16) for s in later_shards]
                       + _TwoLevelGather.scratch(n_later),
        compiler_params=pltpu.CompilerParams(dimension_semantics=("arbitrary",), vmem_limit_bytes=VMEM_LIMIT_BYTES,
                                             collective_id=9),
    )(x2d, x2d, *consts, *later_shards)


def _ffn_fwd_bwd(xhat1, rstd1, target, wgu_t, wdown, ln1_g, ln1_b, ln2_g, ln2_b):
    tokens = xhat1.shape[0]
    tt = min(256, tokens)

    def body(xhat_ref, rstd_ref, tgt_ref, wgu_hbm, wd_hbm, g1_ref, b1_ref, g2_ref, b2_ref,
             dr1_ref, acts_ref, grads_ref, stats_ref, wgu_ref, wd_ref, gu_scr, sems):
        i = pl.program_id(0)

        @pl.when(i == 0)
        def _():
            loads = [pltpu.make_async_copy(wgu_hbm, wgu_ref, sems.at[0]),
                     pltpu.make_async_copy(wd_hbm, wd_ref, sems.at[1])]
            for cp in loads:
                cp.start()
            stats_ref[...] = jnp.zeros_like(stats_ref)
            for cp in loads:
                cp.wait()

        xhat1_t = xhat_ref[...]
        h = xhat1_t * g1_ref[...] + b1_ref[...]
        hb = h.astype(BF16)
        acts_ref[:, 0:D_MODEL] = hb
        gate = _dot_nt(hb, wgu_ref[0:D_FF, :])
        up = _dot_nt(hb, wgu_ref[D_FF:, :])
        gu_scr[:, 0:D_FF] = gate
        gu_scr[:, D_FF:] = up
        ab = (gate * jax.nn.sigmoid(gate) * up).astype(BF16)
        grads_ref[:, 2 * D_FF:] = ab
        xhat2, rstd2 = _ln_stats(ALPHA * h + _dot_nn(ab, wd_ref[...]))
        err = xhat2 * g2_ref[...] + b2_ref[...] - tgt_ref[...]
        dy = err * (1.0 / D_MODEL)
        stats_ref[0:1, :] += _rowsum(dy * xhat2)
        stats_ref[1:2, :] += _rowsum(dy)
        stats_ref[4:5, :] += _rowsum(err * err)
        dr2 = _ln_bwd(dy, xhat2, rstd2, g2_ref[...])
        dr2b = dr2.astype(BF16)
        acts_ref[:, D_MODEL:2 * D_MODEL] = dr2b
        da = _dot_nt(dr2b, wd_ref[...])
        gate = gu_scr[:, 0:D_FF]
        up = gu_scr[:, D_FF:]
        sg = jax.nn.sigmoid(gate)
        dgate = (da * up * (sg * (1.0 + gate * (1.0 - sg)))).astype(BF16)
        dup = (da * (gate * sg)).astype(BF16)
        grads_ref[:, 0:D_FF] = dgate
        grads_ref[:, D_FF:2 * D_FF] = dup
        dh = ALPHA * dr2 + _dot_nn(dgate, wgu_ref[0:D_FF, :]) + _dot_nn(dup, wgu_ref[D_FF:, :])
        stats_ref[2:3, :] += _rowsum(dh * xhat1_t)
        stats_ref[3:4, :] += _rowsum(dh)
        dr1 = _ln_bwd(dh, xhat1_t, rstd_ref[...], g1_ref[...])
        dr1_ref[...] = dr1
        acts_ref[:, 2 * D_MODEL:] = dr1.astype(BF16)

    def tile(cols):
        return pl.BlockSpec((tt, cols), lambda i: (i, 0))

    def whole(a):
        nd = a.ndim
        return pl.BlockSpec(a.shape, lambda i: (0,) * nd)

    any_spec = pl.BlockSpec(memory_space=pl.ANY)
    vecs = [ln1_g, ln1_b, ln2_g, ln2_b]
    return pl.pallas_call(
        body, name="ffn_fwd_bwd", grid=(tokens // tt,),
        in_specs=[tile(D_MODEL), tile(1), tile(D_MODEL), any_spec, any_spec] + [whole(a) for a in vecs],
        out_specs=[tile(D_MODEL), tile(3 * D_MODEL), tile(3 * D_FF), pl.BlockSpec((8, D_MODEL), lambda i: (0, 0))],
        out_shape=[jax.ShapeDtypeStruct((tokens, D_MODEL), F32), jax.ShapeDtypeStruct((tokens, 3 * D_MODEL), BF16),
                   jax.ShapeDtypeStruct((tokens, 3 * D_FF), BF16), jax.ShapeDtypeStruct((8, D_MODEL), F32)],
        scratch_shapes=[pltpu.VMEM(wgu_t.shape, BF16), pltpu.VMEM(wdown.shape, BF16),
                        pltpu.VMEM((tt, 2 * D_FF), F32), pltpu.SemaphoreType.DMA((2,))],
        compiler_params=pltpu.CompilerParams(dimension_semantics=("arbitrary",), vmem_limit_bytes=VMEM_LIMIT_BYTES),
    )(xhat1, rstd1, target, wgu_t, wdown, *vecs)


def _wgrad_exchange(lhs, rhs, chips_per_block, name, gather_rows=(), after=None, collective_id=None,
                    lhs_cols=None, rhs_block=None, token_tile=2048):
    tokens = lhs.shape[0]
    first_col, n_all = (0, lhs.shape[1]) if lhs_cols is None else lhs_cols
    m = rhs.shape[1] if rhs_block is None else D_MODEL
    n = n_all // N_DEV
    tw = min(token_tile, tokens)
    nt = tokens // tw
    cpb = chips_per_block
    nj = N_CHIPS // cpb
    lhs_block0 = first_col // (2 * n * cpb)
    assert lhs_block0 * 2 * n * cpb == first_col
    rhs_col = 0 if rhs_block is None else rhs_block
    ng = len(gather_rows)
    anchors = [] if after is None else [after]

    def body(l_ref, r_ref, *rest):
        small_refs, rest = rest[:ng], rest[ng + len(anchors):]
        bf_ref, own_ref = rest[:2]
        gathered_refs, rest = rest[2:2 + ng], rest[2 + ng:]
        acc, kept, landed, send_sems, recv_sems = rest[:5]
        sendbuf = bf_ref
        j, t = pl.program_id(0), pl.program_id(1)
        first, last = (j == 0) & (t == 0), (j == nj - 1) & (t == nt - 1)
        x, y, c = _mesh_position()
        if ng:
            gather = _TwoLevelGather(small_refs, gathered_refs, *rest[5:8])

            @pl.when(first)
            def _():
                gather.handshake()
                gather.start()
        else:
            @pl.when(first)
            def _():
                barrier = pltpu.get_barrier_semaphore()
                pl.semaphore_signal(barrier, inc=1, device_id=(x, y, 1 - c), device_id_type=MESH)
                pl.semaphore_wait(barrier, 1)

        def copy(q):
            return pltpu.make_async_remote_copy(
                src_ref=sendbuf.at[q], dst_ref=landed.at[q], send_sem=send_sems.at[q], recv_sem=recv_sems.at[q],
                device_id=(x, y, 1 - c), device_id_type=MESH)

        @pl.when(t == 0)
        def _():
            acc[...] = _dot_tn(l_ref[...], r_ref[...])

        @pl.when(t > 0)
        def _():
            acc[...] += _dot_tn(l_ref[...], r_ref[...])

        @pl.when(t == nt - 1)
        def _():
            for qq in range(cpb):
                q = j * cpb + qq
                kept[q] = acc[pl.ds(pl.multiple_of(qq * 2 * n + c * n, 8), n), :]
                sendbuf[q] = acc[pl.ds(pl.multiple_of(qq * 2 * n + (1 - c) * n, 8), n), :].astype(BF16)
                copy(q).start()

        if ng:
            @pl.when((j == nj - 1) & (t == nt // 2))
            def _():
                for a in range(ng):
                    gather.relay(a)

            @pl.when(last)
            def _():
                for a in range(ng):
                    gather.pass_on(a)
                gather.finish()

        @pl.when(last)
        def _():
            for q in range(N_CHIPS):
                copy(q).wait_send()
                copy(q).wait_recv()
            for q in range(N_CHIPS):
                s = kept[q] + landed[q].astype(F32)
                bf_ref[q] = s.astype(BF16)

                @pl.when(q == 2 * x + y)
                def _(s=s):
                    own_ref[...] = s

    any_spec = pl.BlockSpec(memory_space=pl.ANY)
    by_chip = (N_CHIPS, n, m)
    return pl.pallas_call(
        body, name=name, grid=(nj, nt),
        in_specs=[pl.BlockSpec((tw, 2 * n * cpb), lambda j, t: (t, lhs_block0 + j)),
                  pl.BlockSpec((tw, m), lambda j, t: (t, rhs_col))] + [any_spec] * (ng + len(anchors)),
        out_specs=[pl.BlockSpec(by_chip, lambda j, t: (0, 0, 0)), pl.BlockSpec((n, m), lambda j, t: (0, 0))]
                  + [any_spec] * ng,
        out_shape=[jax.ShapeDtypeStruct(by_chip, BF16), jax.ShapeDtypeStruct((n, m), F32)]
                  + [_gathered_shape(s) for s in gather_rows],
        scratch_shapes=[pltpu.VMEM((2 * n * cpb, m), F32), pltpu.VMEM(by_chip, F32), pltpu.VMEM(by_chip, BF16),
                        pltpu.SemaphoreType.DMA((N_CHIPS,)), pltpu.SemaphoreType.DMA((N_CHIPS,))]
                       + (_TwoLevelGather.scratch(ng) if ng else []),
        compiler_params=pltpu.CompilerParams(dimension_semantics=("arbitrary", "arbitrary"),
                                             vmem_limit_bytes=VMEM_LIMIT_BYTES + 4 * 1024 * 1024,
                                             collective_id=collective_id),
    )(lhs, rhs, *gather_rows, *anchors)


def _received_shape(p):
    return jax.ShapeDtypeStruct((3,) + p.shape[1:], p.dtype)


ROW_POOL_SCALE, ROW_SLN_G, ROW_SLN_B, ROW_SGU_B = 0, 1, 2, 3
ROW_LN2_G, ROW_LN2_B, ROW_LN1_G, ROW_LN1_B, ROW_LOSS = 8, 9, 10, 11, 12
VEC_ROWS = 16


def _mixer_bwd(dr1, proj, win_t, wout, pool_w, pool_scale, sln_g, sln_b, sgu_w, sgu_b_t, stats, seq):
    tokens = dr1.shape[0]
    tt = min(1024, seq)
    tiles_per_seq = seq // tt
    nc = tt // GROUP
    n_halo_blocks = tokens // HALO
    n_tiles = tokens // tt

    def body(dr1_ref, dr1n_ref, proj_ref, projh_ref, win_ref, wout_ref, pw_ref, ps_ref, lg_ref, lb_ref, sw_ref, sb_ref,
             stats_ref, gx_ref, dproj_ref, dmat_ref, dvec_ref, du_scr, dv_scr):
        i = pl.program_id(0)
        tile_in_seq = i % tiles_per_seq

        @pl.when(i == 0)
        def _():
            dmat_ref[...] = jnp.zeros_like(dmat_ref)
            dvec_ref[0:8, :] = jnp.zeros((8, D_MODEL), F32)
            dvec_ref[8:16, :] = stats_ref[...]

        dr1_t = dr1_ref[...]
        dr1b = dr1_t.astype(BF16)
        dmix = _dot_nt(dr1b, wout_ref[...])
        dpo_next = _dot_nt(dr1n_ref[...].astype(BF16), wout_ref[0:POOL_WIDTH, :])
        dpo_next = jnp.where(tile_in_seq == tiles_per_seq - 1, 0.0, dpo_next)
        proj = proj_ref[...]
        xp_prev = jnp.where(tile_in_seq == 0, 0.0, projh_ref[...])
        pos = tile_in_seq * tt + lax.broadcasted_iota(jnp.int32, (tt, 1), 0)
        pos_next = (tile_in_seq + 1) * tt + lax.broadcasted_iota(jnp.int32, (HALO, 1), 0)

        for g, w in enumerate(POOL_WINDOWS):
            sl = slice(g * GROUP, (g + 1) * GROUP)
            inv_cnt = _inv_count(pos, w)
            pwb = pw_ref[g].astype(BF16)
            pooledb = _pool_fwd(proj[:, sl], xp_prev[:, sl], inv_cnt, w).astype(BF16)
            pre = _dot_nn(pooledb, pwb)
            dpo = dmix[:, sl]
            dvec_ref[ROW_POOL_SCALE:ROW_POOL_SCALE + 1, sl] += _rowsum(dpo * pre)
            dsb = (dpo * ps_ref[:, sl]).astype(BF16)
            dmat_ref[g] += _dot_tn(pooledb, dsb)
            dpooled = _dot_nt(dsb, pwb)
            dpooled_next = _dot_nt((dpo_next[:, sl] * ps_ref[:, sl]).astype(BF16), pwb)
            dxp = _pool_bwd(dpooled, dpooled_next, inv_cnt, _inv_count(pos_next, w), w)
            dproj_ref[:, sl] = dxp.astype(BF16)

        zu = proj[:, POOL_WIDTH:POOL_WIDTH + SGU_WIDTH]
        zv = proj[:, POOL_WIDTH + SGU_WIDTH:]
        u = _gelu(zu)
        vhat, rstd_v = _ln_stats(_gelu(zv))
        v_ln = vhat * lg_ref[...] + lb_ref[...]
        dsg = dmix[:, POOL_WIDTH:]
        row = lax.broadcasted_iota(jnp.int32, (GROUP, GROUP), 0)
        col = lax.broadcasted_iota(jnp.int32, (GROUP, GROUP), 1)
        for h in range(N_HEADS):
            ws = _masked_sgu_w(sw_ref, h).astype(BF16)
            vh = _to_head_major(v_ln, h, nc).astype(BF16)
            mixed = _dot_nn(ws, vh) + sb_ref[:, h:h + 1]
            dsg_h = _to_head_major(dsg, h, nc)
            du_h = dsg_h * mixed
            dm_h = dsg_h * _to_head_major(u, h, nc)
            pos_sums = lax.dot_general(jnp.ones((8, nc * GROUP), F32), dm_h, (((1,), (1,)), ((), ())),
                                       precision=lax.Precision.HIGH, preferred_element_type=F32)
            dvec_ref[ROW_SGU_B + h:ROW_SGU_B + h + 1, 0:GROUP] += pos_sums[0:1, :]
            dmb = dm_h.astype(BF16)
            dmat_ref[len(POOL_WINDOWS) + h] += jnp.where(row >= col, _dot_nt(dmb, vh), 0.0)
            dv_h = _dot_tn(ws, dmb)
            for c in range(nc):
                rs = slice(c * GROUP, (c + 1) * GROUP)
                cs = slice(h * GROUP, (h + 1) * GROUP)
                du_scr[rs, cs] = du_h[:, c * GROUP:(c + 1) * GROUP]
                dv_scr[rs, cs] = dv_h[:, c * GROUP:(c + 1) * GROUP]
        dv_ln = dv_scr[...]
        dvec_ref[ROW_SLN_B:ROW_SLN_B + 1, 0:SGU_WIDTH] += _rowsum(dv_ln)
        dvec_ref[ROW_SLN_G:ROW_SLN_G + 1, 0:SGU_WIDTH] += _rowsum(dv_ln * vhat)
        dv = _ln_bwd(dv_ln, vhat, rstd_v, lg_ref[...])
        dproj_ref[:, POOL_WIDTH:POOL_WIDTH + SGU_WIDTH] = (du_scr[...] * _gelu_grad(zu)).astype(BF16)
        dproj_ref[:, POOL_WIDTH + SGU_WIDTH:] = (dv * _gelu_grad(zv)).astype(BF16)
        gx_ref[...] = ALPHA * dr1_t + _dot_nn(dproj_ref[...], win_ref[...])

    def tile(cols):
        return pl.BlockSpec((tt, cols), lambda i: (i, 0))

    def whole(a):
        nd = a.ndim
        return pl.BlockSpec(a.shape, lambda i: (0,) * nd)

    def resident(shape):
        nd = len(shape)
        return pl.BlockSpec(shape, lambda i: (0,) * nd)

    next_halo = pl.BlockSpec((HALO, D_MODEL), lambda i: (jnp.minimum((i + 1) * (tt // HALO), n_halo_blocks - 1), 0))
    prev_halo = pl.BlockSpec((HALO, POOL_WIDTH), lambda i: (jnp.maximum(i * (tt // HALO) - 1, 0), 0))
    consts = [win_t, wout, pool_w, pool_scale, sln_g, sln_b, sgu_w, sgu_b_t, stats]
    small_shapes = [(len(POOL_WINDOWS) + N_HEADS, GROUP, GROUP), (VEC_ROWS, D_MODEL)]
    return pl.pallas_call(
        body, name="mixer_bwd", grid=(n_tiles,),
        in_specs=[tile(D_MODEL), next_halo, tile(IN_COLS), prev_halo] + [whole(a) for a in consts],
        out_specs=[tile(D_MODEL), tile(IN_COLS)] + [resident(s) for s in small_shapes],
        out_shape=[jax.ShapeDtypeStruct((tokens, D_MODEL), F32), jax.ShapeDtypeStruct((tokens, IN_COLS), BF16)]
                  + [jax.ShapeDtypeStruct(s, F32) for s in small_shapes],
        scratch_shapes=[pltpu.VMEM((tt, SGU_WIDTH), F32), pltpu.VMEM((tt, SGU_WIDTH), F32)],
        compiler_params=pltpu.CompilerParams(dimension_semantics=("arbitrary",), vmem_limit_bytes=VMEM_LIMIT_BYTES),
    )(dr1, dr1, proj, proj, *consts)


def _owner_copies(src_ref, land_ref, sems):
    x, y, c = _mesh_position()
    return [pltpu.make_async_remote_copy(
        src_ref=src_ref.at[2 * cx + cy], dst_ref=land_ref.at[j], send_sem=sems[j], recv_sem=sems[3 + j],
        device_id=(cx, cy, c), device_id_type=MESH) for j, (cx, cy) in enumerate(_other_chips(x, y))]


def _send_to_owners_start(chip_partial, name, collective_id):
    land = _received_shape(chip_partial)

    def body(src_ref, land_ref, *rest):
        x, y, c = _mesh_position()
        barrier = pltpu.get_barrier_semaphore()
        for cx, cy in _other_chips(x, y):
            pl.semaphore_signal(barrier, inc=1, device_id=(cx, cy, c), device_id_type=MESH)
        pl.semaphore_wait(barrier, 3)
        for cp in _owner_copies(src_ref, land_ref, rest[:6]):
            cp.start()
        rest[8][...] = jnp.zeros_like(rest[8])

    hbm = pl.BlockSpec(memory_space=pltpu.HBM)
    sem = pl.BlockSpec(memory_space=pltpu.SEMAPHORE)
    outs = pl.pallas_call(
        body, name=name,
        out_shape=[pltpu.SemaphoreType.DMA(())] * 6 + [pltpu.HBM(chip_partial.shape, chip_partial.dtype),
                                                       pltpu.HBM(land.shape, land.dtype),
                                                       jax.ShapeDtypeStruct((8, GROUP), F32)],
        in_specs=[hbm, hbm], out_specs=[sem] * 6 + [hbm, hbm, pl.BlockSpec(memory_space=pltpu.VMEM)],
        input_output_aliases={0: 6, 1: 7},
        compiler_params=pltpu.CompilerParams(has_side_effects=pltpu.SideEffectType.DATAFLOW_SIDE_EFFECTING,
                                             collective_id=collective_id),
    )(pltpu.with_memory_space_constraint(chip_partial, pltpu.HBM),
      pltpu.with_memory_space_constraint(lax.empty(land.shape, land.dtype), pltpu.HBM))
    return outs[:6], outs[6], outs[7], outs[8]


def _send_to_owners_wait(sems, src_thru, land_thru, after, name):
    def body(src_ref, land_ref, *rest):
        for cp in _owner_copies(src_ref, land_ref, rest[:6]):
            cp.wait_send()
            cp.wait_recv()

    hbm = pl.BlockSpec(memory_space=pltpu.HBM)
    sem = pl.BlockSpec(memory_space=pltpu.SEMAPHORE)
    return pl.pallas_call(
        body, name=name,
        out_shape=[pltpu.HBM(src_thru.shape, src_thru.dtype), pltpu.HBM(land_thru.shape, land_thru.dtype)],
        in_specs=[hbm, hbm] + [sem] * 6 + [pl.BlockSpec(memory_space=pl.ANY)], out_specs=[hbm, hbm],
        input_output_aliases={0: 0, 1: 1},
        compiler_params=pltpu.CompilerParams(has_side_effects=pltpu.SideEffectType.DATAFLOW_SIDE_EFFECTING),
    )(src_thru, land_thru, *sems, after)[1]


def _adamw_math(w, g, m, v):
    m = ADAM_B1 * m + (1.0 - ADAM_B1) * g
    v = ADAM_B2 * v + (1.0 - ADAM_B2) * (g * g)
    m_hat = m / (1.0 - ADAM_B1 ** ADAM_STEP)
    v_hat = v / (1.0 - ADAM_B2 ** ADAM_STEP)
    delta = -ADAM_LR * (m_hat / (jnp.sqrt(v_hat) + ADAM_EPS) + ADAM_WD * w)
    return delta, m, v


def _sum_adamw(w, m, v, own, received, name):
    rows, cols = w.shape
    rb = _row_block(rows, 256)

    def body(w_ref, m_ref, v_ref, own_ref, rec_ref, g_ref, d_ref, mo_ref, vo_ref):
        g = own_ref[...]
        for j in range(3):
            g = g + rec_ref[j].astype(F32)
        g_ref[...] = g
        d_ref[...], mo_ref[...], vo_ref[...] = _adamw_math(w_ref[...], g, m_ref[...], v_ref[...])

    spec = pl.BlockSpec((rb, cols), lambda r: (r, 0))
    return pl.pallas_call(
        body, name=name, grid=(rows // rb,),
        in_specs=[spec] * 4 + [pl.BlockSpec((3, rb, cols), lambda r: (0, r, 0))],
        out_specs=[spec] * 4, out_shape=[jax.ShapeDtypeStruct((rows, cols), F32)] * 4,
        compiler_params=pltpu.CompilerParams(dimension_semantics=("arbitrary",)),
    )(w, m, v, own, received)


SMALL_NAMES = ("pool_w", "sgu_w", "pool_scale", "sgu_ln_g", "sgu_ln_b", "sgu_b", "ln1_g", "ln1_b", "ln2_g", "ln2_b")
_SMALL_VEC_ROWS = {"pool_scale": (ROW_POOL_SCALE, POOL_WIDTH), "sgu_ln_g": (ROW_SLN_G, SGU_WIDTH),
                   "sgu_ln_b": (ROW_SLN_B, SGU_WIDTH), "ln1_g": (ROW_LN1_G, D_MODEL), "ln1_b": (ROW_LN1_B, D_MODEL),
                   "ln2_g": (ROW_LN2_G, D_MODEL), "ln2_b": (ROW_LN2_B, D_MODEL)}
_SMALL_MAT_FIRST = {"pool_w": 0, "sgu_w": len(POOL_WINDOWS)}


def _small_sum_adamw(mats_all, vecs_all, w, m, v):
    n = len(SMALL_NAMES)

    def body(mats_ref, vecs_ref, *refs):
        w_refs, m_refs, v_refs = refs[:n], refs[n:2 * n], refs[2 * n:3 * n]
        loss_ref = refs[3 * n]
        g_refs, d_refs, mo_refs, vo_refs = (refs[3 * n + 1 + k * n:3 * n + 1 + (k + 1) * n] for k in range(4))
        vec_scr = refs[7 * n + 1]

        def update(k, idx, g):
            d, mo, vo = _adamw_math(w_refs[k][idx], g, m_refs[k][idx], v_refs[k][idx])
            g_refs[k][idx], d_refs[k][idx], mo_refs[k][idx], vo_refs[k][idx] = g, d, mo, vo

        total = vecs_ref[0]
        for dev in range(1, N_DEV):
            total = total + vecs_ref[dev]
        vec_scr[...] = total
        for k, name in enumerate(SMALL_NAMES):
            if name in _SMALL_MAT_FIRST:
                for b in range(4):
                    g = mats_ref[0, _SMALL_MAT_FIRST[name] + b]
                    for dev in range(1, N_DEV):
                        g = g + mats_ref[dev, _SMALL_MAT_FIRST[name] + b]
                    update(k, (0, b), g)
            elif name == "sgu_b":
                update(k, (0,), vec_scr[ROW_SGU_B:ROW_SGU_B + N_HEADS, 0:GROUP])
            else:
                row, width = _SMALL_VEC_ROWS[name]
                update(k, (slice(None), slice(None)), vec_scr[row:row + 1, 0:width])
        loss = jnp.sum(vec_scr[ROW_LOSS:ROW_LOSS + 1, :], axis=1, keepdims=True) * (0.5 / D_MODEL)
        loss_ref[...] = jnp.broadcast_to(loss, loss_ref.shape)

    vmem = pl.BlockSpec(memory_space=pltpu.VMEM)
    shapes = [jax.ShapeDtypeStruct(w[k].shape, F32) for k in SMALL_NAMES]
    outs = pl.pallas_call(
        body, name="small_sum_adamw",
        out_shape=[jax.ShapeDtypeStruct((8, GROUP), F32)] + shapes * 4,
        in_specs=[vmem] * (2 + 3 * n), out_specs=[vmem] * (1 + 4 * n),
        scratch_shapes=[pltpu.VMEM((VEC_ROWS, D_MODEL), F32)],
    )(mats_all, vecs_all, *[w[k] for k in SMALL_NAMES], *[m[k] for k in SMALL_NAMES], *[v[k] for k in SMALL_NAMES])
    return outs[0], outs[1:1 + n], outs[1 + n:1 + 2 * n], outs[1 + 2 * n:1 + 3 * n], outs[1 + 3 * n:]


def kernel(x, w_in, pool_w, pool_scale, sgu_ln_g, sgu_ln_b, sgu_w, sgu_b, w_out, ln1_g, ln1_b, w_gate_up, w_down, ln2_g, ln2_b, loss_target, m_w_in, m_pool_w, m_pool_scale, m_sgu_ln_g, m_sgu_ln_b, m_sgu_w, m_sgu_b, m_w_out, m_ln1_g, m_ln1_b, m_w_gate_up, m_w_down, m_ln2_g, m_ln2_b, v_w_in, v_pool_w, v_pool_scale, v_sgu_ln_g, v_sgu_ln_b, v_sgu_w, v_sgu_b, v_w_out, v_ln1_g, v_ln1_b, v_w_gate_up, v_w_down, v_ln2_g, v_ln2_b):
    bl, seq, _ = x.shape
    tokens = bl * seq
    x2d = x.reshape(tokens, D_MODEL)
    tgt2d = loss_target.reshape(tokens, D_MODEL)

    win_t, wout = _all_gather_rows([w_in[0].T, w_out[0]], "weight_all_gather")

    pool_w3, sgu_w3 = pool_w[0], sgu_w[0]
    sgu_b_t = sgu_b[0].T
    proj, xhat1, rstd1, mix_bf, x_bf, wgu_t, wdown = _mixer_fwd(
        x2d, win_t, wout, pool_w3, pool_scale, sgu_ln_g, sgu_ln_b, sgu_w3, sgu_b_t, ln1_g, ln1_b,
        [w_gate_up[0].T, w_down[0]], seq)
    dr1, acts_bf, grads_bf, stats = _ffn_fwd_bwd(xhat1, rstd1, tgt2d, wgu_t, wdown, ln1_g, ln1_b, ln2_g, ln2_b)

    bf_gu, own_gu = _wgrad_exchange(grads_bf, acts_bf, 1, "wgrad_gate_up", collective_id=1,
                                    lhs_cols=(0, 2 * D_FF), rhs_block=0)
    sent_gu = _send_to_owners_start(bf_gu, "grad_scatter_w_gate_up_start", collective_id=4)
    bf_dn, own_dn = _wgrad_exchange(grads_bf, acts_bf, N_CHIPS, "wgrad_down", after=sent_gu[3], collective_id=2,
                                    lhs_cols=(2 * D_FF, D_FF), rhs_block=1, token_tile=1024)
    sent_dn = _send_to_owners_start(bf_dn, "grad_scatter_w_down_start", collective_id=5)
    bf_out, own_out = _wgrad_exchange(mix_bf, acts_bf, N_CHIPS, "wgrad_out", after=sent_dn[3], collective_id=3,
                                      rhs_block=2)
    sent_out = _send_to_owners_start(bf_out, "grad_scatter_w_out_start", collective_id=6)
    gx, dproj_bf, d_mats, d_vecs = _mixer_bwd(
        dr1, proj, win_t, wout, pool_w3, pool_scale, sgu_ln_g, sgu_ln_b, sgu_w3, sgu_b_t, stats, seq)
    bf_in, own_in, mats_all, vecs_all = _wgrad_exchange(
        dproj_bf, x_bf, N_CHIPS, "wgrad_in", gather_rows=[d_mats.reshape(-1, GROUP), d_vecs], after=sent_out[3],
        collective_id=10)
    sent_in = _send_to_owners_start(bf_in, "grad_scatter_w_in_start", collective_id=7)

    grads, deltas, new_m, new_v = {}, {}, {}, {}
    after = sent_in[3]
    for nm, w, m, v, own, sent, transposed in (("w_gate_up", w_gate_up, m_w_gate_up, v_w_gate_up, own_gu, sent_gu, True),
                                               ("w_down", w_down, m_w_down, v_w_down, own_dn, sent_dn, False),
                                               ("w_out", w_out, m_w_out, v_w_out, own_out, sent_out, False),
                                               ("w_in", w_in, m_w_in, v_w_in, own_in, sent_in, True)):
        rows = (lambda a: a[0].T) if transposed else (lambda a: a[0])
        back = (lambda a: a.T[None]) if transposed else (lambda a: a[None])
        rec = _send_to_owners_wait(*sent[:3], after, "grad_scatter_" + nm + "_wait")
        g, d, mo, vo = _sum_adamw(rows(w), rows(m), rows(v), own, rec, "adamw_" + nm)
        after = vo
        grads[nm], deltas[nm], new_m[nm], new_v[nm] = back(g), back(d), back(mo), back(vo)

    small_w = {"pool_w": pool_w, "pool_scale": pool_scale, "sgu_ln_g": sgu_ln_g, "sgu_ln_b": sgu_ln_b, "sgu_w": sgu_w,
               "sgu_b": sgu_b, "ln1_g": ln1_g, "ln1_b": ln1_b, "ln2_g": ln2_g, "ln2_b": ln2_b}
    small_m = {"pool_w": m_pool_w, "pool_scale": m_pool_scale, "sgu_ln_g": m_sgu_ln_g, "sgu_ln_b": m_sgu_ln_b,
               "sgu_w": m_sgu_w, "sgu_b": m_sgu_b, "ln1_g": m_ln1_g, "ln1_b": m_ln1_b, "ln2_g": m_ln2_g, "ln2_b": m_ln2_b}
    small_v = {"pool_w": v_pool_w, "pool_scale": v_pool_scale, "sgu_ln_g": v_sgu_ln_g, "sgu_ln_b": v_sgu_ln_b,
               "sgu_w": v_sgu_w, "sgu_b": v_sgu_b, "ln1_g": v_ln1_g, "ln1_b": v_ln1_b, "ln2_g": v_ln2_g, "ln2_b": v_ln2_b}
    loss_blk, g_small, d_small, m_small, v_small = _small_sum_adamw(
        mats_all.reshape(N_DEV, 2 * N_HEADS, GROUP, GROUP), vecs_all.reshape(N_DEV, VEC_ROWS, D_MODEL),
        small_w, small_m, small_v)
    for vals, dst in ((g_small, grads), (d_small, deltas), (m_small, new_m), (v_small, new_v)):
        dst.update(zip(SMALL_NAMES, vals))

    order = ["w_in", "pool_w", "pool_scale", "sgu_ln_g", "sgu_ln_b", "sgu_w", "sgu_b", "w_out", "ln1_g", "ln1_b",
             "w_gate_up", "w_down", "ln2_g", "ln2_b"]
    return (loss_blk[0, 0], gx.reshape(bl, seq, D_MODEL), *[grads[k] for k in order], *[deltas[k] for k in order],
            *[new_m[k] for k in order], *[new_v[k] for k in order])
```
